```python
import jax, jax.numpy as jnp
from jax import lax
import numpy as np

D_MODEL = 1024
BATCH = 8
SEQ = 8192
DEPTH = 2

HEAD_DIM = 64
POOL_WINDOWS = (2, 4, 8, 16)
POOL_GROUPS = 4
POOL_GROUP_DIM = D_MODEL // 16
POOL_WIDTH = POOL_GROUPS * POOL_GROUP_DIM
N_Q_HEADS = D_MODEL // 128
N_KV_HEADS = 2
Q_PER_KV = N_Q_HEADS // N_KV_HEADS
WINDOW = 128
ATTN_BLOCK = 128
ATTN_WIDTH = N_Q_HEADS * HEAD_DIM
KV_WIDTH = N_KV_HEADS * HEAD_DIM
CHUNK = 128
SGU_GROUPS = 4
SGU_GROUP_DIM = D_MODEL // 16
SGU_WIDTH = SGU_GROUPS * SGU_GROUP_DIM
N_BRANCHES = 3
IN_COLS = POOL_WIDTH + ATTN_WIDTH + 2 * KV_WIDTH + 2 * SGU_WIDTH + N_BRANCHES * D_MODEL
D_FF = 2816
CONV_WIDTH = 3
ROPE_THETA = 10000.0
EPS = 1e-6

kernel_name = "hybrid_pool_swa_sgu_convffn"


def rms_norm(x, g):
    xf = x.astype(jnp.float32)
    y = xf * lax.rsqrt(jnp.mean(xf * xf, axis=-1, keepdims=True) + EPS)
    return (y * g.astype(jnp.float32)).astype(x.dtype)


def rope_tables(positions):
    inv_freq = ROPE_THETA ** (-jnp.arange(0, HEAD_DIM, 2, dtype=jnp.float32) / HEAD_DIM)
    ang = positions.astype(jnp.float32)[..., None] * inv_freq
    return jnp.cos(ang)[:, :, None, :], jnp.sin(ang)[:, :, None, :]


def apply_rope(t, cos, sin):
    tf = t.astype(jnp.float32)
    t1, t2 = jnp.split(tf, 2, axis=-1)
    return jnp.concatenate([t1 * cos - t2 * sin, t2 * cos + t1 * sin], axis=-1).astype(t.dtype)


def pool_mixer(xa, w_pool, pool_scale):
    B, S, _ = xa.shape
    xf = xa.astype(jnp.float32)
    cs = jnp.concatenate([jnp.zeros((B, 1, POOL_WIDTH), jnp.float32), jnp.cumsum(xf, axis=1)], axis=1)
    t = jnp.arange(S)
    pooled = []
    for g, w in enumerate(POOL_WINDOWS):
        c = cs[..., g * POOL_GROUP_DIM:(g + 1) * POOL_GROUP_DIM]
        upper = c[:, 1:]
        lower = jnp.concatenate([jnp.zeros((B, w - 1, POOL_GROUP_DIM), jnp.float32), c[:, :S - w + 1]], axis=1)
        count = jnp.minimum(t + 1, w).astype(jnp.float32)[None, :, None]
        pooled.append((upper - lower) / count)
    pooled = jnp.stack(pooled, axis=2)
    diff = (pooled - xf.reshape(B, S, POOL_GROUPS, POOL_GROUP_DIM)).astype(xa.dtype)
    mixed = jnp.einsum('bsgc,gcd->bsgd', diff, w_pool).reshape(B, S, POOL_WIDTH)
    return mixed * pool_scale


def swa_attention(q, k, v, sinks):
    B, S = q.shape[:2]
    nb = S // ATTN_BLOCK
    qb = q.reshape(B, nb, ATTN_BLOCK, N_KV_HEADS, Q_PER_KV, HEAD_DIM)

    def band(t):
        tb = t.reshape(B, nb, ATTN_BLOCK, N_KV_HEADS, HEAD_DIM)
        prev = jnp.concatenate([jnp.zeros_like(tb[:, :1]), tb[:, :-1]], axis=1)
        return jnp.concatenate([prev, tb], axis=2)

    kb, vb = band(k), band(v)
    scores = jnp.einsum('bnqhgd,bnkhd->bnhgqk', qb, kb).astype(jnp.float32) * (HEAD_DIM ** -0.5)
    qi = jnp.arange(ATTN_BLOCK)[:, None]
    kj = jnp.arange(2 * ATTN_BLOCK)[None, :]
    dist = qi + ATTN_BLOCK - kj
    in_window = (dist >= 0) & (dist < WINDOW)
    key_pos = (jnp.arange(nb)[:, None, None] - 1) * ATTN_BLOCK + kj[None]
    mask = in_window[None] & (key_pos >= 0)
    scores = jnp.where(mask[None, :, None, None], scores, -jnp.inf)
    sink = sinks.astype(jnp.float32).reshape(N_KV_HEADS, Q_PER_KV)[None, None, :, :, None, None]
    sink = jnp.broadcast_to(sink, scores.shape[:-1] + (1,))
    probs = jax.nn.softmax(jnp.concatenate([scores, sink], axis=-1), axis=-1)[..., :-1]
    out = jnp.einsum('bnhgqk,bnkhd->bnqhgd', probs.astype(v.dtype), vb)
    return out.reshape(B, S, ATTN_WIDTH)


def spatial_gating(u, v, w_s, b_s, v_norm):
    B, S, _ = u.shape
    nc = S // CHUNK
    u = jax.nn.gelu(u)
    vg = rms_norm(jax.nn.gelu(v).reshape(B, S, SGU_GROUPS, SGU_GROUP_DIM), v_norm)
    vc = vg.reshape(B, nc, CHUNK, SGU_GROUPS, SGU_GROUP_DIM)
    w_causal = jnp.tril(w_s)
    s = jnp.einsum('gts,bnsgc->bntgc', w_causal, vc) + b_s.T[None, None, :, :, None]
    return u * s.reshape(B, S, SGU_WIDTH)


def causal_dwconv(x, w, b):
    C = x.shape[-1]
    y = lax.conv_general_dilated(
        x, w[:, None, :].astype(x.dtype), window_strides=(1,),
        padding=((CONV_WIDTH - 1, 0),), dimension_numbers=('NWC', 'WIO', 'NWC'),
        feature_group_count=C)
    return y + b


def _fwd_setup_inputs(seed: int = 0) -> dict:
    key = jax.random.key(seed)
    ks = jax.random.split(key, 24)
    f32 = jnp.float32
    nrm = lambda k, shape, s: jax.random.normal(k, shape, f32) * s
    return {
        "x": nrm(ks[0], (BATCH, SEQ, D_MODEL), 1.0),
        "positions": (jnp.arange(SEQ, dtype=jnp.int32)[None, :]
                      + jax.random.randint(ks[1], (BATCH, 1), 0, SEQ, dtype=jnp.int32)),
        "norm1": 1.0 + nrm(ks[2], (DEPTH, D_MODEL), 0.02),
        "w_in": nrm(ks[3], (DEPTH, D_MODEL, IN_COLS), D_MODEL ** -0.5),
        "q_norm": 1.0 + nrm(ks[4], (DEPTH, HEAD_DIM), 0.02),
        "k_norm": 1.0 + nrm(ks[5], (DEPTH, HEAD_DIM), 0.02),
        "sinks": nrm(ks[6], (DEPTH, N_Q_HEADS), 0.5),
        "w_pool": nrm(ks[7], (DEPTH, POOL_GROUPS, POOL_GROUP_DIM, POOL_GROUP_DIM), POOL_GROUP_DIM ** -0.5),
        "pool_scale": 1.0 + nrm(ks[8], (DEPTH, POOL_WIDTH), 0.02),
        "sgu_v_norm": 1.0 + nrm(ks[9], (DEPTH, SGU_GROUP_DIM), 0.02),
        "w_s": nrm(ks[10], (DEPTH, SGU_GROUPS, CHUNK, CHUNK), CHUNK ** -0.5),
        "b_s": 1.0 + nrm(ks[11], (DEPTH, SGU_GROUPS, CHUNK), 0.02),
        "w_proj_a": nrm(ks[12], (DEPTH, POOL_WIDTH, D_MODEL), POOL_WIDTH ** -0.5),
        "w_proj_b": nrm(ks[13], (DEPTH, ATTN_WIDTH, D_MODEL), ATTN_WIDTH ** -0.5),
        "w_proj_c": nrm(ks[14], (DEPTH, SGU_WIDTH, D_MODEL), SGU_WIDTH ** -0.5),
        "w_out": nrm(ks[15], (DEPTH, D_MODEL, D_MODEL), D_MODEL ** -0.5),
        "norm2": 1.0 + nrm(ks[16], (DEPTH, D_MODEL), 0.02),
        "w_up": nrm(ks[17], (DEPTH, D_MODEL, 2 * D_FF), D_MODEL ** -0.5),
        "conv_w": nrm(ks[18], (DEPTH, CONV_WIDTH, 2 * D_FF), CONV_WIDTH ** -0.5),
        "conv_b": nrm(ks[19], (DEPTH, 2 * D_FF), 0.02),
        "w_down": nrm(ks[20], (DEPTH, D_FF, D_MODEL), D_FF ** -0.5),
    }


def _fwd_reference(x, positions, norm1, w_in, q_norm, k_norm, sinks, w_pool, pool_scale,
              sgu_v_norm, w_s, b_s, w_proj_a, w_proj_b, w_proj_c, w_out, norm2,
              w_up, conv_w, conv_b, w_down):
    B, S, _ = x.shape
    cos, sin = rope_tables(positions)
    splits = np.cumsum([POOL_WIDTH, ATTN_WIDTH, KV_WIDTH, KV_WIDTH, SGU_WIDTH, SGU_WIDTH]).tolist()
    for l in range(DEPTH):
        h = rms_norm(x, norm1[l])
        z = h @ w_in[l]
        x_pool, q, k, v, u_s, v_s, gates = jnp.split(z, splits, axis=-1)

        y_a = pool_mixer(x_pool, w_pool[l], pool_scale[l]) @ w_proj_a[l]

        q = apply_rope(rms_norm(q.reshape(B, S, N_Q_HEADS, HEAD_DIM), q_norm[l]), cos, sin)
        k = apply_rope(rms_norm(k.reshape(B, S, N_KV_HEADS, HEAD_DIM), k_norm[l]), cos, sin)
        v = v.reshape(B, S, N_KV_HEADS, HEAD_DIM)
        y_b = swa_attention(q, k, v, sinks[l]) @ w_proj_b[l]

        y_c = spatial_gating(u_s, v_s, w_s[l], b_s[l], sgu_v_norm[l]) @ w_proj_c[l]

        g = jax.nn.sigmoid(gates.astype(jnp.float32)).astype(x.dtype).reshape(B, S, N_BRANCHES, D_MODEL)
        merged = g[:, :, 0] * y_a + g[:, :, 1] * y_b + g[:, :, 2] * y_c
        x = x + merged @ w_out[l]

        h = rms_norm(x, norm2[l])
        up = causal_dwconv(h @ w_up[l], conv_w[l], conv_b[l])
        gate, val = jnp.split(up, 2, axis=-1)
        x = x + (jax.nn.silu(gate) * val) @ w_down[l]
    return x


import jax as _jax
import jax.numpy as _jnp

TWIN_FORMAT = 'train_step'
FWD_PARAMS = ['x', 'positions', 'norm1', 'w_in', 'q_norm', 'k_norm', 'sinks', 'w_pool', 'pool_scale', 'sgu_v_norm', 'w_s', 'b_s', 'w_proj_a', 'w_proj_b', 'w_proj_c', 'w_out', 'norm2', 'w_up', 'conv_w', 'conv_b', 'w_down']
TWIN_WEIGHTS = ['norm1', 'w_in', 'q_norm', 'k_norm', 'sinks', 'w_pool', 'pool_scale', 'sgu_v_norm', 'w_s', 'b_s', 'w_proj_a', 'w_proj_b', 'w_proj_c', 'w_out', 'norm2', 'w_up', 'conv_w', 'conv_b', 'w_down']
TWIN_DIFF_INPUT = 'x'
TWIN_INPUTS = ['x', 'positions', 'norm1', 'w_in', 'q_norm', 'k_norm', 'sinks', 'w_pool', 'pool_scale', 'sgu_v_norm', 'w_s', 'b_s', 'w_proj_a', 'w_proj_b', 'w_proj_c', 'w_out', 'norm2', 'w_up', 'conv_w', 'conv_b', 'w_down', 'loss_target', 'm_norm1', 'm_w_in', 'm_q_norm', 'm_k_norm', 'm_sinks', 'm_w_pool', 'm_pool_scale', 'm_sgu_v_norm', 'm_w_s', 'm_b_s', 'm_w_proj_a', 'm_w_proj_b', 'm_w_proj_c', 'm_w_out', 'm_norm2', 'm_w_up', 'm_conv_w', 'm_conv_b', 'm_w_down', 'v_norm1', 'v_w_in', 'v_q_norm', 'v_k_norm', 'v_sinks', 'v_w_pool', 'v_pool_scale', 'v_sgu_v_norm', 'v_w_s', 'v_b_s', 'v_w_proj_a', 'v_w_proj_b', 'v_w_proj_c', 'v_w_out', 'v_norm2', 'v_w_up', 'v_conv_w', 'v_conv_b', 'v_w_down']
TWIN_OUTPUTS = ['loss', 'grad_x', 'grad_norm1', 'grad_w_in', 'grad_q_norm', 'grad_k_norm', 'grad_sinks', 'grad_w_pool', 'grad_pool_scale', 'grad_sgu_v_norm', 'grad_w_s', 'grad_b_s', 'grad_w_proj_a', 'grad_w_proj_b', 'grad_w_proj_c', 'grad_w_out', 'grad_norm2', 'grad_w_up', 'grad_conv_w', 'grad_conv_b', 'grad_w_down', 'delta_norm1', 'delta_w_in', 'delta_q_norm', 'delta_k_norm', 'delta_sinks', 'delta_w_pool', 'delta_pool_scale', 'delta_sgu_v_norm', 'delta_w_s', 'delta_b_s', 'delta_w_proj_a', 'delta_w_proj_b', 'delta_w_proj_c', 'delta_w_out', 'delta_norm2', 'delta_w_up', 'delta_conv_w', 'delta_conv_b', 'delta_w_down', 'new_m_norm1', 'new_m_w_in', 'new_m_q_norm', 'new_m_k_norm', 'new_m_sinks', 'new_m_w_pool', 'new_m_pool_scale', 'new_m_sgu_v_norm', 'new_m_w_s', 'new_m_b_s', 'new_m_w_proj_a', 'new_m_w_proj_b', 'new_m_w_proj_c', 'new_m_w_out', 'new_m_norm2', 'new_m_w_up', 'new_m_conv_w', 'new_m_conv_b', 'new_m_w_down', 'new_v_norm1', 'new_v_w_in', 'new_v_q_norm', 'new_v_k_norm', 'new_v_sinks', 'new_v_w_pool', 'new_v_pool_scale', 'new_v_sgu_v_norm', 'new_v_w_s', 'new_v_b_s', 'new_v_w_proj_a', 'new_v_w_proj_b', 'new_v_w_proj_c', 'new_v_w_out', 'new_v_norm2', 'new_v_w_up', 'new_v_conv_w', 'new_v_conv_b', 'new_v_w_down']
TWIN_LEAF_KINDS = {'loss': 'loss', 'grad_x': 'grad_x', 'grad_norm1': 'grad_w', 'grad_w_in': 'grad_w', 'grad_q_norm': 'grad_w', 'grad_k_norm': 'grad_w', 'grad_sinks': 'grad_w', 'grad_w_pool': 'grad_w', 'grad_pool_scale': 'grad_w', 'grad_sgu_v_norm': 'grad_w', 'grad_w_s': 'grad_w', 'grad_b_s': 'grad_w', 'grad_w_proj_a': 'grad_w', 'grad_w_proj_b': 'grad_w', 'grad_w_proj_c': 'grad_w', 'grad_w_out': 'grad_w', 'grad_norm2': 'grad_w', 'grad_w_up': 'grad_w', 'grad_conv_w': 'grad_w', 'grad_conv_b': 'grad_w', 'grad_w_down': 'grad_w', 'delta_norm1': 'delta_w', 'delta_w_in': 'delta_w', 'delta_q_norm': 'delta_w', 'delta_k_norm': 'delta_w', 'delta_sinks': 'delta_w', 'delta_w_pool': 'delta_w', 'delta_pool_scale': 'delta_w', 'delta_sgu_v_norm': 'delta_w', 'delta_w_s': 'delta_w', 'delta_b_s': 'delta_w', 'delta_w_proj_a': 'delta_w', 'delta_w_proj_b': 'delta_w', 'delta_w_proj_c': 'delta_w', 'delta_w_out': 'delta_w', 'delta_norm2': 'delta_w', 'delta_w_up': 'delta_w', 'delta_conv_w': 'delta_w', 'delta_conv_b': 'delta_w', 'delta_w_down': 'delta_w', 'new_m_norm1': 'new_m', 'new_m_w_in': 'new_m', 'new_m_q_norm': 'new_m', 'new_m_k_norm': 'new_m', 'new_m_sinks': 'new_m', 'new_m_w_pool': 'new_m', 'new_m_pool_scale': 'new_m', 'new_m_sgu_v_norm': 'new_m', 'new_m_w_s': 'new_m', 'new_m_b_s': 'new_m', 'new_m_w_proj_a': 'new_m', 'new_m_w_proj_b': 'new_m', 'new_m_w_proj_c': 'new_m', 'new_m_w_out': 'new_m', 'new_m_norm2': 'new_m', 'new_m_w_up': 'new_m', 'new_m_conv_w': 'new_m', 'new_m_conv_b': 'new_m', 'new_m_w_down': 'new_m', 'new_v_norm1': 'new_v', 'new_v_w_in': 'new_v', 'new_v_q_norm': 'new_v', 'new_v_k_norm': 'new_v', 'new_v_sinks': 'new_v', 'new_v_w_pool': 'new_v', 'new_v_pool_scale': 'new_v', 'new_v_sgu_v_norm': 'new_v', 'new_v_w_s': 'new_v', 'new_v_b_s': 'new_v', 'new_v_w_proj_a': 'new_v', 'new_v_w_proj_b': 'new_v', 'new_v_w_proj_c': 'new_v', 'new_v_w_out': 'new_v', 'new_v_norm2': 'new_v', 'new_v_w_up': 'new_v', 'new_v_conv_w': 'new_v', 'new_v_conv_b': 'new_v', 'new_v_w_down': 'new_v'}


def _forward(args):
    return _fwd_reference(*[args[k] for k in FWD_PARAMS])


def _output_shape():
    def fwd():
        inp = _fwd_setup_inputs(0)
        return _fwd_reference(*[inp[k] for k in FWD_PARAMS])
    out = _jax.eval_shape(fwd)
    return out.shape, out.dtype

N_MICROBATCH = 1
ADAM_LR = 0.001
ADAM_B1 = 0.9
ADAM_B2 = 0.999
ADAM_EPS = 1e-08
ADAM_WD = 0.01
ADAM_STEP = 10
PER_EXAMPLE_BATCH_AXIS = {'x': 0, 'positions': 0, 'loss_target': 0}
SHARED_INPUTS = []
_WEIGHT_DTYPES = {'norm1': _jnp.float32, 'w_in': _jnp.float32, 'q_norm': _jnp.float32, 'k_norm': _jnp.float32, 'sinks': _jnp.float32, 'w_pool': _jnp.float32, 'pool_scale': _jnp.float32, 'sgu_v_norm': _jnp.float32, 'w_s': _jnp.float32, 'b_s': _jnp.float32, 'w_proj_a': _jnp.float32, 'w_proj_b': _jnp.float32, 'w_proj_c': _jnp.float32, 'w_out': _jnp.float32, 'norm2': _jnp.float32, 'w_up': _jnp.float32, 'conv_w': _jnp.float32, 'conv_b': _jnp.float32, 'w_down': _jnp.float32}
MOMENT_SCALE = {'norm1': 3.139933e+01, 'w_in': 1.109033e+00, 'q_norm': 3.279977e+00, 'k_norm': 3.264647e+00, 'sinks': 9.897021e-01, 'w_pool': 7.798593e+00, 'pool_scale': 5.933041e+01, 'sgu_v_norm': 6.467475e+01, 'w_s': 5.061837e+00, 'b_s': 1.683169e+01, 'w_proj_a': 1.887485e+00, 'w_proj_b': 7.161573e-01, 'w_proj_c': 4.102887e+00, 'w_out': 4.134248e+00, 'norm2': 4.964770e+01, 'w_up': 1.094297e+00, 'conv_w': 7.270311e+00, 'conv_b': 6.842562e+00, 'w_down': 9.206152e-01}


def _to_microbatches(a, axis):
    t = _jnp.moveaxis(a, axis, 0)
    t = t.reshape((N_MICROBATCH, t.shape[0] // N_MICROBATCH) + t.shape[1:])
    return _jnp.moveaxis(t, 1, axis + 1)


def setup_inputs(seed: int = 0) -> dict:
    inp = _fwd_setup_inputs(seed)
    key = _jax.random.fold_in(_jax.random.key(seed), 7919)
    shape, _ = _output_shape()
    out = dict(inp)
    out["loss_target"] = _jax.random.normal(_jax.random.fold_in(key, 0), shape, _jnp.float32)
    for i, name in enumerate(TWIN_WEIGHTS):
        w = inp[name].astype(_jnp.float32)
        if MOMENT_SCALE is None:
            s = _jnp.sqrt(_jnp.mean(_jnp.square(w)) + 1e-30)
        else:
            s = MOMENT_SCALE[name]
        km, kv = _jax.random.split(_jax.random.fold_in(key, i + 1))
        out[name] = w
        out["m_" + name] = s * _jax.random.normal(km, w.shape, _jnp.float32)
        out["v_" + name] = (s * s) * _jax.random.uniform(kv, w.shape, _jnp.float32, 0.5, 1.5)
    if N_MICROBATCH > 1:
        for name, axis in PER_EXAMPLE_BATCH_AXIS.items():
            out[name] = _to_microbatches(out[name], axis)
    return {'x': out['x'], 'positions': out['positions'], 'norm1': out['norm1'], 'w_in': out['w_in'], 'q_norm': out['q_norm'], 'k_norm': out['k_norm'], 'sinks': out['sinks'], 'w_pool': out['w_pool'], 'pool_scale': out['pool_scale'], 'sgu_v_norm': out['sgu_v_norm'], 'w_s': out['w_s'], 'b_s': out['b_s'], 'w_proj_a': out['w_proj_a'], 'w_proj_b': out['w_proj_b'], 'w_proj_c': out['w_proj_c'], 'w_out': out['w_out'], 'norm2': out['norm2'], 'w_up': out['w_up'], 'conv_w': out['conv_w'], 'conv_b': out['conv_b'], 'w_down': out['w_down'], 'loss_target': out['loss_target'], 'm_norm1': out['m_norm1'], 'm_w_in': out['m_w_in'], 'm_q_norm': out['m_q_norm'], 'm_k_norm': out['m_k_norm'], 'm_sinks': out['m_sinks'], 'm_w_pool': out['m_w_pool'], 'm_pool_scale': out['m_pool_scale'], 'm_sgu_v_norm': out['m_sgu_v_norm'], 'm_w_s': out['m_w_s'], 'm_b_s': out['m_b_s'], 'm_w_proj_a': out['m_w_proj_a'], 'm_w_proj_b': out['m_w_proj_b'], 'm_w_proj_c': out['m_w_proj_c'], 'm_w_out': out['m_w_out'], 'm_norm2': out['m_norm2'], 'm_w_up': out['m_w_up'], 'm_conv_w': out['m_conv_w'], 'm_conv_b': out['m_conv_b'], 'm_w_down': out['m_w_down'], 'v_norm1': out['v_norm1'], 'v_w_in': out['v_w_in'], 'v_q_norm': out['v_q_norm'], 'v_k_norm': out['v_k_norm'], 'v_sinks': out['v_sinks'], 'v_w_pool': out['v_w_pool'], 'v_pool_scale': out['v_pool_scale'], 'v_sgu_v_norm': out['v_sgu_v_norm'], 'v_w_s': out['v_w_s'], 'v_b_s': out['v_b_s'], 'v_w_proj_a': out['v_w_proj_a'], 'v_w_proj_b': out['v_w_proj_b'], 'v_w_proj_c': out['v_w_proj_c'], 'v_w_out': out['v_w_out'], 'v_norm2': out['v_norm2'], 'v_w_up': out['v_w_up'], 'v_conv_w': out['v_conv_w'], 'v_conv_b': out['v_conv_b'], 'v_w_down': out['v_w_down']}


def _loss(weights, diff, rest, loss_target):
    with _jax.named_scope("forward"):
        args = {**rest, TWIN_DIFF_INPUT: diff, **{k: w.astype(_WEIGHT_DTYPES[k]) for k, w in weights.items()}}
        y = _forward(args)
    with _jax.named_scope("loss_head"):
        err = _jnp.square(y.astype(_jnp.float32) - loss_target)
        return 0.5 * _jnp.sum(_jnp.mean(err, axis=-1)) if err.ndim else 0.5 * err


def _adamw(w, g, m, v):
    m = ADAM_B1 * m + (1.0 - ADAM_B1) * g
    v = ADAM_B2 * v + (1.0 - ADAM_B2) * _jnp.square(g)
    m_hat = m / (1.0 - ADAM_B1 ** ADAM_STEP)
    v_hat = v / (1.0 - ADAM_B2 ** ADAM_STEP)
    delta = -ADAM_LR * (m_hat / (_jnp.sqrt(v_hat) + ADAM_EPS) + ADAM_WD * w)
    return delta, m, v


def reference(x, positions, norm1, w_in, q_norm, k_norm, sinks, w_pool, pool_scale, sgu_v_norm, w_s, b_s, w_proj_a, w_proj_b, w_proj_c, w_out, norm2, w_up, conv_w, conv_b, w_down, loss_target, m_norm1, m_w_in, m_q_norm, m_k_norm, m_sinks, m_w_pool, m_pool_scale, m_sgu_v_norm, m_w_s, m_b_s, m_w_proj_a, m_w_proj_b, m_w_proj_c, m_w_out, m_norm2, m_w_up, m_conv_w, m_conv_b, m_w_down, v_norm1, v_w_in, v_q_norm, v_k_norm, v_sinks, v_w_pool, v_pool_scale, v_sgu_v_norm, v_w_s, v_b_s, v_w_proj_a, v_w_proj_b, v_w_proj_c, v_w_out, v_norm2, v_w_up, v_conv_w, v_conv_b, v_w_down):
    given = dict(x=x, positions=positions, norm1=norm1, w_in=w_in, q_norm=q_norm, k_norm=k_norm, sinks=sinks, w_pool=w_pool, pool_scale=pool_scale, sgu_v_norm=sgu_v_norm, w_s=w_s, b_s=b_s, w_proj_a=w_proj_a, w_proj_b=w_proj_b, w_proj_c=w_proj_c, w_out=w_out, norm2=norm2, w_up=w_up, conv_w=conv_w, conv_b=conv_b, w_down=w_down, loss_target=loss_target, m_norm1=m_norm1, m_w_in=m_w_in, m_q_norm=m_q_norm, m_k_norm=m_k_norm, m_sinks=m_sinks, m_w_pool=m_w_pool, m_pool_scale=m_pool_scale, m_sgu_v_norm=m_sgu_v_norm, m_w_s=m_w_s, m_b_s=m_b_s, m_w_proj_a=m_w_proj_a, m_w_proj_b=m_w_proj_b, m_w_proj_c=m_w_proj_c, m_w_out=m_w_out, m_norm2=m_norm2, m_w_up=m_w_up, m_conv_w=m_conv_w, m_conv_b=m_conv_b, m_w_down=m_w_down, v_norm1=v_norm1, v_w_in=v_w_in, v_q_norm=v_q_norm, v_k_norm=v_k_norm, v_sinks=v_sinks, v_w_pool=v_w_pool, v_pool_scale=v_pool_scale, v_sgu_v_norm=v_sgu_v_norm, v_w_s=v_w_s, v_b_s=v_b_s, v_w_proj_a=v_w_proj_a, v_w_proj_b=v_w_proj_b, v_w_proj_c=v_w_proj_c, v_w_out=v_w_out, v_norm2=v_norm2, v_w_up=v_w_up, v_conv_w=v_conv_w, v_conv_b=v_conv_b, v_w_down=v_w_down)
    weights = {n: given[n] for n in TWIN_WEIGHTS}
    shared = {n: given[n] for n in SHARED_INPUTS}
    per_example = {n: given[n] for n in ['x', 'positions']}
    grad_fn = _jax.value_and_grad(_loss, argnums=(0, 1))

    def one_microbatch(ex, loss_target):
        ex = dict(ex)
        diff = ex.pop(TWIN_DIFF_INPUT)
        return grad_fn(weights, diff, {**shared, **ex}, loss_target)

    if N_MICROBATCH == 1:
        loss, (grad_w, grad_x) = one_microbatch(per_example, given["loss_target"])
    else:
        def body(carry, xs):
            loss_sum, grad_sum = carry
            l_k, (gw_k, gx_k) = one_microbatch(xs[0], xs[1])
            with _jax.named_scope("update"):
                return (loss_sum + l_k, _jax.tree.map(_jnp.add, grad_sum, gw_k)), gx_k

        init = (_jnp.zeros((), _jnp.float32), _jax.tree.map(_jnp.zeros_like, weights))
        (loss, grad_w), grad_x = _jax.lax.scan(body, init, (per_example, given["loss_target"]))
    with _jax.named_scope("update"):
        delta_w, new_m, new_v = {}, {}, {}
        for n in TWIN_WEIGHTS:
            delta_w[n], new_m[n], new_v[n] = _adamw(weights[n], grad_w[n], given["m_" + n], given["v_" + n])
    return (loss, grad_x, *[grad_w[n] for n in TWIN_WEIGHTS], *[delta_w[n] for n in TWIN_WEIGHTS],
            *[new_m[n] for n in TWIN_WEIGHTS], *[new_v[n] for n in TWIN_WEIGHTS])
```

```python
import functools
import math

import jax
import jax.numpy as jnp
from jax import lax
from jax.experimental import pallas as pl
from jax.experimental.pallas import tpu as pltpu

F32 = jnp.float32
BF16 = jnp.bfloat16

D_MODEL = 1024
DEPTH = 2
HEAD_DIM = 64
N_Q_HEADS = 8
Q_PER_KV = 4
BLOCK = 128
POOL_W = 256
ATTN_W = 512
KV_W = 128
SGU_W = 256
IN_COLS = 4608
GATE_COL0 = 1536
D_FF = 2816
EPS = 1e-6
ROPE_THETA = 10000.0
N_DEV = 8
LANES = 128
HALO_POOL = 16
HALO_CONV = 8

ADAM_LR = 0.001
ADAM_B1 = 0.9
ADAM_B2 = 0.999
ADAM_EPS = 1e-08
ADAM_WD = 0.01
ADAM_STEP = 10

VMEM_LIMIT = 48 * 1024 * 1024

PACK_COLS = 1024
SHARDED = ("w_in", "w_proj_a", "w_proj_b", "w_proj_c", "w_out", "w_up", "w_down")
PACK_ROWS = {"w_in": 576, "w_proj_a": 32, "w_proj_b": 64, "w_proj_c": 32, "w_out": 128, "w_up": 704, "w_down": 352}
LAYER_ROWS = sum(PACK_ROWS.values())
BIG_ROWS = DEPTH * LAYER_ROWS
CONV_ROWS = 8
BIG_ROWS_PAD = 3840
SMALL_ROW_TILE = 256
SMALL = ("norm1", "q_norm", "k_norm", "sinks", "w_pool", "pool_scale", "sgu_v_norm", "w_s", "b_s", "norm2", "conv_b")

_GELU_C = math.sqrt(2.0 / math.pi)
_GELU_A = 0.044715


def _params(*sem):
    return pltpu.CompilerParams(dimension_semantics=sem, vmem_limit_bytes=VMEM_LIMIT)


def _tile(n, prefs):
    for t in prefs:
        if t <= n and n % t == 0:
            return t
    return n


def _head_mean_matrix():
    r = lax.broadcasted_iota(jnp.int32, (LANES, LANES), 0)
    c = lax.broadcasted_iota(jnp.int32, (LANES, LANES), 1)
    return jnp.where((r >= HEAD_DIM) == (c >= HEAD_DIM), 1.0 / HEAD_DIM, 0.0).astype(F32)


def _head_mean(v, bd):
    return jnp.dot(v, bd, precision=lax.Precision.HIGHEST, preferred_element_type=F32)


def _rot_half(t):
    lane = lax.broadcasted_iota(jnp.int32, t.shape, 1)
    return jnp.where((lane & 32) == 0, pltpu.roll(t, LANES - 32, 1), pltpu.roll(t, 32, 1))


def _norm_rope(t, gn, cosf, sinf, bd):
    r = lax.rsqrt(_head_mean(t * t, bd) + EPS)
    n = t * r
    y = n * gn
    return y * cosf + _rot_half(y) * sinf, n, r


def _norm_rope_bwd(d, t, n, r, gn, cosf, sinf, bd):
    dy = d * cosf + _rot_half(d * sinf)
    dgn = jnp.sum(dy * n, axis=0, keepdims=True)
    u = dy * gn
    dt = r * u - t * (r * r * r) * _head_mean(t * u, bd)
    return dt, dgn


def _gelu(x):
    t = jnp.tanh(_GELU_C * (x + _GELU_A * (x * x * x)))
    return 0.5 * x * (1.0 + t), t


def _gelu_grad(x, t):
    return 0.5 * (1.0 + t) + 0.5 * x * (1.0 - t * t) * (_GELU_C * (1.0 + 3.0 * _GELU_A * x * x))


def _sigmoid(x):
    return jax.nn.sigmoid(x)


def _dot(a, b, ca=1, cb=0):
    return lax.dot_general(a.astype(BF16), b.astype(BF16), (((ca,), (cb,)), ((), ())), preferred_element_type=F32)


def _mm(a, b, *, ta=False, tb=False, add=None, out_dtype=F32, name):
    m = a.shape[1] if ta else a.shape[0]
    k = a.shape[0] if ta else a.shape[1]
    n = b.shape[0] if tb else b.shape[1]
    tm = _tile(m, (1024, 1408, 512, 256, 128))
    tn = _tile(n, (1024, 1152, 1408, 512, 256, 128))
    tk = _tile(k, (1024, 1152, 1408, 512, 256, 128))
    nk = k // tk
    has_add = add is not None

    def body(*refs):
        a_ref, b_ref = refs[0], refs[1]
        add_ref = refs[2] if has_add else None
        o_ref = refs[3] if has_add else refs[2]
        p = _dot(a_ref[...], b_ref[...], 0 if ta else 1, 1 if tb else 0)

        def finish(r):
            if has_add:
                r = r + add_ref[...]
            o_ref[...] = r.astype(out_dtype)

        if nk == 1:
            finish(p)
        else:
            acc_ref = refs[-1]
            kk = pl.program_id(2)

            @pl.when(kk == 0)
            def _():
                acc_ref[...] = p

            @pl.when(kk > 0)
            def _():
                acc_ref[...] += p

            @pl.when(kk == nk - 1)
            def _():
                finish(acc_ref[...])

    a_spec = pl.BlockSpec((tk, tm), lambda i, j, kk: (kk, i)) if ta else pl.BlockSpec((tm, tk), lambda i, j, kk: (i, kk))
    b_spec = pl.BlockSpec((tn, tk), lambda i, j, kk: (j, kk)) if tb else pl.BlockSpec((tk, tn), lambda i, j, kk: (kk, j))
    o_spec = pl.BlockSpec((tm, tn), lambda i, j, kk: (i, j))
    in_specs = [a_spec, b_spec] + ([o_spec] if has_add else [])
    args = (a, b) + ((add,) if has_add else ())
    return pl.pallas_call(
        body, name=name, grid=(m // tm, n // tn, nk), in_specs=in_specs, out_specs=o_spec,
        out_shape=jax.ShapeDtypeStruct((m, n), out_dtype),
        scratch_shapes=[pltpu.VMEM((tm, tn), F32)] if nk > 1 else [],
        compiler_params=_params("parallel", "parallel", "arbitrary"),
    )(*args)


def _rms_fwd(x, g, name):
    s, d = x.shape
    tr = _tile(s, (512, 256, 128))

    def body(x_ref, g_ref, h_ref):
        xv = x_ref[...]
        r = lax.rsqrt(jnp.mean(xv * xv, axis=-1, keepdims=True) + EPS)
        h_ref[...] = ((xv * r) * g_ref[...]).astype(BF16)

    return pl.pallas_call(
        body, name=name, grid=(s // tr,),
        in_specs=[pl.BlockSpec((tr, d), lambda i: (i, 0)), pl.BlockSpec((1, d), lambda i: (0, 0))],
        out_specs=pl.BlockSpec((tr, d), lambda i: (i, 0)),
        out_shape=jax.ShapeDtypeStruct((s, d), BF16), compiler_params=_params("parallel"),
    )(x, g.reshape(1, d))


def _rms_bwd(x, g, dh, dres, name):
    s, d = x.shape
    tr = _tile(s, (512, 256, 128))

    def body(x_ref, g_ref, dh_ref, dres_ref, dx_ref, dg_ref):
        xv = x_ref[...]
        r = lax.rsqrt(jnp.mean(xv * xv, axis=-1, keepdims=True) + EPS)
        dhv = dh_ref[...]
        u = dhv * g_ref[...]
        dx_ref[...] = dres_ref[...] + (r * u - xv * (r * r * r) * jnp.mean(xv * u, axis=-1, keepdims=True))
        part = jnp.sum(dhv * (xv * r), axis=0, keepdims=True)

        @pl.when(pl.program_id(0) == 0)
        def _():
            dg_ref[...] = part

        @pl.when(pl.program_id(0) > 0)
        def _():
            dg_ref[...] += part

    row = pl.BlockSpec((tr, d), lambda i: (i, 0))
    vec = pl.BlockSpec((1, d), lambda i: (0, 0))
    return pl.pallas_call(
        body, name=name, grid=(s // tr,), in_specs=[row, vec, row, row], out_specs=[row, vec],
        out_shape=[jax.ShapeDtypeStruct((s, d), F32), jax.ShapeDtypeStruct((1, d), F32)],
        compiler_params=_params("arbitrary"),
    )(x, g.reshape(1, d), dh, dres)


def _loss_head(y, target):
    s, d = y.shape
    tr = _tile(s, (512, 256, 128))

    def body(y_ref, t_ref, dy_ref, l_ref):
        err = y_ref[...] - t_ref[...]
        dy_ref[...] = err * (1.0 / d)
        part = jnp.sum(jnp.sum(err * err, axis=-1, keepdims=True) * (1.0 / d), axis=0, keepdims=True) * 0.5
        part = jnp.broadcast_to(part, (8, LANES))

        @pl.when(pl.program_id(0) == 0)
        def _():
            l_ref[...] = part

        @pl.when(pl.program_id(0) > 0)
        def _():
            l_ref[...] += part

    row = pl.BlockSpec((tr, d), lambda i: (i, 0))
    acc = pl.BlockSpec((8, LANES), lambda i: (0, 0))
    return pl.pallas_call(
        body, name="loss_head", grid=(s // tr,), in_specs=[row, row], out_specs=[row, acc],
        out_shape=[jax.ShapeDtypeStruct((s, d), F32), jax.ShapeDtypeStruct((8, LANES), F32)],
        compiler_params=_params("arbitrary"),
    )(y, target)


def _pool_lane_select(lane, v2, v4, v8, v16):
    return jnp.where(lane < 64, v2, jnp.where(lane < 128, v4, jnp.where(lane < 192, v8, v16)))


def _pool_diff(xc, xp, row0):
    n = BLOCK + HALO_POOL
    cat = jnp.concatenate([xp, xc], axis=0)
    s2 = cat + pltpu.roll(cat, 1, 0)
    s4 = s2 + pltpu.roll(s2, 2, 0)
    s8 = s4 + pltpu.roll(s4, 4, 0)
    s16 = s8 + pltpu.roll(s8, 8, 0)
    lane = lax.broadcasted_iota(jnp.int32, (n, POOL_W), 1)
    wsum = _pool_lane_select(lane, s2, s4, s8, s16)[HALO_POOL:]
    return wsum / _pool_count(row0, BLOCK) - xc


def _pool_count(row0, rows):
    lane = lax.broadcasted_iota(jnp.int32, (rows, POOL_W), 1)
    t = lax.broadcasted_iota(jnp.int32, (rows, POOL_W), 0) + row0
    return jnp.minimum(t + 1, _pool_lane_select(lane, 2, 4, 8, 16)).astype(F32)


def _pool_fwd(z, bdw, scale, name):
    s = z.shape[0]
    nb = s // BLOCK

    def body(xc_ref, xp_ref, w_ref, sc_ref, a_ref):
        i = pl.program_id(0)
        xp = jnp.where(i > 0, xp_ref[...], 0.0)
        diff = _pool_diff(xc_ref[...], xp, i * BLOCK)
        a_ref[...] = (_dot(diff, w_ref[...]) * sc_ref[...]).astype(BF16)

    return pl.pallas_call(
        body, name=name, grid=(nb,),
        in_specs=[pl.BlockSpec((BLOCK, POOL_W), lambda i: (i, 0)),
                  pl.BlockSpec((HALO_POOL, POOL_W), lambda i: (jnp.maximum(i * (BLOCK // HALO_POOL) - 1, 0), 0)),
                  pl.BlockSpec((POOL_W, POOL_W), lambda i: (0, 0)),
                  pl.BlockSpec((1, POOL_W), lambda i: (0, 0))],
        out_specs=pl.BlockSpec((BLOCK, POOL_W), lambda i: (i, 0)),
        out_shape=jax.ShapeDtypeStruct((s, POOL_W), BF16), compiler_params=_params("parallel"),
    )(z, z, bdw, scale.reshape(1, POOL_W))


def _pool_bwd(z, da, bdw, scale, name):
    s = z.shape[0]
    nb = s // BLOCK
    per = BLOCK // HALO_POOL
    n = BLOCK + HALO_POOL

    def body(xc_ref, xp_ref, dac_ref, dan_ref, w_ref, sc_ref, dx_ref, dw_ref, dsc_ref):
        i = pl.program_id(0)
        xp = jnp.where(i > 0, xp_ref[...], 0.0)
        diff = _pool_diff(xc_ref[...], xp, i * BLOCK)
        mixed = _dot(diff, w_ref[...])
        dac = dac_ref[...]
        dan = jnp.where(i < nb - 1, dan_ref[...], 0.0)
        dmix = jnp.concatenate([dac, dan], axis=0) * sc_ref[...]
        ddiff = _dot(dmix, w_ref[...], 1, 1)
        e = ddiff / _pool_count(i * BLOCK, n)
        f2 = e + pltpu.roll(e, n - 1, 0)
        f4 = f2 + pltpu.roll(f2, n - 2, 0)
        f8 = f4 + pltpu.roll(f4, n - 4, 0)
        f16 = f8 + pltpu.roll(f8, n - 8, 0)
        lane = lax.broadcasted_iota(jnp.int32, (n, POOL_W), 1)
        back = _pool_lane_select(lane, f2, f4, f8, f16)
        dx_ref[...] = (back[:BLOCK] - ddiff[:BLOCK]).astype(BF16)
        dw = _dot(diff, dmix[:BLOCK], 0, 0)
        dsc = jnp.sum(dac * mixed, axis=0, keepdims=True)

        @pl.when(i == 0)
        def _():
            dw_ref[...] = dw
            dsc_ref[...] = dsc

        @pl.when(i > 0)
        def _():
            dw_ref[...] += dw
            dsc_ref[...] += dsc

    blk = pl.BlockSpec((BLOCK, POOL_W), lambda i: (i, 0))
    return pl.pallas_call(
        body, name=name, grid=(nb,),
        in_specs=[blk, pl.BlockSpec((HALO_POOL, POOL_W), lambda i: (jnp.maximum(i * per - 1, 0), 0)),
                  blk, pl.BlockSpec((HALO_POOL, POOL_W), lambda i: (jnp.minimum((i + 1) * per, nb * per - 1), 0)),
                  pl.BlockSpec((POOL_W, POOL_W), lambda i: (0, 0)), pl.BlockSpec((1, POOL_W), lambda i: (0, 0))],
        out_specs=[blk, pl.BlockSpec((POOL_W, POOL_W), lambda i: (0, 0)), pl.BlockSpec((1, POOL_W), lambda i: (0, 0))],
        out_shape=[jax.ShapeDtypeStruct((s, POOL_W), BF16), jax.ShapeDtypeStruct((POOL_W, POOL_W), F32),
                   jax.ShapeDtypeStruct((1, POOL_W), F32)],
        compiler_params=_params("arbitrary"),
    )(z, z, da, da, bdw, scale.reshape(1, POOL_W))


def _attn_setup(zc_ref, zp_ref, cc_ref, cp_ref, sc_ref, sp_ref, qn_ref, kn_ref, bd):
    q = []
    for j in range(ATTN_W // LANES):
        t = zc_ref[:, POOL_W + j * LANES:POOL_W + (j + 1) * LANES]
        q.append((t,) + _norm_rope(t, qn_ref[...], cc_ref[...], sc_ref[...], bd))
    kc_raw = zc_ref[:, POOL_W + ATTN_W:POOL_W + ATTN_W + KV_W]
    kc = _norm_rope(kc_raw, kn_ref[...], cc_ref[...], sc_ref[...], bd)[0]
    kp = _norm_rope(zp_ref[:, :KV_W], kn_ref[...], cp_ref[...], sp_ref[...], bd)[0]
    kband = jnp.concatenate([kp, kc], axis=0).astype(BF16)
    vband = jnp.concatenate([zp_ref[:, KV_W:], zc_ref[:, POOL_W + ATTN_W + KV_W:POOL_W + ATTN_W + 2 * KV_W]], axis=0).astype(BF16)
    return q, kband, vband


def _attn_mask(i):
    row = lax.broadcasted_iota(jnp.int32, (BLOCK, 2 * BLOCK), 0)
    col = lax.broadcasted_iota(jnp.int32, (BLOCK, 2 * BLOCK), 1)
    dist = row + BLOCK - col
    return (dist >= 0) & (dist < BLOCK) & ((col >= BLOCK) | (i > 0))


def _to_kv_lanes(t, h):
    kvh = h // Q_PER_KV
    if (h % 2) != kvh:
        t = pltpu.roll(t, HEAD_DIM, 1)
    lane = lax.broadcasted_iota(jnp.int32, t.shape, 1)
    return jnp.where((lane >= HEAD_DIM) == (kvh == 1), t, 0.0)


def _from_kv_lanes(t, h):
    kvh = h // Q_PER_KV
    lane = lax.broadcasted_iota(jnp.int32, t.shape, 1)
    t = jnp.where((lane >= HEAD_DIM) == (kvh == 1), t, 0.0)
    if (h % 2) != kvh:
        t = pltpu.roll(t, HEAD_DIM, 1)
    return t


def _attn_probs(qh, kband, mask, sink):
    sc = _dot(qh, kband, 1, 1) * (HEAD_DIM ** -0.5)
    sc = jnp.where(mask, sc, -1e30)
    m = jnp.maximum(jnp.max(sc, axis=1, keepdims=True), sink)
    p = jnp.exp(sc - m)
    psink = jnp.exp(sink - m)
    den = jnp.sum(p, axis=1, keepdims=True) + psink
    return p / den, psink / den


def _attn_specs(nb):
    cur = lambda i: (i, 0)
    prev = lambda i: (jnp.maximum(i - 1, 0), 0)
    tab = lambda f: pl.BlockSpec((BLOCK, LANES), f)
    vec = pl.BlockSpec((1, LANES), lambda i: (0, 0))
    return [pl.BlockSpec((BLOCK, 1024), cur),
            pl.BlockSpec((BLOCK, 2 * KV_W), lambda i: (jnp.maximum(i - 1, 0), 3)),
            tab(cur), tab(prev), tab(cur), tab(prev), vec, vec,
            pl.BlockSpec(memory_space=pltpu.SMEM)]


def _attn_fwd(z, cosf, sinf, qn, kn, sinks, name):
    s = z.shape[0]
    nb = s // BLOCK

    def body(zc_ref, zp_ref, cc_ref, cp_ref, sc_ref, sp_ref, qn_ref, kn_ref, sink_ref, o_ref):
        i = pl.program_id(0)
        bd = _head_mean_matrix()
        q, kband, vband = _attn_setup(zc_ref, zp_ref, cc_ref, cp_ref, sc_ref, sp_ref, qn_ref, kn_ref, bd)
        mask = _attn_mask(i)
        out = [None] * (ATTN_W // LANES)
        for h in range(N_Q_HEADS):
            qh = _to_kv_lanes(q[h // 2][1], h)
            probs, _ = _attn_probs(qh, kband, mask, sink_ref[h])
            o = _from_kv_lanes(_dot(probs, vband), h)
            out[h // 2] = o if out[h // 2] is None else out[h // 2] + o
        for j, o in enumerate(out):
            o_ref[:, j * LANES:(j + 1) * LANES] = o.astype(BF16)

    return pl.pallas_call(
        body, name=name, grid=(nb,), in_specs=_attn_specs(nb),
        out_specs=pl.BlockSpec((BLOCK, ATTN_W), lambda i: (i, 0)),
        out_shape=jax.ShapeDtypeStruct((s, ATTN_W), BF16), compiler_params=_params("parallel"),
    )(z, z, cosf, cosf, sinf, sinf, qn, kn, sinks)


def _attn_bwd(z, cosf, sinf, qn, kn, sinks, d_out, name):
    s = z.shape[0]
    nb = s // BLOCK
    nt = ATTN_W // LANES

    def body(zc_ref, zp_ref, cc_ref, cp_ref, sc_ref, sp_ref, qn_ref, kn_ref, sink_ref, do_ref,
             dq_ref, dkc_ref, dkp_ref, dvc_ref, dvp_ref, dqn_ref, dsink_ref):
        i = pl.program_id(0)
        bd = _head_mean_matrix()
        q, kband, vband = _attn_setup(zc_ref, zp_ref, cc_ref, cp_ref, sc_ref, sp_ref, qn_ref, kn_ref, bd)
        mask = _attn_mask(i)

        @pl.when(i == 0)
        def _():
            dqn_ref[...] = jnp.zeros_like(dqn_ref)
            dsink_ref[...] = jnp.zeros_like(dsink_ref)

        dq = [None] * nt
        dk = jnp.zeros((2 * BLOCK, KV_W), F32)
        dv = jnp.zeros((2 * BLOCK, KV_W), F32)
        for h in range(N_Q_HEADS):
            qh = _to_kv_lanes(q[h // 2][1], h)
            probs, psink = _attn_probs(qh, kband, mask, sink_ref[h])
            doh = _to_kv_lanes(do_ref[:, (h // 2) * LANES:(h // 2 + 1) * LANES], h)
            dp = _dot(doh, vband, 1, 1)
            delta = jnp.sum(dp * probs, axis=1, keepdims=True)
            ds = (probs * (dp - delta)) * (HEAD_DIM ** -0.5)
            dsink = jnp.sum(-psink * delta, axis=0, keepdims=True)
            dsink_ref[h:h + 1, :] += jnp.broadcast_to(dsink, (1, LANES))
            t = _from_kv_lanes(_dot(ds, kband), h)
            dq[h // 2] = t if dq[h // 2] is None else dq[h // 2] + t
            dk = dk + _dot(ds, qh, 0, 0)
            dv = dv + _dot(probs, doh, 0, 0)
        dgn = jnp.zeros((1, LANES), F32)
        for j in range(nt):
            t, _, n, r = q[j]
            dt, g = _norm_rope_bwd(dq[j], t, n, r, qn_ref[...], cc_ref[...], sc_ref[...], bd)
            dq_ref[:, j * LANES:(j + 1) * LANES] = dt.astype(BF16)
            dgn = dgn + g
        dqn_ref[...] += jnp.broadcast_to(dgn, (8, LANES))
        dkp_ref[...] = dk[:BLOCK]
        dkc_ref[...] = dk[BLOCK:]
        dvp_ref[...] = dv[:BLOCK]
        dvc_ref[...] = dv[BLOCK:]

        @pl.when(i == nb - 1)
        def _():
            acc = dqn_ref[...]
            dqn_ref[...] = acc + pltpu.roll(acc, HEAD_DIM, 1)

    blk = lambda w: pl.BlockSpec((BLOCK, w), lambda i: (i, 0))
    acc = pl.BlockSpec((8, LANES), lambda i: (0, 0))
    kv = jax.ShapeDtypeStruct((s, KV_W), F32)
    return pl.pallas_call(
        body, name=name, grid=(nb,), in_specs=_attn_specs(nb) + [blk(ATTN_W)],
        out_specs=[blk(ATTN_W), blk(KV_W), blk(KV_W), blk(KV_W), blk(KV_W), acc, acc],
        out_shape=[jax.ShapeDtypeStruct((s, ATTN_W), BF16), kv, kv, kv, kv,
                   jax.ShapeDtypeStruct((8, LANES), F32), jax.ShapeDtypeStruct((8, LANES), F32)],
        compiler_params=_params("arbitrary"),
    )(z, z, cosf, cosf, sinf, sinf, qn, kn, sinks, d_out)


def _kv_post(z, cosf, sinf, kn, dkc, dkp, dvc, dvp, dxp, dq, duv, dz, name):
    s = z.shape[0]
    nb = s // BLOCK

    def body(zk_ref, c_ref, s_ref, kn_ref, dkc_ref, dkp_ref, dvc_ref, dvp_ref, dxp_ref, dq_ref, duv_ref, dz_in,
             dz_ref, dkn_ref):
        j = pl.program_id(0)
        bd = _head_mean_matrix()
        last = j == nb - 1
        d = dkc_ref[...] + jnp.where(last, 0.0, dkp_ref[...])
        t = zk_ref[:, :KV_W]
        _, n, r = _norm_rope(t, kn_ref[...], c_ref[...], s_ref[...], bd)
        dt, g = _norm_rope_bwd(d, t, n, r, kn_ref[...], c_ref[...], s_ref[...], bd)
        dvv = dvc_ref[...] + jnp.where(last, 0.0, dvp_ref[...])
        dz_ref[:, 0:POOL_W] = dxp_ref[...]
        dz_ref[:, POOL_W:POOL_W + ATTN_W] = dq_ref[...]
        dz_ref[:, POOL_W + ATTN_W:POOL_W + ATTN_W + KV_W] = dt.astype(BF16)
        dz_ref[:, POOL_W + ATTN_W + KV_W:POOL_W + ATTN_W + 2 * KV_W] = dvv.astype(BF16)
        dz_ref[:, POOL_W + ATTN_W + 2 * KV_W:GATE_COL0] = duv_ref[...]

        @pl.when(j == 0)
        def _():
            dkn_ref[...] = jnp.zeros_like(dkn_ref)

        dkn_ref[...] += jnp.broadcast_to(g, (8, LANES))

        @pl.when(last)
        def _():
            acc = dkn_ref[...]
            dkn_ref[...] = acc + pltpu.roll(acc, HEAD_DIM, 1)

    cur = lambda w: pl.BlockSpec((BLOCK, w), lambda j: (j, 0))
    nxt = pl.BlockSpec((BLOCK, KV_W), lambda j: (jnp.minimum(j + 1, nb - 1), 0))
    vec = pl.BlockSpec((1, LANES), lambda j: (0, 0))
    return pl.pallas_call(
        body, name=name, grid=(nb,),
        in_specs=[pl.BlockSpec((BLOCK, 2 * KV_W), lambda j: (j, 3)), cur(LANES), cur(LANES), vec,
                  cur(KV_W), nxt, cur(KV_W), nxt, cur(POOL_W), cur(ATTN_W), cur(2 * SGU_W),
                  pl.BlockSpec(memory_space=pl.ANY)],
        out_specs=[pl.BlockSpec((BLOCK, GATE_COL0), lambda j: (j, 0)), pl.BlockSpec((8, LANES), lambda j: (0, 0))],
        out_shape=[jax.ShapeDtypeStruct(dz.shape, dz.dtype), jax.ShapeDtypeStruct((8, LANES), F32)],
        input_output_aliases={11: 0}, compiler_params=_params("arbitrary"),
    )(z, cosf, sinf, kn, dkc, dkp, dvc, dvp, dxp, dq, duv, dz)


def _sgu_setup(z_ref, ws_ref, vn_ref, bd):
    us = z_ref[:, :SGU_W]
    vs = z_ref[:, SGU_W:]
    ug, tu = _gelu(us)
    gv, tv = _gelu(vs)
    rr = jnp.concatenate([lax.rsqrt(_head_mean(gv[:, k * LANES:(k + 1) * LANES] ** 2, bd) + EPS) for k in range(2)], axis=1)
    vg = (gv * rr) * vn_ref[...]
    tril = lax.broadcasted_iota(jnp.int32, (BLOCK, BLOCK), 0) >= lax.broadcasted_iota(jnp.int32, (BLOCK, BLOCK), 1)
    w = [jnp.where(tril, ws_ref[g], 0.0).astype(BF16) for g in range(4)]
    return us, vs, ug, tu, gv, tv, rr, vg, w, tril


def _group_select(parts):
    lane = lax.broadcasted_iota(jnp.int32, parts[0].shape, 1)
    return _pool_lane_select(lane, *parts)


def _sgu_fwd(z, ws, bcol, vn, name):
    s = z.shape[0]
    nb = s // BLOCK

    def body(z_ref, ws_ref, b_ref, vn_ref, c_ref):
        bd = _head_mean_matrix()
        _, _, ug, _, _, _, _, vg, w, _ = _sgu_setup(z_ref, ws_ref, vn_ref, bd)
        sg = _group_select([_dot(w[g], vg) for g in range(4)]) + b_ref[...]
        c_ref[...] = (ug * sg).astype(BF16)

    return pl.pallas_call(
        body, name=name, grid=(nb,),
        in_specs=[pl.BlockSpec((BLOCK, 2 * SGU_W), lambda i: (i, 2)), pl.BlockSpec((4, BLOCK, BLOCK), lambda i: (0, 0, 0)),
                  pl.BlockSpec((BLOCK, SGU_W), lambda i: (0, 0)), pl.BlockSpec((1, SGU_W), lambda i: (0, 0))],
        out_specs=pl.BlockSpec((BLOCK, SGU_W), lambda i: (i, 0)),
        out_shape=jax.ShapeDtypeStruct((s, SGU_W), BF16), compiler_params=_params("parallel"),
    )(z, ws, bcol, vn)


def _sgu_bwd(z, ws, bcol, vn, dc, name):
    s = z.shape[0]
    nb = s // BLOCK

    def body(z_ref, ws_ref, b_ref, vn_ref, dc_ref, duv_ref, dws_ref, db_ref, dvn_ref):
        i = pl.program_id(0)
        bd = _head_mean_matrix()
        us, vs, ug, tu, gv, tv, rr, vg, w, tril = _sgu_setup(z_ref, ws_ref, vn_ref, bd)
        sg = _group_select([_dot(w[g], vg) for g in range(4)]) + b_ref[...]
        dcv = dc_ref[...]
        dug = dcv * sg
        dsg = dcv * ug
        lane = lax.broadcasted_iota(jnp.int32, dsg.shape, 1)

        @pl.when(i == 0)
        def _():
            dws_ref[...] = jnp.zeros_like(dws_ref)
            db_ref[...] = jnp.zeros_like(db_ref)
            dvn_ref[...] = jnp.zeros_like(dvn_ref)

        for g in range(4):
            dsg_g = jnp.where((lane >= g * HEAD_DIM) & (lane < (g + 1) * HEAD_DIM), dsg, 0.0)
            dws_ref[g] += jnp.where(tril, _dot(dsg_g, vg, 1, 1), 0.0)
        dvg = _group_select([_dot(w[g], dsg, 0, 0) for g in range(4)])
        db_ref[...] += dsg
        n = gv * rr
        part = jnp.sum(dvg * n, axis=0, keepdims=True)
        dvn_ref[...] += jnp.broadcast_to(part[:, :LANES] + part[:, LANES:], (8, LANES))
        u = dvg * vn_ref[...]
        tu_ = gv * u
        mean = jnp.concatenate([_head_mean(tu_[:, k * LANES:(k + 1) * LANES], bd) for k in range(2)], axis=1)
        dgv = rr * u - gv * (rr * rr * rr) * mean
        duv_ref[:, :SGU_W] = (dug * _gelu_grad(us, tu)).astype(BF16)
        duv_ref[:, SGU_W:] = (dgv * _gelu_grad(vs, tv)).astype(BF16)

        @pl.when(i == nb - 1)
        def _():
            acc = dvn_ref[...]
            dvn_ref[...] = acc + pltpu.roll(acc, HEAD_DIM, 1)
            for k in range(2):
                db_ref[:, k * LANES:(k + 1) * LANES] = _head_mean(db_ref[:, k * LANES:(k + 1) * LANES], bd) * float(HEAD_DIM)

    return pl.pallas_call(
        body, name=name, grid=(nb,),
        in_specs=[pl.BlockSpec((BLOCK, 2 * SGU_W), lambda i: (i, 2)), pl.BlockSpec((4, BLOCK, BLOCK), lambda i: (0, 0, 0)),
                  pl.BlockSpec((BLOCK, SGU_W), lambda i: (0, 0)), pl.BlockSpec((1, SGU_W), lambda i: (0, 0)),
                  pl.BlockSpec((BLOCK, SGU_W), lambda i: (i, 0))],
        out_specs=[pl.BlockSpec((BLOCK, 2 * SGU_W), lambda i: (i, 0)), pl.BlockSpec((4, BLOCK, BLOCK), lambda i: (0, 0, 0)),
                   pl.BlockSpec((BLOCK, SGU_W), lambda i: (0, 0)), pl.BlockSpec((8, LANES), lambda i: (0, 0))],
        out_shape=[jax.ShapeDtypeStruct((s, 2 * SGU_W), BF16), jax.ShapeDtypeStruct((4, BLOCK, BLOCK), F32),
                   jax.ShapeDtypeStruct((BLOCK, SGU_W), F32), jax.ShapeDtypeStruct((8, LANES), F32)],
        compiler_params=_params("arbitrary"),
    )(z, ws, bcol, vn, dc)


MERGE_TN = 512
MERGE_TM = 256


def _merge_fwd(a, b, c, wpa, wpb, wpc, z, name):
    s = z.shape[0]
    tm = _tile(s, (MERGE_TM, BLOCK))
    gate0 = GATE_COL0 // MERGE_TN

    def body(a_ref, b_ref, c_ref, wa_ref, wb_ref, wc_ref, g0_ref, g1_ref, g2_ref, o_ref):
        r = _sigmoid(g0_ref[...]) * _dot(a_ref[...], wa_ref[...])
        r = r + _sigmoid(g1_ref[...]) * _dot(b_ref[...], wb_ref[...])
        r = r + _sigmoid(g2_ref[...]) * _dot(c_ref[...], wc_ref[...])
        o_ref[...] = r.astype(BF16)

    x_spec = lambda w: pl.BlockSpec((tm, w), lambda i, n: (i, 0))
    w_spec = lambda w: pl.BlockSpec((w, MERGE_TN), lambda i, n: (0, n))
    g_spec = lambda br: pl.BlockSpec((tm, MERGE_TN), lambda i, n: (i, gate0 + 2 * br + n))
    return pl.pallas_call(
        body, name=name, grid=(s // tm, D_MODEL // MERGE_TN),
        in_specs=[x_spec(POOL_W), x_spec(ATTN_W), x_spec(SGU_W), w_spec(POOL_W), w_spec(ATTN_W), w_spec(SGU_W),
                  g_spec(0), g_spec(1), g_spec(2)],
        out_specs=pl.BlockSpec((tm, MERGE_TN), lambda i, n: (i, n)),
        out_shape=jax.ShapeDtypeStruct((s, D_MODEL), BF16), compiler_params=_params("parallel", "parallel"),
    )(a, b, c, wpa, wpb, wpc, z, z, z)


def _branch_bwd(br, xb, wp, z, dm, dz, name):
    s = z.shape[0]
    kb = xb.shape[1]
    tm = _tile(s, (MERGE_TM, BLOCK))
    gate0 = GATE_COL0 // MERGE_TN
    aliased = dz is not None

    def body(*refs):
        x_ref, w_ref, g_ref, dm_ref = refs[:4]
        dz_ref, dy_ref, dx_ref = refs[-3:]
        n = pl.program_id(1)
        y = _dot(x_ref[...], w_ref[...])
        sg = _sigmoid(g_ref[...])
        dmv = dm_ref[...]
        dy = (dmv * sg).astype(BF16)
        dy_ref[...] = dy
        dz_ref[...] = ((dmv * y) * (sg * (1.0 - sg))).astype(BF16)
        dx = _dot(dy, w_ref[...], 1, 1)

        @pl.when(n == 0)
        def _():
            dx_ref[...] = dx

        @pl.when(n > 0)
        def _():
            dx_ref[...] += dx

    in_specs = [pl.BlockSpec((tm, kb), lambda i, n: (i, 0)), pl.BlockSpec((kb, MERGE_TN), lambda i, n: (0, n)),
                pl.BlockSpec((tm, MERGE_TN), lambda i, n: (i, gate0 + 2 * br + n)),
                pl.BlockSpec((tm, MERGE_TN), lambda i, n: (i, n))]
    args = [xb, wp, z, dm]
    if aliased:
        in_specs.append(pl.BlockSpec(memory_space=pl.ANY))
        args.append(dz)
    return pl.pallas_call(
        body, name=name, grid=(s // tm, D_MODEL // MERGE_TN), in_specs=in_specs,
        out_specs=[pl.BlockSpec((tm, MERGE_TN), lambda i, n: (i, gate0 + 2 * br + n)),
                   pl.BlockSpec((tm, MERGE_TN), lambda i, n: (i, n)),
                   pl.BlockSpec((tm, kb), lambda i, n: (i, 0))],
        out_shape=[jax.ShapeDtypeStruct((s, IN_COLS), BF16), jax.ShapeDtypeStruct((s, D_MODEL), BF16),
                   jax.ShapeDtypeStruct((s, kb), F32)],
        input_output_aliases={4: 0} if aliased else {},
        compiler_params=_params("parallel", "arbitrary"),
    )(*args)


FFN_TM = 256
FFN_TC = 1408


def _conv3(cur, prev, w_ref, b_ref):
    cat = jnp.concatenate([prev, cur], axis=0)
    x1 = pltpu.roll(cat, 1, 0)[HALO_CONV:]
    x2 = pltpu.roll(cat, 2, 0)[HALO_CONV:]
    return w_ref[0:1, :] * x2 + w_ref[1:2, :] * x1 + w_ref[2:3, :] * cur + b_ref[...], x1, x2


def _ffn_specs(s, tm, rows_first):
    per = tm // HALO_CONV
    if rows_first:
        cur = pl.BlockSpec((tm, FFN_TC), lambda i, j: (i, j))
        prev = pl.BlockSpec((HALO_CONV, FFN_TC), lambda i, j: (jnp.maximum(i * per - 1, 0), j))
        w = pl.BlockSpec((3, FFN_TC), lambda i, j: (0, j))
        b = pl.BlockSpec((1, FFN_TC), lambda i, j: (0, j))
    else:
        cur = pl.BlockSpec((tm, FFN_TC), lambda j, i: (i, j))
        prev = pl.BlockSpec((HALO_CONV, FFN_TC), lambda j, i: (jnp.maximum(i * per - 1, 0), j))
        w = pl.BlockSpec((3, FFN_TC), lambda j, i: (0, j))
        b = pl.BlockSpec((1, FFN_TC), lambda j, i: (0, j))
    return cur, prev, w, b


def _ffn_act_fwd(upg, upv, cwg, cwv, cbg, cbv, name):
    s = upg.shape[0]
    tm = _tile(s, (FFN_TM, BLOCK))
    cur, prev, w, b = _ffn_specs(s, tm, True)

    def body(g_ref, gp_ref, v_ref, vp_ref, wg_ref, wv_ref, bg_ref, bv_ref, o_ref):
        first = pl.program_id(0) == 0
        gate = _conv3(g_ref[...], jnp.where(first, 0.0, gp_ref[...]), wg_ref, bg_ref)[0]
        val = _conv3(v_ref[...], jnp.where(first, 0.0, vp_ref[...]), wv_ref, bv_ref)[0]
        o_ref[...] = ((gate * _sigmoid(gate)) * val).astype(BF16)

    return pl.pallas_call(
        body, name=name, grid=(s // tm, D_FF // FFN_TC), in_specs=[cur, prev, cur, prev, w, w, b, b], out_specs=cur,
        out_shape=jax.ShapeDtypeStruct((s, D_FF), BF16), compiler_params=_params("parallel", "parallel"),
    )(upg, upg, upv, upv, cwg, cwv, cbg, cbv)


def _ffn_act_bwd(upg, upv, cwg, cwv, cbg, cbv, dact, name):
    s = upg.shape[0]
    tm = _tile(s, (FFN_TM, BLOCK))
    cur, prev, w, b = _ffn_specs(s, tm, False)

    def body(g_ref, gp_ref, v_ref, vp_ref, wg_ref, wv_ref, bg_ref, bv_ref, da_ref, dg_ref, dv_ref, dwg_ref, dwv_ref):
        i = pl.program_id(1)
        first = i == 0
        gc, vc = g_ref[...], v_ref[...]
        gate, g1, g2 = _conv3(gc, jnp.where(first, 0.0, gp_ref[...]), wg_ref, bg_ref)
        val, v1, v2 = _conv3(vc, jnp.where(first, 0.0, vp_ref[...]), wv_ref, bv_ref)
        sg = _sigmoid(gate)
        da = da_ref[...]
        dgate = (da * val) * (sg * (1.0 + gate * (1.0 - sg)))
        dval = da * (gate * sg)
        dg_ref[...] = dgate
        dv_ref[...] = dval

        @pl.when(first)
        def _():
            dwg_ref[...] = jnp.zeros_like(dwg_ref)
            dwv_ref[...] = jnp.zeros_like(dwv_ref)

        for ref, d, x0, x1, x2 in ((dwg_ref, dgate, gc, g1, g2), (dwv_ref, dval, vc, v1, v2)):
            ref[0:1, :] += jnp.sum(d * x2, axis=0, keepdims=True)
            ref[1:2, :] += jnp.sum(d * x1, axis=0, keepdims=True)
            ref[2:3, :] += jnp.sum(d * x0, axis=0, keepdims=True)
            ref[3:4, :] += jnp.sum(d, axis=0, keepdims=True)

    acc = pl.BlockSpec((8, FFN_TC), lambda j, i: (0, j))
    full = jax.ShapeDtypeStruct((s, D_FF), F32)
    accs = jax.ShapeDtypeStruct((8, D_FF), F32)
    return pl.pallas_call(
        body, name=name, grid=(D_FF // FFN_TC, s // tm), in_specs=[cur, prev, cur, prev, w, w, b, b, cur],
        out_specs=[cur, cur, acc, acc], out_shape=[full, full, accs, accs],
        compiler_params=_params("parallel", "arbitrary"),
    )(upg, upg, upv, upv, cwg, cwv, cbg, cbv, dact)


def _conv3_t(d, cw, name):
    s = d.shape[0]
    tm = _tile(s, (FFN_TM, BLOCK))
    per = tm // HALO_CONV
    nrow = s // tm
    n = tm + HALO_CONV

    def body(d_ref, dn_ref, w_ref, o_ref):
        nxt = jnp.where(pl.program_id(0) == nrow - 1, 0.0, dn_ref[...])
        cur = d_ref[...]
        cat = jnp.concatenate([cur, nxt], axis=0)
        d1 = pltpu.roll(cat, n - 1, 0)[:tm]
        d2 = pltpu.roll(cat, n - 2, 0)[:tm]
        o_ref[...] = (w_ref[2:3, :] * cur + w_ref[1:2, :] * d1 + w_ref[0:1, :] * d2).astype(BF16)

    cur = pl.BlockSpec((tm, FFN_TC), lambda i, j: (i, j))
    return pl.pallas_call(
        body, name=name, grid=(nrow, D_FF // FFN_TC),
        in_specs=[cur, pl.BlockSpec((HALO_CONV, FFN_TC), lambda i, j: (jnp.minimum((i + 1) * per, nrow * per - 1), j)),
                  pl.BlockSpec((3, FFN_TC), lambda i, j: (0, j))],
        out_specs=cur, out_shape=jax.ShapeDtypeStruct((s, D_FF), BF16), compiler_params=_params("parallel", "parallel"),
    )(d, d, cw)


def _mesh_place():
    return lax.axis_index("x"), lax.axis_index("y"), lax.axis_index("c")


def _all_gather(shard, name):
    r, c = shard.shape

    def body(x_ref, out_ref, send_sems, recv_sems, local_sem):
        x, y, cc = _mesh_place()
        me, sibling = (x, y, cc), (x, y, 1 - cc)
        chips = [(1 - x, y), (x, 1 - y), (1 - x, 1 - y)]

        def slot(px, py, pc):
            return out_ref.at[4 * px + 2 * py + pc]

        def copy(k, block, to, src=None):
            return pltpu.make_async_remote_copy(
                src_ref=slot(*block) if src is None else src, dst_ref=slot(*block),
                send_sem=send_sems.at[k], recv_sem=recv_sems.at[k], device_id=to, device_id_type=pl.DeviceIdType.MESH)

        mine = pltpu.make_async_copy(x_ref, slot(*me), local_sem)
        mine.start()
        first = [copy(0, me, sibling, src=x_ref)]
        first += [copy(1 + j, me, (*chip, cc), src=x_ref) for j, chip in enumerate(chips)]
        for cp in first:
            cp.start()
        passed = [copy(4 + j, (*chip, cc), sibling) for j, chip in enumerate(chips)]
        for j, chip in enumerate(chips):
            copy(1 + j, (*chip, cc), me).wait_recv()
            passed[j].start()
        copy(0, sibling, me).wait_recv()
        for j, chip in enumerate(chips):
            copy(4 + j, (*chip, 1 - cc), me).wait_recv()
        for cp in first + passed:
            cp.wait_send()
        mine.wait()

    return pl.pallas_call(
        body, name=name, out_shape=jax.ShapeDtypeStruct((N_DEV, r, c), shard.dtype),
        in_specs=[pl.BlockSpec(memory_space=pl.ANY)], out_specs=pl.BlockSpec(memory_space=pl.ANY),
        scratch_shapes=[pltpu.SemaphoreType.DMA((7,)), pltpu.SemaphoreType.DMA((7,)), pltpu.SemaphoreType.DMA(())],
    )(shard)


def _exchange(blocks, name):
    _, r, c = blocks.shape

    def body(g_ref, out_ref, send_sems, recv_sems, local_sem):
        x, y, cc = _mesh_place()
        me = 4 * x + 2 * y + cc
        mine = pltpu.make_async_copy(g_ref.at[me], out_ref.at[me], local_sem)
        mine.start()
        copies = []
        for k in range(1, N_DEV):
            fx, fy, fc = (k >> 2) & 1, (k >> 1) & 1, k & 1
            px = 1 - x if fx else x
            py = 1 - y if fy else y
            pc = 1 - cc if fc else cc
            peer = 4 * px + 2 * py + pc
            copies.append((
                pltpu.make_async_remote_copy(
                    src_ref=g_ref.at[peer], dst_ref=out_ref.at[me], send_sem=send_sems.at[k - 1], recv_sem=recv_sems.at[k - 1],
                    device_id=(px, py, pc), device_id_type=pl.DeviceIdType.MESH),
                pltpu.make_async_remote_copy(
                    src_ref=g_ref.at[peer], dst_ref=out_ref.at[peer], send_sem=send_sems.at[k - 1], recv_sem=recv_sems.at[k - 1],
                    device_id=(px, py, pc), device_id_type=pl.DeviceIdType.MESH)))
        for send, _ in copies:
            send.start()
        for _, land in copies:
            land.wait_recv()
        for send, _ in copies:
            send.wait_send()
        mine.wait()

    return pl.pallas_call(
        body, name=name, out_shape=jax.ShapeDtypeStruct((N_DEV, r, c), blocks.dtype),
        in_specs=[pl.BlockSpec(memory_space=pl.ANY)], out_specs=pl.BlockSpec(memory_space=pl.ANY),
        scratch_shapes=[pltpu.SemaphoreType.DMA((7,)), pltpu.SemaphoreType.DMA((7,)), pltpu.SemaphoreType.DMA(())],
    )(blocks)


def _adamw_sum(parts, w, m, v, name):
    _, r, c = parts.shape
    tr = _tile(r, (256, 128, 64, 32, 16, 8))

    def body(p_ref, w_ref, m_ref, v_ref, g_ref, d_ref, nm_ref, nv_ref):
        g = p_ref[0]
        for k in range(1, N_DEV):
            g = g + p_ref[k]
        nm = ADAM_B1 * m_ref[...] + (1.0 - ADAM_B1) * g
        nv = ADAM_B2 * v_ref[...] + (1.0 - ADAM_B2) * (g * g)
        m_hat = nm / (1.0 - ADAM_B1 ** ADAM_STEP)
        v_hat = nv / (1.0 - ADAM_B2 ** ADAM_STEP)
        g_ref[...] = g
        nm_ref[...] = nm
        nv_ref[...] = nv
        d_ref[...] = -ADAM_LR * (m_hat / (jnp.sqrt(v_hat) + ADAM_EPS) + ADAM_WD * w_ref[...])

    row = pl.BlockSpec((tr, c), lambda i: (i, 0))
    shp = jax.ShapeDtypeStruct((r, c), F32)
    return pl.pallas_call(
        body, name=name, grid=(r // tr,), in_specs=[pl.BlockSpec((N_DEV, tr, c), lambda i: (0, i, 0)), row, row, row],
        out_specs=[row, row, row, row], out_shape=[shp, shp, shp, shp], compiler_params=_params("parallel"),
    )(parts, w, m, v)


def _shard_rows(name, t):
    return t.reshape(PACK_ROWS[name], PACK_COLS)


def _full_to_slots(name, t):
    if name in ("w_out", "w_down"):
        return t.reshape(N_DEV, PACK_ROWS[name], PACK_COLS)
    k, n = t.shape
    return t.reshape(k, N_DEV, n // N_DEV).transpose(1, 0, 2).reshape(N_DEV, PACK_ROWS[name], PACK_COLS)


def _slots_to_full(name, t, shape):
    if name in ("w_out", "w_down"):
        return t.reshape(shape)
    k, n = shape
    return t.reshape(N_DEV, k, n // N_DEV).transpose(1, 0, 2).reshape(k, n)


FULL_SHAPES = {"w_in": (D_MODEL, IN_COLS), "w_proj_a": (POOL_W, D_MODEL), "w_proj_b": (ATTN_W, D_MODEL),
               "w_proj_c": (SGU_W, D_MODEL), "w_out": (D_MODEL, D_MODEL), "w_up": (D_MODEL, 2 * D_FF),
               "w_down": (D_FF, D_MODEL)}


def _pack_local(trees, conv):
    rows = [_shard_rows(n, trees[n][l]) for l in range(DEPTH) for n in SHARDED]
    flat = conv.reshape(-1)
    rows.append(jnp.pad(flat, (0, CONV_ROWS * PACK_COLS - flat.shape[0])).reshape(CONV_ROWS, PACK_COLS))
    rows.append(jnp.zeros((BIG_ROWS_PAD - BIG_ROWS - CONV_ROWS, PACK_COLS), F32))
    return jnp.concatenate(rows, axis=0)


def _unpack_local(buf, shards, conv_shape):
    out, r0 = {n: [] for n in SHARDED}, 0
    for l in range(DEPTH):
        for n in SHARDED:
            out[n].append(buf[r0:r0 + PACK_ROWS[n]].reshape(shards[n].shape[1:]))
            r0 += PACK_ROWS[n]
    res = {n: jnp.stack(out[n]) for n in SHARDED}
    size = conv_shape[0] * conv_shape[1] * conv_shape[2]
    res["conv_w"] = buf[BIG_ROWS:BIG_ROWS + CONV_ROWS].reshape(-1)[:size].reshape(conv_shape)
    return res


def _small_sizes(shapes):
    return [(n, shapes[n], -(-int(math.prod(shapes[n])) // (8 * LANES)) * 8) for n in SMALL]


def _pack_small(tree, shapes):
    rows = []
    for n, shp, nrow in _small_sizes(shapes):
        flat = tree[n].reshape(-1)
        rows.append(jnp.pad(flat, (0, nrow * LANES - flat.shape[0])).reshape(nrow, LANES))
    total = sum(r.shape[0] for r in rows)
    rows.append(jnp.zeros((-total % SMALL_ROW_TILE, LANES), F32))
    return jnp.concatenate(rows, axis=0)


def _unpack_small(buf, shapes):
    out, r0 = {}, 0
    for n, shp, nrow in _small_sizes(shapes):
        out[n] = buf[r0:r0 + nrow].reshape(-1)[:int(math.prod(shp))].reshape(shp)
        r0 += nrow
    return out


def _block_diag(w):
    g = w.shape[0]
    eye = jnp.eye(g, dtype=w.dtype)
    return (eye[:, None, :, None] * w[:, :, None, :]).reshape(g * HEAD_DIM, g * HEAD_DIM)


def kernel(x, positions, norm1, w_in, q_norm, k_norm, sinks, w_pool, pool_scale, sgu_v_norm, w_s, b_s, w_proj_a, w_proj_b, w_proj_c, w_out, norm2, w_up, conv_w, conv_b, w_down, loss_target, m_norm1, m_w_in, m_q_norm, m_k_norm, m_sinks, m_w_pool, m_pool_scale, m_sgu_v_norm, m_w_s, m_b_s, m_w_proj_a, m_w_proj_b, m_w_proj_c, m_w_out, m_norm2, m_w_up, m_conv_w, m_conv_b, m_w_down, v_norm1, v_w_in, v_q_norm, v_k_norm, v_sinks, v_w_pool, v_pool_scale, v_sgu_v_norm, v_w_s, v_b_s, v_w_proj_a, v_w_proj_b, v_w_proj_c, v_w_out, v_norm2, v_w_up, v_conv_w, v_conv_b, v_w_down):
    names = ("norm1", "w_in", "q_norm", "k_norm", "sinks", "w_pool", "pool_scale", "sgu_v_norm", "w_s", "b_s", "w_proj_a",
             "w_proj_b", "w_proj_c", "w_out", "norm2", "w_up", "conv_w", "conv_b", "w_down")
    wts = dict(zip(names, (norm1, w_in, q_norm, k_norm, sinks, w_pool, pool_scale, sgu_v_norm, w_s, b_s, w_proj_a, w_proj_b,
                           w_proj_c, w_out, norm2, w_up, conv_w, conv_b, w_down)))
    mom = dict(zip(names, (m_norm1, m_w_in, m_q_norm, m_k_norm, m_sinks, m_w_pool, m_pool_scale, m_sgu_v_norm, m_w_s, m_b_s,
                           m_w_proj_a, m_w_proj_b, m_w_proj_c, m_w_out, m_norm2, m_w_up, m_conv_w, m_conv_b, m_w_down)))
    var = dict(zip(names, (v_norm1, v_w_in, v_q_norm, v_k_norm, v_sinks, v_w_pool, v_pool_scale, v_sgu_v_norm, v_w_s, v_b_s,
                           v_w_proj_a, v_w_proj_b, v_w_proj_c, v_w_out, v_norm2, v_w_up, v_conv_w, v_conv_b, v_w_down)))
    xs = x[0]
    target = loss_target[0]
    s = xs.shape[0]

    inv_freq = ROPE_THETA ** (-jnp.arange(0, HEAD_DIM, 2, dtype=F32) / HEAD_DIM)
    ang = positions[0].astype(F32)[:, None] * inv_freq
    cosf = jnp.tile(jnp.cos(ang), (1, 4))
    sinf = jnp.tile(jnp.concatenate([-jnp.sin(ang), jnp.sin(ang)], axis=1), (1, 2))

    wpack = jnp.concatenate([_shard_rows(n, wts[n][l]) for l in range(DEPTH) for n in SHARDED], axis=0).astype(BF16)
    wall = _all_gather(wpack, "gather_weights")
    conv_flat = conv_w.reshape(-1)
    conv_local = jnp.pad(conv_flat, (0, CONV_ROWS * PACK_COLS - conv_flat.shape[0])).reshape(CONV_ROWS, PACK_COLS)
    conv_all = _all_gather(conv_local, "gather_conv_w")
    n_conv = DEPTH * 3 * (2 * D_FF // N_DEV)
    conv_full = conv_all.reshape(N_DEV, -1)[:, :n_conv].reshape(N_DEV, DEPTH, 3, 2 * D_FF // N_DEV)
    conv_full = conv_full.transpose(1, 2, 0, 3).reshape(DEPTH, 3, 2 * D_FF)
    full = []
    r0 = 0
    for l in range(DEPTH):
        lw = {}
        for n in SHARDED:
            lw[n] = _slots_to_full(n, wall[:, r0:r0 + PACK_ROWS[n]], FULL_SHAPES[n])
            r0 += PACK_ROWS[n]
        full.append(lw)

    def layer_consts(l):
        return dict(
            bdw=_block_diag(w_pool[l]).astype(BF16), qn=jnp.tile(q_norm[l], 2).reshape(1, LANES),
            kn=jnp.tile(k_norm[l], 2).reshape(1, LANES), vn=jnp.tile(sgu_v_norm[l], 4).reshape(1, SGU_W),
            bcol=jnp.repeat(b_s[l].T, HEAD_DIM, axis=1),
            wug=full[l]["w_up"][:, :D_FF], wuv=full[l]["w_up"][:, D_FF:],
            cwg=conv_full[l][:, :D_FF], cwv=conv_full[l][:, D_FF:],
            cbg=conv_b[l][:D_FF].reshape(1, D_FF), cbv=conv_b[l][D_FF:].reshape(1, D_FF))

    saved = []
    cur = xs
    for l in range(DEPTH):
        fw, k = full[l], layer_consts(l)
        h1 = _rms_fwd(cur, norm1[l], f"rms1_fwd_{l}")
        z = _mm(h1, fw["w_in"], name=f"in_proj_{l}")
        a = _pool_fwd(z, k["bdw"], pool_scale[l], f"pool_fwd_{l}")
        b = _attn_fwd(z, cosf, sinf, k["qn"], k["kn"], sinks[l], f"attn_fwd_{l}")
        c = _sgu_fwd(z, w_s[l], k["bcol"], k["vn"], f"sgu_fwd_{l}")
        merged = _merge_fwd(a, b, c, fw["w_proj_a"], fw["w_proj_b"], fw["w_proj_c"], z, f"merge_fwd_{l}")
        x1 = _mm(merged, fw["w_out"], add=cur, name=f"out_proj_{l}")
        h2 = _rms_fwd(x1, norm2[l], f"rms2_fwd_{l}")
        upg = _mm(h2, k["wug"], name=f"up_gate_{l}")
        upv = _mm(h2, k["wuv"], name=f"up_val_{l}")
        act = _ffn_act_fwd(upg, upv, k["cwg"], k["cwv"], k["cbg"], k["cbv"], f"ffn_act_fwd_{l}")
        x2 = _mm(act, fw["w_down"], add=x1, name=f"down_proj_{l}")
        saved.append(dict(x0=cur, h1=h1, z=z, a=a, b=b, c=c, merged=merged, x1=x1, h2=h2, upg=upg, upv=upv, act=act))
        cur = x2

    dcur, loss_tile = _loss_head(cur, target)
    loss = lax.psum(loss_tile[0, 0], ("x", "y", "c"))

    gfull = [None] * DEPTH
    gsmall = [None] * DEPTH
    gconv = [None] * DEPTH
    for l in reversed(range(DEPTH)):
        fw, k, sv = full[l], layer_consts(l), saved[l]
        dact = _mm(dcur, fw["w_down"], tb=True, name=f"down_proj_bwd_{l}")
        g_down = _mm(sv["act"], dcur, ta=True, name=f"down_proj_wgrad_{l}")
        dgate, dval, dcg, dcv = _ffn_act_bwd(sv["upg"], sv["upv"], k["cwg"], k["cwv"], k["cbg"], k["cbv"], dact, f"ffn_act_bwd_{l}")
        dg0 = _conv3_t(dgate, k["cwg"], f"conv_t_gate_{l}")
        dv0 = _conv3_t(dval, k["cwv"], f"conv_t_val_{l}")
        dh2 = _mm(dg0, k["wug"], tb=True, name=f"up_gate_bwd_{l}")
        dh2 = _mm(dv0, k["wuv"], tb=True, add=dh2, name=f"up_val_bwd_{l}")
        g_up = jnp.concatenate([_mm(sv["h2"], dg0, ta=True, name=f"up_gate_wgrad_{l}"),
                                _mm(sv["h2"], dv0, ta=True, name=f"up_val_wgrad_{l}")], axis=1)
        dx1, g_norm2 = _rms_bwd(sv["x1"], norm2[l], dh2, dcur, f"rms2_bwd_{l}")
        dmerged = _mm(dx1, fw["w_out"], tb=True, name=f"out_proj_bwd_{l}")
        g_out = _mm(sv["merged"], dx1, ta=True, name=f"out_proj_wgrad_{l}")
        dz, dya, da = _branch_bwd(0, sv["a"], fw["w_proj_a"], sv["z"], dmerged, None, f"branch_a_bwd_{l}")
        dz, dyb, db = _branch_bwd(1, sv["b"], fw["w_proj_b"], sv["z"], dmerged, dz, f"branch_b_bwd_{l}")
        dz, dyc, dc = _branch_bwd(2, sv["c"], fw["w_proj_c"], sv["z"], dmerged, dz, f"branch_c_bwd_{l}")
        g_pa = _mm(sv["a"], dya, ta=True, name=f"proj_a_wgrad_{l}")
        g_pb = _mm(sv["b"], dyb, ta=True, name=f"proj_b_wgrad_{l}")
        g_pc = _mm(sv["c"], dyc, ta=True, name=f"proj_c_wgrad_{l}")
        dxp, g_bdw, g_pscale = _pool_bwd(sv["z"], da, k["bdw"], pool_scale[l], f"pool_bwd_{l}")
        dq, dkc, dkp, dvc, dvp, g_qn, g_sink = _attn_bwd(sv["z"], cosf, sinf, k["qn"], k["kn"], sinks[l], db, f"attn_bwd_{l}")
        duv, g_ws, g_bacc, g_vn = _sgu_bwd(sv["z"], w_s[l], k["bcol"], k["vn"], dc, f"sgu_bwd_{l}")
        dz, g_kn = _kv_post(sv["z"], cosf, sinf, k["kn"], dkc, dkp, dvc, dvp, dxp, dq, duv, dz, f"kv_post_{l}")
        dh1 = _mm(dz, fw["w_in"], tb=True, name=f"in_proj_bwd_{l}")
        g_in = _mm(sv["h1"], dz, ta=True, name=f"in_proj_wgrad_{l}")
        dcur, g_norm1 = _rms_bwd(sv["x0"], norm1[l], dh1, dx1, f"rms1_bwd_{l}")
        gfull[l] = dict(w_in=g_in, w_proj_a=g_pa, w_proj_b=g_pb, w_proj_c=g_pc, w_out=g_out, w_up=g_up, w_down=g_down)
        gconv[l] = jnp.concatenate([dcg[0:3], dcv[0:3]], axis=1)
        gsmall[l] = dict(
            norm1=g_norm1[0], q_norm=g_qn[0, :HEAD_DIM], k_norm=g_kn[0, :HEAD_DIM], sinks=g_sink[:, 0],
            w_pool=jnp.stack([g_bdw[g * HEAD_DIM:(g + 1) * HEAD_DIM, g * HEAD_DIM:(g + 1) * HEAD_DIM] for g in range(4)]),
            pool_scale=g_pscale[0], sgu_v_norm=g_vn[0, :HEAD_DIM], w_s=g_ws, b_s=g_bacc[:, ::HEAD_DIM].T,
            norm2=g_norm2[0], conv_b=jnp.concatenate([dcg[3], dcv[3]]))
    grad_x = dcur[None]

    gconv_all = jnp.stack(gconv)
    gconv_slots = gconv_all.reshape(DEPTH, 3, N_DEV, -1).transpose(2, 0, 1, 3).reshape(N_DEV, -1)
    gconv_slots = jnp.pad(gconv_slots, ((0, 0), (0, CONV_ROWS * PACK_COLS - gconv_slots.shape[1]))).reshape(N_DEV, CONV_ROWS, PACK_COLS)
    slots = [_full_to_slots(n, gfull[l][n]) for l in range(DEPTH) for n in SHARDED]
    slots += [gconv_slots, jnp.zeros((N_DEV, BIG_ROWS_PAD - BIG_ROWS - CONV_ROWS, PACK_COLS), F32)]
    parts = _exchange(jnp.concatenate(slots, axis=1), "exchange_grads")
    sh = {n: wts[n] for n in SHARDED}
    g_b, d_b, m_b, v_b = _adamw_sum(
        parts, _pack_local(sh, conv_w), _pack_local({n: mom[n] for n in SHARDED}, m_conv_w),
        _pack_local({n: var[n] for n in SHARDED}, v_conv_w), "adamw_sharded")
    big = [_unpack_local(t, sh, conv_w.shape) for t in (g_b, d_b, m_b, v_b)]

    shapes = {n: wts[n].shape for n in SMALL}
    gs = _pack_small({n: jnp.stack([gsmall[l][n] for l in range(DEPTH)]) for n in SMALL}, shapes)
    gs_all = _all_gather(gs, "gather_small_grads")
    g_s, d_s, m_s, v_s = _adamw_sum(gs_all, _pack_small(wts, shapes), _pack_small(mom, shapes), _pack_small(var, shapes), "adamw_replicated")
    small = [_unpack_small(t, shapes) for t in (g_s, d_s, m_s, v_s)]

    outs = [loss, grad_x]
    for kind in range(4):
        outs += [small[kind][n] if n in SMALL else big[kind][n] for n in names]
    return tuple(outs)
```

```python
import functools
import math

import jax
import jax.numpy as jnp
from jax import lax
from jax.experimental import pallas as pl
from jax.experimental.pallas import tpu as pltpu

F32 = jnp.float32
BF16 = jnp.bfloat16

D_MODEL = 1024
DEPTH = 2
HEAD_DIM = 64
N_Q_HEADS = 8
Q_PER_KV = 4
BLOCK = 128
POOL_W = 256
ATTN_W = 512
KV_W = 128
SGU_W = 256
IN_COLS = 4608
GATE_COL0 = 1536
D_FF = 2816
EPS = 1e-6
ROPE_THETA = 10000.0
N_DEV = 8
LANES = 128
HALO_POOL = 16
HALO_CONV = 8

ADAM_LR = 0.001
ADAM_B1 = 0.9
ADAM_B2 = 0.999
ADAM_EPS = 1e-08
ADAM_WD = 0.01
ADAM_STEP = 10

VMEM_LIMIT = 48 * 1024 * 1024

SHARDED = ("w_in", "w_proj_a", "w_proj_b", "w_proj_c", "w_out", "w_up", "w_down", "conv_w")
ROW_SHARDED = ("w_out", "w_down")
SMALL_ROW_TILE = 256
SMALL = ("norm1", "q_norm", "k_norm", "sinks", "w_pool", "pool_scale", "sgu_v_norm", "w_s", "b_s", "norm2", "conv_b")

_GELU_C = math.sqrt(2.0 / math.pi)
_GELU_A = 0.044715


def _params(*sem):
    return pltpu.CompilerParams(dimension_semantics=sem, vmem_limit_bytes=VMEM_LIMIT)


def _tile(n, prefs):
    for t in prefs:
        if t <= n and n % t == 0:
            return t
    return n


def _head_mean_matrix():
    r = lax.broadcasted_iota(jnp.int32, (LANES, LANES), 0)
    c = lax.broadcasted_iota(jnp.int32, (LANES, LANES), 1)
    return jnp.where((r >= HEAD_DIM) == (c >= HEAD_DIM), 1.0 / HEAD_DIM, 0.0).astype(F32)


def _head_mean(v, bd):
    return jnp.dot(v, bd, precision=lax.Precision.HIGHEST, preferred_element_type=F32)


def _rot_half(t):
    lane = lax.broadcasted_iota(jnp.int32, t.shape, 1)
    return jnp.where((lane & 32) == 0, pltpu.roll(t, LANES - 32, 1), pltpu.roll(t, 32, 1))


def _norm_rope(t, gn, cosf, sinf, bd):
    r = lax.rsqrt(_head_mean(t * t, bd) + EPS)
    n = t * r
    y = n * gn
    return y * cosf + _rot_half(y) * sinf, n, r


def _norm_rope_bwd(d, t, n, r, gn, cosf, sinf, bd):
    dy = d * cosf + _rot_half(d * sinf)
    dgn = jnp.sum(dy * n, axis=0, keepdims=True)
    u = dy * gn
    dt = r * u - t * (r * r * r) * _head_mean(t * u, bd)
    return dt, dgn


def _gelu(x):
    t = jnp.tanh(_GELU_C * (x + _GELU_A * (x * x * x)))
    return 0.5 * x * (1.0 + t), t


def _gelu_grad(x, t):
    return 0.5 * (1.0 + t) + 0.5 * x * (1.0 - t * t) * (_GELU_C * (1.0 + 3.0 * _GELU_A * x * x))


def _sigmoid(x):
    return jax.nn.sigmoid(x)


def _dot(a, b, ca=1, cb=0):
    return lax.dot_general(a.astype(BF16), b.astype(BF16), (((ca,), (cb,)), ((), ())), preferred_element_type=F32)


def _mm(a, b, *, ta=False, tb=False, add=None, out_dtype=F32, name, b_n=None, b_k=None, out_cols=None, out_into=None):
    m = a.shape[1] if ta else a.shape[0]
    k = a.shape[0] if ta else a.shape[1]
    n = b_n[1] if b_n else (b.shape[0] if tb else b.shape[1])
    tm = _tile(m, (1024, 1408, 512, 256, 128))
    tn = _tile(n, (1024, 1152, 1408, 512, 256, 128))
    tk = _tile(k, (1024, 1152, 1408, 512, 256, 128))
    nk = k // tk
    n0 = b_n[0] // tn if b_n else 0
    k0 = b_k[0] // tk if b_k else 0
    o0, n_out = (out_cols[0] // tn, out_cols[1]) if out_cols else (0, n)
    has_add = add is not None
    n_in = 2 + has_add + (out_into is not None)

    def body(*refs):
        a_ref, b_ref = refs[0], refs[1]
        add_ref = refs[2] if has_add else None
        o_ref = refs[n_in]
        p = _dot(a_ref[...], b_ref[...], 0 if ta else 1, 1 if tb else 0)

        def finish(r):
            if has_add:
                r = r + add_ref[...]
            o_ref[...] = r.astype(out_dtype)

        if nk == 1:
            finish(p)
        else:
            acc_ref = refs[-1]
            kk = pl.program_id(2)

            @pl.when(kk == 0)
            def _():
                acc_ref[...] = p

            @pl.when(kk > 0)
            def _():
                acc_ref[...] += p

            @pl.when(kk == nk - 1)
            def _():
                finish(acc_ref[...])

    a_spec = pl.BlockSpec((tk, tm), lambda i, j, kk: (kk, i)) if ta else pl.BlockSpec((tm, tk), lambda i, j, kk: (i, kk))
    if tb:
        b_spec = pl.BlockSpec((tn, tk), lambda i, j, kk: (j + n0, kk + k0))
    else:
        b_spec = pl.BlockSpec((tk, tn), lambda i, j, kk: (kk + k0, j + n0))
    in_specs = [a_spec, b_spec] + ([pl.BlockSpec((tm, tn), lambda i, j, kk: (i, j))] if has_add else [])
    args = (a, b) + ((add,) if has_add else ())
    if out_into is not None:
        in_specs.append(pl.BlockSpec(memory_space=pl.ANY))
        args += (out_into,)
    return pl.pallas_call(
        body, name=name, grid=(m // tm, n // tn, nk), in_specs=in_specs,
        out_specs=pl.BlockSpec((tm, tn), lambda i, j, kk: (i, j + o0)),
        out_shape=jax.ShapeDtypeStruct((m, n_out), out_dtype),
        scratch_shapes=[pltpu.VMEM((tm, tn), F32)] if nk > 1 else [],
        input_output_aliases={n_in - 1: 0} if out_into is not None else {},
        compiler_params=_params("parallel", "parallel", "arbitrary"),
    )(*args)


def _rms_fwd(x, g, name):
    s, d = x.shape
    tr = _tile(s, (512, 256, 128))

    def body(x_ref, g_ref, h_ref):
        xv = x_ref[...]
        r = lax.rsqrt(jnp.mean(xv * xv, axis=-1, keepdims=True) + EPS)
        h_ref[...] = ((xv * r) * g_ref[...]).astype(BF16)

    return pl.pallas_call(
        body, name=name, grid=(s // tr,),
        in_specs=[pl.BlockSpec((tr, d), lambda i: (i, 0)), pl.BlockSpec((1, d), lambda i: (0, 0))],
        out_specs=pl.BlockSpec((tr, d), lambda i: (i, 0)),
        out_shape=jax.ShapeDtypeStruct((s, d), BF16), compiler_params=_params("parallel"),
    )(x, g.reshape(1, d))


def _rms_bwd(x, g, dh, dres, name):
    s, d = x.shape
    tr = _tile(s, (512, 256, 128))

    def body(x_ref, g_ref, dh_ref, dres_ref, dx_ref, dg_ref):
        xv = x_ref[...]
        r = lax.rsqrt(jnp.mean(xv * xv, axis=-1, keepdims=True) + EPS)
        dhv = dh_ref[...]
        u = dhv * g_ref[...]
        dx_ref[...] = dres_ref[...] + (r * u - xv * (r * r * r) * jnp.mean(xv * u, axis=-1, keepdims=True))
        part = jnp.sum(dhv * (xv * r), axis=0, keepdims=True)

        @pl.when(pl.program_id(0) == 0)
        def _():
            dg_ref[...] = part

        @pl.when(pl.program_id(0) > 0)
        def _():
            dg_ref[...] += part

    row = pl.BlockSpec((tr, d), lambda i: (i, 0))
    vec = pl.BlockSpec((1, d), lambda i: (0, 0))
    return pl.pallas_call(
        body, name=name, grid=(s // tr,), in_specs=[row, vec, row, row], out_specs=[row, vec],
        out_shape=[jax.ShapeDtypeStruct((s, d), F32), jax.ShapeDtypeStruct((1, d), F32)],
        compiler_params=_params("arbitrary"),
    )(x, g.reshape(1, d), dh, dres)


def _loss_head(y, target):
    s, d = y.shape
    tr = _tile(s, (512, 256, 128))

    def body(y_ref, t_ref, dy_ref, l_ref):
        err = y_ref[...] - t_ref[...]
        dy_ref[...] = err * (1.0 / d)
        part = jnp.sum(jnp.sum(err * err, axis=-1, keepdims=True) * (1.0 / d), axis=0, keepdims=True) * 0.5
        part = jnp.broadcast_to(part, (8, LANES))

        @pl.when(pl.program_id(0) == 0)
        def _():
            l_ref[...] = part

        @pl.when(pl.program_id(0) > 0)
        def _():
            l_ref[...] += part

    row = pl.BlockSpec((tr, d), lambda i: (i, 0))
    acc = pl.BlockSpec((8, LANES), lambda i: (0, 0))
    return pl.pallas_call(
        body, name="loss_head", grid=(s // tr,), in_specs=[row, row], out_specs=[row, acc],
        out_shape=[jax.ShapeDtypeStruct((s, d), F32), jax.ShapeDtypeStruct((8, LANES), F32)],
        compiler_params=_params("arbitrary"),
    )(y, target)


def _pool_lane_select(lane, v2, v4, v8, v16):
    return jnp.where(lane < 64, v2, jnp.where(lane < 128, v4, jnp.where(lane < 192, v8, v16)))


def _pool_diff(xc, xp, row0):
    n = BLOCK + HALO_POOL
    cat = jnp.concatenate([xp, xc], axis=0)
    s2 = cat + pltpu.roll(cat, 1, 0)
    s4 = s2 + pltpu.roll(s2, 2, 0)
    s8 = s4 + pltpu.roll(s4, 4, 0)
    s16 = s8 + pltpu.roll(s8, 8, 0)
    lane = lax.broadcasted_iota(jnp.int32, (n, POOL_W), 1)
    wsum = _pool_lane_select(lane, s2, s4, s8, s16)[HALO_POOL:]
    return wsum / _pool_count(row0, BLOCK) - xc


def _pool_count(row0, rows):
    lane = lax.broadcasted_iota(jnp.int32, (rows, POOL_W), 1)
    t = lax.broadcasted_iota(jnp.int32, (rows, POOL_W), 0) + row0
    return jnp.minimum(t + 1, _pool_lane_select(lane, 2, 4, 8, 16)).astype(F32)


def _pool_fwd(z, bdw, scale, name):
    s = z.shape[0]
    nb = s // BLOCK

    def body(xc_ref, xp_ref, w_ref, sc_ref, a_ref):
        i = pl.program_id(0)
        xp = jnp.where(i > 0, xp_ref[...], 0.0)
        diff = _pool_diff(xc_ref[...], xp, i * BLOCK)
        a_ref[...] = (_dot(diff, w_ref[...]) * sc_ref[...]).astype(BF16)

    return pl.pallas_call(
        body, name=name, grid=(nb,),
        in_specs=[pl.BlockSpec((BLOCK, POOL_W), lambda i: (i, 0)),
                  pl.BlockSpec((HALO_POOL, POOL_W), lambda i: (jnp.maximum(i * (BLOCK // HALO_POOL) - 1, 0), 0)),
                  pl.BlockSpec((POOL_W, POOL_W), lambda i: (0, 0)),
                  pl.BlockSpec((1, POOL_W), lambda i: (0, 0))],
        out_specs=pl.BlockSpec((BLOCK, POOL_W), lambda i: (i, 0)),
        out_shape=jax.ShapeDtypeStruct((s, POOL_W), BF16), compiler_params=_params("parallel"),
    )(z, z, bdw, scale.reshape(1, POOL_W))


def _pool_bwd(z, da, bdw, scale, name):
    s = z.shape[0]
    nb = s // BLOCK
    per = BLOCK // HALO_POOL
    n = BLOCK + HALO_POOL

    def body(xc_ref, xp_ref, dac_ref, dan_ref, w_ref, sc_ref, dx_ref, dw_ref, dsc_ref):
        i = pl.program_id(0)
        xp = jnp.where(i > 0, xp_ref[...], 0.0)
        diff = _pool_diff(xc_ref[...], xp, i * BLOCK)
        mixed = _dot(diff, w_ref[...])
        dac = dac_ref[...]
        dan = jnp.where(i < nb - 1, dan_ref[...], 0.0)
        dmix = jnp.concatenate([dac, dan], axis=0) * sc_ref[...]
        ddiff = _dot(dmix, w_ref[...], 1, 1)
        e = ddiff / _pool_count(i * BLOCK, n)
        f2 = e + pltpu.roll(e, n - 1, 0)
        f4 = f2 + pltpu.roll(f2, n - 2, 0)
        f8 = f4 + pltpu.roll(f4, n - 4, 0)
        f16 = f8 + pltpu.roll(f8, n - 8, 0)
        lane = lax.broadcasted_iota(jnp.int32, (n, POOL_W), 1)
        back = _pool_lane_select(lane, f2, f4, f8, f16)
        dx_ref[...] = (back[:BLOCK] - ddiff[:BLOCK]).astype(BF16)
        dw = _dot(diff, dmix[:BLOCK], 0, 0)
        dsc = jnp.sum(dac * mixed, axis=0, keepdims=True)

        @pl.when(i == 0)
        def _():
            dw_ref[...] = dw
            dsc_ref[...] = dsc

        @pl.when(i > 0)
        def _():
            dw_ref[...] += dw
            dsc_ref[...] += dsc

    blk = pl.BlockSpec((BLOCK, POOL_W), lambda i: (i, 0))
    return pl.pallas_call(
        body, name=name, grid=(nb,),
        in_specs=[blk, pl.BlockSpec((HALO_POOL, POOL_W), lambda i: (jnp.maximum(i * per - 1, 0), 0)),
                  blk, pl.BlockSpec((HALO_POOL, POOL_W), lambda i: (jnp.minimum((i + 1) * per, nb * per - 1), 0)),
                  pl.BlockSpec((POOL_W, POOL_W), lambda i: (0, 0)), pl.BlockSpec((1, POOL_W), lambda i: (0, 0))],
        out_specs=[blk, pl.BlockSpec((POOL_W, POOL_W), lambda i: (0, 0)), pl.BlockSpec((1, POOL_W), lambda i: (0, 0))],
        out_shape=[jax.ShapeDtypeStruct((s, POOL_W), BF16), jax.ShapeDtypeStruct((POOL_W, POOL_W), F32),
                   jax.ShapeDtypeStruct((1, POOL_W), F32)],
        compiler_params=_params("arbitrary"),
    )(z, z, da, da, bdw, scale.reshape(1, POOL_W))


def _attn_setup(zc_ref, zp_ref, cc_ref, cp_ref, sc_ref, sp_ref, qn_ref, kn_ref, bd):
    q = []
    for j in range(ATTN_W // LANES):
        t = zc_ref[:, POOL_W + j * LANES:POOL_W + (j + 1) * LANES]
        q.append((t,) + _norm_rope(t, qn_ref[...], cc_ref[...], sc_ref[...], bd))
    kc_raw = zc_ref[:, POOL_W + ATTN_W:POOL_W + ATTN_W + KV_W]
    kc = _norm_rope(kc_raw, kn_ref[...], cc_ref[...], sc_ref[...], bd)[0]
    kp = _norm_rope(zp_ref[:, :KV_W], kn_ref[...], cp_ref[...], sp_ref[...], bd)[0]
    kband = jnp.concatenate([kp, kc], axis=0).astype(BF16)
    vband = jnp.concatenate([zp_ref[:, KV_W:], zc_ref[:, POOL_W + ATTN_W + KV_W:POOL_W + ATTN_W + 2 * KV_W]], axis=0).astype(BF16)
    return q, kband, vband


def _attn_mask(i):
    row = lax.broadcasted_iota(jnp.int32, (BLOCK, 2 * BLOCK), 0)
    col = lax.broadcasted_iota(jnp.int32, (BLOCK, 2 * BLOCK), 1)
    dist = row + BLOCK - col
    return (dist >= 0) & (dist < BLOCK) & ((col >= BLOCK) | (i > 0))


def _to_kv_lanes(t, h):
    kvh = h // Q_PER_KV
    if (h % 2) != kvh:
        t = pltpu.roll(t, HEAD_DIM, 1)
    lane = lax.broadcasted_iota(jnp.int32, t.shape, 1)
    return jnp.where((lane >= HEAD_DIM) == (kvh == 1), t, 0.0)


def _from_kv_lanes(t, h):
    kvh = h // Q_PER_KV
    lane = lax.broadcasted_iota(jnp.int32, t.shape, 1)
    t = jnp.where((lane >= HEAD_DIM) == (kvh == 1), t, 0.0)
    if (h % 2) != kvh:
        t = pltpu.roll(t, HEAD_DIM, 1)
    return t


def _attn_probs(qh, kband, mask, sink):
    sc = _dot(qh, kband, 1, 1) * (HEAD_DIM ** -0.5)
    sc = jnp.where(mask, sc, -1e30)
    m = jnp.maximum(jnp.max(sc, axis=1, keepdims=True), sink)
    p = jnp.exp(sc - m)
    psink = jnp.exp(sink - m)
    den = jnp.sum(p, axis=1, keepdims=True) + psink
    return p / den, psink / den


def _attn_specs(nb):
    cur = lambda i: (i, 0)
    prev = lambda i: (jnp.maximum(i - 1, 0), 0)
    tab = lambda f: pl.BlockSpec((BLOCK, LANES), f)
    vec = pl.BlockSpec((1, LANES), lambda i: (0, 0))
    return [pl.BlockSpec((BLOCK, 1024), cur),
            pl.BlockSpec((BLOCK, 2 * KV_W), lambda i: (jnp.maximum(i - 1, 0), 3)),
            tab(cur), tab(prev), tab(cur), tab(prev), vec, vec,
            pl.BlockSpec(memory_space=pltpu.SMEM)]


def _attn_fwd(z, cosf, sinf, qn, kn, sinks, name):
    s = z.shape[0]
    nb = s // BLOCK

    def body(zc_ref, zp_ref, cc_ref, cp_ref, sc_ref, sp_ref, qn_ref, kn_ref, sink_ref, o_ref):
        i = pl.program_id(0)
        bd = _head_mean_matrix()
        q, kband, vband = _attn_setup(zc_ref, zp_ref, cc_ref, cp_ref, sc_ref, sp_ref, qn_ref, kn_ref, bd)
        mask = _attn_mask(i)
        out = [None] * (ATTN_W // LANES)
        for h in range(N_Q_HEADS):
            qh = _to_kv_lanes(q[h // 2][1], h)
            probs, _ = _attn_probs(qh, kband, mask, sink_ref[h])
            o = _from_kv_lanes(_dot(probs, vband), h)
            out[h // 2] = o if out[h // 2] is None else out[h // 2] + o
        for j, o in enumerate(out):
            o_ref[:, j * LANES:(j + 1) * LANES] = o.astype(BF16)

    return pl.pallas_call(
        body, name=name, grid=(nb,), in_specs=_attn_specs(nb),
        out_specs=pl.BlockSpec((BLOCK, ATTN_W), lambda i: (i, 0)),
        out_shape=jax.ShapeDtypeStruct((s, ATTN_W), BF16), compiler_params=_params("parallel"),
    )(z, z, cosf, cosf, sinf, sinf, qn, kn, sinks)


def _attn_bwd(z, cosf, sinf, qn, kn, sinks, d_out, name):
    s = z.shape[0]
    nb = s // BLOCK
    nt = ATTN_W // LANES

    def body(zc_ref, zp_ref, cc_ref, cp_ref, sc_ref, sp_ref, qn_ref, kn_ref, sink_ref, do_ref,
             dq_ref, dkc_ref, dkp_ref, dvc_ref, dvp_ref, dqn_ref, dsink_ref):
        i = pl.program_id(0)
        bd = _head_mean_matrix()
        q, kband, vband = _attn_setup(zc_ref, zp_ref, cc_ref, cp_ref, sc_ref, sp_ref, qn_ref, kn_ref, bd)
        mask = _attn_mask(i)

        @pl.when(i == 0)
        def _():
            dqn_ref[...] = jnp.zeros_like(dqn_ref)
            dsink_ref[...] = jnp.zeros_like(dsink_ref)

        dq = [None] * nt
        dk = jnp.zeros((2 * BLOCK, KV_W), F32)
        dv = jnp.zeros((2 * BLOCK, KV_W), F32)
        for h in range(N_Q_HEADS):
            qh = _to_kv_lanes(q[h // 2][1], h)
            probs, psink = _attn_probs(qh, kband, mask, sink_ref[h])
            doh = _to_kv_lanes(do_ref[:, (h // 2) * LANES:(h // 2 + 1) * LANES], h)
            dp = _dot(doh, vband, 1, 1)
            delta = jnp.sum(dp * probs, axis=1, keepdims=True)
            ds = (probs * (dp - delta)) * (HEAD_DIM ** -0.5)
            dsink = jnp.sum(-psink * delta, axis=0, keepdims=True)
            dsink_ref[h:h + 1, :] += jnp.broadcast_to(dsink, (1, LANES))
            t = _from_kv_lanes(_dot(ds, kband), h)
            dq[h // 2] = t if dq[h // 2] is None else dq[h // 2] + t
            dk = dk + _dot(ds, qh, 0, 0)
            dv = dv + _dot(probs, doh, 0, 0)
        dgn = jnp.zeros((1, LANES), F32)
        for j in range(nt):
            t, _, n, r = q[j]
            dt, g = _norm_rope_bwd(dq[j], t, n, r, qn_ref[...], cc_ref[...], sc_ref[...], bd)
            dq_ref[:, j * LANES:(j + 1) * LANES] = dt.astype(BF16)
            dgn = dgn + g
        dqn_ref[...] += jnp.broadcast_to(dgn, (8, LANES))
        dkp_ref[...] = dk[:BLOCK]
        dkc_ref[...] = dk[BLOCK:]
        dvp_ref[...] = dv[:BLOCK]
        dvc_ref[...] = dv[BLOCK:]

        @pl.when(i == nb - 1)
        def _():
            acc = dqn_ref[...]
            dqn_ref[...] = acc + pltpu.roll(acc, HEAD_DIM, 1)

    blk = lambda w: pl.BlockSpec((BLOCK, w), lambda i: (i, 0))
    acc = pl.BlockSpec((8, LANES), lambda i: (0, 0))
    kv = jax.ShapeDtypeStruct((s, KV_W), F32)
    return pl.pallas_call(
        body, name=name, grid=(nb,), in_specs=_attn_specs(nb) + [blk(ATTN_W)],
        out_specs=[blk(ATTN_W), blk(KV_W), blk(KV_W), blk(KV_W), blk(KV_W), acc, acc],
        out_shape=[jax.ShapeDtypeStruct((s, ATTN_W), BF16), kv, kv, kv, kv,
                   jax.ShapeDtypeStruct((8, LANES), F32), jax.ShapeDtypeStruct((8, LANES), F32)],
        compiler_params=_params("arbitrary"),
    )(z, z, cosf, cosf, sinf, sinf, qn, kn, sinks, d_out)


def _kv_post(z, cosf, sinf, kn, dkc, dkp, dvc, dvp, dxp, dq, duv, dz, name):
    s = z.shape[0]
    nb = s // BLOCK

    def body(zk_ref, c_ref, s_ref, kn_ref, dkc_ref, dkp_ref, dvc_ref, dvp_ref, dxp_ref, dq_ref, duv_ref, dz_in,
             dz_ref, dkn_ref):
        j = pl.program_id(0)
        bd = _head_mean_matrix()
        last = j == nb - 1
        d = dkc_ref[...] + jnp.where(last, 0.0, dkp_ref[...])
        t = zk_ref[:, :KV_W]
        _, n, r = _norm_rope(t, kn_ref[...], c_ref[...], s_ref[...], bd)
        dt, g = _norm_rope_bwd(d, t, n, r, kn_ref[...], c_ref[...], s_ref[...], bd)
        dvv = dvc_ref[...] + jnp.where(last, 0.0, dvp_ref[...])
        dz_ref[:, 0:POOL_W] = dxp_ref[...]
        dz_ref[:, POOL_W:POOL_W + ATTN_W] = dq_ref[...]
        dz_ref[:, POOL_W + ATTN_W:POOL_W + ATTN_W + KV_W] = dt.astype(BF16)
        dz_ref[:, POOL_W + ATTN_W + KV_W:POOL_W + ATTN_W + 2 * KV_W] = dvv.astype(BF16)
        dz_ref[:, POOL_W + ATTN_W + 2 * KV_W:GATE_COL0] = duv_ref[...]

        @pl.when(j == 0)
        def _():
            dkn_ref[...] = jnp.zeros_like(dkn_ref)

        dkn_ref[...] += jnp.broadcast_to(g, (8, LANES))

        @pl.when(last)
        def _():
            acc = dkn_ref[...]
            dkn_ref[...] = acc + pltpu.roll(acc, HEAD_DIM, 1)

    cur = lambda w: pl.BlockSpec((BLOCK, w), lambda j: (j, 0))
    nxt = pl.BlockSpec((BLOCK, KV_W), lambda j: (jnp.minimum(j + 1, nb - 1), 0))
    vec = pl.BlockSpec((1, LANES), lambda j: (0, 0))
    return pl.pallas_call(
        body, name=name, grid=(nb,),
        in_specs=[pl.BlockSpec((BLOCK, 2 * KV_W), lambda j: (j, 3)), cur(LANES), cur(LANES), vec,
                  cur(KV_W), nxt, cur(KV_W), nxt, cur(POOL_W), cur(ATTN_W), cur(2 * SGU_W),
                  pl.BlockSpec(memory_space=pl.ANY)],
        out_specs=[pl.BlockSpec((BLOCK, GATE_COL0), lambda j: (j, 0)), pl.BlockSpec((8, LANES), lambda j: (0, 0))],
        out_shape=[jax.ShapeDtypeStruct(dz.shape, dz.dtype), jax.ShapeDtypeStruct((8, LANES), F32)],
        input_output_aliases={11: 0}, compiler_params=_params("arbitrary"),
    )(z, cosf, sinf, kn, dkc, dkp, dvc, dvp, dxp, dq, duv, dz)


def _sgu_setup(z_ref, ws_ref, vn_ref, bd):
    us = z_ref[:, :SGU_W]
    vs = z_ref[:, SGU_W:]
    ug, tu = _gelu(us)
    gv, tv = _gelu(vs)
    rr = jnp.concatenate([lax.rsqrt(_head_mean(gv[:, k * LANES:(k + 1) * LANES] ** 2, bd) + EPS) for k in range(2)], axis=1)
    vg = (gv * rr) * vn_ref[...]
    tril = lax.broadcasted_iota(jnp.int32, (BLOCK, BLOCK), 0) >= lax.broadcasted_iota(jnp.int32, (BLOCK, BLOCK), 1)
    w = [jnp.where(tril, ws_ref[g], 0.0).astype(BF16) for g in range(4)]
    return us, vs, ug, tu, gv, tv, rr, vg, w, tril


def _group_select(parts):
    lane = lax.broadcasted_iota(jnp.int32, parts[0].shape, 1)
    return _pool_lane_select(lane, *parts)


def _sgu_fwd(z, ws, bcol, vn, name):
    s = z.shape[0]
    nb = s // BLOCK

    def body(z_ref, ws_ref, b_ref, vn_ref, c_ref):
        bd = _head_mean_matrix()
        _, _, ug, _, _, _, _, vg, w, _ = _sgu_setup(z_ref, ws_ref, vn_ref, bd)
        sg = _group_select([_dot(w[g], vg) for g in range(4)]) + b_ref[...]
        c_ref[...] = (ug * sg).astype(BF16)

    return pl.pallas_call(
        body, name=name, grid=(nb,),
        in_specs=[pl.BlockSpec((BLOCK, 2 * SGU_W), lambda i: (i, 2)), pl.BlockSpec((4, BLOCK, BLOCK), lambda i: (0, 0, 0)),
                  pl.BlockSpec((BLOCK, SGU_W), lambda i: (0, 0)), pl.BlockSpec((1, SGU_W), lambda i: (0, 0))],
        out_specs=pl.BlockSpec((BLOCK, SGU_W), lambda i: (i, 0)),
        out_shape=jax.ShapeDtypeStruct((s, SGU_W), BF16), compiler_params=_params("parallel"),
    )(z, ws, bcol, vn)


def _sgu_bwd(z, ws, bcol, vn, dc, name):
    s = z.shape[0]
    nb = s // BLOCK

    def body(z_ref, ws_ref, b_ref, vn_ref, dc_ref, duv_ref, dws_ref, db_ref, dvn_ref):
        i = pl.program_id(0)
        bd = _head_mean_matrix()
        us, vs, ug, tu, gv, tv, rr, vg, w, tril = _sgu_setup(z_ref, ws_ref, vn_ref, bd)
        sg = _group_select([_dot(w[g], vg) for g in range(4)]) + b_ref[...]
        dcv = dc_ref[...]
        dug = dcv * sg
        dsg = dcv * ug
        lane = lax.broadcasted_iota(jnp.int32, dsg.shape, 1)

        @pl.when(i == 0)
        def _():
            dws_ref[...] = jnp.zeros_like(dws_ref)
            db_ref[...] = jnp.zeros_like(db_ref)
            dvn_ref[...] = jnp.zeros_like(dvn_ref)

        for g in range(4):
            dsg_g = jnp.where((lane >= g * HEAD_DIM) & (lane < (g + 1) * HEAD_DIM), dsg, 0.0)
            dws_ref[g] += jnp.where(tril, _dot(dsg_g, vg, 1, 1), 0.0)
        dvg = _group_select([_dot(w[g], dsg, 0, 0) for g in range(4)])
        db_ref[...] += dsg
        n = gv * rr
        part = jnp.sum(dvg * n, axis=0, keepdims=True)
        dvn_ref[...] += jnp.broadcast_to(part[:, :LANES] + part[:, LANES:], (8, LANES))
        u = dvg * vn_ref[...]
        tu_ = gv * u
        mean = jnp.concatenate([_head_mean(tu_[:, k * LANES:(k + 1) * LANES], bd) for k in range(2)], axis=1)
        dgv = rr * u - gv * (rr * rr * rr) * mean
        duv_ref[:, :SGU_W] = (dug * _gelu_grad(us, tu)).astype(BF16)
        duv_ref[:, SGU_W:] = (dgv * _gelu_grad(vs, tv)).astype(BF16)

        @pl.when(i == nb - 1)
        def _():
            acc = dvn_ref[...]
            dvn_ref[...] = acc + pltpu.roll(acc, HEAD_DIM, 1)
            for k in range(2):
                db_ref[:, k * LANES:(k + 1) * LANES] = _head_mean(db_ref[:, k * LANES:(k + 1) * LANES], bd) * float(HEAD_DIM)

    return pl.pallas_call(
        body, name=name, grid=(nb,),
        in_specs=[pl.BlockSpec((BLOCK, 2 * SGU_W), lambda i: (i, 2)), pl.BlockSpec((4, BLOCK, BLOCK), lambda i: (0, 0, 0)),
                  pl.BlockSpec((BLOCK, SGU_W), lambda i: (0, 0)), pl.BlockSpec((1, SGU_W), lambda i: (0, 0)),
                  pl.BlockSpec((BLOCK, SGU_W), lambda i: (i, 0))],
        out_specs=[pl.BlockSpec((BLOCK, 2 * SGU_W), lambda i: (i, 0)), pl.BlockSpec((4, BLOCK, BLOCK), lambda i: (0, 0, 0)),
                   pl.BlockSpec((BLOCK, SGU_W), lambda i: (0, 0)), pl.BlockSpec((8, LANES), lambda i: (0, 0))],
        out_shape=[jax.ShapeDtypeStruct((s, 2 * SGU_W), BF16), jax.ShapeDtypeStruct((4, BLOCK, BLOCK), F32),
                   jax.ShapeDtypeStruct((BLOCK, SGU_W), F32), jax.ShapeDtypeStruct((8, LANES), F32)],
        compiler_params=_params("arbitrary"),
    )(z, ws, bcol, vn, dc)


MERGE_TN = 512
MERGE_TM = 256


def _merge_fwd(a, b, c, wpa, wpb, wpc, z, name):
    s = z.shape[0]
    tm = _tile(s, (MERGE_TM, BLOCK))
    gate0 = GATE_COL0 // MERGE_TN

    def body(a_ref, b_ref, c_ref, wa_ref, wb_ref, wc_ref, g0_ref, g1_ref, g2_ref, o_ref):
        r = _sigmoid(g0_ref[...]) * _dot(a_ref[...], wa_ref[...])
        r = r + _sigmoid(g1_ref[...]) * _dot(b_ref[...], wb_ref[...])
        r = r + _sigmoid(g2_ref[...]) * _dot(c_ref[...], wc_ref[...])
        o_ref[...] = r.astype(BF16)

    x_spec = lambda w: pl.BlockSpec((tm, w), lambda i, n: (i, 0))
    w_spec = lambda w: pl.BlockSpec((w, MERGE_TN), lambda i, n: (0, n))
    g_spec = lambda br: pl.BlockSpec((tm, MERGE_TN), lambda i, n: (i, gate0 + 2 * br + n))
    return pl.pallas_call(
        body, name=name, grid=(s // tm, D_MODEL // MERGE_TN),
        in_specs=[x_spec(POOL_W), x_spec(ATTN_W), x_spec(SGU_W), w_spec(POOL_W), w_spec(ATTN_W), w_spec(SGU_W),
                  g_spec(0), g_spec(1), g_spec(2)],
        out_specs=pl.BlockSpec((tm, MERGE_TN), lambda i, n: (i, n)),
        out_shape=jax.ShapeDtypeStruct((s, D_MODEL), BF16), compiler_params=_params("parallel", "parallel"),
    )(a, b, c, wpa, wpb, wpc, z, z, z)


def _branch_bwd(br, xb, wp, z, dm, dz, name):
    s = z.shape[0]
    kb = xb.shape[1]
    tm = _tile(s, (MERGE_TM, BLOCK))
    gate0 = GATE_COL0 // MERGE_TN
    aliased = dz is not None

    def body(*refs):
        x_ref, w_ref, g_ref, dm_ref = refs[:4]
        dz_ref, dy_ref, dx_ref = refs[-3:]
        n = pl.program_id(1)
        y = _dot(x_ref[...], w_ref[...])
        sg = _sigmoid(g_ref[...])
        dmv = dm_ref[...]
        dy = (dmv * sg).astype(BF16)
        dy_ref[...] = dy
        dz_ref[...] = ((dmv * y) * (sg * (1.0 - sg))).astype(BF16)
        dx = _dot(dy, w_ref[...], 1, 1)

        @pl.when(n == 0)
        def _():
            dx_ref[...] = dx

        @pl.when(n > 0)
        def _():
            dx_ref[...] += dx

    in_specs = [pl.BlockSpec((tm, kb), lambda i, n: (i, 0)), pl.BlockSpec((kb, MERGE_TN), lambda i, n: (0, n)),
                pl.BlockSpec((tm, MERGE_TN), lambda i, n: (i, gate0 + 2 * br + n)),
                pl.BlockSpec((tm, MERGE_TN), lambda i, n: (i, n))]
    args = [xb, wp, z, dm]
    if aliased:
        in_specs.append(pl.BlockSpec(memory_space=pl.ANY))
        args.append(dz)
    return pl.pallas_call(
        body, name=name, grid=(s // tm, D_MODEL // MERGE_TN), in_specs=in_specs,
        out_specs=[pl.BlockSpec((tm, MERGE_TN), lambda i, n: (i, gate0 + 2 * br + n)),
                   pl.BlockSpec((tm, MERGE_TN), lambda i, n: (i, n)),
                   pl.BlockSpec((tm, kb), lambda i, n: (i, 0))],
        out_shape=[jax.ShapeDtypeStruct((s, IN_COLS), BF16), jax.ShapeDtypeStruct((s, D_MODEL), BF16),
                   jax.ShapeDtypeStruct((s, kb), F32)],
        input_output_aliases={4: 0} if aliased else {},
        compiler_params=_params("parallel", "arbitrary"),
    )(*args)


FFN_TM = 256
FFN_TC = 1408


def _conv3(cur, prev, w_ref, b_ref):
    cat = jnp.concatenate([prev, cur], axis=0)
    x1 = pltpu.roll(cat, 1, 0)[HALO_CONV:]
    x2 = pltpu.roll(cat, 2, 0)[HALO_CONV:]
    return w_ref[0:1, :] * x2 + w_ref[1:2, :] * x1 + w_ref[2:3, :] * cur + b_ref[...], x1, x2


def _ffn_specs(s, tm, rows_first):
    per = tm // HALO_CONV
    if rows_first:
        cur = pl.BlockSpec((tm, FFN_TC), lambda i, j: (i, j))
        prev = pl.BlockSpec((HALO_CONV, FFN_TC), lambda i, j: (jnp.maximum(i * per - 1, 0), j))
        w = pl.BlockSpec((3, FFN_TC), lambda i, j: (0, j))
        b = pl.BlockSpec((1, FFN_TC), lambda i, j: (0, j))
    else:
        cur = pl.BlockSpec((tm, FFN_TC), lambda j, i: (i, j))
        prev = pl.BlockSpec((HALO_CONV, FFN_TC), lambda j, i: (jnp.maximum(i * per - 1, 0), j))
        w = pl.BlockSpec((3, FFN_TC), lambda j, i: (0, j))
        b = pl.BlockSpec((1, FFN_TC), lambda j, i: (0, j))
    return cur, prev, w, b


def _ffn_act_fwd(upg, upv, cwg, cwv, cbg, cbv, name):
    s = upg.shape[0]
    tm = _tile(s, (FFN_TM, BLOCK))
    cur, prev, w, b = _ffn_specs(s, tm, True)

    def body(g_ref, gp_ref, v_ref, vp_ref, wg_ref, wv_ref, bg_ref, bv_ref, o_ref):
        first = pl.program_id(0) == 0
        gate = _conv3(g_ref[...], jnp.where(first, 0.0, gp_ref[...]), wg_ref, bg_ref)[0]
        val = _conv3(v_ref[...], jnp.where(first, 0.0, vp_ref[...]), wv_ref, bv_ref)[0]
        o_ref[...] = ((gate * _sigmoid(gate)) * val).astype(BF16)

    return pl.pallas_call(
        body, name=name, grid=(s // tm, D_FF // FFN_TC), in_specs=[cur, prev, cur, prev, w, w, b, b], out_specs=cur,
        out_shape=jax.ShapeDtypeStruct((s, D_FF), BF16), compiler_params=_params("parallel", "parallel"),
    )(upg, upg, upv, upv, cwg, cwv, cbg, cbv)


def _ffn_act_bwd(upg, upv, cwg, cwv, cbg, cbv, dact, name):
    s = upg.shape[0]
    tm = _tile(s, (FFN_TM, BLOCK))
    cur, prev, w, b = _ffn_specs(s, tm, False)

    def body(g_ref, gp_ref, v_ref, vp_ref, wg_ref, wv_ref, bg_ref, bv_ref, da_ref, dg_ref, dv_ref, dwg_ref, dwv_ref):
        i = pl.program_id(1)
        first = i == 0
        gc, vc = g_ref[...], v_ref[...]
        gate, g1, g2 = _conv3(gc, jnp.where(first, 0.0, gp_ref[...]), wg_ref, bg_ref)
        val, v1, v2 = _conv3(vc, jnp.where(first, 0.0, vp_ref[...]), wv_ref, bv_ref)
        sg = _sigmoid(gate)
        da = da_ref[...]
        dgate = (da * val) * (sg * (1.0 + gate * (1.0 - sg)))
        dval = da * (gate * sg)
        dg_ref[...] = dgate
        dv_ref[...] = dval

        @pl.when(first)
        def _():
            dwg_ref[...] = jnp.zeros_like(dwg_ref)
            dwv_ref[...] = jnp.zeros_like(dwv_ref)

        for ref, d, x0, x1, x2 in ((dwg_ref, dgate, gc, g1, g2), (dwv_ref, dval, vc, v1, v2)):
            ref[0:1, :] += jnp.sum(d * x2, axis=0, keepdims=True)
            ref[1:2, :] += jnp.sum(d * x1, axis=0, keepdims=True)
            ref[2:3, :] += jnp.sum(d * x0, axis=0, keepdims=True)
            ref[3:4, :] += jnp.sum(d, axis=0, keepdims=True)

    acc = pl.BlockSpec((8, FFN_TC), lambda j, i: (0, j))
    full = jax.ShapeDtypeStruct((s, D_FF), F32)
    accs = jax.ShapeDtypeStruct((8, D_FF), F32)
    return pl.pallas_call(
        body, name=name, grid=(D_FF // FFN_TC, s // tm), in_specs=[cur, prev, cur, prev, w, w, b, b, cur],
        out_specs=[cur, cur, acc, acc], out_shape=[full, full, accs, accs],
        compiler_params=_params("parallel", "arbitrary"),
    )(upg, upg, upv, upv, cwg, cwv, cbg, cbv, dact)


def _conv3_t(d, cw, name):
    s = d.shape[0]
    tm = _tile(s, (FFN_TM, BLOCK))
    per = tm // HALO_CONV
    nrow = s // tm
    n = tm + HALO_CONV

    def body(d_ref, dn_ref, w_ref, o_ref):
        nxt = jnp.where(pl.program_id(0) == nrow - 1, 0.0, dn_ref[...])
        cur = d_ref[...]
        cat = jnp.concatenate([cur, nxt], axis=0)
        d1 = pltpu.roll(cat, n - 1, 0)[:tm]
        d2 = pltpu.roll(cat, n - 2, 0)[:tm]
        o_ref[...] = (w_ref[2:3, :] * cur + w_ref[1:2, :] * d1 + w_ref[0:1, :] * d2).astype(BF16)

    cur = pl.BlockSpec((tm, FFN_TC), lambda i, j: (i, j))
    return pl.pallas_call(
        body, name=name, grid=(nrow, D_FF // FFN_TC),
        in_specs=[cur, pl.BlockSpec((HALO_CONV, FFN_TC), lambda i, j: (jnp.minimum((i + 1) * per, nrow * per - 1), j)),
                  pl.BlockSpec((3, FFN_TC), lambda i, j: (0, j))],
        out_specs=cur, out_shape=jax.ShapeDtypeStruct((s, D_FF), BF16), compiler_params=_params("parallel", "parallel"),
    )(d, d, cw)


def _mesh_place():
    return lax.axis_index("x"), lax.axis_index("y"), lax.axis_index("c")


def _all_gather(shards, name):
    na = len(shards)

    def body(*refs):
        x_refs, out_refs = refs[:na], refs[na:2 * na]
        send_sems, recv_sems, local_sems = refs[2 * na:]
        x, y, cc = _mesh_place()
        me, sibling = (x, y, cc), (x, y, 1 - cc)
        chips = [(1 - x, y), (x, 1 - y), (1 - x, 1 - y)]

        def copy(k, a, block, to, from_input=False):
            slot = out_refs[a].at[4 * block[0] + 2 * block[1] + block[2]]
            return pltpu.make_async_remote_copy(
                src_ref=x_refs[a] if from_input else slot, dst_ref=slot, send_sem=send_sems.at[k * na + a],
                recv_sem=recv_sems.at[k * na + a], device_id=to, device_id_type=pl.DeviceIdType.MESH)

        mine = [pltpu.make_async_copy(x_refs[a], out_refs[a].at[4 * x + 2 * y + cc], local_sems.at[a]) for a in range(na)]
        for cp in mine:
            cp.start()
        first = [copy(0, a, me, sibling, True) for a in range(na)]
        first += [copy(1 + j, a, me, (*chip, cc), True) for j, chip in enumerate(chips) for a in range(na)]
        for cp in first:
            cp.start()
        passed = []
        for j, chip in enumerate(chips):
            for a in range(na):
                copy(1 + j, a, (*chip, cc), me).wait_recv()
                passed.append(copy(4 + j, a, (*chip, cc), sibling))
                passed[-1].start()
        for a in range(na):
            copy(0, a, sibling, me).wait_recv()
        for j, chip in enumerate(chips):
            for a in range(na):
                copy(4 + j, a, (*chip, 1 - cc), me).wait_recv()
        for cp in first + passed:
            cp.wait_send()
        for cp in mine:
            cp.wait()

    hbm = pl.BlockSpec(memory_space=pl.ANY)
    return pl.pallas_call(
        body, name=name, out_shape=[jax.ShapeDtypeStruct((N_DEV,) + t.shape, t.dtype) for t in shards],
        in_specs=[hbm] * na, out_specs=[hbm] * na,
        scratch_shapes=[pltpu.SemaphoreType.DMA((7 * na,)), pltpu.SemaphoreType.DMA((7 * na,)), pltpu.SemaphoreType.DMA((na,))],
    )(*shards)


def _exchange(blocks, name):
    na = len(blocks)

    def body(*refs):
        g_refs, out_refs = refs[:na], refs[na:2 * na]
        send_sems, recv_sems, local_sems = refs[2 * na:]
        x, y, cc = _mesh_place()
        me = 4 * x + 2 * y + cc
        mine = [pltpu.make_async_copy(g_refs[a].at[me], out_refs[a].at[me], local_sems.at[a]) for a in range(na)]
        for cp in mine:
            cp.start()
        sends, lands = [], []
        for k in range(1, N_DEV):
            px = 1 - x if (k >> 2) & 1 else x
            py = 1 - y if (k >> 1) & 1 else y
            pc = 1 - cc if k & 1 else cc
            peer = 4 * px + 2 * py + pc
            for a in range(na):
                sem = (k - 1) * na + a
                sends.append(pltpu.make_async_remote_copy(
                    src_ref=g_refs[a].at[peer], dst_ref=out_refs[a].at[me], send_sem=send_sems.at[sem], recv_sem=recv_sems.at[sem],
                    device_id=(px, py, pc), device_id_type=pl.DeviceIdType.MESH))
                lands.append(pltpu.make_async_remote_copy(
                    src_ref=g_refs[a].at[peer], dst_ref=out_refs[a].at[peer], send_sem=send_sems.at[sem], recv_sem=recv_sems.at[sem],
                    device_id=(px, py, pc), device_id_type=pl.DeviceIdType.MESH))
        for cp in sends:
            cp.start()
        for cp in lands:
            cp.wait_recv()
        for cp in sends:
            cp.wait_send()
        for cp in mine:
            cp.wait()

    hbm = pl.BlockSpec(memory_space=pl.ANY)
    return pl.pallas_call(
        body, name=name, out_shape=[jax.ShapeDtypeStruct(t.shape, t.dtype) for t in blocks],
        in_specs=[hbm] * na, out_specs=[hbm] * na,
        scratch_shapes=[pltpu.SemaphoreType.DMA((7 * na,)), pltpu.SemaphoreType.DMA((7 * na,)), pltpu.SemaphoreType.DMA((na,))],
    )(*blocks)


def _adamw_sum(parts, w, m, v, name):
    _, r, c = parts.shape
    tr = _tile(r, (256, 128, 64, 32, 16, 8))

    def body(p_ref, w_ref, m_ref, v_ref, g_ref, d_ref, nm_ref, nv_ref):
        _adam_store(_sum_parts(p_ref), w_ref, m_ref, v_ref, g_ref, d_ref, nm_ref, nv_ref)

    row = pl.BlockSpec((tr, c), lambda i: (i, 0))
    shp = jax.ShapeDtypeStruct((r, c), F32)
    return pl.pallas_call(
        body, name=name, grid=(r // tr,), in_specs=[pl.BlockSpec((N_DEV, tr, c), lambda i: (0, i, 0)), row, row, row],
        out_specs=[row, row, row, row], out_shape=[shp, shp, shp, shp], compiler_params=_params("parallel"),
    )(parts, w, m, v)


def _sum_parts(p_ref):
    g = p_ref[0].astype(F32)
    for k in range(1, N_DEV):
        g = g + p_ref[k].astype(F32)
    return g


def _adam_store(g, w_ref, m_ref, v_ref, g_ref, d_ref, nm_ref, nv_ref):
    nm = ADAM_B1 * m_ref[...] + (1.0 - ADAM_B1) * g
    nv = ADAM_B2 * v_ref[...] + (1.0 - ADAM_B2) * (g * g)
    m_hat = nm / (1.0 - ADAM_B1 ** ADAM_STEP)
    v_hat = nv / (1.0 - ADAM_B2 ** ADAM_STEP)
    g_ref[...] = g
    nm_ref[...] = nm
    nv_ref[...] = nv
    d_ref[...] = -ADAM_LR * (m_hat / (jnp.sqrt(v_hat) + ADAM_EPS) + ADAM_WD * w_ref[...])


def _adamw_weight(parts, w, m, v, name):
    _, r, c = w.shape
    tr = _tile(r, (256, 128, 176))
    nr = r // tr

    def body(p0_ref, p1_ref, w_ref, m_ref, v_ref, g_ref, d_ref, nm_ref, nv_ref):
        g = jnp.where(pl.program_id(0) == 0, _sum_parts(p0_ref), _sum_parts(p1_ref))
        _adam_store(g, w_ref, m_ref, v_ref, g_ref, d_ref, nm_ref, nv_ref)

    part = lambda layer: pl.BlockSpec(
        (N_DEV, tr, c), lambda l, i: (0, jnp.where(l == layer, i, (nr - 1) * (1 - layer)), 0))
    row = pl.BlockSpec((None, tr, c), lambda l, i: (l, i, 0))
    shp = jax.ShapeDtypeStruct(w.shape, F32)
    return pl.pallas_call(
        body, name=name, grid=(DEPTH, nr), in_specs=[part(0), part(1), row, row, row],
        out_specs=[row, row, row, row], out_shape=[shp, shp, shp, shp], compiler_params=_params("arbitrary", "arbitrary"),
    )(parts[0], parts[1], w, m, v)


def _full_to_slots(name, t):
    k, n = t.shape
    if name in ROW_SHARDED:
        return t.reshape(N_DEV, k // N_DEV, n)
    return t.reshape(k, N_DEV, n // N_DEV).transpose(1, 0, 2)


def _slots_to_full(name, t):
    _, r, c = t.shape
    if name in ROW_SHARDED:
        return t.reshape(N_DEV * r, c)
    return t.transpose(1, 0, 2).reshape(r, N_DEV * c)


def _small_sizes(shapes):
    return [(n, shapes[n], -(-int(math.prod(shapes[n])) // (8 * LANES)) * 8) for n in SMALL]


def _pack_small(tree, shapes):
    rows = []
    for n, shp, nrow in _small_sizes(shapes):
        flat = tree[n].reshape(-1)
        rows.append(jnp.pad(flat, (0, nrow * LANES - flat.shape[0])).reshape(nrow, LANES))
    total = sum(r.shape[0] for r in rows)
    rows.append(jnp.zeros((-total % SMALL_ROW_TILE, LANES), F32))
    return jnp.concatenate(rows, axis=0)


def _unpack_small(buf, shapes):
    out, r0 = {}, 0
    for n, shp, nrow in _small_sizes(shapes):
        out[n] = buf[r0:r0 + nrow].reshape(-1)[:int(math.prod(shp))].reshape(shp)
        r0 += nrow
    return out


def _block_diag(w):
    g = w.shape[0]
    eye = jnp.eye(g, dtype=w.dtype)
    return (eye[:, None, :, None] * w[:, :, None, :]).reshape(g * HEAD_DIM, g * HEAD_DIM)


def kernel(x, positions, norm1, w_in, q_norm, k_norm, sinks, w_pool, pool_scale, sgu_v_norm, w_s, b_s, w_proj_a, w_proj_b, w_proj_c, w_out, norm2, w_up, conv_w, conv_b, w_down, loss_target, m_norm1, m_w_in, m_q_norm, m_k_norm, m_sinks, m_w_pool, m_pool_scale, m_sgu_v_norm, m_w_s, m_b_s, m_w_proj_a, m_w_proj_b, m_w_proj_c, m_w_out, m_norm2, m_w_up, m_conv_w, m_conv_b, m_w_down, v_norm1, v_w_in, v_q_norm, v_k_norm, v_sinks, v_w_pool, v_pool_scale, v_sgu_v_norm, v_w_s, v_b_s, v_w_proj_a, v_w_proj_b, v_w_proj_c, v_w_out, v_norm2, v_w_up, v_conv_w, v_conv_b, v_w_down):
    names = ("norm1", "w_in", "q_norm", "k_norm", "sinks", "w_pool", "pool_scale", "sgu_v_norm", "w_s", "b_s", "w_proj_a",
             "w_proj_b", "w_proj_c", "w_out", "norm2", "w_up", "conv_w", "conv_b", "w_down")
    wts = dict(zip(names, (norm1, w_in, q_norm, k_norm, sinks, w_pool, pool_scale, sgu_v_norm, w_s, b_s, w_proj_a, w_proj_b,
                           w_proj_c, w_out, norm2, w_up, conv_w, conv_b, w_down)))
    mom = dict(zip(names, (m_norm1, m_w_in, m_q_norm, m_k_norm, m_sinks, m_w_pool, m_pool_scale, m_sgu_v_norm, m_w_s, m_b_s,
                           m_w_proj_a, m_w_proj_b, m_w_proj_c, m_w_out, m_norm2, m_w_up, m_conv_w, m_conv_b, m_w_down)))
    var = dict(zip(names, (v_norm1, v_w_in, v_q_norm, v_k_norm, v_sinks, v_w_pool, v_pool_scale, v_sgu_v_norm, v_w_s, v_b_s,
                           v_w_proj_a, v_w_proj_b, v_w_proj_c, v_w_out, v_norm2, v_w_up, v_conv_w, v_conv_b, v_w_down)))
    xs = x[0]
    target = loss_target[0]
    s = xs.shape[0]

    inv_freq = ROPE_THETA ** (-jnp.arange(0, HEAD_DIM, 2, dtype=F32) / HEAD_DIM)
    ang = positions[0].astype(F32)[:, None] * inv_freq
    cosf = jnp.tile(jnp.cos(ang), (1, 4))
    sinf = jnp.tile(jnp.concatenate([-jnp.sin(ang), jnp.sin(ang)], axis=1), (1, 2))

    full = []
    for l in range(DEPTH):
        local = [wts[n][l] if n == "conv_w" else wts[n][l].astype(BF16) for n in SHARDED]
        gathered = _all_gather(local, f"gather_weights_{l}")
        full.append({n: _slots_to_full(n, t) for n, t in zip(SHARDED, gathered)})

    def layer_consts(l):
        return dict(
            bdw=_block_diag(w_pool[l]).astype(BF16), qn=jnp.tile(q_norm[l], 2).reshape(1, LANES),
            kn=jnp.tile(k_norm[l], 2).reshape(1, LANES), vn=jnp.tile(sgu_v_norm[l], 4).reshape(1, SGU_W),
            bcol=jnp.repeat(b_s[l].T, HEAD_DIM, axis=1),
            cwg=full[l]["conv_w"][:, :D_FF], cwv=full[l]["conv_w"][:, D_FF:],
            cbg=conv_b[l][:D_FF].reshape(1, D_FF), cbv=conv_b[l][D_FF:].reshape(1, D_FF))

    gate_cols, val_cols = (0, D_FF), (D_FF, D_FF)

    saved = []
    cur = xs
    for l in range(DEPTH):
        fw, k = full[l], layer_consts(l)
        h1 = _rms_fwd(cur, norm1[l], f"rms1_fwd_{l}")
        z = _mm(h1, fw["w_in"], name=f"in_proj_{l}")
        a = _pool_fwd(z, k["bdw"], pool_scale[l], f"pool_fwd_{l}")
        b = _attn_fwd(z, cosf, sinf, k["qn"], k["kn"], sinks[l], f"attn_fwd_{l}")
        c = _sgu_fwd(z, w_s[l], k["bcol"], k["vn"], f"sgu_fwd_{l}")
        merged = _merge_fwd(a, b, c, fw["w_proj_a"], fw["w_proj_b"], fw["w_proj_c"], z, f"merge_fwd_{l}")
        x1 = _mm(merged, fw["w_out"], add=cur, name=f"out_proj_{l}")
        h2 = _rms_fwd(x1, norm2[l], f"rms2_fwd_{l}")
        upg = _mm(h2, fw["w_up"], b_n=gate_cols, name=f"up_gate_{l}")
        upv = _mm(h2, fw["w_up"], b_n=val_cols, name=f"up_val_{l}")
        act = _ffn_act_fwd(upg, upv, k["cwg"], k["cwv"], k["cbg"], k["cbv"], f"ffn_act_fwd_{l}")
        x2 = _mm(act, fw["w_down"], add=x1, name=f"down_proj_{l}")
        saved.append(dict(x0=cur, h1=h1, z=z, a=a, b=b, c=c, merged=merged, x1=x1, h2=h2, upg=upg, upv=upv, act=act))
        cur = x2

    dcur, loss_tile = _loss_head(cur, target)
    loss = lax.psum(loss_tile[0, 0], ("x", "y", "c"))

    parts = [None] * DEPTH
    gsmall = [None] * DEPTH
    for l in reversed(range(DEPTH)):
        fw, k, sv = full[l], layer_consts(l), saved[l]
        wgrad = functools.partial(_mm, ta=True, out_dtype=BF16)
        dact = _mm(dcur, fw["w_down"], tb=True, name=f"down_proj_bwd_{l}")
        g_down = wgrad(sv["act"], dcur, name=f"down_proj_wgrad_{l}")
        dgate, dval, dcg, dcv = _ffn_act_bwd(sv["upg"], sv["upv"], k["cwg"], k["cwv"], k["cbg"], k["cbv"], dact, f"ffn_act_bwd_{l}")
        dg0 = _conv3_t(dgate, k["cwg"], f"conv_t_gate_{l}")
        dv0 = _conv3_t(dval, k["cwv"], f"conv_t_val_{l}")
        dh2 = _mm(dg0, fw["w_up"], tb=True, b_k=gate_cols, name=f"up_gate_bwd_{l}")
        dh2 = _mm(dv0, fw["w_up"], tb=True, b_k=val_cols, add=dh2, name=f"up_val_bwd_{l}")
        g_up = wgrad(sv["h2"], dg0, out_cols=(0, 2 * D_FF), name=f"up_gate_wgrad_{l}")
        g_up = wgrad(sv["h2"], dv0, out_cols=(D_FF, 2 * D_FF), out_into=g_up, name=f"up_val_wgrad_{l}")
        dx1, g_norm2 = _rms_bwd(sv["x1"], norm2[l], dh2, dcur, f"rms2_bwd_{l}")
        dmerged = _mm(dx1, fw["w_out"], tb=True, name=f"out_proj_bwd_{l}")
        g_out = wgrad(sv["merged"], dx1, name=f"out_proj_wgrad_{l}")
        dz, dya, da = _branch_bwd(0, sv["a"], fw["w_proj_a"], sv["z"], dmerged, None, f"branch_a_bwd_{l}")
        dz, dyb, db = _branch_bwd(1, sv["b"], fw["w_proj_b"], sv["z"], dmerged, dz, f"branch_b_bwd_{l}")
        dz, dyc, dc = _branch_bwd(2, sv["c"], fw["w_proj_c"], sv["z"], dmerged, dz, f"branch_c_bwd_{l}")
        g_pa = wgrad(sv["a"], dya, name=f"proj_a_wgrad_{l}")
        g_pb = wgrad(sv["b"], dyb, name=f"proj_b_wgrad_{l}")
        g_pc = wgrad(sv["c"], dyc, name=f"proj_c_wgrad_{l}")
        dxp, g_bdw, g_pscale = _pool_bwd(sv["z"], da, k["bdw"], pool_scale[l], f"pool_bwd_{l}")
        dq, dkc, dkp, dvc, dvp, g_qn, g_sink = _attn_bwd(sv["z"], cosf, sinf, k["qn"], k["kn"], sinks[l], db, f"attn_bwd_{l}")
        duv, g_ws, g_bacc, g_vn = _sgu_bwd(sv["z"], w_s[l], k["bcol"], k["vn"], dc, f"sgu_bwd_{l}")
        dz, g_kn = _kv_post(sv["z"], cosf, sinf, k["kn"], dkc, dkp, dvc, dvp, dxp, dq, duv, dz, f"kv_post_{l}")
        dh1 = _mm(dz, fw["w_in"], tb=True, name=f"in_proj_bwd_{l}")
        g_in = wgrad(sv["h1"], dz, name=f"in_proj_wgrad_{l}")
        dcur, g_norm1 = _rms_bwd(sv["x0"], norm1[l], dh1, dx1, f"rms1_bwd_{l}")
        gfull = dict(w_in=g_in, w_proj_a=g_pa, w_proj_b=g_pb, w_proj_c=g_pc, w_out=g_out, w_up=g_up, w_down=g_down,
                     conv_w=jnp.concatenate([dcg[0:3], dcv[0:3]], axis=1))
        parts[l] = _exchange([_full_to_slots(n, gfull[n]) for n in SHARDED], f"exchange_grads_{l}")
        gsmall[l] = dict(
            norm1=g_norm1[0], q_norm=g_qn[0, :HEAD_DIM], k_norm=g_kn[0, :HEAD_DIM], sinks=g_sink[:, 0],
            w_pool=jnp.stack([g_bdw[g * HEAD_DIM:(g + 1) * HEAD_DIM, g * HEAD_DIM:(g + 1) * HEAD_DIM] for g in range(4)]),
            pool_scale=g_pscale[0], sgu_v_norm=g_vn[0, :HEAD_DIM], w_s=g_ws, b_s=g_bacc[:, ::HEAD_DIM].T,
            norm2=g_norm2[0], conv_b=jnp.concatenate([dcg[3], dcv[3]]))
    grad_x = dcur[None]

    big = {n: _adamw_weight([parts[l][i] for l in range(DEPTH)], wts[n], mom[n], var[n], f"adamw_{n}")
           for i, n in enumerate(SHARDED)}

    shapes = {n: wts[n].shape for n in SMALL}
    gs = _pack_small({n: jnp.stack([gsmall[l][n] for l in range(DEPTH)]) for n in SMALL}, shapes)
    gs_all = _all_gather([gs], "gather_small_grads")[0]
    g_s, d_s, m_s, v_s = _adamw_sum(gs_all, _pack_small(wts, shapes), _pack_small(mom, shapes), _pack_small(var, shapes), "adamw_replicated")
    small = [_unpack_small(t, shapes) for t in (g_s, d_s, m_s, v_s)]

    outs = [loss, grad_x]
    for kind in range(4):
        outs += [small[kind][n] if n in SMALL else big[n][kind] for n in names]
    return tuple(outs)
```

```python
import functools
import math

import jax
import jax.numpy as jnp
from jax import lax
from jax.experimental import pallas as pl
from jax.experimental.pallas import tpu as pltpu

F32 = jnp.float32
BF16 = jnp.bfloat16

D_MODEL = 1024
DEPTH = 2
HEAD_DIM = 64
N_Q_HEADS = 8
Q_PER_KV = 4
BLOCK = 128
POOL_W = 256
ATTN_W = 512
KV_W = 128
SGU_W = 256
IN_COLS = 4608
GATE_COL0 = 1536
D_FF = 2816
EPS = 1e-6
ROPE_THETA = 10000.0
N_DEV = 8
LANES = 128
HALO_POOL = 16
HALO_CONV = 8

ADAM_LR = 0.001
ADAM_B1 = 0.9
ADAM_B2 = 0.999
ADAM_EPS = 1e-08
ADAM_WD = 0.01
ADAM_STEP = 10

VMEM_LIMIT = 48 * 1024 * 1024

SHARDED = ("w_in", "w_proj_a", "w_proj_b", "w_proj_c", "w_out", "w_up", "w_down", "conv_w")
ROW_SHARDED = ("w_out", "w_down")
SMALL_ROW_TILE = 256
SMALL = ("norm1", "q_norm", "k_norm", "sinks", "w_pool", "pool_scale", "sgu_v_norm", "w_s", "b_s", "norm2", "conv_b")

_GELU_C = math.sqrt(2.0 / math.pi)
_GELU_A = 0.044715


def _params(*sem):
    return pltpu.CompilerParams(dimension_semantics=sem, vmem_limit_bytes=VMEM_LIMIT)


def _tile(n, prefs):
    for t in prefs:
        if t <= n and n % t == 0:
            return t
    return n


def _head_mean_matrix():
    r = lax.broadcasted_iota(jnp.int32, (LANES, LANES), 0)
    c = lax.broadcasted_iota(jnp.int32, (LANES, LANES), 1)
    return jnp.where((r >= HEAD_DIM) == (c >= HEAD_DIM), 1.0 / HEAD_DIM, 0.0).astype(F32)


def _head_mean(v, bd):
    return jnp.dot(v, bd, precision=lax.Precision.HIGHEST, preferred_element_type=F32)


def _rot_half(t):
    lane = lax.broadcasted_iota(jnp.int32, t.shape, 1)
    return jnp.where((lane & 32) == 0, pltpu.roll(t, LANES - 32, 1), pltpu.roll(t, 32, 1))


def _norm_rope(t, gn, cosf, sinf, bd):
    r = lax.rsqrt(_head_mean(t * t, bd) + EPS)
    n = t * r
    y = n * gn
    return y * cosf + _rot_half(y) * sinf, n, r


def _norm_rope_bwd(d, t, n, r, gn, cosf, sinf, bd):
    dy = d * cosf + _rot_half(d * sinf)
    dgn = jnp.sum(dy * n, axis=0, keepdims=True)
    u = dy * gn
    dt = r * u - t * (r * r * r) * _head_mean(t * u, bd)
    return dt, dgn


def _gelu(x):
    t = jnp.tanh(_GELU_C * (x + _GELU_A * (x * x * x)))
    return 0.5 * x * (1.0 + t), t


def _gelu_grad(x, t):
    return 0.5 * (1.0 + t) + 0.5 * x * (1.0 - t * t) * (_GELU_C * (1.0 + 3.0 * _GELU_A * x * x))


def _sigmoid(x):
    return jax.nn.sigmoid(x)


def _dot(a, b, ca=1, cb=0):
    return lax.dot_general(a.astype(BF16), b.astype(BF16), (((ca,), (cb,)), ((), ())), preferred_element_type=F32)


def _mm(a, b, *, ta=False, tb=False, add=None, out_dtype=F32, name, b_n=None, b_k=None, out_cols=None, out_into=None):
    m = a.shape[1] if ta else a.shape[0]
    k = a.shape[0] if ta else a.shape[1]
    n = b_n[1] if b_n else (b.shape[0] if tb else b.shape[1])
    tm = _tile(m, (1024, 1408, 512, 256, 128))
    tn = _tile(n, (1024, 1152, 1408, 512, 256, 128))
    tk = _tile(k, (1024, 1152, 1408, 512, 256, 128))
    nk = k // tk
    n0 = b_n[0] // tn if b_n else 0
    k0 = b_k[0] // tk if b_k else 0
    o0, n_out = (out_cols[0] // tn, out_cols[1]) if out_cols else (0, n)
    has_add = add is not None
    n_in = 2 + has_add + (out_into is not None)

    def body(*refs):
        a_ref, b_ref = refs[0], refs[1]
        add_ref = refs[2] if has_add else None
        o_ref = refs[n_in]
        p = _dot(a_ref[...], b_ref[...], 0 if ta else 1, 1 if tb else 0)

        def finish(r):
            if has_add:
                r = r + add_ref[...]
            o_ref[...] = r.astype(out_dtype)

        if nk == 1:
            finish(p)
        else:
            acc_ref = refs[-1]
            kk = pl.program_id(2)

            @pl.when(kk == 0)
            def _():
                acc_ref[...] = p

            @pl.when(kk > 0)
            def _():
                acc_ref[...] += p

            @pl.when(kk == nk - 1)
            def _():
                finish(acc_ref[...])

    a_spec = pl.BlockSpec((tk, tm), lambda i, j, kk: (kk, i)) if ta else pl.BlockSpec((tm, tk), lambda i, j, kk: (i, kk))
    if tb:
        b_spec = pl.BlockSpec((tn, tk), lambda i, j, kk: (j + n0, kk + k0))
    else:
        b_spec = pl.BlockSpec((tk, tn), lambda i, j, kk: (kk + k0, j + n0))
    in_specs = [a_spec, b_spec] + ([pl.BlockSpec((tm, tn), lambda i, j, kk: (i, j))] if has_add else [])
    args = (a, b) + ((add,) if has_add else ())
    if out_into is not None:
        in_specs.append(pl.BlockSpec(memory_space=pl.ANY))
        args += (out_into,)
    return pl.pallas_call(
        body, name=name, grid=(m // tm, n // tn, nk), in_specs=in_specs,
        out_specs=pl.BlockSpec((tm, tn), lambda i, j, kk: (i, j + o0)),
        out_shape=jax.ShapeDtypeStruct((m, n_out), out_dtype),
        scratch_shapes=[pltpu.VMEM((tm, tn), F32)] if nk > 1 else [],
        input_output_aliases={n_in - 1: 0} if out_into is not None else {},
        compiler_params=_params("parallel", "parallel", "arbitrary"),
    )(*args)


def _rms_fwd(x, g, name):
    s, d = x.shape
    tr = _tile(s, (512, 256, 128))

    def body(x_ref, g_ref, h_ref):
        xv = x_ref[...]
        r = lax.rsqrt(jnp.mean(xv * xv, axis=-1, keepdims=True) + EPS)
        h_ref[...] = ((xv * r) * g_ref[...]).astype(BF16)

    return pl.pallas_call(
        body, name=name, grid=(s // tr,),
        in_specs=[pl.BlockSpec((tr, d), lambda i: (i, 0)), pl.BlockSpec((1, d), lambda i: (0, 0))],
        out_specs=pl.BlockSpec((tr, d), lambda i: (i, 0)),
        out_shape=jax.ShapeDtypeStruct((s, d), BF16), compiler_params=_params("parallel"),
    )(x, g.reshape(1, d))


def _rms_bwd(x, g, dh, dres, name):
    s, d = x.shape
    tr = _tile(s, (512, 256, 128))

    def body(x_ref, g_ref, dh_ref, dres_ref, dx_ref, dg_ref):
        xv = x_ref[...]
        r = lax.rsqrt(jnp.mean(xv * xv, axis=-1, keepdims=True) + EPS)
        dhv = dh_ref[...]
        u = dhv * g_ref[...]
        dx_ref[...] = dres_ref[...] + (r * u - xv * (r * r * r) * jnp.mean(xv * u, axis=-1, keepdims=True))
        part = jnp.sum(dhv * (xv * r), axis=0, keepdims=True)

        @pl.when(pl.program_id(0) == 0)
        def _():
            dg_ref[...] = part

        @pl.when(pl.program_id(0) > 0)
        def _():
            dg_ref[...] += part

    row = pl.BlockSpec((tr, d), lambda i: (i, 0))
    vec = pl.BlockSpec((1, d), lambda i: (0, 0))
    return pl.pallas_call(
        body, name=name, grid=(s // tr,), in_specs=[row, vec, row, row], out_specs=[row, vec],
        out_shape=[jax.ShapeDtypeStruct((s, d), F32), jax.ShapeDtypeStruct((1, d), F32)],
        compiler_params=_params("arbitrary"),
    )(x, g.reshape(1, d), dh, dres)


def _loss_head(y, target):
    s, d = y.shape
    tr = _tile(s, (512, 256, 128))

    def body(y_ref, t_ref, dy_ref, l_ref):
        err = y_ref[...] - t_ref[...]
        dy_ref[...] = err * (1.0 / d)
        part = jnp.sum(jnp.sum(err * err, axis=-1, keepdims=True) * (1.0 / d), axis=0, keepdims=True) * 0.5
        part = jnp.broadcast_to(part, (8, LANES))

        @pl.when(pl.program_id(0) == 0)
        def _():
            l_ref[...] = part

        @pl.when(pl.program_id(0) > 0)
        def _():
            l_ref[...] += part

    row = pl.BlockSpec((tr, d), lambda i: (i, 0))
    acc = pl.BlockSpec((8, LANES), lambda i: (0, 0))
    return pl.pallas_call(
        body, name="loss_head", grid=(s // tr,), in_specs=[row, row], out_specs=[row, acc],
        out_shape=[jax.ShapeDtypeStruct((s, d), F32), jax.ShapeDtypeStruct((8, LANES), F32)],
        compiler_params=_params("arbitrary"),
    )(y, target)


def _pool_lane_select(lane, v2, v4, v8, v16):
    return jnp.where(lane < 64, v2, jnp.where(lane < 128, v4, jnp.where(lane < 192, v8, v16)))


def _pool_diff(xc, xp, row0):
    n = BLOCK + HALO_POOL
    cat = jnp.concatenate([xp, xc], axis=0)
    s2 = cat + pltpu.roll(cat, 1, 0)
    s4 = s2 + pltpu.roll(s2, 2, 0)
    s8 = s4 + pltpu.roll(s4, 4, 0)
    s16 = s8 + pltpu.roll(s8, 8, 0)
    lane = lax.broadcasted_iota(jnp.int32, (n, POOL_W), 1)
    wsum = _pool_lane_select(lane, s2, s4, s8, s16)[HALO_POOL:]
    return wsum / _pool_count(row0, BLOCK) - xc


def _pool_count(row0, rows):
    lane = lax.broadcasted_iota(jnp.int32, (rows, POOL_W), 1)
    t = lax.broadcasted_iota(jnp.int32, (rows, POOL_W), 0) + row0
    return jnp.minimum(t + 1, _pool_lane_select(lane, 2, 4, 8, 16)).astype(F32)


def _pool_fwd(z, bdw, scale, name):
    s = z.shape[0]
    nb = s // BLOCK

    def body(xc_ref, xp_ref, w_ref, sc_ref, a_ref):
        i = pl.program_id(0)
        xp = jnp.where(i > 0, xp_ref[...], 0.0)
        diff = _pool_diff(xc_ref[...], xp, i * BLOCK)
        a_ref[...] = (_dot(diff, w_ref[...]) * sc_ref[...]).astype(BF16)

    return pl.pallas_call(
        body, name=name, grid=(nb,),
        in_specs=[pl.BlockSpec((BLOCK, POOL_W), lambda i: (i, 0)),
                  pl.BlockSpec((HALO_POOL, POOL_W), lambda i: (jnp.maximum(i * (BLOCK // HALO_POOL) - 1, 0), 0)),
                  pl.BlockSpec((POOL_W, POOL_W), lambda i: (0, 0)),
                  pl.BlockSpec((1, POOL_W), lambda i: (0, 0))],
        out_specs=pl.BlockSpec((BLOCK, POOL_W), lambda i: (i, 0)),
        out_shape=jax.ShapeDtypeStruct((s, POOL_W), BF16), compiler_params=_params("parallel"),
    )(z, z, bdw, scale.reshape(1, POOL_W))


def _pool_bwd(z, da, bdw, scale, name):
    s = z.shape[0]
    nb = s // BLOCK
    per = BLOCK // HALO_POOL
    n = BLOCK + HALO_POOL

    def body(xc_ref, xp_ref, dac_ref, dan_ref, w_ref, sc_ref, dx_ref, dw_ref, dsc_ref):
        i = pl.program_id(0)
        xp = jnp.where(i > 0, xp_ref[...], 0.0)
        diff = _pool_diff(xc_ref[...], xp, i * BLOCK)
        mixed = _dot(diff, w_ref[...])
        dac = dac_ref[...]
        dan = jnp.where(i < nb - 1, dan_ref[...], 0.0)
        dmix = jnp.concatenate([dac, dan], axis=0) * sc_ref[...]
        ddiff = _dot(dmix, w_ref[...], 1, 1)
        e = ddiff / _pool_count(i * BLOCK, n)
        f2 = e + pltpu.roll(e, n - 1, 0)
        f4 = f2 + pltpu.roll(f2, n - 2, 0)
        f8 = f4 + pltpu.roll(f4, n - 4, 0)
        f16 = f8 + pltpu.roll(f8, n - 8, 0)
        lane = lax.broadcasted_iota(jnp.int32, (n, POOL_W), 1)
        back = _pool_lane_select(lane, f2, f4, f8, f16)
        dx_ref[...] = (back[:BLOCK] - ddiff[:BLOCK]).astype(BF16)
        dw = _dot(diff, dmix[:BLOCK], 0, 0)
        dsc = jnp.sum(dac * mixed, axis=0, keepdims=True)

        @pl.when(i == 0)
        def _():
            dw_ref[...] = dw
            dsc_ref[...] = dsc

        @pl.when(i > 0)
        def _():
            dw_ref[...] += dw
            dsc_ref[...] += dsc

    blk = pl.BlockSpec((BLOCK, POOL_W), lambda i: (i, 0))
    return pl.pallas_call(
        body, name=name, grid=(nb,),
        in_specs=[blk, pl.BlockSpec((HALO_POOL, POOL_W), lambda i: (jnp.maximum(i * per - 1, 0), 0)),
                  blk, pl.BlockSpec((HALO_POOL, POOL_W), lambda i: (jnp.minimum((i + 1) * per, nb * per - 1), 0)),
                  pl.BlockSpec((POOL_W, POOL_W), lambda i: (0, 0)), pl.BlockSpec((1, POOL_W), lambda i: (0, 0))],
        out_specs=[blk, pl.BlockSpec((POOL_W, POOL_W), lambda i: (0, 0)), pl.BlockSpec((1, POOL_W), lambda i: (0, 0))],
        out_shape=[jax.ShapeDtypeStruct((s, POOL_W), BF16), jax.ShapeDtypeStruct((POOL_W, POOL_W), F32),
                   jax.ShapeDtypeStruct((1, POOL_W), F32)],
        compiler_params=_params("arbitrary"),
    )(z, z, da, da, bdw, scale.reshape(1, POOL_W))


def _attn_setup(zc_ref, zp_ref, cc_ref, cp_ref, sc_ref, sp_ref, qn_ref, kn_ref, bd):
    q = []
    for j in range(ATTN_W // LANES):
        t = zc_ref[:, POOL_W + j * LANES:POOL_W + (j + 1) * LANES]
        q.append((t,) + _norm_rope(t, qn_ref[...], cc_ref[...], sc_ref[...], bd))
    kc_raw = zc_ref[:, POOL_W + ATTN_W:POOL_W + ATTN_W + KV_W]
    kc = _norm_rope(kc_raw, kn_ref[...], cc_ref[...], sc_ref[...], bd)[0]
    kp = _norm_rope(zp_ref[:, :KV_W], kn_ref[...], cp_ref[...], sp_ref[...], bd)[0]
    kband = jnp.concatenate([kp, kc], axis=0).astype(BF16)
    vband = jnp.concatenate([zp_ref[:, KV_W:], zc_ref[:, POOL_W + ATTN_W + KV_W:POOL_W + ATTN_W + 2 * KV_W]], axis=0).astype(BF16)
    return q, kband, vband


def _attn_mask(i):
    row = lax.broadcasted_iota(jnp.int32, (BLOCK, 2 * BLOCK), 0)
    col = lax.broadcasted_iota(jnp.int32, (BLOCK, 2 * BLOCK), 1)
    dist = row + BLOCK - col
    return (dist >= 0) & (dist < BLOCK) & ((col >= BLOCK) | (i > 0))


def _to_kv_lanes(t, h):
    kvh = h // Q_PER_KV
    if (h % 2) != kvh:
        t = pltpu.roll(t, HEAD_DIM, 1)
    lane = lax.broadcasted_iota(jnp.int32, t.shape, 1)
    return jnp.where((lane >= HEAD_DIM) == (kvh == 1), t, 0.0)


def _from_kv_lanes(t, h):
    kvh = h // Q_PER_KV
    lane = lax.broadcasted_iota(jnp.int32, t.shape, 1)
    t = jnp.where((lane >= HEAD_DIM) == (kvh == 1), t, 0.0)
    if (h % 2) != kvh:
        t = pltpu.roll(t, HEAD_DIM, 1)
    return t


def _attn_probs(qh, kband, mask, sink):
    sc = _dot(qh, kband, 1, 1) * (HEAD_DIM ** -0.5)
    sc = jnp.where(mask, sc, -1e30)
    m = jnp.maximum(jnp.max(sc, axis=1, keepdims=True), sink)
    p = jnp.exp(sc - m)
    psink = jnp.exp(sink - m)
    den = jnp.sum(p, axis=1, keepdims=True) + psink
    return p / den, psink / den


def _attn_specs(nb):
    cur = lambda i: (i, 0)
    prev = lambda i: (jnp.maximum(i - 1, 0), 0)
    tab = lambda f: pl.BlockSpec((BLOCK, LANES), f)
    vec = pl.BlockSpec((1, LANES), lambda i: (0, 0))
    return [pl.BlockSpec((BLOCK, 1024), cur),
            pl.BlockSpec((BLOCK, 2 * KV_W), lambda i: (jnp.maximum(i - 1, 0), 3)),
            tab(cur), tab(prev), tab(cur), tab(prev), vec, vec,
            pl.BlockSpec(memory_space=pltpu.SMEM)]


def _attn_fwd(z, cosf, sinf, qn, kn, sinks, name):
    s = z.shape[0]
    nb = s // BLOCK

    def body(zc_ref, zp_ref, cc_ref, cp_ref, sc_ref, sp_ref, qn_ref, kn_ref, sink_ref, o_ref):
        i = pl.program_id(0)
        bd = _head_mean_matrix()
        q, kband, vband = _attn_setup(zc_ref, zp_ref, cc_ref, cp_ref, sc_ref, sp_ref, qn_ref, kn_ref, bd)
        mask = _attn_mask(i)
        out = [None] * (ATTN_W // LANES)
        for h in range(N_Q_HEADS):
            qh = _to_kv_lanes(q[h // 2][1], h)
            probs, _ = _attn_probs(qh, kband, mask, sink_ref[h])
            o = _from_kv_lanes(_dot(probs, vband), h)
            out[h // 2] = o if out[h // 2] is None else out[h // 2] + o
        for j, o in enumerate(out):
            o_ref[:, j * LANES:(j + 1) * LANES] = o.astype(BF16)

    return pl.pallas_call(
        body, name=name, grid=(nb,), in_specs=_attn_specs(nb),
        out_specs=pl.BlockSpec((BLOCK, ATTN_W), lambda i: (i, 0)),
        out_shape=jax.ShapeDtypeStruct((s, ATTN_W), BF16), compiler_params=_params("parallel"),
    )(z, z, cosf, cosf, sinf, sinf, qn, kn, sinks)


def _attn_bwd(z, cosf, sinf, qn, kn, sinks, d_out, name):
    s = z.shape[0]
    nb = s // BLOCK
    nt = ATTN_W // LANES

    def body(zc_ref, zp_ref, cc_ref, cp_ref, sc_ref, sp_ref, qn_ref, kn_ref, sink_ref, do_ref,
             dq_ref, dkc_ref, dkp_ref, dvc_ref, dvp_ref, dqn_ref, dsink_ref):
        i = pl.program_id(0)
        bd = _head_mean_matrix()
        q, kband, vband = _attn_setup(zc_ref, zp_ref, cc_ref, cp_ref, sc_ref, sp_ref, qn_ref, kn_ref, bd)
        mask = _attn_mask(i)

        @pl.when(i == 0)
        def _():
            dqn_ref[...] = jnp.zeros_like(dqn_ref)
            dsink_ref[...] = jnp.zeros_like(dsink_ref)

        dq = [None] * nt
        dk = jnp.zeros((2 * BLOCK, KV_W), F32)
        dv = jnp.zeros((2 * BLOCK, KV_W), F32)
        for h in range(N_Q_HEADS):
            qh = _to_kv_lanes(q[h // 2][1], h)
            probs, psink = _attn_probs(qh, kband, mask, sink_ref[h])
            doh = _to_kv_lanes(do_ref[:, (h // 2) * LANES:(h // 2 + 1) * LANES], h)
            dp = _dot(doh, vband, 1, 1)
            delta = jnp.sum(dp * probs, axis=1, keepdims=True)
            ds = (probs * (dp - delta)) * (HEAD_DIM ** -0.5)
            dsink = jnp.sum(-psink * delta, axis=0, keepdims=True)
            dsink_ref[h:h + 1, :] += jnp.broadcast_to(dsink, (1, LANES))
            t = _from_kv_lanes(_dot(ds, kband), h)
            dq[h // 2] = t if dq[h // 2] is None else dq[h // 2] + t
            dk = dk + _dot(ds, qh, 0, 0)
            dv = dv + _dot(probs, doh, 0, 0)
        dgn = jnp.zeros((1, LANES), F32)
        for j in range(nt):
            t, _, n, r = q[j]
            dt, g = _norm_rope_bwd(dq[j], t, n, r, qn_ref[...], cc_ref[...], sc_ref[...], bd)
            dq_ref[:, j * LANES:(j + 1) * LANES] = dt.astype(BF16)
            dgn = dgn + g
        dqn_ref[...] += jnp.broadcast_to(dgn, (8, LANES))
        dkp_ref[...] = dk[:BLOCK]
        dkc_ref[...] = dk[BLOCK:]
        dvp_ref[...] = dv[:BLOCK]
        dvc_ref[...] = dv[BLOCK:]

        @pl.when(i == nb - 1)
        def _():
            acc = dqn_ref[...]
            dqn_ref[...] = acc + pltpu.roll(acc, HEAD_DIM, 1)

    blk = lambda w: pl.BlockSpec((BLOCK, w), lambda i: (i, 0))
    acc = pl.BlockSpec((8, LANES), lambda i: (0, 0))
    kv = jax.ShapeDtypeStruct((s, KV_W), F32)
    return pl.pallas_call(
        body, name=name, grid=(nb,), in_specs=_attn_specs(nb) + [blk(ATTN_W)],
        out_specs=[blk(ATTN_W), blk(KV_W), blk(KV_W), blk(KV_W), blk(KV_W), acc, acc],
        out_shape=[jax.ShapeDtypeStruct((s, ATTN_W), BF16), kv, kv, kv, kv,
                   jax.ShapeDtypeStruct((8, LANES), F32), jax.ShapeDtypeStruct((8, LANES), F32)],
        compiler_params=_params("arbitrary"),
    )(z, z, cosf, cosf, sinf, sinf, qn, kn, sinks, d_out)


def _kv_post(z, cosf, sinf, kn, dkc, dkp, dvc, dvp, dxp, dq, duv, dz, name):
    s = z.shape[0]
    nb = s // BLOCK

    def body(zk_ref, c_ref, s_ref, kn_ref, dkc_ref, dkp_ref, dvc_ref, dvp_ref, dxp_ref, dq_ref, duv_ref, dz_in,
             dz_ref, dkn_ref):
        j = pl.program_id(0)
        bd = _head_mean_matrix()
        last = j == nb - 1
        d = dkc_ref[...] + jnp.where(last, 0.0, dkp_ref[...])
        t = zk_ref[:, :KV_W]
        _, n, r = _norm_rope(t, kn_ref[...], c_ref[...], s_ref[...], bd)
        dt, g = _norm_rope_bwd(d, t, n, r, kn_ref[...], c_ref[...], s_ref[...], bd)
        dvv = dvc_ref[...] + jnp.where(last, 0.0, dvp_ref[...])
        dz_ref[:, 0:POOL_W] = dxp_ref[...]
        dz_ref[:, POOL_W:POOL_W + ATTN_W] = dq_ref[...]
        dz_ref[:, POOL_W + ATTN_W:POOL_W + ATTN_W + KV_W] = dt.astype(BF16)
        dz_ref[:, POOL_W + ATTN_W + KV_W:POOL_W + ATTN_W + 2 * KV_W] = dvv.astype(BF16)
        dz_ref[:, POOL_W + ATTN_W + 2 * KV_W:GATE_COL0] = duv_ref[...]

        @pl.when(j == 0)
        def _():
            dkn_ref[...] = jnp.zeros_like(dkn_ref)

        dkn_ref[...] += jnp.broadcast_to(g, (8, LANES))

        @pl.when(last)
        def _():
            acc = dkn_ref[...]
            dkn_ref[...] = acc + pltpu.roll(acc, HEAD_DIM, 1)

    cur = lambda w: pl.BlockSpec((BLOCK, w), lambda j: (j, 0))
    nxt = pl.BlockSpec((BLOCK, KV_W), lambda j: (jnp.minimum(j + 1, nb - 1), 0))
    vec = pl.BlockSpec((1, LANES), lambda j: (0, 0))
    return pl.pallas_call(
        body, name=name, grid=(nb,),
        in_specs=[pl.BlockSpec((BLOCK, 2 * KV_W), lambda j: (j, 3)), cur(LANES), cur(LANES), vec,
                  cur(KV_W), nxt, cur(KV_W), nxt, cur(POOL_W), cur(ATTN_W), cur(2 * SGU_W),
                  pl.BlockSpec(memory_space=pl.ANY)],
        out_specs=[pl.BlockSpec((BLOCK, GATE_COL0), lambda j: (j, 0)), pl.BlockSpec((8, LANES), lambda j: (0, 0))],
        out_shape=[jax.ShapeDtypeStruct(dz.shape, dz.dtype), jax.ShapeDtypeStruct((8, LANES), F32)],
        input_output_aliases={11: 0}, compiler_params=_params("arbitrary"),
    )(z, cosf, sinf, kn, dkc, dkp, dvc, dvp, dxp, dq, duv, dz)


def _sgu_setup(z_ref, ws_ref, vn_ref, bd):
    us = z_ref[:, :SGU_W]
    vs = z_ref[:, SGU_W:]
    ug, tu = _gelu(us)
    gv, tv = _gelu(vs)
    rr = jnp.concatenate([lax.rsqrt(_head_mean(gv[:, k * LANES:(k + 1) * LANES] ** 2, bd) + EPS) for k in range(2)], axis=1)
    vg = (gv * rr) * vn_ref[...]
    tril = lax.broadcasted_iota(jnp.int32, (BLOCK, BLOCK), 0) >= lax.broadcasted_iota(jnp.int32, (BLOCK, BLOCK), 1)
    w = [jnp.where(tril, ws_ref[g], 0.0).astype(BF16) for g in range(4)]
    return us, vs, ug, tu, gv, tv, rr, vg, w, tril


def _group_select(parts):
    lane = lax.broadcasted_iota(jnp.int32, parts[0].shape, 1)
    return _pool_lane_select(lane, *parts)


def _sgu_fwd(z, ws, bcol, vn, name):
    s = z.shape[0]
    nb = s // BLOCK

    def body(z_ref, ws_ref, b_ref, vn_ref, c_ref):
        bd = _head_mean_matrix()
        _, _, ug, _, _, _, _, vg, w, _ = _sgu_setup(z_ref, ws_ref, vn_ref, bd)
        sg = _group_select([_dot(w[g], vg) for g in range(4)]) + b_ref[...]
        c_ref[...] = (ug * sg).astype(BF16)

    return pl.pallas_call(
        body, name=name, grid=(nb,),
        in_specs=[pl.BlockSpec((BLOCK, 2 * SGU_W), lambda i: (i, 2)), pl.BlockSpec((4, BLOCK, BLOCK), lambda i: (0, 0, 0)),
                  pl.BlockSpec((BLOCK, SGU_W), lambda i: (0, 0)), pl.BlockSpec((1, SGU_W), lambda i: (0, 0))],
        out_specs=pl.BlockSpec((BLOCK, SGU_W), lambda i: (i, 0)),
        out_shape=jax.ShapeDtypeStruct((s, SGU_W), BF16), compiler_params=_params("parallel"),
    )(z, ws, bcol, vn)


def _sgu_bwd(z, ws, bcol, vn, dc, name):
    s = z.shape[0]
    nb = s // BLOCK

    def body(z_ref, ws_ref, b_ref, vn_ref, dc_ref, duv_ref, dws_ref, db_ref, dvn_ref):
        i = pl.program_id(0)
        bd = _head_mean_matrix()
        us, vs, ug, tu, gv, tv, rr, vg, w, tril = _sgu_setup(z_ref, ws_ref, vn_ref, bd)
        sg = _group_select([_dot(w[g], vg) for g in range(4)]) + b_ref[...]
        dcv = dc_ref[...]
        dug = dcv * sg
        dsg = dcv * ug
        lane = lax.broadcasted_iota(jnp.int32, dsg.shape, 1)

        @pl.when(i == 0)
        def _():
            dws_ref[...] = jnp.zeros_like(dws_ref)
            db_ref[...] = jnp.zeros_like(db_ref)
            dvn_ref[...] = jnp.zeros_like(dvn_ref)

        for g in range(4):
            dsg_g = jnp.where((lane >= g * HEAD_DIM) & (lane < (g + 1) * HEAD_DIM), dsg, 0.0)
            dws_ref[g] += jnp.where(tril, _dot(dsg_g, vg, 1, 1), 0.0)
        dvg = _group_select([_dot(w[g], dsg, 0, 0) for g in range(4)])
        db_ref[...] += dsg
        n = gv * rr
        part = jnp.sum(dvg * n, axis=0, keepdims=True)
        dvn_ref[...] += jnp.broadcast_to(part[:, :LANES] + part[:, LANES:], (8, LANES))
        u = dvg * vn_ref[...]
        tu_ = gv * u
        mean = jnp.concatenate([_head_mean(tu_[:, k * LANES:(k + 1) * LANES], bd) for k in range(2)], axis=1)
        dgv = rr * u - gv * (rr * rr * rr) * mean
        duv_ref[:, :SGU_W] = (dug * _gelu_grad(us, tu)).astype(BF16)
        duv_ref[:, SGU_W:] = (dgv * _gelu_grad(vs, tv)).astype(BF16)

        @pl.when(i == nb - 1)
        def _():
            acc = dvn_ref[...]
            dvn_ref[...] = acc + pltpu.roll(acc, HEAD_DIM, 1)
            for k in range(2):
                db_ref[:, k * LANES:(k + 1) * LANES] = _head_mean(db_ref[:, k * LANES:(k + 1) * LANES], bd) * float(HEAD_DIM)

    return pl.pallas_call(
        body, name=name, grid=(nb,),
        in_specs=[pl.BlockSpec((BLOCK, 2 * SGU_W), lambda i: (i, 2)), pl.BlockSpec((4, BLOCK, BLOCK), lambda i: (0, 0, 0)),
                  pl.BlockSpec((BLOCK, SGU_W), lambda i: (0, 0)), pl.BlockSpec((1, SGU_W), lambda i: (0, 0)),
                  pl.BlockSpec((BLOCK, SGU_W), lambda i: (i, 0))],
        out_specs=[pl.BlockSpec((BLOCK, 2 * SGU_W), lambda i: (i, 0)), pl.BlockSpec((4, BLOCK, BLOCK), lambda i: (0, 0, 0)),
                   pl.BlockSpec((BLOCK, SGU_W), lambda i: (0, 0)), pl.BlockSpec((8, LANES), lambda i: (0, 0))],
        out_shape=[jax.ShapeDtypeStruct((s, 2 * SGU_W), BF16), jax.ShapeDtypeStruct((4, BLOCK, BLOCK), F32),
                   jax.ShapeDtypeStruct((BLOCK, SGU_W), F32), jax.ShapeDtypeStruct((8, LANES), F32)],
        compiler_params=_params("arbitrary"),
    )(z, ws, bcol, vn, dc)


MERGE_TN = 512
MERGE_TM = 256


def _merge_fwd(a, b, c, wpa, wpb, wpc, z, name):
    s = z.shape[0]
    tm = _tile(s, (MERGE_TM, BLOCK))
    gate0 = GATE_COL0 // MERGE_TN

    def body(a_ref, b_ref, c_ref, wa_ref, wb_ref, wc_ref, g0_ref, g1_ref, g2_ref, o_ref):
        r = _sigmoid(g0_ref[...]) * _dot(a_ref[...], wa_ref[...])
        r = r + _sigmoid(g1_ref[...]) * _dot(b_ref[...], wb_ref[...])
        r = r + _sigmoid(g2_ref[...]) * _dot(c_ref[...], wc_ref[...])
        o_ref[...] = r.astype(BF16)

    x_spec = lambda w: pl.BlockSpec((tm, w), lambda i, n: (i, 0))
    w_spec = lambda w: pl.BlockSpec((w, MERGE_TN), lambda i, n: (0, n))
    g_spec = lambda br: pl.BlockSpec((tm, MERGE_TN), lambda i, n: (i, gate0 + 2 * br + n))
    return pl.pallas_call(
        body, name=name, grid=(s // tm, D_MODEL // MERGE_TN),
        in_specs=[x_spec(POOL_W), x_spec(ATTN_W), x_spec(SGU_W), w_spec(POOL_W), w_spec(ATTN_W), w_spec(SGU_W),
                  g_spec(0), g_spec(1), g_spec(2)],
        out_specs=pl.BlockSpec((tm, MERGE_TN), lambda i, n: (i, n)),
        out_shape=jax.ShapeDtypeStruct((s, D_MODEL), BF16), compiler_params=_params("parallel", "parallel"),
    )(a, b, c, wpa, wpb, wpc, z, z, z)


def _branch_bwd(br, xb, wp, z, dm, dz, name):
    s = z.shape[0]
    kb = xb.shape[1]
    tm = _tile(s, (MERGE_TM, BLOCK))
    gate0 = GATE_COL0 // MERGE_TN
    aliased = dz is not None

    def body(*refs):
        x_ref, w_ref, g_ref, dm_ref = refs[:4]
        dz_ref, dy_ref, dx_ref = refs[-3:]
        n = pl.program_id(1)
        y = _dot(x_ref[...], w_ref[...])
        sg = _sigmoid(g_ref[...])
        dmv = dm_ref[...]
        dy = (dmv * sg).astype(BF16)
        dy_ref[...] = dy
        dz_ref[...] = ((dmv * y) * (sg * (1.0 - sg))).astype(BF16)
        dx = _dot(dy, w_ref[...], 1, 1)

        @pl.when(n == 0)
        def _():
            dx_ref[...] = dx

        @pl.when(n > 0)
        def _():
            dx_ref[...] += dx

    in_specs = [pl.BlockSpec((tm, kb), lambda i, n: (i, 0)), pl.BlockSpec((kb, MERGE_TN), lambda i, n: (0, n)),
                pl.BlockSpec((tm, MERGE_TN), lambda i, n: (i, gate0 + 2 * br + n)),
                pl.BlockSpec((tm, MERGE_TN), lambda i, n: (i, n))]
    args = [xb, wp, z, dm]
    if aliased:
        in_specs.append(pl.BlockSpec(memory_space=pl.ANY))
        args.append(dz)
    return pl.pallas_call(
        body, name=name, grid=(s // tm, D_MODEL // MERGE_TN), in_specs=in_specs,
        out_specs=[pl.BlockSpec((tm, MERGE_TN), lambda i, n: (i, gate0 + 2 * br + n)),
                   pl.BlockSpec((tm, MERGE_TN), lambda i, n: (i, n)),
                   pl.BlockSpec((tm, kb), lambda i, n: (i, 0))],
        out_shape=[jax.ShapeDtypeStruct((s, IN_COLS), BF16), jax.ShapeDtypeStruct((s, D_MODEL), BF16),
                   jax.ShapeDtypeStruct((s, kb), F32)],
        input_output_aliases={4: 0} if aliased else {},
        compiler_params=_params("parallel", "arbitrary"),
    )(*args)


FFN_TM = 256
FFN_TC = 1408


def _conv3(cur, prev, w_ref, b_ref):
    cat = jnp.concatenate([prev, cur], axis=0)
    x1 = pltpu.roll(cat, 1, 0)[HALO_CONV:]
    x2 = pltpu.roll(cat, 2, 0)[HALO_CONV:]
    return w_ref[0:1, :] * x2 + w_ref[1:2, :] * x1 + w_ref[2:3, :] * cur + b_ref[...], x1, x2


def _ffn_specs(s, tm, rows_first):
    per = tm // HALO_CONV
    if rows_first:
        cur = pl.BlockSpec((tm, FFN_TC), lambda i, j: (i, j))
        prev = pl.BlockSpec((HALO_CONV, FFN_TC), lambda i, j: (jnp.maximum(i * per - 1, 0), j))
        w = pl.BlockSpec((3, FFN_TC), lambda i, j: (0, j))
        b = pl.BlockSpec((1, FFN_TC), lambda i, j: (0, j))
    else:
        cur = pl.BlockSpec((tm, FFN_TC), lambda j, i: (i, j))
        prev = pl.BlockSpec((HALO_CONV, FFN_TC), lambda j, i: (jnp.maximum(i * per - 1, 0), j))
        w = pl.BlockSpec((3, FFN_TC), lambda j, i: (0, j))
        b = pl.BlockSpec((1, FFN_TC), lambda j, i: (0, j))
    return cur, prev, w, b


def _ffn_act_fwd(upg, upv, cwg, cwv, cbg, cbv, name):
    s = upg.shape[0]
    tm = _tile(s, (FFN_TM, BLOCK))
    cur, prev, w, b = _ffn_specs(s, tm, True)

    def body(g_ref, gp_ref, v_ref, vp_ref, wg_ref, wv_ref, bg_ref, bv_ref, o_ref):
        first = pl.program_id(0) == 0
        gate = _conv3(g_ref[...], jnp.where(first, 0.0, gp_ref[...]), wg_ref, bg_ref)[0]
        val = _conv3(v_ref[...], jnp.where(first, 0.0, vp_ref[...]), wv_ref, bv_ref)[0]
        o_ref[...] = ((gate * _sigmoid(gate)) * val).astype(BF16)

    return pl.pallas_call(
        body, name=name, grid=(s // tm, D_FF // FFN_TC), in_specs=[cur, prev, cur, prev, w, w, b, b], out_specs=cur,
        out_shape=jax.ShapeDtypeStruct((s, D_FF), BF16), compiler_params=_params("parallel", "parallel"),
    )(upg, upg, upv, upv, cwg, cwv, cbg, cbv)


def _ffn_act_bwd(upg, upv, cwg, cwv, cbg, cbv, dact, name):
    s = upg.shape[0]
    tm = _tile(s, (FFN_TM, BLOCK))
    cur, prev, w, b = _ffn_specs(s, tm, False)

    def body(g_ref, gp_ref, v_ref, vp_ref, wg_ref, wv_ref, bg_ref, bv_ref, da_ref, dg_ref, dv_ref, dwg_ref, dwv_ref):
        i = pl.program_id(1)
        first = i == 0
        gc, vc = g_ref[...], v_ref[...]
        gate, g1, g2 = _conv3(gc, jnp.where(first, 0.0, gp_ref[...]), wg_ref, bg_ref)
        val, v1, v2 = _conv3(vc, jnp.where(first, 0.0, vp_ref[...]), wv_ref, bv_ref)
        sg = _sigmoid(gate)
        da = da_ref[...]
        dgate = (da * val) * (sg * (1.0 + gate * (1.0 - sg)))
        dval = da * (gate * sg)
        dg_ref[...] = dgate
        dv_ref[...] = dval

        @pl.when(first)
        def _():
            dwg_ref[...] = jnp.zeros_like(dwg_ref)
            dwv_ref[...] = jnp.zeros_like(dwv_ref)

        for ref, d, x0, x1, x2 in ((dwg_ref, dgate, gc, g1, g2), (dwv_ref, dval, vc, v1, v2)):
            ref[0:1, :] += jnp.sum(d * x2, axis=0, keepdims=True)
            ref[1:2, :] += jnp.sum(d * x1, axis=0, keepdims=True)
            ref[2:3, :] += jnp.sum(d * x0, axis=0, keepdims=True)
            ref[3:4, :] += jnp.sum(d, axis=0, keepdims=True)

    acc = pl.BlockSpec((8, FFN_TC), lambda j, i: (0, j))
    full = jax.ShapeDtypeStruct((s, D_FF), F32)
    accs = jax.ShapeDtypeStruct((8, D_FF), F32)
    return pl.pallas_call(
        body, name=name, grid=(D_FF // FFN_TC, s // tm), in_specs=[cur, prev, cur, prev, w, w, b, b, cur],
        out_specs=[cur, cur, acc, acc], out_shape=[full, full, accs, accs],
        compiler_params=_params("parallel", "arbitrary"),
    )(upg, upg, upv, upv, cwg, cwv, cbg, cbv, dact)


def _conv3_t(d, cw, name):
    s = d.shape[0]
    tm = _tile(s, (FFN_TM, BLOCK))
    per = tm // HALO_CONV
    nrow = s // tm
    n = tm + HALO_CONV

    def body(d_ref, dn_ref, w_ref, o_ref):
        nxt = jnp.where(pl.program_id(0) == nrow - 1, 0.0, dn_ref[...])
        cur = d_ref[...]
        cat = jnp.concatenate([cur, nxt], axis=0)
        d1 = pltpu.roll(cat, n - 1, 0)[:tm]
        d2 = pltpu.roll(cat, n - 2, 0)[:tm]
        o_ref[...] = (w_ref[2:3, :] * cur + w_ref[1:2, :] * d1 + w_ref[0:1, :] * d2).astype(BF16)

    cur = pl.BlockSpec((tm, FFN_TC), lambda i, j: (i, j))
    return pl.pallas_call(
        body, name=name, grid=(nrow, D_FF // FFN_TC),
        in_specs=[cur, pl.BlockSpec((HALO_CONV, FFN_TC), lambda i, j: (jnp.minimum((i + 1) * per, nrow * per - 1), j)),
                  pl.BlockSpec((3, FFN_TC), lambda i, j: (0, j))],
        out_specs=cur, out_shape=jax.ShapeDtypeStruct((s, D_FF), BF16), compiler_params=_params("parallel", "parallel"),
    )(d, d, cw)


def _mesh_place():
    return lax.axis_index("x"), lax.axis_index("y"), lax.axis_index("c")


def _all_gather(shards, name):
    na = len(shards)

    def body(*refs):
        x_refs, out_refs = refs[:na], refs[na:2 * na]
        send_sems, recv_sems, local_sems = refs[2 * na:]
        x, y, cc = _mesh_place()
        me, sibling = (x, y, cc), (x, y, 1 - cc)
        chips = [(1 - x, y), (x, 1 - y), (1 - x, 1 - y)]

        def copy(k, a, block, to, from_input=False):
            slot = out_refs[a].at[4 * block[0] + 2 * block[1] + block[2]]
            return pltpu.make_async_remote_copy(
                src_ref=x_refs[a] if from_input else slot, dst_ref=slot, send_sem=send_sems.at[k * na + a],
                recv_sem=recv_sems.at[k * na + a], device_id=to, device_id_type=pl.DeviceIdType.MESH)

        mine = [pltpu.make_async_copy(x_refs[a], out_refs[a].at[4 * x + 2 * y + cc], local_sems.at[a]) for a in range(na)]
        for cp in mine:
            cp.start()
        first = [copy(0, a, me, sibling, True) for a in range(na)]
        first += [copy(1 + j, a, me, (*chip, cc), True) for j, chip in enumerate(chips) for a in range(na)]
        for cp in first:
            cp.start()
        passed = []
        for j, chip in enumerate(chips):
            for a in range(na):
                copy(1 + j, a, (*chip, cc), me).wait_recv()
                passed.append(copy(4 + j, a, (*chip, cc), sibling))
                passed[-1].start()
        for a in range(na):
            copy(0, a, sibling, me).wait_recv()
        for j, chip in enumerate(chips):
            for a in range(na):
                copy(4 + j, a, (*chip, 1 - cc), me).wait_recv()
        for cp in first + passed:
            cp.wait_send()
        for cp in mine:
            cp.wait()

    hbm = pl.BlockSpec(memory_space=pl.ANY)
    return pl.pallas_call(
        body, name=name, out_shape=[jax.ShapeDtypeStruct((N_DEV,) + t.shape, t.dtype) for t in shards],
        in_specs=[hbm] * na, out_specs=[hbm] * na,
        scratch_shapes=[pltpu.SemaphoreType.DMA((7 * na,)), pltpu.SemaphoreType.DMA((7 * na,)), pltpu.SemaphoreType.DMA((na,))],
    )(*shards)


def _exchange(blocks, name):
    na = len(blocks)

    def body(*refs):
        g_refs, out_refs = refs[:na], refs[na:2 * na]
        send_sems, recv_sems, local_sems = refs[2 * na:]
        x, y, cc = _mesh_place()
        me = 4 * x + 2 * y + cc
        mine = [pltpu.make_async_copy(g_refs[a].at[me], out_refs[a].at[me], local_sems.at[a]) for a in range(na)]
        for cp in mine:
            cp.start()
        sends, lands = [], []
        for k in range(1, N_DEV):
            px = 1 - x if (k >> 2) & 1 else x
            py = 1 - y if (k >> 1) & 1 else y
            pc = 1 - cc if k & 1 else cc
            peer = 4 * px + 2 * py + pc
            for a in range(na):
                sem = (k - 1) * na + a
                sends.append(pltpu.make_async_remote_copy(
                    src_ref=g_refs[a].at[peer], dst_ref=out_refs[a].at[me], send_sem=send_sems.at[sem], recv_sem=recv_sems.at[sem],
                    device_id=(px, py, pc), device_id_type=pl.DeviceIdType.MESH))
                lands.append(pltpu.make_async_remote_copy(
                    src_ref=g_refs[a].at[peer], dst_ref=out_refs[a].at[peer], send_sem=send_sems.at[sem], recv_sem=recv_sems.at[sem],
                    device_id=(px, py, pc), device_id_type=pl.DeviceIdType.MESH))
        for cp in sends:
            cp.start()
        for cp in lands:
            cp.wait_recv()
        for cp in sends:
            cp.wait_send()
        for cp in mine:
            cp.wait()

    hbm = pl.BlockSpec(memory_space=pl.ANY)
    return pl.pallas_call(
        body, name=name, out_shape=[jax.ShapeDtypeStruct(t.shape, t.dtype) for t in blocks],
        in_specs=[hbm] * na, out_specs=[hbm] * na,
        scratch_shapes=[pltpu.SemaphoreType.DMA((7 * na,)), pltpu.SemaphoreType.DMA((7 * na,)), pltpu.SemaphoreType.DMA((na,))],
    )(*blocks)


def _peer(k):
    x, y, cc = _mesh_place()
    px = 1 - x if (k >> 2) & 1 else x
    py = 1 - y if (k >> 1) & 1 else y
    pc = 1 - cc if k & 1 else cc
    return (px, py, pc), 4 * px + 2 * py + pc


def _push_copy(src_ref, land_ref, k, a, na, send_sems, recv_sems, indexed, landed):
    x, y, cc = _mesh_place()
    place, peer = _peer(k)
    sem = (k - 1) * na + a
    return pltpu.make_async_remote_copy(
        src_ref=src_ref.at[peer] if indexed else src_ref, dst_ref=land_ref.at[peer if landed else 4 * x + 2 * y + cc],
        send_sem=send_sems.at[sem], recv_sem=recv_sems.at[sem], device_id=place, device_id_type=pl.DeviceIdType.MESH)


_HBM = pl.BlockSpec(memory_space=pltpu.HBM)
_SEM = pl.BlockSpec(memory_space=pltpu.SEMAPHORE)
_EFFECT = pltpu.SideEffectType.DATAFLOW_SIDE_EFFECTING


def _push_start(srcs, indexed, name):
    na = len(srcs)
    lands = [lax.empty(t.shape if indexed else (N_DEV,) + t.shape, t.dtype) for t in srcs]

    def body(*refs):
        src_refs, land_refs = refs[:na], refs[na:2 * na]
        send_sems, recv_sems = refs[2 * na], refs[2 * na + 1]
        token = refs[-1]
        for k in range(1, N_DEV):
            for a in range(na):
                _push_copy(src_refs[a], land_refs[a], k, a, na, send_sems, recv_sems, indexed, False).start()
        token[...] = jnp.zeros_like(token)

    sems = pltpu.SemaphoreType.DMA((7 * na,))
    out = pl.pallas_call(
        body, name=name,
        out_shape=(sems, sems, *[pltpu.HBM(t.shape, t.dtype) for t in srcs], *[pltpu.HBM(t.shape, t.dtype) for t in lands],
                   jax.ShapeDtypeStruct((8, LANES), F32)),
        in_specs=[_HBM] * (2 * na), out_specs=(_SEM, _SEM, *[_HBM] * (2 * na), pl.BlockSpec(memory_space=pltpu.VMEM)),
        input_output_aliases={i: 2 + i for i in range(2 * na)},
        compiler_params=pltpu.CompilerParams(has_side_effects=_EFFECT),
    )(*[pltpu.with_memory_space_constraint(t, pltpu.HBM) for t in srcs + lands])
    return out[0], out[1], list(out[2:2 + na]), list(out[2 + na:2 + 2 * na]), out[-1]


def _push_wait(started, indexed, after, name):
    send_sems, recv_sems, srcs, lands, _ = started
    na = len(srcs)

    def body(*refs):
        src_refs, land_refs = refs[:na], refs[na:2 * na]
        send_sems, recv_sems = refs[2 * na], refs[2 * na + 1]
        for k in range(1, N_DEV):
            for a in range(na):
                copy = _push_copy(src_refs[a], land_refs[a], k, a, na, send_sems, recv_sems, indexed, True)
                copy.wait_send()
                copy.wait_recv()

    out = pl.pallas_call(
        body, name=name, out_shape=[pltpu.HBM(t.shape, t.dtype) for t in srcs + lands],
        in_specs=[_HBM] * (2 * na) + [_SEM, _SEM, pl.BlockSpec(memory_space=pl.ANY)], out_specs=[_HBM] * (2 * na),
        input_output_aliases={i: i for i in range(2 * na)},
        compiler_params=pltpu.CompilerParams(has_side_effects=_EFFECT),
    )(*srcs, *lands, send_sems, recv_sems, after)
    x, y, cc = _mesh_place()
    me = 4 * x + 2 * y + cc
    return [lax.dynamic_update_index_in_dim(
        land, lax.dynamic_index_in_dim(src, me, 0, keepdims=False) if indexed else src, me, 0)
        for src, land in zip(out[:na], out[na:])]


def _adamw_sum(parts, w, m, v, name):
    _, r, c = parts.shape
    tr = _tile(r, (256, 128, 64, 32, 16, 8))

    def body(p_ref, w_ref, m_ref, v_ref, g_ref, d_ref, nm_ref, nv_ref):
        _adam_store(_sum_parts(p_ref), w_ref, m_ref, v_ref, g_ref, d_ref, nm_ref, nv_ref)

    row = pl.BlockSpec((tr, c), lambda i: (i, 0))
    shp = jax.ShapeDtypeStruct((r, c), F32)
    return pl.pallas_call(
        body, name=name, grid=(r // tr,), in_specs=[pl.BlockSpec((N_DEV, tr, c), lambda i: (0, i, 0)), row, row, row],
        out_specs=[row, row, row, row], out_shape=[shp, shp, shp, shp], compiler_params=_params("parallel"),
    )(parts, w, m, v)


def _sum_parts(p_ref):
    g = p_ref[0].astype(F32)
    for k in range(1, N_DEV):
        g = g + p_ref[k].astype(F32)
    return g


def _adam_store(g, w_ref, m_ref, v_ref, g_ref, d_ref, nm_ref, nv_ref):
    nm = ADAM_B1 * m_ref[...] + (1.0 - ADAM_B1) * g
    nv = ADAM_B2 * v_ref[...] + (1.0 - ADAM_B2) * (g * g)
    m_hat = nm / (1.0 - ADAM_B1 ** ADAM_STEP)
    v_hat = nv / (1.0 - ADAM_B2 ** ADAM_STEP)
    g_ref[...] = g
    nm_ref[...] = nm
    nv_ref[...] = nv
    d_ref[...] = -ADAM_LR * (m_hat / (jnp.sqrt(v_hat) + ADAM_EPS) + ADAM_WD * w_ref[...])


def _adamw_weight(parts, w, m, v, name):
    _, r, c = w.shape
    tr = _tile(r, (256, 128, 176))
    nr = r // tr

    def body(p0_ref, p1_ref, w_ref, m_ref, v_ref, g_ref, d_ref, nm_ref, nv_ref):
        g = jnp.where(pl.program_id(0) == 0, _sum_parts(p0_ref), _sum_parts(p1_ref))
        _adam_store(g, w_ref, m_ref, v_ref, g_ref, d_ref, nm_ref, nv_ref)

    part = lambda layer: pl.BlockSpec(
        (N_DEV, tr, c), lambda l, i: (0, jnp.where(l == layer, i, (nr - 1) * (1 - layer)), 0))
    row = pl.BlockSpec((None, tr, c), lambda l, i: (l, i, 0))
    shp = jax.ShapeDtypeStruct(w.shape, F32)
    return pl.pallas_call(
        body, name=name, grid=(DEPTH, nr), in_specs=[part(0), part(1), row, row, row],
        out_specs=[row, row, row, row], out_shape=[shp, shp, shp, shp], compiler_params=_params("arbitrary", "arbitrary"),
    )(parts[0], parts[1], w, m, v)


def _full_to_slots(name, t):
    k, n = t.shape
    if name in ROW_SHARDED:
        return t.reshape(N_DEV, k // N_DEV, n)
    return t.reshape(k, N_DEV, n // N_DEV).transpose(1, 0, 2)


def _slots_to_full(name, t):
    _, r, c = t.shape
    if name in ROW_SHARDED:
        return t.reshape(N_DEV * r, c)
    return t.transpose(1, 0, 2).reshape(r, N_DEV * c)


def _small_sizes(shapes):
    return [(n, shapes[n], -(-int(math.prod(shapes[n])) // (8 * LANES)) * 8) for n in SMALL]


def _pack_small(tree, shapes):
    rows = []
    for n, shp, nrow in _small_sizes(shapes):
        flat = tree[n].reshape(-1)
        rows.append(jnp.pad(flat, (0, nrow * LANES - flat.shape[0])).reshape(nrow, LANES))
    total = sum(r.shape[0] for r in rows)
    rows.append(jnp.zeros((-total % SMALL_ROW_TILE, LANES), F32))
    return jnp.concatenate(rows, axis=0)


def _unpack_small(buf, shapes):
    out, r0 = {}, 0
    for n, shp, nrow in _small_sizes(shapes):
        out[n] = buf[r0:r0 + nrow].reshape(-1)[:int(math.prod(shp))].reshape(shp)
        r0 += nrow
    return out


def _block_diag(w):
    g = w.shape[0]
    eye = jnp.eye(g, dtype=w.dtype)
    return (eye[:, None, :, None] * w[:, :, None, :]).reshape(g * HEAD_DIM, g * HEAD_DIM)


def kernel(x, positions, norm1, w_in, q_norm, k_norm, sinks, w_pool, pool_scale, sgu_v_norm, w_s, b_s, w_proj_a, w_proj_b, w_proj_c, w_out, norm2, w_up, conv_w, conv_b, w_down, loss_target, m_norm1, m_w_in, m_q_norm, m_k_norm, m_sinks, m_w_pool, m_pool_scale, m_sgu_v_norm, m_w_s, m_b_s, m_w_proj_a, m_w_proj_b, m_w_proj_c, m_w_out, m_norm2, m_w_up, m_conv_w, m_conv_b, m_w_down, v_norm1, v_w_in, v_q_norm, v_k_norm, v_sinks, v_w_pool, v_pool_scale, v_sgu_v_norm, v_w_s, v_b_s, v_w_proj_a, v_w_proj_b, v_w_proj_c, v_w_out, v_norm2, v_w_up, v_conv_w, v_conv_b, v_w_down):
    names = ("norm1", "w_in", "q_norm", "k_norm", "sinks", "w_pool", "pool_scale", "sgu_v_norm", "w_s", "b_s", "w_proj_a",
             "w_proj_b", "w_proj_c", "w_out", "norm2", "w_up", "conv_w", "conv_b", "w_down")
    wts = dict(zip(names, (norm1, w_in, q_norm, k_norm, sinks, w_pool, pool_scale, sgu_v_norm, w_s, b_s, w_proj_a, w_proj_b,
                           w_proj_c, w_out, norm2, w_up, conv_w, conv_b, w_down)))
    mom = dict(zip(names, (m_norm1, m_w_in, m_q_norm, m_k_norm, m_sinks, m_w_pool, m_pool_scale, m_sgu_v_norm, m_w_s, m_b_s,
                           m_w_proj_a, m_w_proj_b, m_w_proj_c, m_w_out, m_norm2, m_w_up, m_conv_w, m_conv_b, m_w_down)))
    var = dict(zip(names, (v_norm1, v_w_in, v_q_norm, v_k_norm, v_sinks, v_w_pool, v_pool_scale, v_sgu_v_norm, v_w_s, v_b_s,
                           v_w_proj_a, v_w_proj_b, v_w_proj_c, v_w_out, v_norm2, v_w_up, v_conv_w, v_conv_b, v_w_down)))
    xs = x[0]
    target = loss_target[0]
    s = xs.shape[0]

    inv_freq = ROPE_THETA ** (-jnp.arange(0, HEAD_DIM, 2, dtype=F32) / HEAD_DIM)
    ang = positions[0].astype(F32)[:, None] * inv_freq
    cosf = jnp.tile(jnp.cos(ang), (1, 4))
    sinf = jnp.tile(jnp.concatenate([-jnp.sin(ang), jnp.sin(ang)], axis=1), (1, 2))

    local = [[wts[n][l] if n == "conv_w" else wts[n][l].astype(BF16) for n in SHARDED] for l in range(DEPTH)]
    full = [{n: _slots_to_full(n, t) for n, t in zip(SHARDED, _all_gather(local[0], "gather_weights_0"))}, None]
    gather1 = _push_start(local[1], False, "gather_weights_1_start")
    norm1_first = norm1[0] + gather1[4][0, 0]

    def layer_consts(l):
        return dict(
            bdw=_block_diag(w_pool[l]).astype(BF16), qn=jnp.tile(q_norm[l], 2).reshape(1, LANES),
            kn=jnp.tile(k_norm[l], 2).reshape(1, LANES), vn=jnp.tile(sgu_v_norm[l], 4).reshape(1, SGU_W),
            bcol=jnp.repeat(b_s[l].T, HEAD_DIM, axis=1),
            cwg=full[l]["conv_w"][:, :D_FF], cwv=full[l]["conv_w"][:, D_FF:],
            cbg=conv_b[l][:D_FF].reshape(1, D_FF), cbv=conv_b[l][D_FF:].reshape(1, D_FF))

    gate_cols, val_cols = (0, D_FF), (D_FF, D_FF)

    saved = []
    cur = xs
    for l in range(DEPTH):
        if l == 1:
            landed = _push_wait(gather1, False, cur, "gather_weights_1_wait")
            full[1] = {n: _slots_to_full(n, t) for n, t in zip(SHARDED, landed)}
        fw, k = full[l], layer_consts(l)
        h1 = _rms_fwd(cur, norm1_first if l == 0 else norm1[l], f"rms1_fwd_{l}")
        z = _mm(h1, fw["w_in"], name=f"in_proj_{l}")
        a = _pool_fwd(z, k["bdw"], pool_scale[l], f"pool_fwd_{l}")
        b = _attn_fwd(z, cosf, sinf, k["qn"], k["kn"], sinks[l], f"attn_fwd_{l}")
        c = _sgu_fwd(z, w_s[l], k["bcol"], k["vn"], f"sgu_fwd_{l}")
        merged = _merge_fwd(a, b, c, fw["w_proj_a"], fw["w_proj_b"], fw["w_proj_c"], z, f"merge_fwd_{l}")
        x1 = _mm(merged, fw["w_out"], add=cur, name=f"out_proj_{l}")
        h2 = _rms_fwd(x1, norm2[l], f"rms2_fwd_{l}")
        upg = _mm(h2, fw["w_up"], b_n=gate_cols, name=f"up_gate_{l}")
        upv = _mm(h2, fw["w_up"], b_n=val_cols, name=f"up_val_{l}")
        act = _ffn_act_fwd(upg, upv, k["cwg"], k["cwv"], k["cbg"], k["cbv"], f"ffn_act_fwd_{l}")
        x2 = _mm(act, fw["w_down"], add=x1, name=f"down_proj_{l}")
        saved.append(dict(x0=cur, h1=h1, z=z, a=a, b=b, c=c, merged=merged, x1=x1, h2=h2, upg=upg, upv=upv, act=act))
        cur = x2

    dcur, loss_tile = _loss_head(cur, target)
    loss = lax.psum(loss_tile[0, 0], ("x", "y", "c"))

    parts = [None] * DEPTH
    gsmall = [None] * DEPTH
    for l in reversed(range(DEPTH)):
        fw, k, sv = full[l], layer_consts(l), saved[l]
        wgrad = functools.partial(_mm, ta=True, out_dtype=BF16)
        w_down_l = fw["w_down"] if l == DEPTH - 1 else fw["w_down"] + exchange1[4][0, 0].astype(BF16)
        dact = _mm(dcur, w_down_l, tb=True, name=f"down_proj_bwd_{l}")
        g_down = wgrad(sv["act"], dcur, name=f"down_proj_wgrad_{l}")
        dgate, dval, dcg, dcv = _ffn_act_bwd(sv["upg"], sv["upv"], k["cwg"], k["cwv"], k["cbg"], k["cbv"], dact, f"ffn_act_bwd_{l}")
        dg0 = _conv3_t(dgate, k["cwg"], f"conv_t_gate_{l}")
        dv0 = _conv3_t(dval, k["cwv"], f"conv_t_val_{l}")
        dh2 = _mm(dg0, fw["w_up"], tb=True, b_k=gate_cols, name=f"up_gate_bwd_{l}")
        dh2 = _mm(dv0, fw["w_up"], tb=True, b_k=val_cols, add=dh2, name=f"up_val_bwd_{l}")
        g_up = wgrad(sv["h2"], dg0, out_cols=(0, 2 * D_FF), name=f"up_gate_wgrad_{l}")
        g_up = wgrad(sv["h2"], dv0, out_cols=(D_FF, 2 * D_FF), out_into=g_up, name=f"up_val_wgrad_{l}")
        dx1, g_norm2 = _rms_bwd(sv["x1"], norm2[l], dh2, dcur, f"rms2_bwd_{l}")
        dmerged = _mm(dx1, fw["w_out"], tb=True, name=f"out_proj_bwd_{l}")
        g_out = wgrad(sv["merged"], dx1, name=f"out_proj_wgrad_{l}")
        dz, dya, da = _branch_bwd(0, sv["a"], fw["w_proj_a"], sv["z"], dmerged, None, f"branch_a_bwd_{l}")
        dz, dyb, db = _branch_bwd(1, sv["b"], fw["w_proj_b"], sv["z"], dmerged, dz, f"branch_b_bwd_{l}")
        dz, dyc, dc = _branch_bwd(2, sv["c"], fw["w_proj_c"], sv["z"], dmerged, dz, f"branch_c_bwd_{l}")
        g_pa = wgrad(sv["a"], dya, name=f"proj_a_wgrad_{l}")
        g_pb = wgrad(sv["b"], dyb, name=f"proj_b_wgrad_{l}")
        g_pc = wgrad(sv["c"], dyc, name=f"proj_c_wgrad_{l}")
        dxp, g_bdw, g_pscale = _pool_bwd(sv["z"], da, k["bdw"], pool_scale[l], f"pool_bwd_{l}")
        dq, dkc, dkp, dvc, dvp, g_qn, g_sink = _attn_bwd(sv["z"], cosf, sinf, k["qn"], k["kn"], sinks[l], db, f"attn_bwd_{l}")
        duv, g_ws, g_bacc, g_vn = _sgu_bwd(sv["z"], w_s[l], k["bcol"], k["vn"], dc, f"sgu_bwd_{l}")
        dz, g_kn = _kv_post(sv["z"], cosf, sinf, k["kn"], dkc, dkp, dvc, dvp, dxp, dq, duv, dz, f"kv_post_{l}")
        dh1 = _mm(dz, fw["w_in"], tb=True, name=f"in_proj_bwd_{l}")
        g_in = wgrad(sv["h1"], dz, name=f"in_proj_wgrad_{l}")
        dcur, g_norm1 = _rms_bwd(sv["x0"], norm1[l], dh1, dx1, f"rms1_bwd_{l}")
        gfull = dict(w_in=g_in, w_proj_a=g_pa, w_proj_b=g_pb, w_proj_c=g_pc, w_out=g_out, w_up=g_up, w_down=g_down,
                     conv_w=jnp.concatenate([dcg[0:3], dcv[0:3]], axis=1))
        slots = [_full_to_slots(n, gfull[n]) for n in SHARDED]
        if l == DEPTH - 1:
            exchange1 = _push_start(slots, True, "exchange_grads_1_start")
        else:
            parts[1] = _push_wait(exchange1, True, dcur, "exchange_grads_1_wait")
            parts[0] = _exchange(slots, "exchange_grads_0")
        gsmall[l] = dict(
            norm1=g_norm1[0], q_norm=g_qn[0, :HEAD_DIM], k_norm=g_kn[0, :HEAD_DIM], sinks=g_sink[:, 0],
            w_pool=jnp.stack([g_bdw[g * HEAD_DIM:(g + 1) * HEAD_DIM, g * HEAD_DIM:(g + 1) * HEAD_DIM] for g in range(4)]),
            pool_scale=g_pscale[0], sgu_v_norm=g_vn[0, :HEAD_DIM], w_s=g_ws, b_s=g_bacc[:, ::HEAD_DIM].T,
            norm2=g_norm2[0], conv_b=jnp.concatenate([dcg[3], dcv[3]]))
    grad_x = dcur[None]

    big = {n: _adamw_weight([parts[l][i] for l in range(DEPTH)], wts[n], mom[n], var[n], f"adamw_{n}")
           for i, n in enumerate(SHARDED)}

    shapes = {n: wts[n].shape for n in SMALL}
    gs = _pack_small({n: jnp.stack([gsmall[l][n] for l in range(DEPTH)]) for n in SMALL}, shapes)
    gs_all = _all_gather([gs], "gather_small_grads")[0]
    g_s, d_s, m_s, v_s = _adamw_sum(gs_all, _pack_small(wts, shapes), _pack_small(mom, shapes), _pack_small(var, shapes), "adamw_replicated")
    small = [_unpack_small(t, shapes) for t in (g_s, d_s, m_s, v_s)]

    outs = [loss, grad_x]
    for kind in range(4):
        outs += [small[kind][n] if n in SMALL else big[n][kind] for n in names]
    return tuple(outs)
```

```python
import functools
import math

import jax
import jax.numpy as jnp
from jax import lax
from jax.experimental import pallas as pl
from jax.experimental.pallas import tpu as pltpu

F32 = jnp.float32
BF16 = jnp.bfloat16

D_MODEL = 1024
DEPTH = 2
HEAD_DIM = 64
N_Q_HEADS = 8
Q_PER_KV = 4
BLOCK = 128
POOL_W = 256
ATTN_W = 512
KV_W = 128
SGU_W = 256
IN_COLS = 4608
GATE_COL0 = 1536
D_FF = 2816
EPS = 1e-6
ROPE_THETA = 10000.0
N_DEV = 8
LANES = 128
HALO_POOL = 16
HALO_CONV = 8

ADAM_LR = 0.001
ADAM_B1 = 0.9
ADAM_B2 = 0.999
ADAM_EPS = 1e-08
ADAM_WD = 0.01
ADAM_STEP = 10

VMEM_LIMIT = 48 * 1024 * 1024

SHARDED = ("w_in", "w_proj_a", "w_proj_b", "w_proj_c", "w_out", "w_up", "w_down", "conv_w")
ROW_SHARDED = ("w_out", "w_down")
SMALL_ROW_TILE = 256
SMALL = ("norm1", "q_norm", "k_norm", "sinks", "w_pool", "pool_scale", "sgu_v_norm", "w_s", "b_s", "norm2", "conv_b")

_GELU_C = math.sqrt(2.0 / math.pi)
_GELU_A = 0.044715


def _params(*sem):
    return pltpu.CompilerParams(dimension_semantics=sem, vmem_limit_bytes=VMEM_LIMIT)


def _tile(n, prefs):
    for t in prefs:
        if t <= n and n % t == 0:
            return t
    return n


def _head_mean_matrix():
    r = lax.broadcasted_iota(jnp.int32, (LANES, LANES), 0)
    c = lax.broadcasted_iota(jnp.int32, (LANES, LANES), 1)
    return jnp.where((r >= HEAD_DIM) == (c >= HEAD_DIM), 1.0 / HEAD_DIM, 0.0).astype(F32)


def _head_mean(v, bd):
    return jnp.dot(v, bd, precision=lax.Precision.HIGHEST, preferred_element_type=F32)


def _rot_half(t):
    lane = lax.broadcasted_iota(jnp.int32, t.shape, 1)
    return jnp.where((lane & 32) == 0, pltpu.roll(t, LANES - 32, 1), pltpu.roll(t, 32, 1))


def _norm_rope(t, gn, cosf, sinf, bd):
    r = lax.rsqrt(_head_mean(t * t, bd) + EPS)
    n = t * r
    y = n * gn
    return y * cosf + _rot_half(y) * sinf, n, r


def _norm_rope_bwd(d, t, n, r, gn, cosf, sinf, bd):
    dy = d * cosf + _rot_half(d * sinf)
    dgn = jnp.sum(dy * n, axis=0, keepdims=True)
    u = dy * gn
    dt = r * u - t * (r * r * r) * _head_mean(t * u, bd)
    return dt, dgn


def _gelu(x):
    t = jnp.tanh(_GELU_C * (x + _GELU_A * (x * x * x)))
    return 0.5 * x * (1.0 + t), t


def _gelu_grad(x, t):
    return 0.5 * (1.0 + t) + 0.5 * x * (1.0 - t * t) * (_GELU_C * (1.0 + 3.0 * _GELU_A * x * x))


def _sigmoid(x):
    return jax.nn.sigmoid(x)


def _dot(a, b, ca=1, cb=0):
    return lax.dot_general(a.astype(BF16), b.astype(BF16), (((ca,), (cb,)), ((), ())), preferred_element_type=F32)


def _mm(a, b, *, ta=False, tb=False, add=None, out_dtype=F32, name, b_n=None, b_k=None, out_cols=None, out_into=None):
    m = a.shape[1] if ta else a.shape[0]
    k = a.shape[0] if ta else a.shape[1]
    n = b_n[1] if b_n else (b.shape[0] if tb else b.shape[1])
    tm = _tile(m, (1024, 1408, 512, 256, 128))
    tn = _tile(n, (1024, 1152, 1408, 512, 256, 128))
    tk = _tile(k, (1024, 1152, 1408, 512, 256, 128))
    nk = k // tk
    n0 = b_n[0] // tn if b_n else 0
    k0 = b_k[0] // tk if b_k else 0
    o0, n_out = (out_cols[0] // tn, out_cols[1]) if out_cols else (0, n)
    has_add = add is not None
    n_in = 2 + has_add + (out_into is not None)

    def body(*refs):
        a_ref, b_ref = refs[0], refs[1]
        add_ref = refs[2] if has_add else None
        o_ref = refs[n_in]
        p = _dot(a_ref[...], b_ref[...], 0 if ta else 1, 1 if tb else 0)

        def finish(r):
            if has_add:
                r = r + add_ref[...]
            o_ref[...] = r.astype(out_dtype)

        if nk == 1:
            finish(p)
        else:
            acc_ref = refs[-1]
            kk = pl.program_id(2)

            @pl.when(kk == 0)
            def _():
                acc_ref[...] = p

            @pl.when(kk > 0)
            def _():
                acc_ref[...] += p

            @pl.when(kk == nk - 1)
            def _():
                finish(acc_ref[...])

    a_spec = pl.BlockSpec((tk, tm), lambda i, j, kk: (kk, i)) if ta else pl.BlockSpec((tm, tk), lambda i, j, kk: (i, kk))
    if tb:
        b_spec = pl.BlockSpec((tn, tk), lambda i, j, kk: (j + n0, kk + k0))
    else:
        b_spec = pl.BlockSpec((tk, tn), lambda i, j, kk: (kk + k0, j + n0))
    in_specs = [a_spec, b_spec] + ([pl.BlockSpec((tm, tn), lambda i, j, kk: (i, j))] if has_add else [])
    args = (a, b) + ((add,) if has_add else ())
    if out_into is not None:
        in_specs.append(pl.BlockSpec(memory_space=pl.ANY))
        args += (out_into,)
    return pl.pallas_call(
        body, name=name, grid=(m // tm, n // tn, nk), in_specs=in_specs,
        out_specs=pl.BlockSpec((tm, tn), lambda i, j, kk: (i, j + o0)),
        out_shape=jax.ShapeDtypeStruct((m, n_out), out_dtype),
        scratch_shapes=[pltpu.VMEM((tm, tn), F32)] if nk > 1 else [],
        input_output_aliases={n_in - 1: 0} if out_into is not None else {},
        compiler_params=_params("parallel", "parallel", "arbitrary"),
    )(*args)


def _rms_fwd(x, g, name):
    s, d = x.shape
    tr = _tile(s, (512, 256, 128))

    def body(x_ref, g_ref, h_ref):
        xv = x_ref[...]
        r = lax.rsqrt(jnp.mean(xv * xv, axis=-1, keepdims=True) + EPS)
        h_ref[...] = ((xv * r) * g_ref[...]).astype(BF16)

    return pl.pallas_call(
        body, name=name, grid=(s // tr,),
        in_specs=[pl.BlockSpec((tr, d), lambda i: (i, 0)), pl.BlockSpec((1, d), lambda i: (0, 0))],
        out_specs=pl.BlockSpec((tr, d), lambda i: (i, 0)),
        out_shape=jax.ShapeDtypeStruct((s, d), BF16), compiler_params=_params("parallel"),
    )(x, g.reshape(1, d))


def _rms_bwd(x, g, dh, dres, name):
    s, d = x.shape
    tr = _tile(s, (512, 256, 128))

    def body(x_ref, g_ref, dh_ref, dres_ref, dx_ref, dg_ref):
        xv = x_ref[...]
        r = lax.rsqrt(jnp.mean(xv * xv, axis=-1, keepdims=True) + EPS)
        dhv = dh_ref[...]
        u = dhv * g_ref[...]
        dx_ref[...] = dres_ref[...] + (r * u - xv * (r * r * r) * jnp.mean(xv * u, axis=-1, keepdims=True))
        part = jnp.sum(dhv * (xv * r), axis=0, keepdims=True)

        @pl.when(pl.program_id(0) == 0)
        def _():
            dg_ref[...] = part

        @pl.when(pl.program_id(0) > 0)
        def _():
            dg_ref[...] += part

    row = pl.BlockSpec((tr, d), lambda i: (i, 0))
    vec = pl.BlockSpec((1, d), lambda i: (0, 0))
    return pl.pallas_call(
        body, name=name, grid=(s // tr,), in_specs=[row, vec, row, row], out_specs=[row, vec],
        out_shape=[jax.ShapeDtypeStruct((s, d), F32), jax.ShapeDtypeStruct((1, d), F32)],
        compiler_params=_params("arbitrary"),
    )(x, g.reshape(1, d), dh, dres)


def _loss_head(y, target):
    s, d = y.shape
    tr = _tile(s, (512, 256, 128))

    def body(y_ref, t_ref, dy_ref, l_ref):
        err = y_ref[...] - t_ref[...]
        dy_ref[...] = err * (1.0 / d)
        part = jnp.sum(jnp.sum(err * err, axis=-1, keepdims=True) * (1.0 / d), axis=0, keepdims=True) * 0.5
        part = jnp.broadcast_to(part, (8, LANES))

        @pl.when(pl.program_id(0) == 0)
        def _():
            l_ref[...] = part

        @pl.when(pl.program_id(0) > 0)
        def _():
            l_ref[...] += part

    row = pl.BlockSpec((tr, d), lambda i: (i, 0))
    acc = pl.BlockSpec((8, LANES), lambda i: (0, 0))
    return pl.pallas_call(
        body, name="loss_head", grid=(s // tr,), in_specs=[row, row], out_specs=[row, acc],
        out_shape=[jax.ShapeDtypeStruct((s, d), F32), jax.ShapeDtypeStruct((8, LANES), F32)],
        compiler_params=_params("arbitrary"),
    )(y, target)


def _pool_lane_select(lane, v2, v4, v8, v16):
    return jnp.where(lane < 64, v2, jnp.where(lane < 128, v4, jnp.where(lane < 192, v8, v16)))


def _pool_diff(xc, xp, row0):
    n = BLOCK + HALO_POOL
    cat = jnp.concatenate([xp, xc], axis=0)
    s2 = cat + pltpu.roll(cat, 1, 0)
    s4 = s2 + pltpu.roll(s2, 2, 0)
    s8 = s4 + pltpu.roll(s4, 4, 0)
    s16 = s8 + pltpu.roll(s8, 8, 0)
    lane = lax.broadcasted_iota(jnp.int32, (n, POOL_W), 1)
    wsum = _pool_lane_select(lane, s2, s4, s8, s16)[HALO_POOL:]
    return wsum / _pool_count(row0, BLOCK) - xc


def _pool_count(row0, rows):
    lane = lax.broadcasted_iota(jnp.int32, (rows, POOL_W), 1)
    t = lax.broadcasted_iota(jnp.int32, (rows, POOL_W), 0) + row0
    return jnp.minimum(t + 1, _pool_lane_select(lane, 2, 4, 8, 16)).astype(F32)


def _pool_fwd(z, bdw, scale, name):
    s = z.shape[0]
    nb = s // BLOCK

    def body(xc_ref, xp_ref, w_ref, sc_ref, a_ref):
        i = pl.program_id(0)
        xp = jnp.where(i > 0, xp_ref[...], 0.0)
        diff = _pool_diff(xc_ref[...], xp, i * BLOCK)
        a_ref[...] = (_dot(diff, w_ref[...]) * sc_ref[...]).astype(BF16)

    return pl.pallas_call(
        body, name=name, grid=(nb,),
        in_specs=[pl.BlockSpec((BLOCK, POOL_W), lambda i: (i, 0)),
                  pl.BlockSpec((HALO_POOL, POOL_W), lambda i: (jnp.maximum(i * (BLOCK // HALO_POOL) - 1, 0), 0)),
                  pl.BlockSpec((POOL_W, POOL_W), lambda i: (0, 0)),
                  pl.BlockSpec((1, POOL_W), lambda i: (0, 0))],
        out_specs=pl.BlockSpec((BLOCK, POOL_W), lambda i: (i, 0)),
        out_shape=jax.ShapeDtypeStruct((s, POOL_W), BF16), compiler_params=_params("parallel"),
    )(z, z, bdw, scale.reshape(1, POOL_W))


def _pool_bwd(z, da, bdw, scale, name):
    s = z.shape[0]
    nb = s // BLOCK
    per = BLOCK // HALO_POOL
    n = BLOCK + HALO_POOL

    def body(xc_ref, xp_ref, dac_ref, dan_ref, w_ref, sc_ref, dx_ref, dw_ref, dsc_ref):
        i = pl.program_id(0)
        xp = jnp.where(i > 0, xp_ref[...], 0.0)
        diff = _pool_diff(xc_ref[...], xp, i * BLOCK)
        mixed = _dot(diff, w_ref[...])
        dac = dac_ref[...]
        dan = jnp.where(i < nb - 1, dan_ref[...], 0.0)
        dmix = jnp.concatenate([dac, dan], axis=0) * sc_ref[...]
        ddiff = _dot(dmix, w_ref[...], 1, 1)
        e = ddiff / _pool_count(i * BLOCK, n)
        f2 = e + pltpu.roll(e, n - 1, 0)
        f4 = f2 + pltpu.roll(f2, n - 2, 0)
        f8 = f4 + pltpu.roll(f4, n - 4, 0)
        f16 = f8 + pltpu.roll(f8, n - 8, 0)
        lane = lax.broadcasted_iota(jnp.int32, (n, POOL_W), 1)
        back = _pool_lane_select(lane, f2, f4, f8, f16)
        dx_ref[...] = (back[:BLOCK] - ddiff[:BLOCK]).astype(BF16)
        dw = _dot(diff, dmix[:BLOCK], 0, 0)
        dsc = jnp.sum(dac * mixed, axis=0, keepdims=True)

        @pl.when(i == 0)
        def _():
            dw_ref[...] = dw
            dsc_ref[...] = dsc

        @pl.when(i > 0)
        def _():
            dw_ref[...] += dw
            dsc_ref[...] += dsc

    blk = pl.BlockSpec((BLOCK, POOL_W), lambda i: (i, 0))
    return pl.pallas_call(
        body, name=name, grid=(nb,),
        in_specs=[blk, pl.BlockSpec((HALO_POOL, POOL_W), lambda i: (jnp.maximum(i * per - 1, 0), 0)),
                  blk, pl.BlockSpec((HALO_POOL, POOL_W), lambda i: (jnp.minimum((i + 1) * per, nb * per - 1), 0)),
                  pl.BlockSpec((POOL_W, POOL_W), lambda i: (0, 0)), pl.BlockSpec((1, POOL_W), lambda i: (0, 0))],
        out_specs=[blk, pl.BlockSpec((POOL_W, POOL_W), lambda i: (0, 0)), pl.BlockSpec((1, POOL_W), lambda i: (0, 0))],
        out_shape=[jax.ShapeDtypeStruct((s, POOL_W), BF16), jax.ShapeDtypeStruct((POOL_W, POOL_W), F32),
                   jax.ShapeDtypeStruct((1, POOL_W), F32)],
        compiler_params=_params("arbitrary"),
    )(z, z, da, da, bdw, scale.reshape(1, POOL_W))


def _attn_setup(zc_ref, zp_ref, cc_ref, cp_ref, sc_ref, sp_ref, qn_ref, kn_ref, bd):
    q = []
    for j in range(ATTN_W // LANES):
        t = zc_ref[:, POOL_W + j * LANES:POOL_W + (j + 1) * LANES]
        q.append((t,) + _norm_rope(t, qn_ref[...], cc_ref[...], sc_ref[...], bd))
    kc_raw = zc_ref[:, POOL_W + ATTN_W:POOL_W + ATTN_W + KV_W]
    kc = _norm_rope(kc_raw, kn_ref[...], cc_ref[...], sc_ref[...], bd)[0]
    kp = _norm_rope(zp_ref[:, :KV_W], kn_ref[...], cp_ref[...], sp_ref[...], bd)[0]
    kband = jnp.concatenate([kp, kc], axis=0).astype(BF16)
    vband = jnp.concatenate([zp_ref[:, KV_W:], zc_ref[:, POOL_W + ATTN_W + KV_W:POOL_W + ATTN_W + 2 * KV_W]], axis=0).astype(BF16)
    return q, kband, vband


def _attn_mask(i):
    row = lax.broadcasted_iota(jnp.int32, (BLOCK, 2 * BLOCK), 0)
    col = lax.broadcasted_iota(jnp.int32, (BLOCK, 2 * BLOCK), 1)
    dist = row + BLOCK - col
    return (dist >= 0) & (dist < BLOCK) & ((col >= BLOCK) | (i > 0))


def _to_kv_lanes(t, h):
    kvh = h // Q_PER_KV
    if (h % 2) != kvh:
        t = pltpu.roll(t, HEAD_DIM, 1)
    lane = lax.broadcasted_iota(jnp.int32, t.shape, 1)
    return jnp.where((lane >= HEAD_DIM) == (kvh == 1), t, 0.0)


def _from_kv_lanes(t, h):
    kvh = h // Q_PER_KV
    lane = lax.broadcasted_iota(jnp.int32, t.shape, 1)
    t = jnp.where((lane >= HEAD_DIM) == (kvh == 1), t, 0.0)
    if (h % 2) != kvh:
        t = pltpu.roll(t, HEAD_DIM, 1)
    return t


def _attn_probs(qh, kband, mask, sink):
    sc = _dot(qh, kband, 1, 1) * (HEAD_DIM ** -0.5)
    sc = jnp.where(mask, sc, -1e30)
    m = jnp.maximum(jnp.max(sc, axis=1, keepdims=True), sink)
    p = jnp.exp(sc - m)
    psink = jnp.exp(sink - m)
    den = jnp.sum(p, axis=1, keepdims=True) + psink
    return p / den, psink / den


def _attn_specs(nb):
    cur = lambda i: (i, 0)
    prev = lambda i: (jnp.maximum(i - 1, 0), 0)
    tab = lambda f: pl.BlockSpec((BLOCK, LANES), f)
    vec = pl.BlockSpec((1, LANES), lambda i: (0, 0))
    return [pl.BlockSpec((BLOCK, 1024), cur),
            pl.BlockSpec((BLOCK, 2 * KV_W), lambda i: (jnp.maximum(i - 1, 0), 3)),
            tab(cur), tab(prev), tab(cur), tab(prev), vec, vec,
            pl.BlockSpec(memory_space=pltpu.SMEM)]


def _attn_fwd(z, cosf, sinf, qn, kn, sinks, name):
    s = z.shape[0]
    nb = s // BLOCK

    def body(zc_ref, zp_ref, cc_ref, cp_ref, sc_ref, sp_ref, qn_ref, kn_ref, sink_ref, o_ref):
        i = pl.program_id(0)
        bd = _head_mean_matrix()
        q, kband, vband = _attn_setup(zc_ref, zp_ref, cc_ref, cp_ref, sc_ref, sp_ref, qn_ref, kn_ref, bd)
        mask = _attn_mask(i)
        out = [None] * (ATTN_W // LANES)
        for h in range(N_Q_HEADS):
            qh = _to_kv_lanes(q[h // 2][1], h)
            probs, _ = _attn_probs(qh, kband, mask, sink_ref[h])
            o = _from_kv_lanes(_dot(probs, vband), h)
            out[h // 2] = o if out[h // 2] is None else out[h // 2] + o
        for j, o in enumerate(out):
            o_ref[:, j * LANES:(j + 1) * LANES] = o.astype(BF16)

    return pl.pallas_call(
        body, name=name, grid=(nb,), in_specs=_attn_specs(nb),
        out_specs=pl.BlockSpec((BLOCK, ATTN_W), lambda i: (i, 0)),
        out_shape=jax.ShapeDtypeStruct((s, ATTN_W), BF16), compiler_params=_params("parallel"),
    )(z, z, cosf, cosf, sinf, sinf, qn, kn, sinks)


def _attn_bwd(z, cosf, sinf, qn, kn, sinks, d_out, name):
    s = z.shape[0]
    nb = s // BLOCK
    nt = ATTN_W // LANES

    def body(zc_ref, zp_ref, cc_ref, cp_ref, sc_ref, sp_ref, qn_ref, kn_ref, sink_ref, do_ref,
             dq_ref, dkc_ref, dkp_ref, dvc_ref, dvp_ref, dqn_ref, dsink_ref):
        i = pl.program_id(0)
        bd = _head_mean_matrix()
        q, kband, vband = _attn_setup(zc_ref, zp_ref, cc_ref, cp_ref, sc_ref, sp_ref, qn_ref, kn_ref, bd)
        mask = _attn_mask(i)

        @pl.when(i == 0)
        def _():
            dqn_ref[...] = jnp.zeros_like(dqn_ref)
            dsink_ref[...] = jnp.zeros_like(dsink_ref)

        dq = [None] * nt
        dk = jnp.zeros((2 * BLOCK, KV_W), F32)
        dv = jnp.zeros((2 * BLOCK, KV_W), F32)
        for h in range(N_Q_HEADS):
            qh = _to_kv_lanes(q[h // 2][1], h)
            probs, psink = _attn_probs(qh, kband, mask, sink_ref[h])
            doh = _to_kv_lanes(do_ref[:, (h // 2) * LANES:(h // 2 + 1) * LANES], h)
            dp = _dot(doh, vband, 1, 1)
            delta = jnp.sum(dp * probs, axis=1, keepdims=True)
            ds = (probs * (dp - delta)) * (HEAD_DIM ** -0.5)
            dsink = jnp.sum(-psink * delta, axis=0, keepdims=True)
            dsink_ref[h:h + 1, :] += jnp.broadcast_to(dsink, (1, LANES))
            t = _from_kv_lanes(_dot(ds, kband), h)
            dq[h // 2] = t if dq[h // 2] is None else dq[h // 2] + t
            dk = dk + _dot(ds, qh, 0, 0)
            dv = dv + _dot(probs, doh, 0, 0)
        dgn = jnp.zeros((1, LANES), F32)
        for j in range(nt):
            t, _, n, r = q[j]
            dt, g = _norm_rope_bwd(dq[j], t, n, r, qn_ref[...], cc_ref[...], sc_ref[...], bd)
            dq_ref[:, j * LANES:(j + 1) * LANES] = dt.astype(BF16)
            dgn = dgn + g
        dqn_ref[...] += jnp.broadcast_to(dgn, (8, LANES))
        dkp_ref[...] = dk[:BLOCK]
        dkc_ref[...] = dk[BLOCK:]
        dvp_ref[...] = dv[:BLOCK]
        dvc_ref[...] = dv[BLOCK:]

        @pl.when(i == nb - 1)
        def _():
            acc = dqn_ref[...]
            dqn_ref[...] = acc + pltpu.roll(acc, HEAD_DIM, 1)

    blk = lambda w: pl.BlockSpec((BLOCK, w), lambda i: (i, 0))
    acc = pl.BlockSpec((8, LANES), lambda i: (0, 0))
    kv = jax.ShapeDtypeStruct((s, KV_W), F32)
    return pl.pallas_call(
        body, name=name, grid=(nb,), in_specs=_attn_specs(nb) + [blk(ATTN_W)],
        out_specs=[blk(ATTN_W), blk(KV_W), blk(KV_W), blk(KV_W), blk(KV_W), acc, acc],
        out_shape=[jax.ShapeDtypeStruct((s, ATTN_W), BF16), kv, kv, kv, kv,
                   jax.ShapeDtypeStruct((8, LANES), F32), jax.ShapeDtypeStruct((8, LANES), F32)],
        compiler_params=_params("arbitrary"),
    )(z, z, cosf, cosf, sinf, sinf, qn, kn, sinks, d_out)


def _kv_post(z, cosf, sinf, kn, dkc, dkp, dvc, dvp, dxp, dq, duv, dz, name):
    s = z.shape[0]
    nb = s // BLOCK

    def body(zk_ref, c_ref, s_ref, kn_ref, dkc_ref, dkp_ref, dvc_ref, dvp_ref, dxp_ref, dq_ref, duv_ref, dz_in,
             dz_ref, dkn_ref):
        j = pl.program_id(0)
        bd = _head_mean_matrix()
        last = j == nb - 1
        d = dkc_ref[...] + jnp.where(last, 0.0, dkp_ref[...])
        t = zk_ref[:, :KV_W]
        _, n, r = _norm_rope(t, kn_ref[...], c_ref[...], s_ref[...], bd)
        dt, g = _norm_rope_bwd(d, t, n, r, kn_ref[...], c_ref[...], s_ref[...], bd)
        dvv = dvc_ref[...] + jnp.where(last, 0.0, dvp_ref[...])
        dz_ref[:, 0:POOL_W] = dxp_ref[...]
        dz_ref[:, POOL_W:POOL_W + ATTN_W] = dq_ref[...]
        dz_ref[:, POOL_W + ATTN_W:POOL_W + ATTN_W + KV_W] = dt.astype(BF16)
        dz_ref[:, POOL_W + ATTN_W + KV_W:POOL_W + ATTN_W + 2 * KV_W] = dvv.astype(BF16)
        dz_ref[:, POOL_W + ATTN_W + 2 * KV_W:GATE_COL0] = duv_ref[...]

        @pl.when(j == 0)
        def _():
            dkn_ref[...] = jnp.zeros_like(dkn_ref)

        dkn_ref[...] += jnp.broadcast_to(g, (8, LANES))

        @pl.when(last)
        def _():
            acc = dkn_ref[...]
            dkn_ref[...] = acc + pltpu.roll(acc, HEAD_DIM, 1)

    cur = lambda w: pl.BlockSpec((BLOCK, w), lambda j: (j, 0))
    nxt = pl.BlockSpec((BLOCK, KV_W), lambda j: (jnp.minimum(j + 1, nb - 1), 0))
    vec = pl.BlockSpec((1, LANES), lambda j: (0, 0))
    return pl.pallas_call(
        body, name=name, grid=(nb,),
        in_specs=[pl.BlockSpec((BLOCK, 2 * KV_W), lambda j: (j, 3)), cur(LANES), cur(LANES), vec,
                  cur(KV_W), nxt, cur(KV_W), nxt, cur(POOL_W), cur(ATTN_W), cur(2 * SGU_W),
                  pl.BlockSpec(memory_space=pl.ANY)],
        out_specs=[pl.BlockSpec((BLOCK, GATE_COL0), lambda j: (j, 0)), pl.BlockSpec((8, LANES), lambda j: (0, 0))],
        out_shape=[jax.ShapeDtypeStruct(dz.shape, dz.dtype), jax.ShapeDtypeStruct((8, LANES), F32)],
        input_output_aliases={11: 0}, compiler_params=_params("arbitrary"),
    )(z, cosf, sinf, kn, dkc, dkp, dvc, dvp, dxp, dq, duv, dz)


def _sgu_setup(z_ref, ws_ref, vn_ref, bd):
    us = z_ref[:, :SGU_W]
    vs = z_ref[:, SGU_W:]
    ug, tu = _gelu(us)
    gv, tv = _gelu(vs)
    rr = jnp.concatenate([lax.rsqrt(_head_mean(gv[:, k * LANES:(k + 1) * LANES] ** 2, bd) + EPS) for k in range(2)], axis=1)
    vg = (gv * rr) * vn_ref[...]
    tril = lax.broadcasted_iota(jnp.int32, (BLOCK, BLOCK), 0) >= lax.broadcasted_iota(jnp.int32, (BLOCK, BLOCK), 1)
    w = [jnp.where(tril, ws_ref[g], 0.0).astype(BF16) for g in range(4)]
    return us, vs, ug, tu, gv, tv, rr, vg, w, tril


def _group_select(parts):
    lane = lax.broadcasted_iota(jnp.int32, parts[0].shape, 1)
    return _pool_lane_select(lane, *parts)


def _sgu_fwd(z, ws, bcol, vn, name):
    s = z.shape[0]
    nb = s // BLOCK

    def body(z_ref, ws_ref, b_ref, vn_ref, c_ref):
        bd = _head_mean_matrix()
        _, _, ug, _, _, _, _, vg, w, _ = _sgu_setup(z_ref, ws_ref, vn_ref, bd)
        sg = _group_select([_dot(w[g], vg) for g in range(4)]) + b_ref[...]
        c_ref[...] = (ug * sg).astype(BF16)

    return pl.pallas_call(
        body, name=name, grid=(nb,),
        in_specs=[pl.BlockSpec((BLOCK, 2 * SGU_W), lambda i: (i, 2)), pl.BlockSpec((4, BLOCK, BLOCK), lambda i: (0, 0, 0)),
                  pl.BlockSpec((BLOCK, SGU_W), lambda i: (0, 0)), pl.BlockSpec((1, SGU_W), lambda i: (0, 0))],
        out_specs=pl.BlockSpec((BLOCK, SGU_W), lambda i: (i, 0)),
        out_shape=jax.ShapeDtypeStruct((s, SGU_W), BF16), compiler_params=_params("parallel"),
    )(z, ws, bcol, vn)


def _sgu_bwd(z, ws, bcol, vn, dc, name):
    s = z.shape[0]
    nb = s // BLOCK

    def body(z_ref, ws_ref, b_ref, vn_ref, dc_ref, duv_ref, dws_ref, db_ref, dvn_ref):
        i = pl.program_id(0)
        bd = _head_mean_matrix()
        us, vs, ug, tu, gv, tv, rr, vg, w, tril = _sgu_setup(z_ref, ws_ref, vn_ref, bd)
        sg = _group_select([_dot(w[g], vg) for g in range(4)]) + b_ref[...]
        dcv = dc_ref[...]
        dug = dcv * sg
        dsg = dcv * ug
        lane = lax.broadcasted_iota(jnp.int32, dsg.shape, 1)

        @pl.when(i == 0)
        def _():
            dws_ref[...] = jnp.zeros_like(dws_ref)
            db_ref[...] = jnp.zeros_like(db_ref)
            dvn_ref[...] = jnp.zeros_like(dvn_ref)

        for g in range(4):
            dsg_g = jnp.where((lane >= g * HEAD_DIM) & (lane < (g + 1) * HEAD_DIM), dsg, 0.0)
            dws_ref[g] += jnp.where(tril, _dot(dsg_g, vg, 1, 1), 0.0)
        dvg = _group_select([_dot(w[g], dsg, 0, 0) for g in range(4)])
        db_ref[...] += dsg
        n = gv * rr
        part = jnp.sum(dvg * n, axis=0, keepdims=True)
        dvn_ref[...] += jnp.broadcast_to(part[:, :LANES] + part[:, LANES:], (8, LANES))
        u = dvg * vn_ref[...]
        tu_ = gv * u
        mean = jnp.concatenate([_head_mean(tu_[:, k * LANES:(k + 1) * LANES], bd) for k in range(2)], axis=1)
        dgv = rr * u - gv * (rr * rr * rr) * mean
        duv_ref[:, :SGU_W] = (dug * _gelu_grad(us, tu)).astype(BF16)
        duv_ref[:, SGU_W:] = (dgv * _gelu_grad(vs, tv)).astype(BF16)

        @pl.when(i == nb - 1)
        def _():
            acc = dvn_ref[...]
            dvn_ref[...] = acc + pltpu.roll(acc, HEAD_DIM, 1)
            for k in range(2):
                db_ref[:, k * LANES:(k + 1) * LANES] = _head_mean(db_ref[:, k * LANES:(k + 1) * LANES], bd) * float(HEAD_DIM)

    return pl.pallas_call(
        body, name=name, grid=(nb,),
        in_specs=[pl.BlockSpec((BLOCK, 2 * SGU_W), lambda i: (i, 2)), pl.BlockSpec((4, BLOCK, BLOCK), lambda i: (0, 0, 0)),
                  pl.BlockSpec((BLOCK, SGU_W), lambda i: (0, 0)), pl.BlockSpec((1, SGU_W), lambda i: (0, 0)),
                  pl.BlockSpec((BLOCK, SGU_W), lambda i: (i, 0))],
        out_specs=[pl.BlockSpec((BLOCK, 2 * SGU_W), lambda i: (i, 0)), pl.BlockSpec((4, BLOCK, BLOCK), lambda i: (0, 0, 0)),
                   pl.BlockSpec((BLOCK, SGU_W), lambda i: (0, 0)), pl.BlockSpec((8, LANES), lambda i: (0, 0))],
        out_shape=[jax.ShapeDtypeStruct((s, 2 * SGU_W), BF16), jax.ShapeDtypeStruct((4, BLOCK, BLOCK), F32),
                   jax.ShapeDtypeStruct((BLOCK, SGU_W), F32), jax.ShapeDtypeStruct((8, LANES), F32)],
        compiler_params=_params("arbitrary"),
    )(z, ws, bcol, vn, dc)


MERGE_TN = 512
MERGE_TM = 256


def _merge_fwd(a, b, c, wpa, wpb, wpc, z, name):
    s = z.shape[0]
    tm = _tile(s, (MERGE_TM, BLOCK))
    gate0 = GATE_COL0 // MERGE_TN

    def body(a_ref, b_ref, c_ref, wa_ref, wb_ref, wc_ref, g0_ref, g1_ref, g2_ref, o_ref):
        r = _sigmoid(g0_ref[...]) * _dot(a_ref[...], wa_ref[...])
        r = r + _sigmoid(g1_ref[...]) * _dot(b_ref[...], wb_ref[...])
        r = r + _sigmoid(g2_ref[...]) * _dot(c_ref[...], wc_ref[...])
        o_ref[...] = r.astype(BF16)

    x_spec = lambda w: pl.BlockSpec((tm, w), lambda i, n: (i, 0))
    w_spec = lambda w: pl.BlockSpec((w, MERGE_TN), lambda i, n: (0, n))
    g_spec = lambda br: pl.BlockSpec((tm, MERGE_TN), lambda i, n: (i, gate0 + 2 * br + n))
    return pl.pallas_call(
        body, name=name, grid=(s // tm, D_MODEL // MERGE_TN),
        in_specs=[x_spec(POOL_W), x_spec(ATTN_W), x_spec(SGU_W), w_spec(POOL_W), w_spec(ATTN_W), w_spec(SGU_W),
                  g_spec(0), g_spec(1), g_spec(2)],
        out_specs=pl.BlockSpec((tm, MERGE_TN), lambda i, n: (i, n)),
        out_shape=jax.ShapeDtypeStruct((s, D_MODEL), BF16), compiler_params=_params("parallel", "parallel"),
    )(a, b, c, wpa, wpb, wpc, z, z, z)


def _branch_bwd(br, xb, wp, z, dm, dz, name):
    s = z.shape[0]
    kb = xb.shape[1]
    tm = _tile(s, (MERGE_TM, BLOCK))
    gate0 = GATE_COL0 // MERGE_TN
    aliased = dz is not None

    def body(*refs):
        x_ref, w_ref, g_ref, dm_ref = refs[:4]
        dz_ref, dy_ref, dx_ref = refs[-3:]
        n = pl.program_id(1)
        y = _dot(x_ref[...], w_ref[...])
        sg = _sigmoid(g_ref[...])
        dmv = dm_ref[...]
        dy = (dmv * sg).astype(BF16)
        dy_ref[...] = dy
        dz_ref[...] = ((dmv * y) * (sg * (1.0 - sg))).astype(BF16)
        dx = _dot(dy, w_ref[...], 1, 1)

        @pl.when(n == 0)
        def _():
            dx_ref[...] = dx

        @pl.when(n > 0)
        def _():
            dx_ref[...] += dx

    in_specs = [pl.BlockSpec((tm, kb), lambda i, n: (i, 0)), pl.BlockSpec((kb, MERGE_TN), lambda i, n: (0, n)),
                pl.BlockSpec((tm, MERGE_TN), lambda i, n: (i, gate0 + 2 * br + n)),
                pl.BlockSpec((tm, MERGE_TN), lambda i, n: (i, n))]
    args = [xb, wp, z, dm]
    if aliased:
        in_specs.append(pl.BlockSpec(memory_space=pl.ANY))
        args.append(dz)
    return pl.pallas_call(
        body, name=name, grid=(s // tm, D_MODEL // MERGE_TN), in_specs=in_specs,
        out_specs=[pl.BlockSpec((tm, MERGE_TN), lambda i, n: (i, gate0 + 2 * br + n)),
                   pl.BlockSpec((tm, MERGE_TN), lambda i, n: (i, n)),
                   pl.BlockSpec((tm, kb), lambda i, n: (i, 0))],
        out_shape=[jax.ShapeDtypeStruct((s, IN_COLS), BF16), jax.ShapeDtypeStruct((s, D_MODEL), BF16),
                   jax.ShapeDtypeStruct((s, kb), F32)],
        input_output_aliases={4: 0} if aliased else {},
        compiler_params=_params("parallel", "arbitrary"),
    )(*args)


FFN_TM = 256
FFN_TC = 1408


def _conv3(cur, prev, w_ref, b_ref):
    cat = jnp.concatenate([prev, cur], axis=0)
    x1 = pltpu.roll(cat, 1, 0)[HALO_CONV:]
    x2 = pltpu.roll(cat, 2, 0)[HALO_CONV:]
    return w_ref[0:1, :] * x2 + w_ref[1:2, :] * x1 + w_ref[2:3, :] * cur + b_ref[...], x1, x2


def _ffn_specs(s, tm, rows_first):
    per = tm // HALO_CONV
    if rows_first:
        cur = pl.BlockSpec((tm, FFN_TC), lambda i, j: (i, j))
        prev = pl.BlockSpec((HALO_CONV, FFN_TC), lambda i, j: (jnp.maximum(i * per - 1, 0), j))
        w = pl.BlockSpec((3, FFN_TC), lambda i, j: (0, j))
        b = pl.BlockSpec((1, FFN_TC), lambda i, j: (0, j))
    else:
        cur = pl.BlockSpec((tm, FFN_TC), lambda j, i: (i, j))
        prev = pl.BlockSpec((HALO_CONV, FFN_TC), lambda j, i: (jnp.maximum(i * per - 1, 0), j))
        w = pl.BlockSpec((3, FFN_TC), lambda j, i: (0, j))
        b = pl.BlockSpec((1, FFN_TC), lambda j, i: (0, j))
    return cur, prev, w, b


def _ffn_act_fwd(upg, upv, cwg, cwv, cbg, cbv, name):
    s = upg.shape[0]
    tm = _tile(s, (FFN_TM, BLOCK))
    cur, prev, w, b = _ffn_specs(s, tm, True)

    def body(g_ref, gp_ref, v_ref, vp_ref, wg_ref, wv_ref, bg_ref, bv_ref, o_ref):
        first = pl.program_id(0) == 0
        gate = _conv3(g_ref[...], jnp.where(first, 0.0, gp_ref[...]), wg_ref, bg_ref)[0]
        val = _conv3(v_ref[...], jnp.where(first, 0.0, vp_ref[...]), wv_ref, bv_ref)[0]
        o_ref[...] = ((gate * _sigmoid(gate)) * val).astype(BF16)

    return pl.pallas_call(
        body, name=name, grid=(s // tm, D_FF // FFN_TC), in_specs=[cur, prev, cur, prev, w, w, b, b], out_specs=cur,
        out_shape=jax.ShapeDtypeStruct((s, D_FF), BF16), compiler_params=_params("parallel", "parallel"),
    )(upg, upg, upv, upv, cwg, cwv, cbg, cbv)


def _ffn_act_bwd(upg, upv, cwg, cwv, cbg, cbv, dact, name):
    s = upg.shape[0]
    tm = _tile(s, (FFN_TM, BLOCK))
    cur, prev, w, b = _ffn_specs(s, tm, False)

    def body(g_ref, gp_ref, v_ref, vp_ref, wg_ref, wv_ref, bg_ref, bv_ref, da_ref, dg_ref, dv_ref, dwg_ref, dwv_ref):
        i = pl.program_id(1)
        first = i == 0
        gc, vc = g_ref[...], v_ref[...]
        gate, g1, g2 = _conv3(gc, jnp.where(first, 0.0, gp_ref[...]), wg_ref, bg_ref)
        val, v1, v2 = _conv3(vc, jnp.where(first, 0.0, vp_ref[...]), wv_ref, bv_ref)
        sg = _sigmoid(gate)
        da = da_ref[...]
        dgate = (da * val) * (sg * (1.0 + gate * (1.0 - sg)))
        dval = da * (gate * sg)
        dg_ref[...] = dgate
        dv_ref[...] = dval

        @pl.when(first)
        def _():
            dwg_ref[...] = jnp.zeros_like(dwg_ref)
            dwv_ref[...] = jnp.zeros_like(dwv_ref)

        for ref, d, x0, x1, x2 in ((dwg_ref, dgate, gc, g1, g2), (dwv_ref, dval, vc, v1, v2)):
            ref[0:1, :] += jnp.sum(d * x2, axis=0, keepdims=True)
            ref[1:2, :] += jnp.sum(d * x1, axis=0, keepdims=True)
            ref[2:3, :] += jnp.sum(d * x0, axis=0, keepdims=True)
            ref[3:4, :] += jnp.sum(d, axis=0, keepdims=True)

    acc = pl.BlockSpec((8, FFN_TC), lambda j, i: (0, j))
    full = jax.ShapeDtypeStruct((s, D_FF), F32)
    accs = jax.ShapeDtypeStruct((8, D_FF), F32)
    return pl.pallas_call(
        body, name=name, grid=(D_FF // FFN_TC, s // tm), in_specs=[cur, prev, cur, prev, w, w, b, b, cur],
        out_specs=[cur, cur, acc, acc], out_shape=[full, full, accs, accs],
        compiler_params=_params("parallel", "arbitrary"),
    )(upg, upg, upv, upv, cwg, cwv, cbg, cbv, dact)


def _conv3_t(d, cw, name):
    s = d.shape[0]
    tm = _tile(s, (FFN_TM, BLOCK))
    per = tm // HALO_CONV
    nrow = s // tm
    n = tm + HALO_CONV

    def body(d_ref, dn_ref, w_ref, o_ref):
        nxt = jnp.where(pl.program_id(0) == nrow - 1, 0.0, dn_ref[...])
        cur = d_ref[...]
        cat = jnp.concatenate([cur, nxt], axis=0)
        d1 = pltpu.roll(cat, n - 1, 0)[:tm]
        d2 = pltpu.roll(cat, n - 2, 0)[:tm]
        o_ref[...] = (w_ref[2:3, :] * cur + w_ref[1:2, :] * d1 + w_ref[0:1, :] * d2).astype(BF16)

    cur = pl.BlockSpec((tm, FFN_TC), lambda i, j: (i, j))
    return pl.pallas_call(
        body, name=name, grid=(nrow, D_FF // FFN_TC),
        in_specs=[cur, pl.BlockSpec((HALO_CONV, FFN_TC), lambda i, j: (jnp.minimum((i + 1) * per, nrow * per - 1), j)),
                  pl.BlockSpec((3, FFN_TC), lambda i, j: (0, j))],
        out_specs=cur, out_shape=jax.ShapeDtypeStruct((s, D_FF), BF16), compiler_params=_params("parallel", "parallel"),
    )(d, d, cw)


def _mesh_place():
    return lax.axis_index("x"), lax.axis_index("y"), lax.axis_index("c")


def _all_gather(shards, name):
    na = len(shards)

    def body(*refs):
        x_refs, out_refs = refs[:na], refs[na:2 * na]
        send_sems, recv_sems, local_sems = refs[2 * na:]
        x, y, cc = _mesh_place()
        me, sibling = (x, y, cc), (x, y, 1 - cc)
        chips = [(1 - x, y), (x, 1 - y), (1 - x, 1 - y)]

        def copy(k, a, block, to, from_input=False):
            slot = out_refs[a].at[4 * block[0] + 2 * block[1] + block[2]]
            return pltpu.make_async_remote_copy(
                src_ref=x_refs[a] if from_input else slot, dst_ref=slot, send_sem=send_sems.at[k * na + a],
                recv_sem=recv_sems.at[k * na + a], device_id=to, device_id_type=pl.DeviceIdType.MESH)

        mine = [pltpu.make_async_copy(x_refs[a], out_refs[a].at[4 * x + 2 * y + cc], local_sems.at[a]) for a in range(na)]
        for cp in mine:
            cp.start()
        first = [copy(0, a, me, sibling, True) for a in range(na)]
        first += [copy(1 + j, a, me, (*chip, cc), True) for j, chip in enumerate(chips) for a in range(na)]
        for cp in first:
            cp.start()
        passed = []
        for j, chip in enumerate(chips):
            for a in range(na):
                copy(1 + j, a, (*chip, cc), me).wait_recv()
                passed.append(copy(4 + j, a, (*chip, cc), sibling))
                passed[-1].start()
        for a in range(na):
            copy(0, a, sibling, me).wait_recv()
        for j, chip in enumerate(chips):
            for a in range(na):
                copy(4 + j, a, (*chip, 1 - cc), me).wait_recv()
        for cp in first + passed:
            cp.wait_send()
        for cp in mine:
            cp.wait()

    hbm = pl.BlockSpec(memory_space=pl.ANY)
    return pl.pallas_call(
        body, name=name, out_shape=[jax.ShapeDtypeStruct((N_DEV,) + t.shape, t.dtype) for t in shards],
        in_specs=[hbm] * na, out_specs=[hbm] * na,
        scratch_shapes=[pltpu.SemaphoreType.DMA((7 * na,)), pltpu.SemaphoreType.DMA((7 * na,)), pltpu.SemaphoreType.DMA((na,))],
    )(*shards)


def _exchange(blocks, name):
    na = len(blocks)

    def body(*refs):
        g_refs, out_refs = refs[:na], refs[na:2 * na]
        send_sems, recv_sems, local_sems = refs[2 * na:]
        x, y, cc = _mesh_place()
        me = 4 * x + 2 * y + cc
        mine = [pltpu.make_async_copy(g_refs[a].at[me], out_refs[a].at[me], local_sems.at[a]) for a in range(na)]
        for cp in mine:
            cp.start()
        sends, lands = [], []
        for k in range(1, N_DEV):
            px = 1 - x if (k >> 2) & 1 else x
            py = 1 - y if (k >> 1) & 1 else y
            pc = 1 - cc if k & 1 else cc
            peer = 4 * px + 2 * py + pc
            for a in range(na):
                sem = (k - 1) * na + a
                sends.append(pltpu.make_async_remote_copy(
                    src_ref=g_refs[a].at[peer], dst_ref=out_refs[a].at[me], send_sem=send_sems.at[sem], recv_sem=recv_sems.at[sem],
                    device_id=(px, py, pc), device_id_type=pl.DeviceIdType.MESH))
                lands.append(pltpu.make_async_remote_copy(
                    src_ref=g_refs[a].at[peer], dst_ref=out_refs[a].at[peer], send_sem=send_sems.at[sem], recv_sem=recv_sems.at[sem],
                    device_id=(px, py, pc), device_id_type=pl.DeviceIdType.MESH))
        for cp in sends:
            cp.start()
        for cp in lands:
            cp.wait_recv()
        for cp in sends:
            cp.wait_send()
        for cp in mine:
            cp.wait()

    hbm = pl.BlockSpec(memory_space=pl.ANY)
    return pl.pallas_call(
        body, name=name, out_shape=[jax.ShapeDtypeStruct(t.shape, t.dtype) for t in blocks],
        in_specs=[hbm] * na, out_specs=[hbm] * na,
        scratch_shapes=[pltpu.SemaphoreType.DMA((7 * na,)), pltpu.SemaphoreType.DMA((7 * na,)), pltpu.SemaphoreType.DMA((na,))],
    )(*blocks)


def _peer(k):
    x, y, cc = _mesh_place()
    px = 1 - x if (k >> 2) & 1 else x
    py = 1 - y if (k >> 1) & 1 else y
    pc = 1 - cc if k & 1 else cc
    return (px, py, pc), 4 * px + 2 * py + pc


def _push_copy(src_ref, land_ref, k, a, na, send_sems, recv_sems, indexed, landed):
    x, y, cc = _mesh_place()
    place, peer = _peer(k)
    sem = (k - 1) * na + a
    return pltpu.make_async_remote_copy(
        src_ref=src_ref.at[peer] if indexed else src_ref, dst_ref=land_ref.at[peer if landed else 4 * x + 2 * y + cc],
        send_sem=send_sems.at[sem], recv_sem=recv_sems.at[sem], device_id=place, device_id_type=pl.DeviceIdType.MESH)


_HBM = pl.BlockSpec(memory_space=pltpu.HBM)
_SEM = pl.BlockSpec(memory_space=pltpu.SEMAPHORE)
_EFFECT = pltpu.SideEffectType.DATAFLOW_SIDE_EFFECTING


def _push_start(srcs, indexed, name):
    na = len(srcs)
    lands = [lax.empty(t.shape if indexed else (N_DEV,) + t.shape, t.dtype) for t in srcs]

    def body(*refs):
        src_refs, land_refs = refs[:na], refs[na:2 * na]
        send_sems, recv_sems = refs[2 * na], refs[2 * na + 1]
        token = refs[-1]
        for k in range(1, N_DEV):
            for a in range(na):
                _push_copy(src_refs[a], land_refs[a], k, a, na, send_sems, recv_sems, indexed, False).start()
        token[...] = jnp.zeros_like(token)

    sems = pltpu.SemaphoreType.DMA((7 * na,))
    out = pl.pallas_call(
        body, name=name,
        out_shape=(sems, sems, *[pltpu.HBM(t.shape, t.dtype) for t in srcs], *[pltpu.HBM(t.shape, t.dtype) for t in lands],
                   jax.ShapeDtypeStruct((8, LANES), F32)),
        in_specs=[_HBM] * (2 * na), out_specs=(_SEM, _SEM, *[_HBM] * (2 * na), pl.BlockSpec(memory_space=pltpu.VMEM)),
        input_output_aliases={i: 2 + i for i in range(2 * na)},
        compiler_params=pltpu.CompilerParams(has_side_effects=_EFFECT),
    )(*[pltpu.with_memory_space_constraint(t, pltpu.HBM) for t in srcs + lands])
    return out[0], out[1], list(out[2:2 + na]), list(out[2 + na:2 + 2 * na]), out[-1]


def _push_wait(started, indexed, after, name):
    send_sems, recv_sems, srcs, lands, _ = started
    na = len(srcs)

    def body(*refs):
        src_refs, land_refs = refs[:na], refs[na:2 * na]
        send_sems, recv_sems = refs[2 * na], refs[2 * na + 1]
        for k in range(1, N_DEV):
            for a in range(na):
                copy = _push_copy(src_refs[a], land_refs[a], k, a, na, send_sems, recv_sems, indexed, True)
                copy.wait_send()
                copy.wait_recv()

    out = pl.pallas_call(
        body, name=name, out_shape=[pltpu.HBM(t.shape, t.dtype) for t in srcs + lands],
        in_specs=[_HBM] * (2 * na) + [_SEM, _SEM, pl.BlockSpec(memory_space=pl.ANY)], out_specs=[_HBM] * (2 * na),
        input_output_aliases={i: i for i in range(2 * na)},
        compiler_params=pltpu.CompilerParams(has_side_effects=_EFFECT),
    )(*srcs, *lands, send_sems, recv_sems, after)
    x, y, cc = _mesh_place()
    me = 4 * x + 2 * y + cc
    return [lax.dynamic_update_index_in_dim(
        land, lax.dynamic_index_in_dim(src, me, 0, keepdims=False) if indexed else src, me, 0)
        for src, land in zip(out[:na], out[na:])]


def _adamw_sum(parts, w, m, v, name):
    _, r, c = parts.shape
    tr = _tile(r, (256, 128, 64, 32, 16, 8))

    def body(p_ref, w_ref, m_ref, v_ref, g_ref, d_ref, nm_ref, nv_ref):
        _adam_store(_sum_parts(p_ref), w_ref, m_ref, v_ref, g_ref, d_ref, nm_ref, nv_ref)

    row = pl.BlockSpec((tr, c), lambda i: (i, 0))
    shp = jax.ShapeDtypeStruct((r, c), F32)
    return pl.pallas_call(
        body, name=name, grid=(r // tr,), in_specs=[pl.BlockSpec((N_DEV, tr, c), lambda i: (0, i, 0)), row, row, row],
        out_specs=[row, row, row, row], out_shape=[shp, shp, shp, shp], compiler_params=_params("parallel"),
    )(parts, w, m, v)


def _sum_parts(p_ref):
    g = p_ref[0].astype(F32)
    for k in range(1, N_DEV):
        g = g + p_ref[k].astype(F32)
    return g


def _adam_store(g, w_ref, m_ref, v_ref, g_ref, d_ref, nm_ref, nv_ref):
    nm = ADAM_B1 * m_ref[...] + (1.0 - ADAM_B1) * g
    nv = ADAM_B2 * v_ref[...] + (1.0 - ADAM_B2) * (g * g)
    m_hat = nm / (1.0 - ADAM_B1 ** ADAM_STEP)
    v_hat = nv / (1.0 - ADAM_B2 ** ADAM_STEP)
    g_ref[...] = g
    nm_ref[...] = nm
    nv_ref[...] = nv
    d_ref[...] = -ADAM_LR * (m_hat / (jnp.sqrt(v_hat) + ADAM_EPS) + ADAM_WD * w_ref[...])


def _adamw_weight(parts, w, m, v, name):
    _, r, c = w.shape
    tr = _tile(r, (256, 128, 176))
    nr = r // tr

    def body(p0_ref, p1_ref, w_ref, m_ref, v_ref, g_ref, d_ref, nm_ref, nv_ref):
        g = jnp.where(pl.program_id(0) == 0, _sum_parts(p0_ref), _sum_parts(p1_ref))
        _adam_store(g, w_ref, m_ref, v_ref, g_ref, d_ref, nm_ref, nv_ref)

    part = lambda layer: pl.BlockSpec(
        (N_DEV, tr, c), lambda l, i: (0, jnp.where(l == layer, i, (nr - 1) * (1 - layer)), 0))
    row = pl.BlockSpec((None, tr, c), lambda l, i: (l, i, 0))
    shp = jax.ShapeDtypeStruct(w.shape, F32)
    return pl.pallas_call(
        body, name=name, grid=(DEPTH, nr), in_specs=[part(0), part(1), row, row, row],
        out_specs=[row, row, row, row], out_shape=[shp, shp, shp, shp], compiler_params=_params("arbitrary", "arbitrary"),
    )(parts[0], parts[1], w, m, v)


def _full_to_slots(name, t):
    k, n = t.shape
    if name in ROW_SHARDED:
        return t.reshape(N_DEV, k // N_DEV, n)
    return t.reshape(k, N_DEV, n // N_DEV).transpose(1, 0, 2)


def _slots_to_full(name, t):
    _, r, c = t.shape
    if name in ROW_SHARDED:
        return t.reshape(N_DEV * r, c)
    return t.transpose(1, 0, 2).reshape(r, N_DEV * c)


def _small_sizes(shapes):
    return [(n, shapes[n], -(-int(math.prod(shapes[n])) // (8 * LANES)) * 8) for n in SMALL]


def _pack_small(tree, shapes):
    rows = []
    for n, shp, nrow in _small_sizes(shapes):
        flat = tree[n].reshape(-1)
        rows.append(jnp.pad(flat, (0, nrow * LANES - flat.shape[0])).reshape(nrow, LANES))
    total = sum(r.shape[0] for r in rows)
    rows.append(jnp.zeros((-total % SMALL_ROW_TILE, LANES), F32))
    return jnp.concatenate(rows, axis=0)


def _unpack_small(buf, shapes):
    out, r0 = {}, 0
    for n, shp, nrow in _small_sizes(shapes):
        out[n] = buf[r0:r0 + nrow].reshape(-1)[:int(math.prod(shp))].reshape(shp)
        r0 += nrow
    return out


def _block_diag(w):
    g = w.shape[0]
    eye = jnp.eye(g, dtype=w.dtype)
    return (eye[:, None, :, None] * w[:, :, None, :]).reshape(g * HEAD_DIM, g * HEAD_DIM)


def kernel(x, positions, norm1, w_in, q_norm, k_norm, sinks, w_pool, pool_scale, sgu_v_norm, w_s, b_s, w_proj_a, w_proj_b, w_proj_c, w_out, norm2, w_up, conv_w, conv_b, w_down, loss_target, m_norm1, m_w_in, m_q_norm, m_k_norm, m_sinks, m_w_pool, m_pool_scale, m_sgu_v_norm, m_w_s, m_b_s, m_w_proj_a, m_w_proj_b, m_w_proj_c, m_w_out, m_norm2, m_w_up, m_conv_w, m_conv_b, m_w_down, v_norm1, v_w_in, v_q_norm, v_k_norm, v_sinks, v_w_pool, v_pool_scale, v_sgu_v_norm, v_w_s, v_b_s, v_w_proj_a, v_w_proj_b, v_w_proj_c, v_w_out, v_norm2, v_w_up, v_conv_w, v_conv_b, v_w_down):
    names = ("norm1", "w_in", "q_norm", "k_norm", "sinks", "w_pool", "pool_scale", "sgu_v_norm", "w_s", "b_s", "w_proj_a",
             "w_proj_b", "w_proj_c", "w_out", "norm2", "w_up", "conv_w", "conv_b", "w_down")
    wts = dict(zip(names, (norm1, w_in, q_norm, k_norm, sinks, w_pool, pool_scale, sgu_v_norm, w_s, b_s, w_proj_a, w_proj_b,
                           w_proj_c, w_out, norm2, w_up, conv_w, conv_b, w_down)))
    mom = dict(zip(names, (m_norm1, m_w_in, m_q_norm, m_k_norm, m_sinks, m_w_pool, m_pool_scale, m_sgu_v_norm, m_w_s, m_b_s,
                           m_w_proj_a, m_w_proj_b, m_w_proj_c, m_w_out, m_norm2, m_w_up, m_conv_w, m_conv_b, m_w_down)))
    var = dict(zip(names, (v_norm1, v_w_in, v_q_norm, v_k_norm, v_sinks, v_w_pool, v_pool_scale, v_sgu_v_norm, v_w_s, v_b_s,
                           v_w_proj_a, v_w_proj_b, v_w_proj_c, v_w_out, v_norm2, v_w_up, v_conv_w, v_conv_b, v_w_down)))
    xs = x[0]
    target = loss_target[0]
    s = xs.shape[0]

    inv_freq = ROPE_THETA ** (-jnp.arange(0, HEAD_DIM, 2, dtype=F32) / HEAD_DIM)
    ang = positions[0].astype(F32)[:, None] * inv_freq
    cosf = jnp.tile(jnp.cos(ang), (1, 4))
    sinf = jnp.tile(jnp.concatenate([-jnp.sin(ang), jnp.sin(ang)], axis=1), (1, 2))

    local = [{n: wts[n][l] if n == "conv_w" else wts[n][l].astype(BF16) for n in SHARDED} for l in range(DEPTH)]
    later = SHARDED[1:]
    full = [{"w_in": _slots_to_full("w_in", _all_gather([local[0]["w_in"]], "gather_w_in_0")[0])}, None]
    gather0 = _push_start([local[0][n] for n in later], False, "gather_rest_0_start")
    norm1_first = norm1[0] + gather0[4][0, 0]

    def layer_consts(l):
        return dict(
            bdw=_block_diag(w_pool[l]).astype(BF16), qn=jnp.tile(q_norm[l], 2).reshape(1, LANES),
            kn=jnp.tile(k_norm[l], 2).reshape(1, LANES), vn=jnp.tile(sgu_v_norm[l], 4).reshape(1, SGU_W),
            bcol=jnp.repeat(b_s[l].T, HEAD_DIM, axis=1),
            cbg=conv_b[l][:D_FF].reshape(1, D_FF), cbv=conv_b[l][D_FF:].reshape(1, D_FF))

    gate_cols, val_cols = (0, D_FF), (D_FF, D_FF)

    saved = []
    cur = xs
    for l in range(DEPTH):
        if l == 1:
            landed = _push_wait(gather1, False, cur, "gather_weights_1_wait")
            full[1] = {n: _slots_to_full(n, t) for n, t in zip(SHARDED, landed)}
        fw, k = full[l], layer_consts(l)
        h1 = _rms_fwd(cur, norm1_first if l == 0 else norm1[l], f"rms1_fwd_{l}")
        z = _mm(h1, fw["w_in"], name=f"in_proj_{l}")
        a = _pool_fwd(z, k["bdw"], pool_scale[l], f"pool_fwd_{l}")
        b = _attn_fwd(z, cosf, sinf, k["qn"], k["kn"], sinks[l], f"attn_fwd_{l}")
        c = _sgu_fwd(z, w_s[l], k["bcol"], k["vn"], f"sgu_fwd_{l}")
        w_proj_a_l = fw.get("w_proj_a")
        if l == 0:
            landed = _push_wait(gather0, False, c, "gather_rest_0_wait")
            fw.update({n: _slots_to_full(n, t) for n, t in zip(later, landed)})
            gather1 = _push_start([local[1][n] for n in SHARDED], False, "gather_weights_1_start")
            w_proj_a_l = fw["w_proj_a"] + gather1[4][0, 0].astype(BF16)
        merged = _merge_fwd(a, b, c, w_proj_a_l, fw["w_proj_b"], fw["w_proj_c"], z, f"merge_fwd_{l}")
        x1 = _mm(merged, fw["w_out"], add=cur, name=f"out_proj_{l}")
        h2 = _rms_fwd(x1, norm2[l], f"rms2_fwd_{l}")
        upg = _mm(h2, fw["w_up"], b_n=gate_cols, name=f"up_gate_{l}")
        upv = _mm(h2, fw["w_up"], b_n=val_cols, name=f"up_val_{l}")
        k["cwg"], k["cwv"] = fw["conv_w"][:, :D_FF], fw["conv_w"][:, D_FF:]
        act = _ffn_act_fwd(upg, upv, k["cwg"], k["cwv"], k["cbg"], k["cbv"], f"ffn_act_fwd_{l}")
        x2 = _mm(act, fw["w_down"], add=x1, name=f"down_proj_{l}")
        saved.append(dict(x0=cur, h1=h1, z=z, a=a, b=b, c=c, merged=merged, x1=x1, h2=h2, upg=upg, upv=upv, act=act))
        cur = x2

    dcur, loss_tile = _loss_head(cur, target)
    loss = lax.psum(loss_tile[0, 0], ("x", "y", "c"))

    gsmall = [None] * DEPTH

    def slots_of(grads):
        return [_full_to_slots(n, t) for n, t in grads.items()]

    for l in reversed(range(DEPTH)):
        fw, k, sv = full[l], layer_consts(l), saved[l]
        k["cwg"], k["cwv"] = fw["conv_w"][:, :D_FF], fw["conv_w"][:, D_FF:]
        staged = l == 0
        wgrad = functools.partial(_mm, ta=True, out_dtype=BF16)
        w_down_l = fw["w_down"] + exchange1[4][0, 0].astype(BF16) if staged else fw["w_down"]
        dact = _mm(dcur, w_down_l, tb=True, name=f"down_proj_bwd_{l}")
        g_down = wgrad(sv["act"], dcur, name=f"down_proj_wgrad_{l}")
        dgate, dval, dcg, dcv = _ffn_act_bwd(sv["upg"], sv["upv"], k["cwg"], k["cwv"], k["cbg"], k["cbv"], dact, f"ffn_act_bwd_{l}")
        dg0 = _conv3_t(dgate, k["cwg"], f"conv_t_gate_{l}")
        dv0 = _conv3_t(dval, k["cwv"], f"conv_t_val_{l}")
        dh2 = _mm(dg0, fw["w_up"], tb=True, b_k=gate_cols, name=f"up_gate_bwd_{l}")
        dh2 = _mm(dv0, fw["w_up"], tb=True, b_k=val_cols, add=dh2, name=f"up_val_bwd_{l}")
        g_up = wgrad(sv["h2"], dg0, out_cols=(0, 2 * D_FF), name=f"up_gate_wgrad_{l}")
        g_up = wgrad(sv["h2"], dv0, out_cols=(D_FF, 2 * D_FF), out_into=g_up, name=f"up_val_wgrad_{l}")
        g_ffn = dict(w_up=g_up, w_down=g_down, conv_w=jnp.concatenate([dcg[0:3], dcv[0:3]], axis=1))
        norm2_l = norm2[l]
        if staged:
            parts1 = dict(zip(SHARDED, _push_wait(exchange1, True, g_up, "exchange_grads_1_wait")))
            exchange_ffn = _push_start(slots_of(g_ffn), True, "exchange_ffn_0_start")
            norm2_l = norm2_l + exchange_ffn[4][0, 0]
        dx1, g_norm2 = _rms_bwd(sv["x1"], norm2_l, dh2, dcur, f"rms2_bwd_{l}")
        dmerged = _mm(dx1, fw["w_out"], tb=True, name=f"out_proj_bwd_{l}")
        g_out = wgrad(sv["merged"], dx1, name=f"out_proj_wgrad_{l}")
        dz, dya, da = _branch_bwd(0, sv["a"], fw["w_proj_a"], sv["z"], dmerged, None, f"branch_a_bwd_{l}")
        dz, dyb, db = _branch_bwd(1, sv["b"], fw["w_proj_b"], sv["z"], dmerged, dz, f"branch_b_bwd_{l}")
        dz, dyc, dc = _branch_bwd(2, sv["c"], fw["w_proj_c"], sv["z"], dmerged, dz, f"branch_c_bwd_{l}")
        g_mix = dict(w_proj_a=wgrad(sv["a"], dya, name=f"proj_a_wgrad_{l}"), w_proj_b=wgrad(sv["b"], dyb, name=f"proj_b_wgrad_{l}"),
                     w_proj_c=wgrad(sv["c"], dyc, name=f"proj_c_wgrad_{l}"), w_out=g_out)
        pool_scale_l = pool_scale[l]
        if staged:
            exchange_mix = _push_start(slots_of(g_mix), True, "exchange_mixer_0_start")
            pool_scale_l = pool_scale_l + exchange_mix[4][0, 0]
        dxp, g_bdw, g_pscale = _pool_bwd(sv["z"], da, k["bdw"], pool_scale_l, f"pool_bwd_{l}")
        dq, dkc, dkp, dvc, dvp, g_qn, g_sink = _attn_bwd(sv["z"], cosf, sinf, k["qn"], k["kn"], sinks[l], db, f"attn_bwd_{l}")
        duv, g_ws, g_bacc, g_vn = _sgu_bwd(sv["z"], w_s[l], k["bcol"], k["vn"], dc, f"sgu_bwd_{l}")
        dz, g_kn = _kv_post(sv["z"], cosf, sinf, k["kn"], dkc, dkp, dvc, dvp, dxp, dq, duv, dz, f"kv_post_{l}")
        g_in = dict(w_in=wgrad(sv["h1"], dz, name=f"in_proj_wgrad_{l}"))
        norm1_l = norm1[l]
        if staged:
            exchange_in = _push_start(slots_of(g_in), True, "exchange_w_in_0_start")
            norm1_l = norm1_l + exchange_in[4][0, 0]
        dh1 = _mm(dz, fw["w_in"], tb=True, name=f"in_proj_bwd_{l}")
        dcur, g_norm1 = _rms_bwd(sv["x0"], norm1_l, dh1, dx1, f"rms1_bwd_{l}")
        if not staged:
            exchange1 = _push_start(slots_of({n: {**g_in, **g_mix, **g_ffn}[n] for n in SHARDED}), True, "exchange_grads_1_start")
        gsmall[l] = dict(
            norm1=g_norm1[0], q_norm=g_qn[0, :HEAD_DIM], k_norm=g_kn[0, :HEAD_DIM], sinks=g_sink[:, 0],
            w_pool=jnp.stack([g_bdw[g * HEAD_DIM:(g + 1) * HEAD_DIM, g * HEAD_DIM:(g + 1) * HEAD_DIM] for g in range(4)]),
            pool_scale=g_pscale[0], sgu_v_norm=g_vn[0, :HEAD_DIM], w_s=g_ws, b_s=g_bacc[:, ::HEAD_DIM].T,
            norm2=g_norm2[0], conv_b=jnp.concatenate([dcg[3], dcv[3]]))
    grad_x = dcur[None]

    shapes = {n: wts[n].shape for n in SMALL}
    gs = _pack_small({n: jnp.stack([gsmall[l][n] for l in range(DEPTH)]) for n in SMALL}, shapes)
    gs_all = _all_gather([gs], "gather_small_grads")[0]
    g_s, d_s, m_s, v_s = _adamw_sum(gs_all, _pack_small(wts, shapes), _pack_small(mom, shapes), _pack_small(var, shapes), "adamw_replicated")
    small = [_unpack_small(t, shapes) for t in (g_s, d_s, m_s, v_s)]

    parts0 = dict(zip(g_ffn, _push_wait(exchange_ffn, True, g_s, "exchange_ffn_0_wait")))
    parts0.update(zip(g_mix, _push_wait(exchange_mix, True, g_s, "exchange_mixer_0_wait")))
    update = lambda n: _adamw_weight([parts0[n], parts1[n]], wts[n], mom[n], var[n], f"adamw_{n}")
    big = {n: update(n) for n in SHARDED[1:]}
    parts0.update(zip(g_in, _push_wait(exchange_in, True, big["w_up"][0], "exchange_w_in_0_wait")))
    big["w_in"] = update("w_in")

    outs = [loss, grad_x]
    for kind in range(4):
        outs += [small[kind][n] if n in SMALL else big[n][kind] for n in names]
    return tuple(outs)
```

```python
import functools
import math

import jax
import jax.numpy as jnp
from jax import lax
from jax.experimental import pallas as pl
from jax.experimental.pallas import tpu as pltpu

F32 = jnp.float32
BF16 = jnp.bfloat16

D_MODEL = 1024
DEPTH = 2
HEAD_DIM = 64
N_Q_HEADS = 8
Q_PER_KV = 4
BLOCK = 128
POOL_W = 256
ATTN_W = 512
KV_W = 128
SGU_W = 256
IN_COLS = 4608
GATE_COL0 = 1536
D_FF = 2816
EPS = 1e-6
ROPE_THETA = 10000.0
N_DEV = 8
LANES = 128
HALO_POOL = 16
HALO_CONV = 8

ADAM_LR = 0.001
ADAM_B1 = 0.9
ADAM_B2 = 0.999
ADAM_EPS = 1e-08
ADAM_WD = 0.01
ADAM_STEP = 10

VMEM_LIMIT = 48 * 1024 * 1024

SHARDED = ("w_in", "w_proj_a", "w_proj_b", "w_proj_c", "w_out", "w_up", "w_down", "conv_w")
ROW_SHARDED = ("w_out", "w_down")
SMALL_ROW_TILE = 256
SMALL = ("norm1", "q_norm", "k_norm", "sinks", "w_pool", "pool_scale", "sgu_v_norm", "w_s", "b_s", "norm2", "conv_b")

_GELU_C = math.sqrt(2.0 / math.pi)
_GELU_A = 0.044715


def _params(*sem):
    return pltpu.CompilerParams(dimension_semantics=sem, vmem_limit_bytes=VMEM_LIMIT)


def _tile(n, prefs):
    for t in prefs:
        if t <= n and n % t == 0:
            return t
    return n


def _head_mean_matrix():
    r = lax.broadcasted_iota(jnp.int32, (LANES, LANES), 0)
    c = lax.broadcasted_iota(jnp.int32, (LANES, LANES), 1)
    return jnp.where((r >= HEAD_DIM) == (c >= HEAD_DIM), 1.0 / HEAD_DIM, 0.0).astype(BF16)


def _head_mean(v, bd):
    hi = v.astype(BF16)
    rest = v - hi.astype(F32)
    mid = rest.astype(BF16)
    lo = (rest - mid.astype(F32)).astype(BF16)
    mm = lambda p: jnp.dot(p, bd, preferred_element_type=F32)
    return mm(hi) + (mm(mid) + mm(lo))


def _rot_half(t):
    lane = lax.broadcasted_iota(jnp.int32, t.shape, 1)
    return jnp.where((lane & 32) == 0, pltpu.roll(t, LANES - 32, 1), pltpu.roll(t, 32, 1))


def _norm_rope(t, gn, cosf, sinf, bd):
    r = lax.rsqrt(_head_mean(t * t, bd) + EPS)
    n = t * r
    y = n * gn
    return y * cosf + _rot_half(y) * sinf, n, r


def _norm_rope_bwd(d, t, n, r, gn, cosf, sinf, bd):
    dy = d * cosf + _rot_half(d * sinf)
    dgn = jnp.sum(dy * n, axis=0, keepdims=True)
    u = dy * gn
    dt = r * u - t * (r * r * r) * _head_mean(t * u, bd)
    return dt, dgn


def _gelu(x):
    t = jnp.tanh(_GELU_C * (x + _GELU_A * (x * x * x)))
    return 0.5 * x * (1.0 + t), t


def _gelu_grad(x, t):
    return 0.5 * (1.0 + t) + 0.5 * x * (1.0 - t * t) * (_GELU_C * (1.0 + 3.0 * _GELU_A * x * x))


def _sigmoid(x):
    return jax.nn.sigmoid(x)


def _dot(a, b, ca=1, cb=0):
    return lax.dot_general(a.astype(BF16), b.astype(BF16), (((ca,), (cb,)), ((), ())), preferred_element_type=F32)


def _mm(a, b, *, ta=False, tb=False, add=None, out_dtype=F32, name, b_n=None, b_k=None, out_cols=None, out_into=None):
    m = a.shape[1] if ta else a.shape[0]
    k = a.shape[0] if ta else a.shape[1]
    n = b_n[1] if b_n else (b.shape[0] if tb else b.shape[1])
    tm = _tile(m, (1024, 1408, 512, 256, 128))
    tn = _tile(n, (1024, 1152, 1408, 512, 256, 128))
    tk = _tile(k, (1024, 1152, 1408, 512, 256, 128))
    nk = k // tk
    n0 = b_n[0] // tn if b_n else 0
    k0 = b_k[0] // tk if b_k else 0
    o0, n_out = (out_cols[0] // tn, out_cols[1]) if out_cols else (0, n)
    has_add = add is not None
    n_in = 2 + has_add + (out_into is not None)

    def body(*refs):
        a_ref, b_ref = refs[0], refs[1]
        add_ref = refs[2] if has_add else None
        o_ref = refs[n_in]
        p = _dot(a_ref[...], b_ref[...], 0 if ta else 1, 1 if tb else 0)

        def finish(r):
            if has_add:
                r = r + add_ref[...]
            o_ref[...] = r.astype(out_dtype)

        if nk == 1:
            finish(p)
        else:
            acc_ref = refs[-1]
            kk = pl.program_id(2)

            @pl.when(kk == 0)
            def _():
                acc_ref[...] = p

            @pl.when(kk > 0)
            def _():
                acc_ref[...] += p

            @pl.when(kk == nk - 1)
            def _():
                finish(acc_ref[...])

    a_spec = pl.BlockSpec((tk, tm), lambda i, j, kk: (kk, i)) if ta else pl.BlockSpec((tm, tk), lambda i, j, kk: (i, kk))
    if tb:
        b_spec = pl.BlockSpec((tn, tk), lambda i, j, kk: (j + n0, kk + k0))
    else:
        b_spec = pl.BlockSpec((tk, tn), lambda i, j, kk: (kk + k0, j + n0))
    in_specs = [a_spec, b_spec] + ([pl.BlockSpec((tm, tn), lambda i, j, kk: (i, j))] if has_add else [])
    args = (a, b) + ((add,) if has_add else ())
    if out_into is not None:
        in_specs.append(pl.BlockSpec(memory_space=pl.ANY))
        args += (out_into,)
    return pl.pallas_call(
        body, name=name, grid=(m // tm, n // tn, nk), in_specs=in_specs,
        out_specs=pl.BlockSpec((tm, tn), lambda i, j, kk: (i, j + o0)),
        out_shape=jax.ShapeDtypeStruct((m, n_out), out_dtype),
        scratch_shapes=[pltpu.VMEM((tm, tn), F32)] if nk > 1 else [],
        input_output_aliases={n_in - 1: 0} if out_into is not None else {},
        compiler_params=_params("parallel", "parallel", "arbitrary"),
    )(*args)


def _rms_fwd(x, g, name):
    s, d = x.shape
    tr = _tile(s, (512, 256, 128))

    def body(x_ref, g_ref, h_ref):
        xv = x_ref[...]
        r = lax.rsqrt(jnp.mean(xv * xv, axis=-1, keepdims=True) + EPS)
        h_ref[...] = ((xv * r) * g_ref[...]).astype(BF16)

    return pl.pallas_call(
        body, name=name, grid=(s // tr,),
        in_specs=[pl.BlockSpec((tr, d), lambda i: (i, 0)), pl.BlockSpec((1, d), lambda i: (0, 0))],
        out_specs=pl.BlockSpec((tr, d), lambda i: (i, 0)),
        out_shape=jax.ShapeDtypeStruct((s, d), BF16), compiler_params=_params("parallel"),
    )(x, g.reshape(1, d))


def _rms_bwd(x, g, dh, dres, name):
    s, d = x.shape
    tr = _tile(s, (512, 256, 128))

    def body(x_ref, g_ref, dh_ref, dres_ref, dx_ref, dg_ref):
        xv = x_ref[...]
        r = lax.rsqrt(jnp.mean(xv * xv, axis=-1, keepdims=True) + EPS)
        dhv = dh_ref[...]
        u = dhv * g_ref[...]
        dx_ref[...] = dres_ref[...] + (r * u - xv * (r * r * r) * jnp.mean(xv * u, axis=-1, keepdims=True))
        part = jnp.sum(dhv * (xv * r), axis=0, keepdims=True)

        @pl.when(pl.program_id(0) == 0)
        def _():
            dg_ref[...] = part

        @pl.when(pl.program_id(0) > 0)
        def _():
            dg_ref[...] += part

    row = pl.BlockSpec((tr, d), lambda i: (i, 0))
    vec = pl.BlockSpec((1, d), lambda i: (0, 0))
    return pl.pallas_call(
        body, name=name, grid=(s // tr,), in_specs=[row, vec, row, row], out_specs=[row, vec],
        out_shape=[jax.ShapeDtypeStruct((s, d), F32), jax.ShapeDtypeStruct((1, d), F32)],
        compiler_params=_params("arbitrary"),
    )(x, g.reshape(1, d), dh, dres)


def _loss_head(y, target):
    s, d = y.shape
    tr = _tile(s, (512, 256, 128))

    def body(y_ref, t_ref, dy_ref, l_ref):
        err = y_ref[...] - t_ref[...]
        dy_ref[...] = err * (1.0 / d)
        part = jnp.sum(jnp.sum(err * err, axis=-1, keepdims=True) * (1.0 / d), axis=0, keepdims=True) * 0.5
        part = jnp.broadcast_to(part, (8, LANES))

        @pl.when(pl.program_id(0) == 0)
        def _():
            l_ref[...] = part

        @pl.when(pl.program_id(0) > 0)
        def _():
            l_ref[...] += part

    row = pl.BlockSpec((tr, d), lambda i: (i, 0))
    acc = pl.BlockSpec((8, LANES), lambda i: (0, 0))
    return pl.pallas_call(
        body, name="loss_head", grid=(s // tr,), in_specs=[row, row], out_specs=[row, acc],
        out_shape=[jax.ShapeDtypeStruct((s, d), F32), jax.ShapeDtypeStruct((8, LANES), F32)],
        compiler_params=_params("arbitrary"),
    )(y, target)


def _pool_lane_select(lane, v2, v4, v8, v16):
    return jnp.where(lane < 64, v2, jnp.where(lane < 128, v4, jnp.where(lane < 192, v8, v16)))


def _pool_diff(xc, xp, row0):
    n = BLOCK + HALO_POOL
    cat = jnp.concatenate([xp, xc], axis=0)
    s2 = cat + pltpu.roll(cat, 1, 0)
    s4 = s2 + pltpu.roll(s2, 2, 0)
    s8 = s4 + pltpu.roll(s4, 4, 0)
    s16 = s8 + pltpu.roll(s8, 8, 0)
    lane = lax.broadcasted_iota(jnp.int32, (n, POOL_W), 1)
    wsum = _pool_lane_select(lane, s2, s4, s8, s16)[HALO_POOL:]
    return wsum / _pool_count(row0, BLOCK) - xc


def _pool_count(row0, rows):
    lane = lax.broadcasted_iota(jnp.int32, (rows, POOL_W), 1)
    t = lax.broadcasted_iota(jnp.int32, (rows, POOL_W), 0) + row0
    return jnp.minimum(t + 1, _pool_lane_select(lane, 2, 4, 8, 16)).astype(F32)


def _pool_fwd(z, bdw, scale, name):
    s = z.shape[0]
    nb = s // BLOCK

    def body(xc_ref, xp_ref, w_ref, sc_ref, a_ref):
        i = pl.program_id(0)
        xp = jnp.where(i > 0, xp_ref[...], 0.0)
        diff = _pool_diff(xc_ref[...], xp, i * BLOCK)
        a_ref[...] = (_dot(diff, w_ref[...]) * sc_ref[...]).astype(BF16)

    return pl.pallas_call(
        body, name=name, grid=(nb,),
        in_specs=[pl.BlockSpec((BLOCK, POOL_W), lambda i: (i, 0)),
                  pl.BlockSpec((HALO_POOL, POOL_W), lambda i: (jnp.maximum(i * (BLOCK // HALO_POOL) - 1, 0), 0)),
                  pl.BlockSpec((POOL_W, POOL_W), lambda i: (0, 0)),
                  pl.BlockSpec((1, POOL_W), lambda i: (0, 0))],
        out_specs=pl.BlockSpec((BLOCK, POOL_W), lambda i: (i, 0)),
        out_shape=jax.ShapeDtypeStruct((s, POOL_W), BF16), compiler_params=_params("parallel"),
    )(z, z, bdw, scale.reshape(1, POOL_W))


def _pool_bwd(z, da, bdw, scale, name):
    s = z.shape[0]
    nb = s // BLOCK
    per = BLOCK // HALO_POOL
    n = BLOCK + HALO_POOL

    def body(xc_ref, xp_ref, dac_ref, dan_ref, w_ref, sc_ref, dx_ref, dw_ref, dsc_ref):
        i = pl.program_id(0)
        xp = jnp.where(i > 0, xp_ref[...], 0.0)
        diff = _pool_diff(xc_ref[...], xp, i * BLOCK)
        mixed = _dot(diff, w_ref[...])
        dac = dac_ref[...]
        dan = jnp.where(i < nb - 1, dan_ref[...], 0.0)
        dmix = jnp.concatenate([dac, dan], axis=0) * sc_ref[...]
        ddiff = _dot(dmix, w_ref[...], 1, 1)
        e = ddiff / _pool_count(i * BLOCK, n)
        f2 = e + pltpu.roll(e, n - 1, 0)
        f4 = f2 + pltpu.roll(f2, n - 2, 0)
        f8 = f4 + pltpu.roll(f4, n - 4, 0)
        f16 = f8 + pltpu.roll(f8, n - 8, 0)
        lane = lax.broadcasted_iota(jnp.int32, (n, POOL_W), 1)
        back = _pool_lane_select(lane, f2, f4, f8, f16)
        dx_ref[...] = (back[:BLOCK] - ddiff[:BLOCK]).astype(BF16)
        dw = _dot(diff, dmix[:BLOCK], 0, 0)
        dsc = jnp.sum(dac * mixed, axis=0, keepdims=True)

        @pl.when(i == 0)
        def _():
            dw_ref[...] = dw
            dsc_ref[...] = dsc

        @pl.when(i > 0)
        def _():
            dw_ref[...] += dw
            dsc_ref[...] += dsc

    blk = pl.BlockSpec((BLOCK, POOL_W), lambda i: (i, 0))
    return pl.pallas_call(
        body, name=name, grid=(nb,),
        in_specs=[blk, pl.BlockSpec((HALO_POOL, POOL_W), lambda i: (jnp.maximum(i * per - 1, 0), 0)),
                  blk, pl.BlockSpec((HALO_POOL, POOL_W), lambda i: (jnp.minimum((i + 1) * per, nb * per - 1), 0)),
                  pl.BlockSpec((POOL_W, POOL_W), lambda i: (0, 0)), pl.BlockSpec((1, POOL_W), lambda i: (0, 0))],
        out_specs=[blk, pl.BlockSpec((POOL_W, POOL_W), lambda i: (0, 0)), pl.BlockSpec((1, POOL_W), lambda i: (0, 0))],
        out_shape=[jax.ShapeDtypeStruct((s, POOL_W), BF16), jax.ShapeDtypeStruct((POOL_W, POOL_W), F32),
                   jax.ShapeDtypeStruct((1, POOL_W), F32)],
        compiler_params=_params("arbitrary"),
    )(z, z, da, da, bdw, scale.reshape(1, POOL_W))


def _attn_setup(zc_ref, zp_ref, cc_ref, cp_ref, sc_ref, sp_ref, qn_ref, kn_ref, bd):
    q = []
    for j in range(ATTN_W // LANES):
        t = zc_ref[:, POOL_W + j * LANES:POOL_W + (j + 1) * LANES]
        q.append((t,) + _norm_rope(t, qn_ref[...], cc_ref[...], sc_ref[...], bd))
    kc_raw = zc_ref[:, POOL_W + ATTN_W:POOL_W + ATTN_W + KV_W]
    kc = _norm_rope(kc_raw, kn_ref[...], cc_ref[...], sc_ref[...], bd)[0]
    kp = _norm_rope(zp_ref[:, :KV_W], kn_ref[...], cp_ref[...], sp_ref[...], bd)[0]
    kband = jnp.concatenate([kp, kc], axis=0).astype(BF16)
    vband = jnp.concatenate([zp_ref[:, KV_W:], zc_ref[:, POOL_W + ATTN_W + KV_W:POOL_W + ATTN_W + 2 * KV_W]], axis=0).astype(BF16)
    return q, kband, vband


def _attn_mask(i):
    row = lax.broadcasted_iota(jnp.int32, (Q_PER_KV * BLOCK, 2 * BLOCK), 0) & (BLOCK - 1)
    col = lax.broadcasted_iota(jnp.int32, (Q_PER_KV * BLOCK, 2 * BLOCK), 1)
    dist = row + BLOCK - col
    return (dist >= 0) & (dist < BLOCK) & ((col >= BLOCK) | (i > 0))


def _stack_heads(tiles, kvh):
    return jnp.concatenate([_to_kv_lanes(tiles[h // 2], h) for h in range(kvh * Q_PER_KV, (kvh + 1) * Q_PER_KV)], axis=0)


def _unstack_heads(stacked, kvh, tiles):
    for g in range(Q_PER_KV):
        h = kvh * Q_PER_KV + g
        t = _from_kv_lanes(stacked[g * BLOCK:(g + 1) * BLOCK], h)
        tiles[h // 2] = t if tiles[h // 2] is None else tiles[h // 2] + t


def _sink_column(sink_ref, kvh):
    grp = lax.broadcasted_iota(jnp.int32, (Q_PER_KV * BLOCK, 1), 0) >> 7
    s = [sink_ref[kvh * Q_PER_KV + g] for g in range(Q_PER_KV)]
    return jnp.where(grp == 0, s[0], jnp.where(grp == 1, s[1], jnp.where(grp == 2, s[2], s[3])))


def _to_kv_lanes(t, h):
    kvh = h // Q_PER_KV
    if (h % 2) != kvh:
        t = pltpu.roll(t, HEAD_DIM, 1)
    lane = lax.broadcasted_iota(jnp.int32, t.shape, 1)
    return jnp.where((lane >= HEAD_DIM) == (kvh == 1), t, 0.0)


def _from_kv_lanes(t, h):
    kvh = h // Q_PER_KV
    lane = lax.broadcasted_iota(jnp.int32, t.shape, 1)
    t = jnp.where((lane >= HEAD_DIM) == (kvh == 1), t, 0.0)
    if (h % 2) != kvh:
        t = pltpu.roll(t, HEAD_DIM, 1)
    return t


def _attn_probs(qh, kband, mask, sink):
    sc = _dot(qh, kband, 1, 1) * (HEAD_DIM ** -0.5)
    sc = jnp.where(mask, sc, -1e30)
    m = jnp.maximum(jnp.max(sc, axis=1, keepdims=True), sink)
    p = jnp.exp(sc - m)
    psink = jnp.exp(sink - m)
    den = jnp.sum(p, axis=1, keepdims=True) + psink
    return p / den, psink / den


def _attn_specs(nb):
    cur = lambda i: (i, 0)
    prev = lambda i: (jnp.maximum(i - 1, 0), 0)
    tab = lambda f: pl.BlockSpec((BLOCK, LANES), f)
    vec = pl.BlockSpec((1, LANES), lambda i: (0, 0))
    return [pl.BlockSpec((BLOCK, 1024), cur),
            pl.BlockSpec((BLOCK, 2 * KV_W), lambda i: (jnp.maximum(i - 1, 0), 3)),
            tab(cur), tab(prev), tab(cur), tab(prev), vec, vec,
            pl.BlockSpec(memory_space=pltpu.SMEM)]


def _attn_fwd(z, cosf, sinf, qn, kn, sinks, name):
    s = z.shape[0]
    nb = s // BLOCK

    def body(zc_ref, zp_ref, cc_ref, cp_ref, sc_ref, sp_ref, qn_ref, kn_ref, sink_ref, o_ref):
        i = pl.program_id(0)
        bd = _head_mean_matrix()
        q, kband, vband = _attn_setup(zc_ref, zp_ref, cc_ref, cp_ref, sc_ref, sp_ref, qn_ref, kn_ref, bd)
        mask = _attn_mask(i)
        out = [None] * (ATTN_W // LANES)
        for kvh in range(N_Q_HEADS // Q_PER_KV):
            qs = _stack_heads([t[1] for t in q], kvh)
            probs, _ = _attn_probs(qs, kband, mask, _sink_column(sink_ref, kvh))
            _unstack_heads(_dot(probs, vband), kvh, out)
        for j, o in enumerate(out):
            o_ref[:, j * LANES:(j + 1) * LANES] = o.astype(BF16)

    return pl.pallas_call(
        body, name=name, grid=(nb,), in_specs=_attn_specs(nb),
        out_specs=pl.BlockSpec((BLOCK, ATTN_W), lambda i: (i, 0)),
        out_shape=jax.ShapeDtypeStruct((s, ATTN_W), BF16), compiler_params=_params("parallel"),
    )(z, z, cosf, cosf, sinf, sinf, qn, kn, sinks)


def _attn_bwd(z, cosf, sinf, qn, kn, sinks, d_out, name):
    s = z.shape[0]
    nb = s // BLOCK
    nt = ATTN_W // LANES

    def body(zc_ref, zp_ref, cc_ref, cp_ref, sc_ref, sp_ref, qn_ref, kn_ref, sink_ref, do_ref,
             dq_ref, dkc_ref, dkp_ref, dvc_ref, dvp_ref, dqn_ref, dsink_ref):
        i = pl.program_id(0)
        bd = _head_mean_matrix()
        q, kband, vband = _attn_setup(zc_ref, zp_ref, cc_ref, cp_ref, sc_ref, sp_ref, qn_ref, kn_ref, bd)
        mask = _attn_mask(i)

        @pl.when(i == 0)
        def _():
            dqn_ref[...] = jnp.zeros_like(dqn_ref)
            dsink_ref[...] = jnp.zeros_like(dsink_ref)

        dq = [None] * nt
        dk = jnp.zeros((2 * BLOCK, KV_W), F32)
        dv = jnp.zeros((2 * BLOCK, KV_W), F32)
        d_tiles = [do_ref[:, j * LANES:(j + 1) * LANES] for j in range(nt)]
        for kvh in range(N_Q_HEADS // Q_PER_KV):
            qs = _stack_heads([t[1] for t in q], kvh)
            probs, psink = _attn_probs(qs, kband, mask, _sink_column(sink_ref, kvh))
            dos = _stack_heads(d_tiles, kvh)
            dp = _dot(dos, vband, 1, 1)
            delta = jnp.sum(dp * probs, axis=1, keepdims=True)
            ds = (probs * (dp - delta)) * (HEAD_DIM ** -0.5)
            dsink = -psink * delta
            for g in range(Q_PER_KV):
                h = kvh * Q_PER_KV + g
                dsink_ref[h:h + 1, :] += jnp.broadcast_to(jnp.sum(dsink[g * BLOCK:(g + 1) * BLOCK], axis=0, keepdims=True), (1, LANES))
            _unstack_heads(_dot(ds, kband), kvh, dq)
            dk = dk + _dot(ds, qs, 0, 0)
            dv = dv + _dot(probs, dos, 0, 0)
        dgn = jnp.zeros((1, LANES), F32)
        for j in range(nt):
            t, _, n, r = q[j]
            dt, g = _norm_rope_bwd(dq[j], t, n, r, qn_ref[...], cc_ref[...], sc_ref[...], bd)
            dq_ref[:, j * LANES:(j + 1) * LANES] = dt.astype(BF16)
            dgn = dgn + g
        dqn_ref[...] += jnp.broadcast_to(dgn, (8, LANES))
        dkp_ref[...] = dk[:BLOCK]
        dkc_ref[...] = dk[BLOCK:]
        dvp_ref[...] = dv[:BLOCK]
        dvc_ref[...] = dv[BLOCK:]

        @pl.when(i == nb - 1)
        def _():
            acc = dqn_ref[...]
            dqn_ref[...] = acc + pltpu.roll(acc, HEAD_DIM, 1)

    blk = lambda w: pl.BlockSpec((BLOCK, w), lambda i: (i, 0))
    acc = pl.BlockSpec((8, LANES), lambda i: (0, 0))
    kv = jax.ShapeDtypeStruct((s, KV_W), F32)
    return pl.pallas_call(
        body, name=name, grid=(nb,), in_specs=_attn_specs(nb) + [blk(ATTN_W)],
        out_specs=[blk(ATTN_W), blk(KV_W), blk(KV_W), blk(KV_W), blk(KV_W), acc, acc],
        out_shape=[jax.ShapeDtypeStruct((s, ATTN_W), BF16), kv, kv, kv, kv,
                   jax.ShapeDtypeStruct((8, LANES), F32), jax.ShapeDtypeStruct((8, LANES), F32)],
        compiler_params=_params("arbitrary"),
    )(z, z, cosf, cosf, sinf, sinf, qn, kn, sinks, d_out)


def _kv_post(z, cosf, sinf, kn, dkc, dkp, dvc, dvp, dxp, dq, duv, dz, name):
    s = z.shape[0]
    nb = s // BLOCK

    def body(zk_ref, c_ref, s_ref, kn_ref, dkc_ref, dkp_ref, dvc_ref, dvp_ref, dxp_ref, dq_ref, duv_ref, dz_in,
             dz_ref, dkn_ref):
        j = pl.program_id(0)
        bd = _head_mean_matrix()
        last = j == nb - 1
        d = dkc_ref[...] + jnp.where(last, 0.0, dkp_ref[...])
        t = zk_ref[:, :KV_W]
        _, n, r = _norm_rope(t, kn_ref[...], c_ref[...], s_ref[...], bd)
        dt, g = _norm_rope_bwd(d, t, n, r, kn_ref[...], c_ref[...], s_ref[...], bd)
        dvv = dvc_ref[...] + jnp.where(last, 0.0, dvp_ref[...])
        dz_ref[:, 0:POOL_W] = dxp_ref[...]
        dz_ref[:, POOL_W:POOL_W + ATTN_W] = dq_ref[...]
        dz_ref[:, POOL_W + ATTN_W:POOL_W + ATTN_W + KV_W] = dt.astype(BF16)
        dz_ref[:, POOL_W + ATTN_W + KV_W:POOL_W + ATTN_W + 2 * KV_W] = dvv.astype(BF16)
        dz_ref[:, POOL_W + ATTN_W + 2 * KV_W:GATE_COL0] = duv_ref[...]

        @pl.when(j == 0)
        def _():
            dkn_ref[...] = jnp.zeros_like(dkn_ref)

        dkn_ref[...] += jnp.broadcast_to(g, (8, LANES))

        @pl.when(last)
        def _():
            acc = dkn_ref[...]
            dkn_ref[...] = acc + pltpu.roll(acc, HEAD_DIM, 1)

    cur = lambda w: pl.BlockSpec((BLOCK, w), lambda j: (j, 0))
    nxt = pl.BlockSpec((BLOCK, KV_W), lambda j: (jnp.minimum(j + 1, nb - 1), 0))
    vec = pl.BlockSpec((1, LANES), lambda j: (0, 0))
    return pl.pallas_call(
        body, name=name, grid=(nb,),
        in_specs=[pl.BlockSpec((BLOCK, 2 * KV_W), lambda j: (j, 3)), cur(LANES), cur(LANES), vec,
                  cur(KV_W), nxt, cur(KV_W), nxt, cur(POOL_W), cur(ATTN_W), cur(2 * SGU_W),
                  pl.BlockSpec(memory_space=pl.ANY)],
        out_specs=[pl.BlockSpec((BLOCK, GATE_COL0), lambda j: (j, 0)), pl.BlockSpec((8, LANES), lambda j: (0, 0))],
        out_shape=[jax.ShapeDtypeStruct(dz.shape, dz.dtype), jax.ShapeDtypeStruct((8, LANES), F32)],
        input_output_aliases={11: 0}, compiler_params=_params("arbitrary"),
    )(z, cosf, sinf, kn, dkc, dkp, dvc, dvp, dxp, dq, duv, dz)


def _sgu_setup(z_ref, ws_ref, vn_ref, bd):
    us = z_ref[:, :SGU_W]
    vs = z_ref[:, SGU_W:]
    ug, tu = _gelu(us)
    gv, tv = _gelu(vs)
    rr = jnp.concatenate([lax.rsqrt(_head_mean(gv[:, k * LANES:(k + 1) * LANES] ** 2, bd) + EPS) for k in range(2)], axis=1)
    vg = (gv * rr) * vn_ref[...]
    tril = lax.broadcasted_iota(jnp.int32, (BLOCK, BLOCK), 0) >= lax.broadcasted_iota(jnp.int32, (BLOCK, BLOCK), 1)
    w = [jnp.where(tril, ws_ref[g], 0.0).astype(BF16) for g in range(4)]
    return us, vs, ug, tu, gv, tv, rr, vg, w, tril


def _group_select(parts):
    lane = lax.broadcasted_iota(jnp.int32, parts[0].shape, 1)
    return _pool_lane_select(lane, *parts)


def _sgu_fwd(z, ws, bcol, vn, name):
    s = z.shape[0]
    nb = s // BLOCK

    def body(z_ref, ws_ref, b_ref, vn_ref, c_ref):
        bd = _head_mean_matrix()
        _, _, ug, _, _, _, _, vg, w, _ = _sgu_setup(z_ref, ws_ref, vn_ref, bd)
        sg = _group_select([_dot(w[g], vg) for g in range(4)]) + b_ref[...]
        c_ref[...] = (ug * sg).astype(BF16)

    return pl.pallas_call(
        body, name=name, grid=(nb,),
        in_specs=[pl.BlockSpec((BLOCK, 2 * SGU_W), lambda i: (i, 2)), pl.BlockSpec((4, BLOCK, BLOCK), lambda i: (0, 0, 0)),
                  pl.BlockSpec((BLOCK, SGU_W), lambda i: (0, 0)), pl.BlockSpec((1, SGU_W), lambda i: (0, 0))],
        out_specs=pl.BlockSpec((BLOCK, SGU_W), lambda i: (i, 0)),
        out_shape=jax.ShapeDtypeStruct((s, SGU_W), BF16), compiler_params=_params("parallel"),
    )(z, ws, bcol, vn)


def _sgu_bwd(z, ws, bcol, vn, dc, name):
    s = z.shape[0]
    nb = s // BLOCK

    def body(z_ref, ws_ref, b_ref, vn_ref, dc_ref, duv_ref, dws_ref, db_ref, dvn_ref):
        i = pl.program_id(0)
        bd = _head_mean_matrix()
        us, vs, ug, tu, gv, tv, rr, vg, w, tril = _sgu_setup(z_ref, ws_ref, vn_ref, bd)
        sg = _group_select([_dot(w[g], vg) for g in range(4)]) + b_ref[...]
        dcv = dc_ref[...]
        dug = dcv * sg
        dsg = dcv * ug
        lane = lax.broadcasted_iota(jnp.int32, dsg.shape, 1)

        @pl.when(i == 0)
        def _():
            dws_ref[...] = jnp.zeros_like(dws_ref)
            db_ref[...] = jnp.zeros_like(db_ref)
            dvn_ref[...] = jnp.zeros_like(dvn_ref)

        for g in range(4):
            dsg_g = jnp.where((lane >= g * HEAD_DIM) & (lane < (g + 1) * HEAD_DIM), dsg, 0.0)
            dws_ref[g] += jnp.where(tril, _dot(dsg_g, vg, 1, 1), 0.0)
        dvg = _group_select([_dot(w[g], dsg, 0, 0) for g in range(4)])
        db_ref[...] += dsg
        n = gv * rr
        part = jnp.sum(dvg * n, axis=0, keepdims=True)
        dvn_ref[...] += jnp.broadcast_to(part[:, :LANES] + part[:, LANES:], (8, LANES))
        u = dvg * vn_ref[...]
        tu_ = gv * u
        mean = jnp.concatenate([_head_mean(tu_[:, k * LANES:(k + 1) * LANES], bd) for k in range(2)], axis=1)
        dgv = rr * u - gv * (rr * rr * rr) * mean
        duv_ref[:, :SGU_W] = (dug * _gelu_grad(us, tu)).astype(BF16)
        duv_ref[:, SGU_W:] = (dgv * _gelu_grad(vs, tv)).astype(BF16)

        @pl.when(i == nb - 1)
        def _():
            acc = dvn_ref[...]
            dvn_ref[...] = acc + pltpu.roll(acc, HEAD_DIM, 1)
            for k in range(2):
                db_ref[:, k * LANES:(k + 1) * LANES] = _head_mean(db_ref[:, k * LANES:(k + 1) * LANES], bd) * float(HEAD_DIM)

    return pl.pallas_call(
        body, name=name, grid=(nb,),
        in_specs=[pl.BlockSpec((BLOCK, 2 * SGU_W), lambda i: (i, 2)), pl.BlockSpec((4, BLOCK, BLOCK), lambda i: (0, 0, 0)),
                  pl.BlockSpec((BLOCK, SGU_W), lambda i: (0, 0)), pl.BlockSpec((1, SGU_W), lambda i: (0, 0)),
                  pl.BlockSpec((BLOCK, SGU_W), lambda i: (i, 0))],
        out_specs=[pl.BlockSpec((BLOCK, 2 * SGU_W), lambda i: (i, 0)), pl.BlockSpec((4, BLOCK, BLOCK), lambda i: (0, 0, 0)),
                   pl.BlockSpec((BLOCK, SGU_W), lambda i: (0, 0)), pl.BlockSpec((8, LANES), lambda i: (0, 0))],
        out_shape=[jax.ShapeDtypeStruct((s, 2 * SGU_W), BF16), jax.ShapeDtypeStruct((4, BLOCK, BLOCK), F32),
                   jax.ShapeDtypeStruct((BLOCK, SGU_W), F32), jax.ShapeDtypeStruct((8, LANES), F32)],
        compiler_params=_params("arbitrary"),
    )(z, ws, bcol, vn, dc)


MERGE_TN = 512
MERGE_TM = 512


def _merge_fwd(a, b, c, wpa, wpb, wpc, z, name):
    s = z.shape[0]
    tm = _tile(s, (MERGE_TM, BLOCK))
    gate0 = GATE_COL0 // MERGE_TN

    def body(a_ref, b_ref, c_ref, wa_ref, wb_ref, wc_ref, g0_ref, g1_ref, g2_ref, o_ref):
        r = _sigmoid(g0_ref[...]) * _dot(a_ref[...], wa_ref[...])
        r = r + _sigmoid(g1_ref[...]) * _dot(b_ref[...], wb_ref[...])
        r = r + _sigmoid(g2_ref[...]) * _dot(c_ref[...], wc_ref[...])
        o_ref[...] = r.astype(BF16)

    x_spec = lambda w: pl.BlockSpec((tm, w), lambda i, n: (i, 0))
    w_spec = lambda w: pl.BlockSpec((w, MERGE_TN), lambda i, n: (0, n))
    g_spec = lambda br: pl.BlockSpec((tm, MERGE_TN), lambda i, n: (i, gate0 + 2 * br + n))
    return pl.pallas_call(
        body, name=name, grid=(s // tm, D_MODEL // MERGE_TN),
        in_specs=[x_spec(POOL_W), x_spec(ATTN_W), x_spec(SGU_W), w_spec(POOL_W), w_spec(ATTN_W), w_spec(SGU_W),
                  g_spec(0), g_spec(1), g_spec(2)],
        out_specs=pl.BlockSpec((tm, MERGE_TN), lambda i, n: (i, n)),
        out_shape=jax.ShapeDtypeStruct((s, D_MODEL), BF16), compiler_params=_params("parallel", "parallel"),
    )(a, b, c, wpa, wpb, wpc, z, z, z)


def _branch_bwd(br, xb, wp, z, dm, dz, name):
    s = z.shape[0]
    kb = xb.shape[1]
    tm = _tile(s, (MERGE_TM, BLOCK))
    gate0 = GATE_COL0 // MERGE_TN
    aliased = dz is not None

    def body(*refs):
        x_ref, w_ref, g_ref, dm_ref = refs[:4]
        dz_ref, dy_ref, dx_ref = refs[-3:]
        n = pl.program_id(1)
        y = _dot(x_ref[...], w_ref[...])
        sg = _sigmoid(g_ref[...])
        dmv = dm_ref[...]
        dy = (dmv * sg).astype(BF16)
        dy_ref[...] = dy
        dz_ref[...] = ((dmv * y) * (sg * (1.0 - sg))).astype(BF16)
        dx = _dot(dy, w_ref[...], 1, 1)

        @pl.when(n == 0)
        def _():
            dx_ref[...] = dx

        @pl.when(n > 0)
        def _():
            dx_ref[...] += dx

    in_specs = [pl.BlockSpec((tm, kb), lambda i, n: (i, 0)), pl.BlockSpec((kb, MERGE_TN), lambda i, n: (0, n)),
                pl.BlockSpec((tm, MERGE_TN), lambda i, n: (i, gate0 + 2 * br + n)),
                pl.BlockSpec((tm, MERGE_TN), lambda i, n: (i, n))]
    args = [xb, wp, z, dm]
    if aliased:
        in_specs.append(pl.BlockSpec(memory_space=pl.ANY))
        args.append(dz)
    return pl.pallas_call(
        body, name=name, grid=(s // tm, D_MODEL // MERGE_TN), in_specs=in_specs,
        out_specs=[pl.BlockSpec((tm, MERGE_TN), lambda i, n: (i, gate0 + 2 * br + n)),
                   pl.BlockSpec((tm, MERGE_TN), lambda i, n: (i, n)),
                   pl.BlockSpec((tm, kb), lambda i, n: (i, 0))],
        out_shape=[jax.ShapeDtypeStruct((s, IN_COLS), BF16), jax.ShapeDtypeStruct((s, D_MODEL), BF16),
                   jax.ShapeDtypeStruct((s, kb), F32)],
        input_output_aliases={4: 0} if aliased else {},
        compiler_params=_params("parallel", "arbitrary"),
    )(*args)


FFN_TM = 256
FFN_TC = 1408


def _conv3(cur, prev, w_ref, b_ref):
    cat = jnp.concatenate([prev, cur], axis=0)
    x1 = pltpu.roll(cat, 1, 0)[HALO_CONV:]
    x2 = pltpu.roll(cat, 2, 0)[HALO_CONV:]
    return w_ref[0:1, :] * x2 + w_ref[1:2, :] * x1 + w_ref[2:3, :] * cur + b_ref[...], x1, x2


def _ffn_specs(s, tm, rows_first):
    per = tm // HALO_CONV
    if rows_first:
        cur = pl.BlockSpec((tm, FFN_TC), lambda i, j: (i, j))
        prev = pl.BlockSpec((HALO_CONV, FFN_TC), lambda i, j: (jnp.maximum(i * per - 1, 0), j))
        w = pl.BlockSpec((3, FFN_TC), lambda i, j: (0, j))
        b = pl.BlockSpec((1, FFN_TC), lambda i, j: (0, j))
    else:
        cur = pl.BlockSpec((tm, FFN_TC), lambda j, i: (i, j))
        prev = pl.BlockSpec((HALO_CONV, FFN_TC), lambda j, i: (jnp.maximum(i * per - 1, 0), j))
        w = pl.BlockSpec((3, FFN_TC), lambda j, i: (0, j))
        b = pl.BlockSpec((1, FFN_TC), lambda j, i: (0, j))
    return cur, prev, w, b


def _ffn_act_fwd(upg, upv, cwg, cwv, cbg, cbv, name):
    s = upg.shape[0]
    tm = _tile(s, (FFN_TM, BLOCK))
    cur, prev, w, b = _ffn_specs(s, tm, True)

    def body(g_ref, gp_ref, v_ref, vp_ref, wg_ref, wv_ref, bg_ref, bv_ref, o_ref):
        first = pl.program_id(0) == 0
        gate = _conv3(g_ref[...], jnp.where(first, 0.0, gp_ref[...]), wg_ref, bg_ref)[0]
        val = _conv3(v_ref[...], jnp.where(first, 0.0, vp_ref[...]), wv_ref, bv_ref)[0]
        o_ref[...] = ((gate * _sigmoid(gate)) * val).astype(BF16)

    return pl.pallas_call(
        body, name=name, grid=(s // tm, D_FF // FFN_TC), in_specs=[cur, prev, cur, prev, w, w, b, b], out_specs=cur,
        out_shape=jax.ShapeDtypeStruct((s, D_FF), BF16), compiler_params=_params("parallel", "parallel"),
    )(upg, upg, upv, upv, cwg, cwv, cbg, cbv)


def _ffn_bwd(upg, upv, cwg, cwv, cbg, cbv, dact, name):
    s = upg.shape[0]
    tm = _tile(s, (FFN_TM, BLOCK))
    per = tm // HALO_CONV
    nrow = s // tm
    n = tm + HALO_CONV
    cur, prev, w, b = _ffn_specs(s, tm, False)
    nxt = pl.BlockSpec((HALO_CONV, FFN_TC), lambda j, i: (jnp.minimum((i + 1) * per, nrow * per - 1), j))

    def body(g_ref, gp_ref, gn_ref, v_ref, vp_ref, vn_ref, wg_ref, wv_ref, bg_ref, bv_ref, da_ref, dan_ref,
             dg_ref, dv_ref, dwg_ref, dwv_ref):
        i = pl.program_id(1)
        first = i == 0
        g_ext = jnp.concatenate([g_ref[...], gn_ref[...]], axis=0)
        v_ext = jnp.concatenate([v_ref[...], vn_ref[...]], axis=0)
        da = jnp.concatenate([da_ref[...], jnp.where(i == nrow - 1, 0.0, dan_ref[...])], axis=0)
        gate, g1, g2 = _conv3(g_ext, jnp.where(first, 0.0, gp_ref[...]), wg_ref, bg_ref)
        val, v1, v2 = _conv3(v_ext, jnp.where(first, 0.0, vp_ref[...]), wv_ref, bv_ref)
        sg = _sigmoid(gate)
        dgate = (da * val) * (sg * (1.0 + gate * (1.0 - sg)))
        dval = da * (gate * sg)

        @pl.when(first)
        def _():
            dwg_ref[...] = jnp.zeros_like(dwg_ref)
            dwv_ref[...] = jnp.zeros_like(dwv_ref)

        for ref, o_ref, w_ref, d, x0, x1, x2 in ((dwg_ref, dg_ref, wg_ref, dgate, g_ext, g1, g2),
                                                 (dwv_ref, dv_ref, wv_ref, dval, v_ext, v1, v2)):
            dc = d[:tm]
            ref[0:1, :] += jnp.sum(dc * x2[:tm], axis=0, keepdims=True)
            ref[1:2, :] += jnp.sum(dc * x1[:tm], axis=0, keepdims=True)
            ref[2:3, :] += jnp.sum(dc * x0[:tm], axis=0, keepdims=True)
            ref[3:4, :] += jnp.sum(dc, axis=0, keepdims=True)
            d1 = pltpu.roll(d, n - 1, 0)[:tm]
            d2 = pltpu.roll(d, n - 2, 0)[:tm]
            o_ref[...] = (w_ref[2:3, :] * dc + w_ref[1:2, :] * d1 + w_ref[0:1, :] * d2).astype(BF16)

    acc = pl.BlockSpec((8, FFN_TC), lambda j, i: (0, j))
    full = jax.ShapeDtypeStruct((s, D_FF), BF16)
    accs = jax.ShapeDtypeStruct((8, D_FF), F32)
    return pl.pallas_call(
        body, name=name, grid=(D_FF // FFN_TC, nrow), in_specs=[cur, prev, nxt, cur, prev, nxt, w, w, b, b, cur, nxt],
        out_specs=[cur, cur, acc, acc], out_shape=[full, full, accs, accs],
        compiler_params=_params("parallel", "arbitrary"),
    )(upg, upg, upg, upv, upv, upv, cwg, cwv, cbg, cbv, dact, dact)


def _mesh_place():
    return lax.axis_index("x"), lax.axis_index("y"), lax.axis_index("c")


def _all_gather(shards, name):
    na = len(shards)

    def body(*refs):
        x_refs, out_refs = refs[:na], refs[na:2 * na]
        send_sems, recv_sems, local_sems = refs[2 * na:]
        x, y, cc = _mesh_place()
        me, sibling = (x, y, cc), (x, y, 1 - cc)
        chips = [(1 - x, y), (x, 1 - y), (1 - x, 1 - y)]

        def copy(k, a, block, to, from_input=False):
            slot = out_refs[a].at[4 * block[0] + 2 * block[1] + block[2]]
            return pltpu.make_async_remote_copy(
                src_ref=x_refs[a] if from_input else slot, dst_ref=slot, send_sem=send_sems.at[k * na + a],
                recv_sem=recv_sems.at[k * na + a], device_id=to, device_id_type=pl.DeviceIdType.MESH)

        mine = [pltpu.make_async_copy(x_refs[a], out_refs[a].at[4 * x + 2 * y + cc], local_sems.at[a]) for a in range(na)]
        for cp in mine:
            cp.start()
        first = [copy(0, a, me, sibling, True) for a in range(na)]
        first += [copy(1 + j, a, me, (*chip, cc), True) for j, chip in enumerate(chips) for a in range(na)]
        for cp in first:
            cp.start()
        passed = []
        for j, chip in enumerate(chips):
            for a in range(na):
                copy(1 + j, a, (*chip, cc), me).wait_recv()
                passed.append(copy(4 + j, a, (*chip, cc), sibling))
                passed[-1].start()
        for a in range(na):
            copy(0, a, sibling, me).wait_recv()
        for j, chip in enumerate(chips):
            for a in range(na):
                copy(4 + j, a, (*chip, 1 - cc), me).wait_recv()
        for cp in first + passed:
            cp.wait_send()
        for cp in mine:
            cp.wait()

    hbm = pl.BlockSpec(memory_space=pl.ANY)
    return pl.pallas_call(
        body, name=name, out_shape=[jax.ShapeDtypeStruct((N_DEV,) + t.shape, t.dtype) for t in shards],
        in_specs=[hbm] * na, out_specs=[hbm] * na,
        scratch_shapes=[pltpu.SemaphoreType.DMA((7 * na,)), pltpu.SemaphoreType.DMA((7 * na,)), pltpu.SemaphoreType.DMA((na,))],
    )(*shards)


def _exchange(blocks, name):
    na = len(blocks)

    def body(*refs):
        g_refs, out_refs = refs[:na], refs[na:2 * na]
        send_sems, recv_sems, local_sems = refs[2 * na:]
        x, y, cc = _mesh_place()
        me = 4 * x + 2 * y + cc
        mine = [pltpu.make_async_copy(g_refs[a].at[me], out_refs[a].at[me], local_sems.at[a]) for a in range(na)]
        for cp in mine:
            cp.start()
        sends, lands = [], []
        for k in range(1, N_DEV):
            px = 1 - x if (k >> 2) & 1 else x
            py = 1 - y if (k >> 1) & 1 else y
            pc = 1 - cc if k & 1 else cc
            peer = 4 * px + 2 * py + pc
            for a in range(na):
                sem = (k - 1) * na + a
                sends.append(pltpu.make_async_remote_copy(
                    src_ref=g_refs[a].at[peer], dst_ref=out_refs[a].at[me], send_sem=send_sems.at[sem], recv_sem=recv_sems.at[sem],
                    device_id=(px, py, pc), device_id_type=pl.DeviceIdType.MESH))
                lands.append(pltpu.make_async_remote_copy(
                    src_ref=g_refs[a].at[peer], dst_ref=out_refs[a].at[peer], send_sem=send_sems.at[sem], recv_sem=recv_sems.at[sem],
                    device_id=(px, py, pc), device_id_type=pl.DeviceIdType.MESH))
        for cp in sends:
            cp.start()
        for cp in lands:
            cp.wait_recv()
        for cp in sends:
            cp.wait_send()
        for cp in mine:
            cp.wait()

    hbm = pl.BlockSpec(memory_space=pl.ANY)
    return pl.pallas_call(
        body, name=name, out_shape=[jax.ShapeDtypeStruct(t.shape, t.dtype) for t in blocks],
        in_specs=[hbm] * na, out_specs=[hbm] * na,
        scratch_shapes=[pltpu.SemaphoreType.DMA((7 * na,)), pltpu.SemaphoreType.DMA((7 * na,)), pltpu.SemaphoreType.DMA((na,))],
    )(*blocks)


def _peer(k):
    x, y, cc = _mesh_place()
    px = 1 - x if (k >> 2) & 1 else x
    py = 1 - y if (k >> 1) & 1 else y
    pc = 1 - cc if k & 1 else cc
    return (px, py, pc), 4 * px + 2 * py + pc


def _push_copy(src_ref, land_ref, k, a, na, send_sems, recv_sems, indexed, landed):
    x, y, cc = _mesh_place()
    place, peer = _peer(k)
    sem = (k - 1) * na + a
    return pltpu.make_async_remote_copy(
        src_ref=src_ref.at[peer] if indexed else src_ref, dst_ref=land_ref.at[peer if landed else 4 * x + 2 * y + cc],
        send_sem=send_sems.at[sem], recv_sem=recv_sems.at[sem], device_id=place, device_id_type=pl.DeviceIdType.MESH)


_HBM = pl.BlockSpec(memory_space=pltpu.HBM)
_SEM = pl.BlockSpec(memory_space=pltpu.SEMAPHORE)
_EFFECT = pltpu.SideEffectType.DATAFLOW_SIDE_EFFECTING


def _push_start(srcs, indexed, name):
    na = len(srcs)
    lands = [lax.empty(t.shape if indexed else (N_DEV,) + t.shape, t.dtype) for t in srcs]

    def body(*refs):
        src_refs, land_refs = refs[:na], refs[na:2 * na]
        send_sems, recv_sems = refs[2 * na], refs[2 * na + 1]
        token = refs[-1]
        for k in range(1, N_DEV):
            for a in range(na):
                _push_copy(src_refs[a], land_refs[a], k, a, na, send_sems, recv_sems, indexed, False).start()
        token[...] = jnp.zeros_like(token)

    sems = pltpu.SemaphoreType.DMA((7 * na,))
    out = pl.pallas_call(
        body, name=name,
        out_shape=(sems, sems, *[pltpu.HBM(t.shape, t.dtype) for t in srcs], *[pltpu.HBM(t.shape, t.dtype) for t in lands],
                   jax.ShapeDtypeStruct((8, LANES), F32)),
        in_specs=[_HBM] * (2 * na), out_specs=(_SEM, _SEM, *[_HBM] * (2 * na), pl.BlockSpec(memory_space=pltpu.VMEM)),
        input_output_aliases={i: 2 + i for i in range(2 * na)},
        compiler_params=pltpu.CompilerParams(has_side_effects=_EFFECT),
    )(*[pltpu.with_memory_space_constraint(t, pltpu.HBM) for t in srcs + lands])
    return out[0], out[1], list(out[2:2 + na]), list(out[2 + na:2 + 2 * na]), out[-1]


def _push_wait(started, indexed, after, name):
    send_sems, recv_sems, srcs, lands, _ = started
    na = len(srcs)

    def body(*refs):
        src_refs, land_refs = refs[:na], refs[na:2 * na]
        send_sems, recv_sems = refs[2 * na], refs[2 * na + 1]
        for k in range(1, N_DEV):
            for a in range(na):
                copy = _push_copy(src_refs[a], land_refs[a], k, a, na, send_sems, recv_sems, indexed, True)
                copy.wait_send()
                copy.wait_recv()

    out = pl.pallas_call(
        body, name=name, out_shape=[pltpu.HBM(t.shape, t.dtype) for t in srcs + lands],
        in_specs=[_HBM] * (2 * na) + [_SEM, _SEM, pl.BlockSpec(memory_space=pl.ANY)], out_specs=[_HBM] * (2 * na),
        input_output_aliases={i: i for i in range(2 * na)},
        compiler_params=pltpu.CompilerParams(has_side_effects=_EFFECT),
    )(*srcs, *lands, send_sems, recv_sems, after)
    x, y, cc = _mesh_place()
    me = 4 * x + 2 * y + cc
    return [lax.dynamic_update_index_in_dim(
        land, lax.dynamic_index_in_dim(src, me, 0, keepdims=False) if indexed else src, me, 0)
        for src, land in zip(out[:na], out[na:])]


def _adamw_sum(parts, w, m, v, name):
    _, r, c = parts.shape
    tr = _tile(r, (256, 128, 64, 32, 16, 8))

    def body(p_ref, w_ref, m_ref, v_ref, g_ref, d_ref, nm_ref, nv_ref):
        _adam_store(_sum_parts(p_ref), w_ref, m_ref, v_ref, g_ref, d_ref, nm_ref, nv_ref)

    row = pl.BlockSpec((tr, c), lambda i: (i, 0))
    shp = jax.ShapeDtypeStruct((r, c), F32)
    return pl.pallas_call(
        body, name=name, grid=(r // tr,), in_specs=[pl.BlockSpec((N_DEV, tr, c), lambda i: (0, i, 0)), row, row, row],
        out_specs=[row, row, row, row], out_shape=[shp, shp, shp, shp], compiler_params=_params("parallel"),
    )(parts, w, m, v)


def _sum_parts(p_ref):
    g = p_ref[0].astype(F32)
    for k in range(1, N_DEV):
        g = g + p_ref[k].astype(F32)
    return g


def _adam_store(g, w_ref, m_ref, v_ref, g_ref, d_ref, nm_ref, nv_ref):
    nm = ADAM_B1 * m_ref[...] + (1.0 - ADAM_B1) * g
    nv = ADAM_B2 * v_ref[...] + (1.0 - ADAM_B2) * (g * g)
    m_hat = nm / (1.0 - ADAM_B1 ** ADAM_STEP)
    v_hat = nv / (1.0 - ADAM_B2 ** ADAM_STEP)
    g_ref[...] = g
    nm_ref[...] = nm
    nv_ref[...] = nv
    d_ref[...] = -ADAM_LR * (m_hat / (jnp.sqrt(v_hat) + ADAM_EPS) + ADAM_WD * w_ref[...])


def _adamw_weight(parts, w, m, v, name):
    _, r, c = w.shape
    tr = _tile(r, (256, 128, 176))
    nr = r // tr

    def body(p0_ref, p1_ref, w_ref, m_ref, v_ref, g_ref, d_ref, nm_ref, nv_ref):
        g = jnp.where(pl.program_id(0) == 0, _sum_parts(p0_ref), _sum_parts(p1_ref))
        _adam_store(g, w_ref, m_ref, v_ref, g_ref, d_ref, nm_ref, nv_ref)

    part = lambda layer: pl.BlockSpec(
        (N_DEV, tr, c), lambda l, i: (0, jnp.where(l == layer, i, (nr - 1) * (1 - layer)), 0))
    row = pl.BlockSpec((None, tr, c), lambda l, i: (l, i, 0))
    shp = jax.ShapeDtypeStruct(w.shape, F32)
    return pl.pallas_call(
        body, name=name, grid=(DEPTH, nr), in_specs=[part(0), part(1), row, row, row],
        out_specs=[row, row, row, row], out_shape=[shp, shp, shp, shp], compiler_params=_params("arbitrary", "arbitrary"),
    )(parts[0], parts[1], w, m, v)


def _full_to_slots(name, t):
    k, n = t.shape
    if name in ROW_SHARDED:
        return t.reshape(N_DEV, k // N_DEV, n)
    return t.reshape(k, N_DEV, n // N_DEV).transpose(1, 0, 2)


def _slots_to_full(name, t):
    _, r, c = t.shape
    if name in ROW_SHARDED:
        return t.reshape(N_DEV * r, c)
    return t.transpose(1, 0, 2).reshape(r, N_DEV * c)


def _small_sizes(shapes):
    return [(n, shapes[n], -(-int(math.prod(shapes[n])) // (8 * LANES)) * 8) for n in SMALL]


def _pack_small(tree, shapes):
    rows = []
    for n, shp, nrow in _small_sizes(shapes):
        flat = tree[n].reshape(-1)
        rows.append(jnp.pad(flat, (0, nrow * LANES - flat.shape[0])).reshape(nrow, LANES))
    total = sum(r.shape[0] for r in rows)
    rows.append(jnp.zeros((-total % SMALL_ROW_TILE, LANES), F32))
    return jnp.concatenate(rows, axis=0)


def _unpack_small(buf, shapes):
    out, r0 = {}, 0
    for n, shp, nrow in _small_sizes(shapes):
        out[n] = buf[r0:r0 + nrow].reshape(-1)[:int(math.prod(shp))].reshape(shp)
        r0 += nrow
    return out


def _block_diag(w):
    g = w.shape[0]
    eye = jnp.eye(g, dtype=w.dtype)
    return (eye[:, None, :, None] * w[:, :, None, :]).reshape(g * HEAD_DIM, g * HEAD_DIM)


def kernel(x, positions, norm1, w_in, q_norm, k_norm, sinks, w_pool, pool_scale, sgu_v_norm, w_s, b_s, w_proj_a, w_proj_b, w_proj_c, w_out, norm2, w_up, conv_w, conv_b, w_down, loss_target, m_norm1, m_w_in, m_q_norm, m_k_norm, m_sinks, m_w_pool, m_pool_scale, m_sgu_v_norm, m_w_s, m_b_s, m_w_proj_a, m_w_proj_b, m_w_proj_c, m_w_out, m_norm2, m_w_up, m_conv_w, m_conv_b, m_w_down, v_norm1, v_w_in, v_q_norm, v_k_norm, v_sinks, v_w_pool, v_pool_scale, v_sgu_v_norm, v_w_s, v_b_s, v_w_proj_a, v_w_proj_b, v_w_proj_c, v_w_out, v_norm2, v_w_up, v_conv_w, v_conv_b, v_w_down):
    names = ("norm1", "w_in", "q_norm", "k_norm", "sinks", "w_pool", "pool_scale", "sgu_v_norm", "w_s", "b_s", "w_proj_a",
             "w_proj_b", "w_proj_c", "w_out", "norm2", "w_up", "conv_w", "conv_b", "w_down")
    wts = dict(zip(names, (norm1, w_in, q_norm, k_norm, sinks, w_pool, pool_scale, sgu_v_norm, w_s, b_s, w_proj_a, w_proj_b,
                           w_proj_c, w_out, norm2, w_up, conv_w, conv_b, w_down)))
    mom = dict(zip(names, (m_norm1, m_w_in, m_q_norm, m_k_norm, m_sinks, m_w_pool, m_pool_scale, m_sgu_v_norm, m_w_s, m_b_s,
                           m_w_proj_a, m_w_proj_b, m_w_proj_c, m_w_out, m_norm2, m_w_up, m_conv_w, m_conv_b, m_w_down)))
    var = dict(zip(names, (v_norm1, v_w_in, v_q_norm, v_k_norm, v_sinks, v_w_pool, v_pool_scale, v_sgu_v_norm, v_w_s, v_b_s,
                           v_w_proj_a, v_w_proj_b, v_w_proj_c, v_w_out, v_norm2, v_w_up, v_conv_w, v_conv_b, v_w_down)))
    xs = x[0]
    target = loss_target[0]
    s = xs.shape[0]

    inv_freq = ROPE_THETA ** (-jnp.arange(0, HEAD_DIM, 2, dtype=F32) / HEAD_DIM)
    ang = positions[0].astype(F32)[:, None] * inv_freq
    cosf = jnp.tile(jnp.cos(ang), (1, 4))
    sinf = jnp.tile(jnp.concatenate([-jnp.sin(ang), jnp.sin(ang)], axis=1), (1, 2))

    local = [{n: wts[n][l] if n == "conv_w" else wts[n][l].astype(BF16) for n in SHARDED} for l in range(DEPTH)]
    later = SHARDED[1:]
    full = [{"w_in": _slots_to_full("w_in", _all_gather([local[0]["w_in"]], "gather_w_in_0")[0])}, None]
    gather0 = _push_start([local[0][n] for n in later], False, "gather_rest_0_start")
    norm1_first = norm1[0] + gather0[4][0, 0]

    def layer_consts(l):
        return dict(
            bdw=_block_diag(w_pool[l]).astype(BF16), qn=jnp.tile(q_norm[l], 2).reshape(1, LANES),
            kn=jnp.tile(k_norm[l], 2).reshape(1, LANES), vn=jnp.tile(sgu_v_norm[l], 4).reshape(1, SGU_W),
            bcol=jnp.repeat(b_s[l].T, HEAD_DIM, axis=1),
            cbg=conv_b[l][:D_FF].reshape(1, D_FF), cbv=conv_b[l][D_FF:].reshape(1, D_FF))

    gate_cols, val_cols = (0, D_FF), (D_FF, D_FF)

    saved = []
    cur = xs
    for l in range(DEPTH):
        if l == 1:
            landed = _push_wait(gather1, False, cur, "gather_weights_1_wait")
            full[1] = {n: _slots_to_full(n, t) for n, t in zip(SHARDED, landed)}
        fw, k = full[l], layer_consts(l)
        h1 = _rms_fwd(cur, norm1_first if l == 0 else norm1[l], f"rms1_fwd_{l}")
        z = _mm(h1, fw["w_in"], name=f"in_proj_{l}")
        a = _pool_fwd(z, k["bdw"], pool_scale[l], f"pool_fwd_{l}")
        b = _attn_fwd(z, cosf, sinf, k["qn"], k["kn"], sinks[l], f"attn_fwd_{l}")
        c = _sgu_fwd(z, w_s[l], k["bcol"], k["vn"], f"sgu_fwd_{l}")
        w_proj_a_l = fw.get("w_proj_a")
        if l == 0:
            landed = _push_wait(gather0, False, c, "gather_rest_0_wait")
            fw.update({n: _slots_to_full(n, t) for n, t in zip(later, landed)})
            gather1 = _push_start([local[1][n] for n in SHARDED], False, "gather_weights_1_start")
            w_proj_a_l = fw["w_proj_a"] + gather1[4][0, 0].astype(BF16)
        merged = _merge_fwd(a, b, c, w_proj_a_l, fw["w_proj_b"], fw["w_proj_c"], z, f"merge_fwd_{l}")
        x1 = _mm(merged, fw["w_out"], add=cur, name=f"out_proj_{l}")
        h2 = _rms_fwd(x1, norm2[l], f"rms2_fwd_{l}")
        upg = _mm(h2, fw["w_up"], b_n=gate_cols, name=f"up_gate_{l}")
        upv = _mm(h2, fw["w_up"], b_n=val_cols, name=f"up_val_{l}")
        k["cwg"], k["cwv"] = fw["conv_w"][:, :D_FF], fw["conv_w"][:, D_FF:]
        act = _ffn_act_fwd(upg, upv, k["cwg"], k["cwv"], k["cbg"], k["cbv"], f"ffn_act_fwd_{l}")
        x2 = _mm(act, fw["w_down"], add=x1, name=f"down_proj_{l}")
        saved.append(dict(x0=cur, h1=h1, z=z, a=a, b=b, c=c, merged=merged, x1=x1, h2=h2, upg=upg, upv=upv, act=act))
        cur = x2

    dcur, loss_tile = _loss_head(cur, target)
    loss = lax.psum(loss_tile[0, 0], ("x", "y", "c"))

    gsmall = [None] * DEPTH

    def slots_of(grads):
        return [_full_to_slots(n, t) for n, t in grads.items()]

    for l in reversed(range(DEPTH)):
        fw, k, sv = full[l], layer_consts(l), saved[l]
        k["cwg"], k["cwv"] = fw["conv_w"][:, :D_FF], fw["conv_w"][:, D_FF:]
        staged = l == 0
        wgrad = functools.partial(_mm, ta=True, out_dtype=BF16)
        w_down_l = fw["w_down"] + exchange1[4][0, 0].astype(BF16) if staged else fw["w_down"]
        dact = _mm(dcur, w_down_l, tb=True, name=f"down_proj_bwd_{l}")
        g_down = wgrad(sv["act"], dcur, name=f"down_proj_wgrad_{l}")
        dg0, dv0, dcg, dcv = _ffn_bwd(sv["upg"], sv["upv"], k["cwg"], k["cwv"], k["cbg"], k["cbv"], dact, f"ffn_bwd_{l}")
        dh2 = _mm(dg0, fw["w_up"], tb=True, b_k=gate_cols, name=f"up_gate_bwd_{l}")
        dh2 = _mm(dv0, fw["w_up"], tb=True, b_k=val_cols, add=dh2, name=f"up_val_bwd_{l}")
        g_up = wgrad(sv["h2"], dg0, out_cols=(0, 2 * D_FF), name=f"up_gate_wgrad_{l}")
        g_up = wgrad(sv["h2"], dv0, out_cols=(D_FF, 2 * D_FF), out_into=g_up, name=f"up_val_wgrad_{l}")
        g_ffn = dict(w_up=g_up, w_down=g_down, conv_w=jnp.concatenate([dcg[0:3], dcv[0:3]], axis=1))
        norm2_l = norm2[l]
        if staged:
            parts1 = dict(zip(SHARDED, _push_wait(exchange1, True, g_up, "exchange_grads_1_wait")))
            exchange_ffn = _push_start(slots_of(g_ffn), True, "exchange_ffn_0_start")
            norm2_l = norm2_l + exchange_ffn[4][0, 0]
        dx1, g_norm2 = _rms_bwd(sv["x1"], norm2_l, dh2, dcur, f"rms2_bwd_{l}")
        dmerged = _mm(dx1, fw["w_out"], tb=True, name=f"out_proj_bwd_{l}")
        g_out = wgrad(sv["merged"], dx1, name=f"out_proj_wgrad_{l}")
        dz, dya, da = _branch_bwd(0, sv["a"], fw["w_proj_a"], sv["z"], dmerged, None, f"branch_a_bwd_{l}")
        dz, dyb, db = _branch_bwd(1, sv["b"], fw["w_proj_b"], sv["z"], dmerged, dz, f"branch_b_bwd_{l}")
        dz, dyc, dc = _branch_bwd(2, sv["c"], fw["w_proj_c"], sv["z"], dmerged, dz, f"branch_c_bwd_{l}")
        g_mix = dict(w_proj_a=wgrad(sv["a"], dya, name=f"proj_a_wgrad_{l}"), w_proj_b=wgrad(sv["b"], dyb, name=f"proj_b_wgrad_{l}"),
                     w_proj_c=wgrad(sv["c"], dyc, name=f"proj_c_wgrad_{l}"), w_out=g_out)
        pool_scale_l = pool_scale[l]
        if staged:
            exchange_mix = _push_start(slots_of(g_mix), True, "exchange_mixer_0_start")
            pool_scale_l = pool_scale_l + exchange_mix[4][0, 0]
        dxp, g_bdw, g_pscale = _pool_bwd(sv["z"], da, k["bdw"], pool_scale_l, f"pool_bwd_{l}")
        dq, dkc, dkp, dvc, dvp, g_qn, g_sink = _attn_bwd(sv["z"], cosf, sinf, k["qn"], k["kn"], sinks[l], db, f"attn_bwd_{l}")
        duv, g_ws, g_bacc, g_vn = _sgu_bwd(sv["z"], w_s[l], k["bcol"], k["vn"], dc, f"sgu_bwd_{l}")
        dz, g_kn = _kv_post(sv["z"], cosf, sinf, k["kn"], dkc, dkp, dvc, dvp, dxp, dq, duv, dz, f"kv_post_{l}")
        g_in = dict(w_in=wgrad(sv["h1"], dz, name=f"in_proj_wgrad_{l}"))
        norm1_l = norm1[l]
        if staged:
            exchange_in = _push_start(slots_of(g_in), True, "exchange_w_in_0_start")
            norm1_l = norm1_l + exchange_in[4][0, 0]
        dh1 = _mm(dz, fw["w_in"], tb=True, name=f"in_proj_bwd_{l}")
        dcur, g_norm1 = _rms_bwd(sv["x0"], norm1_l, dh1, dx1, f"rms1_bwd_{l}")
        if not staged:
            exchange1 = _push_start(slots_of({n: {**g_in, **g_mix, **g_ffn}[n] for n in SHARDED}), True, "exchange_grads_1_start")
        gsmall[l] = dict(
            norm1=g_norm1[0], q_norm=g_qn[0, :HEAD_DIM], k_norm=g_kn[0, :HEAD_DIM], sinks=g_sink[:, 0],
            w_pool=jnp.stack([g_bdw[g * HEAD_DIM:(g + 1) * HEAD_DIM, g * HEAD_DIM:(g + 1) * HEAD_DIM] for g in range(4)]),
            pool_scale=g_pscale[0], sgu_v_norm=g_vn[0, :HEAD_DIM], w_s=g_ws, b_s=g_bacc[:, ::HEAD_DIM].T,
            norm2=g_norm2[0], conv_b=jnp.concatenate([dcg[3], dcv[3]]))
    grad_x = dcur[None]

    shapes = {n: wts[n].shape for n in SMALL}
    gs = _pack_small({n: jnp.stack([gsmall[l][n] for l in range(DEPTH)]) for n in SMALL}, shapes)
    gs_all = _all_gather([gs], "gather_small_grads")[0]
    g_s, d_s, m_s, v_s = _adamw_sum(gs_all, _pack_small(wts, shapes), _pack_small(mom, shapes), _pack_small(var, shapes), "adamw_replicated")
    small = [_unpack_small(t, shapes) for t in (g_s, d_s, m_s, v_s)]

    parts0 = dict(zip(g_ffn, _push_wait(exchange_ffn, True, g_s, "exchange_ffn_0_wait")))
    parts0.update(zip(g_mix, _push_wait(exchange_mix, True, g_s, "exchange_mixer_0_wait")))
    update = lambda n: _adamw_weight([parts0[n], parts1[n]], wts[n], mom[n], var[n], f"adamw_{n}")
    big = {n: update(n) for n in SHARDED[1:]}
    parts0.update(zip(g_in, _push_wait(exchange_in, True, big["w_up"][0], "exchange_w_in_0_wait")))
    big["w_in"] = update("w_in")

    outs = [loss, grad_x]
    for kind in range(4):
        outs += [small[kind][n] if n in SMALL else big[n][kind] for n in names]
    return tuple(outs)
```

```python
import functools
import math

import jax
import jax.numpy as jnp
from jax import lax
from jax.experimental import pallas as pl
from jax.experimental.pallas import tpu as pltpu

F32 = jnp.float32
BF16 = jnp.bfloat16

D_MODEL = 1024
DEPTH = 2
HEAD_DIM = 64
N_Q_HEADS = 8
Q_PER_KV = 4
BLOCK = 128
POOL_W = 256
ATTN_W = 512
KV_W = 128
SGU_W = 256
IN_COLS = 4608
GATE_COL0 = 1536
D_FF = 2816
EPS = 1e-6
ROPE_THETA = 10000.0
N_DEV = 8
LANES = 128
HALO_POOL = 16
HALO_CONV = 8

ADAM_LR = 0.001
ADAM_B1 = 0.9
ADAM_B2 = 0.999
ADAM_EPS = 1e-08
ADAM_WD = 0.01
ADAM_STEP = 10

VMEM_LIMIT = 48 * 1024 * 1024

SHARDED = ("w_in", "w_proj_a", "w_proj_b", "w_proj_c", "w_out", "w_up", "w_down", "conv_w")
ROW_SHARDED = ("w_out", "w_down")
SMALL_ROW_TILE = 256
SMALL = ("norm1", "q_norm", "k_norm", "sinks", "w_pool", "pool_scale", "sgu_v_norm", "w_s", "b_s", "norm2", "conv_b")

_GELU_C = math.sqrt(2.0 / math.pi)
_GELU_A = 0.044715


def _params(*sem):
    return pltpu.CompilerParams(dimension_semantics=sem, vmem_limit_bytes=VMEM_LIMIT)


def _tile(n, prefs):
    for t in prefs:
        if t <= n and n % t == 0:
            return t
    return n


def _head_mean_matrix():
    r = lax.broadcasted_iota(jnp.int32, (LANES, LANES), 0)
    c = lax.broadcasted_iota(jnp.int32, (LANES, LANES), 1)
    return jnp.where((r >= HEAD_DIM) == (c >= HEAD_DIM), 1.0 / HEAD_DIM, 0.0).astype(BF16)


def _head_mean(v, bd):
    hi = v.astype(BF16)
    rest = v - hi.astype(F32)
    mid = rest.astype(BF16)
    lo = (rest - mid.astype(F32)).astype(BF16)
    mm = lambda p: jnp.dot(p, bd, preferred_element_type=F32)
    return mm(hi) + (mm(mid) + mm(lo))


def _rot_half(t):
    lane = lax.broadcasted_iota(jnp.int32, t.shape, 1)
    return jnp.where((lane & 32) == 0, pltpu.roll(t, LANES - 32, 1), pltpu.roll(t, 32, 1))


def _norm_rope(t, gn, cosf, sinf, bd):
    r = lax.rsqrt(_head_mean(t * t, bd) + EPS)
    n = t * r
    y = n * gn
    return y * cosf + _rot_half(y) * sinf, n, r


def _norm_rope_bwd(d, t, n, r, gn, cosf, sinf, bd):
    dy = d * cosf + _rot_half(d * sinf)
    dgn = jnp.sum(dy * n, axis=0, keepdims=True)
    u = dy * gn
    dt = r * u - t * (r * r * r) * _head_mean(t * u, bd)
    return dt, dgn


def _gelu(x):
    t = jnp.tanh(_GELU_C * (x + _GELU_A * (x * x * x)))
    return 0.5 * x * (1.0 + t), t


def _gelu_grad(x, t):
    return 0.5 * (1.0 + t) + 0.5 * x * (1.0 - t * t) * (_GELU_C * (1.0 + 3.0 * _GELU_A * x * x))


def _sigmoid(x):
    return jax.nn.sigmoid(x)


def _dot(a, b, ca=1, cb=0):
    return lax.dot_general(a.astype(BF16), b.astype(BF16), (((ca,), (cb,)), ((), ())), preferred_element_type=F32)


def _mm(a, b, *, ta=False, tb=False, add=None, out_dtype=F32, name, b_n=None, b_k=None, out_cols=None, out_into=None):
    m = a.shape[1] if ta else a.shape[0]
    k = a.shape[0] if ta else a.shape[1]
    n = b_n[1] if b_n else (b.shape[0] if tb else b.shape[1])
    tm = _tile(m, (1024, 1408, 512, 256, 128))
    tn = _tile(n, (1024, 1152, 1408, 512, 256, 128))
    tk = _tile(k, (1024, 1152, 1408, 512, 256, 128))
    nk = k // tk
    n0 = b_n[0] // tn if b_n else 0
    k0 = b_k[0] // tk if b_k else 0
    o0, n_out = (out_cols[0] // tn, out_cols[1]) if out_cols else (0, n)
    has_add = add is not None
    n_in = 2 + has_add + (out_into is not None)

    def body(*refs):
        a_ref, b_ref = refs[0], refs[1]
        add_ref = refs[2] if has_add else None
        o_ref = refs[n_in]
        p = _dot(a_ref[...], b_ref[...], 0 if ta else 1, 1 if tb else 0)

        def finish(r):
            if has_add:
                r = r + add_ref[...]
            o_ref[...] = r.astype(out_dtype)

        if nk == 1:
            finish(p)
        else:
            acc_ref = refs[-1]
            kk = pl.program_id(2)

            @pl.when(kk == 0)
            def _():
                acc_ref[...] = p

            @pl.when(kk > 0)
            def _():
                acc_ref[...] += p

            @pl.when(kk == nk - 1)
            def _():
                finish(acc_ref[...])

    a_spec = pl.BlockSpec((tk, tm), lambda i, j, kk: (kk, i)) if ta else pl.BlockSpec((tm, tk), lambda i, j, kk: (i, kk))
    if tb:
        b_spec = pl.BlockSpec((tn, tk), lambda i, j, kk: (j + n0, kk + k0))
    else:
        b_spec = pl.BlockSpec((tk, tn), lambda i, j, kk: (kk + k0, j + n0))
    in_specs = [a_spec, b_spec] + ([pl.BlockSpec((tm, tn), lambda i, j, kk: (i, j))] if has_add else [])
    args = (a, b) + ((add,) if has_add else ())
    if out_into is not None:
        in_specs.append(pl.BlockSpec(memory_space=pl.ANY))
        args += (out_into,)
    return pl.pallas_call(
        body, name=name, grid=(m // tm, n // tn, nk), in_specs=in_specs,
        out_specs=pl.BlockSpec((tm, tn), lambda i, j, kk: (i, j + o0)),
        out_shape=jax.ShapeDtypeStruct((m, n_out), out_dtype),
        scratch_shapes=[pltpu.VMEM((tm, tn), F32)] if nk > 1 else [],
        input_output_aliases={n_in - 1: 0} if out_into is not None else {},
        compiler_params=_params("parallel", "parallel", "arbitrary"),
    )(*args)


def _rms_fwd(x, g, name):
    s, d = x.shape
    tr = _tile(s, (512, 256, 128))

    def body(x_ref, g_ref, h_ref):
        xv = x_ref[...]
        r = lax.rsqrt(jnp.mean(xv * xv, axis=-1, keepdims=True) + EPS)
        h_ref[...] = ((xv * r) * g_ref[...]).astype(BF16)

    return pl.pallas_call(
        body, name=name, grid=(s // tr,),
        in_specs=[pl.BlockSpec((tr, d), lambda i: (i, 0)), pl.BlockSpec((1, d), lambda i: (0, 0))],
        out_specs=pl.BlockSpec((tr, d), lambda i: (i, 0)),
        out_shape=jax.ShapeDtypeStruct((s, d), BF16), compiler_params=_params("parallel"),
    )(x, g.reshape(1, d))


def _rms_bwd(x, g, dh, dres, name):
    s, d = x.shape
    tr = _tile(s, (512, 256, 128))

    def body(x_ref, g_ref, dh_ref, dres_ref, dx_ref, dg_ref):
        xv = x_ref[...]
        r = lax.rsqrt(jnp.mean(xv * xv, axis=-1, keepdims=True) + EPS)
        dhv = dh_ref[...]
        u = dhv * g_ref[...]
        dx_ref[...] = dres_ref[...] + (r * u - xv * (r * r * r) * jnp.mean(xv * u, axis=-1, keepdims=True))
        part = jnp.sum(dhv * (xv * r), axis=0, keepdims=True)

        @pl.when(pl.program_id(0) == 0)
        def _():
            dg_ref[...] = part

        @pl.when(pl.program_id(0) > 0)
        def _():
            dg_ref[...] += part

    row = pl.BlockSpec((tr, d), lambda i: (i, 0))
    vec = pl.BlockSpec((1, d), lambda i: (0, 0))
    return pl.pallas_call(
        body, name=name, grid=(s // tr,), in_specs=[row, vec, row, row], out_specs=[row, vec],
        out_shape=[jax.ShapeDtypeStruct((s, d), F32), jax.ShapeDtypeStruct((1, d), F32)],
        compiler_params=_params("arbitrary"),
    )(x, g.reshape(1, d), dh, dres)


def _loss_head(y, target):
    s, d = y.shape
    tr = _tile(s, (512, 256, 128))

    def body(y_ref, t_ref, dy_ref, l_ref):
        err = y_ref[...] - t_ref[...]
        dy_ref[...] = err * (1.0 / d)
        part = jnp.sum(jnp.sum(err * err, axis=-1, keepdims=True) * (1.0 / d), axis=0, keepdims=True) * 0.5
        part = jnp.broadcast_to(part, (8, LANES))

        @pl.when(pl.program_id(0) == 0)
        def _():
            l_ref[...] = part

        @pl.when(pl.program_id(0) > 0)
        def _():
            l_ref[...] += part

    row = pl.BlockSpec((tr, d), lambda i: (i, 0))
    acc = pl.BlockSpec((8, LANES), lambda i: (0, 0))
    return pl.pallas_call(
        body, name="loss_head", grid=(s // tr,), in_specs=[row, row], out_specs=[row, acc],
        out_shape=[jax.ShapeDtypeStruct((s, d), F32), jax.ShapeDtypeStruct((8, LANES), F32)],
        compiler_params=_params("arbitrary"),
    )(y, target)


def _pool_lane_select(lane, v2, v4, v8, v16):
    return jnp.where(lane < 64, v2, jnp.where(lane < 128, v4, jnp.where(lane < 192, v8, v16)))


def _pool_diff(xc, xp, row0):
    n = BLOCK + HALO_POOL
    cat = jnp.concatenate([xp, xc], axis=0)
    s2 = cat + pltpu.roll(cat, 1, 0)
    s4 = s2 + pltpu.roll(s2, 2, 0)
    s8 = s4 + pltpu.roll(s4, 4, 0)
    s16 = s8 + pltpu.roll(s8, 8, 0)
    lane = lax.broadcasted_iota(jnp.int32, (n, POOL_W), 1)
    wsum = _pool_lane_select(lane, s2, s4, s8, s16)[HALO_POOL:]
    return wsum / _pool_count(row0, BLOCK) - xc


def _pool_count(row0, rows):
    lane = lax.broadcasted_iota(jnp.int32, (rows, POOL_W), 1)
    t = lax.broadcasted_iota(jnp.int32, (rows, POOL_W), 0) + row0
    return jnp.minimum(t + 1, _pool_lane_select(lane, 2, 4, 8, 16)).astype(F32)


def _pool_fwd(z, bdw, scale, name):
    s = z.shape[0]
    nb = s // BLOCK

    def body(xc_ref, xp_ref, w_ref, sc_ref, a_ref):
        i = pl.program_id(0)
        xp = jnp.where(i > 0, xp_ref[...], 0.0)
        diff = _pool_diff(xc_ref[...], xp, i * BLOCK)
        a_ref[...] = (_dot(diff, w_ref[...]) * sc_ref[...]).astype(BF16)

    return pl.pallas_call(
        body, name=name, grid=(nb,),
        in_specs=[pl.BlockSpec((BLOCK, POOL_W), lambda i: (i, 0)),
                  pl.BlockSpec((HALO_POOL, POOL_W), lambda i: (jnp.maximum(i * (BLOCK // HALO_POOL) - 1, 0), 0)),
                  pl.BlockSpec((POOL_W, POOL_W), lambda i: (0, 0)),
                  pl.BlockSpec((1, POOL_W), lambda i: (0, 0))],
        out_specs=pl.BlockSpec((BLOCK, POOL_W), lambda i: (i, 0)),
        out_shape=jax.ShapeDtypeStruct((s, POOL_W), BF16), compiler_params=_params("parallel"),
    )(z, z, bdw, scale.reshape(1, POOL_W))


def _pool_bwd(z, da, bdw, scale, name):
    s = z.shape[0]
    nb = s // BLOCK
    per = BLOCK // HALO_POOL
    n = BLOCK + HALO_POOL

    def body(xc_ref, xp_ref, dac_ref, dan_ref, w_ref, sc_ref, dx_ref, dw_ref, dsc_ref):
        i = pl.program_id(0)
        xp = jnp.where(i > 0, xp_ref[...], 0.0)
        diff = _pool_diff(xc_ref[...], xp, i * BLOCK)
        mixed = _dot(diff, w_ref[...])
        dac = dac_ref[...]
        dan = jnp.where(i < nb - 1, dan_ref[...], 0.0)
        dmix = jnp.concatenate([dac, dan], axis=0) * sc_ref[...]
        ddiff = _dot(dmix, w_ref[...], 1, 1)
        e = ddiff / _pool_count(i * BLOCK, n)
        f2 = e + pltpu.roll(e, n - 1, 0)
        f4 = f2 + pltpu.roll(f2, n - 2, 0)
        f8 = f4 + pltpu.roll(f4, n - 4, 0)
        f16 = f8 + pltpu.roll(f8, n - 8, 0)
        lane = lax.broadcasted_iota(jnp.int32, (n, POOL_W), 1)
        back = _pool_lane_select(lane, f2, f4, f8, f16)
        dx_ref[...] = (back[:BLOCK] - ddiff[:BLOCK]).astype(BF16)
        dw = _dot(diff, dmix[:BLOCK], 0, 0)
        dsc = jnp.sum(dac * mixed, axis=0, keepdims=True)

        @pl.when(i == 0)
        def _():
            dw_ref[...] = dw
            dsc_ref[...] = dsc

        @pl.when(i > 0)
        def _():
            dw_ref[...] += dw
            dsc_ref[...] += dsc

    blk = pl.BlockSpec((BLOCK, POOL_W), lambda i: (i, 0))
    return pl.pallas_call(
        body, name=name, grid=(nb,),
        in_specs=[blk, pl.BlockSpec((HALO_POOL, POOL_W), lambda i: (jnp.maximum(i * per - 1, 0), 0)),
                  blk, pl.BlockSpec((HALO_POOL, POOL_W), lambda i: (jnp.minimum((i + 1) * per, nb * per - 1), 0)),
                  pl.BlockSpec((POOL_W, POOL_W), lambda i: (0, 0)), pl.BlockSpec((1, POOL_W), lambda i: (0, 0))],
        out_specs=[blk, pl.BlockSpec((POOL_W, POOL_W), lambda i: (0, 0)), pl.BlockSpec((1, POOL_W), lambda i: (0, 0))],
        out_shape=[jax.ShapeDtypeStruct((s, POOL_W), BF16), jax.ShapeDtypeStruct((POOL_W, POOL_W), F32),
                   jax.ShapeDtypeStruct((1, POOL_W), F32)],
        compiler_params=_params("arbitrary"),
    )(z, z, da, da, bdw, scale.reshape(1, POOL_W))


def _attn_setup(zc_ref, zp_ref, cc_ref, cp_ref, sc_ref, sp_ref, qn_ref, kn_ref, bd):
    q = []
    for j in range(ATTN_W // LANES):
        t = zc_ref[:, POOL_W + j * LANES:POOL_W + (j + 1) * LANES]
        q.append((t,) + _norm_rope(t, qn_ref[...], cc_ref[...], sc_ref[...], bd))
    kc_raw = zc_ref[:, POOL_W + ATTN_W:POOL_W + ATTN_W + KV_W]
    kc = _norm_rope(kc_raw, kn_ref[...], cc_ref[...], sc_ref[...], bd)[0]
    kp = _norm_rope(zp_ref[:, :KV_W], kn_ref[...], cp_ref[...], sp_ref[...], bd)[0]
    kband = jnp.concatenate([kp, kc], axis=0).astype(BF16)
    vband = jnp.concatenate([zp_ref[:, KV_W:], zc_ref[:, POOL_W + ATTN_W + KV_W:POOL_W + ATTN_W + 2 * KV_W]], axis=0).astype(BF16)
    return q, kband, vband


def _attn_mask(i):
    row = lax.broadcasted_iota(jnp.int32, (Q_PER_KV * BLOCK, 2 * BLOCK), 0) & (BLOCK - 1)
    col = lax.broadcasted_iota(jnp.int32, (Q_PER_KV * BLOCK, 2 * BLOCK), 1)
    dist = row + BLOCK - col
    return (dist >= 0) & (dist < BLOCK) & ((col >= BLOCK) | (i > 0))


def _stack_heads(tiles, kvh):
    return jnp.concatenate([_to_kv_lanes(tiles[h // 2], h) for h in range(kvh * Q_PER_KV, (kvh + 1) * Q_PER_KV)], axis=0)


def _unstack_heads(stacked, kvh, tiles):
    for g in range(Q_PER_KV):
        h = kvh * Q_PER_KV + g
        t = _from_kv_lanes(stacked[g * BLOCK:(g + 1) * BLOCK], h)
        tiles[h // 2] = t if tiles[h // 2] is None else tiles[h // 2] + t


def _sink_column(sink_ref, kvh):
    grp = lax.broadcasted_iota(jnp.int32, (Q_PER_KV * BLOCK, 1), 0) >> 7
    s = [sink_ref[kvh * Q_PER_KV + g] for g in range(Q_PER_KV)]
    return jnp.where(grp == 0, s[0], jnp.where(grp == 1, s[1], jnp.where(grp == 2, s[2], s[3])))


def _to_kv_lanes(t, h):
    kvh = h // Q_PER_KV
    if (h % 2) != kvh:
        t = pltpu.roll(t, HEAD_DIM, 1)
    lane = lax.broadcasted_iota(jnp.int32, t.shape, 1)
    return jnp.where((lane >= HEAD_DIM) == (kvh == 1), t, 0.0)


def _from_kv_lanes(t, h):
    kvh = h // Q_PER_KV
    lane = lax.broadcasted_iota(jnp.int32, t.shape, 1)
    t = jnp.where((lane >= HEAD_DIM) == (kvh == 1), t, 0.0)
    if (h % 2) != kvh:
        t = pltpu.roll(t, HEAD_DIM, 1)
    return t


def _attn_probs(qh, kband, mask, sink):
    sc = _dot(qh, kband, 1, 1) * (HEAD_DIM ** -0.5)
    sc = jnp.where(mask, sc, -1e30)
    m = jnp.maximum(jnp.max(sc, axis=1, keepdims=True), sink)
    p = jnp.exp(sc - m)
    psink = jnp.exp(sink - m)
    den = jnp.sum(p, axis=1, keepdims=True) + psink
    return p / den, psink / den


def _attn_specs(nb):
    cur = lambda i: (i, 0)
    prev = lambda i: (jnp.maximum(i - 1, 0), 0)
    tab = lambda f: pl.BlockSpec((BLOCK, LANES), f)
    vec = pl.BlockSpec((1, LANES), lambda i: (0, 0))
    return [pl.BlockSpec((BLOCK, 1024), cur),
            pl.BlockSpec((BLOCK, 2 * KV_W), lambda i: (jnp.maximum(i - 1, 0), 3)),
            tab(cur), tab(prev), tab(cur), tab(prev), vec, vec,
            pl.BlockSpec(memory_space=pltpu.SMEM)]


def _attn_fwd(z, cosf, sinf, qn, kn, sinks, name):
    s = z.shape[0]
    nb = s // BLOCK

    def body(zc_ref, zp_ref, cc_ref, cp_ref, sc_ref, sp_ref, qn_ref, kn_ref, sink_ref, o_ref):
        i = pl.program_id(0)
        bd = _head_mean_matrix()
        q, kband, vband = _attn_setup(zc_ref, zp_ref, cc_ref, cp_ref, sc_ref, sp_ref, qn_ref, kn_ref, bd)
        mask = _attn_mask(i)
        out = [None] * (ATTN_W // LANES)
        for kvh in range(N_Q_HEADS // Q_PER_KV):
            qs = _stack_heads([t[1] for t in q], kvh)
            probs, _ = _attn_probs(qs, kband, mask, _sink_column(sink_ref, kvh))
            _unstack_heads(_dot(probs, vband), kvh, out)
        for j, o in enumerate(out):
            o_ref[:, j * LANES:(j + 1) * LANES] = o.astype(BF16)

    return pl.pallas_call(
        body, name=name, grid=(nb,), in_specs=_attn_specs(nb),
        out_specs=pl.BlockSpec((BLOCK, ATTN_W), lambda i: (i, 0)),
        out_shape=jax.ShapeDtypeStruct((s, ATTN_W), BF16), compiler_params=_params("parallel"),
    )(z, z, cosf, cosf, sinf, sinf, qn, kn, sinks)


def _attn_bwd(z, cosf, sinf, qn, kn, sinks, d_out, name):
    s = z.shape[0]
    nb = s // BLOCK
    nt = ATTN_W // LANES

    def body(zc_ref, zp_ref, cc_ref, cp_ref, sc_ref, sp_ref, qn_ref, kn_ref, sink_ref, do_ref,
             dq_ref, dkc_ref, dkp_ref, dvc_ref, dvp_ref, dqn_ref, dsink_ref):
        i = pl.program_id(0)
        bd = _head_mean_matrix()
        q, kband, vband = _attn_setup(zc_ref, zp_ref, cc_ref, cp_ref, sc_ref, sp_ref, qn_ref, kn_ref, bd)
        mask = _attn_mask(i)

        @pl.when(i == 0)
        def _():
            dqn_ref[...] = jnp.zeros_like(dqn_ref)
            dsink_ref[...] = jnp.zeros_like(dsink_ref)

        dq = [None] * nt
        dk = jnp.zeros((2 * BLOCK, KV_W), F32)
        dv = jnp.zeros((2 * BLOCK, KV_W), F32)
        d_tiles = [do_ref[:, j * LANES:(j + 1) * LANES] for j in range(nt)]
        for kvh in range(N_Q_HEADS // Q_PER_KV):
            qs = _stack_heads([t[1] for t in q], kvh)
            probs, psink = _attn_probs(qs, kband, mask, _sink_column(sink_ref, kvh))
            dos = _stack_heads(d_tiles, kvh)
            dp = _dot(dos, vband, 1, 1)
            delta = jnp.sum(dp * probs, axis=1, keepdims=True)
            ds = (probs * (dp - delta)) * (HEAD_DIM ** -0.5)
            dsink = -psink * delta
            for g in range(Q_PER_KV):
                h = kvh * Q_PER_KV + g
                dsink_ref[h:h + 1, :] += jnp.broadcast_to(jnp.sum(dsink[g * BLOCK:(g + 1) * BLOCK], axis=0, keepdims=True), (1, LANES))
            _unstack_heads(_dot(ds, kband), kvh, dq)
            dk = dk + _dot(ds, qs, 0, 0)
            dv = dv + _dot(probs, dos, 0, 0)
        dgn = jnp.zeros((1, LANES), F32)
        for j in range(nt):
            t, _, n, r = q[j]
            dt, g = _norm_rope_bwd(dq[j], t, n, r, qn_ref[...], cc_ref[...], sc_ref[...], bd)
            dq_ref[:, j * LANES:(j + 1) * LANES] = dt.astype(BF16)
            dgn = dgn + g
        dqn_ref[...] += jnp.broadcast_to(dgn, (8, LANES))
        dkp_ref[...] = dk[:BLOCK]
        dkc_ref[...] = dk[BLOCK:]
        dvp_ref[...] = dv[:BLOCK]
        dvc_ref[...] = dv[BLOCK:]

        @pl.when(i == nb - 1)
        def _():
            acc = dqn_ref[...]
            dqn_ref[...] = acc + pltpu.roll(acc, HEAD_DIM, 1)

    blk = lambda w: pl.BlockSpec((BLOCK, w), lambda i: (i, 0))
    acc = pl.BlockSpec((8, LANES), lambda i: (0, 0))
    kv = jax.ShapeDtypeStruct((s, KV_W), F32)
    return pl.pallas_call(
        body, name=name, grid=(nb,), in_specs=_attn_specs(nb) + [blk(ATTN_W)],
        out_specs=[blk(ATTN_W), blk(KV_W), blk(KV_W), blk(KV_W), blk(KV_W), acc, acc],
        out_shape=[jax.ShapeDtypeStruct((s, ATTN_W), BF16), kv, kv, kv, kv,
                   jax.ShapeDtypeStruct((8, LANES), F32), jax.ShapeDtypeStruct((8, LANES), F32)],
        compiler_params=_params("arbitrary"),
    )(z, z, cosf, cosf, sinf, sinf, qn, kn, sinks, d_out)


def _kv_post(z, cosf, sinf, kn, dkc, dkp, dvc, dvp, dxp, dq, duv, dz, name):
    s = z.shape[0]
    nb = s // BLOCK

    def body(zk_ref, c_ref, s_ref, kn_ref, dkc_ref, dkp_ref, dvc_ref, dvp_ref, dxp_ref, dq_ref, duv_ref, dz_in,
             dz_ref, dkn_ref):
        j = pl.program_id(0)
        bd = _head_mean_matrix()
        last = j == nb - 1
        d = dkc_ref[...] + jnp.where(last, 0.0, dkp_ref[...])
        t = zk_ref[:, :KV_W]
        _, n, r = _norm_rope(t, kn_ref[...], c_ref[...], s_ref[...], bd)
        dt, g = _norm_rope_bwd(d, t, n, r, kn_ref[...], c_ref[...], s_ref[...], bd)
        dvv = dvc_ref[...] + jnp.where(last, 0.0, dvp_ref[...])
        dz_ref[:, 0:POOL_W] = dxp_ref[...]
        dz_ref[:, POOL_W:POOL_W + ATTN_W] = dq_ref[...]
        dz_ref[:, POOL_W + ATTN_W:POOL_W + ATTN_W + KV_W] = dt.astype(BF16)
        dz_ref[:, POOL_W + ATTN_W + KV_W:POOL_W + ATTN_W + 2 * KV_W] = dvv.astype(BF16)
        dz_ref[:, POOL_W + ATTN_W + 2 * KV_W:GATE_COL0] = duv_ref[...]

        @pl.when(j == 0)
        def _():
            dkn_ref[...] = jnp.zeros_like(dkn_ref)

        dkn_ref[...] += jnp.broadcast_to(g, (8, LANES))

        @pl.when(last)
        def _():
            acc = dkn_ref[...]
            dkn_ref[...] = acc + pltpu.roll(acc, HEAD_DIM, 1)

    cur = lambda w: pl.BlockSpec((BLOCK, w), lambda j: (j, 0))
    nxt = pl.BlockSpec((BLOCK, KV_W), lambda j: (jnp.minimum(j + 1, nb - 1), 0))
    vec = pl.BlockSpec((1, LANES), lambda j: (0, 0))
    return pl.pallas_call(
        body, name=name, grid=(nb,),
        in_specs=[pl.BlockSpec((BLOCK, 2 * KV_W), lambda j: (j, 3)), cur(LANES), cur(LANES), vec,
                  cur(KV_W), nxt, cur(KV_W), nxt, cur(POOL_W), cur(ATTN_W), cur(2 * SGU_W),
                  pl.BlockSpec(memory_space=pl.ANY)],
        out_specs=[pl.BlockSpec((BLOCK, GATE_COL0), lambda j: (j, 0)), pl.BlockSpec((8, LANES), lambda j: (0, 0))],
        out_shape=[jax.ShapeDtypeStruct(dz.shape, dz.dtype), jax.ShapeDtypeStruct((8, LANES), F32)],
        input_output_aliases={11: 0}, compiler_params=_params("arbitrary"),
    )(z, cosf, sinf, kn, dkc, dkp, dvc, dvp, dxp, dq, duv, dz)


def _sgu_setup(z_ref, ws_ref, vn_ref, bd):
    us = z_ref[:, :SGU_W]
    vs = z_ref[:, SGU_W:]
    ug, tu = _gelu(us)
    gv, tv = _gelu(vs)
    rr = jnp.concatenate([lax.rsqrt(_head_mean(gv[:, k * LANES:(k + 1) * LANES] ** 2, bd) + EPS) for k in range(2)], axis=1)
    vg = (gv * rr) * vn_ref[...]
    tril = lax.broadcasted_iota(jnp.int32, (BLOCK, BLOCK), 0) >= lax.broadcasted_iota(jnp.int32, (BLOCK, BLOCK), 1)
    w = [jnp.where(tril, ws_ref[g], 0.0).astype(BF16) for g in range(4)]
    return us, vs, ug, tu, gv, tv, rr, vg, w, tril


def _group_select(parts):
    lane = lax.broadcasted_iota(jnp.int32, parts[0].shape, 1)
    return _pool_lane_select(lane, *parts)


def _sgu_fwd(z, ws, bcol, vn, name):
    s = z.shape[0]
    nb = s // BLOCK

    def body(z_ref, ws_ref, b_ref, vn_ref, c_ref):
        bd = _head_mean_matrix()
        _, _, ug, _, _, _, _, vg, w, _ = _sgu_setup(z_ref, ws_ref, vn_ref, bd)
        sg = _group_select([_dot(w[g], vg) for g in range(4)]) + b_ref[...]
        c_ref[...] = (ug * sg).astype(BF16)

    return pl.pallas_call(
        body, name=name, grid=(nb,),
        in_specs=[pl.BlockSpec((BLOCK, 2 * SGU_W), lambda i: (i, 2)), pl.BlockSpec((4, BLOCK, BLOCK), lambda i: (0, 0, 0)),
                  pl.BlockSpec((BLOCK, SGU_W), lambda i: (0, 0)), pl.BlockSpec((1, SGU_W), lambda i: (0, 0))],
        out_specs=pl.BlockSpec((BLOCK, SGU_W), lambda i: (i, 0)),
        out_shape=jax.ShapeDtypeStruct((s, SGU_W), BF16), compiler_params=_params("parallel"),
    )(z, ws, bcol, vn)


def _sgu_bwd(z, ws, bcol, vn, dc, name):
    s = z.shape[0]
    nb = s // BLOCK

    def body(z_ref, ws_ref, b_ref, vn_ref, dc_ref, duv_ref, dws_ref, db_ref, dvn_ref):
        i = pl.program_id(0)
        bd = _head_mean_matrix()
        us, vs, ug, tu, gv, tv, rr, vg, w, tril = _sgu_setup(z_ref, ws_ref, vn_ref, bd)
        sg = _group_select([_dot(w[g], vg) for g in range(4)]) + b_ref[...]
        dcv = dc_ref[...]
        dug = dcv * sg
        dsg = dcv * ug
        lane = lax.broadcasted_iota(jnp.int32, dsg.shape, 1)

        @pl.when(i == 0)
        def _():
            dws_ref[...] = jnp.zeros_like(dws_ref)
            db_ref[...] = jnp.zeros_like(db_ref)
            dvn_ref[...] = jnp.zeros_like(dvn_ref)

        for g in range(4):
            dsg_g = jnp.where((lane >= g * HEAD_DIM) & (lane < (g + 1) * HEAD_DIM), dsg, 0.0)
            dws_ref[g] += jnp.where(tril, _dot(dsg_g, vg, 1, 1), 0.0)
        dvg = _group_select([_dot(w[g], dsg, 0, 0) for g in range(4)])
        db_ref[...] += dsg
        n = gv * rr
        part = jnp.sum(dvg * n, axis=0, keepdims=True)
        dvn_ref[...] += jnp.broadcast_to(part[:, :LANES] + part[:, LANES:], (8, LANES))
        u = dvg * vn_ref[...]
        tu_ = gv * u
        mean = jnp.concatenate([_head_mean(tu_[:, k * LANES:(k + 1) * LANES], bd) for k in range(2)], axis=1)
        dgv = rr * u - gv * (rr * rr * rr) * mean
        duv_ref[:, :SGU_W] = (dug * _gelu_grad(us, tu)).astype(BF16)
        duv_ref[:, SGU_W:] = (dgv * _gelu_grad(vs, tv)).astype(BF16)

        @pl.when(i == nb - 1)
        def _():
            acc = dvn_ref[...]
            dvn_ref[...] = acc + pltpu.roll(acc, HEAD_DIM, 1)
            for k in range(2):
                db_ref[:, k * LANES:(k + 1) * LANES] = _head_mean(db_ref[:, k * LANES:(k + 1) * LANES], bd) * float(HEAD_DIM)

    return pl.pallas_call(
        body, name=name, grid=(nb,),
        in_specs=[pl.BlockSpec((BLOCK, 2 * SGU_W), lambda i: (i, 2)), pl.BlockSpec((4, BLOCK, BLOCK), lambda i: (0, 0, 0)),
                  pl.BlockSpec((BLOCK, SGU_W), lambda i: (0, 0)), pl.BlockSpec((1, SGU_W), lambda i: (0, 0)),
                  pl.BlockSpec((BLOCK, SGU_W), lambda i: (i, 0))],
        out_specs=[pl.BlockSpec((BLOCK, 2 * SGU_W), lambda i: (i, 0)), pl.BlockSpec((4, BLOCK, BLOCK), lambda i: (0, 0, 0)),
                   pl.BlockSpec((BLOCK, SGU_W), lambda i: (0, 0)), pl.BlockSpec((8, LANES), lambda i: (0, 0))],
        out_shape=[jax.ShapeDtypeStruct((s, 2 * SGU_W), BF16), jax.ShapeDtypeStruct((4, BLOCK, BLOCK), F32),
                   jax.ShapeDtypeStruct((BLOCK, SGU_W), F32), jax.ShapeDtypeStruct((8, LANES), F32)],
        compiler_params=_params("arbitrary"),
    )(z, ws, bcol, vn, dc)


MERGE_TN = 512
MERGE_TM = 512


def _merge_fwd(a, b, c, wpa, wpb, wpc, z, name):
    s = z.shape[0]
    tm = _tile(s, (MERGE_TM, BLOCK))
    gate0 = GATE_COL0 // MERGE_TN

    def body(a_ref, b_ref, c_ref, wa_ref, wb_ref, wc_ref, g0_ref, g1_ref, g2_ref, o_ref):
        r = _sigmoid(g0_ref[...]) * _dot(a_ref[...], wa_ref[...])
        r = r + _sigmoid(g1_ref[...]) * _dot(b_ref[...], wb_ref[...])
        r = r + _sigmoid(g2_ref[...]) * _dot(c_ref[...], wc_ref[...])
        o_ref[...] = r.astype(BF16)

    x_spec = lambda w: pl.BlockSpec((tm, w), lambda i, n: (i, 0))
    w_spec = lambda w: pl.BlockSpec((w, MERGE_TN), lambda i, n: (0, n))
    g_spec = lambda br: pl.BlockSpec((tm, MERGE_TN), lambda i, n: (i, gate0 + 2 * br + n))
    return pl.pallas_call(
        body, name=name, grid=(s // tm, D_MODEL // MERGE_TN),
        in_specs=[x_spec(POOL_W), x_spec(ATTN_W), x_spec(SGU_W), w_spec(POOL_W), w_spec(ATTN_W), w_spec(SGU_W),
                  g_spec(0), g_spec(1), g_spec(2)],
        out_specs=pl.BlockSpec((tm, MERGE_TN), lambda i, n: (i, n)),
        out_shape=jax.ShapeDtypeStruct((s, D_MODEL), BF16), compiler_params=_params("parallel", "parallel"),
    )(a, b, c, wpa, wpb, wpc, z, z, z)


def _branch_bwd(br, xb, wp, z, dm, dz, name):
    s = z.shape[0]
    kb = xb.shape[1]
    tm = _tile(s, (MERGE_TM, BLOCK))
    gate0 = GATE_COL0 // MERGE_TN
    aliased = dz is not None

    def body(*refs):
        x_ref, w_ref, g_ref, dm_ref = refs[:4]
        dz_ref, dy_ref, dx_ref = refs[-3:]
        n = pl.program_id(1)
        y = _dot(x_ref[...], w_ref[...])
        sg = _sigmoid(g_ref[...])
        dmv = dm_ref[...]
        dy = (dmv * sg).astype(BF16)
        dy_ref[...] = dy
        dz_ref[...] = ((dmv * y) * (sg * (1.0 - sg))).astype(BF16)
        dx = _dot(dy, w_ref[...], 1, 1)

        @pl.when(n == 0)
        def _():
            dx_ref[...] = dx

        @pl.when(n > 0)
        def _():
            dx_ref[...] += dx

    in_specs = [pl.BlockSpec((tm, kb), lambda i, n: (i, 0)), pl.BlockSpec((kb, MERGE_TN), lambda i, n: (0, n)),
                pl.BlockSpec((tm, MERGE_TN), lambda i, n: (i, gate0 + 2 * br + n)),
                pl.BlockSpec((tm, MERGE_TN), lambda i, n: (i, n))]
    args = [xb, wp, z, dm]
    if aliased:
        in_specs.append(pl.BlockSpec(memory_space=pl.ANY))
        args.append(dz)
    return pl.pallas_call(
        body, name=name, grid=(s // tm, D_MODEL // MERGE_TN), in_specs=in_specs,
        out_specs=[pl.BlockSpec((tm, MERGE_TN), lambda i, n: (i, gate0 + 2 * br + n)),
                   pl.BlockSpec((tm, MERGE_TN), lambda i, n: (i, n)),
                   pl.BlockSpec((tm, kb), lambda i, n: (i, 0))],
        out_shape=[jax.ShapeDtypeStruct((s, IN_COLS), BF16), jax.ShapeDtypeStruct((s, D_MODEL), BF16),
                   jax.ShapeDtypeStruct((s, kb), F32)],
        input_output_aliases={4: 0} if aliased else {},
        compiler_params=_params("parallel", "arbitrary"),
    )(*args)


FFN_TM = 256
FFN_TC = 2816
FFN_STRIP = 256
FFN_UNROLL = 4


def _conv3(cur, prev, w_ref, b_ref):
    cat = jnp.concatenate([prev, cur], axis=0)
    x1 = pltpu.roll(cat, 1, 0)[HALO_CONV:]
    x2 = pltpu.roll(cat, 2, 0)[HALO_CONV:]
    return w_ref[0:1, :] * x2 + w_ref[1:2, :] * x1 + w_ref[2:3, :] * cur + b_ref[...], x1, x2


def _ffn_specs(s, tm, rows_first):
    per = tm // HALO_CONV
    if rows_first:
        cur = pl.BlockSpec((tm, FFN_TC), lambda i, j: (i, j))
        prev = pl.BlockSpec((HALO_CONV, FFN_TC), lambda i, j: (jnp.maximum(i * per - 1, 0), j))
        w = pl.BlockSpec((3, FFN_TC), lambda i, j: (0, j))
        b = pl.BlockSpec((1, FFN_TC), lambda i, j: (0, j))
    else:
        cur = pl.BlockSpec((tm, FFN_TC), lambda j, i: (i, j))
        prev = pl.BlockSpec((HALO_CONV, FFN_TC), lambda j, i: (jnp.maximum(i * per - 1, 0), j))
        w = pl.BlockSpec((3, FFN_TC), lambda j, i: (0, j))
        b = pl.BlockSpec((1, FFN_TC), lambda j, i: (0, j))
    return cur, prev, w, b


def _ffn_act_fwd(upg, upv, cwg, cwv, cbg, cbv, name):
    s = upg.shape[0]
    tm = _tile(s, (FFN_TM, BLOCK))
    cur, prev, w, b = _ffn_specs(s, tm, True)

    def body(g_ref, gp_ref, v_ref, vp_ref, wg_ref, wv_ref, bg_ref, bv_ref, o_ref):
        first = pl.program_id(0) == 0
        gate = _conv3(g_ref[...], jnp.where(first, 0.0, gp_ref[...]), wg_ref, bg_ref)[0]
        val = _conv3(v_ref[...], jnp.where(first, 0.0, vp_ref[...]), wv_ref, bv_ref)[0]
        o_ref[...] = ((gate * _sigmoid(gate)) * val).astype(BF16)

    return pl.pallas_call(
        body, name=name, grid=(s // tm, D_FF // FFN_TC), in_specs=[cur, prev, cur, prev, w, w, b, b], out_specs=cur,
        out_shape=jax.ShapeDtypeStruct((s, D_FF), BF16), compiler_params=_params("parallel", "parallel"),
    )(upg, upg, upv, upv, cwg, cwv, cbg, cbv)


def _ffn_bwd(upg, upv, cwg, cwv, cbg, cbv, dact, name):
    s = upg.shape[0]
    tm = _tile(s, (FFN_TM, BLOCK))
    per = tm // HALO_CONV
    nrow = s // tm
    n = tm + HALO_CONV
    cur, prev, w, b = _ffn_specs(s, tm, False)
    nxt = pl.BlockSpec((HALO_CONV, FFN_TC), lambda j, i: (jnp.minimum((i + 1) * per, nrow * per - 1), j))

    nch = tm // 8
    rows8 = lambda r: pl.ds(pl.multiple_of(r * 8, 8), 8)

    def body(g_ref, gp_ref, gn_ref, v_ref, vp_ref, vn_ref, wg_ref, wv_ref, bg_ref, bv_ref, da_ref, dan_ref,
             dg_ref, dv_ref, dwg_ref, dwv_ref, og_ref, ov_ref):
        i = pl.program_id(1)
        first = i == 0
        last = i == nrow - 1
        row = lax.broadcasted_iota(jnp.int32, (8, FFN_STRIP), 0)

        keep_down = {k: row >= k for k in (1, 2)}
        keep_up = {k: row < 8 - k for k in (1, 2)}

        def down(cur, prev, k):
            return jnp.where(keep_down[k], pltpu.roll(cur, k, 0), pltpu.roll(prev, k, 0))

        def up(cur, nxt, k):
            return jnp.where(keep_up[k], pltpu.roll(cur, 8 - k, 0), pltpu.roll(nxt, 8 - k, 0))

        @pl.when(first)
        def _():
            dwg_ref[...] = jnp.zeros_like(dwg_ref)
            dwv_ref[...] = jnp.zeros_like(dwv_ref)

        for c in range(FFN_TC // FFN_STRIP):
            cols = slice(c * FFN_STRIP, (c + 1) * FFN_STRIP)
            wg = [functools.partial(lambda k: wg_ref[k:k + 1, cols], k) for k in range(3)]
            wv = [functools.partial(lambda k: wv_ref[k:k + 1, cols], k) for k in range(3)]

            def conv_grads(g_cur, g_prev, v_cur, v_prev, da):
                gate = wg[0]() * down(g_cur, g_prev, 2) + wg[1]() * down(g_cur, g_prev, 1) + wg[2]() * g_cur + bg_ref[:, cols]
                val = wv[0]() * down(v_cur, v_prev, 2) + wv[1]() * down(v_cur, v_prev, 1) + wv[2]() * v_cur + bv_ref[:, cols]
                sg = _sigmoid(gate)
                return (da * val) * (sg * (1.0 + gate * (1.0 - sg))), da * (gate * sg)

            def passes(q, carry):
                for u in range(FFN_UNROLL):
                    carry = one_pass(q * FFN_UNROLL + u, carry)
                return carry

            def one_pass(r, carry, tile_end=False):
                dg_cur, dv_cur, acc = carry
                g_r, v_r = g_ref[rows8(r), cols], v_ref[rows8(r), cols]
                if tile_end:
                    g_n, v_n, da_n = gn_ref[:, cols], vn_ref[:, cols], jnp.where(last, 0.0, dan_ref[:, cols])
                else:
                    g_n, v_n, da_n = g_ref[rows8(r + 1), cols], v_ref[rows8(r + 1), cols], da_ref[rows8(r + 1), cols]
                dg_n, dv_n = conv_grads(g_n, g_r, v_n, v_r, da_n)
                new_acc = []
                for o_ref, w, d_cur, d_n, x0, a in ((og_ref, wg, dg_cur, dg_n, g_r, acc[:4]), (ov_ref, wv, dv_cur, dv_n, v_r, acc[4:])):
                    d1, d2 = up(d_cur, d_n, 1), up(d_cur, d_n, 2)
                    o_ref[rows8(r), cols] = w[2]() * d_cur + w[1]() * d1 + w[0]() * d2
                    new_acc += [a[0] + d2 * x0, a[1] + d1 * x0, a[2] + d_cur * x0, a[3] + d_cur]
                return dg_n, dv_n, tuple(new_acc)

            g_p = jnp.where(first, 0.0, gp_ref[:, cols])
            v_p = jnp.where(first, 0.0, vp_ref[:, cols])
            dg0, dv0 = conv_grads(g_ref[0:8, cols], g_p, v_ref[0:8, cols], v_p, da_ref[0:8, cols])
            zero = jnp.zeros((8, FFN_STRIP), F32)
            carry = lax.fori_loop(0, nch // FFN_UNROLL - 1, passes, (dg0, dv0, (zero,) * 8))
            for r in range(nch - FFN_UNROLL, nch):
                carry = one_pass(r, carry, tile_end=r == nch - 1)
            for ref, a in ((dwg_ref, carry[2][:4]), (dwv_ref, carry[2][4:])):
                for k in range(4):
                    ref[k:k + 1, cols] += jnp.sum(a[k], axis=0, keepdims=True)
        dg_ref[...] = og_ref[...].astype(BF16)
        dv_ref[...] = ov_ref[...].astype(BF16)

    acc = pl.BlockSpec((8, FFN_TC), lambda j, i: (0, j))
    full = jax.ShapeDtypeStruct((s, D_FF), BF16)
    accs = jax.ShapeDtypeStruct((8, D_FF), F32)
    return pl.pallas_call(
        body, name=name, grid=(D_FF // FFN_TC, nrow), in_specs=[cur, prev, nxt, cur, prev, nxt, w, w, b, b, cur, nxt],
        out_specs=[cur, cur, acc, acc], out_shape=[full, full, accs, accs],
        scratch_shapes=[pltpu.VMEM((tm, FFN_TC), F32), pltpu.VMEM((tm, FFN_TC), F32)],
        compiler_params=_params("parallel", "arbitrary"),
    )(upg, upg, upg, upv, upv, upv, cwg, cwv, cbg, cbv, dact, dact)


def _mesh_place():
    return lax.axis_index("x"), lax.axis_index("y"), lax.axis_index("c")


def _all_gather(shards, name):
    na = len(shards)

    def body(*refs):
        x_refs, out_refs = refs[:na], refs[na:2 * na]
        send_sems, recv_sems, local_sems = refs[2 * na:]
        x, y, cc = _mesh_place()
        me, sibling = (x, y, cc), (x, y, 1 - cc)
        chips = [(1 - x, y), (x, 1 - y), (1 - x, 1 - y)]

        def copy(k, a, block, to, from_input=False):
            slot = out_refs[a].at[4 * block[0] + 2 * block[1] + block[2]]
            return pltpu.make_async_remote_copy(
                src_ref=x_refs[a] if from_input else slot, dst_ref=slot, send_sem=send_sems.at[k * na + a],
                recv_sem=recv_sems.at[k * na + a], device_id=to, device_id_type=pl.DeviceIdType.MESH)

        mine = [pltpu.make_async_copy(x_refs[a], out_refs[a].at[4 * x + 2 * y + cc], local_sems.at[a]) for a in range(na)]
        for cp in mine:
            cp.start()
        first = [copy(0, a, me, sibling, True) for a in range(na)]
        first += [copy(1 + j, a, me, (*chip, cc), True) for j, chip in enumerate(chips) for a in range(na)]
        for cp in first:
            cp.start()
        passed = []
        for j, chip in enumerate(chips):
            for a in range(na):
                copy(1 + j, a, (*chip, cc), me).wait_recv()
                passed.append(copy(4 + j, a, (*chip, cc), sibling))
                passed[-1].start()
        for a in range(na):
            copy(0, a, sibling, me).wait_recv()
        for j, chip in enumerate(chips):
            for a in range(na):
                copy(4 + j, a, (*chip, 1 - cc), me).wait_recv()
        for cp in first + passed:
            cp.wait_send()
        for cp in mine:
            cp.wait()

    hbm = pl.BlockSpec(memory_space=pl.ANY)
    return pl.pallas_call(
        body, name=name, out_shape=[jax.ShapeDtypeStruct((N_DEV,) + t.shape, t.dtype) for t in shards],
        in_specs=[hbm] * na, out_specs=[hbm] * na,
        scratch_shapes=[pltpu.SemaphoreType.DMA((7 * na,)), pltpu.SemaphoreType.DMA((7 * na,)), pltpu.SemaphoreType.DMA((na,))],
    )(*shards)


def _exchange(blocks, name):
    na = len(blocks)

    def body(*refs):
        g_refs, out_refs = refs[:na], refs[na:2 * na]
        send_sems, recv_sems, local_sems = refs[2 * na:]
        x, y, cc = _mesh_place()
        me = 4 * x + 2 * y + cc
        mine = [pltpu.make_async_copy(g_refs[a].at[me], out_refs[a].at[me], local_sems.at[a]) for a in range(na)]
        for cp in mine:
            cp.start()
        sends, lands = [], []
        for k in range(1, N_DEV):
            px = 1 - x if (k >> 2) & 1 else x
            py = 1 - y if (k >> 1) & 1 else y
            pc = 1 - cc if k & 1 else cc
            peer = 4 * px + 2 * py + pc
            for a in range(na):
                sem = (k - 1) * na + a
                sends.append(pltpu.make_async_remote_copy(
                    src_ref=g_refs[a].at[peer], dst_ref=out_refs[a].at[me], send_sem=send_sems.at[sem], recv_sem=recv_sems.at[sem],
                    device_id=(px, py, pc), device_id_type=pl.DeviceIdType.MESH))
                lands.append(pltpu.make_async_remote_copy(
                    src_ref=g_refs[a].at[peer], dst_ref=out_refs[a].at[peer], send_sem=send_sems.at[sem], recv_sem=recv_sems.at[sem],
                    device_id=(px, py, pc), device_id_type=pl.DeviceIdType.MESH))
        for cp in sends:
            cp.start()
        for cp in lands:
            cp.wait_recv()
        for cp in sends:
            cp.wait_send()
        for cp in mine:
            cp.wait()

    hbm = pl.BlockSpec(memory_space=pl.ANY)
    return pl.pallas_call(
        body, name=name, out_shape=[jax.ShapeDtypeStruct(t.shape, t.dtype) for t in blocks],
        in_specs=[hbm] * na, out_specs=[hbm] * na,
        scratch_shapes=[pltpu.SemaphoreType.DMA((7 * na,)), pltpu.SemaphoreType.DMA((7 * na,)), pltpu.SemaphoreType.DMA((na,))],
    )(*blocks)


def _peer(k):
    x, y, cc = _mesh_place()
    px = 1 - x if (k >> 2) & 1 else x
    py = 1 - y if (k >> 1) & 1 else y
    pc = 1 - cc if k & 1 else cc
    return (px, py, pc), 4 * px + 2 * py + pc


def _push_copy(src_ref, land_ref, k, a, na, send_sems, recv_sems, indexed, landed):
    x, y, cc = _mesh_place()
    place, peer = _peer(k)
    sem = (k - 1) * na + a
    return pltpu.make_async_remote_copy(
        src_ref=src_ref.at[peer] if indexed else src_ref, dst_ref=land_ref.at[peer if landed else 4 * x + 2 * y + cc],
        send_sem=send_sems.at[sem], recv_sem=recv_sems.at[sem], device_id=place, device_id_type=pl.DeviceIdType.MESH)


_HBM = pl.BlockSpec(memory_space=pltpu.HBM)
_SEM = pl.BlockSpec(memory_space=pltpu.SEMAPHORE)
_EFFECT = pltpu.SideEffectType.DATAFLOW_SIDE_EFFECTING


def _push_start(srcs, indexed, name):
    na = len(srcs)
    lands = [lax.empty(t.shape if indexed else (N_DEV,) + t.shape, t.dtype) for t in srcs]

    def body(*refs):
        src_refs, land_refs = refs[:na], refs[na:2 * na]
        send_sems, recv_sems = refs[2 * na], refs[2 * na + 1]
        token = refs[-1]
        for k in range(1, N_DEV):
            for a in range(na):
                _push_copy(src_refs[a], land_refs[a], k, a, na, send_sems, recv_sems, indexed, False).start()
        token[...] = jnp.zeros_like(token)

    sems = pltpu.SemaphoreType.DMA((7 * na,))
    out = pl.pallas_call(
        body, name=name,
        out_shape=(sems, sems, *[pltpu.HBM(t.shape, t.dtype) for t in srcs], *[pltpu.HBM(t.shape, t.dtype) for t in lands],
                   jax.ShapeDtypeStruct((8, LANES), F32)),
        in_specs=[_HBM] * (2 * na), out_specs=(_SEM, _SEM, *[_HBM] * (2 * na), pl.BlockSpec(memory_space=pltpu.VMEM)),
        input_output_aliases={i: 2 + i for i in range(2 * na)},
        compiler_params=pltpu.CompilerParams(has_side_effects=_EFFECT),
    )(*[pltpu.with_memory_space_constraint(t, pltpu.HBM) for t in srcs + lands])
    return out[0], out[1], list(out[2:2 + na]), list(out[2 + na:2 + 2 * na]), out[-1]


def _push_wait(started, indexed, after, name):
    send_sems, recv_sems, srcs, lands, _ = started
    na = len(srcs)

    def body(*refs):
        src_refs, land_refs = refs[:na], refs[na:2 * na]
        send_sems, recv_sems = refs[2 * na], refs[2 * na + 1]
        for k in range(1, N_DEV):
            for a in range(na):
                copy = _push_copy(src_refs[a], land_refs[a], k, a, na, send_sems, recv_sems, indexed, True)
                copy.wait_send()
                copy.wait_recv()

    out = pl.pallas_call(
        body, name=name, out_shape=[pltpu.HBM(t.shape, t.dtype) for t in srcs + lands],
        in_specs=[_HBM] * (2 * na) + [_SEM, _SEM, pl.BlockSpec(memory_space=pl.ANY)], out_specs=[_HBM] * (2 * na),
        input_output_aliases={i: i for i in range(2 * na)},
        compiler_params=pltpu.CompilerParams(has_side_effects=_EFFECT),
    )(*srcs, *lands, send_sems, recv_sems, after)
    x, y, cc = _mesh_place()
    me = 4 * x + 2 * y + cc
    return [lax.dynamic_update_index_in_dim(
        land, lax.dynamic_index_in_dim(src, me, 0, keepdims=False) if indexed else src, me, 0)
        for src, land in zip(out[:na], out[na:])]


def _adamw_sum(parts, w, m, v, name):
    _, r, c = parts.shape
    tr = _tile(r, (256, 128, 64, 32, 16, 8))

    def body(p_ref, w_ref, m_ref, v_ref, g_ref, d_ref, nm_ref, nv_ref):
        _adam_store(_sum_parts(p_ref), w_ref, m_ref, v_ref, g_ref, d_ref, nm_ref, nv_ref)

    row = pl.BlockSpec((tr, c), lambda i: (i, 0))
    shp = jax.ShapeDtypeStruct((r, c), F32)
    return pl.pallas_call(
        body, name=name, grid=(r // tr,), in_specs=[pl.BlockSpec((N_DEV, tr, c), lambda i: (0, i, 0)), row, row, row],
        out_specs=[row, row, row, row], out_shape=[shp, shp, shp, shp], compiler_params=_params("parallel"),
    )(parts, w, m, v)


def _sum_parts(p_ref):
    g = p_ref[0].astype(F32)
    for k in range(1, N_DEV):
        g = g + p_ref[k].astype(F32)
    return g


def _adam_store(g, w_ref, m_ref, v_ref, g_ref, d_ref, nm_ref, nv_ref):
    nm = ADAM_B1 * m_ref[...] + (1.0 - ADAM_B1) * g
    nv = ADAM_B2 * v_ref[...] + (1.0 - ADAM_B2) * (g * g)
    m_hat = nm / (1.0 - ADAM_B1 ** ADAM_STEP)
    v_hat = nv / (1.0 - ADAM_B2 ** ADAM_STEP)
    g_ref[...] = g
    nm_ref[...] = nm
    nv_ref[...] = nv
    d_ref[...] = -ADAM_LR * (m_hat / (jnp.sqrt(v_hat) + ADAM_EPS) + ADAM_WD * w_ref[...])


def _adamw_weight(parts, w, m, v, name):
    _, r, c = w.shape
    tr = _tile(r, (256, 128, 176))
    nr = r // tr

    def body(p0_ref, p1_ref, w_ref, m_ref, v_ref, g_ref, d_ref, nm_ref, nv_ref):
        g = jnp.where(pl.program_id(0) == 0, _sum_parts(p0_ref), _sum_parts(p1_ref))
        _adam_store(g, w_ref, m_ref, v_ref, g_ref, d_ref, nm_ref, nv_ref)

    part = lambda layer: pl.BlockSpec(
        (N_DEV, tr, c), lambda l, i: (0, jnp.where(l == layer, i, (nr - 1) * (1 - layer)), 0))
    row = pl.BlockSpec((None, tr, c), lambda l, i: (l, i, 0))
    shp = jax.ShapeDtypeStruct(w.shape, F32)
    return pl.pallas_call(
        body, name=name, grid=(DEPTH, nr), in_specs=[part(0), part(1), row, row, row],
        out_specs=[row, row, row, row], out_shape=[shp, shp, shp, shp], compiler_params=_params("arbitrary", "arbitrary"),
    )(parts[0], parts[1], w, m, v)


def _full_to_slots(name, t):
    k, n = t.shape
    if name in ROW_SHARDED:
        return t.reshape(N_DEV, k // N_DEV, n)
    return t.reshape(k, N_DEV, n // N_DEV).transpose(1, 0, 2)


def _slots_to_full(name, t):
    _, r, c = t.shape
    if name in ROW_SHARDED:
        return t.reshape(N_DEV * r, c)
    return t.transpose(1, 0, 2).reshape(r, N_DEV * c)


def _small_sizes(shapes):
    return [(n, shapes[n], -(-int(math.prod(shapes[n])) // (8 * LANES)) * 8) for n in SMALL]


def _pack_small(tree, shapes):
    rows = []
    for n, shp, nrow in _small_sizes(shapes):
        flat = tree[n].reshape(-1)
        rows.append(jnp.pad(flat, (0, nrow * LANES - flat.shape[0])).reshape(nrow, LANES))
    total = sum(r.shape[0] for r in rows)
    rows.append(jnp.zeros((-total % SMALL_ROW_TILE, LANES), F32))
    return jnp.concatenate(rows, axis=0)


def _unpack_small(buf, shapes):
    out, r0 = {}, 0
    for n, shp, nrow in _small_sizes(shapes):
        out[n] = buf[r0:r0 + nrow].reshape(-1)[:int(math.prod(shp))].reshape(shp)
        r0 += nrow
    return out


def _block_diag(w):
    g = w.shape[0]
    eye = jnp.eye(g, dtype=w.dtype)
    return (eye[:, None, :, None] * w[:, :, None, :]).reshape(g * HEAD_DIM, g * HEAD_DIM)


def kernel(x, positions, norm1, w_in, q_norm, k_norm, sinks, w_pool, pool_scale, sgu_v_norm, w_s, b_s, w_proj_a, w_proj_b, w_proj_c, w_out, norm2, w_up, conv_w, conv_b, w_down, loss_target, m_norm1, m_w_in, m_q_norm, m_k_norm, m_sinks, m_w_pool, m_pool_scale, m_sgu_v_norm, m_w_s, m_b_s, m_w_proj_a, m_w_proj_b, m_w_proj_c, m_w_out, m_norm2, m_w_up, m_conv_w, m_conv_b, m_w_down, v_norm1, v_w_in, v_q_norm, v_k_norm, v_sinks, v_w_pool, v_pool_scale, v_sgu_v_norm, v_w_s, v_b_s, v_w_proj_a, v_w_proj_b, v_w_proj_c, v_w_out, v_norm2, v_w_up, v_conv_w, v_conv_b, v_w_down):
    names = ("norm1", "w_in", "q_norm", "k_norm", "sinks", "w_pool", "pool_scale", "sgu_v_norm", "w_s", "b_s", "w_proj_a",
             "w_proj_b", "w_proj_c", "w_out", "norm2", "w_up", "conv_w", "conv_b", "w_down")
    wts = dict(zip(names, (norm1, w_in, q_norm, k_norm, sinks, w_pool, pool_scale, sgu_v_norm, w_s, b_s, w_proj_a, w_proj_b,
                           w_proj_c, w_out, norm2, w_up, conv_w, conv_b, w_down)))
    mom = dict(zip(names, (m_norm1, m_w_in, m_q_norm, m_k_norm, m_sinks, m_w_pool, m_pool_scale, m_sgu_v_norm, m_w_s, m_b_s,
                           m_w_proj_a, m_w_proj_b, m_w_proj_c, m_w_out, m_norm2, m_w_up, m_conv_w, m_conv_b, m_w_down)))
    var = dict(zip(names, (v_norm1, v_w_in, v_q_norm, v_k_norm, v_sinks, v_w_pool, v_pool_scale, v_sgu_v_norm, v_w_s, v_b_s,
                           v_w_proj_a, v_w_proj_b, v_w_proj_c, v_w_out, v_norm2, v_w_up, v_conv_w, v_conv_b, v_w_down)))
    xs = x[0]
    target = loss_target[0]
    s = xs.shape[0]

    inv_freq = ROPE_THETA ** (-jnp.arange(0, HEAD_DIM, 2, dtype=F32) / HEAD_DIM)
    ang = positions[0].astype(F32)[:, None] * inv_freq
    cosf = jnp.tile(jnp.cos(ang), (1, 4))
    sinf = jnp.tile(jnp.concatenate([-jnp.sin(ang), jnp.sin(ang)], axis=1), (1, 2))

    local = [{n: wts[n][l] if n == "conv_w" else wts[n][l].astype(BF16) for n in SHARDED} for l in range(DEPTH)]
    later = SHARDED[1:]
    full = [{"w_in": _slots_to_full("w_in", _all_gather([local[0]["w_in"]], "gather_w_in_0")[0])}, None]
    gather0 = _push_start([local[0][n] for n in later], False, "gather_rest_0_start")
    norm1_first = norm1[0] + gather0[4][0, 0]

    def layer_consts(l):
        return dict(
            bdw=_block_diag(w_pool[l]).astype(BF16), qn=jnp.tile(q_norm[l], 2).reshape(1, LANES),
            kn=jnp.tile(k_norm[l], 2).reshape(1, LANES), vn=jnp.tile(sgu_v_norm[l], 4).reshape(1, SGU_W),
            bcol=jnp.repeat(b_s[l].T, HEAD_DIM, axis=1),
            cbg=conv_b[l][:D_FF].reshape(1, D_FF), cbv=conv_b[l][D_FF:].reshape(1, D_FF))

    gate_cols, val_cols = (0, D_FF), (D_FF, D_FF)

    saved = []
    cur = xs
    for l in range(DEPTH):
        if l == 1:
            landed = _push_wait(gather1, False, cur, "gather_weights_1_wait")
            full[1] = {n: _slots_to_full(n, t) for n, t in zip(SHARDED, landed)}
        fw, k = full[l], layer_consts(l)
        h1 = _rms_fwd(cur, norm1_first if l == 0 else norm1[l], f"rms1_fwd_{l}")
        z = _mm(h1, fw["w_in"], name=f"in_proj_{l}")
        a = _pool_fwd(z, k["bdw"], pool_scale[l], f"pool_fwd_{l}")
        b = _attn_fwd(z, cosf, sinf, k["qn"], k["kn"], sinks[l], f"attn_fwd_{l}")
        c = _sgu_fwd(z, w_s[l], k["bcol"], k["vn"], f"sgu_fwd_{l}")
        w_proj_a_l = fw.get("w_proj_a")
        if l == 0:
            landed = _push_wait(gather0, False, c, "gather_rest_0_wait")
            fw.update({n: _slots_to_full(n, t) for n, t in zip(later, landed)})
            gather1 = _push_start([local[1][n] for n in SHARDED], False, "gather_weights_1_start")
            w_proj_a_l = fw["w_proj_a"] + gather1[4][0, 0].astype(BF16)
        merged = _merge_fwd(a, b, c, w_proj_a_l, fw["w_proj_b"], fw["w_proj_c"], z, f"merge_fwd_{l}")
        x1 = _mm(merged, fw["w_out"], add=cur, name=f"out_proj_{l}")
        h2 = _rms_fwd(x1, norm2[l], f"rms2_fwd_{l}")
        upg = _mm(h2, fw["w_up"], b_n=gate_cols, name=f"up_gate_{l}")
        upv = _mm(h2, fw["w_up"], b_n=val_cols, name=f"up_val_{l}")
        k["cwg"], k["cwv"] = fw["conv_w"][:, :D_FF], fw["conv_w"][:, D_FF:]
        act = _ffn_act_fwd(upg, upv, k["cwg"], k["cwv"], k["cbg"], k["cbv"], f"ffn_act_fwd_{l}")
        x2 = _mm(act, fw["w_down"], add=x1, name=f"down_proj_{l}")
        saved.append(dict(x0=cur, h1=h1, z=z, a=a, b=b, c=c, merged=merged, x1=x1, h2=h2, upg=upg, upv=upv, act=act))
        cur = x2

    dcur, loss_tile = _loss_head(cur, target)
    loss = lax.psum(loss_tile[0, 0], ("x", "y", "c"))

    gsmall = [None] * DEPTH

    def slots_of(grads):
        return [_full_to_slots(n, t) for n, t in grads.items()]

    for l in reversed(range(DEPTH)):
        fw, k, sv = full[l], layer_consts(l), saved[l]
        k["cwg"], k["cwv"] = fw["conv_w"][:, :D_FF], fw["conv_w"][:, D_FF:]
        staged = l == 0
        wgrad = functools.partial(_mm, ta=True, out_dtype=BF16)
        w_down_l = fw["w_down"] + exchange1[4][0, 0].astype(BF16) if staged else fw["w_down"]
        dact = _mm(dcur, w_down_l, tb=True, name=f"down_proj_bwd_{l}")
        g_down = wgrad(sv["act"], dcur, name=f"down_proj_wgrad_{l}")
        dg0, dv0, dcg, dcv = _ffn_bwd(sv["upg"], sv["upv"], k["cwg"], k["cwv"], k["cbg"], k["cbv"], dact, f"ffn_bwd_{l}")
        dh2 = _mm(dg0, fw["w_up"], tb=True, b_k=gate_cols, name=f"up_gate_bwd_{l}")
        dh2 = _mm(dv0, fw["w_up"], tb=True, b_k=val_cols, add=dh2, name=f"up_val_bwd_{l}")
        g_up = wgrad(sv["h2"], dg0, out_cols=(0, 2 * D_FF), name=f"up_gate_wgrad_{l}")
        g_up = wgrad(sv["h2"], dv0, out_cols=(D_FF, 2 * D_FF), out_into=g_up, name=f"up_val_wgrad_{l}")
        g_ffn = dict(w_up=g_up, w_down=g_down, conv_w=jnp.concatenate([dcg[0:3], dcv[0:3]], axis=1))
        norm2_l = norm2[l]
        if staged:
            parts1 = dict(zip(SHARDED, _push_wait(exchange1, True, g_up, "exchange_grads_1_wait")))
            exchange_ffn = _push_start(slots_of(g_ffn), True, "exchange_ffn_0_start")
            norm2_l = norm2_l + exchange_ffn[4][0, 0]
        dx1, g_norm2 = _rms_bwd(sv["x1"], norm2_l, dh2, dcur, f"rms2_bwd_{l}")
        dmerged = _mm(dx1, fw["w_out"], tb=True, name=f"out_proj_bwd_{l}")
        g_out = wgrad(sv["merged"], dx1, name=f"out_proj_wgrad_{l}")
        dz, dya, da = _branch_bwd(0, sv["a"], fw["w_proj_a"], sv["z"], dmerged, None, f"branch_a_bwd_{l}")
        dz, dyb, db = _branch_bwd(1, sv["b"], fw["w_proj_b"], sv["z"], dmerged, dz, f"branch_b_bwd_{l}")
        dz, dyc, dc = _branch_bwd(2, sv["c"], fw["w_proj_c"], sv["z"], dmerged, dz, f"branch_c_bwd_{l}")
        g_mix = dict(w_proj_a=wgrad(sv["a"], dya, name=f"proj_a_wgrad_{l}"), w_proj_b=wgrad(sv["b"], dyb, name=f"proj_b_wgrad_{l}"),
                     w_proj_c=wgrad(sv["c"], dyc, name=f"proj_c_wgrad_{l}"), w_out=g_out)
        pool_scale_l = pool_scale[l]
        if staged:
            exchange_mix = _push_start(slots_of(g_mix), True, "exchange_mixer_0_start")
            pool_scale_l = pool_scale_l + exchange_mix[4][0, 0]
        dxp, g_bdw, g_pscale = _pool_bwd(sv["z"], da, k["bdw"], pool_scale_l, f"pool_bwd_{l}")
        dq, dkc, dkp, dvc, dvp, g_qn, g_sink = _attn_bwd(sv["z"], cosf, sinf, k["qn"], k["kn"], sinks[l], db, f"attn_bwd_{l}")
        duv, g_ws, g_bacc, g_vn = _sgu_bwd(sv["z"], w_s[l], k["bcol"], k["vn"], dc, f"sgu_bwd_{l}")
        dz, g_kn = _kv_post(sv["z"], cosf, sinf, k["kn"], dkc, dkp, dvc, dvp, dxp, dq, duv, dz, f"kv_post_{l}")
        g_in = dict(w_in=wgrad(sv["h1"], dz, name=f"in_proj_wgrad_{l}"))
        norm1_l = norm1[l]
        if staged:
            exchange_in = _push_start(slots_of(g_in), True, "exchange_w_in_0_start")
            norm1_l = norm1_l + exchange_in[4][0, 0]
        dh1 = _mm(dz, fw["w_in"], tb=True, name=f"in_proj_bwd_{l}")
        dcur, g_norm1 = _rms_bwd(sv["x0"], norm1_l, dh1, dx1, f"rms1_bwd_{l}")
        if not staged:
            exchange1 = _push_start(slots_of({n: {**g_in, **g_mix, **g_ffn}[n] for n in SHARDED}), True, "exchange_grads_1_start")
        gsmall[l] = dict(
            norm1=g_norm1[0], q_norm=g_qn[0, :HEAD_DIM], k_norm=g_kn[0, :HEAD_DIM], sinks=g_sink[:, 0],
            w_pool=jnp.stack([g_bdw[g * HEAD_DIM:(g + 1) * HEAD_DIM, g * HEAD_DIM:(g + 1) * HEAD_DIM] for g in range(4)]),
            pool_scale=g_pscale[0], sgu_v_norm=g_vn[0, :HEAD_DIM], w_s=g_ws, b_s=g_bacc[:, ::HEAD_DIM].T,
            norm2=g_norm2[0], conv_b=jnp.concatenate([dcg[3], dcv[3]]))
    grad_x = dcur[None]

    shapes = {n: wts[n].shape for n in SMALL}
    gs = _pack_small({n: jnp.stack([gsmall[l][n] for l in range(DEPTH)]) for n in SMALL}, shapes)
    gs_all = _all_gather([gs], "gather_small_grads")[0]
    g_s, d_s, m_s, v_s = _adamw_sum(gs_all, _pack_small(wts, shapes), _pack_small(mom, shapes), _pack_small(var, shapes), "adamw_replicated")
    small = [_unpack_small(t, shapes) for t in (g_s, d_s, m_s, v_s)]

    parts0 = dict(zip(g_ffn, _push_wait(exchange_ffn, True, g_s, "exchange_ffn_0_wait")))
    parts0.update(zip(g_mix, _push_wait(exchange_mix, True, g_s, "exchange_mixer_0_wait")))
    update = lambda n: _adamw_weight([parts0[n], parts1[n]], wts[n], mom[n], var[n], f"adamw_{n}")
    big = {n: update(n) for n in SHARDED[1:]}
    parts0.update(zip(g_in, _push_wait(exchange_in, True, big["w_up"][0], "exchange_w_in_0_wait")))
    big["w_in"] = update("w_in")

    outs = [loss, grad_x]
    for kind in range(4):
        outs += [small[kind][n] if n in SMALL else big[n][kind] for n in names]
    return tuple(outs)
```

```python
import functools
import math

import jax
import jax.numpy as jnp
from jax import lax
from jax.experimental import pallas as pl
from jax.experimental.pallas import tpu as pltpu

F32 = jnp.float32
BF16 = jnp.bfloat16

D_MODEL = 1024
DEPTH = 2
HEAD_DIM = 64
N_Q_HEADS = 8
Q_PER_KV = 4
BLOCK = 128
POOL_W = 256
ATTN_W = 512
KV_W = 128
SGU_W = 256
IN_COLS = 4608
GATE_COL0 = 1536
D_FF = 2816
EPS = 1e-6
ROPE_THETA = 10000.0
N_DEV = 8
LANES = 128
HALO_POOL = 16
HALO_CONV = 8

ADAM_LR = 0.001
ADAM_B1 = 0.9
ADAM_B2 = 0.999
ADAM_EPS = 1e-08
ADAM_WD = 0.01
ADAM_STEP = 10

VMEM_LIMIT = 48 * 1024 * 1024
MM_VMEM_BUDGET = 40 * 1024 * 1024

SHARDED = ("w_in", "w_proj_a", "w_proj_b", "w_proj_c", "w_out", "w_up", "w_down", "conv_w")
ROW_SHARDED = ("w_out", "w_down")
SMALL_ROW_TILE = 256
SMALL = ("norm1", "q_norm", "k_norm", "sinks", "w_pool", "pool_scale", "sgu_v_norm", "w_s", "b_s", "norm2", "conv_b")

_GELU_C = math.sqrt(2.0 / math.pi)
_GELU_A = 0.044715


def _params(*sem):
    return pltpu.CompilerParams(dimension_semantics=sem, vmem_limit_bytes=VMEM_LIMIT)


def _tile(n, prefs):
    for t in prefs:
        if t <= n and n % t == 0:
            return t
    return n


def _head_mean_matrix():
    r = lax.broadcasted_iota(jnp.int32, (LANES, LANES), 0)
    c = lax.broadcasted_iota(jnp.int32, (LANES, LANES), 1)
    return jnp.where((r >= HEAD_DIM) == (c >= HEAD_DIM), 1.0 / HEAD_DIM, 0.0).astype(BF16)


def _head_mean(v, bd):
    hi = v.astype(BF16)
    rest = v - hi.astype(F32)
    mid = rest.astype(BF16)
    lo = (rest - mid.astype(F32)).astype(BF16)
    mm = lambda p: jnp.dot(p, bd, preferred_element_type=F32)
    return mm(hi) + (mm(mid) + mm(lo))


def _rot_half(t):
    lane = lax.broadcasted_iota(jnp.int32, t.shape, 1)
    return jnp.where((lane & 32) == 0, pltpu.roll(t, LANES - 32, 1), pltpu.roll(t, 32, 1))


def _norm_rope(t, gn, cosf, sinf, bd):
    r = lax.rsqrt(_head_mean(t * t, bd) + EPS)
    n = t * r
    y = n * gn
    return y * cosf + _rot_half(y) * sinf, n, r


def _norm_rope_bwd(d, t, n, r, gn, cosf, sinf, bd):
    dy = d * cosf + _rot_half(d * sinf)
    dgn = jnp.sum(dy * n, axis=0, keepdims=True)
    u = dy * gn
    dt = r * u - t * (r * r * r) * _head_mean(t * u, bd)
    return dt, dgn


def _gelu(x):
    t = jnp.tanh(_GELU_C * (x + _GELU_A * (x * x * x)))
    return 0.5 * x * (1.0 + t), t


def _gelu_grad(x, t):
    return 0.5 * (1.0 + t) + 0.5 * x * (1.0 - t * t) * (_GELU_C * (1.0 + 3.0 * _GELU_A * x * x))


def _sigmoid(x):
    return jax.nn.sigmoid(x)


def _dot(a, b, ca=1, cb=0):
    return lax.dot_general(a.astype(BF16), b.astype(BF16), (((ca,), (cb,)), ((), ())), preferred_element_type=F32)


def _mm(a, b, *, ta=False, tb=False, add=None, out_dtype=F32, name, b_n=None, b_k=None, out_cols=None, out_into=None):
    m = a.shape[1] if ta else a.shape[0]
    k = a.shape[0] if ta else a.shape[1]
    n = b_n[1] if b_n else (b.shape[0] if tb else b.shape[1])
    tn = _tile(n, (1024, 1152, 1408, 512, 256, 128))
    has_add = add is not None
    fits = []
    for tm in (1024, 1408, 512, 256, 128):
        for tk in (k, 4608, 2816, 2048, 1408, 1152, 1024, 512, 256, 128):
            if tm <= m and m % tm == 0 and tk <= k and k % tk == 0:
                need = (2 * (tm * tk * a.dtype.itemsize + tk * tn * b.dtype.itemsize) + 2 * tm * tn * jnp.dtype(out_dtype).itemsize
                        + 2 * tm * tn * 4 * has_add + tm * tn * 4 * (tk < k))
                if need <= MM_VMEM_BUDGET:
                    fits.append((k // tk, -tm, tm, tk))
    if fits:
        _, _, tm, tk = min(fits)
    else:
        tm, tk = _tile(m, (256, 128)), _tile(k, (512, 256, 128))
    nk = k // tk
    n0 = b_n[0] // tn if b_n else 0
    k0 = b_k[0] // tk if b_k else 0
    o0, n_out = (out_cols[0] // tn, out_cols[1]) if out_cols else (0, n)
    n_in = 2 + has_add + (out_into is not None)

    def body(*refs):
        a_ref, b_ref = refs[0], refs[1]
        add_ref = refs[2] if has_add else None
        o_ref = refs[n_in]
        def finish(r):
            if has_add:
                r = r + add_ref[...]
            o_ref[...] = r.astype(out_dtype)

        if nk == 1:
            finish(_dot(a_ref[...], b_ref[...], 0 if ta else 1, 1 if tb else 0))
        else:
            acc_ref = refs[-1]
            kk = pl.program_id(2)

            @pl.when(kk == 0)
            def _():
                acc_ref[...] = jnp.zeros_like(acc_ref)

            acc_ref[...] += _dot(a_ref[...], b_ref[...], 0 if ta else 1, 1 if tb else 0)

            @pl.when(kk == nk - 1)
            def _():
                finish(acc_ref[...])

    a_spec = pl.BlockSpec((tk, tm), lambda i, j, kk: (kk, i)) if ta else pl.BlockSpec((tm, tk), lambda i, j, kk: (i, kk))
    if tb:
        b_spec = pl.BlockSpec((tn, tk), lambda i, j, kk: (j + n0, kk + k0))
    else:
        b_spec = pl.BlockSpec((tk, tn), lambda i, j, kk: (kk + k0, j + n0))
    in_specs = [a_spec, b_spec] + ([pl.BlockSpec((tm, tn), lambda i, j, kk: (i, j))] if has_add else [])
    args = (a, b) + ((add,) if has_add else ())
    if out_into is not None:
        in_specs.append(pl.BlockSpec(memory_space=pl.ANY))
        args += (out_into,)
    return pl.pallas_call(
        body, name=name, grid=(m // tm, n // tn, nk), in_specs=in_specs,
        out_specs=pl.BlockSpec((tm, tn), lambda i, j, kk: (i, j + o0)),
        out_shape=jax.ShapeDtypeStruct((m, n_out), out_dtype),
        scratch_shapes=[pltpu.VMEM((tm, tn), F32)] if nk > 1 else [],
        input_output_aliases={n_in - 1: 0} if out_into is not None else {},
        compiler_params=_params("parallel", "parallel", "arbitrary"),
    )(*args)


def _rms_fwd(x, g, name):
    s, d = x.shape
    tr = _tile(s, (512, 256, 128))

    def body(x_ref, g_ref, h_ref):
        xv = x_ref[...]
        r = lax.rsqrt(jnp.mean(xv * xv, axis=-1, keepdims=True) + EPS)
        h_ref[...] = ((xv * r) * g_ref[...]).astype(BF16)

    return pl.pallas_call(
        body, name=name, grid=(s // tr,),
        in_specs=[pl.BlockSpec((tr, d), lambda i: (i, 0)), pl.BlockSpec((1, d), lambda i: (0, 0))],
        out_specs=pl.BlockSpec((tr, d), lambda i: (i, 0)),
        out_shape=jax.ShapeDtypeStruct((s, d), BF16), compiler_params=_params("parallel"),
    )(x, g.reshape(1, d))


def _rms_bwd(x, g, dh, dres, name):
    s, d = x.shape
    tr = _tile(s, (512, 256, 128))

    def body(x_ref, g_ref, dh_ref, dres_ref, dx_ref, dg_ref):
        xv = x_ref[...]
        r = lax.rsqrt(jnp.mean(xv * xv, axis=-1, keepdims=True) + EPS)
        dhv = dh_ref[...]
        u = dhv * g_ref[...]
        dx_ref[...] = dres_ref[...] + (r * u - xv * (r * r * r) * jnp.mean(xv * u, axis=-1, keepdims=True))
        part = jnp.sum(dhv * (xv * r), axis=0, keepdims=True)

        @pl.when(pl.program_id(0) == 0)
        def _():
            dg_ref[...] = part

        @pl.when(pl.program_id(0) > 0)
        def _():
            dg_ref[...] += part

    row = pl.BlockSpec((tr, d), lambda i: (i, 0))
    vec = pl.BlockSpec((1, d), lambda i: (0, 0))
    return pl.pallas_call(
        body, name=name, grid=(s // tr,), in_specs=[row, vec, row, row], out_specs=[row, vec],
        out_shape=[jax.ShapeDtypeStruct((s, d), F32), jax.ShapeDtypeStruct((1, d), F32)],
        compiler_params=_params("arbitrary"),
    )(x, g.reshape(1, d), dh, dres)


def _loss_head(y, target):
    s, d = y.shape
    tr = _tile(s, (512, 256, 128))

    def body(y_ref, t_ref, dy_ref, l_ref):
        err = y_ref[...] - t_ref[...]
        dy_ref[...] = err * (1.0 / d)
        part = jnp.sum(jnp.sum(err * err, axis=-1, keepdims=True) * (1.0 / d), axis=0, keepdims=True) * 0.5
        part = jnp.broadcast_to(part, (8, LANES))

        @pl.when(pl.program_id(0) == 0)
        def _():
            l_ref[...] = part

        @pl.when(pl.program_id(0) > 0)
        def _():
            l_ref[...] += part

    row = pl.BlockSpec((tr, d), lambda i: (i, 0))
    acc = pl.BlockSpec((8, LANES), lambda i: (0, 0))
    return pl.pallas_call(
        body, name="loss_head", grid=(s // tr,), in_specs=[row, row], out_specs=[row, acc],
        out_shape=[jax.ShapeDtypeStruct((s, d), F32), jax.ShapeDtypeStruct((8, LANES), F32)],
        compiler_params=_params("arbitrary"),
    )(y, target)


def _pool_lane_select(lane, v2, v4, v8, v16):
    return jnp.where(lane < 64, v2, jnp.where(lane < 128, v4, jnp.where(lane < 192, v8, v16)))


def _pool_diff(xc, xp, row0):
    n = BLOCK + HALO_POOL
    cat = jnp.concatenate([xp, xc], axis=0)
    s2 = cat + pltpu.roll(cat, 1, 0)
    s4 = s2 + pltpu.roll(s2, 2, 0)
    s8 = s4 + pltpu.roll(s4, 4, 0)
    s16 = s8 + pltpu.roll(s8, 8, 0)
    lane = lax.broadcasted_iota(jnp.int32, (n, POOL_W), 1)
    wsum = _pool_lane_select(lane, s2, s4, s8, s16)[HALO_POOL:]
    return wsum / _pool_count(row0, BLOCK) - xc


def _pool_count(row0, rows):
    lane = lax.broadcasted_iota(jnp.int32, (rows, POOL_W), 1)
    t = lax.broadcasted_iota(jnp.int32, (rows, POOL_W), 0) + row0
    return jnp.minimum(t + 1, _pool_lane_select(lane, 2, 4, 8, 16)).astype(F32)


def _pool_fwd(z, bdw, scale, name):
    s = z.shape[0]
    nb = s // BLOCK

    def body(xc_ref, xp_ref, w_ref, sc_ref, a_ref):
        i = pl.program_id(0)
        xp = jnp.where(i > 0, xp_ref[...], 0.0)
        diff = _pool_diff(xc_ref[...], xp, i * BLOCK)
        a_ref[...] = (_dot(diff, w_ref[...]) * sc_ref[...]).astype(BF16)

    return pl.pallas_call(
        body, name=name, grid=(nb,),
        in_specs=[pl.BlockSpec((BLOCK, POOL_W), lambda i: (i, 0)),
                  pl.BlockSpec((HALO_POOL, POOL_W), lambda i: (jnp.maximum(i * (BLOCK // HALO_POOL) - 1, 0), 0)),
                  pl.BlockSpec((POOL_W, POOL_W), lambda i: (0, 0)),
                  pl.BlockSpec((1, POOL_W), lambda i: (0, 0))],
        out_specs=pl.BlockSpec((BLOCK, POOL_W), lambda i: (i, 0)),
        out_shape=jax.ShapeDtypeStruct((s, POOL_W), BF16), compiler_params=_params("parallel"),
    )(z, z, bdw, scale.reshape(1, POOL_W))


def _pool_bwd(z, da, bdw, scale, name):
    s = z.shape[0]
    nb = s // BLOCK
    per = BLOCK // HALO_POOL
    n = BLOCK + HALO_POOL

    def body(xc_ref, xp_ref, dac_ref, dan_ref, w_ref, sc_ref, dx_ref, dw_ref, dsc_ref):
        i = pl.program_id(0)
        xp = jnp.where(i > 0, xp_ref[...], 0.0)
        diff = _pool_diff(xc_ref[...], xp, i * BLOCK)
        mixed = _dot(diff, w_ref[...])
        dac = dac_ref[...]
        dan = jnp.where(i < nb - 1, dan_ref[...], 0.0)
        dmix = jnp.concatenate([dac, dan], axis=0) * sc_ref[...]
        ddiff = _dot(dmix, w_ref[...], 1, 1)
        e = ddiff / _pool_count(i * BLOCK, n)
        f2 = e + pltpu.roll(e, n - 1, 0)
        f4 = f2 + pltpu.roll(f2, n - 2, 0)
        f8 = f4 + pltpu.roll(f4, n - 4, 0)
        f16 = f8 + pltpu.roll(f8, n - 8, 0)
        lane = lax.broadcasted_iota(jnp.int32, (n, POOL_W), 1)
        back = _pool_lane_select(lane, f2, f4, f8, f16)
        dx_ref[...] = (back[:BLOCK] - ddiff[:BLOCK]).astype(BF16)
        dw = _dot(diff, dmix[:BLOCK], 0, 0)
        dsc = jnp.sum(dac * mixed, axis=0, keepdims=True)

        @pl.when(i == 0)
        def _():
            dw_ref[...] = dw
            dsc_ref[...] = dsc

        @pl.when(i > 0)
        def _():
            dw_ref[...] += dw
            dsc_ref[...] += dsc

    blk = pl.BlockSpec((BLOCK, POOL_W), lambda i: (i, 0))
    return pl.pallas_call(
        body, name=name, grid=(nb,),
        in_specs=[blk, pl.BlockSpec((HALO_POOL, POOL_W), lambda i: (jnp.maximum(i * per - 1, 0), 0)),
                  blk, pl.BlockSpec((HALO_POOL, POOL_W), lambda i: (jnp.minimum((i + 1) * per, nb * per - 1), 0)),
                  pl.BlockSpec((POOL_W, POOL_W), lambda i: (0, 0)), pl.BlockSpec((1, POOL_W), lambda i: (0, 0))],
        out_specs=[blk, pl.BlockSpec((POOL_W, POOL_W), lambda i: (0, 0)), pl.BlockSpec((1, POOL_W), lambda i: (0, 0))],
        out_shape=[jax.ShapeDtypeStruct((s, POOL_W), BF16), jax.ShapeDtypeStruct((POOL_W, POOL_W), F32),
                   jax.ShapeDtypeStruct((1, POOL_W), F32)],
        compiler_params=_params("arbitrary"),
    )(z, z, da, da, bdw, scale.reshape(1, POOL_W))


def _attn_setup(zc_ref, zp_ref, cc_ref, cp_ref, sc_ref, sp_ref, qn_ref, kn_ref, bd):
    q = []
    for j in range(ATTN_W // LANES):
        t = zc_ref[:, POOL_W + j * LANES:POOL_W + (j + 1) * LANES]
        q.append((t,) + _norm_rope(t, qn_ref[...], cc_ref[...], sc_ref[...], bd))
    kc_raw = zc_ref[:, POOL_W + ATTN_W:POOL_W + ATTN_W + KV_W]
    kc = _norm_rope(kc_raw, kn_ref[...], cc_ref[...], sc_ref[...], bd)[0]
    kp = _norm_rope(zp_ref[:, :KV_W], kn_ref[...], cp_ref[...], sp_ref[...], bd)[0]
    kband = jnp.concatenate([kp, kc], axis=0).astype(BF16)
    vband = jnp.concatenate([zp_ref[:, KV_W:], zc_ref[:, POOL_W + ATTN_W + KV_W:POOL_W + ATTN_W + 2 * KV_W]], axis=0).astype(BF16)
    return q, kband, vband


def _attn_mask(i):
    row = lax.broadcasted_iota(jnp.int32, (Q_PER_KV * BLOCK, 2 * BLOCK), 0) & (BLOCK - 1)
    col = lax.broadcasted_iota(jnp.int32, (Q_PER_KV * BLOCK, 2 * BLOCK), 1)
    dist = row + BLOCK - col
    return (dist >= 0) & (dist < BLOCK) & ((col >= BLOCK) | (i > 0))


def _stack_heads(tiles, kvh):
    return jnp.concatenate([_to_kv_lanes(tiles[h // 2], h) for h in range(kvh * Q_PER_KV, (kvh + 1) * Q_PER_KV)], axis=0)


def _unstack_heads(stacked, kvh, tiles):
    for g in range(Q_PER_KV):
        h = kvh * Q_PER_KV + g
        t = _from_kv_lanes(stacked[g * BLOCK:(g + 1) * BLOCK], h)
        tiles[h // 2] = t if tiles[h // 2] is None else tiles[h // 2] + t


def _sink_column(sink_ref, kvh):
    grp = lax.broadcasted_iota(jnp.int32, (Q_PER_KV * BLOCK, 1), 0) >> 7
    s = [sink_ref[kvh * Q_PER_KV + g] for g in range(Q_PER_KV)]
    return jnp.where(grp == 0, s[0], jnp.where(grp == 1, s[1], jnp.where(grp == 2, s[2], s[3])))


def _to_kv_lanes(t, h):
    kvh = h // Q_PER_KV
    if (h % 2) != kvh:
        t = pltpu.roll(t, HEAD_DIM, 1)
    lane = lax.broadcasted_iota(jnp.int32, t.shape, 1)
    return jnp.where((lane >= HEAD_DIM) == (kvh == 1), t, 0.0)


def _from_kv_lanes(t, h):
    kvh = h // Q_PER_KV
    lane = lax.broadcasted_iota(jnp.int32, t.shape, 1)
    t = jnp.where((lane >= HEAD_DIM) == (kvh == 1), t, 0.0)
    if (h % 2) != kvh:
        t = pltpu.roll(t, HEAD_DIM, 1)
    return t


def _attn_probs(qh, kband, mask, sink):
    sc = _dot(qh, kband, 1, 1) * (HEAD_DIM ** -0.5)
    sc = jnp.where(mask, sc, -1e30)
    m = jnp.maximum(jnp.max(sc, axis=1, keepdims=True), sink)
    p = jnp.exp(sc - m)
    psink = jnp.exp(sink - m)
    den = jnp.sum(p, axis=1, keepdims=True) + psink
    return p / den, psink / den


def _attn_specs(nb):
    cur = lambda i: (i, 0)
    prev = lambda i: (jnp.maximum(i - 1, 0), 0)
    tab = lambda f: pl.BlockSpec((BLOCK, LANES), f)
    vec = pl.BlockSpec((1, LANES), lambda i: (0, 0))
    return [pl.BlockSpec((BLOCK, 1024), cur),
            pl.BlockSpec((BLOCK, 2 * KV_W), lambda i: (jnp.maximum(i - 1, 0), 3)),
            tab(cur), tab(prev), tab(cur), tab(prev), vec, vec,
            pl.BlockSpec(memory_space=pltpu.SMEM)]


def _attn_fwd(z, cosf, sinf, qn, kn, sinks, name):
    s = z.shape[0]
    nb = s // BLOCK

    def body(zc_ref, zp_ref, cc_ref, cp_ref, sc_ref, sp_ref, qn_ref, kn_ref, sink_ref, o_ref):
        i = pl.program_id(0)
        bd = _head_mean_matrix()
        q, kband, vband = _attn_setup(zc_ref, zp_ref, cc_ref, cp_ref, sc_ref, sp_ref, qn_ref, kn_ref, bd)
        mask = _attn_mask(i)
        out = [None] * (ATTN_W // LANES)
        for kvh in range(N_Q_HEADS // Q_PER_KV):
            qs = _stack_heads([t[1] for t in q], kvh)
            probs, _ = _attn_probs(qs, kband, mask, _sink_column(sink_ref, kvh))
            _unstack_heads(_dot(probs, vband), kvh, out)
        for j, o in enumerate(out):
            o_ref[:, j * LANES:(j + 1) * LANES] = o.astype(BF16)

    return pl.pallas_call(
        body, name=name, grid=(nb,), in_specs=_attn_specs(nb),
        out_specs=pl.BlockSpec((BLOCK, ATTN_W), lambda i: (i, 0)),
        out_shape=jax.ShapeDtypeStruct((s, ATTN_W), BF16), compiler_params=_params("parallel"),
    )(z, z, cosf, cosf, sinf, sinf, qn, kn, sinks)


def _attn_bwd(z, cosf, sinf, qn, kn, sinks, d_out, name):
    s = z.shape[0]
    nb = s // BLOCK
    nt = ATTN_W // LANES

    def body(zc_ref, zp_ref, cc_ref, cp_ref, sc_ref, sp_ref, qn_ref, kn_ref, sink_ref, do_ref,
             dq_ref, dkc_ref, dkp_ref, dvc_ref, dvp_ref, dqn_ref, dsink_ref):
        i = pl.program_id(0)
        bd = _head_mean_matrix()
        q, kband, vband = _attn_setup(zc_ref, zp_ref, cc_ref, cp_ref, sc_ref, sp_ref, qn_ref, kn_ref, bd)
        mask = _attn_mask(i)

        @pl.when(i == 0)
        def _():
            dqn_ref[...] = jnp.zeros_like(dqn_ref)
            dsink_ref[...] = jnp.zeros_like(dsink_ref)

        dq = [None] * nt
        dk = jnp.zeros((2 * BLOCK, KV_W), F32)
        dv = jnp.zeros((2 * BLOCK, KV_W), F32)
        d_tiles = [do_ref[:, j * LANES:(j + 1) * LANES] for j in range(nt)]
        for kvh in range(N_Q_HEADS // Q_PER_KV):
            qs = _stack_heads([t[1] for t in q], kvh)
            probs, psink = _attn_probs(qs, kband, mask, _sink_column(sink_ref, kvh))
            dos = _stack_heads(d_tiles, kvh)
            dp = _dot(dos, vband, 1, 1)
            delta = jnp.sum(dp * probs, axis=1, keepdims=True)
            ds = (probs * (dp - delta)) * (HEAD_DIM ** -0.5)
            dsink = -psink * delta
            for g in range(Q_PER_KV):
                h = kvh * Q_PER_KV + g
                dsink_ref[h:h + 1, :] += jnp.broadcast_to(jnp.sum(dsink[g * BLOCK:(g + 1) * BLOCK], axis=0, keepdims=True), (1, LANES))
            _unstack_heads(_dot(ds, kband), kvh, dq)
            dk = dk + _dot(ds, qs, 0, 0)
            dv = dv + _dot(probs, dos, 0, 0)
        dgn = jnp.zeros((1, LANES), F32)
        for j in range(nt):
            t, _, n, r = q[j]
            dt, g = _norm_rope_bwd(dq[j], t, n, r, qn_ref[...], cc_ref[...], sc_ref[...], bd)
            dq_ref[:, j * LANES:(j + 1) * LANES] = dt.astype(BF16)
            dgn = dgn + g
        dqn_ref[...] += jnp.broadcast_to(dgn, (8, LANES))
        dkp_ref[...] = dk[:BLOCK]
        dkc_ref[...] = dk[BLOCK:]
        dvp_ref[...] = dv[:BLOCK]
        dvc_ref[...] = dv[BLOCK:]

        @pl.when(i == nb - 1)
        def _():
            acc = dqn_ref[...]
            dqn_ref[...] = acc + pltpu.roll(acc, HEAD_DIM, 1)

    blk = lambda w: pl.BlockSpec((BLOCK, w), lambda i: (i, 0))
    acc = pl.BlockSpec((8, LANES), lambda i: (0, 0))
    kv = jax.ShapeDtypeStruct((s, KV_W), F32)
    return pl.pallas_call(
        body, name=name, grid=(nb,), in_specs=_attn_specs(nb) + [blk(ATTN_W)],
        out_specs=[blk(ATTN_W), blk(KV_W), blk(KV_W), blk(KV_W), blk(KV_W), acc, acc],
        out_shape=[jax.ShapeDtypeStruct((s, ATTN_W), BF16), kv, kv, kv, kv,
                   jax.ShapeDtypeStruct((8, LANES), F32), jax.ShapeDtypeStruct((8, LANES), F32)],
        compiler_params=_params("arbitrary"),
    )(z, z, cosf, cosf, sinf, sinf, qn, kn, sinks, d_out)


def _kv_post(z, cosf, sinf, kn, dkc, dkp, dvc, dvp, dxp, dq, duv, dz, name):
    s = z.shape[0]
    nb = s // BLOCK

    def body(zk_ref, c_ref, s_ref, kn_ref, dkc_ref, dkp_ref, dvc_ref, dvp_ref, dxp_ref, dq_ref, duv_ref, dz_in,
             dz_ref, dkn_ref):
        j = pl.program_id(0)
        bd = _head_mean_matrix()
        last = j == nb - 1
        d = dkc_ref[...] + jnp.where(last, 0.0, dkp_ref[...])
        t = zk_ref[:, :KV_W]
        _, n, r = _norm_rope(t, kn_ref[...], c_ref[...], s_ref[...], bd)
        dt, g = _norm_rope_bwd(d, t, n, r, kn_ref[...], c_ref[...], s_ref[...], bd)
        dvv = dvc_ref[...] + jnp.where(last, 0.0, dvp_ref[...])
        dz_ref[:, 0:POOL_W] = dxp_ref[...]
        dz_ref[:, POOL_W:POOL_W + ATTN_W] = dq_ref[...]
        dz_ref[:, POOL_W + ATTN_W:POOL_W + ATTN_W + KV_W] = dt.astype(BF16)
        dz_ref[:, POOL_W + ATTN_W + KV_W:POOL_W + ATTN_W + 2 * KV_W] = dvv.astype(BF16)
        dz_ref[:, POOL_W + ATTN_W + 2 * KV_W:GATE_COL0] = duv_ref[...]

        @pl.when(j == 0)
        def _():
            dkn_ref[...] = jnp.zeros_like(dkn_ref)

        dkn_ref[...] += jnp.broadcast_to(g, (8, LANES))

        @pl.when(last)
        def _():
            acc = dkn_ref[...]
            dkn_ref[...] = acc + pltpu.roll(acc, HEAD_DIM, 1)

    cur = lambda w: pl.BlockSpec((BLOCK, w), lambda j: (j, 0))
    nxt = pl.BlockSpec((BLOCK, KV_W), lambda j: (jnp.minimum(j + 1, nb - 1), 0))
    vec = pl.BlockSpec((1, LANES), lambda j: (0, 0))
    return pl.pallas_call(
        body, name=name, grid=(nb,),
        in_specs=[pl.BlockSpec((BLOCK, 2 * KV_W), lambda j: (j, 3)), cur(LANES), cur(LANES), vec,
                  cur(KV_W), nxt, cur(KV_W), nxt, cur(POOL_W), cur(ATTN_W), cur(2 * SGU_W),
                  pl.BlockSpec(memory_space=pl.ANY)],
        out_specs=[pl.BlockSpec((BLOCK, GATE_COL0), lambda j: (j, 0)), pl.BlockSpec((8, LANES), lambda j: (0, 0))],
        out_shape=[jax.ShapeDtypeStruct(dz.shape, dz.dtype), jax.ShapeDtypeStruct((8, LANES), F32)],
        input_output_aliases={11: 0}, compiler_params=_params("arbitrary"),
    )(z, cosf, sinf, kn, dkc, dkp, dvc, dvp, dxp, dq, duv, dz)


def _sgu_setup(z_ref, ws_ref, vn_ref, bd):
    us = z_ref[:, :SGU_W]
    vs = z_ref[:, SGU_W:]
    ug, tu = _gelu(us)
    gv, tv = _gelu(vs)
    rr = jnp.concatenate([lax.rsqrt(_head_mean(gv[:, k * LANES:(k + 1) * LANES] ** 2, bd) + EPS) for k in range(2)], axis=1)
    vg = (gv * rr) * vn_ref[...]
    tril = lax.broadcasted_iota(jnp.int32, (BLOCK, BLOCK), 0) >= lax.broadcasted_iota(jnp.int32, (BLOCK, BLOCK), 1)
    w = [jnp.where(tril, ws_ref[g], 0.0).astype(BF16) for g in range(4)]
    return us, vs, ug, tu, gv, tv, rr, vg, w, tril


def _group_select(parts):
    lane = lax.broadcasted_iota(jnp.int32, parts[0].shape, 1)
    return _pool_lane_select(lane, *parts)


def _sgu_fwd(z, ws, bcol, vn, name):
    s = z.shape[0]
    nb = s // BLOCK

    def body(z_ref, ws_ref, b_ref, vn_ref, c_ref):
        bd = _head_mean_matrix()
        _, _, ug, _, _, _, _, vg, w, _ = _sgu_setup(z_ref, ws_ref, vn_ref, bd)
        sg = _group_select([_dot(w[g], vg) for g in range(4)]) + b_ref[...]
        c_ref[...] = (ug * sg).astype(BF16)

    return pl.pallas_call(
        body, name=name, grid=(nb,),
        in_specs=[pl.BlockSpec((BLOCK, 2 * SGU_W), lambda i: (i, 2)), pl.BlockSpec((4, BLOCK, BLOCK), lambda i: (0, 0, 0)),
                  pl.BlockSpec((BLOCK, SGU_W), lambda i: (0, 0)), pl.BlockSpec((1, SGU_W), lambda i: (0, 0))],
        out_specs=pl.BlockSpec((BLOCK, SGU_W), lambda i: (i, 0)),
        out_shape=jax.ShapeDtypeStruct((s, SGU_W), BF16), compiler_params=_params("parallel"),
    )(z, ws, bcol, vn)


def _sgu_bwd(z, ws, bcol, vn, dc, name):
    s = z.shape[0]
    nb = s // BLOCK

    def body(z_ref, ws_ref, b_ref, vn_ref, dc_ref, duv_ref, dws_ref, db_ref, dvn_ref):
        i = pl.program_id(0)
        bd = _head_mean_matrix()
        us, vs, ug, tu, gv, tv, rr, vg, w, tril = _sgu_setup(z_ref, ws_ref, vn_ref, bd)
        sg = _group_select([_dot(w[g], vg) for g in range(4)]) + b_ref[...]
        dcv = dc_ref[...]
        dug = dcv * sg
        dsg = dcv * ug
        lane = lax.broadcasted_iota(jnp.int32, dsg.shape, 1)

        @pl.when(i == 0)
        def _():
            dws_ref[...] = jnp.zeros_like(dws_ref)
            db_ref[...] = jnp.zeros_like(db_ref)
            dvn_ref[...] = jnp.zeros_like(dvn_ref)

        for g in range(4):
            dsg_g = jnp.where((lane >= g * HEAD_DIM) & (lane < (g + 1) * HEAD_DIM), dsg, 0.0)
            dws_ref[g] += jnp.where(tril, _dot(dsg_g, vg, 1, 1), 0.0)
        dvg = _group_select([_dot(w[g], dsg, 0, 0) for g in range(4)])
        db_ref[...] += dsg
        n = gv * rr
        part = jnp.sum(dvg * n, axis=0, keepdims=True)
        dvn_ref[...] += jnp.broadcast_to(part[:, :LANES] + part[:, LANES:], (8, LANES))
        u = dvg * vn_ref[...]
        tu_ = gv * u
        mean = jnp.concatenate([_head_mean(tu_[:, k * LANES:(k + 1) * LANES], bd) for k in range(2)], axis=1)
        dgv = rr * u - gv * (rr * rr * rr) * mean
        duv_ref[:, :SGU_W] = (dug * _gelu_grad(us, tu)).astype(BF16)
        duv_ref[:, SGU_W:] = (dgv * _gelu_grad(vs, tv)).astype(BF16)

        @pl.when(i == nb - 1)
        def _():
            acc = dvn_ref[...]
            dvn_ref[...] = acc + pltpu.roll(acc, HEAD_DIM, 1)
            for k in range(2):
                db_ref[:, k * LANES:(k + 1) * LANES] = _head_mean(db_ref[:, k * LANES:(k + 1) * LANES], bd) * float(HEAD_DIM)

    return pl.pallas_call(
        body, name=name, grid=(nb,),
        in_specs=[pl.BlockSpec((BLOCK, 2 * SGU_W), lambda i: (i, 2)), pl.BlockSpec((4, BLOCK, BLOCK), lambda i: (0, 0, 0)),
                  pl.BlockSpec((BLOCK, SGU_W), lambda i: (0, 0)), pl.BlockSpec((1, SGU_W), lambda i: (0, 0)),
                  pl.BlockSpec((BLOCK, SGU_W), lambda i: (i, 0))],
        out_specs=[pl.BlockSpec((BLOCK, 2 * SGU_W), lambda i: (i, 0)), pl.BlockSpec((4, BLOCK, BLOCK), lambda i: (0, 0, 0)),
                   pl.BlockSpec((BLOCK, SGU_W), lambda i: (0, 0)), pl.BlockSpec((8, LANES), lambda i: (0, 0))],
        out_shape=[jax.ShapeDtypeStruct((s, 2 * SGU_W), BF16), jax.ShapeDtypeStruct((4, BLOCK, BLOCK), F32),
                   jax.ShapeDtypeStruct((BLOCK, SGU_W), F32), jax.ShapeDtypeStruct((8, LANES), F32)],
        compiler_params=_params("arbitrary"),
    )(z, ws, bcol, vn, dc)


MERGE_TN = 512
MERGE_TM = 512


def _merge_fwd(a, b, c, wpa, wpb, wpc, z, name):
    s = z.shape[0]
    tm = _tile(s, (MERGE_TM, BLOCK))
    gate0 = GATE_COL0 // MERGE_TN

    def body(a_ref, b_ref, c_ref, wa_ref, wb_ref, wc_ref, g0_ref, g1_ref, g2_ref, o_ref):
        r = _sigmoid(g0_ref[...]) * _dot(a_ref[...], wa_ref[...])
        r = r + _sigmoid(g1_ref[...]) * _dot(b_ref[...], wb_ref[...])
        r = r + _sigmoid(g2_ref[...]) * _dot(c_ref[...], wc_ref[...])
        o_ref[...] = r.astype(BF16)

    x_spec = lambda w: pl.BlockSpec((tm, w), lambda i, n: (i, 0))
    w_spec = lambda w: pl.BlockSpec((w, MERGE_TN), lambda i, n: (0, n))
    g_spec = lambda br: pl.BlockSpec((tm, MERGE_TN), lambda i, n: (i, gate0 + 2 * br + n))
    return pl.pallas_call(
        body, name=name, grid=(s // tm, D_MODEL // MERGE_TN),
        in_specs=[x_spec(POOL_W), x_spec(ATTN_W), x_spec(SGU_W), w_spec(POOL_W), w_spec(ATTN_W), w_spec(SGU_W),
                  g_spec(0), g_spec(1), g_spec(2)],
        out_specs=pl.BlockSpec((tm, MERGE_TN), lambda i, n: (i, n)),
        out_shape=jax.ShapeDtypeStruct((s, D_MODEL), BF16), compiler_params=_params("parallel", "parallel"),
    )(a, b, c, wpa, wpb, wpc, z, z, z)


def _branch_bwd(br, xb, wp, z, dm, dz, name):
    s = z.shape[0]
    kb = xb.shape[1]
    tm = _tile(s, (MERGE_TM, BLOCK))
    gate0 = GATE_COL0 // MERGE_TN
    aliased = dz is not None

    def body(*refs):
        x_ref, w_ref, g_ref, dm_ref = refs[:4]
        dz_ref, dy_ref, dx_ref = refs[-3:]
        n = pl.program_id(1)
        y = _dot(x_ref[...], w_ref[...])
        sg = _sigmoid(g_ref[...])
        dmv = dm_ref[...]
        dy = (dmv * sg).astype(BF16)
        dy_ref[...] = dy
        dz_ref[...] = ((dmv * y) * (sg * (1.0 - sg))).astype(BF16)
        dx = _dot(dy, w_ref[...], 1, 1)

        @pl.when(n == 0)
        def _():
            dx_ref[...] = dx

        @pl.when(n > 0)
        def _():
            dx_ref[...] += dx

    in_specs = [pl.BlockSpec((tm, kb), lambda i, n: (i, 0)), pl.BlockSpec((kb, MERGE_TN), lambda i, n: (0, n)),
                pl.BlockSpec((tm, MERGE_TN), lambda i, n: (i, gate0 + 2 * br + n)),
                pl.BlockSpec((tm, MERGE_TN), lambda i, n: (i, n))]
    args = [xb, wp, z, dm]
    if aliased:
        in_specs.append(pl.BlockSpec(memory_space=pl.ANY))
        args.append(dz)
    return pl.pallas_call(
        body, name=name, grid=(s // tm, D_MODEL // MERGE_TN), in_specs=in_specs,
        out_specs=[pl.BlockSpec((tm, MERGE_TN), lambda i, n: (i, gate0 + 2 * br + n)),
                   pl.BlockSpec((tm, MERGE_TN), lambda i, n: (i, n)),
                   pl.BlockSpec((tm, kb), lambda i, n: (i, 0))],
        out_shape=[jax.ShapeDtypeStruct((s, IN_COLS), BF16), jax.ShapeDtypeStruct((s, D_MODEL), BF16),
                   jax.ShapeDtypeStruct((s, kb), F32)],
        input_output_aliases={4: 0} if aliased else {},
        compiler_params=_params("parallel", "arbitrary"),
    )(*args)


FFN_TM = 256
FFN_TC = 2816
FFN_STRIP = 256
FFN_UNROLL = 4


def _conv3(cur, prev, w_ref, b_ref):
    cat = jnp.concatenate([prev, cur], axis=0)
    x1 = pltpu.roll(cat, 1, 0)[HALO_CONV:]
    x2 = pltpu.roll(cat, 2, 0)[HALO_CONV:]
    return w_ref[0:1, :] * x2 + w_ref[1:2, :] * x1 + w_ref[2:3, :] * cur + b_ref[...], x1, x2


def _ffn_specs(s, tm, rows_first):
    per = tm // HALO_CONV
    if rows_first:
        cur = pl.BlockSpec((tm, FFN_TC), lambda i, j: (i, j))
        prev = pl.BlockSpec((HALO_CONV, FFN_TC), lambda i, j: (jnp.maximum(i * per - 1, 0), j))
        w = pl.BlockSpec((3, FFN_TC), lambda i, j: (0, j))
        b = pl.BlockSpec((1, FFN_TC), lambda i, j: (0, j))
    else:
        cur = pl.BlockSpec((tm, FFN_TC), lambda j, i: (i, j))
        prev = pl.BlockSpec((HALO_CONV, FFN_TC), lambda j, i: (jnp.maximum(i * per - 1, 0), j))
        w = pl.BlockSpec((3, FFN_TC), lambda j, i: (0, j))
        b = pl.BlockSpec((1, FFN_TC), lambda j, i: (0, j))
    return cur, prev, w, b


def _ffn_act_fwd(upg, upv, cwg, cwv, cbg, cbv, name):
    s = upg.shape[0]
    tm = _tile(s, (FFN_TM, BLOCK))
    cur, prev, w, b = _ffn_specs(s, tm, True)

    def body(g_ref, gp_ref, v_ref, vp_ref, wg_ref, wv_ref, bg_ref, bv_ref, o_ref):
        first = pl.program_id(0) == 0
        gate = _conv3(g_ref[...], jnp.where(first, 0.0, gp_ref[...]), wg_ref, bg_ref)[0]
        val = _conv3(v_ref[...], jnp.where(first, 0.0, vp_ref[...]), wv_ref, bv_ref)[0]
        o_ref[...] = ((gate * _sigmoid(gate)) * val).astype(BF16)

    return pl.pallas_call(
        body, name=name, grid=(s // tm, D_FF // FFN_TC), in_specs=[cur, prev, cur, prev, w, w, b, b], out_specs=cur,
        out_shape=jax.ShapeDtypeStruct((s, D_FF), BF16), compiler_params=_params("parallel", "parallel"),
    )(upg, upg, upv, upv, cwg, cwv, cbg, cbv)


def _ffn_bwd(upg, upv, cwg, cwv, cbg, cbv, dact, name):
    s = upg.shape[0]
    tm = _tile(s, (FFN_TM, BLOCK))
    per = tm // HALO_CONV
    nrow = s // tm
    n = tm + HALO_CONV
    cur, prev, w, b = _ffn_specs(s, tm, False)
    nxt = pl.BlockSpec((HALO_CONV, FFN_TC), lambda j, i: (jnp.minimum((i + 1) * per, nrow * per - 1), j))

    nch = tm // 8
    rows8 = lambda r: pl.ds(pl.multiple_of(r * 8, 8), 8)

    def body(g_ref, gp_ref, gn_ref, v_ref, vp_ref, vn_ref, wg_ref, wv_ref, bg_ref, bv_ref, da_ref, dan_ref,
             dg_ref, dv_ref, dwg_ref, dwv_ref, og_ref, ov_ref):
        i = pl.program_id(1)
        first = i == 0
        last = i == nrow - 1
        row = lax.broadcasted_iota(jnp.int32, (8, FFN_STRIP), 0)

        keep_down = {k: row >= k for k in (1, 2)}
        keep_up = {k: row < 8 - k for k in (1, 2)}

        def down(cur, prev, k):
            return jnp.where(keep_down[k], pltpu.roll(cur, k, 0), pltpu.roll(prev, k, 0))

        def up(cur, nxt, k):
            return jnp.where(keep_up[k], pltpu.roll(cur, 8 - k, 0), pltpu.roll(nxt, 8 - k, 0))

        @pl.when(first)
        def _():
            dwg_ref[...] = jnp.zeros_like(dwg_ref)
            dwv_ref[...] = jnp.zeros_like(dwv_ref)

        for c in range(FFN_TC // FFN_STRIP):
            cols = slice(c * FFN_STRIP, (c + 1) * FFN_STRIP)
            wg = [functools.partial(lambda k: wg_ref[k:k + 1, cols], k) for k in range(3)]
            wv = [functools.partial(lambda k: wv_ref[k:k + 1, cols], k) for k in range(3)]

            def conv_grads(g_cur, g_prev, v_cur, v_prev, da):
                gate = wg[0]() * down(g_cur, g_prev, 2) + wg[1]() * down(g_cur, g_prev, 1) + wg[2]() * g_cur + bg_ref[:, cols]
                val = wv[0]() * down(v_cur, v_prev, 2) + wv[1]() * down(v_cur, v_prev, 1) + wv[2]() * v_cur + bv_ref[:, cols]
                sg = _sigmoid(gate)
                return (da * val) * (sg * (1.0 + gate * (1.0 - sg))), da * (gate * sg)

            def passes(q, carry):
                for u in range(FFN_UNROLL):
                    carry = one_pass(q * FFN_UNROLL + u, carry)
                return carry

            def one_pass(r, carry, tile_end=False):
                dg_cur, dv_cur, acc = carry
                g_r, v_r = g_ref[rows8(r), cols], v_ref[rows8(r), cols]
                if tile_end:
                    g_n, v_n, da_n = gn_ref[:, cols], vn_ref[:, cols], jnp.where(last, 0.0, dan_ref[:, cols])
                else:
                    g_n, v_n, da_n = g_ref[rows8(r + 1), cols], v_ref[rows8(r + 1), cols], da_ref[rows8(r + 1), cols]
                dg_n, dv_n = conv_grads(g_n, g_r, v_n, v_r, da_n)
                new_acc = []
                for o_ref, w, d_cur, d_n, x0, a in ((og_ref, wg, dg_cur, dg_n, g_r, acc[:4]), (ov_ref, wv, dv_cur, dv_n, v_r, acc[4:])):
                    d1, d2 = up(d_cur, d_n, 1), up(d_cur, d_n, 2)
                    o_ref[rows8(r), cols] = w[2]() * d_cur + w[1]() * d1 + w[0]() * d2
                    new_acc += [a[0] + d2 * x0, a[1] + d1 * x0, a[2] + d_cur * x0, a[3] + d_cur]
                return dg_n, dv_n, tuple(new_acc)

            g_p = jnp.where(first, 0.0, gp_ref[:, cols])
            v_p = jnp.where(first, 0.0, vp_ref[:, cols])
            dg0, dv0 = conv_grads(g_ref[0:8, cols], g_p, v_ref[0:8, cols], v_p, da_ref[0:8, cols])
            zero = jnp.zeros((8, FFN_STRIP), F32)
            carry = lax.fori_loop(0, nch // FFN_UNROLL - 1, passes, (dg0, dv0, (zero,) * 8))
            for r in range(nch - FFN_UNROLL, nch):
                carry = one_pass(r, carry, tile_end=r == nch - 1)
            for ref, a in ((dwg_ref, carry[2][:4]), (dwv_ref, carry[2][4:])):
                for k in range(4):
                    ref[k:k + 1, cols] += jnp.sum(a[k], axis=0, keepdims=True)
        dg_ref[...] = og_ref[...].astype(BF16)
        dv_ref[...] = ov_ref[...].astype(BF16)

    acc = pl.BlockSpec((8, FFN_TC), lambda j, i: (0, j))
    full = jax.ShapeDtypeStruct((s, D_FF), BF16)
    accs = jax.ShapeDtypeStruct((8, D_FF), F32)
    return pl.pallas_call(
        body, name=name, grid=(D_FF // FFN_TC, nrow), in_specs=[cur, prev, nxt, cur, prev, nxt, w, w, b, b, cur, nxt],
        out_specs=[cur, cur, acc, acc], out_shape=[full, full, accs, accs],
        scratch_shapes=[pltpu.VMEM((tm, FFN_TC), F32), pltpu.VMEM((tm, FFN_TC), F32)],
        compiler_params=_params("parallel", "arbitrary"),
    )(upg, upg, upg, upv, upv, upv, cwg, cwv, cbg, cbv, dact, dact)


def _mesh_place():
    return lax.axis_index("x"), lax.axis_index("y"), lax.axis_index("c")


def _all_gather(shards, name):
    na = len(shards)

    def body(*refs):
        x_refs, out_refs = refs[:na], refs[na:2 * na]
        send_sems, recv_sems, local_sems = refs[2 * na:]
        x, y, cc = _mesh_place()
        me, sibling = (x, y, cc), (x, y, 1 - cc)
        chips = [(1 - x, y), (x, 1 - y), (1 - x, 1 - y)]

        def copy(k, a, block, to, from_input=False):
            slot = out_refs[a].at[4 * block[0] + 2 * block[1] + block[2]]
            return pltpu.make_async_remote_copy(
                src_ref=x_refs[a] if from_input else slot, dst_ref=slot, send_sem=send_sems.at[k * na + a],
                recv_sem=recv_sems.at[k * na + a], device_id=to, device_id_type=pl.DeviceIdType.MESH)

        mine = [pltpu.make_async_copy(x_refs[a], out_refs[a].at[4 * x + 2 * y + cc], local_sems.at[a]) for a in range(na)]
        for cp in mine:
            cp.start()
        first = [copy(0, a, me, sibling, True) for a in range(na)]
        first += [copy(1 + j, a, me, (*chip, cc), True) for j, chip in enumerate(chips) for a in range(na)]
        for cp in first:
            cp.start()
        passed = []
        for j, chip in enumerate(chips):
            for a in range(na):
                copy(1 + j, a, (*chip, cc), me).wait_recv()
                passed.append(copy(4 + j, a, (*chip, cc), sibling))
                passed[-1].start()
        for a in range(na):
            copy(0, a, sibling, me).wait_recv()
        for j, chip in enumerate(chips):
            for a in range(na):
                copy(4 + j, a, (*chip, 1 - cc), me).wait_recv()
        for cp in first + passed:
            cp.wait_send()
        for cp in mine:
            cp.wait()

    hbm = pl.BlockSpec(memory_space=pl.ANY)
    return pl.pallas_call(
        body, name=name, out_shape=[jax.ShapeDtypeStruct((N_DEV,) + t.shape, t.dtype) for t in shards],
        in_specs=[hbm] * na, out_specs=[hbm] * na,
        scratch_shapes=[pltpu.SemaphoreType.DMA((7 * na,)), pltpu.SemaphoreType.DMA((7 * na,)), pltpu.SemaphoreType.DMA((na,))],
    )(*shards)


def _exchange(blocks, name):
    na = len(blocks)

    def body(*refs):
        g_refs, out_refs = refs[:na], refs[na:2 * na]
        send_sems, recv_sems, local_sems = refs[2 * na:]
        x, y, cc = _mesh_place()
        me = 4 * x + 2 * y + cc
        mine = [pltpu.make_async_copy(g_refs[a].at[me], out_refs[a].at[me], local_sems.at[a]) for a in range(na)]
        for cp in mine:
            cp.start()
        sends, lands = [], []
        for k in range(1, N_DEV):
            px = 1 - x if (k >> 2) & 1 else x
            py = 1 - y if (k >> 1) & 1 else y
            pc = 1 - cc if k & 1 else cc
            peer = 4 * px + 2 * py + pc
            for a in range(na):
                sem = (k - 1) * na + a
                sends.append(pltpu.make_async_remote_copy(
                    src_ref=g_refs[a].at[peer], dst_ref=out_refs[a].at[me], send_sem=send_sems.at[sem], recv_sem=recv_sems.at[sem],
                    device_id=(px, py, pc), device_id_type=pl.DeviceIdType.MESH))
                lands.append(pltpu.make_async_remote_copy(
                    src_ref=g_refs[a].at[peer], dst_ref=out_refs[a].at[peer], send_sem=send_sems.at[sem], recv_sem=recv_sems.at[sem],
                    device_id=(px, py, pc), device_id_type=pl.DeviceIdType.MESH))
        for cp in sends:
            cp.start()
        for cp in lands:
            cp.wait_recv()
        for cp in sends:
            cp.wait_send()
        for cp in mine:
            cp.wait()

    hbm = pl.BlockSpec(memory_space=pl.ANY)
    return pl.pallas_call(
        body, name=name, out_shape=[jax.ShapeDtypeStruct(t.shape, t.dtype) for t in blocks],
        in_specs=[hbm] * na, out_specs=[hbm] * na,
        scratch_shapes=[pltpu.SemaphoreType.DMA((7 * na,)), pltpu.SemaphoreType.DMA((7 * na,)), pltpu.SemaphoreType.DMA((na,))],
    )(*blocks)


def _peer(k):
    x, y, cc = _mesh_place()
    px = 1 - x if (k >> 2) & 1 else x
    py = 1 - y if (k >> 1) & 1 else y
    pc = 1 - cc if k & 1 else cc
    return (px, py, pc), 4 * px + 2 * py + pc


def _push_copy(src_ref, land_ref, k, a, na, send_sems, recv_sems, indexed, landed):
    x, y, cc = _mesh_place()
    place, peer = _peer(k)
    sem = (k - 1) * na + a
    return pltpu.make_async_remote_copy(
        src_ref=src_ref.at[peer] if indexed else src_ref, dst_ref=land_ref.at[peer if landed else 4 * x + 2 * y + cc],
        send_sem=send_sems.at[sem], recv_sem=recv_sems.at[sem], device_id=place, device_id_type=pl.DeviceIdType.MESH)


_HBM = pl.BlockSpec(memory_space=pltpu.HBM)
_SEM = pl.BlockSpec(memory_space=pltpu.SEMAPHORE)
_EFFECT = pltpu.SideEffectType.DATAFLOW_SIDE_EFFECTING


def _push_start(srcs, indexed, name):
    na = len(srcs)
    lands = [lax.empty(t.shape if indexed else (N_DEV,) + t.shape, t.dtype) for t in srcs]

    def body(*refs):
        src_refs, land_refs = refs[:na], refs[na:2 * na]
        send_sems, recv_sems = refs[2 * na], refs[2 * na + 1]
        token = refs[-1]
        for k in range(1, N_DEV):
            for a in range(na):
                _push_copy(src_refs[a], land_refs[a], k, a, na, send_sems, recv_sems, indexed, False).start()
        token[...] = jnp.zeros_like(token)

    sems = pltpu.SemaphoreType.DMA((7 * na,))
    out = pl.pallas_call(
        body, name=name,
        out_shape=(sems, sems, *[pltpu.HBM(t.shape, t.dtype) for t in srcs], *[pltpu.HBM(t.shape, t.dtype) for t in lands],
                   jax.ShapeDtypeStruct((8, LANES), F32)),
        in_specs=[_HBM] * (2 * na), out_specs=(_SEM, _SEM, *[_HBM] * (2 * na), pl.BlockSpec(memory_space=pltpu.VMEM)),
        input_output_aliases={i: 2 + i for i in range(2 * na)},
        compiler_params=pltpu.CompilerParams(has_side_effects=_EFFECT),
    )(*[pltpu.with_memory_space_constraint(t, pltpu.HBM) for t in srcs + lands])
    return out[0], out[1], list(out[2:2 + na]), list(out[2 + na:2 + 2 * na]), out[-1]


def _push_wait(started, indexed, after, name):
    send_sems, recv_sems, srcs, lands, _ = started
    na = len(srcs)

    def body(*refs):
        src_refs, land_refs = refs[:na], refs[na:2 * na]
        send_sems, recv_sems = refs[2 * na], refs[2 * na + 1]
        for k in range(1, N_DEV):
            for a in range(na):
                copy = _push_copy(src_refs[a], land_refs[a], k, a, na, send_sems, recv_sems, indexed, True)
                copy.wait_send()
                copy.wait_recv()

    out = pl.pallas_call(
        body, name=name, out_shape=[pltpu.HBM(t.shape, t.dtype) for t in srcs + lands],
        in_specs=[_HBM] * (2 * na) + [_SEM, _SEM, pl.BlockSpec(memory_space=pl.ANY)], out_specs=[_HBM] * (2 * na),
        input_output_aliases={i: i for i in range(2 * na)},
        compiler_params=pltpu.CompilerParams(has_side_effects=_EFFECT),
    )(*srcs, *lands, send_sems, recv_sems, after)
    x, y, cc = _mesh_place()
    me = 4 * x + 2 * y + cc
    return [lax.dynamic_update_index_in_dim(
        land, lax.dynamic_index_in_dim(src, me, 0, keepdims=False) if indexed else src, me, 0)
        for src, land in zip(out[:na], out[na:])]


def _adamw_sum(parts, w, m, v, name):
    _, r, c = parts.shape
    tr = _tile(r, (256, 128, 64, 32, 16, 8))

    def body(p_ref, w_ref, m_ref, v_ref, g_ref, d_ref, nm_ref, nv_ref):
        _adam_store(_sum_parts(p_ref), w_ref, m_ref, v_ref, g_ref, d_ref, nm_ref, nv_ref)

    row = pl.BlockSpec((tr, c), lambda i: (i, 0))
    shp = jax.ShapeDtypeStruct((r, c), F32)
    return pl.pallas_call(
        body, name=name, grid=(r // tr,), in_specs=[pl.BlockSpec((N_DEV, tr, c), lambda i: (0, i, 0)), row, row, row],
        out_specs=[row, row, row, row], out_shape=[shp, shp, shp, shp], compiler_params=_params("parallel"),
    )(parts, w, m, v)


def _sum_parts(p_ref):
    g = p_ref[0].astype(F32)
    for k in range(1, N_DEV):
        g = g + p_ref[k].astype(F32)
    return g


def _adam_store(g, w_ref, m_ref, v_ref, g_ref, d_ref, nm_ref, nv_ref):
    nm = ADAM_B1 * m_ref[...] + (1.0 - ADAM_B1) * g
    nv = ADAM_B2 * v_ref[...] + (1.0 - ADAM_B2) * (g * g)
    m_hat = nm / (1.0 - ADAM_B1 ** ADAM_STEP)
    v_hat = nv / (1.0 - ADAM_B2 ** ADAM_STEP)
    g_ref[...] = g
    nm_ref[...] = nm
    nv_ref[...] = nv
    d_ref[...] = -ADAM_LR * (m_hat / (jnp.sqrt(v_hat) + ADAM_EPS) + ADAM_WD * w_ref[...])


def _adamw_weight(parts, w, m, v, name):
    _, r, c = w.shape
    tr = _tile(r, (256, 128, 176))
    nr = r // tr

    def body(p0_ref, p1_ref, w_ref, m_ref, v_ref, g_ref, d_ref, nm_ref, nv_ref):
        g = jnp.where(pl.program_id(0) == 0, _sum_parts(p0_ref), _sum_parts(p1_ref))
        _adam_store(g, w_ref, m_ref, v_ref, g_ref, d_ref, nm_ref, nv_ref)

    part = lambda layer: pl.BlockSpec(
        (N_DEV, tr, c), lambda l, i: (0, jnp.where(l == layer, i, (nr - 1) * (1 - layer)), 0))
    row = pl.BlockSpec((None, tr, c), lambda l, i: (l, i, 0))
    shp = jax.ShapeDtypeStruct(w.shape, F32)
    return pl.pallas_call(
        body, name=name, grid=(DEPTH, nr), in_specs=[part(0), part(1), row, row, row],
        out_specs=[row, row, row, row], out_shape=[shp, shp, shp, shp], compiler_params=_params("arbitrary", "arbitrary"),
    )(parts[0], parts[1], w, m, v)


def _full_to_slots(name, t):
    k, n = t.shape
    if name in ROW_SHARDED:
        return t.reshape(N_DEV, k // N_DEV, n)
    return t.reshape(k, N_DEV, n // N_DEV).transpose(1, 0, 2)


def _slots_to_full(name, t):
    _, r, c = t.shape
    if name in ROW_SHARDED:
        return t.reshape(N_DEV * r, c)
    return t.transpose(1, 0, 2).reshape(r, N_DEV * c)


def _small_sizes(shapes):
    return [(n, shapes[n], -(-int(math.prod(shapes[n])) // (8 * LANES)) * 8) for n in SMALL]


def _pack_small(tree, shapes):
    rows = []
    for n, shp, nrow in _small_sizes(shapes):
        flat = tree[n].reshape(-1)
        rows.append(jnp.pad(flat, (0, nrow * LANES - flat.shape[0])).reshape(nrow, LANES))
    total = sum(r.shape[0] for r in rows)
    rows.append(jnp.zeros((-total % SMALL_ROW_TILE, LANES), F32))
    return jnp.concatenate(rows, axis=0)


def _unpack_small(buf, shapes):
    out, r0 = {}, 0
    for n, shp, nrow in _small_sizes(shapes):
        out[n] = buf[r0:r0 + nrow].reshape(-1)[:int(math.prod(shp))].reshape(shp)
        r0 += nrow
    return out


def _block_diag(w):
    g = w.shape[0]
    eye = jnp.eye(g, dtype=w.dtype)
    return (eye[:, None, :, None] * w[:, :, None, :]).reshape(g * HEAD_DIM, g * HEAD_DIM)


def kernel(x, positions, norm1, w_in, q_norm, k_norm, sinks, w_pool, pool_scale, sgu_v_norm, w_s, b_s, w_proj_a, w_proj_b, w_proj_c, w_out, norm2, w_up, conv_w, conv_b, w_down, loss_target, m_norm1, m_w_in, m_q_norm, m_k_norm, m_sinks, m_w_pool, m_pool_scale, m_sgu_v_norm, m_w_s, m_b_s, m_w_proj_a, m_w_proj_b, m_w_proj_c, m_w_out, m_norm2, m_w_up, m_conv_w, m_conv_b, m_w_down, v_norm1, v_w_in, v_q_norm, v_k_norm, v_sinks, v_w_pool, v_pool_scale, v_sgu_v_norm, v_w_s, v_b_s, v_w_proj_a, v_w_proj_b, v_w_proj_c, v_w_out, v_norm2, v_w_up, v_conv_w, v_conv_b, v_w_down):
    names = ("norm1", "w_in", "q_norm", "k_norm", "sinks", "w_pool", "pool_scale", "sgu_v_norm", "w_s", "b_s", "w_proj_a",
             "w_proj_b", "w_proj_c", "w_out", "norm2", "w_up", "conv_w", "conv_b", "w_down")
    wts = dict(zip(names, (norm1, w_in, q_norm, k_norm, sinks, w_pool, pool_scale, sgu_v_norm, w_s, b_s, w_proj_a, w_proj_b,
                           w_proj_c, w_out, norm2, w_up, conv_w, conv_b, w_down)))
    mom = dict(zip(names, (m_norm1, m_w_in, m_q_norm, m_k_norm, m_sinks, m_w_pool, m_pool_scale, m_sgu_v_norm, m_w_s, m_b_s,
                           m_w_proj_a, m_w_proj_b, m_w_proj_c, m_w_out, m_norm2, m_w_up, m_conv_w, m_conv_b, m_w_down)))
    var = dict(zip(names, (v_norm1, v_w_in, v_q_norm, v_k_norm, v_sinks, v_w_pool, v_pool_scale, v_sgu_v_norm, v_w_s, v_b_s,
                           v_w_proj_a, v_w_proj_b, v_w_proj_c, v_w_out, v_norm2, v_w_up, v_conv_w, v_conv_b, v_w_down)))
    xs = x[0]
    target = loss_target[0]
    s = xs.shape[0]

    inv_freq = ROPE_THETA ** (-jnp.arange(0, HEAD_DIM, 2, dtype=F32) / HEAD_DIM)
    ang = positions[0].astype(F32)[:, None] * inv_freq
    cosf = jnp.tile(jnp.cos(ang), (1, 4))
    sinf = jnp.tile(jnp.concatenate([-jnp.sin(ang), jnp.sin(ang)], axis=1), (1, 2))

    local = [{n: wts[n][l] if n == "conv_w" else wts[n][l].astype(BF16) for n in SHARDED} for l in range(DEPTH)]
    later = SHARDED[1:]
    full = [{"w_in": _slots_to_full("w_in", _all_gather([local[0]["w_in"]], "gather_w_in_0")[0])}, None]
    gather0 = _push_start([local[0][n] for n in later], False, "gather_rest_0_start")
    norm1_first = norm1[0] + gather0[4][0, 0]

    def layer_consts(l):
        return dict(
            bdw=_block_diag(w_pool[l]).astype(BF16), qn=jnp.tile(q_norm[l], 2).reshape(1, LANES),
            kn=jnp.tile(k_norm[l], 2).reshape(1, LANES), vn=jnp.tile(sgu_v_norm[l], 4).reshape(1, SGU_W),
            bcol=jnp.repeat(b_s[l].T, HEAD_DIM, axis=1),
            cbg=conv_b[l][:D_FF].reshape(1, D_FF), cbv=conv_b[l][D_FF:].reshape(1, D_FF))

    gate_cols, val_cols = (0, D_FF), (D_FF, D_FF)

    saved = []
    cur = xs
    for l in range(DEPTH):
        if l == 1:
            landed = _push_wait(gather1, False, cur, "gather_weights_1_wait")
            full[1] = {n: _slots_to_full(n, t) for n, t in zip(SHARDED, landed)}
        fw, k = full[l], layer_consts(l)
        h1 = _rms_fwd(cur, norm1_first if l == 0 else norm1[l], f"rms1_fwd_{l}")
        z = _mm(h1, fw["w_in"], name=f"in_proj_{l}")
        a = _pool_fwd(z, k["bdw"], pool_scale[l], f"pool_fwd_{l}")
        b = _attn_fwd(z, cosf, sinf, k["qn"], k["kn"], sinks[l], f"attn_fwd_{l}")
        c = _sgu_fwd(z, w_s[l], k["bcol"], k["vn"], f"sgu_fwd_{l}")
        w_proj_a_l = fw.get("w_proj_a")
        if l == 0:
            landed = _push_wait(gather0, False, c, "gather_rest_0_wait")
            fw.update({n: _slots_to_full(n, t) for n, t in zip(later, landed)})
            gather1 = _push_start([local[1][n] for n in SHARDED], False, "gather_weights_1_start")
            w_proj_a_l = fw["w_proj_a"] + gather1[4][0, 0].astype(BF16)
        merged = _merge_fwd(a, b, c, w_proj_a_l, fw["w_proj_b"], fw["w_proj_c"], z, f"merge_fwd_{l}")
        x1 = _mm(merged, fw["w_out"], add=cur, name=f"out_proj_{l}")
        h2 = _rms_fwd(x1, norm2[l], f"rms2_fwd_{l}")
        upg = _mm(h2, fw["w_up"], b_n=gate_cols, name=f"up_gate_{l}")
        upv = _mm(h2, fw["w_up"], b_n=val_cols, name=f"up_val_{l}")
        k["cwg"], k["cwv"] = fw["conv_w"][:, :D_FF], fw["conv_w"][:, D_FF:]
        act = _ffn_act_fwd(upg, upv, k["cwg"], k["cwv"], k["cbg"], k["cbv"], f"ffn_act_fwd_{l}")
        x2 = _mm(act, fw["w_down"], add=x1, name=f"down_proj_{l}")
        saved.append(dict(x0=cur, h1=h1, z=z, a=a, b=b, c=c, merged=merged, x1=x1, h2=h2, upg=upg, upv=upv, act=act))
        cur = x2

    dcur, loss_tile = _loss_head(cur, target)
    loss = lax.psum(loss_tile[0, 0], ("x", "y", "c"))

    gsmall = [None] * DEPTH

    def slots_of(grads):
        return [_full_to_slots(n, t) for n, t in grads.items()]

    for l in reversed(range(DEPTH)):
        fw, k, sv = full[l], layer_consts(l), saved[l]
        k["cwg"], k["cwv"] = fw["conv_w"][:, :D_FF], fw["conv_w"][:, D_FF:]
        staged = l == 0
        wgrad = functools.partial(_mm, ta=True, out_dtype=BF16)
        w_down_l = fw["w_down"] + exchange1[4][0, 0].astype(BF16) if staged else fw["w_down"]
        dact = _mm(dcur, w_down_l, tb=True, name=f"down_proj_bwd_{l}")
        g_down = wgrad(sv["act"], dcur, name=f"down_proj_wgrad_{l}")
        dg0, dv0, dcg, dcv = _ffn_bwd(sv["upg"], sv["upv"], k["cwg"], k["cwv"], k["cbg"], k["cbv"], dact, f"ffn_bwd_{l}")
        dh2 = _mm(dg0, fw["w_up"], tb=True, b_k=gate_cols, name=f"up_gate_bwd_{l}")
        dh2 = _mm(dv0, fw["w_up"], tb=True, b_k=val_cols, add=dh2, name=f"up_val_bwd_{l}")
        g_up = wgrad(sv["h2"], dg0, out_cols=(0, 2 * D_FF), name=f"up_gate_wgrad_{l}")
        g_up = wgrad(sv["h2"], dv0, out_cols=(D_FF, 2 * D_FF), out_into=g_up, name=f"up_val_wgrad_{l}")
        g_ffn = dict(w_up=g_up, w_down=g_down, conv_w=jnp.concatenate([dcg[0:3], dcv[0:3]], axis=1))
        norm2_l = norm2[l]
        if staged:
            parts1 = dict(zip(SHARDED, _push_wait(exchange1, True, g_up, "exchange_grads_1_wait")))
            exchange_ffn = _push_start(slots_of(g_ffn), True, "exchange_ffn_0_start")
            norm2_l = norm2_l + exchange_ffn[4][0, 0]
        dx1, g_norm2 = _rms_bwd(sv["x1"], norm2_l, dh2, dcur, f"rms2_bwd_{l}")
        dmerged = _mm(dx1, fw["w_out"], tb=True, name=f"out_proj_bwd_{l}")
        g_out = wgrad(sv["merged"], dx1, name=f"out_proj_wgrad_{l}")
        dz, dya, da = _branch_bwd(0, sv["a"], fw["w_proj_a"], sv["z"], dmerged, None, f"branch_a_bwd_{l}")
        dz, dyb, db = _branch_bwd(1, sv["b"], fw["w_proj_b"], sv["z"], dmerged, dz, f"branch_b_bwd_{l}")
        dz, dyc, dc = _branch_bwd(2, sv["c"], fw["w_proj_c"], sv["z"], dmerged, dz, f"branch_c_bwd_{l}")
        g_mix = dict(w_proj_a=wgrad(sv["a"], dya, name=f"proj_a_wgrad_{l}"), w_proj_b=wgrad(sv["b"], dyb, name=f"proj_b_wgrad_{l}"),
                     w_proj_c=wgrad(sv["c"], dyc, name=f"proj_c_wgrad_{l}"), w_out=g_out)
        pool_scale_l = pool_scale[l]
        if staged:
            exchange_mix = _push_start(slots_of(g_mix), True, "exchange_mixer_0_start")
            pool_scale_l = pool_scale_l + exchange_mix[4][0, 0]
        dxp, g_bdw, g_pscale = _pool_bwd(sv["z"], da, k["bdw"], pool_scale_l, f"pool_bwd_{l}")
        dq, dkc, dkp, dvc, dvp, g_qn, g_sink = _attn_bwd(sv["z"], cosf, sinf, k["qn"], k["kn"], sinks[l], db, f"attn_bwd_{l}")
        duv, g_ws, g_bacc, g_vn = _sgu_bwd(sv["z"], w_s[l], k["bcol"], k["vn"], dc, f"sgu_bwd_{l}")
        dz, g_kn = _kv_post(sv["z"], cosf, sinf, k["kn"], dkc, dkp, dvc, dvp, dxp, dq, duv, dz, f"kv_post_{l}")
        g_in = dict(w_in=wgrad(sv["h1"], dz, name=f"in_proj_wgrad_{l}"))
        norm1_l = norm1[l]
        if staged:
            exchange_in = _push_start(slots_of(g_in), True, "exchange_w_in_0_start")
            norm1_l = norm1_l + exchange_in[4][0, 0]
        dh1 = _mm(dz, fw["w_in"], tb=True, name=f"in_proj_bwd_{l}")
        dcur, g_norm1 = _rms_bwd(sv["x0"], norm1_l, dh1, dx1, f"rms1_bwd_{l}")
        if not staged:
            exchange1 = _push_start(slots_of({n: {**g_in, **g_mix, **g_ffn}[n] for n in SHARDED}), True, "exchange_grads_1_start")
        gsmall[l] = dict(
            norm1=g_norm1[0], q_norm=g_qn[0, :HEAD_DIM], k_norm=g_kn[0, :HEAD_DIM], sinks=g_sink[:, 0],
            w_pool=jnp.stack([g_bdw[g * HEAD_DIM:(g + 1) * HEAD_DIM, g * HEAD_DIM:(g + 1) * HEAD_DIM] for g in range(4)]),
            pool_scale=g_pscale[0], sgu_v_norm=g_vn[0, :HEAD_DIM], w_s=g_ws, b_s=g_bacc[:, ::HEAD_DIM].T,
            norm2=g_norm2[0], conv_b=jnp.concatenate([dcg[3], dcv[3]]))
    grad_x = dcur[None]

    shapes = {n: wts[n].shape for n in SMALL}
    gs = _pack_small({n: jnp.stack([gsmall[l][n] for l in range(DEPTH)]) for n in SMALL}, shapes)
    gs_all = _all_gather([gs], "gather_small_grads")[0]
    g_s, d_s, m_s, v_s = _adamw_sum(gs_all, _pack_small(wts, shapes), _pack_small(mom, shapes), _pack_small(var, shapes), "adamw_replicated")
    small = [_unpack_small(t, shapes) for t in (g_s, d_s, m_s, v_s)]

    parts0 = dict(zip(g_ffn, _push_wait(exchange_ffn, True, g_s, "exchange_ffn_0_wait")))
    parts0.update(zip(g_mix, _push_wait(exchange_mix, True, g_s, "exchange_mixer_0_wait")))
    update = lambda n: _adamw_weight([parts0[n], parts1[n]], wts[n], mom[n], var[n], f"adamw_{n}")
    big = {n: update(n) for n in SHARDED[1:]}
    parts0.update(zip(g_in, _push_wait(exchange_in, True, big["w_up"][0], "exchange_w_in_0_wait")))
    big["w_in"] = update("w_in")

    outs = [loss, grad_x]
    for kind in range(4):
        outs += [small[kind][n] if n in SMALL else big[n][kind] for n in names]
    return tuple(outs)
```

```python
import functools
import math

import jax
import jax.numpy as jnp
from jax import lax
from jax.experimental import pallas as pl
from jax.experimental.pallas import tpu as pltpu

F32 = jnp.float32
BF16 = jnp.bfloat16

D_MODEL = 1024
DEPTH = 2
HEAD_DIM = 64
N_Q_HEADS = 8
Q_PER_KV = 4
BLOCK = 128
POOL_W = 256
ATTN_W = 512
KV_W = 128
SGU_W = 256
IN_COLS = 4608
GATE_COL0 = 1536
D_FF = 2816
EPS = 1e-6
ROPE_THETA = 10000.0
N_DEV = 8
LANES = 128
HALO_POOL = 16
HALO_CONV = 8

ADAM_LR = 0.001
ADAM_B1 = 0.9
ADAM_B2 = 0.999
ADAM_EPS = 1e-08
ADAM_WD = 0.01
ADAM_STEP = 10

VMEM_LIMIT = 48 * 1024 * 1024
MM_VMEM_BUDGET = 40 * 1024 * 1024

SHARDED = ("w_in", "w_proj_a", "w_proj_b", "w_proj_c", "w_out", "w_up", "w_down", "conv_w")
ROW_SHARDED = ("w_out", "w_down")
SMALL_ROW_TILE = 256
SMALL = ("norm1", "q_norm", "k_norm", "sinks", "w_pool", "pool_scale", "sgu_v_norm", "w_s", "b_s", "norm2", "conv_b")

_GELU_C = math.sqrt(2.0 / math.pi)
_GELU_A = 0.044715


def _params(*sem):
    return pltpu.CompilerParams(dimension_semantics=sem, vmem_limit_bytes=VMEM_LIMIT)


def _tile(n, prefs):
    for t in prefs:
        if t <= n and n % t == 0:
            return t
    return n


def _head_mean_matrix():
    r = lax.broadcasted_iota(jnp.int32, (LANES, LANES), 0)
    c = lax.broadcasted_iota(jnp.int32, (LANES, LANES), 1)
    return jnp.where((r >= HEAD_DIM) == (c >= HEAD_DIM), 1.0 / HEAD_DIM, 0.0).astype(BF16)


def _head_mean(v, bd):
    hi = v.astype(BF16)
    rest = v - hi.astype(F32)
    mid = rest.astype(BF16)
    lo = (rest - mid.astype(F32)).astype(BF16)
    mm = lambda p: jnp.dot(p, bd, preferred_element_type=F32)
    return mm(hi) + (mm(mid) + mm(lo))


def _rot_half(t):
    lane = lax.broadcasted_iota(jnp.int32, t.shape, 1)
    return jnp.where((lane & 32) == 0, pltpu.roll(t, LANES - 32, 1), pltpu.roll(t, 32, 1))


def _norm_rope(t, gn, cosf, sinf, bd):
    r = lax.rsqrt(_head_mean(t * t, bd) + EPS)
    n = t * r
    y = n * gn
    return y * cosf + _rot_half(y) * sinf, n, r


def _norm_rope_bwd(d, t, n, r, gn, cosf, sinf, bd):
    dy = d * cosf + _rot_half(d * sinf)
    dgn = jnp.sum(dy * n, axis=0, keepdims=True)
    u = dy * gn
    dt = r * u - t * (r * r * r) * _head_mean(t * u, bd)
    return dt, dgn


def _gelu(x):
    t = jnp.tanh(_GELU_C * (x + _GELU_A * (x * x * x)))
    return 0.5 * x * (1.0 + t), t


def _gelu_grad(x, t):
    return 0.5 * (1.0 + t) + 0.5 * x * (1.0 - t * t) * (_GELU_C * (1.0 + 3.0 * _GELU_A * x * x))


def _sigmoid(x):
    return 0.5 * jnp.tanh(0.5 * x) + 0.5


def _dot(a, b, ca=1, cb=0):
    return lax.dot_general(a.astype(BF16), b.astype(BF16), (((ca,), (cb,)), ((), ())), preferred_element_type=F32)


def _mm(a, b, *, ta=False, tb=False, add=None, out_dtype=F32, name, b_n=None, b_k=None, out_cols=None, out_into=None):
    m = a.shape[1] if ta else a.shape[0]
    k = a.shape[0] if ta else a.shape[1]
    n = b_n[1] if b_n else (b.shape[0] if tb else b.shape[1])
    tn = _tile(n, (1024, 1152, 1408, 512, 256, 128))
    has_add = add is not None
    fits = []
    for tm in (1024, 1408, 512, 256, 128):
        for tk in (k, 4608, 2816, 2048, 1408, 1152, 1024, 512, 256, 128):
            if tm <= m and m % tm == 0 and tk <= k and k % tk == 0:
                need = (2 * (tm * tk * a.dtype.itemsize + tk * tn * b.dtype.itemsize) + 2 * tm * tn * jnp.dtype(out_dtype).itemsize
                        + 2 * tm * tn * 4 * has_add + tm * tn * 4 * (tk < k))
                if need <= MM_VMEM_BUDGET:
                    fits.append((k // tk, -tm, tm, tk))
    if fits:
        _, _, tm, tk = min(fits)
    else:
        tm, tk = _tile(m, (256, 128)), _tile(k, (512, 256, 128))
    nk = k // tk
    n0 = b_n[0] // tn if b_n else 0
    k0 = b_k[0] // tk if b_k else 0
    o0, n_out = (out_cols[0] // tn, out_cols[1]) if out_cols else (0, n)
    n_in = 2 + has_add + (out_into is not None)

    def body(*refs):
        a_ref, b_ref = refs[0], refs[1]
        add_ref = refs[2] if has_add else None
        o_ref = refs[n_in]
        def finish(r):
            if has_add:
                r = r + add_ref[...]
            o_ref[...] = r.astype(out_dtype)

        if nk == 1:
            finish(_dot(a_ref[...], b_ref[...], 0 if ta else 1, 1 if tb else 0))
        else:
            acc_ref = refs[-1]
            kk = pl.program_id(2)

            @pl.when(kk == 0)
            def _():
                acc_ref[...] = jnp.zeros_like(acc_ref)

            acc_ref[...] += _dot(a_ref[...], b_ref[...], 0 if ta else 1, 1 if tb else 0)

            @pl.when(kk == nk - 1)
            def _():
                finish(acc_ref[...])

    a_spec = pl.BlockSpec((tk, tm), lambda i, j, kk: (kk, i)) if ta else pl.BlockSpec((tm, tk), lambda i, j, kk: (i, kk))
    if tb:
        b_spec = pl.BlockSpec((tn, tk), lambda i, j, kk: (j + n0, kk + k0))
    else:
        b_spec = pl.BlockSpec((tk, tn), lambda i, j, kk: (kk + k0, j + n0))
    in_specs = [a_spec, b_spec] + ([pl.BlockSpec((tm, tn), lambda i, j, kk: (i, j))] if has_add else [])
    args = (a, b) + ((add,) if has_add else ())
    if out_into is not None:
        in_specs.append(pl.BlockSpec(memory_space=pl.ANY))
        args += (out_into,)
    return pl.pallas_call(
        body, name=name, grid=(m // tm, n // tn, nk), in_specs=in_specs,
        out_specs=pl.BlockSpec((tm, tn), lambda i, j, kk: (i, j + o0)),
        out_shape=jax.ShapeDtypeStruct((m, n_out), out_dtype),
        scratch_shapes=[pltpu.VMEM((tm, tn), F32)] if nk > 1 else [],
        input_output_aliases={n_in - 1: 0} if out_into is not None else {},
        compiler_params=_params("parallel", "parallel", "arbitrary"),
    )(*args)


def _rms_fwd(x, g, name):
    s, d = x.shape
    tr = _tile(s, (512, 256, 128))

    def body(x_ref, g_ref, h_ref):
        xv = x_ref[...]
        r = lax.rsqrt(jnp.mean(xv * xv, axis=-1, keepdims=True) + EPS)
        h_ref[...] = ((xv * r) * g_ref[...]).astype(BF16)

    return pl.pallas_call(
        body, name=name, grid=(s // tr,),
        in_specs=[pl.BlockSpec((tr, d), lambda i: (i, 0)), pl.BlockSpec((1, d), lambda i: (0, 0))],
        out_specs=pl.BlockSpec((tr, d), lambda i: (i, 0)),
        out_shape=jax.ShapeDtypeStruct((s, d), BF16), compiler_params=_params("parallel"),
    )(x, g.reshape(1, d))


def _rms_bwd(x, g, dh, dres, name):
    s, d = x.shape
    tr = _tile(s, (512, 256, 128))

    def body(x_ref, g_ref, dh_ref, dres_ref, dx_ref, dg_ref):
        xv = x_ref[...]
        r = lax.rsqrt(jnp.mean(xv * xv, axis=-1, keepdims=True) + EPS)
        dhv = dh_ref[...]
        u = dhv * g_ref[...]
        dx_ref[...] = dres_ref[...] + (r * u - xv * (r * r * r) * jnp.mean(xv * u, axis=-1, keepdims=True))
        part = jnp.sum(dhv * (xv * r), axis=0, keepdims=True)

        @pl.when(pl.program_id(0) == 0)
        def _():
            dg_ref[...] = part

        @pl.when(pl.program_id(0) > 0)
        def _():
            dg_ref[...] += part

    row = pl.BlockSpec((tr, d), lambda i: (i, 0))
    vec = pl.BlockSpec((1, d), lambda i: (0, 0))
    return pl.pallas_call(
        body, name=name, grid=(s // tr,), in_specs=[row, vec, row, row], out_specs=[row, vec],
        out_shape=[jax.ShapeDtypeStruct((s, d), F32), jax.ShapeDtypeStruct((1, d), F32)],
        compiler_params=_params("arbitrary"),
    )(x, g.reshape(1, d), dh, dres)


def _loss_head(y, target):
    s, d = y.shape
    tr = _tile(s, (512, 256, 128))

    def body(y_ref, t_ref, dy_ref, l_ref):
        err = y_ref[...] - t_ref[...]
        dy_ref[...] = err * (1.0 / d)
        part = jnp.sum(jnp.sum(err * err, axis=-1, keepdims=True) * (1.0 / d), axis=0, keepdims=True) * 0.5
        part = jnp.broadcast_to(part, (8, LANES))

        @pl.when(pl.program_id(0) == 0)
        def _():
            l_ref[...] = part

        @pl.when(pl.program_id(0) > 0)
        def _():
            l_ref[...] += part

    row = pl.BlockSpec((tr, d), lambda i: (i, 0))
    acc = pl.BlockSpec((8, LANES), lambda i: (0, 0))
    return pl.pallas_call(
        body, name="loss_head", grid=(s // tr,), in_specs=[row, row], out_specs=[row, acc],
        out_shape=[jax.ShapeDtypeStruct((s, d), F32), jax.ShapeDtypeStruct((8, LANES), F32)],
        compiler_params=_params("arbitrary"),
    )(y, target)


def _pool_lane_select(lane, v2, v4, v8, v16):
    return jnp.where(lane < 64, v2, jnp.where(lane < 128, v4, jnp.where(lane < 192, v8, v16)))


def _pool_diff(xc, xp, row0):
    n = BLOCK + HALO_POOL
    cat = jnp.concatenate([xp, xc], axis=0)
    s2 = cat + pltpu.roll(cat, 1, 0)
    s4 = s2 + pltpu.roll(s2, 2, 0)
    s8 = s4 + pltpu.roll(s4, 4, 0)
    s16 = s8 + pltpu.roll(s8, 8, 0)
    lane = lax.broadcasted_iota(jnp.int32, (n, POOL_W), 1)
    wsum = _pool_lane_select(lane, s2, s4, s8, s16)[HALO_POOL:]
    return wsum / _pool_count(row0, BLOCK) - xc


def _pool_count(row0, rows):
    lane = lax.broadcasted_iota(jnp.int32, (rows, POOL_W), 1)
    t = lax.broadcasted_iota(jnp.int32, (rows, POOL_W), 0) + row0
    return jnp.minimum(t + 1, _pool_lane_select(lane, 2, 4, 8, 16)).astype(F32)


def _pool_fwd(z, bdw, scale, name):
    s = z.shape[0]
    nb = s // BLOCK

    def body(xc_ref, xp_ref, w_ref, sc_ref, a_ref):
        i = pl.program_id(0)
        xp = jnp.where(i > 0, xp_ref[...], 0.0)
        diff = _pool_diff(xc_ref[...], xp, i * BLOCK)
        a_ref[...] = (_dot(diff, w_ref[...]) * sc_ref[...]).astype(BF16)

    return pl.pallas_call(
        body, name=name, grid=(nb,),
        in_specs=[pl.BlockSpec((BLOCK, POOL_W), lambda i: (i, 0)),
                  pl.BlockSpec((HALO_POOL, POOL_W), lambda i: (jnp.maximum(i * (BLOCK // HALO_POOL) - 1, 0), 0)),
                  pl.BlockSpec((POOL_W, POOL_W), lambda i: (0, 0)),
                  pl.BlockSpec((1, POOL_W), lambda i: (0, 0))],
        out_specs=pl.BlockSpec((BLOCK, POOL_W), lambda i: (i, 0)),
        out_shape=jax.ShapeDtypeStruct((s, POOL_W), BF16), compiler_params=_params("parallel"),
    )(z, z, bdw, scale.reshape(1, POOL_W))


def _pool_bwd(z, da, bdw, scale, name):
    s = z.shape[0]
    nb = s // BLOCK
    per = BLOCK // HALO_POOL
    n = BLOCK + HALO_POOL

    def body(xc_ref, xp_ref, dac_ref, dan_ref, w_ref, sc_ref, dx_ref, dw_ref, dsc_ref):
        i = pl.program_id(0)
        xp = jnp.where(i > 0, xp_ref[...], 0.0)
        diff = _pool_diff(xc_ref[...], xp, i * BLOCK)
        mixed = _dot(diff, w_ref[...])
        dac = dac_ref[...]
        dan = jnp.where(i < nb - 1, dan_ref[...], 0.0)
        dmix = jnp.concatenate([dac, dan], axis=0) * sc_ref[...]
        ddiff = _dot(dmix, w_ref[...], 1, 1)
        e = ddiff / _pool_count(i * BLOCK, n)
        f2 = e + pltpu.roll(e, n - 1, 0)
        f4 = f2 + pltpu.roll(f2, n - 2, 0)
        f8 = f4 + pltpu.roll(f4, n - 4, 0)
        f16 = f8 + pltpu.roll(f8, n - 8, 0)
        lane = lax.broadcasted_iota(jnp.int32, (n, POOL_W), 1)
        back = _pool_lane_select(lane, f2, f4, f8, f16)
        dx_ref[...] = (back[:BLOCK] - ddiff[:BLOCK]).astype(BF16)
        dw = _dot(diff, dmix[:BLOCK], 0, 0)
        dsc = jnp.sum(dac * mixed, axis=0, keepdims=True)

        @pl.when(i == 0)
        def _():
            dw_ref[...] = dw
            dsc_ref[...] = dsc

        @pl.when(i > 0)
        def _():
            dw_ref[...] += dw
            dsc_ref[...] += dsc

    blk = pl.BlockSpec((BLOCK, POOL_W), lambda i: (i, 0))
    return pl.pallas_call(
        body, name=name, grid=(nb,),
        in_specs=[blk, pl.BlockSpec((HALO_POOL, POOL_W), lambda i: (jnp.maximum(i * per - 1, 0), 0)),
                  blk, pl.BlockSpec((HALO_POOL, POOL_W), lambda i: (jnp.minimum((i + 1) * per, nb * per - 1), 0)),
                  pl.BlockSpec((POOL_W, POOL_W), lambda i: (0, 0)), pl.BlockSpec((1, POOL_W), lambda i: (0, 0))],
        out_specs=[blk, pl.BlockSpec((POOL_W, POOL_W), lambda i: (0, 0)), pl.BlockSpec((1, POOL_W), lambda i: (0, 0))],
        out_shape=[jax.ShapeDtypeStruct((s, POOL_W), BF16), jax.ShapeDtypeStruct((POOL_W, POOL_W), F32),
                   jax.ShapeDtypeStruct((1, POOL_W), F32)],
        compiler_params=_params("arbitrary"),
    )(z, z, da, da, bdw, scale.reshape(1, POOL_W))


def _attn_setup(zc_ref, zp_ref, cc_ref, cp_ref, sc_ref, sp_ref, qn_ref, kn_ref, bd):
    q = []
    for j in range(ATTN_W // LANES):
        t = zc_ref[:, POOL_W + j * LANES:POOL_W + (j + 1) * LANES]
        q.append((t,) + _norm_rope(t, qn_ref[...], cc_ref[...], sc_ref[...], bd))
    kc_raw = zc_ref[:, POOL_W + ATTN_W:POOL_W + ATTN_W + KV_W]
    kc = _norm_rope(kc_raw, kn_ref[...], cc_ref[...], sc_ref[...], bd)[0]
    kp = _norm_rope(zp_ref[:, :KV_W], kn_ref[...], cp_ref[...], sp_ref[...], bd)[0]
    kband = jnp.concatenate([kp, kc], axis=0).astype(BF16)
    vband = jnp.concatenate([zp_ref[:, KV_W:], zc_ref[:, POOL_W + ATTN_W + KV_W:POOL_W + ATTN_W + 2 * KV_W]], axis=0).astype(BF16)
    return q, kband, vband


def _attn_mask(i):
    row = lax.broadcasted_iota(jnp.int32, (Q_PER_KV * BLOCK, 2 * BLOCK), 0) & (BLOCK - 1)
    col = lax.broadcasted_iota(jnp.int32, (Q_PER_KV * BLOCK, 2 * BLOCK), 1)
    dist = row + BLOCK - col
    return (dist >= 0) & (dist < BLOCK) & ((col >= BLOCK) | (i > 0))


def _stack_heads(tiles, kvh):
    return jnp.concatenate([_to_kv_lanes(tiles[h // 2], h) for h in range(kvh * Q_PER_KV, (kvh + 1) * Q_PER_KV)], axis=0)


def _unstack_heads(stacked, kvh, tiles):
    for g in range(Q_PER_KV):
        h = kvh * Q_PER_KV + g
        t = _from_kv_lanes(stacked[g * BLOCK:(g + 1) * BLOCK], h)
        tiles[h // 2] = t if tiles[h // 2] is None else tiles[h // 2] + t


def _sink_column(sink_ref, kvh):
    grp = lax.broadcasted_iota(jnp.int32, (Q_PER_KV * BLOCK, 1), 0) >> 7
    s = [sink_ref[kvh * Q_PER_KV + g] for g in range(Q_PER_KV)]
    return jnp.where(grp == 0, s[0], jnp.where(grp == 1, s[1], jnp.where(grp == 2, s[2], s[3])))


def _to_kv_lanes(t, h):
    kvh = h // Q_PER_KV
    if (h % 2) != kvh:
        t = pltpu.roll(t, HEAD_DIM, 1)
    lane = lax.broadcasted_iota(jnp.int32, t.shape, 1)
    return jnp.where((lane >= HEAD_DIM) == (kvh == 1), t, 0.0)


def _from_kv_lanes(t, h):
    kvh = h // Q_PER_KV
    lane = lax.broadcasted_iota(jnp.int32, t.shape, 1)
    t = jnp.where((lane >= HEAD_DIM) == (kvh == 1), t, 0.0)
    if (h % 2) != kvh:
        t = pltpu.roll(t, HEAD_DIM, 1)
    return t


def _attn_probs(qh, kband, mask, sink):
    sc = _dot(qh, kband, 1, 1) * (HEAD_DIM ** -0.5)
    sc = jnp.where(mask, sc, -1e30)
    m = jnp.maximum(jnp.max(sc, axis=1, keepdims=True), sink)
    p = jnp.exp(sc - m)
    psink = jnp.exp(sink - m)
    den = jnp.sum(p, axis=1, keepdims=True) + psink
    return p / den, psink / den


def _attn_specs(nb):
    cur = lambda i: (i, 0)
    prev = lambda i: (jnp.maximum(i - 1, 0), 0)
    tab = lambda f: pl.BlockSpec((BLOCK, LANES), f)
    vec = pl.BlockSpec((1, LANES), lambda i: (0, 0))
    return [pl.BlockSpec((BLOCK, 1024), cur),
            pl.BlockSpec((BLOCK, 2 * KV_W), lambda i: (jnp.maximum(i - 1, 0), 3)),
            tab(cur), tab(prev), tab(cur), tab(prev), vec, vec,
            pl.BlockSpec(memory_space=pltpu.SMEM)]


def _attn_fwd(z, cosf, sinf, qn, kn, sinks, name):
    s = z.shape[0]
    nb = s // BLOCK

    def body(zc_ref, zp_ref, cc_ref, cp_ref, sc_ref, sp_ref, qn_ref, kn_ref, sink_ref, o_ref):
        i = pl.program_id(0)
        bd = _head_mean_matrix()
        q, kband, vband = _attn_setup(zc_ref, zp_ref, cc_ref, cp_ref, sc_ref, sp_ref, qn_ref, kn_ref, bd)
        mask = _attn_mask(i)
        out = [None] * (ATTN_W // LANES)
        for kvh in range(N_Q_HEADS // Q_PER_KV):
            qs = _stack_heads([t[1] for t in q], kvh)
            probs, _ = _attn_probs(qs, kband, mask, _sink_column(sink_ref, kvh))
            _unstack_heads(_dot(probs, vband), kvh, out)
        for j, o in enumerate(out):
            o_ref[:, j * LANES:(j + 1) * LANES] = o.astype(BF16)

    return pl.pallas_call(
        body, name=name, grid=(nb,), in_specs=_attn_specs(nb),
        out_specs=pl.BlockSpec((BLOCK, ATTN_W), lambda i: (i, 0)),
        out_shape=jax.ShapeDtypeStruct((s, ATTN_W), BF16), compiler_params=_params("parallel"),
    )(z, z, cosf, cosf, sinf, sinf, qn, kn, sinks)


def _attn_bwd(z, cosf, sinf, qn, kn, sinks, d_out, name):
    s = z.shape[0]
    nb = s // BLOCK
    nt = ATTN_W // LANES

    def body(zc_ref, zp_ref, cc_ref, cp_ref, sc_ref, sp_ref, qn_ref, kn_ref, sink_ref, do_ref,
             dq_ref, dkc_ref, dkp_ref, dvc_ref, dvp_ref, dqn_ref, dsink_ref):
        i = pl.program_id(0)
        bd = _head_mean_matrix()
        q, kband, vband = _attn_setup(zc_ref, zp_ref, cc_ref, cp_ref, sc_ref, sp_ref, qn_ref, kn_ref, bd)
        mask = _attn_mask(i)

        @pl.when(i == 0)
        def _():
            dqn_ref[...] = jnp.zeros_like(dqn_ref)
            dsink_ref[...] = jnp.zeros_like(dsink_ref)

        dq = [None] * nt
        dk = jnp.zeros((2 * BLOCK, KV_W), F32)
        dv = jnp.zeros((2 * BLOCK, KV_W), F32)
        d_tiles = [do_ref[:, j * LANES:(j + 1) * LANES] for j in range(nt)]
        for kvh in range(N_Q_HEADS // Q_PER_KV):
            qs = _stack_heads([t[1] for t in q], kvh)
            probs, psink = _attn_probs(qs, kband, mask, _sink_column(sink_ref, kvh))
            dos = _stack_heads(d_tiles, kvh)
            dp = _dot(dos, vband, 1, 1)
            delta = jnp.sum(dp * probs, axis=1, keepdims=True)
            ds = (probs * (dp - delta)) * (HEAD_DIM ** -0.5)
            dsink = -psink * delta
            for g in range(Q_PER_KV):
                h = kvh * Q_PER_KV + g
                dsink_ref[h:h + 1, :] += jnp.broadcast_to(jnp.sum(dsink[g * BLOCK:(g + 1) * BLOCK], axis=0, keepdims=True), (1, LANES))
            _unstack_heads(_dot(ds, kband), kvh, dq)
            dk = dk + _dot(ds, qs, 0, 0)
            dv = dv + _dot(probs, dos, 0, 0)
        dgn = jnp.zeros((1, LANES), F32)
        for j in range(nt):
            t, _, n, r = q[j]
            dt, g = _norm_rope_bwd(dq[j], t, n, r, qn_ref[...], cc_ref[...], sc_ref[...], bd)
            dq_ref[:, j * LANES:(j + 1) * LANES] = dt.astype(BF16)
            dgn = dgn + g
        dqn_ref[...] += jnp.broadcast_to(dgn, (8, LANES))
        dkp_ref[...] = dk[:BLOCK]
        dkc_ref[...] = dk[BLOCK:]
        dvp_ref[...] = dv[:BLOCK]
        dvc_ref[...] = dv[BLOCK:]

        @pl.when(i == nb - 1)
        def _():
            acc = dqn_ref[...]
            dqn_ref[...] = acc + pltpu.roll(acc, HEAD_DIM, 1)

    blk = lambda w: pl.BlockSpec((BLOCK, w), lambda i: (i, 0))
    acc = pl.BlockSpec((8, LANES), lambda i: (0, 0))
    kv = jax.ShapeDtypeStruct((s, KV_W), F32)
    return pl.pallas_call(
        body, name=name, grid=(nb,), in_specs=_attn_specs(nb) + [blk(ATTN_W)],
        out_specs=[blk(ATTN_W), blk(KV_W), blk(KV_W), blk(KV_W), blk(KV_W), acc, acc],
        out_shape=[jax.ShapeDtypeStruct((s, ATTN_W), BF16), kv, kv, kv, kv,
                   jax.ShapeDtypeStruct((8, LANES), F32), jax.ShapeDtypeStruct((8, LANES), F32)],
        compiler_params=_params("arbitrary"),
    )(z, z, cosf, cosf, sinf, sinf, qn, kn, sinks, d_out)


def _kv_post(z, cosf, sinf, kn, dkc, dkp, dvc, dvp, dxp, dq, duv, dz, name):
    s = z.shape[0]
    nb = s // BLOCK

    def body(zk_ref, c_ref, s_ref, kn_ref, dkc_ref, dkp_ref, dvc_ref, dvp_ref, dxp_ref, dq_ref, duv_ref, dz_in,
             dz_ref, dkn_ref):
        j = pl.program_id(0)
        bd = _head_mean_matrix()
        last = j == nb - 1
        d = dkc_ref[...] + jnp.where(last, 0.0, dkp_ref[...])
        t = zk_ref[:, :KV_W]
        _, n, r = _norm_rope(t, kn_ref[...], c_ref[...], s_ref[...], bd)
        dt, g = _norm_rope_bwd(d, t, n, r, kn_ref[...], c_ref[...], s_ref[...], bd)
        dvv = dvc_ref[...] + jnp.where(last, 0.0, dvp_ref[...])
        dz_ref[:, 0:POOL_W] = dxp_ref[...]
        dz_ref[:, POOL_W:POOL_W + ATTN_W] = dq_ref[...]
        dz_ref[:, POOL_W + ATTN_W:POOL_W + ATTN_W + KV_W] = dt.astype(BF16)
        dz_ref[:, POOL_W + ATTN_W + KV_W:POOL_W + ATTN_W + 2 * KV_W] = dvv.astype(BF16)
        dz_ref[:, POOL_W + ATTN_W + 2 * KV_W:GATE_COL0] = duv_ref[...]

        @pl.when(j == 0)
        def _():
            dkn_ref[...] = jnp.zeros_like(dkn_ref)

        dkn_ref[...] += jnp.broadcast_to(g, (8, LANES))

        @pl.when(last)
        def _():
            acc = dkn_ref[...]
            dkn_ref[...] = acc + pltpu.roll(acc, HEAD_DIM, 1)

    cur = lambda w: pl.BlockSpec((BLOCK, w), lambda j: (j, 0))
    nxt = pl.BlockSpec((BLOCK, KV_W), lambda j: (jnp.minimum(j + 1, nb - 1), 0))
    vec = pl.BlockSpec((1, LANES), lambda j: (0, 0))
    return pl.pallas_call(
        body, name=name, grid=(nb,),
        in_specs=[pl.BlockSpec((BLOCK, 2 * KV_W), lambda j: (j, 3)), cur(LANES), cur(LANES), vec,
                  cur(KV_W), nxt, cur(KV_W), nxt, cur(POOL_W), cur(ATTN_W), cur(2 * SGU_W),
                  pl.BlockSpec(memory_space=pl.ANY)],
        out_specs=[pl.BlockSpec((BLOCK, GATE_COL0), lambda j: (j, 0)), pl.BlockSpec((8, LANES), lambda j: (0, 0))],
        out_shape=[jax.ShapeDtypeStruct(dz.shape, dz.dtype), jax.ShapeDtypeStruct((8, LANES), F32)],
        input_output_aliases={11: 0}, compiler_params=_params("arbitrary"),
    )(z, cosf, sinf, kn, dkc, dkp, dvc, dvp, dxp, dq, duv, dz)


def _sgu_setup(z_ref, ws_ref, vn_ref, bd):
    us = z_ref[:, :SGU_W]
    vs = z_ref[:, SGU_W:]
    ug, tu = _gelu(us)
    gv, tv = _gelu(vs)
    rr = jnp.concatenate([lax.rsqrt(_head_mean(gv[:, k * LANES:(k + 1) * LANES] ** 2, bd) + EPS) for k in range(2)], axis=1)
    vg = (gv * rr) * vn_ref[...]
    tril = lax.broadcasted_iota(jnp.int32, (BLOCK, BLOCK), 0) >= lax.broadcasted_iota(jnp.int32, (BLOCK, BLOCK), 1)
    w = [jnp.where(tril, ws_ref[g], 0.0).astype(BF16) for g in range(4)]
    return us, vs, ug, tu, gv, tv, rr, vg, w, tril


def _group_select(parts):
    lane = lax.broadcasted_iota(jnp.int32, parts[0].shape, 1)
    return _pool_lane_select(lane, *parts)


def _sgu_fwd(z, ws, bcol, vn, name):
    s = z.shape[0]
    nb = s // BLOCK

    def body(z_ref, ws_ref, b_ref, vn_ref, c_ref):
        bd = _head_mean_matrix()
        _, _, ug, _, _, _, _, vg, w, _ = _sgu_setup(z_ref, ws_ref, vn_ref, bd)
        sg = _group_select([_dot(w[g], vg) for g in range(4)]) + b_ref[...]
        c_ref[...] = (ug * sg).astype(BF16)

    return pl.pallas_call(
        body, name=name, grid=(nb,),
        in_specs=[pl.BlockSpec((BLOCK, 2 * SGU_W), lambda i: (i, 2)), pl.BlockSpec((4, BLOCK, BLOCK), lambda i: (0, 0, 0)),
                  pl.BlockSpec((BLOCK, SGU_W), lambda i: (0, 0)), pl.BlockSpec((1, SGU_W), lambda i: (0, 0))],
        out_specs=pl.BlockSpec((BLOCK, SGU_W), lambda i: (i, 0)),
        out_shape=jax.ShapeDtypeStruct((s, SGU_W), BF16), compiler_params=_params("parallel"),
    )(z, ws, bcol, vn)


def _sgu_bwd(z, ws, bcol, vn, dc, name):
    s = z.shape[0]
    nb = s // BLOCK

    def body(z_ref, ws_ref, b_ref, vn_ref, dc_ref, duv_ref, dws_ref, db_ref, dvn_ref):
        i = pl.program_id(0)
        bd = _head_mean_matrix()
        us, vs, ug, tu, gv, tv, rr, vg, w, tril = _sgu_setup(z_ref, ws_ref, vn_ref, bd)
        sg = _group_select([_dot(w[g], vg) for g in range(4)]) + b_ref[...]
        dcv = dc_ref[...]
        dug = dcv * sg
        dsg = dcv * ug
        lane = lax.broadcasted_iota(jnp.int32, dsg.shape, 1)

        @pl.when(i == 0)
        def _():
            dws_ref[...] = jnp.zeros_like(dws_ref)
            db_ref[...] = jnp.zeros_like(db_ref)
            dvn_ref[...] = jnp.zeros_like(dvn_ref)

        for g in range(4):
            dsg_g = jnp.where((lane >= g * HEAD_DIM) & (lane < (g + 1) * HEAD_DIM), dsg, 0.0)
            dws_ref[g] += jnp.where(tril, _dot(dsg_g, vg, 1, 1), 0.0)
        dvg = _group_select([_dot(w[g], dsg, 0, 0) for g in range(4)])
        db_ref[...] += dsg
        n = gv * rr
        part = jnp.sum(dvg * n, axis=0, keepdims=True)
        dvn_ref[...] += jnp.broadcast_to(part[:, :LANES] + part[:, LANES:], (8, LANES))
        u = dvg * vn_ref[...]
        tu_ = gv * u
        mean = jnp.concatenate([_head_mean(tu_[:, k * LANES:(k + 1) * LANES], bd) for k in range(2)], axis=1)
        dgv = rr * u - gv * (rr * rr * rr) * mean
        duv_ref[:, :SGU_W] = (dug * _gelu_grad(us, tu)).astype(BF16)
        duv_ref[:, SGU_W:] = (dgv * _gelu_grad(vs, tv)).astype(BF16)

        @pl.when(i == nb - 1)
        def _():
            acc = dvn_ref[...]
            dvn_ref[...] = acc + pltpu.roll(acc, HEAD_DIM, 1)
            for k in range(2):
                db_ref[:, k * LANES:(k + 1) * LANES] = _head_mean(db_ref[:, k * LANES:(k + 1) * LANES], bd) * float(HEAD_DIM)

    return pl.pallas_call(
        body, name=name, grid=(nb,),
        in_specs=[pl.BlockSpec((BLOCK, 2 * SGU_W), lambda i: (i, 2)), pl.BlockSpec((4, BLOCK, BLOCK), lambda i: (0, 0, 0)),
                  pl.BlockSpec((BLOCK, SGU_W), lambda i: (0, 0)), pl.BlockSpec((1, SGU_W), lambda i: (0, 0)),
                  pl.BlockSpec((BLOCK, SGU_W), lambda i: (i, 0))],
        out_specs=[pl.BlockSpec((BLOCK, 2 * SGU_W), lambda i: (i, 0)), pl.BlockSpec((4, BLOCK, BLOCK), lambda i: (0, 0, 0)),
                   pl.BlockSpec((BLOCK, SGU_W), lambda i: (0, 0)), pl.BlockSpec((8, LANES), lambda i: (0, 0))],
        out_shape=[jax.ShapeDtypeStruct((s, 2 * SGU_W), BF16), jax.ShapeDtypeStruct((4, BLOCK, BLOCK), F32),
                   jax.ShapeDtypeStruct((BLOCK, SGU_W), F32), jax.ShapeDtypeStruct((8, LANES), F32)],
        compiler_params=_params("arbitrary"),
    )(z, ws, bcol, vn, dc)


MERGE_TN = 512
MERGE_TM = 512


def _merge_fwd(a, b, c, wpa, wpb, wpc, z, name):
    s = z.shape[0]
    tm = _tile(s, (MERGE_TM, BLOCK))
    gate0 = GATE_COL0 // MERGE_TN

    def body(a_ref, b_ref, c_ref, wa_ref, wb_ref, wc_ref, g0_ref, g1_ref, g2_ref, o_ref):
        r = _sigmoid(g0_ref[...]) * _dot(a_ref[...], wa_ref[...])
        r = r + _sigmoid(g1_ref[...]) * _dot(b_ref[...], wb_ref[...])
        r = r + _sigmoid(g2_ref[...]) * _dot(c_ref[...], wc_ref[...])
        o_ref[...] = r.astype(BF16)

    x_spec = lambda w: pl.BlockSpec((tm, w), lambda i, n: (i, 0))
    w_spec = lambda w: pl.BlockSpec((w, MERGE_TN), lambda i, n: (0, n))
    g_spec = lambda br: pl.BlockSpec((tm, MERGE_TN), lambda i, n: (i, gate0 + 2 * br + n))
    return pl.pallas_call(
        body, name=name, grid=(s // tm, D_MODEL // MERGE_TN),
        in_specs=[x_spec(POOL_W), x_spec(ATTN_W), x_spec(SGU_W), w_spec(POOL_W), w_spec(ATTN_W), w_spec(SGU_W),
                  g_spec(0), g_spec(1), g_spec(2)],
        out_specs=pl.BlockSpec((tm, MERGE_TN), lambda i, n: (i, n)),
        out_shape=jax.ShapeDtypeStruct((s, D_MODEL), BF16), compiler_params=_params("parallel", "parallel"),
    )(a, b, c, wpa, wpb, wpc, z, z, z)


def _branch_bwd(br, xb, wp, z, dm, dz, name):
    s = z.shape[0]
    kb = xb.shape[1]
    tm = _tile(s, (MERGE_TM, BLOCK))
    gate0 = GATE_COL0 // MERGE_TN
    aliased = dz is not None

    def body(*refs):
        x_ref, w_ref, g_ref, dm_ref = refs[:4]
        dz_ref, dy_ref, dx_ref = refs[-3:]
        n = pl.program_id(1)
        y = _dot(x_ref[...], w_ref[...])
        sg = _sigmoid(g_ref[...])
        dmv = dm_ref[...]
        dy = (dmv * sg).astype(BF16)
        dy_ref[...] = dy
        dz_ref[...] = ((dmv * y) * (sg * (1.0 - sg))).astype(BF16)
        dx = _dot(dy, w_ref[...], 1, 1)

        @pl.when(n == 0)
        def _():
            dx_ref[...] = dx

        @pl.when(n > 0)
        def _():
            dx_ref[...] += dx

    in_specs = [pl.BlockSpec((tm, kb), lambda i, n: (i, 0)), pl.BlockSpec((kb, MERGE_TN), lambda i, n: (0, n)),
                pl.BlockSpec((tm, MERGE_TN), lambda i, n: (i, gate0 + 2 * br + n)),
                pl.BlockSpec((tm, MERGE_TN), lambda i, n: (i, n))]
    args = [xb, wp, z, dm]
    if aliased:
        in_specs.append(pl.BlockSpec(memory_space=pl.ANY))
        args.append(dz)
    return pl.pallas_call(
        body, name=name, grid=(s // tm, D_MODEL // MERGE_TN), in_specs=in_specs,
        out_specs=[pl.BlockSpec((tm, MERGE_TN), lambda i, n: (i, gate0 + 2 * br + n)),
                   pl.BlockSpec((tm, MERGE_TN), lambda i, n: (i, n)),
                   pl.BlockSpec((tm, kb), lambda i, n: (i, 0))],
        out_shape=[jax.ShapeDtypeStruct((s, IN_COLS), BF16), jax.ShapeDtypeStruct((s, D_MODEL), BF16),
                   jax.ShapeDtypeStruct((s, kb), F32)],
        input_output_aliases={4: 0} if aliased else {},
        compiler_params=_params("parallel", "arbitrary"),
    )(*args)


FFN_TM = 256
FFN_TC = 2816
FFN_STRIP = 256
FFN_UNROLL = 4


def _conv3(cur, prev, w_ref, b_ref):
    cat = jnp.concatenate([prev, cur], axis=0)
    x1 = pltpu.roll(cat, 1, 0)[HALO_CONV:]
    x2 = pltpu.roll(cat, 2, 0)[HALO_CONV:]
    return w_ref[0:1, :] * x2 + w_ref[1:2, :] * x1 + w_ref[2:3, :] * cur + b_ref[...], x1, x2


def _ffn_specs(s, tm, rows_first):
    per = tm // HALO_CONV
    if rows_first:
        cur = pl.BlockSpec((tm, FFN_TC), lambda i, j: (i, j))
        prev = pl.BlockSpec((HALO_CONV, FFN_TC), lambda i, j: (jnp.maximum(i * per - 1, 0), j))
        w = pl.BlockSpec((3, FFN_TC), lambda i, j: (0, j))
        b = pl.BlockSpec((1, FFN_TC), lambda i, j: (0, j))
    else:
        cur = pl.BlockSpec((tm, FFN_TC), lambda j, i: (i, j))
        prev = pl.BlockSpec((HALO_CONV, FFN_TC), lambda j, i: (jnp.maximum(i * per - 1, 0), j))
        w = pl.BlockSpec((3, FFN_TC), lambda j, i: (0, j))
        b = pl.BlockSpec((1, FFN_TC), lambda j, i: (0, j))
    return cur, prev, w, b


def _ffn_act_fwd(upg, upv, cwg, cwv, cbg, cbv, name):
    s = upg.shape[0]
    tm = _tile(s, (FFN_TM, BLOCK))
    cur, prev, w, b = _ffn_specs(s, tm, True)

    def body(g_ref, gp_ref, v_ref, vp_ref, wg_ref, wv_ref, bg_ref, bv_ref, o_ref):
        first = pl.program_id(0) == 0
        gate = _conv3(g_ref[...], jnp.where(first, 0.0, gp_ref[...]), wg_ref, bg_ref)[0]
        val = _conv3(v_ref[...], jnp.where(first, 0.0, vp_ref[...]), wv_ref, bv_ref)[0]
        o_ref[...] = ((gate * _sigmoid(gate)) * val).astype(BF16)

    return pl.pallas_call(
        body, name=name, grid=(s // tm, D_FF // FFN_TC), in_specs=[cur, prev, cur, prev, w, w, b, b], out_specs=cur,
        out_shape=jax.ShapeDtypeStruct((s, D_FF), BF16), compiler_params=_params("parallel", "parallel"),
    )(upg, upg, upv, upv, cwg, cwv, cbg, cbv)


def _ffn_bwd(upg, upv, cwg, cwv, cbg, cbv, dact, name):
    s = upg.shape[0]
    tm = _tile(s, (FFN_TM, BLOCK))
    per = tm // HALO_CONV
    nrow = s // tm
    n = tm + HALO_CONV
    cur, prev, w, b = _ffn_specs(s, tm, False)
    nxt = pl.BlockSpec((HALO_CONV, FFN_TC), lambda j, i: (jnp.minimum((i + 1) * per, nrow * per - 1), j))

    nch = tm // 8
    rows8 = lambda r: pl.ds(pl.multiple_of(r * 8, 8), 8)

    def body(g_ref, gp_ref, gn_ref, v_ref, vp_ref, vn_ref, wg_ref, wv_ref, bg_ref, bv_ref, da_ref, dan_ref,
             dg_ref, dv_ref, dwg_ref, dwv_ref, og_ref, ov_ref):
        i = pl.program_id(1)
        first = i == 0
        last = i == nrow - 1
        row = lax.broadcasted_iota(jnp.int32, (8, FFN_STRIP), 0)

        keep_down = {k: row >= k for k in (1, 2)}
        keep_up = {k: row < 8 - k for k in (1, 2)}

        def down(cur, prev, k):
            return jnp.where(keep_down[k], pltpu.roll(cur, k, 0), pltpu.roll(prev, k, 0))

        def up(cur, nxt, k):
            return jnp.where(keep_up[k], pltpu.roll(cur, 8 - k, 0), pltpu.roll(nxt, 8 - k, 0))

        @pl.when(first)
        def _():
            dwg_ref[...] = jnp.zeros_like(dwg_ref)
            dwv_ref[...] = jnp.zeros_like(dwv_ref)

        for c in range(FFN_TC // FFN_STRIP):
            cols = slice(c * FFN_STRIP, (c + 1) * FFN_STRIP)
            wg = [functools.partial(lambda k: wg_ref[k:k + 1, cols], k) for k in range(3)]
            wv = [functools.partial(lambda k: wv_ref[k:k + 1, cols], k) for k in range(3)]

            def conv_grads(g_cur, g_prev, v_cur, v_prev, da):
                gate = wg[0]() * down(g_cur, g_prev, 2) + wg[1]() * down(g_cur, g_prev, 1) + wg[2]() * g_cur + bg_ref[:, cols]
                val = wv[0]() * down(v_cur, v_prev, 2) + wv[1]() * down(v_cur, v_prev, 1) + wv[2]() * v_cur + bv_ref[:, cols]
                sg = _sigmoid(gate)
                return (da * val) * (sg * (1.0 + gate * (1.0 - sg))), da * (gate * sg)

            def passes(q, carry):
                for u in range(FFN_UNROLL):
                    carry = one_pass(q * FFN_UNROLL + u, carry)
                return carry

            def one_pass(r, carry, tile_end=False):
                dg_cur, dv_cur, acc = carry
                g_r, v_r = g_ref[rows8(r), cols], v_ref[rows8(r), cols]
                if tile_end:
                    g_n, v_n, da_n = gn_ref[:, cols], vn_ref[:, cols], jnp.where(last, 0.0, dan_ref[:, cols])
                else:
                    g_n, v_n, da_n = g_ref[rows8(r + 1), cols], v_ref[rows8(r + 1), cols], da_ref[rows8(r + 1), cols]
                dg_n, dv_n = conv_grads(g_n, g_r, v_n, v_r, da_n)
                new_acc = []
                for o_ref, w, d_cur, d_n, x0, a in ((og_ref, wg, dg_cur, dg_n, g_r, acc[:4]), (ov_ref, wv, dv_cur, dv_n, v_r, acc[4:])):
                    d1, d2 = up(d_cur, d_n, 1), up(d_cur, d_n, 2)
                    o_ref[rows8(r), cols] = w[2]() * d_cur + w[1]() * d1 + w[0]() * d2
                    new_acc += [a[0] + d2 * x0, a[1] + d1 * x0, a[2] + d_cur * x0, a[3] + d_cur]
                return dg_n, dv_n, tuple(new_acc)

            g_p = jnp.where(first, 0.0, gp_ref[:, cols])
            v_p = jnp.where(first, 0.0, vp_ref[:, cols])
            dg0, dv0 = conv_grads(g_ref[0:8, cols], g_p, v_ref[0:8, cols], v_p, da_ref[0:8, cols])
            zero = jnp.zeros((8, FFN_STRIP), F32)
            carry = lax.fori_loop(0, nch // FFN_UNROLL - 1, passes, (dg0, dv0, (zero,) * 8))
            for r in range(nch - FFN_UNROLL, nch):
                carry = one_pass(r, carry, tile_end=r == nch - 1)
            for ref, a in ((dwg_ref, carry[2][:4]), (dwv_ref, carry[2][4:])):
                for k in range(4):
                    ref[k:k + 1, cols] += jnp.sum(a[k], axis=0, keepdims=True)
        dg_ref[...] = og_ref[...].astype(BF16)
        dv_ref[...] = ov_ref[...].astype(BF16)

    acc = pl.BlockSpec((8, FFN_TC), lambda j, i: (0, j))
    full = jax.ShapeDtypeStruct((s, D_FF), BF16)
    accs = jax.ShapeDtypeStruct((8, D_FF), F32)
    return pl.pallas_call(
        body, name=name, grid=(D_FF // FFN_TC, nrow), in_specs=[cur, prev, nxt, cur, prev, nxt, w, w, b, b, cur, nxt],
        out_specs=[cur, cur, acc, acc], out_shape=[full, full, accs, accs],
        scratch_shapes=[pltpu.VMEM((tm, FFN_TC), F32), pltpu.VMEM((tm, FFN_TC), F32)],
        compiler_params=_params("parallel", "arbitrary"),
    )(upg, upg, upg, upv, upv, upv, cwg, cwv, cbg, cbv, dact, dact)


def _mesh_place():
    return lax.axis_index("x"), lax.axis_index("y"), lax.axis_index("c")


def _all_gather(shards, name):
    na = len(shards)

    def body(*refs):
        x_refs, out_refs = refs[:na], refs[na:2 * na]
        send_sems, recv_sems, local_sems = refs[2 * na:]
        x, y, cc = _mesh_place()
        me, sibling = (x, y, cc), (x, y, 1 - cc)
        chips = [(1 - x, y), (x, 1 - y), (1 - x, 1 - y)]

        def copy(k, a, block, to, from_input=False):
            slot = out_refs[a].at[4 * block[0] + 2 * block[1] + block[2]]
            return pltpu.make_async_remote_copy(
                src_ref=x_refs[a] if from_input else slot, dst_ref=slot, send_sem=send_sems.at[k * na + a],
                recv_sem=recv_sems.at[k * na + a], device_id=to, device_id_type=pl.DeviceIdType.MESH)

        mine = [pltpu.make_async_copy(x_refs[a], out_refs[a].at[4 * x + 2 * y + cc], local_sems.at[a]) for a in range(na)]
        for cp in mine:
            cp.start()
        first = [copy(0, a, me, sibling, True) for a in range(na)]
        first += [copy(1 + j, a, me, (*chip, cc), True) for j, chip in enumerate(chips) for a in range(na)]
        for cp in first:
            cp.start()
        passed = []
        for j, chip in enumerate(chips):
            for a in range(na):
                copy(1 + j, a, (*chip, cc), me).wait_recv()
                passed.append(copy(4 + j, a, (*chip, cc), sibling))
                passed[-1].start()
        for a in range(na):
            copy(0, a, sibling, me).wait_recv()
        for j, chip in enumerate(chips):
            for a in range(na):
                copy(4 + j, a, (*chip, 1 - cc), me).wait_recv()
        for cp in first + passed:
            cp.wait_send()
        for cp in mine:
            cp.wait()

    hbm = pl.BlockSpec(memory_space=pl.ANY)
    return pl.pallas_call(
        body, name=name, out_shape=[jax.ShapeDtypeStruct((N_DEV,) + t.shape, t.dtype) for t in shards],
        in_specs=[hbm] * na, out_specs=[hbm] * na,
        scratch_shapes=[pltpu.SemaphoreType.DMA((7 * na,)), pltpu.SemaphoreType.DMA((7 * na,)), pltpu.SemaphoreType.DMA((na,))],
    )(*shards)


def _exchange(blocks, name):
    na = len(blocks)

    def body(*refs):
        g_refs, out_refs = refs[:na], refs[na:2 * na]
        send_sems, recv_sems, local_sems = refs[2 * na:]
        x, y, cc = _mesh_place()
        me = 4 * x + 2 * y + cc
        mine = [pltpu.make_async_copy(g_refs[a].at[me], out_refs[a].at[me], local_sems.at[a]) for a in range(na)]
        for cp in mine:
            cp.start()
        sends, lands = [], []
        for k in range(1, N_DEV):
            px = 1 - x if (k >> 2) & 1 else x
            py = 1 - y if (k >> 1) & 1 else y
            pc = 1 - cc if k & 1 else cc
            peer = 4 * px + 2 * py + pc
            for a in range(na):
                sem = (k - 1) * na + a
                sends.append(pltpu.make_async_remote_copy(
                    src_ref=g_refs[a].at[peer], dst_ref=out_refs[a].at[me], send_sem=send_sems.at[sem], recv_sem=recv_sems.at[sem],
                    device_id=(px, py, pc), device_id_type=pl.DeviceIdType.MESH))
                lands.append(pltpu.make_async_remote_copy(
                    src_ref=g_refs[a].at[peer], dst_ref=out_refs[a].at[peer], send_sem=send_sems.at[sem], recv_sem=recv_sems.at[sem],
                    device_id=(px, py, pc), device_id_type=pl.DeviceIdType.MESH))
        for cp in sends:
            cp.start()
        for cp in lands:
            cp.wait_recv()
        for cp in sends:
            cp.wait_send()
        for cp in mine:
            cp.wait()

    hbm = pl.BlockSpec(memory_space=pl.ANY)
    return pl.pallas_call(
        body, name=name, out_shape=[jax.ShapeDtypeStruct(t.shape, t.dtype) for t in blocks],
        in_specs=[hbm] * na, out_specs=[hbm] * na,
        scratch_shapes=[pltpu.SemaphoreType.DMA((7 * na,)), pltpu.SemaphoreType.DMA((7 * na,)), pltpu.SemaphoreType.DMA((na,))],
    )(*blocks)


def _peer(k):
    x, y, cc = _mesh_place()
    px = 1 - x if (k >> 2) & 1 else x
    py = 1 - y if (k >> 1) & 1 else y
    pc = 1 - cc if k & 1 else cc
    return (px, py, pc), 4 * px + 2 * py + pc


def _push_copy(src_ref, land_ref, k, a, na, send_sems, recv_sems, indexed, landed):
    x, y, cc = _mesh_place()
    place, peer = _peer(k)
    sem = (k - 1) * na + a
    return pltpu.make_async_remote_copy(
        src_ref=src_ref.at[peer] if indexed else src_ref, dst_ref=land_ref.at[peer if landed else 4 * x + 2 * y + cc],
        send_sem=send_sems.at[sem], recv_sem=recv_sems.at[sem], device_id=place, device_id_type=pl.DeviceIdType.MESH)


_HBM = pl.BlockSpec(memory_space=pltpu.HBM)
_SEM = pl.BlockSpec(memory_space=pltpu.SEMAPHORE)
_EFFECT = pltpu.SideEffectType.DATAFLOW_SIDE_EFFECTING


def _push_start(srcs, indexed, name):
    na = len(srcs)
    lands = [lax.empty(t.shape if indexed else (N_DEV,) + t.shape, t.dtype) for t in srcs]

    def body(*refs):
        src_refs, land_refs = refs[:na], refs[na:2 * na]
        send_sems, recv_sems = refs[2 * na], refs[2 * na + 1]
        token = refs[-1]
        for k in range(1, N_DEV):
            for a in range(na):
                _push_copy(src_refs[a], land_refs[a], k, a, na, send_sems, recv_sems, indexed, False).start()
        token[...] = jnp.zeros_like(token)

    sems = pltpu.SemaphoreType.DMA((7 * na,))
    out = pl.pallas_call(
        body, name=name,
        out_shape=(sems, sems, *[pltpu.HBM(t.shape, t.dtype) for t in srcs], *[pltpu.HBM(t.shape, t.dtype) for t in lands],
                   jax.ShapeDtypeStruct((8, LANES), F32)),
        in_specs=[_HBM] * (2 * na), out_specs=(_SEM, _SEM, *[_HBM] * (2 * na), pl.BlockSpec(memory_space=pltpu.VMEM)),
        input_output_aliases={i: 2 + i for i in range(2 * na)},
        compiler_params=pltpu.CompilerParams(has_side_effects=_EFFECT),
    )(*[pltpu.with_memory_space_constraint(t, pltpu.HBM) for t in srcs + lands])
    return out[0], out[1], list(out[2:2 + na]), list(out[2 + na:2 + 2 * na]), out[-1]


def _push_wait(started, indexed, after, name):
    send_sems, recv_sems, srcs, lands, _ = started
    na = len(srcs)

    def body(*refs):
        src_refs, land_refs = refs[:na], refs[na:2 * na]
        send_sems, recv_sems = refs[2 * na], refs[2 * na + 1]
        for k in range(1, N_DEV):
            for a in range(na):
                copy = _push_copy(src_refs[a], land_refs[a], k, a, na, send_sems, recv_sems, indexed, True)
                copy.wait_send()
                copy.wait_recv()

    out = pl.pallas_call(
        body, name=name, out_shape=[pltpu.HBM(t.shape, t.dtype) for t in srcs + lands],
        in_specs=[_HBM] * (2 * na) + [_SEM, _SEM, pl.BlockSpec(memory_space=pl.ANY)], out_specs=[_HBM] * (2 * na),
        input_output_aliases={i: i for i in range(2 * na)},
        compiler_params=pltpu.CompilerParams(has_side_effects=_EFFECT),
    )(*srcs, *lands, send_sems, recv_sems, after)
    x, y, cc = _mesh_place()
    me = 4 * x + 2 * y + cc
    return [lax.dynamic_update_index_in_dim(
        land, lax.dynamic_index_in_dim(src, me, 0, keepdims=False) if indexed else src, me, 0)
        for src, land in zip(out[:na], out[na:])]


def _adamw_sum(parts, w, m, v, name):
    _, r, c = parts.shape
    tr = _tile(r, (256, 128, 64, 32, 16, 8))

    def body(p_ref, w_ref, m_ref, v_ref, g_ref, d_ref, nm_ref, nv_ref):
        _adam_store(_sum_parts(p_ref), w_ref, m_ref, v_ref, g_ref, d_ref, nm_ref, nv_ref)

    row = pl.BlockSpec((tr, c), lambda i: (i, 0))
    shp = jax.ShapeDtypeStruct((r, c), F32)
    return pl.pallas_call(
        body, name=name, grid=(r // tr,), in_specs=[pl.BlockSpec((N_DEV, tr, c), lambda i: (0, i, 0)), row, row, row],
        out_specs=[row, row, row, row], out_shape=[shp, shp, shp, shp], compiler_params=_params("parallel"),
    )(parts, w, m, v)


def _sum_parts(p_ref):
    g = p_ref[0].astype(F32)
    for k in range(1, N_DEV):
        g = g + p_ref[k].astype(F32)
    return g


def _adam_store(g, w_ref, m_ref, v_ref, g_ref, d_ref, nm_ref, nv_ref):
    nm = ADAM_B1 * m_ref[...] + (1.0 - ADAM_B1) * g
    nv = ADAM_B2 * v_ref[...] + (1.0 - ADAM_B2) * (g * g)
    m_hat = nm / (1.0 - ADAM_B1 ** ADAM_STEP)
    v_hat = nv / (1.0 - ADAM_B2 ** ADAM_STEP)
    g_ref[...] = g
    nm_ref[...] = nm
    nv_ref[...] = nv
    d_ref[...] = -ADAM_LR * (m_hat / (jnp.sqrt(v_hat) + ADAM_EPS) + ADAM_WD * w_ref[...])


def _adamw_weight(parts, w, m, v, name):
    _, r, c = w.shape
    tr = _tile(r, (256, 128, 176))
    nr = r // tr

    def body(p0_ref, p1_ref, w_ref, m_ref, v_ref, g_ref, d_ref, nm_ref, nv_ref):
        g = jnp.where(pl.program_id(0) == 0, _sum_parts(p0_ref), _sum_parts(p1_ref))
        _adam_store(g, w_ref, m_ref, v_ref, g_ref, d_ref, nm_ref, nv_ref)

    part = lambda layer: pl.BlockSpec(
        (N_DEV, tr, c), lambda l, i: (0, jnp.where(l == layer, i, (nr - 1) * (1 - layer)), 0))
    row = pl.BlockSpec((None, tr, c), lambda l, i: (l, i, 0))
    shp = jax.ShapeDtypeStruct(w.shape, F32)
    return pl.pallas_call(
        body, name=name, grid=(DEPTH, nr), in_specs=[part(0), part(1), row, row, row],
        out_specs=[row, row, row, row], out_shape=[shp, shp, shp, shp], compiler_params=_params("arbitrary", "arbitrary"),
    )(parts[0], parts[1], w, m, v)


def _full_to_slots(name, t):
    k, n = t.shape
    if name in ROW_SHARDED:
        return t.reshape(N_DEV, k // N_DEV, n)
    return t.reshape(k, N_DEV, n // N_DEV).transpose(1, 0, 2)


def _slots_to_full(name, t):
    _, r, c = t.shape
    if name in ROW_SHARDED:
        return t.reshape(N_DEV * r, c)
    return t.transpose(1, 0, 2).reshape(r, N_DEV * c)


def _small_sizes(shapes):
    return [(n, shapes[n], -(-int(math.prod(shapes[n])) // (8 * LANES)) * 8) for n in SMALL]


def _pack_small(tree, shapes):
    rows = []
    for n, shp, nrow in _small_sizes(shapes):
        flat = tree[n].reshape(-1)
        rows.append(jnp.pad(flat, (0, nrow * LANES - flat.shape[0])).reshape(nrow, LANES))
    total = sum(r.shape[0] for r in rows)
    rows.append(jnp.zeros((-total % SMALL_ROW_TILE, LANES), F32))
    return jnp.concatenate(rows, axis=0)


def _unpack_small(buf, shapes):
    out, r0 = {}, 0
    for n, shp, nrow in _small_sizes(shapes):
        out[n] = buf[r0:r0 + nrow].reshape(-1)[:int(math.prod(shp))].reshape(shp)
        r0 += nrow
    return out


def _block_diag(w):
    g = w.shape[0]
    eye = jnp.eye(g, dtype=w.dtype)
    return (eye[:, None, :, None] * w[:, :, None, :]).reshape(g * HEAD_DIM, g * HEAD_DIM)


def kernel(x, positions, norm1, w_in, q_norm, k_norm, sinks, w_pool, pool_scale, sgu_v_norm, w_s, b_s, w_proj_a, w_proj_b, w_proj_c, w_out, norm2, w_up, conv_w, conv_b, w_down, loss_target, m_norm1, m_w_in, m_q_norm, m_k_norm, m_sinks, m_w_pool, m_pool_scale, m_sgu_v_norm, m_w_s, m_b_s, m_w_proj_a, m_w_proj_b, m_w_proj_c, m_w_out, m_norm2, m_w_up, m_conv_w, m_conv_b, m_w_down, v_norm1, v_w_in, v_q_norm, v_k_norm, v_sinks, v_w_pool, v_pool_scale, v_sgu_v_norm, v_w_s, v_b_s, v_w_proj_a, v_w_proj_b, v_w_proj_c, v_w_out, v_norm2, v_w_up, v_conv_w, v_conv_b, v_w_down):
    names = ("norm1", "w_in", "q_norm", "k_norm", "sinks", "w_pool", "pool_scale", "sgu_v_norm", "w_s", "b_s", "w_proj_a",
             "w_proj_b", "w_proj_c", "w_out", "norm2", "w_up", "conv_w", "conv_b", "w_down")
    wts = dict(zip(names, (norm1, w_in, q_norm, k_norm, sinks, w_pool, pool_scale, sgu_v_norm, w_s, b_s, w_proj_a, w_proj_b,
                           w_proj_c, w_out, norm2, w_up, conv_w, conv_b, w_down)))
    mom = dict(zip(names, (m_norm1, m_w_in, m_q_norm, m_k_norm, m_sinks, m_w_pool, m_pool_scale, m_sgu_v_norm, m_w_s, m_b_s,
                           m_w_proj_a, m_w_proj_b, m_w_proj_c, m_w_out, m_norm2, m_w_up, m_conv_w, m_conv_b, m_w_down)))
    var = dict(zip(names, (v_norm1, v_w_in, v_q_norm, v_k_norm, v_sinks, v_w_pool, v_pool_scale, v_sgu_v_norm, v_w_s, v_b_s,
                           v_w_proj_a, v_w_proj_b, v_w_proj_c, v_w_out, v_norm2, v_w_up, v_conv_w, v_conv_b, v_w_down)))
    xs = x[0]
    target = loss_target[0]
    s = xs.shape[0]

    inv_freq = ROPE_THETA ** (-jnp.arange(0, HEAD_DIM, 2, dtype=F32) / HEAD_DIM)
    ang = positions[0].astype(F32)[:, None] * inv_freq
    cosf = jnp.tile(jnp.cos(ang), (1, 4))
    sinf = jnp.tile(jnp.concatenate([-jnp.sin(ang), jnp.sin(ang)], axis=1), (1, 2))

    local = [{n: wts[n][l] if n == "conv_w" else wts[n][l].astype(BF16) for n in SHARDED} for l in range(DEPTH)]
    later = SHARDED[1:]
    full = [{"w_in": _slots_to_full("w_in", _all_gather([local[0]["w_in"]], "gather_w_in_0")[0])}, None]
    gather0 = _push_start([local[0][n] for n in later], False, "gather_rest_0_start")
    norm1_first = norm1[0] + gather0[4][0, 0]

    def layer_consts(l):
        return dict(
            bdw=_block_diag(w_pool[l]).astype(BF16), qn=jnp.tile(q_norm[l], 2).reshape(1, LANES),
            kn=jnp.tile(k_norm[l], 2).reshape(1, LANES), vn=jnp.tile(sgu_v_norm[l], 4).reshape(1, SGU_W),
            bcol=jnp.repeat(b_s[l].T, HEAD_DIM, axis=1),
            cbg=conv_b[l][:D_FF].reshape(1, D_FF), cbv=conv_b[l][D_FF:].reshape(1, D_FF))

    gate_cols, val_cols = (0, D_FF), (D_FF, D_FF)

    saved = []
    cur = xs
    for l in range(DEPTH):
        if l == 1:
            landed = _push_wait(gather1, False, cur, "gather_weights_1_wait")
            full[1] = {n: _slots_to_full(n, t) for n, t in zip(SHARDED, landed)}
        fw, k = full[l], layer_consts(l)
        h1 = _rms_fwd(cur, norm1_first if l == 0 else norm1[l], f"rms1_fwd_{l}")
        z = _mm(h1, fw["w_in"], name=f"in_proj_{l}")
        a = _pool_fwd(z, k["bdw"], pool_scale[l], f"pool_fwd_{l}")
        b = _attn_fwd(z, cosf, sinf, k["qn"], k["kn"], sinks[l], f"attn_fwd_{l}")
        c = _sgu_fwd(z, w_s[l], k["bcol"], k["vn"], f"sgu_fwd_{l}")
        w_proj_a_l = fw.get("w_proj_a")
        if l == 0:
            landed = _push_wait(gather0, False, c, "gather_rest_0_wait")
            fw.update({n: _slots_to_full(n, t) for n, t in zip(later, landed)})
            gather1 = _push_start([local[1][n] for n in SHARDED], False, "gather_weights_1_start")
            w_proj_a_l = fw["w_proj_a"] + gather1[4][0, 0].astype(BF16)
        merged = _merge_fwd(a, b, c, w_proj_a_l, fw["w_proj_b"], fw["w_proj_c"], z, f"merge_fwd_{l}")
        x1 = _mm(merged, fw["w_out"], add=cur, name=f"out_proj_{l}")
        h2 = _rms_fwd(x1, norm2[l], f"rms2_fwd_{l}")
        upg = _mm(h2, fw["w_up"], b_n=gate_cols, name=f"up_gate_{l}")
        upv = _mm(h2, fw["w_up"], b_n=val_cols, name=f"up_val_{l}")
        k["cwg"], k["cwv"] = fw["conv_w"][:, :D_FF], fw["conv_w"][:, D_FF:]
        act = _ffn_act_fwd(upg, upv, k["cwg"], k["cwv"], k["cbg"], k["cbv"], f"ffn_act_fwd_{l}")
        x2 = _mm(act, fw["w_down"], add=x1, name=f"down_proj_{l}")
        saved.append(dict(x0=cur, h1=h1, z=z, a=a, b=b, c=c, merged=merged, x1=x1, h2=h2, upg=upg, upv=upv, act=act))
        cur = x2

    dcur, loss_tile = _loss_head(cur, target)
    loss = lax.psum(loss_tile[0, 0], ("x", "y", "c"))

    gsmall = [None] * DEPTH

    def slots_of(grads):
        return [_full_to_slots(n, t) for n, t in grads.items()]

    for l in reversed(range(DEPTH)):
        fw, k, sv = full[l], layer_consts(l), saved[l]
        k["cwg"], k["cwv"] = fw["conv_w"][:, :D_FF], fw["conv_w"][:, D_FF:]
        staged = l == 0
        wgrad = functools.partial(_mm, ta=True, out_dtype=BF16)
        w_down_l = fw["w_down"] + exchange1[4][0, 0].astype(BF16) if staged else fw["w_down"]
        dact = _mm(dcur, w_down_l, tb=True, name=f"down_proj_bwd_{l}")
        g_down = wgrad(sv["act"], dcur, name=f"down_proj_wgrad_{l}")
        dg0, dv0, dcg, dcv = _ffn_bwd(sv["upg"], sv["upv"], k["cwg"], k["cwv"], k["cbg"], k["cbv"], dact, f"ffn_bwd_{l}")
        dh2 = _mm(dg0, fw["w_up"], tb=True, b_k=gate_cols, name=f"up_gate_bwd_{l}")
        dh2 = _mm(dv0, fw["w_up"], tb=True, b_k=val_cols, add=dh2, name=f"up_val_bwd_{l}")
        g_up = wgrad(sv["h2"], dg0, out_cols=(0, 2 * D_FF), name=f"up_gate_wgrad_{l}")
        g_up = wgrad(sv["h2"], dv0, out_cols=(D_FF, 2 * D_FF), out_into=g_up, name=f"up_val_wgrad_{l}")
        g_ffn = dict(w_up=g_up, w_down=g_down, conv_w=jnp.concatenate([dcg[0:3], dcv[0:3]], axis=1))
        norm2_l = norm2[l]
        if staged:
            parts1 = dict(zip(SHARDED, _push_wait(exchange1, True, g_up, "exchange_grads_1_wait")))
            exchange_ffn = _push_start(slots_of(g_ffn), True, "exchange_ffn_0_start")
            norm2_l = norm2_l + exchange_ffn[4][0, 0]
        dx1, g_norm2 = _rms_bwd(sv["x1"], norm2_l, dh2, dcur, f"rms2_bwd_{l}")
        dmerged = _mm(dx1, fw["w_out"], tb=True, name=f"out_proj_bwd_{l}")
        g_out = wgrad(sv["merged"], dx1, name=f"out_proj_wgrad_{l}")
        dz, dya, da = _branch_bwd(0, sv["a"], fw["w_proj_a"], sv["z"], dmerged, None, f"branch_a_bwd_{l}")
        dz, dyb, db = _branch_bwd(1, sv["b"], fw["w_proj_b"], sv["z"], dmerged, dz, f"branch_b_bwd_{l}")
        dz, dyc, dc = _branch_bwd(2, sv["c"], fw["w_proj_c"], sv["z"], dmerged, dz, f"branch_c_bwd_{l}")
        g_mix = dict(w_proj_a=wgrad(sv["a"], dya, name=f"proj_a_wgrad_{l}"), w_proj_b=wgrad(sv["b"], dyb, name=f"proj_b_wgrad_{l}"),
                     w_proj_c=wgrad(sv["c"], dyc, name=f"proj_c_wgrad_{l}"), w_out=g_out)
        pool_scale_l = pool_scale[l]
        if staged:
            exchange_mix = _push_start(slots_of(g_mix), True, "exchange_mixer_0_start")
            pool_scale_l = pool_scale_l + exchange_mix[4][0, 0]
        dxp, g_bdw, g_pscale = _pool_bwd(sv["z"], da, k["bdw"], pool_scale_l, f"pool_bwd_{l}")
        dq, dkc, dkp, dvc, dvp, g_qn, g_sink = _attn_bwd(sv["z"], cosf, sinf, k["qn"], k["kn"], sinks[l], db, f"attn_bwd_{l}")
        duv, g_ws, g_bacc, g_vn = _sgu_bwd(sv["z"], w_s[l], k["bcol"], k["vn"], dc, f"sgu_bwd_{l}")
        dz, g_kn = _kv_post(sv["z"], cosf, sinf, k["kn"], dkc, dkp, dvc, dvp, dxp, dq, duv, dz, f"kv_post_{l}")
        g_in = dict(w_in=wgrad(sv["h1"], dz, name=f"in_proj_wgrad_{l}"))
        norm1_l = norm1[l]
        if staged:
            exchange_in = _push_start(slots_of(g_in), True, "exchange_w_in_0_start")
            norm1_l = norm1_l + exchange_in[4][0, 0]
        dh1 = _mm(dz, fw["w_in"], tb=True, name=f"in_proj_bwd_{l}")
        dcur, g_norm1 = _rms_bwd(sv["x0"], norm1_l, dh1, dx1, f"rms1_bwd_{l}")
        if not staged:
            exchange1 = _push_start(slots_of({n: {**g_in, **g_mix, **g_ffn}[n] for n in SHARDED}), True, "exchange_grads_1_start")
        gsmall[l] = dict(
            norm1=g_norm1[0], q_norm=g_qn[0, :HEAD_DIM], k_norm=g_kn[0, :HEAD_DIM], sinks=g_sink[:, 0],
            w_pool=jnp.stack([g_bdw[g * HEAD_DIM:(g + 1) * HEAD_DIM, g * HEAD_DIM:(g + 1) * HEAD_DIM] for g in range(4)]),
            pool_scale=g_pscale[0], sgu_v_norm=g_vn[0, :HEAD_DIM], w_s=g_ws, b_s=g_bacc[:, ::HEAD_DIM].T,
            norm2=g_norm2[0], conv_b=jnp.concatenate([dcg[3], dcv[3]]))
    grad_x = dcur[None]

    shapes = {n: wts[n].shape for n in SMALL}
    gs = _pack_small({n: jnp.stack([gsmall[l][n] for l in range(DEPTH)]) for n in SMALL}, shapes)
    rows = gs.shape[0] // N_DEV
    mx, my, mc = _mesh_place()
    eighth = lambda t: lax.dynamic_slice_in_dim(t, (4 * mx + 2 * my + mc) * rows, rows, 0)
    gs_parts = _exchange([gs.reshape(N_DEV, rows, LANES)], "exchange_small_grads")[0]
    g_s = _adamw_sum(gs_parts, eighth(_pack_small(wts, shapes)), eighth(_pack_small(mom, shapes)), eighth(_pack_small(var, shapes)),
                     "adamw_replicated")
    small = [_unpack_small(t.reshape(N_DEV * rows, LANES), shapes) for t in _all_gather(list(g_s), "gather_small_updates")]
    g_s = g_s[0]

    parts0 = dict(zip(g_ffn, _push_wait(exchange_ffn, True, g_s, "exchange_ffn_0_wait")))
    parts0.update(zip(g_mix, _push_wait(exchange_mix, True, g_s, "exchange_mixer_0_wait")))
    update = lambda n: _adamw_weight([parts0[n], parts1[n]], wts[n], mom[n], var[n], f"adamw_{n}")
    big = {n: update(n) for n in SHARDED[1:]}
    parts0.update(zip(g_in, _push_wait(exchange_in, True, big["w_up"][0], "exchange_w_in_0_wait")))
    big["w_in"] = update("w_in")

    outs = [loss, grad_x]
    for kind in range(4):
        outs += [small[kind][n] if n in SMALL else big[n][kind] for n in names]
    return tuple(outs)
```

```python
import functools
import math

import jax
import jax.numpy as jnp
from jax import lax
from jax.experimental import pallas as pl
from jax.experimental.pallas import tpu as pltpu

F32 = jnp.float32
BF16 = jnp.bfloat16

D_MODEL = 1024
DEPTH = 2
HEAD_DIM = 64
N_Q_HEADS = 8
Q_PER_KV = 4
BLOCK = 128
POOL_W = 256
ATTN_W = 512
KV_W = 128
SGU_W = 256
IN_COLS = 4608
GATE_COL0 = 1536
D_FF = 2816
EPS = 1e-6
ROPE_THETA = 10000.0
N_DEV = 8
LANES = 128
HALO_POOL = 16
HALO_CONV = 8

ADAM_LR = 0.001
ADAM_B1 = 0.9
ADAM_B2 = 0.999
ADAM_EPS = 1e-08
ADAM_WD = 0.01
ADAM_STEP = 10

VMEM_LIMIT = 48 * 1024 * 1024
MM_VMEM_BUDGET = 40 * 1024 * 1024

SHARDED = ("w_in", "w_proj_a", "w_proj_b", "w_proj_c", "w_out", "w_up", "w_down", "conv_w")
ROW_SHARDED = ("w_out", "w_down")
SMALL_ROW_TILE = 256
SMALL = ("norm1", "q_norm", "k_norm", "sinks", "w_pool", "pool_scale", "sgu_v_norm", "w_s", "b_s", "norm2", "conv_b")
SMALL_LATE = ("norm1", "k_norm")
SMALL_EARLY = tuple(n for n in SMALL if n not in SMALL_LATE)

_GELU_C = math.sqrt(2.0 / math.pi)
_GELU_A = 0.044715


def _params(*sem):
    return pltpu.CompilerParams(dimension_semantics=sem, vmem_limit_bytes=VMEM_LIMIT)


def _tile(n, prefs):
    for t in prefs:
        if t <= n and n % t == 0:
            return t
    return n


def _head_mean_matrix():
    r = lax.broadcasted_iota(jnp.int32, (LANES, LANES), 0)
    c = lax.broadcasted_iota(jnp.int32, (LANES, LANES), 1)
    return jnp.where((r >= HEAD_DIM) == (c >= HEAD_DIM), 1.0 / HEAD_DIM, 0.0).astype(BF16)


def _head_mean(v, bd):
    hi = v.astype(BF16)
    rest = v - hi.astype(F32)
    mid = rest.astype(BF16)
    lo = (rest - mid.astype(F32)).astype(BF16)
    mm = lambda p: jnp.dot(p, bd, preferred_element_type=F32)
    return mm(hi) + (mm(mid) + mm(lo))


def _rot_half(t):
    lane = lax.broadcasted_iota(jnp.int32, t.shape, 1)
    return jnp.where((lane & 32) == 0, pltpu.roll(t, LANES - 32, 1), pltpu.roll(t, 32, 1))


def _norm_rope(t, gn, cosf, sinf, bd):
    r = lax.rsqrt(_head_mean(t * t, bd) + EPS)
    n = t * r
    y = n * gn
    return y * cosf + _rot_half(y) * sinf, n, r


def _norm_rope_bwd(d, t, n, r, gn, cosf, sinf, bd):
    dy = d * cosf + _rot_half(d * sinf)
    dgn = jnp.sum(dy * n, axis=0, keepdims=True)
    u = dy * gn
    dt = r * u - t * (r * r * r) * _head_mean(t * u, bd)
    return dt, dgn


def _gelu(x):
    t = jnp.tanh(_GELU_C * (x + _GELU_A * (x * x * x)))
    return 0.5 * x * (1.0 + t), t


def _gelu_grad(x, t):
    return 0.5 * (1.0 + t) + 0.5 * x * (1.0 - t * t) * (_GELU_C * (1.0 + 3.0 * _GELU_A * x * x))


def _sigmoid(x):
    return jax.nn.sigmoid(x)


def _dot(a, b, ca=1, cb=0):
    return lax.dot_general(a.astype(BF16), b.astype(BF16), (((ca,), (cb,)), ((), ())), preferred_element_type=F32)


def _mm(a, b, *, ta=False, tb=False, add=None, out_dtype=F32, name, b_n=None, b_k=None, out_cols=None, out_into=None):
    m = a.shape[1] if ta else a.shape[0]
    k = a.shape[0] if ta else a.shape[1]
    n = b_n[1] if b_n else (b.shape[0] if tb else b.shape[1])
    tn = _tile(n, (1024, 1152, 1408, 512, 256, 128))
    has_add = add is not None
    fits = []
    for tm in (2048, 1024, 1408, 512, 256, 128):
        for tk in (k, 4608, 2816, 2048, 1408, 1152, 1024, 512, 256, 128):
            if tm <= m and m % tm == 0 and tk <= k and k % tk == 0:
                need = (2 * (tm * tk * a.dtype.itemsize + tk * tn * b.dtype.itemsize) + 2 * tm * tn * jnp.dtype(out_dtype).itemsize
                        + 2 * tm * tn * 4 * has_add + tm * tn * 4 * (tk < k))
                if need <= MM_VMEM_BUDGET:
                    fits.append((k // tk, -tm, tm, tk))
    if fits:
        _, _, tm, tk = min(fits)
    else:
        tm, tk = _tile(m, (256, 128)), _tile(k, (512, 256, 128))
    nk = k // tk
    n0 = b_n[0] // tn if b_n else 0
    k0 = b_k[0] // tk if b_k else 0
    o0, n_out = (out_cols[0] // tn, out_cols[1]) if out_cols else (0, n)
    n_in = 2 + has_add + (out_into is not None)

    def body(*refs):
        a_ref, b_ref = refs[0], refs[1]
        add_ref = refs[2] if has_add else None
        o_ref = refs[n_in]
        def finish(r):
            if has_add:
                r = r + add_ref[...]
            o_ref[...] = r.astype(out_dtype)

        if nk == 1:
            finish(_dot(a_ref[...], b_ref[...], 0 if ta else 1, 1 if tb else 0))
        else:
            acc_ref = refs[-1]
            kk = pl.program_id(2)

            @pl.when(kk == 0)
            def _():
                acc_ref[...] = jnp.zeros_like(acc_ref)

            acc_ref[...] += _dot(a_ref[...], b_ref[...], 0 if ta else 1, 1 if tb else 0)

            @pl.when(kk == nk - 1)
            def _():
                finish(acc_ref[...])

    a_spec = pl.BlockSpec((tk, tm), lambda i, j, kk: (kk, i)) if ta else pl.BlockSpec((tm, tk), lambda i, j, kk: (i, kk))
    if tb:
        b_spec = pl.BlockSpec((tn, tk), lambda i, j, kk: (j + n0, kk + k0))
    else:
        b_spec = pl.BlockSpec((tk, tn), lambda i, j, kk: (kk + k0, j + n0))
    in_specs = [a_spec, b_spec] + ([pl.BlockSpec((tm, tn), lambda i, j, kk: (i, j))] if has_add else [])
    args = (a, b) + ((add,) if has_add else ())
    if out_into is not None:
        in_specs.append(pl.BlockSpec(memory_space=pl.ANY))
        args += (out_into,)
    return pl.pallas_call(
        body, name=name, grid=(m // tm, n // tn, nk), in_specs=in_specs,
        out_specs=pl.BlockSpec((tm, tn), lambda i, j, kk: (i, j + o0)),
        out_shape=jax.ShapeDtypeStruct((m, n_out), out_dtype),
        scratch_shapes=[pltpu.VMEM((tm, tn), F32)] if nk > 1 else [],
        input_output_aliases={n_in - 1: 0} if out_into is not None else {},
        compiler_params=_params("parallel", "parallel", "arbitrary"),
    )(*args)


def _rms_fwd(x, g, name):
    s, d = x.shape
    tr = _tile(s, (512, 256, 128))

    def body(x_ref, g_ref, h_ref):
        xv = x_ref[...]
        r = lax.rsqrt(jnp.mean(xv * xv, axis=-1, keepdims=True) + EPS)
        h_ref[...] = ((xv * r) * g_ref[...]).astype(BF16)

    return pl.pallas_call(
        body, name=name, grid=(s // tr,),
        in_specs=[pl.BlockSpec((tr, d), lambda i: (i, 0)), pl.BlockSpec((1, d), lambda i: (0, 0))],
        out_specs=pl.BlockSpec((tr, d), lambda i: (i, 0)),
        out_shape=jax.ShapeDtypeStruct((s, d), BF16), compiler_params=_params("parallel"),
    )(x, g.reshape(1, d))


def _rms_bwd(x, g, dh, dres, name):
    s, d = x.shape
    tr = _tile(s, (512, 256, 128))

    def body(x_ref, g_ref, dh_ref, dres_ref, dx_ref, dg_ref):
        xv = x_ref[...]
        r = lax.rsqrt(jnp.mean(xv * xv, axis=-1, keepdims=True) + EPS)
        dhv = dh_ref[...]
        u = dhv * g_ref[...]
        dx_ref[...] = dres_ref[...] + (r * u - xv * (r * r * r) * jnp.mean(xv * u, axis=-1, keepdims=True))
        part = jnp.sum(dhv * (xv * r), axis=0, keepdims=True)

        @pl.when(pl.program_id(0) == 0)
        def _():
            dg_ref[...] = part

        @pl.when(pl.program_id(0) > 0)
        def _():
            dg_ref[...] += part

    row = pl.BlockSpec((tr, d), lambda i: (i, 0))
    vec = pl.BlockSpec((1, d), lambda i: (0, 0))
    return pl.pallas_call(
        body, name=name, grid=(s // tr,), in_specs=[row, vec, row, row], out_specs=[row, vec],
        out_shape=[jax.ShapeDtypeStruct((s, d), F32), jax.ShapeDtypeStruct((1, d), F32)],
        compiler_params=_params("arbitrary"),
    )(x, g.reshape(1, d), dh, dres)


def _loss_head(y, target):
    s, d = y.shape
    tr = _tile(s, (512, 256, 128))

    def body(y_ref, t_ref, dy_ref, l_ref):
        err = y_ref[...] - t_ref[...]
        dy_ref[...] = err * (1.0 / d)
        part = jnp.sum(jnp.sum(err * err, axis=-1, keepdims=True) * (1.0 / d), axis=0, keepdims=True) * 0.5
        part = jnp.broadcast_to(part, (8, LANES))

        @pl.when(pl.program_id(0) == 0)
        def _():
            l_ref[...] = part

        @pl.when(pl.program_id(0) > 0)
        def _():
            l_ref[...] += part

    row = pl.BlockSpec((tr, d), lambda i: (i, 0))
    acc = pl.BlockSpec((8, LANES), lambda i: (0, 0))
    return pl.pallas_call(
        body, name="loss_head", grid=(s // tr,), in_specs=[row, row], out_specs=[row, acc],
        out_shape=[jax.ShapeDtypeStruct((s, d), F32), jax.ShapeDtypeStruct((8, LANES), F32)],
        compiler_params=_params("arbitrary"),
    )(y, target)


def _pool_lane_select(lane, v2, v4, v8, v16):
    return jnp.where(lane < 64, v2, jnp.where(lane < 128, v4, jnp.where(lane < 192, v8, v16)))


def _pool_diff(xc, xp, row0):
    n = BLOCK + HALO_POOL
    cat = jnp.concatenate([xp, xc], axis=0)
    s2 = cat + pltpu.roll(cat, 1, 0)
    s4 = s2 + pltpu.roll(s2, 2, 0)
    s8 = s4 + pltpu.roll(s4, 4, 0)
    s16 = s8 + pltpu.roll(s8, 8, 0)
    lane = lax.broadcasted_iota(jnp.int32, (n, POOL_W), 1)
    wsum = _pool_lane_select(lane, s2, s4, s8, s16)[HALO_POOL:]
    return wsum / _pool_count(row0, BLOCK) - xc


def _pool_count(row0, rows):
    lane = lax.broadcasted_iota(jnp.int32, (rows, POOL_W), 1)
    t = lax.broadcasted_iota(jnp.int32, (rows, POOL_W), 0) + row0
    return jnp.minimum(t + 1, _pool_lane_select(lane, 2, 4, 8, 16)).astype(F32)


def _pool_fwd(z, bdw, scale, name):
    s = z.shape[0]
    nb = s // BLOCK

    def body(xc_ref, xp_ref, w_ref, sc_ref, a_ref):
        i = pl.program_id(0)
        xp = jnp.where(i > 0, xp_ref[...], 0.0)
        diff = _pool_diff(xc_ref[...], xp, i * BLOCK)
        a_ref[...] = (_dot(diff, w_ref[...]) * sc_ref[...]).astype(BF16)

    return pl.pallas_call(
        body, name=name, grid=(nb,),
        in_specs=[pl.BlockSpec((BLOCK, POOL_W), lambda i: (i, 0)),
                  pl.BlockSpec((HALO_POOL, POOL_W), lambda i: (jnp.maximum(i * (BLOCK // HALO_POOL) - 1, 0), 0)),
                  pl.BlockSpec((POOL_W, POOL_W), lambda i: (0, 0)),
                  pl.BlockSpec((1, POOL_W), lambda i: (0, 0))],
        out_specs=pl.BlockSpec((BLOCK, POOL_W), lambda i: (i, 0)),
        out_shape=jax.ShapeDtypeStruct((s, POOL_W), BF16), compiler_params=_params("parallel"),
    )(z, z, bdw, scale.reshape(1, POOL_W))


def _pool_bwd(z, da, bdw, scale, name):
    s = z.shape[0]
    nb = s // BLOCK
    per = BLOCK // HALO_POOL
    n = BLOCK + HALO_POOL

    def body(xc_ref, xp_ref, dac_ref, dan_ref, w_ref, sc_ref, dx_ref, dw_ref, dsc_ref):
        i = pl.program_id(0)
        xp = jnp.where(i > 0, xp_ref[...], 0.0)
        diff = _pool_diff(xc_ref[...], xp, i * BLOCK)
        mixed = _dot(diff, w_ref[...])
        dac = dac_ref[...]
        dan = jnp.where(i < nb - 1, dan_ref[...], 0.0)
        dmix = jnp.concatenate([dac, dan], axis=0) * sc_ref[...]
        ddiff = _dot(dmix, w_ref[...], 1, 1)
        e = ddiff / _pool_count(i * BLOCK, n)
        f2 = e + pltpu.roll(e, n - 1, 0)
        f4 = f2 + pltpu.roll(f2, n - 2, 0)
        f8 = f4 + pltpu.roll(f4, n - 4, 0)
        f16 = f8 + pltpu.roll(f8, n - 8, 0)
        lane = lax.broadcasted_iota(jnp.int32, (n, POOL_W), 1)
        back = _pool_lane_select(lane, f2, f4, f8, f16)
        dx_ref[...] = (back[:BLOCK] - ddiff[:BLOCK]).astype(BF16)
        dw = _dot(diff, dmix[:BLOCK], 0, 0)
        dsc = jnp.sum(dac * mixed, axis=0, keepdims=True)

        @pl.when(i == 0)
        def _():
            dw_ref[...] = dw
            dsc_ref[...] = dsc

        @pl.when(i > 0)
        def _():
            dw_ref[...] += dw
            dsc_ref[...] += dsc

    blk = pl.BlockSpec((BLOCK, POOL_W), lambda i: (i, 0))
    return pl.pallas_call(
        body, name=name, grid=(nb,),
        in_specs=[blk, pl.BlockSpec((HALO_POOL, POOL_W), lambda i: (jnp.maximum(i * per - 1, 0), 0)),
                  blk, pl.BlockSpec((HALO_POOL, POOL_W), lambda i: (jnp.minimum((i + 1) * per, nb * per - 1), 0)),
                  pl.BlockSpec((POOL_W, POOL_W), lambda i: (0, 0)), pl.BlockSpec((1, POOL_W), lambda i: (0, 0))],
        out_specs=[blk, pl.BlockSpec((POOL_W, POOL_W), lambda i: (0, 0)), pl.BlockSpec((1, POOL_W), lambda i: (0, 0))],
        out_shape=[jax.ShapeDtypeStruct((s, POOL_W), BF16), jax.ShapeDtypeStruct((POOL_W, POOL_W), F32),
                   jax.ShapeDtypeStruct((1, POOL_W), F32)],
        compiler_params=_params("arbitrary"),
    )(z, z, da, da, bdw, scale.reshape(1, POOL_W))


def _attn_setup(zc_ref, zp_ref, cc_ref, cp_ref, sc_ref, sp_ref, qn_ref, kn_ref, bd):
    q = []
    for j in range(ATTN_W // LANES):
        t = zc_ref[:, POOL_W + j * LANES:POOL_W + (j + 1) * LANES]
        q.append((t,) + _norm_rope(t, qn_ref[...], cc_ref[...], sc_ref[...], bd))
    kc_raw = zc_ref[:, POOL_W + ATTN_W:POOL_W + ATTN_W + KV_W]
    kc = _norm_rope(kc_raw, kn_ref[...], cc_ref[...], sc_ref[...], bd)[0]
    kp = _norm_rope(zp_ref[:, :KV_W], kn_ref[...], cp_ref[...], sp_ref[...], bd)[0]
    kband = jnp.concatenate([kp, kc], axis=0).astype(BF16)
    vband = jnp.concatenate([zp_ref[:, KV_W:], zc_ref[:, POOL_W + ATTN_W + KV_W:POOL_W + ATTN_W + 2 * KV_W]], axis=0).astype(BF16)
    return q, kband, vband


def _attn_mask(i):
    row = lax.broadcasted_iota(jnp.int32, (Q_PER_KV * BLOCK, 2 * BLOCK), 0) & (BLOCK - 1)
    col = lax.broadcasted_iota(jnp.int32, (Q_PER_KV * BLOCK, 2 * BLOCK), 1)
    dist = row + BLOCK - col
    return (dist >= 0) & (dist < BLOCK) & ((col >= BLOCK) | (i > 0))


def _stack_heads(tiles, kvh):
    return jnp.concatenate([_to_kv_lanes(tiles[h // 2], h) for h in range(kvh * Q_PER_KV, (kvh + 1) * Q_PER_KV)], axis=0)


def _unstack_heads(stacked, kvh, tiles):
    for g in range(Q_PER_KV):
        h = kvh * Q_PER_KV + g
        t = _from_kv_lanes(stacked[g * BLOCK:(g + 1) * BLOCK], h)
        tiles[h // 2] = t if tiles[h // 2] is None else tiles[h // 2] + t


def _sink_column(sink_ref, kvh):
    grp = lax.broadcasted_iota(jnp.int32, (Q_PER_KV * BLOCK, 1), 0) >> 7
    s = [sink_ref[kvh * Q_PER_KV + g] for g in range(Q_PER_KV)]
    return jnp.where(grp == 0, s[0], jnp.where(grp == 1, s[1], jnp.where(grp == 2, s[2], s[3])))


def _to_kv_lanes(t, h):
    kvh = h // Q_PER_KV
    if (h % 2) != kvh:
        t = pltpu.roll(t, HEAD_DIM, 1)
    lane = lax.broadcasted_iota(jnp.int32, t.shape, 1)
    return jnp.where((lane >= HEAD_DIM) == (kvh == 1), t, 0.0)


def _from_kv_lanes(t, h):
    kvh = h // Q_PER_KV
    lane = lax.broadcasted_iota(jnp.int32, t.shape, 1)
    t = jnp.where((lane >= HEAD_DIM) == (kvh == 1), t, 0.0)
    if (h % 2) != kvh:
        t = pltpu.roll(t, HEAD_DIM, 1)
    return t


def _attn_probs(qh, kband, mask, sink):
    sc = _dot(qh, kband, 1, 1) * (HEAD_DIM ** -0.5)
    sc = jnp.where(mask, sc, -1e30)
    m = jnp.maximum(jnp.max(sc, axis=1, keepdims=True), sink)
    p = jnp.exp(sc - m)
    psink = jnp.exp(sink - m)
    den = jnp.sum(p, axis=1, keepdims=True) + psink
    return p / den, psink / den


def _attn_specs(nb):
    cur = lambda i: (i, 0)
    prev = lambda i: (jnp.maximum(i - 1, 0), 0)
    tab = lambda f: pl.BlockSpec((BLOCK, LANES), f)
    vec = pl.BlockSpec((1, LANES), lambda i: (0, 0))
    return [pl.BlockSpec((BLOCK, 1024), cur),
            pl.BlockSpec((BLOCK, 2 * KV_W), lambda i: (jnp.maximum(i - 1, 0), 3)),
            tab(cur), tab(prev), tab(cur), tab(prev), vec, vec,
            pl.BlockSpec(memory_space=pltpu.SMEM)]


def _attn_fwd(z, cosf, sinf, qn, kn, sinks, name):
    s = z.shape[0]
    nb = s // BLOCK

    def body(zc_ref, zp_ref, cc_ref, cp_ref, sc_ref, sp_ref, qn_ref, kn_ref, sink_ref, o_ref):
        i = pl.program_id(0)
        bd = _head_mean_matrix()
        q, kband, vband = _attn_setup(zc_ref, zp_ref, cc_ref, cp_ref, sc_ref, sp_ref, qn_ref, kn_ref, bd)
        mask = _attn_mask(i)
        out = [None] * (ATTN_W // LANES)
        for kvh in range(N_Q_HEADS // Q_PER_KV):
            qs = _stack_heads([t[1] for t in q], kvh)
            probs, _ = _attn_probs(qs, kband, mask, _sink_column(sink_ref, kvh))
            _unstack_heads(_dot(probs, vband), kvh, out)
        for j, o in enumerate(out):
            o_ref[:, j * LANES:(j + 1) * LANES] = o.astype(BF16)

    return pl.pallas_call(
        body, name=name, grid=(nb,), in_specs=_attn_specs(nb),
        out_specs=pl.BlockSpec((BLOCK, ATTN_W), lambda i: (i, 0)),
        out_shape=jax.ShapeDtypeStruct((s, ATTN_W), BF16), compiler_params=_params("parallel"),
    )(z, z, cosf, cosf, sinf, sinf, qn, kn, sinks)


def _attn_bwd(z, cosf, sinf, qn, kn, sinks, d_out, name):
    s = z.shape[0]
    nb = s // BLOCK
    nt = ATTN_W // LANES

    def body(zc_ref, zp_ref, cc_ref, cp_ref, sc_ref, sp_ref, qn_ref, kn_ref, sink_ref, do_ref,
             dq_ref, dkc_ref, dkp_ref, dvc_ref, dvp_ref, dqn_ref, dsink_ref):
        i = pl.program_id(0)
        bd = _head_mean_matrix()
        q, kband, vband = _attn_setup(zc_ref, zp_ref, cc_ref, cp_ref, sc_ref, sp_ref, qn_ref, kn_ref, bd)
        mask = _attn_mask(i)

        @pl.when(i == 0)
        def _():
            dqn_ref[...] = jnp.zeros_like(dqn_ref)
            dsink_ref[...] = jnp.zeros_like(dsink_ref)

        dq = [None] * nt
        dk = jnp.zeros((2 * BLOCK, KV_W), F32)
        dv = jnp.zeros((2 * BLOCK, KV_W), F32)
        d_tiles = [do_ref[:, j * LANES:(j + 1) * LANES] for j in range(nt)]
        for kvh in range(N_Q_HEADS // Q_PER_KV):
            qs = _stack_heads([t[1] for t in q], kvh)
            probs, psink = _attn_probs(qs, kband, mask, _sink_column(sink_ref, kvh))
            dos = _stack_heads(d_tiles, kvh)
            dp = _dot(dos, vband, 1, 1)
            delta = jnp.sum(dp * probs, axis=1, keepdims=True)
            ds = (probs * (dp - delta)) * (HEAD_DIM ** -0.5)
            dsink = -psink * delta
            for g in range(Q_PER_KV):
                h = kvh * Q_PER_KV + g
                dsink_ref[h:h + 1, :] += jnp.broadcast_to(jnp.sum(dsink[g * BLOCK:(g + 1) * BLOCK], axis=0, keepdims=True), (1, LANES))
            _unstack_heads(_dot(ds, kband), kvh, dq)
            dk = dk + _dot(ds, qs, 0, 0)
            dv = dv + _dot(probs, dos, 0, 0)
        dgn = jnp.zeros((1, LANES), F32)
        for j in range(nt):
            t, _, n, r = q[j]
            dt, g = _norm_rope_bwd(dq[j], t, n, r, qn_ref[...], cc_ref[...], sc_ref[...], bd)
            dq_ref[:, j * LANES:(j + 1) * LANES] = dt.astype(BF16)
            dgn = dgn + g
        dqn_ref[...] += jnp.broadcast_to(dgn, (8, LANES))
        dkp_ref[...] = dk[:BLOCK]
        dkc_ref[...] = dk[BLOCK:]
        dvp_ref[...] = dv[:BLOCK]
        dvc_ref[...] = dv[BLOCK:]

        @pl.when(i == nb - 1)
        def _():
            acc = dqn_ref[...]
            dqn_ref[...] = acc + pltpu.roll(acc, HEAD_DIM, 1)

    blk = lambda w: pl.BlockSpec((BLOCK, w), lambda i: (i, 0))
    acc = pl.BlockSpec((8, LANES), lambda i: (0, 0))
    kv = jax.ShapeDtypeStruct((s, KV_W), F32)
    return pl.pallas_call(
        body, name=name, grid=(nb,), in_specs=_attn_specs(nb) + [blk(ATTN_W)],
        out_specs=[blk(ATTN_W), blk(KV_W), blk(KV_W), blk(KV_W), blk(KV_W), acc, acc],
        out_shape=[jax.ShapeDtypeStruct((s, ATTN_W), BF16), kv, kv, kv, kv,
                   jax.ShapeDtypeStruct((8, LANES), F32), jax.ShapeDtypeStruct((8, LANES), F32)],
        compiler_params=_params("arbitrary"),
    )(z, z, cosf, cosf, sinf, sinf, qn, kn, sinks, d_out)


def _kv_post(z, cosf, sinf, kn, dkc, dkp, dvc, dvp, dxp, dq, duv, dz, name):
    s = z.shape[0]
    nb = s // BLOCK

    def body(zk_ref, c_ref, s_ref, kn_ref, dkc_ref, dkp_ref, dvc_ref, dvp_ref, dxp_ref, dq_ref, duv_ref, dz_in,
             dz_ref, dkn_ref):
        j = pl.program_id(0)
        bd = _head_mean_matrix()
        last = j == nb - 1
        d = dkc_ref[...] + jnp.where(last, 0.0, dkp_ref[...])
        t = zk_ref[:, :KV_W]
        _, n, r = _norm_rope(t, kn_ref[...], c_ref[...], s_ref[...], bd)
        dt, g = _norm_rope_bwd(d, t, n, r, kn_ref[...], c_ref[...], s_ref[...], bd)
        dvv = dvc_ref[...] + jnp.where(last, 0.0, dvp_ref[...])
        dz_ref[:, 0:POOL_W] = dxp_ref[...]
        dz_ref[:, POOL_W:POOL_W + ATTN_W] = dq_ref[...]
        dz_ref[:, POOL_W + ATTN_W:POOL_W + ATTN_W + KV_W] = dt.astype(BF16)
        dz_ref[:, POOL_W + ATTN_W + KV_W:POOL_W + ATTN_W + 2 * KV_W] = dvv.astype(BF16)
        dz_ref[:, POOL_W + ATTN_W + 2 * KV_W:GATE_COL0] = duv_ref[...]

        @pl.when(j == 0)
        def _():
            dkn_ref[...] = jnp.zeros_like(dkn_ref)

        dkn_ref[...] += jnp.broadcast_to(g, (8, LANES))

        @pl.when(last)
        def _():
            acc = dkn_ref[...]
            dkn_ref[...] = acc + pltpu.roll(acc, HEAD_DIM, 1)

    cur = lambda w: pl.BlockSpec((BLOCK, w), lambda j: (j, 0))
    nxt = pl.BlockSpec((BLOCK, KV_W), lambda j: (jnp.minimum(j + 1, nb - 1), 0))
    vec = pl.BlockSpec((1, LANES), lambda j: (0, 0))
    return pl.pallas_call(
        body, name=name, grid=(nb,),
        in_specs=[pl.BlockSpec((BLOCK, 2 * KV_W), lambda j: (j, 3)), cur(LANES), cur(LANES), vec,
                  cur(KV_W), nxt, cur(KV_W), nxt, cur(POOL_W), cur(ATTN_W), cur(2 * SGU_W),
                  pl.BlockSpec(memory_space=pl.ANY)],
        out_specs=[pl.BlockSpec((BLOCK, GATE_COL0), lambda j: (j, 0)), pl.BlockSpec((8, LANES), lambda j: (0, 0))],
        out_shape=[jax.ShapeDtypeStruct(dz.shape, dz.dtype), jax.ShapeDtypeStruct((8, LANES), F32)],
        input_output_aliases={11: 0}, compiler_params=_params("arbitrary"),
    )(z, cosf, sinf, kn, dkc, dkp, dvc, dvp, dxp, dq, duv, dz)


def _sgu_setup(z_ref, ws_ref, vn_ref, bd):
    us = z_ref[:, :SGU_W]
    vs = z_ref[:, SGU_W:]
    ug, tu = _gelu(us)
    gv, tv = _gelu(vs)
    rr = jnp.concatenate([lax.rsqrt(_head_mean(gv[:, k * LANES:(k + 1) * LANES] ** 2, bd) + EPS) for k in range(2)], axis=1)
    vg = (gv * rr) * vn_ref[...]
    tril = lax.broadcasted_iota(jnp.int32, (BLOCK, BLOCK), 0) >= lax.broadcasted_iota(jnp.int32, (BLOCK, BLOCK), 1)
    w = [jnp.where(tril, ws_ref[g], 0.0).astype(BF16) for g in range(4)]
    return us, vs, ug, tu, gv, tv, rr, vg, w, tril


def _group_select(parts):
    lane = lax.broadcasted_iota(jnp.int32, parts[0].shape, 1)
    return _pool_lane_select(lane, *parts)


def _sgu_fwd(z, ws, bcol, vn, name):
    s = z.shape[0]
    nb = s // BLOCK

    def body(z_ref, ws_ref, b_ref, vn_ref, c_ref):
        bd = _head_mean_matrix()
        _, _, ug, _, _, _, _, vg, w, _ = _sgu_setup(z_ref, ws_ref, vn_ref, bd)
        sg = _group_select([_dot(w[g], vg) for g in range(4)]) + b_ref[...]
        c_ref[...] = (ug * sg).astype(BF16)

    return pl.pallas_call(
        body, name=name, grid=(nb,),
        in_specs=[pl.BlockSpec((BLOCK, 2 * SGU_W), lambda i: (i, 2)), pl.BlockSpec((4, BLOCK, BLOCK), lambda i: (0, 0, 0)),
                  pl.BlockSpec((BLOCK, SGU_W), lambda i: (0, 0)), pl.BlockSpec((1, SGU_W), lambda i: (0, 0))],
        out_specs=pl.BlockSpec((BLOCK, SGU_W), lambda i: (i, 0)),
        out_shape=jax.ShapeDtypeStruct((s, SGU_W), BF16), compiler_params=_params("parallel"),
    )(z, ws, bcol, vn)


def _sgu_bwd(z, ws, bcol, vn, dc, name):
    s = z.shape[0]
    nb = s // BLOCK

    def body(z_ref, ws_ref, b_ref, vn_ref, dc_ref, duv_ref, dws_ref, db_ref, dvn_ref):
        i = pl.program_id(0)
        bd = _head_mean_matrix()
        us, vs, ug, tu, gv, tv, rr, vg, w, tril = _sgu_setup(z_ref, ws_ref, vn_ref, bd)
        sg = _group_select([_dot(w[g], vg) for g in range(4)]) + b_ref[...]
        dcv = dc_ref[...]
        dug = dcv * sg
        dsg = dcv * ug
        lane = lax.broadcasted_iota(jnp.int32, dsg.shape, 1)

        @pl.when(i == 0)
        def _():
            dws_ref[...] = jnp.zeros_like(dws_ref)
            db_ref[...] = jnp.zeros_like(db_ref)
            dvn_ref[...] = jnp.zeros_like(dvn_ref)

        for g in range(4):
            dsg_g = jnp.where((lane >= g * HEAD_DIM) & (lane < (g + 1) * HEAD_DIM), dsg, 0.0)
            dws_ref[g] += jnp.where(tril, _dot(dsg_g, vg, 1, 1), 0.0)
        dvg = _group_select([_dot(w[g], dsg, 0, 0) for g in range(4)])
        db_ref[...] += dsg
        n = gv * rr
        part = jnp.sum(dvg * n, axis=0, keepdims=True)
        dvn_ref[...] += jnp.broadcast_to(part[:, :LANES] + part[:, LANES:], (8, LANES))
        u = dvg * vn_ref[...]
        tu_ = gv * u
        mean = jnp.concatenate([_head_mean(tu_[:, k * LANES:(k + 1) * LANES], bd) for k in range(2)], axis=1)
        dgv = rr * u - gv * (rr * rr * rr) * mean
        duv_ref[:, :SGU_W] = (dug * _gelu_grad(us, tu)).astype(BF16)
        duv_ref[:, SGU_W:] = (dgv * _gelu_grad(vs, tv)).astype(BF16)

        @pl.when(i == nb - 1)
        def _():
            acc = dvn_ref[...]
            dvn_ref[...] = acc + pltpu.roll(acc, HEAD_DIM, 1)
            for k in range(2):
                db_ref[:, k * LANES:(k + 1) * LANES] = _head_mean(db_ref[:, k * LANES:(k + 1) * LANES], bd) * float(HEAD_DIM)

    return pl.pallas_call(
        body, name=name, grid=(nb,),
        in_specs=[pl.BlockSpec((BLOCK, 2 * SGU_W), lambda i: (i, 2)), pl.BlockSpec((4, BLOCK, BLOCK), lambda i: (0, 0, 0)),
                  pl.BlockSpec((BLOCK, SGU_W), lambda i: (0, 0)), pl.BlockSpec((1, SGU_W), lambda i: (0, 0)),
                  pl.BlockSpec((BLOCK, SGU_W), lambda i: (i, 0))],
        out_specs=[pl.BlockSpec((BLOCK, 2 * SGU_W), lambda i: (i, 0)), pl.BlockSpec((4, BLOCK, BLOCK), lambda i: (0, 0, 0)),
                   pl.BlockSpec((BLOCK, SGU_W), lambda i: (0, 0)), pl.BlockSpec((8, LANES), lambda i: (0, 0))],
        out_shape=[jax.ShapeDtypeStruct((s, 2 * SGU_W), BF16), jax.ShapeDtypeStruct((4, BLOCK, BLOCK), F32),
                   jax.ShapeDtypeStruct((BLOCK, SGU_W), F32), jax.ShapeDtypeStruct((8, LANES), F32)],
        compiler_params=_params("arbitrary"),
    )(z, ws, bcol, vn, dc)


MERGE_TN = 512
MERGE_TM = 512


def _merge_fwd(a, b, c, wpa, wpb, wpc, z, name):
    s = z.shape[0]
    tm = _tile(s, (MERGE_TM, BLOCK))
    gate0 = GATE_COL0 // MERGE_TN

    def body(a_ref, b_ref, c_ref, wa_ref, wb_ref, wc_ref, g0_ref, g1_ref, g2_ref, o_ref):
        r = _sigmoid(g0_ref[...]) * _dot(a_ref[...], wa_ref[...])
        r = r + _sigmoid(g1_ref[...]) * _dot(b_ref[...], wb_ref[...])
        r = r + _sigmoid(g2_ref[...]) * _dot(c_ref[...], wc_ref[...])
        o_ref[...] = r.astype(BF16)

    x_spec = lambda w: pl.BlockSpec((tm, w), lambda i, n: (i, 0))
    w_spec = lambda w: pl.BlockSpec((w, MERGE_TN), lambda i, n: (0, n))
    g_spec = lambda br: pl.BlockSpec((tm, MERGE_TN), lambda i, n: (i, gate0 + 2 * br + n))
    return pl.pallas_call(
        body, name=name, grid=(s // tm, D_MODEL // MERGE_TN),
        in_specs=[x_spec(POOL_W), x_spec(ATTN_W), x_spec(SGU_W), w_spec(POOL_W), w_spec(ATTN_W), w_spec(SGU_W),
                  g_spec(0), g_spec(1), g_spec(2)],
        out_specs=pl.BlockSpec((tm, MERGE_TN), lambda i, n: (i, n)),
        out_shape=jax.ShapeDtypeStruct((s, D_MODEL), BF16), compiler_params=_params("parallel", "parallel"),
    )(a, b, c, wpa, wpb, wpc, z, z, z)


def _branch_bwd(br, xb, wp, z, dm, dz, name):
    s = z.shape[0]
    kb = xb.shape[1]
    tm = _tile(s, (MERGE_TM, BLOCK))
    gate0 = GATE_COL0 // MERGE_TN
    aliased = dz is not None

    def body(*refs):
        x_ref, w_ref, g_ref, dm_ref = refs[:4]
        dz_ref, dy_ref, dx_ref = refs[-3:]
        n = pl.program_id(1)
        y = _dot(x_ref[...], w_ref[...])
        sg = _sigmoid(g_ref[...])
        dmv = dm_ref[...]
        dy = (dmv * sg).astype(BF16)
        dy_ref[...] = dy
        dz_ref[...] = ((dmv * y) * (sg * (1.0 - sg))).astype(BF16)
        dx = _dot(dy, w_ref[...], 1, 1)

        @pl.when(n == 0)
        def _():
            dx_ref[...] = dx

        @pl.when(n > 0)
        def _():
            dx_ref[...] += dx

    in_specs = [pl.BlockSpec((tm, kb), lambda i, n: (i, 0)), pl.BlockSpec((kb, MERGE_TN), lambda i, n: (0, n)),
                pl.BlockSpec((tm, MERGE_TN), lambda i, n: (i, gate0 + 2 * br + n)),
                pl.BlockSpec((tm, MERGE_TN), lambda i, n: (i, n))]
    args = [xb, wp, z, dm]
    if aliased:
        in_specs.append(pl.BlockSpec(memory_space=pl.ANY))
        args.append(dz)
    return pl.pallas_call(
        body, name=name, grid=(s // tm, D_MODEL // MERGE_TN), in_specs=in_specs,
        out_specs=[pl.BlockSpec((tm, MERGE_TN), lambda i, n: (i, gate0 + 2 * br + n)),
                   pl.BlockSpec((tm, MERGE_TN), lambda i, n: (i, n)),
                   pl.BlockSpec((tm, kb), lambda i, n: (i, 0))],
        out_shape=[jax.ShapeDtypeStruct((s, IN_COLS), BF16), jax.ShapeDtypeStruct((s, D_MODEL), BF16),
                   jax.ShapeDtypeStruct((s, kb), F32)],
        input_output_aliases={4: 0} if aliased else {},
        compiler_params=_params("parallel", "arbitrary"),
    )(*args)


FFN_TM = 256
FFN_TC = 2816
FFN_STRIP = 256
FFN_UNROLL = 4


def _conv3(cur, prev, w_ref, b_ref):
    cat = jnp.concatenate([prev, cur], axis=0)
    x1 = pltpu.roll(cat, 1, 0)[HALO_CONV:]
    x2 = pltpu.roll(cat, 2, 0)[HALO_CONV:]
    return w_ref[0:1, :] * x2 + w_ref[1:2, :] * x1 + w_ref[2:3, :] * cur + b_ref[...], x1, x2


def _ffn_specs(s, tm, rows_first):
    per = tm // HALO_CONV
    if rows_first:
        cur = pl.BlockSpec((tm, FFN_TC), lambda i, j: (i, j))
        prev = pl.BlockSpec((HALO_CONV, FFN_TC), lambda i, j: (jnp.maximum(i * per - 1, 0), j))
        w = pl.BlockSpec((3, FFN_TC), lambda i, j: (0, j))
        b = pl.BlockSpec((1, FFN_TC), lambda i, j: (0, j))
    else:
        cur = pl.BlockSpec((tm, FFN_TC), lambda j, i: (i, j))
        prev = pl.BlockSpec((HALO_CONV, FFN_TC), lambda j, i: (jnp.maximum(i * per - 1, 0), j))
        w = pl.BlockSpec((3, FFN_TC), lambda j, i: (0, j))
        b = pl.BlockSpec((1, FFN_TC), lambda j, i: (0, j))
    return cur, prev, w, b


def _ffn_act_fwd(upg, upv, cwg, cwv, cbg, cbv, name):
    s = upg.shape[0]
    tm = _tile(s, (FFN_TM, BLOCK))
    cur, prev, w, b = _ffn_specs(s, tm, True)

    def body(g_ref, gp_ref, v_ref, vp_ref, wg_ref, wv_ref, bg_ref, bv_ref, o_ref):
        first = pl.program_id(0) == 0
        gate = _conv3(g_ref[...], jnp.where(first, 0.0, gp_ref[...]), wg_ref, bg_ref)[0]
        val = _conv3(v_ref[...], jnp.where(first, 0.0, vp_ref[...]), wv_ref, bv_ref)[0]
        o_ref[...] = ((gate * _sigmoid(gate)) * val).astype(BF16)

    return pl.pallas_call(
        body, name=name, grid=(s // tm, D_FF // FFN_TC), in_specs=[cur, prev, cur, prev, w, w, b, b], out_specs=cur,
        out_shape=jax.ShapeDtypeStruct((s, D_FF), BF16), compiler_params=_params("parallel", "parallel"),
    )(upg, upg, upv, upv, cwg, cwv, cbg, cbv)


def _ffn_bwd(upg, upv, cwg, cwv, cbg, cbv, dact, name):
    s = upg.shape[0]
    tm = _tile(s, (FFN_TM, BLOCK))
    per = tm // HALO_CONV
    nrow = s // tm
    n = tm + HALO_CONV
    cur, prev, w, b = _ffn_specs(s, tm, False)
    nxt = pl.BlockSpec((HALO_CONV, FFN_TC), lambda j, i: (jnp.minimum((i + 1) * per, nrow * per - 1), j))

    nch = tm // 8
    rows8 = lambda r: pl.ds(pl.multiple_of(r * 8, 8), 8)

    def body(g_ref, gp_ref, gn_ref, v_ref, vp_ref, vn_ref, wg_ref, wv_ref, bg_ref, bv_ref, da_ref, dan_ref,
             dg_ref, dv_ref, dwg_ref, dwv_ref, og_ref, ov_ref):
        i = pl.program_id(1)
        first = i == 0
        last = i == nrow - 1
        row = lax.broadcasted_iota(jnp.int32, (8, FFN_STRIP), 0)

        keep_down = {k: row >= k for k in (1, 2)}
        keep_up = {k: row < 8 - k for k in (1, 2)}

        def down(cur, prev, k):
            return jnp.where(keep_down[k], pltpu.roll(cur, k, 0), pltpu.roll(prev, k, 0))

        def up(cur, nxt, k):
            return jnp.where(keep_up[k], pltpu.roll(cur, 8 - k, 0), pltpu.roll(nxt, 8 - k, 0))

        @pl.when(first)
        def _():
            dwg_ref[...] = jnp.zeros_like(dwg_ref)
            dwv_ref[...] = jnp.zeros_like(dwv_ref)

        for c in range(FFN_TC // FFN_STRIP):
            cols = slice(c * FFN_STRIP, (c + 1) * FFN_STRIP)
            wg = [functools.partial(lambda k: wg_ref[k:k + 1, cols], k) for k in range(3)]
            wv = [functools.partial(lambda k: wv_ref[k:k + 1, cols], k) for k in range(3)]

            def conv_grads(g_cur, g_prev, v_cur, v_prev, da):
                gate = wg[0]() * down(g_cur, g_prev, 2) + wg[1]() * down(g_cur, g_prev, 1) + wg[2]() * g_cur + bg_ref[:, cols]
                val = wv[0]() * down(v_cur, v_prev, 2) + wv[1]() * down(v_cur, v_prev, 1) + wv[2]() * v_cur + bv_ref[:, cols]
                sg = _sigmoid(gate)
                return (da * val) * (sg * (1.0 + gate * (1.0 - sg))), da * (gate * sg)

            def passes(q, carry):
                for u in range(FFN_UNROLL):
                    carry = one_pass(q * FFN_UNROLL + u, carry)
                return carry

            def one_pass(r, carry, tile_end=False):
                dg_cur, dv_cur, acc = carry
                g_r, v_r = g_ref[rows8(r), cols], v_ref[rows8(r), cols]
                if tile_end:
                    g_n, v_n, da_n = gn_ref[:, cols], vn_ref[:, cols], jnp.where(last, 0.0, dan_ref[:, cols])
                else:
                    g_n, v_n, da_n = g_ref[rows8(r + 1), cols], v_ref[rows8(r + 1), cols], da_ref[rows8(r + 1), cols]
                dg_n, dv_n = conv_grads(g_n, g_r, v_n, v_r, da_n)
                new_acc = []
                for o_ref, w, d_cur, d_n, x0, a in ((og_ref, wg, dg_cur, dg_n, g_r, acc[:4]), (ov_ref, wv, dv_cur, dv_n, v_r, acc[4:])):
                    d1, d2 = up(d_cur, d_n, 1), up(d_cur, d_n, 2)
                    o_ref[rows8(r), cols] = w[2]() * d_cur + w[1]() * d1 + w[0]() * d2
                    new_acc += [a[0] + d2 * x0, a[1] + d1 * x0, a[2] + d_cur * x0, a[3] + d_cur]
                return dg_n, dv_n, tuple(new_acc)

            g_p = jnp.where(first, 0.0, gp_ref[:, cols])
            v_p = jnp.where(first, 0.0, vp_ref[:, cols])
            dg0, dv0 = conv_grads(g_ref[0:8, cols], g_p, v_ref[0:8, cols], v_p, da_ref[0:8, cols])
            zero = jnp.zeros((8, FFN_STRIP), F32)
            carry = lax.fori_loop(0, nch // FFN_UNROLL - 1, passes, (dg0, dv0, (zero,) * 8))
            for r in range(nch - FFN_UNROLL, nch):
                carry = one_pass(r, carry, tile_end=r == nch - 1)
            for ref, a in ((dwg_ref, carry[2][:4]), (dwv_ref, carry[2][4:])):
                for k in range(4):
                    ref[k:k + 1, cols] += jnp.sum(a[k], axis=0, keepdims=True)
        dg_ref[...] = og_ref[...].astype(BF16)
        dv_ref[...] = ov_ref[...].astype(BF16)

    acc = pl.BlockSpec((8, FFN_TC), lambda j, i: (0, j))
    full = jax.ShapeDtypeStruct((s, D_FF), BF16)
    accs = jax.ShapeDtypeStruct((8, D_FF), F32)
    return pl.pallas_call(
        body, name=name, grid=(D_FF // FFN_TC, nrow), in_specs=[cur, prev, nxt, cur, prev, nxt, w, w, b, b, cur, nxt],
        out_specs=[cur, cur, acc, acc], out_shape=[full, full, accs, accs],
        scratch_shapes=[pltpu.VMEM((tm, FFN_TC), F32), pltpu.VMEM((tm, FFN_TC), F32)],
        compiler_params=_params("parallel", "arbitrary"),
    )(upg, upg, upg, upv, upv, upv, cwg, cwv, cbg, cbv, dact, dact)


def _mesh_place():
    return lax.axis_index("x"), lax.axis_index("y"), lax.axis_index("c")


def _all_gather(shards, name):
    na = len(shards)

    def body(*refs):
        x_refs, out_refs = refs[:na], refs[na:2 * na]
        send_sems, recv_sems, local_sems = refs[2 * na:]
        x, y, cc = _mesh_place()
        me, sibling = (x, y, cc), (x, y, 1 - cc)
        chips = [(1 - x, y), (x, 1 - y), (1 - x, 1 - y)]

        def copy(k, a, block, to, from_input=False):
            slot = out_refs[a].at[4 * block[0] + 2 * block[1] + block[2]]
            return pltpu.make_async_remote_copy(
                src_ref=x_refs[a] if from_input else slot, dst_ref=slot, send_sem=send_sems.at[k * na + a],
                recv_sem=recv_sems.at[k * na + a], device_id=to, device_id_type=pl.DeviceIdType.MESH)

        mine = [pltpu.make_async_copy(x_refs[a], out_refs[a].at[4 * x + 2 * y + cc], local_sems.at[a]) for a in range(na)]
        for cp in mine:
            cp.start()
        first = [copy(0, a, me, sibling, True) for a in range(na)]
        first += [copy(1 + j, a, me, (*chip, cc), True) for j, chip in enumerate(chips) for a in range(na)]
        for cp in first:
            cp.start()
        passed = []
        for j, chip in enumerate(chips):
            for a in range(na):
                copy(1 + j, a, (*chip, cc), me).wait_recv()
                passed.append(copy(4 + j, a, (*chip, cc), sibling))
                passed[-1].start()
        for a in range(na):
            copy(0, a, sibling, me).wait_recv()
        for j, chip in enumerate(chips):
            for a in range(na):
                copy(4 + j, a, (*chip, 1 - cc), me).wait_recv()
        for cp in first + passed:
            cp.wait_send()
        for cp in mine:
            cp.wait()

    hbm = pl.BlockSpec(memory_space=pl.ANY)
    return pl.pallas_call(
        body, name=name, out_shape=[jax.ShapeDtypeStruct((N_DEV,) + t.shape, t.dtype) for t in shards],
        in_specs=[hbm] * na, out_specs=[hbm] * na,
        scratch_shapes=[pltpu.SemaphoreType.DMA((7 * na,)), pltpu.SemaphoreType.DMA((7 * na,)), pltpu.SemaphoreType.DMA((na,))],
    )(*shards)


def _exchange(blocks, name):
    na = len(blocks)

    def body(*refs):
        g_refs, out_refs = refs[:na], refs[na:2 * na]
        send_sems, recv_sems, local_sems = refs[2 * na:]
        x, y, cc = _mesh_place()
        me = 4 * x + 2 * y + cc
        mine = [pltpu.make_async_copy(g_refs[a].at[me], out_refs[a].at[me], local_sems.at[a]) for a in range(na)]
        for cp in mine:
            cp.start()
        sends, lands = [], []
        for k in range(1, N_DEV):
            px = 1 - x if (k >> 2) & 1 else x
            py = 1 - y if (k >> 1) & 1 else y
            pc = 1 - cc if k & 1 else cc
            peer = 4 * px + 2 * py + pc
            for a in range(na):
                sem = (k - 1) * na + a
                sends.append(pltpu.make_async_remote_copy(
                    src_ref=g_refs[a].at[peer], dst_ref=out_refs[a].at[me], send_sem=send_sems.at[sem], recv_sem=recv_sems.at[sem],
                    device_id=(px, py, pc), device_id_type=pl.DeviceIdType.MESH))
                lands.append(pltpu.make_async_remote_copy(
                    src_ref=g_refs[a].at[peer], dst_ref=out_refs[a].at[peer], send_sem=send_sems.at[sem], recv_sem=recv_sems.at[sem],
                    device_id=(px, py, pc), device_id_type=pl.DeviceIdType.MESH))
        for cp in sends:
            cp.start()
        for cp in lands:
            cp.wait_recv()
        for cp in sends:
            cp.wait_send()
        for cp in mine:
            cp.wait()

    hbm = pl.BlockSpec(memory_space=pl.ANY)
    return pl.pallas_call(
        body, name=name, out_shape=[jax.ShapeDtypeStruct(t.shape, t.dtype) for t in blocks],
        in_specs=[hbm] * na, out_specs=[hbm] * na,
        scratch_shapes=[pltpu.SemaphoreType.DMA((7 * na,)), pltpu.SemaphoreType.DMA((7 * na,)), pltpu.SemaphoreType.DMA((na,))],
    )(*blocks)


def _peer(k):
    x, y, cc = _mesh_place()
    px = 1 - x if (k >> 2) & 1 else x
    py = 1 - y if (k >> 1) & 1 else y
    pc = 1 - cc if k & 1 else cc
    return (px, py, pc), 4 * px + 2 * py + pc


def _push_copy(src_ref, land_ref, k, a, na, send_sems, recv_sems, indexed, landed):
    x, y, cc = _mesh_place()
    place, peer = _peer(k)
    sem = (k - 1) * na + a
    return pltpu.make_async_remote_copy(
        src_ref=src_ref.at[peer] if indexed else src_ref, dst_ref=land_ref.at[peer if landed else 4 * x + 2 * y + cc],
        send_sem=send_sems.at[sem], recv_sem=recv_sems.at[sem], device_id=place, device_id_type=pl.DeviceIdType.MESH)


_HBM = pl.BlockSpec(memory_space=pltpu.HBM)
_SEM = pl.BlockSpec(memory_space=pltpu.SEMAPHORE)
_EFFECT = pltpu.SideEffectType.DATAFLOW_SIDE_EFFECTING


def _push_start(srcs, indexed, name):
    na = len(srcs)
    lands = [lax.empty(t.shape if indexed else (N_DEV,) + t.shape, t.dtype) for t in srcs]

    def body(*refs):
        src_refs, land_refs = refs[:na], refs[na:2 * na]
        send_sems, recv_sems = refs[2 * na], refs[2 * na + 1]
        token = refs[-1]
        for k in range(1, N_DEV):
            for a in range(na):
                _push_copy(src_refs[a], land_refs[a], k, a, na, send_sems, recv_sems, indexed, False).start()
        token[...] = jnp.zeros_like(token)

    sems = pltpu.SemaphoreType.DMA((7 * na,))
    out = pl.pallas_call(
        body, name=name,
        out_shape=(sems, sems, *[pltpu.HBM(t.shape, t.dtype) for t in srcs], *[pltpu.HBM(t.shape, t.dtype) for t in lands],
                   jax.ShapeDtypeStruct((8, LANES), F32)),
        in_specs=[_HBM] * (2 * na), out_specs=(_SEM, _SEM, *[_HBM] * (2 * na), pl.BlockSpec(memory_space=pltpu.VMEM)),
        input_output_aliases={i: 2 + i for i in range(2 * na)},
        compiler_params=pltpu.CompilerParams(has_side_effects=_EFFECT),
    )(*[pltpu.with_memory_space_constraint(t, pltpu.HBM) for t in srcs + lands])
    return out[0], out[1], list(out[2:2 + na]), list(out[2 + na:2 + 2 * na]), out[-1]


def _push_wait(started, indexed, after, name):
    send_sems, recv_sems, srcs, lands, _ = started
    na = len(srcs)

    def body(*refs):
        src_refs, land_refs = refs[:na], refs[na:2 * na]
        send_sems, recv_sems = refs[2 * na], refs[2 * na + 1]
        for k in range(1, N_DEV):
            for a in range(na):
                copy = _push_copy(src_refs[a], land_refs[a], k, a, na, send_sems, recv_sems, indexed, True)
                copy.wait_send()
                copy.wait_recv()

    out = pl.pallas_call(
        body, name=name, out_shape=[pltpu.HBM(t.shape, t.dtype) for t in srcs + lands],
        in_specs=[_HBM] * (2 * na) + [_SEM, _SEM, pl.BlockSpec(memory_space=pl.ANY)], out_specs=[_HBM] * (2 * na),
        input_output_aliases={i: i for i in range(2 * na)},
        compiler_params=pltpu.CompilerParams(has_side_effects=_EFFECT),
    )(*srcs, *lands, send_sems, recv_sems, after)
    x, y, cc = _mesh_place()
    me = 4 * x + 2 * y + cc
    return [lax.dynamic_update_index_in_dim(
        land, lax.dynamic_index_in_dim(src, me, 0, keepdims=False) if indexed else src, me, 0)
        for src, land in zip(out[:na], out[na:])]


def _adamw_sum(parts, w, m, v, name):
    _, r, c = parts.shape
    tr = _tile(r, (256, 128, 64, 32, 16, 8))

    def body(p_ref, w_ref, m_ref, v_ref, g_ref, d_ref, nm_ref, nv_ref):
        _adam_store(_sum_parts(p_ref), w_ref, m_ref, v_ref, g_ref, d_ref, nm_ref, nv_ref)

    row = pl.BlockSpec((tr, c), lambda i: (i, 0))
    shp = jax.ShapeDtypeStruct((r, c), F32)
    return pl.pallas_call(
        body, name=name, grid=(r // tr,), in_specs=[pl.BlockSpec((N_DEV, tr, c), lambda i: (0, i, 0)), row, row, row],
        out_specs=[row, row, row, row], out_shape=[shp, shp, shp, shp], compiler_params=_params("parallel"),
    )(parts, w, m, v)


def _sum_parts(p_ref):
    g = p_ref[0].astype(F32)
    for k in range(1, N_DEV):
        g = g + p_ref[k].astype(F32)
    return g


def _adam_store(g, w_ref, m_ref, v_ref, g_ref, d_ref, nm_ref, nv_ref):
    nm = ADAM_B1 * m_ref[...] + (1.0 - ADAM_B1) * g
    nv = ADAM_B2 * v_ref[...] + (1.0 - ADAM_B2) * (g * g)
    m_hat = nm / (1.0 - ADAM_B1 ** ADAM_STEP)
    v_hat = nv / (1.0 - ADAM_B2 ** ADAM_STEP)
    g_ref[...] = g
    nm_ref[...] = nm
    nv_ref[...] = nv
    d_ref[...] = -ADAM_LR * (m_hat / (jnp.sqrt(v_hat) + ADAM_EPS) + ADAM_WD * w_ref[...])


def _adamw_weight(parts, w, m, v, name):
    _, r, c = w.shape
    tr = _tile(r, (256, 128, 176))
    nr = r // tr

    def body(p0_ref, p1_ref, w_ref, m_ref, v_ref, g_ref, d_ref, nm_ref, nv_ref):
        g = jnp.where(pl.program_id(0) == 0, _sum_parts(p0_ref), _sum_parts(p1_ref))
        _adam_store(g, w_ref, m_ref, v_ref, g_ref, d_ref, nm_ref, nv_ref)

    part = lambda layer: pl.BlockSpec(
        (N_DEV, tr, c), lambda l, i: (0, jnp.where(l == layer, i, (nr - 1) * (1 - layer)), 0))
    row = pl.BlockSpec((None, tr, c), lambda l, i: (l, i, 0))
    shp = jax.ShapeDtypeStruct(w.shape, F32)
    return pl.pallas_call(
        body, name=name, grid=(DEPTH, nr), in_specs=[part(0), part(1), row, row, row],
        out_specs=[row, row, row, row], out_shape=[shp, shp, shp, shp], compiler_params=_params("arbitrary", "arbitrary"),
    )(parts[0], parts[1], w, m, v)


def _full_to_slots(name, t):
    k, n = t.shape
    if name in ROW_SHARDED:
        return t.reshape(N_DEV, k // N_DEV, n)
    return t.reshape(k, N_DEV, n // N_DEV).transpose(1, 0, 2)


def _slots_to_full(name, t):
    _, r, c = t.shape
    if name in ROW_SHARDED:
        return t.reshape(N_DEV * r, c)
    return t.transpose(1, 0, 2).reshape(r, N_DEV * c)


def _small_sizes(shapes, names):
    return [(n, shapes[n], -(-int(math.prod(shapes[n])) // (8 * LANES)) * 8) for n in names]


def _pack_small(tree, shapes, names, row_tile):
    rows = []
    for n, shp, nrow in _small_sizes(shapes, names):
        flat = tree[n].reshape(-1)
        rows.append(jnp.pad(flat, (0, nrow * LANES - flat.shape[0])).reshape(nrow, LANES))
    total = sum(r.shape[0] for r in rows)
    if total % row_tile:
        rows.append(jnp.zeros((-total % row_tile, LANES), F32))
    return jnp.concatenate(rows, axis=0)


def _unpack_small(buf, shapes, names):
    out, r0 = {}, 0
    for n, shp, nrow in _small_sizes(shapes, names):
        out[n] = buf[r0:r0 + nrow].reshape(-1)[:int(math.prod(shp))].reshape(shp)
        r0 += nrow
    return out


def _block_diag(w):
    g = w.shape[0]
    eye = jnp.eye(g, dtype=w.dtype)
    return (eye[:, None, :, None] * w[:, :, None, :]).reshape(g * HEAD_DIM, g * HEAD_DIM)


def kernel(x, positions, norm1, w_in, q_norm, k_norm, sinks, w_pool, pool_scale, sgu_v_norm, w_s, b_s, w_proj_a, w_proj_b, w_proj_c, w_out, norm2, w_up, conv_w, conv_b, w_down, loss_target, m_norm1, m_w_in, m_q_norm, m_k_norm, m_sinks, m_w_pool, m_pool_scale, m_sgu_v_norm, m_w_s, m_b_s, m_w_proj_a, m_w_proj_b, m_w_proj_c, m_w_out, m_norm2, m_w_up, m_conv_w, m_conv_b, m_w_down, v_norm1, v_w_in, v_q_norm, v_k_norm, v_sinks, v_w_pool, v_pool_scale, v_sgu_v_norm, v_w_s, v_b_s, v_w_proj_a, v_w_proj_b, v_w_proj_c, v_w_out, v_norm2, v_w_up, v_conv_w, v_conv_b, v_w_down):
    names = ("norm1", "w_in", "q_norm", "k_norm", "sinks", "w_pool", "pool_scale", "sgu_v_norm", "w_s", "b_s", "w_proj_a",
             "w_proj_b", "w_proj_c", "w_out", "norm2", "w_up", "conv_w", "conv_b", "w_down")
    wts = dict(zip(names, (norm1, w_in, q_norm, k_norm, sinks, w_pool, pool_scale, sgu_v_norm, w_s, b_s, w_proj_a, w_proj_b,
                           w_proj_c, w_out, norm2, w_up, conv_w, conv_b, w_down)))
    mom = dict(zip(names, (m_norm1, m_w_in, m_q_norm, m_k_norm, m_sinks, m_w_pool, m_pool_scale, m_sgu_v_norm, m_w_s, m_b_s,
                           m_w_proj_a, m_w_proj_b, m_w_proj_c, m_w_out, m_norm2, m_w_up, m_conv_w, m_conv_b, m_w_down)))
    var = dict(zip(names, (v_norm1, v_w_in, v_q_norm, v_k_norm, v_sinks, v_w_pool, v_pool_scale, v_sgu_v_norm, v_w_s, v_b_s,
                           v_w_proj_a, v_w_proj_b, v_w_proj_c, v_w_out, v_norm2, v_w_up, v_conv_w, v_conv_b, v_w_down)))
    xs = x[0]
    target = loss_target[0]
    s = xs.shape[0]

    inv_freq = ROPE_THETA ** (-jnp.arange(0, HEAD_DIM, 2, dtype=F32) / HEAD_DIM)
    ang = positions[0].astype(F32)[:, None] * inv_freq
    cosf = jnp.tile(jnp.cos(ang), (1, 4))
    sinf = jnp.tile(jnp.concatenate([-jnp.sin(ang), jnp.sin(ang)], axis=1), (1, 2))

    local = [{n: wts[n][l] if n == "conv_w" else wts[n][l].astype(BF16) for n in SHARDED} for l in range(DEPTH)]
    later = SHARDED[1:]
    full = [{"w_in": _slots_to_full("w_in", _all_gather([local[0]["w_in"]], "gather_w_in_0")[0])}, None]
    gather0 = _push_start([local[0][n] for n in later], False, "gather_rest_0_start")
    norm1_first = norm1[0] + gather0[4][0, 0]

    def layer_consts(l):
        return dict(
            bdw=_block_diag(w_pool[l]).astype(BF16), qn=jnp.tile(q_norm[l], 2).reshape(1, LANES),
            kn=jnp.tile(k_norm[l], 2).reshape(1, LANES), vn=jnp.tile(sgu_v_norm[l], 4).reshape(1, SGU_W),
            bcol=jnp.repeat(b_s[l].T, HEAD_DIM, axis=1),
            cbg=conv_b[l][:D_FF].reshape(1, D_FF), cbv=conv_b[l][D_FF:].reshape(1, D_FF))

    gate_cols, val_cols = (0, D_FF), (D_FF, D_FF)

    saved = []
    cur = xs
    for l in range(DEPTH):
        if l == 1:
            landed = _push_wait(gather1, False, cur, "gather_weights_1_wait")
            full[1] = {n: _slots_to_full(n, t) for n, t in zip(SHARDED, landed)}
        fw, k = full[l], layer_consts(l)
        h1 = _rms_fwd(cur, norm1_first if l == 0 else norm1[l], f"rms1_fwd_{l}")
        z = _mm(h1, fw["w_in"], name=f"in_proj_{l}")
        a = _pool_fwd(z, k["bdw"], pool_scale[l], f"pool_fwd_{l}")
        b = _attn_fwd(z, cosf, sinf, k["qn"], k["kn"], sinks[l], f"attn_fwd_{l}")
        c = _sgu_fwd(z, w_s[l], k["bcol"], k["vn"], f"sgu_fwd_{l}")
        w_proj_a_l = fw.get("w_proj_a")
        if l == 0:
            landed = _push_wait(gather0, False, c, "gather_rest_0_wait")
            fw.update({n: _slots_to_full(n, t) for n, t in zip(later, landed)})
            gather1 = _push_start([local[1][n] for n in SHARDED], False, "gather_weights_1_start")
            w_proj_a_l = fw["w_proj_a"] + gather1[4][0, 0].astype(BF16)
        merged = _merge_fwd(a, b, c, w_proj_a_l, fw["w_proj_b"], fw["w_proj_c"], z, f"merge_fwd_{l}")
        x1 = _mm(merged, fw["w_out"], add=cur, name=f"out_proj_{l}")
        h2 = _rms_fwd(x1, norm2[l], f"rms2_fwd_{l}")
        upg = _mm(h2, fw["w_up"], b_n=gate_cols, name=f"up_gate_{l}")
        upv = _mm(h2, fw["w_up"], b_n=val_cols, name=f"up_val_{l}")
        k["cwg"], k["cwv"] = fw["conv_w"][:, :D_FF], fw["conv_w"][:, D_FF:]
        act = _ffn_act_fwd(upg, upv, k["cwg"], k["cwv"], k["cbg"], k["cbv"], f"ffn_act_fwd_{l}")
        x2 = _mm(act, fw["w_down"], add=x1, name=f"down_proj_{l}")
        saved.append(dict(x0=cur, h1=h1, z=z, a=a, b=b, c=c, merged=merged, x1=x1, h2=h2, upg=upg, upv=upv, act=act))
        cur = x2

    dcur, loss_tile = _loss_head(cur, target)
    loss = lax.psum(loss_tile[0, 0], ("x", "y", "c"))

    gsmall = [None] * DEPTH
    small_shapes = {n: wts[n].shape for n in SMALL}

    def slots_of(grads):
        return [_full_to_slots(n, t) for n, t in grads.items()]

    for l in reversed(range(DEPTH)):
        fw, k, sv = full[l], layer_consts(l), saved[l]
        k["cwg"], k["cwv"] = fw["conv_w"][:, :D_FF], fw["conv_w"][:, D_FF:]
        staged = l == 0
        wgrad = functools.partial(_mm, ta=True, out_dtype=BF16)
        w_down_l = fw["w_down"] + exchange1[4][0, 0].astype(BF16) if staged else fw["w_down"]
        dact = _mm(dcur, w_down_l, tb=True, name=f"down_proj_bwd_{l}")
        g_down = wgrad(sv["act"], dcur, name=f"down_proj_wgrad_{l}")
        dg0, dv0, dcg, dcv = _ffn_bwd(sv["upg"], sv["upv"], k["cwg"], k["cwv"], k["cbg"], k["cbv"], dact, f"ffn_bwd_{l}")
        dh2 = _mm(dg0, fw["w_up"], tb=True, b_k=gate_cols, name=f"up_gate_bwd_{l}")
        dh2 = _mm(dv0, fw["w_up"], tb=True, b_k=val_cols, add=dh2, name=f"up_val_bwd_{l}")
        g_up = wgrad(sv["h2"], dg0, out_cols=(0, 2 * D_FF), name=f"up_gate_wgrad_{l}")
        g_up = wgrad(sv["h2"], dv0, out_cols=(D_FF, 2 * D_FF), out_into=g_up, name=f"up_val_wgrad_{l}")
        g_ffn = dict(w_up=g_up, w_down=g_down, conv_w=jnp.concatenate([dcg[0:3], dcv[0:3]], axis=1))
        norm2_l = norm2[l]
        if staged:
            parts1 = dict(zip(SHARDED, _push_wait(exchange1, True, g_up, "exchange_grads_1_wait")))
            exchange_ffn = _push_start(slots_of(g_ffn), True, "exchange_ffn_0_start")
            norm2_l = norm2_l + exchange_ffn[4][0, 0]
        dx1, g_norm2 = _rms_bwd(sv["x1"], norm2_l, dh2, dcur, f"rms2_bwd_{l}")
        dmerged = _mm(dx1, fw["w_out"], tb=True, name=f"out_proj_bwd_{l}")
        g_out = wgrad(sv["merged"], dx1, name=f"out_proj_wgrad_{l}")
        dz, dya, da = _branch_bwd(0, sv["a"], fw["w_proj_a"], sv["z"], dmerged, None, f"branch_a_bwd_{l}")
        dz, dyb, db = _branch_bwd(1, sv["b"], fw["w_proj_b"], sv["z"], dmerged, dz, f"branch_b_bwd_{l}")
        dz, dyc, dc = _branch_bwd(2, sv["c"], fw["w_proj_c"], sv["z"], dmerged, dz, f"branch_c_bwd_{l}")
        g_mix = dict(w_proj_a=wgrad(sv["a"], dya, name=f"proj_a_wgrad_{l}"), w_proj_b=wgrad(sv["b"], dyb, name=f"proj_b_wgrad_{l}"),
                     w_proj_c=wgrad(sv["c"], dyc, name=f"proj_c_wgrad_{l}"), w_out=g_out)
        pool_scale_l = pool_scale[l]
        if staged:
            exchange_mix = _push_start(slots_of(g_mix), True, "exchange_mixer_0_start")
            pool_scale_l = pool_scale_l + exchange_mix[4][0, 0]
        dxp, g_bdw, g_pscale = _pool_bwd(sv["z"], da, k["bdw"], pool_scale_l, f"pool_bwd_{l}")
        dq, dkc, dkp, dvc, dvp, g_qn, g_sink = _attn_bwd(sv["z"], cosf, sinf, k["qn"], k["kn"], sinks[l], db, f"attn_bwd_{l}")
        duv, g_ws, g_bacc, g_vn = _sgu_bwd(sv["z"], w_s[l], k["bcol"], k["vn"], dc, f"sgu_bwd_{l}")
        gsmall[l] = dict(
            q_norm=g_qn[0, :HEAD_DIM], sinks=g_sink[:, 0],
            w_pool=jnp.stack([g_bdw[g * HEAD_DIM:(g + 1) * HEAD_DIM, g * HEAD_DIM:(g + 1) * HEAD_DIM] for g in range(4)]),
            pool_scale=g_pscale[0], sgu_v_norm=g_vn[0, :HEAD_DIM], w_s=g_ws, b_s=g_bacc[:, ::HEAD_DIM].T,
            norm2=g_norm2[0], conv_b=jnp.concatenate([dcg[3], dcv[3]]))
        kn_l = k["kn"]
        if staged:
            early = _pack_small({n: jnp.stack([gsmall[i][n] for i in range(DEPTH)]) for n in SMALL_EARLY}, small_shapes,
                                SMALL_EARLY, SMALL_ROW_TILE)
            gather_early = _push_start([early], False, "gather_small_grads_start")
            kn_l = kn_l + gather_early[4][0, 0]
        dz, g_kn = _kv_post(sv["z"], cosf, sinf, kn_l, dkc, dkp, dvc, dvp, dxp, dq, duv, dz, f"kv_post_{l}")
        g_in = dict(w_in=wgrad(sv["h1"], dz, name=f"in_proj_wgrad_{l}"))
        norm1_l = norm1[l]
        if staged:
            exchange_in = _push_start(slots_of(g_in), True, "exchange_w_in_0_start")
            norm1_l = norm1_l + exchange_in[4][0, 0]
        dh1 = _mm(dz, fw["w_in"], tb=True, name=f"in_proj_bwd_{l}")
        dcur, g_norm1 = _rms_bwd(sv["x0"], norm1_l, dh1, dx1, f"rms1_bwd_{l}")
        if not staged:
            exchange1 = _push_start(slots_of({n: {**g_in, **g_mix, **g_ffn}[n] for n in SHARDED}), True, "exchange_grads_1_start")
        gsmall[l].update(norm1=g_norm1[0], k_norm=g_kn[0, :HEAD_DIM])
    grad_x = dcur[None]

    def update_small(gathered, names, row_tile, name):
        pack = lambda tree: _pack_small(tree, small_shapes, names, row_tile)
        return [_unpack_small(t, small_shapes, names) for t in _adamw_sum(gathered, pack(wts), pack(mom), pack(var), name)]

    late = _pack_small({n: jnp.stack([gsmall[i][n] for i in range(DEPTH)]) for n in SMALL_LATE}, small_shapes, SMALL_LATE, 8)
    small = update_small(_all_gather([late], "gather_late_small_grads")[0], SMALL_LATE, 8, "adamw_replicated_late")
    early_all = _push_wait(gather_early, False, small[0]["norm1"], "gather_small_grads_wait")[0]
    for kind, tree in enumerate(update_small(early_all, SMALL_EARLY, SMALL_ROW_TILE, "adamw_replicated")):
        small[kind].update(tree)
    g_s = small[0]["norm2"]

    parts0 = dict(zip(g_ffn, _push_wait(exchange_ffn, True, g_s, "exchange_ffn_0_wait")))
    parts0.update(zip(g_mix, _push_wait(exchange_mix, True, g_s, "exchange_mixer_0_wait")))
    update = lambda n: _adamw_weight([parts0[n], parts1[n]], wts[n], mom[n], var[n], f"adamw_{n}")
    big = {n: update(n) for n in SHARDED[1:]}
    parts0.update(zip(g_in, _push_wait(exchange_in, True, big["w_up"][0], "exchange_w_in_0_wait")))
    big["w_in"] = update("w_in")

    outs = [loss, grad_x]
    for kind in range(4):
        outs += [small[kind][n] if n in SMALL else big[n][kind] for n in names]
    return tuple(outs)
```

```python
import functools
import math

import jax
import jax.numpy as jnp
import numpy as np
from jax import lax
from jax.experimental import pallas as pl
from jax.experimental.pallas import tpu as pltpu

F32 = jnp.float32
BF16 = jnp.bfloat16

D_MODEL = 1024
DEPTH = 2
HEAD_DIM = 64
N_Q_HEADS = 8
Q_PER_KV = 4
BLOCK = 128
POOL_W = 256
ATTN_W = 512
KV_W = 128
SGU_W = 256
IN_COLS = 4608
GATE_COL0 = 1536
D_FF = 2816
EPS = 1e-6
ROPE_THETA = 10000.0
N_DEV = 8
LANES = 128
HALO_POOL = 16
HALO_CONV = 8

ADAM_LR = 0.001
ADAM_B1 = 0.9
ADAM_B2 = 0.999
ADAM_EPS = 1e-08
ADAM_WD = 0.01
ADAM_STEP = 10

VMEM_LIMIT = 48 * 1024 * 1024
MM_VMEM_BUDGET = 40 * 1024 * 1024

SHARDED = ("w_in", "w_proj_a", "w_proj_b", "w_proj_c", "w_out", "w_up", "w_down", "conv_w")
ROW_SHARDED = ("w_out", "w_down")
SMALL_ROW_TILE = 256
SMALL = ("norm1", "q_norm", "k_norm", "sinks", "w_pool", "pool_scale", "sgu_v_norm", "w_s", "b_s", "norm2", "conv_b")
SMALL_LATE = ("norm1", "k_norm")
SMALL_EARLY = tuple(n for n in SMALL if n not in SMALL_LATE)

_GELU_C = math.sqrt(2.0 / math.pi)
_GELU_A = 0.044715


def _params(*sem):
    return pltpu.CompilerParams(dimension_semantics=sem, vmem_limit_bytes=VMEM_LIMIT)


def _tile(n, prefs):
    for t in prefs:
        if t <= n and n % t == 0:
            return t
    return n


def _head_mean_matrix():
    r = lax.broadcasted_iota(jnp.int32, (LANES, LANES), 0)
    c = lax.broadcasted_iota(jnp.int32, (LANES, LANES), 1)
    return jnp.where((r >= HEAD_DIM) == (c >= HEAD_DIM), 1.0 / HEAD_DIM, 0.0).astype(BF16)


def _head_mean(v, bd):
    hi = v.astype(BF16)
    rest = v - hi.astype(F32)
    mid = rest.astype(BF16)
    lo = (rest - mid.astype(F32)).astype(BF16)
    mm = lambda p: jnp.dot(p, bd, preferred_element_type=F32)
    return mm(hi) + (mm(mid) + mm(lo))


def _rot_half(t):
    lane = lax.broadcasted_iota(jnp.int32, t.shape, 1)
    return jnp.where((lane & 32) == 0, pltpu.roll(t, LANES - 32, 1), pltpu.roll(t, 32, 1))


def _norm_rope(t, gn, cosf, sinf, bd):
    r = lax.rsqrt(_head_mean(t * t, bd) + EPS)
    n = t * r
    y = n * gn
    return y * cosf + _rot_half(y) * sinf, n, r


def _norm_rope_bwd(d, t, n, r, gn, cosf, sinf, bd):
    dy = d * cosf + _rot_half(d * sinf)
    dgn = jnp.sum(dy * n, axis=0, keepdims=True)
    u = dy * gn
    dt = r * u - t * (r * r * r) * _head_mean(t * u, bd)
    return dt, dgn


def _gelu(x):
    t = jnp.tanh(_GELU_C * (x + _GELU_A * (x * x * x)))
    return 0.5 * x * (1.0 + t), t


def _gelu_grad(x, t):
    return 0.5 * (1.0 + t) + 0.5 * x * (1.0 - t * t) * (_GELU_C * (1.0 + 3.0 * _GELU_A * x * x))


def _sigmoid(x):
    return jax.nn.sigmoid(x)


def _dot(a, b, ca=1, cb=0):
    return lax.dot_general(a.astype(BF16), b.astype(BF16), (((ca,), (cb,)), ((), ())), preferred_element_type=F32)


def _mm(a, b, *, ta=False, tb=False, add=None, out_dtype=F32, name, b_n=None, b_k=None, out_cols=None, out_into=None):
    m = a.shape[1] if ta else a.shape[0]
    k = a.shape[0] if ta else a.shape[1]
    n = b_n[1] if b_n else (b.shape[0] if tb else b.shape[1])
    tn = _tile(n, (1024, 1152, 1408, 512, 256, 128))
    has_add = add is not None
    fits = []
    for tm in (2048, 1024, 1408, 512, 256, 128):
        for tk in (k, 4608, 2816, 2048, 1408, 1152, 1024, 512, 256, 128):
            if tm <= m and m % tm == 0 and tk <= k and k % tk == 0:
                need = (2 * (tm * tk * a.dtype.itemsize + tk * tn * b.dtype.itemsize) + 2 * tm * tn * jnp.dtype(out_dtype).itemsize
                        + 2 * tm * tn * 4 * has_add + tm * tn * 4 * (tk < k))
                if need <= MM_VMEM_BUDGET:
                    fits.append((k // tk, -tm, tm, tk))
    if fits:
        _, _, tm, tk = min(fits)
    else:
        tm, tk = _tile(m, (256, 128)), _tile(k, (512, 256, 128))
    nk = k // tk
    n0 = b_n[0] // tn if b_n else 0
    k0 = b_k[0] // tk if b_k else 0
    o0, n_out = (out_cols[0] // tn, out_cols[1]) if out_cols else (0, n)
    n_in = 2 + has_add + (out_into is not None)

    def body(*refs):
        a_ref, b_ref = refs[0], refs[1]
        add_ref = refs[2] if has_add else None
        o_ref = refs[n_in]
        def finish(r):
            if has_add:
                r = r + add_ref[...]
            o_ref[...] = r.astype(out_dtype)

        if nk == 1:
            finish(_dot(a_ref[...], b_ref[...], 0 if ta else 1, 1 if tb else 0))
        else:
            acc_ref = refs[-1]
            kk = pl.program_id(2)

            @pl.when(kk == 0)
            def _():
                acc_ref[...] = jnp.zeros_like(acc_ref)

            acc_ref[...] += _dot(a_ref[...], b_ref[...], 0 if ta else 1, 1 if tb else 0)

            @pl.when(kk == nk - 1)
            def _():
                finish(acc_ref[...])

    a_spec = pl.BlockSpec((tk, tm), lambda i, j, kk: (kk, i)) if ta else pl.BlockSpec((tm, tk), lambda i, j, kk: (i, kk))
    if tb:
        b_spec = pl.BlockSpec((tn, tk), lambda i, j, kk: (j + n0, kk + k0))
    else:
        b_spec = pl.BlockSpec((tk, tn), lambda i, j, kk: (kk + k0, j + n0))
    in_specs = [a_spec, b_spec] + ([pl.BlockSpec((tm, tn), lambda i, j, kk: (i, j))] if has_add else [])
    args = (a, b) + ((add,) if has_add else ())
    if out_into is not None:
        in_specs.append(pl.BlockSpec(memory_space=pl.ANY))
        args += (out_into,)
    return pl.pallas_call(
        body, name=name, grid=(m // tm, n // tn, nk), in_specs=in_specs,
        out_specs=pl.BlockSpec((tm, tn), lambda i, j, kk: (i, j + o0)),
        out_shape=jax.ShapeDtypeStruct((m, n_out), out_dtype),
        scratch_shapes=[pltpu.VMEM((tm, tn), F32)] if nk > 1 else [],
        input_output_aliases={n_in - 1: 0} if out_into is not None else {},
        compiler_params=_params("parallel", "parallel", "arbitrary"),
    )(*args)


def _rms_fwd(x, g, name):
    s, d = x.shape
    tr = _tile(s, (512, 256, 128))

    def body(x_ref, g_ref, h_ref):
        xv = x_ref[...]
        r = lax.rsqrt(jnp.mean(xv * xv, axis=-1, keepdims=True) + EPS)
        h_ref[...] = ((xv * r) * g_ref[...]).astype(BF16)

    return pl.pallas_call(
        body, name=name, grid=(s // tr,),
        in_specs=[pl.BlockSpec((tr, d), lambda i: (i, 0)), pl.BlockSpec((1, d), lambda i: (0, 0))],
        out_specs=pl.BlockSpec((tr, d), lambda i: (i, 0)),
        out_shape=jax.ShapeDtypeStruct((s, d), BF16), compiler_params=_params("parallel"),
    )(x, g.reshape(1, d))


def _rms_bwd(x, g, dh, dres, name):
    s, d = x.shape
    tr = _tile(s, (512, 256, 128))

    def body(x_ref, g_ref, dh_ref, dres_ref, dx_ref, dg_ref):
        xv = x_ref[...]
        r = lax.rsqrt(jnp.mean(xv * xv, axis=-1, keepdims=True) + EPS)
        dhv = dh_ref[...]
        u = dhv * g_ref[...]
        dx_ref[...] = dres_ref[...] + (r * u - xv * (r * r * r) * jnp.mean(xv * u, axis=-1, keepdims=True))
        part = jnp.sum(dhv * (xv * r), axis=0, keepdims=True)

        @pl.when(pl.program_id(0) == 0)
        def _():
            dg_ref[...] = part

        @pl.when(pl.program_id(0) > 0)
        def _():
            dg_ref[...] += part

    row = pl.BlockSpec((tr, d), lambda i: (i, 0))
    vec = pl.BlockSpec((1, d), lambda i: (0, 0))
    return pl.pallas_call(
        body, name=name, grid=(s // tr,), in_specs=[row, vec, row, row], out_specs=[row, vec],
        out_shape=[jax.ShapeDtypeStruct((s, d), F32), jax.ShapeDtypeStruct((1, d), F32)],
        compiler_params=_params("arbitrary"),
    )(x, g.reshape(1, d), dh, dres)


def _loss_head(y, target):
    s, d = y.shape
    tr = _tile(s, (512, 256, 128))

    def body(y_ref, t_ref, dy_ref, l_ref):
        err = y_ref[...] - t_ref[...]
        dy_ref[...] = err * (1.0 / d)
        part = jnp.sum(jnp.sum(err * err, axis=-1, keepdims=True) * (1.0 / d), axis=0, keepdims=True) * 0.5
        part = jnp.broadcast_to(part, (8, LANES))

        @pl.when(pl.program_id(0) == 0)
        def _():
            l_ref[...] = part

        @pl.when(pl.program_id(0) > 0)
        def _():
            l_ref[...] += part

    row = pl.BlockSpec((tr, d), lambda i: (i, 0))
    acc = pl.BlockSpec((8, LANES), lambda i: (0, 0))
    return pl.pallas_call(
        body, name="loss_head", grid=(s // tr,), in_specs=[row, row], out_specs=[row, acc],
        out_shape=[jax.ShapeDtypeStruct((s, d), F32), jax.ShapeDtypeStruct((8, LANES), F32)],
        compiler_params=_params("arbitrary"),
    )(y, target)


def _pool_lane_select(lane, v2, v4, v8, v16):
    return jnp.where(lane < 64, v2, jnp.where(lane < 128, v4, jnp.where(lane < 192, v8, v16)))


def _pool_diff(xc, xp, row0):
    n = BLOCK + HALO_POOL
    cat = jnp.concatenate([xp, xc], axis=0)
    s2 = cat + pltpu.roll(cat, 1, 0)
    s4 = s2 + pltpu.roll(s2, 2, 0)
    s8 = s4 + pltpu.roll(s4, 4, 0)
    s16 = s8 + pltpu.roll(s8, 8, 0)
    lane = lax.broadcasted_iota(jnp.int32, (n, POOL_W), 1)
    wsum = _pool_lane_select(lane, s2, s4, s8, s16)[HALO_POOL:]
    return wsum / _pool_count(row0, BLOCK) - xc


def _pool_count(row0, rows):
    lane = lax.broadcasted_iota(jnp.int32, (rows, POOL_W), 1)
    t = lax.broadcasted_iota(jnp.int32, (rows, POOL_W), 0) + row0
    return jnp.minimum(t + 1, _pool_lane_select(lane, 2, 4, 8, 16)).astype(F32)


def _pool_fwd(z, bdw, scale, name):
    s = z.shape[0]
    nb = s // BLOCK

    def body(xc_ref, xp_ref, w_ref, sc_ref, a_ref):
        i = pl.program_id(0)
        xp = jnp.where(i > 0, xp_ref[...], 0.0)
        diff = _pool_diff(xc_ref[...], xp, i * BLOCK)
        a_ref[...] = (_dot(diff, w_ref[...]) * sc_ref[...]).astype(BF16)

    return pl.pallas_call(
        body, name=name, grid=(nb,),
        in_specs=[pl.BlockSpec((BLOCK, POOL_W), lambda i: (i, 0)),
                  pl.BlockSpec((HALO_POOL, POOL_W), lambda i: (jnp.maximum(i * (BLOCK // HALO_POOL) - 1, 0), 0)),
                  pl.BlockSpec((POOL_W, POOL_W), lambda i: (0, 0)),
                  pl.BlockSpec((1, POOL_W), lambda i: (0, 0))],
        out_specs=pl.BlockSpec((BLOCK, POOL_W), lambda i: (i, 0)),
        out_shape=jax.ShapeDtypeStruct((s, POOL_W), BF16), compiler_params=_params("parallel"),
    )(z, z, bdw, scale.reshape(1, POOL_W))


def _pool_bwd(z, da, bdw, scale, name):
    s = z.shape[0]
    nb = s // BLOCK
    per = BLOCK // HALO_POOL
    n = BLOCK + HALO_POOL

    def body(xc_ref, xp_ref, dac_ref, dan_ref, w_ref, sc_ref, dx_ref, dw_ref, dsc_ref):
        i = pl.program_id(0)
        xp = jnp.where(i > 0, xp_ref[...], 0.0)
        diff = _pool_diff(xc_ref[...], xp, i * BLOCK)
        mixed = _dot(diff, w_ref[...])
        dac = dac_ref[...]
        dan = jnp.where(i < nb - 1, dan_ref[...], 0.0)
        dmix = jnp.concatenate([dac, dan], axis=0) * sc_ref[...]
        ddiff = _dot(dmix, w_ref[...], 1, 1)
        e = ddiff / _pool_count(i * BLOCK, n)
        f2 = e + pltpu.roll(e, n - 1, 0)
        f4 = f2 + pltpu.roll(f2, n - 2, 0)
        f8 = f4 + pltpu.roll(f4, n - 4, 0)
        f16 = f8 + pltpu.roll(f8, n - 8, 0)
        lane = lax.broadcasted_iota(jnp.int32, (n, POOL_W), 1)
        back = _pool_lane_select(lane, f2, f4, f8, f16)
        dx_ref[...] = (back[:BLOCK] - ddiff[:BLOCK]).astype(BF16)
        dw = _dot(diff, dmix[:BLOCK], 0, 0)
        dsc = jnp.sum(dac * mixed, axis=0, keepdims=True)

        @pl.when(i == 0)
        def _():
            dw_ref[...] = dw
            dsc_ref[...] = dsc

        @pl.when(i > 0)
        def _():
            dw_ref[...] += dw
            dsc_ref[...] += dsc

    blk = pl.BlockSpec((BLOCK, POOL_W), lambda i: (i, 0))
    return pl.pallas_call(
        body, name=name, grid=(nb,),
        in_specs=[blk, pl.BlockSpec((HALO_POOL, POOL_W), lambda i: (jnp.maximum(i * per - 1, 0), 0)),
                  blk, pl.BlockSpec((HALO_POOL, POOL_W), lambda i: (jnp.minimum((i + 1) * per, nb * per - 1), 0)),
                  pl.BlockSpec((POOL_W, POOL_W), lambda i: (0, 0)), pl.BlockSpec((1, POOL_W), lambda i: (0, 0))],
        out_specs=[blk, pl.BlockSpec((POOL_W, POOL_W), lambda i: (0, 0)), pl.BlockSpec((1, POOL_W), lambda i: (0, 0))],
        out_shape=[jax.ShapeDtypeStruct((s, POOL_W), BF16), jax.ShapeDtypeStruct((POOL_W, POOL_W), F32),
                   jax.ShapeDtypeStruct((1, POOL_W), F32)],
        compiler_params=_params("arbitrary"),
    )(z, z, da, da, bdw, scale.reshape(1, POOL_W))


def _attn_setup(zc_ref, zp_ref, cc_ref, cp_ref, sc_ref, sp_ref, qn_ref, kn_ref, bd):
    q = []
    for j in range(ATTN_W // LANES):
        t = zc_ref[:, POOL_W + j * LANES:POOL_W + (j + 1) * LANES]
        q.append((t,) + _norm_rope(t, qn_ref[...], cc_ref[...], sc_ref[...], bd))
    kc_raw = zc_ref[:, POOL_W + ATTN_W:POOL_W + ATTN_W + KV_W]
    kc = _norm_rope(kc_raw, kn_ref[...], cc_ref[...], sc_ref[...], bd)[0]
    kp = _norm_rope(zp_ref[:, :KV_W], kn_ref[...], cp_ref[...], sp_ref[...], bd)[0]
    kband = jnp.concatenate([kp, kc], axis=0).astype(BF16)
    vband = jnp.concatenate([zp_ref[:, KV_W:], zc_ref[:, POOL_W + ATTN_W + KV_W:POOL_W + ATTN_W + 2 * KV_W]], axis=0).astype(BF16)
    return q, kband, vband


MASKED = -1e30


def _window_bias(first_block):
    row = np.arange(Q_PER_KV * BLOCK)[:, None] % BLOCK
    col = np.arange(2 * BLOCK)[None, :]
    dist = row + BLOCK - col
    inside = (dist >= 0) & (dist < BLOCK) & ((col >= BLOCK) | (not first_block))
    return jnp.asarray(np.where(inside, 0.0, MASKED), F32)


def _stack_heads(tiles, kvh):
    return jnp.concatenate([_to_kv_lanes(tiles[h // 2], h) for h in range(kvh * Q_PER_KV, (kvh + 1) * Q_PER_KV)], axis=0)


def _unstack_heads(stacked, kvh, tiles):
    for g in range(Q_PER_KV):
        h = kvh * Q_PER_KV + g
        t = _from_kv_lanes(stacked[g * BLOCK:(g + 1) * BLOCK], h)
        tiles[h // 2] = t if tiles[h // 2] is None else tiles[h // 2] + t


def _sink_column(sink_ref, kvh):
    grp = lax.broadcasted_iota(jnp.int32, (Q_PER_KV * BLOCK, 1), 0) >> 7
    s = [sink_ref[kvh * Q_PER_KV + g] for g in range(Q_PER_KV)]
    return jnp.where(grp == 0, s[0], jnp.where(grp == 1, s[1], jnp.where(grp == 2, s[2], s[3])))


def _to_kv_lanes(t, h):
    kvh = h // Q_PER_KV
    if (h % 2) != kvh:
        t = pltpu.roll(t, HEAD_DIM, 1)
    lane = lax.broadcasted_iota(jnp.int32, t.shape, 1)
    return jnp.where((lane >= HEAD_DIM) == (kvh == 1), t, 0.0)


def _from_kv_lanes(t, h):
    kvh = h // Q_PER_KV
    lane = lax.broadcasted_iota(jnp.int32, t.shape, 1)
    t = jnp.where((lane >= HEAD_DIM) == (kvh == 1), t, 0.0)
    if (h % 2) != kvh:
        t = pltpu.roll(t, HEAD_DIM, 1)
    return t


def _attn_probs(qh, kband, bias, sink):
    sc = _dot(qh, kband, 1, 1) * (HEAD_DIM ** -0.5) + bias
    m = jnp.maximum(jnp.max(sc, axis=1, keepdims=True), sink)
    p = jnp.exp(sc - m)
    psink = jnp.exp(sink - m)
    den = jnp.sum(p, axis=1, keepdims=True) + psink
    return p / den, psink / den


def _attn_specs(nb):
    cur = lambda i: (i, 0)
    prev = lambda i: (jnp.maximum(i - 1, 0), 0)
    tab = lambda f: pl.BlockSpec((BLOCK, LANES), f)
    vec = pl.BlockSpec((1, LANES), lambda i: (0, 0))
    return [pl.BlockSpec((BLOCK, 1024), cur),
            pl.BlockSpec((BLOCK, 2 * KV_W), lambda i: (jnp.maximum(i - 1, 0), 3)),
            tab(cur), tab(prev), tab(cur), tab(prev), vec, vec,
            pl.BlockSpec(memory_space=pltpu.SMEM)] + [pl.BlockSpec((Q_PER_KV * BLOCK, 2 * BLOCK), lambda i: (0, 0))] * 2


def _attn_fwd(z, cosf, sinf, qn, kn, sinks, name):
    s = z.shape[0]
    nb = s // BLOCK

    def body(zc_ref, zp_ref, cc_ref, cp_ref, sc_ref, sp_ref, qn_ref, kn_ref, sink_ref, bias_ref, bias0_ref, o_ref):
        i = pl.program_id(0)
        bd = _head_mean_matrix()
        q, kband, vband = _attn_setup(zc_ref, zp_ref, cc_ref, cp_ref, sc_ref, sp_ref, qn_ref, kn_ref, bd)
        mask = jnp.where(i > 0, bias_ref[...], bias0_ref[...])
        out = [None] * (ATTN_W // LANES)
        for kvh in range(N_Q_HEADS // Q_PER_KV):
            qs = _stack_heads([t[1] for t in q], kvh)
            probs, _ = _attn_probs(qs, kband, mask, _sink_column(sink_ref, kvh))
            _unstack_heads(_dot(probs, vband), kvh, out)
        for j, o in enumerate(out):
            o_ref[:, j * LANES:(j + 1) * LANES] = o.astype(BF16)

    return pl.pallas_call(
        body, name=name, grid=(nb,), in_specs=_attn_specs(nb),
        out_specs=pl.BlockSpec((BLOCK, ATTN_W), lambda i: (i, 0)),
        out_shape=jax.ShapeDtypeStruct((s, ATTN_W), BF16), compiler_params=_params("parallel"),
    )(z, z, cosf, cosf, sinf, sinf, qn, kn, sinks, _window_bias(False), _window_bias(True))


def _attn_bwd(z, cosf, sinf, qn, kn, sinks, d_out, name):
    s = z.shape[0]
    nb = s // BLOCK
    nt = ATTN_W // LANES

    def body(zc_ref, zp_ref, cc_ref, cp_ref, sc_ref, sp_ref, qn_ref, kn_ref, sink_ref, bias_ref, bias0_ref, do_ref,
             dq_ref, dkc_ref, dkp_ref, dvc_ref, dvp_ref, dqn_ref, dsink_ref):
        i = pl.program_id(0)
        bd = _head_mean_matrix()
        q, kband, vband = _attn_setup(zc_ref, zp_ref, cc_ref, cp_ref, sc_ref, sp_ref, qn_ref, kn_ref, bd)
        mask = jnp.where(i > 0, bias_ref[...], bias0_ref[...])

        @pl.when(i == 0)
        def _():
            dqn_ref[...] = jnp.zeros_like(dqn_ref)
            dsink_ref[...] = jnp.zeros_like(dsink_ref)

        dq = [None] * nt
        dk = jnp.zeros((2 * BLOCK, KV_W), F32)
        dv = jnp.zeros((2 * BLOCK, KV_W), F32)
        d_tiles = [do_ref[:, j * LANES:(j + 1) * LANES] for j in range(nt)]
        for kvh in range(N_Q_HEADS // Q_PER_KV):
            qs = _stack_heads([t[1] for t in q], kvh)
            probs, psink = _attn_probs(qs, kband, mask, _sink_column(sink_ref, kvh))
            dos = _stack_heads(d_tiles, kvh)
            dp = _dot(dos, vband, 1, 1)
            delta = jnp.sum(dp * probs, axis=1, keepdims=True)
            ds = (probs * (dp - delta)) * (HEAD_DIM ** -0.5)
            dsink = -psink * delta
            for g in range(Q_PER_KV):
                h = kvh * Q_PER_KV + g
                dsink_ref[h:h + 1, :] += jnp.broadcast_to(jnp.sum(dsink[g * BLOCK:(g + 1) * BLOCK], axis=0, keepdims=True), (1, LANES))
            _unstack_heads(_dot(ds, kband), kvh, dq)
            dk = dk + _dot(ds, qs, 0, 0)
            dv = dv + _dot(probs, dos, 0, 0)
        dgn = jnp.zeros((1, LANES), F32)
        for j in range(nt):
            t, _, n, r = q[j]
            dt, g = _norm_rope_bwd(dq[j], t, n, r, qn_ref[...], cc_ref[...], sc_ref[...], bd)
            dq_ref[:, j * LANES:(j + 1) * LANES] = dt.astype(BF16)
            dgn = dgn + g
        dqn_ref[...] += jnp.broadcast_to(dgn, (8, LANES))
        dkp_ref[...] = dk[:BLOCK]
        dkc_ref[...] = dk[BLOCK:]
        dvp_ref[...] = dv[:BLOCK]
        dvc_ref[...] = dv[BLOCK:]

        @pl.when(i == nb - 1)
        def _():
            acc = dqn_ref[...]
            dqn_ref[...] = acc + pltpu.roll(acc, HEAD_DIM, 1)

    blk = lambda w: pl.BlockSpec((BLOCK, w), lambda i: (i, 0))
    acc = pl.BlockSpec((8, LANES), lambda i: (0, 0))
    kv = jax.ShapeDtypeStruct((s, KV_W), F32)
    return pl.pallas_call(
        body, name=name, grid=(nb,), in_specs=_attn_specs(nb) + [blk(ATTN_W)],
        out_specs=[blk(ATTN_W), blk(KV_W), blk(KV_W), blk(KV_W), blk(KV_W), acc, acc],
        out_shape=[jax.ShapeDtypeStruct((s, ATTN_W), BF16), kv, kv, kv, kv,
                   jax.ShapeDtypeStruct((8, LANES), F32), jax.ShapeDtypeStruct((8, LANES), F32)],
        compiler_params=_params("arbitrary"),
    )(z, z, cosf, cosf, sinf, sinf, qn, kn, sinks, _window_bias(False), _window_bias(True), d_out)


def _kv_post(z, cosf, sinf, kn, dkc, dkp, dvc, dvp, dxp, dq, duv, dz, name):
    s = z.shape[0]
    nb = s // BLOCK

    def body(zk_ref, c_ref, s_ref, kn_ref, dkc_ref, dkp_ref, dvc_ref, dvp_ref, dxp_ref, dq_ref, duv_ref, dz_in,
             dz_ref, dkn_ref):
        j = pl.program_id(0)
        bd = _head_mean_matrix()
        last = j == nb - 1
        d = dkc_ref[...] + jnp.where(last, 0.0, dkp_ref[...])
        t = zk_ref[:, :KV_W]
        _, n, r = _norm_rope(t, kn_ref[...], c_ref[...], s_ref[...], bd)
        dt, g = _norm_rope_bwd(d, t, n, r, kn_ref[...], c_ref[...], s_ref[...], bd)
        dvv = dvc_ref[...] + jnp.where(last, 0.0, dvp_ref[...])
        dz_ref[:, 0:POOL_W] = dxp_ref[...]
        dz_ref[:, POOL_W:POOL_W + ATTN_W] = dq_ref[...]
        dz_ref[:, POOL_W + ATTN_W:POOL_W + ATTN_W + KV_W] = dt.astype(BF16)
        dz_ref[:, POOL_W + ATTN_W + KV_W:POOL_W + ATTN_W + 2 * KV_W] = dvv.astype(BF16)
        dz_ref[:, POOL_W + ATTN_W + 2 * KV_W:GATE_COL0] = duv_ref[...]

        @pl.when(j == 0)
        def _():
            dkn_ref[...] = jnp.zeros_like(dkn_ref)

        dkn_ref[...] += jnp.broadcast_to(g, (8, LANES))

        @pl.when(last)
        def _():
            acc = dkn_ref[...]
            dkn_ref[...] = acc + pltpu.roll(acc, HEAD_DIM, 1)

    cur = lambda w: pl.BlockSpec((BLOCK, w), lambda j: (j, 0))
    nxt = pl.BlockSpec((BLOCK, KV_W), lambda j: (jnp.minimum(j + 1, nb - 1), 0))
    vec = pl.BlockSpec((1, LANES), lambda j: (0, 0))
    return pl.pallas_call(
        body, name=name, grid=(nb,),
        in_specs=[pl.BlockSpec((BLOCK, 2 * KV_W), lambda j: (j, 3)), cur(LANES), cur(LANES), vec,
                  cur(KV_W), nxt, cur(KV_W), nxt, cur(POOL_W), cur(ATTN_W), cur(2 * SGU_W),
                  pl.BlockSpec(memory_space=pl.ANY)],
        out_specs=[pl.BlockSpec((BLOCK, GATE_COL0), lambda j: (j, 0)), pl.BlockSpec((8, LANES), lambda j: (0, 0))],
        out_shape=[jax.ShapeDtypeStruct(dz.shape, dz.dtype), jax.ShapeDtypeStruct((8, LANES), F32)],
        input_output_aliases={11: 0}, compiler_params=_params("arbitrary"),
    )(z, cosf, sinf, kn, dkc, dkp, dvc, dvp, dxp, dq, duv, dz)


def _sgu_setup(z_ref, ws_ref, vn_ref, bd):
    us = z_ref[:, :SGU_W]
    vs = z_ref[:, SGU_W:]
    ug, tu = _gelu(us)
    gv, tv = _gelu(vs)
    rr = jnp.concatenate([lax.rsqrt(_head_mean(gv[:, k * LANES:(k + 1) * LANES] ** 2, bd) + EPS) for k in range(2)], axis=1)
    vg = (gv * rr) * vn_ref[...]
    tril = lax.broadcasted_iota(jnp.int32, (BLOCK, BLOCK), 0) >= lax.broadcasted_iota(jnp.int32, (BLOCK, BLOCK), 1)
    w = [jnp.where(tril, ws_ref[g], 0.0).astype(BF16) for g in range(4)]
    return us, vs, ug, tu, gv, tv, rr, vg, w, tril


def _group_select(parts):
    lane = lax.broadcasted_iota(jnp.int32, parts[0].shape, 1)
    return _pool_lane_select(lane, *parts)


def _sgu_fwd(z, ws, bcol, vn, name):
    s = z.shape[0]
    nb = s // BLOCK

    def body(z_ref, ws_ref, b_ref, vn_ref, c_ref):
        bd = _head_mean_matrix()
        _, _, ug, _, _, _, _, vg, w, _ = _sgu_setup(z_ref, ws_ref, vn_ref, bd)
        sg = _group_select([_dot(w[g], vg) for g in range(4)]) + b_ref[...]
        c_ref[...] = (ug * sg).astype(BF16)

    return pl.pallas_call(
        body, name=name, grid=(nb,),
        in_specs=[pl.BlockSpec((BLOCK, 2 * SGU_W), lambda i: (i, 2)), pl.BlockSpec((4, BLOCK, BLOCK), lambda i: (0, 0, 0)),
                  pl.BlockSpec((BLOCK, SGU_W), lambda i: (0, 0)), pl.BlockSpec((1, SGU_W), lambda i: (0, 0))],
        out_specs=pl.BlockSpec((BLOCK, SGU_W), lambda i: (i, 0)),
        out_shape=jax.ShapeDtypeStruct((s, SGU_W), BF16), compiler_params=_params("parallel"),
    )(z, ws, bcol, vn)


def _sgu_bwd(z, ws, bcol, vn, dc, name):
    s = z.shape[0]
    nb = s // BLOCK

    def body(z_ref, ws_ref, b_ref, vn_ref, dc_ref, duv_ref, dws_ref, db_ref, dvn_ref):
        i = pl.program_id(0)
        bd = _head_mean_matrix()
        us, vs, ug, tu, gv, tv, rr, vg, w, tril = _sgu_setup(z_ref, ws_ref, vn_ref, bd)
        sg = _group_select([_dot(w[g], vg) for g in range(4)]) + b_ref[...]
        dcv = dc_ref[...]
        dug = dcv * sg
        dsg = dcv * ug
        lane = lax.broadcasted_iota(jnp.int32, dsg.shape, 1)

        @pl.when(i == 0)
        def _():
            dws_ref[...] = jnp.zeros_like(dws_ref)
            db_ref[...] = jnp.zeros_like(db_ref)
            dvn_ref[...] = jnp.zeros_like(dvn_ref)

        for g in range(4):
            dsg_g = jnp.where((lane >= g * HEAD_DIM) & (lane < (g + 1) * HEAD_DIM), dsg, 0.0)
            dws_ref[g] += jnp.where(tril, _dot(dsg_g, vg, 1, 1), 0.0)
        dvg = _group_select([_dot(w[g], dsg, 0, 0) for g in range(4)])
        db_ref[...] += dsg
        n = gv * rr
        part = jnp.sum(dvg * n, axis=0, keepdims=True)
        dvn_ref[...] += jnp.broadcast_to(part[:, :LANES] + part[:, LANES:], (8, LANES))
        u = dvg * vn_ref[...]
        tu_ = gv * u
        mean = jnp.concatenate([_head_mean(tu_[:, k * LANES:(k + 1) * LANES], bd) for k in range(2)], axis=1)
        dgv = rr * u - gv * (rr * rr * rr) * mean
        duv_ref[:, :SGU_W] = (dug * _gelu_grad(us, tu)).astype(BF16)
        duv_ref[:, SGU_W:] = (dgv * _gelu_grad(vs, tv)).astype(BF16)

        @pl.when(i == nb - 1)
        def _():
            acc = dvn_ref[...]
            dvn_ref[...] = acc + pltpu.roll(acc, HEAD_DIM, 1)
            for k in range(2):
                db_ref[:, k * LANES:(k + 1) * LANES] = _head_mean(db_ref[:, k * LANES:(k + 1) * LANES], bd) * float(HEAD_DIM)

    return pl.pallas_call(
        body, name=name, grid=(nb,),
        in_specs=[pl.BlockSpec((BLOCK, 2 * SGU_W), lambda i: (i, 2)), pl.BlockSpec((4, BLOCK, BLOCK), lambda i: (0, 0, 0)),
                  pl.BlockSpec((BLOCK, SGU_W), lambda i: (0, 0)), pl.BlockSpec((1, SGU_W), lambda i: (0, 0)),
                  pl.BlockSpec((BLOCK, SGU_W), lambda i: (i, 0))],
        out_specs=[pl.BlockSpec((BLOCK, 2 * SGU_W), lambda i: (i, 0)), pl.BlockSpec((4, BLOCK, BLOCK), lambda i: (0, 0, 0)),
                   pl.BlockSpec((BLOCK, SGU_W), lambda i: (0, 0)), pl.BlockSpec((8, LANES), lambda i: (0, 0))],
        out_shape=[jax.ShapeDtypeStruct((s, 2 * SGU_W), BF16), jax.ShapeDtypeStruct((4, BLOCK, BLOCK), F32),
                   jax.ShapeDtypeStruct((BLOCK, SGU_W), F32), jax.ShapeDtypeStruct((8, LANES), F32)],
        compiler_params=_params("arbitrary"),
    )(z, ws, bcol, vn, dc)


MERGE_TN = 512
MERGE_TM = 1024


def _merge_fwd(a, b, c, wpa, wpb, wpc, z, name):
    s = z.shape[0]
    tm = _tile(s, (MERGE_TM, BLOCK))
    gate0 = GATE_COL0 // MERGE_TN

    def body(a_ref, b_ref, c_ref, wa_ref, wb_ref, wc_ref, g0_ref, g1_ref, g2_ref, o_ref):
        r = _sigmoid(g0_ref[...]) * _dot(a_ref[...], wa_ref[...])
        r = r + _sigmoid(g1_ref[...]) * _dot(b_ref[...], wb_ref[...])
        r = r + _sigmoid(g2_ref[...]) * _dot(c_ref[...], wc_ref[...])
        o_ref[...] = r.astype(BF16)

    x_spec = lambda w: pl.BlockSpec((tm, w), lambda i, n: (i, 0))
    w_spec = lambda w: pl.BlockSpec((w, MERGE_TN), lambda i, n: (0, n))
    g_spec = lambda br: pl.BlockSpec((tm, MERGE_TN), lambda i, n: (i, gate0 + 2 * br + n))
    return pl.pallas_call(
        body, name=name, grid=(s // tm, D_MODEL // MERGE_TN),
        in_specs=[x_spec(POOL_W), x_spec(ATTN_W), x_spec(SGU_W), w_spec(POOL_W), w_spec(ATTN_W), w_spec(SGU_W),
                  g_spec(0), g_spec(1), g_spec(2)],
        out_specs=pl.BlockSpec((tm, MERGE_TN), lambda i, n: (i, n)),
        out_shape=jax.ShapeDtypeStruct((s, D_MODEL), BF16), compiler_params=_params("parallel", "parallel"),
    )(a, b, c, wpa, wpb, wpc, z, z, z)


def _branch_bwd(br, xb, wp, z, dm, dz, name):
    s = z.shape[0]
    kb = xb.shape[1]
    tm = _tile(s, (MERGE_TM, BLOCK))
    gate0 = GATE_COL0 // MERGE_TN
    aliased = dz is not None

    def body(*refs):
        x_ref, w_ref, g_ref, dm_ref = refs[:4]
        dz_ref, dy_ref, dx_ref = refs[-3:]
        n = pl.program_id(1)
        y = _dot(x_ref[...], w_ref[...])
        sg = _sigmoid(g_ref[...])
        dmv = dm_ref[...]
        dy = (dmv * sg).astype(BF16)
        dy_ref[...] = dy
        dz_ref[...] = ((dmv * y) * (sg * (1.0 - sg))).astype(BF16)
        dx = _dot(dy, w_ref[...], 1, 1)

        @pl.when(n == 0)
        def _():
            dx_ref[...] = dx

        @pl.when(n > 0)
        def _():
            dx_ref[...] += dx

    in_specs = [pl.BlockSpec((tm, kb), lambda i, n: (i, 0)), pl.BlockSpec((kb, MERGE_TN), lambda i, n: (0, n)),
                pl.BlockSpec((tm, MERGE_TN), lambda i, n: (i, gate0 + 2 * br + n)),
                pl.BlockSpec((tm, MERGE_TN), lambda i, n: (i, n))]
    args = [xb, wp, z, dm]
    if aliased:
        in_specs.append(pl.BlockSpec(memory_space=pl.ANY))
        args.append(dz)
    return pl.pallas_call(
        body, name=name, grid=(s // tm, D_MODEL // MERGE_TN), in_specs=in_specs,
        out_specs=[pl.BlockSpec((tm, MERGE_TN), lambda i, n: (i, gate0 + 2 * br + n)),
                   pl.BlockSpec((tm, MERGE_TN), lambda i, n: (i, n)),
                   pl.BlockSpec((tm, kb), lambda i, n: (i, 0))],
        out_shape=[jax.ShapeDtypeStruct((s, IN_COLS), BF16), jax.ShapeDtypeStruct((s, D_MODEL), BF16),
                   jax.ShapeDtypeStruct((s, kb), F32)],
        input_output_aliases={4: 0} if aliased else {},
        compiler_params=_params("parallel", "arbitrary"),
    )(*args)


FFN_TM = 256
FFN_TC = 2816
FFN_STRIP = 256
FFN_UNROLL = 4


def _conv3(cur, prev, w_ref, b_ref):
    cat = jnp.concatenate([prev, cur], axis=0)
    x1 = pltpu.roll(cat, 1, 0)[HALO_CONV:]
    x2 = pltpu.roll(cat, 2, 0)[HALO_CONV:]
    return w_ref[0:1, :] * x2 + w_ref[1:2, :] * x1 + w_ref[2:3, :] * cur + b_ref[...], x1, x2


def _ffn_specs(s, tm, rows_first):
    per = tm // HALO_CONV
    if rows_first:
        cur = pl.BlockSpec((tm, FFN_TC), lambda i, j: (i, j))
        prev = pl.BlockSpec((HALO_CONV, FFN_TC), lambda i, j: (jnp.maximum(i * per - 1, 0), j))
        w = pl.BlockSpec((3, FFN_TC), lambda i, j: (0, j))
        b = pl.BlockSpec((1, FFN_TC), lambda i, j: (0, j))
    else:
        cur = pl.BlockSpec((tm, FFN_TC), lambda j, i: (i, j))
        prev = pl.BlockSpec((HALO_CONV, FFN_TC), lambda j, i: (jnp.maximum(i * per - 1, 0), j))
        w = pl.BlockSpec((3, FFN_TC), lambda j, i: (0, j))
        b = pl.BlockSpec((1, FFN_TC), lambda j, i: (0, j))
    return cur, prev, w, b


def _ffn_act_fwd(upg, upv, cwg, cwv, cbg, cbv, name):
    s = upg.shape[0]
    tm = _tile(s, (FFN_TM, BLOCK))
    cur, prev, w, b = _ffn_specs(s, tm, True)

    def body(g_ref, gp_ref, v_ref, vp_ref, wg_ref, wv_ref, bg_ref, bv_ref, o_ref):
        first = pl.program_id(0) == 0
        gate = _conv3(g_ref[...], jnp.where(first, 0.0, gp_ref[...]), wg_ref, bg_ref)[0]
        val = _conv3(v_ref[...], jnp.where(first, 0.0, vp_ref[...]), wv_ref, bv_ref)[0]
        o_ref[...] = ((gate * _sigmoid(gate)) * val).astype(BF16)

    return pl.pallas_call(
        body, name=name, grid=(s // tm, D_FF // FFN_TC), in_specs=[cur, prev, cur, prev, w, w, b, b], out_specs=cur,
        out_shape=jax.ShapeDtypeStruct((s, D_FF), BF16), compiler_params=_params("parallel", "parallel"),
    )(upg, upg, upv, upv, cwg, cwv, cbg, cbv)


def _ffn_bwd(upg, upv, cwg, cwv, cbg, cbv, dact, name):
    s = upg.shape[0]
    tm = _tile(s, (FFN_TM, BLOCK))
    per = tm // HALO_CONV
    nrow = s // tm
    n = tm + HALO_CONV
    cur, prev, w, b = _ffn_specs(s, tm, False)
    nxt = pl.BlockSpec((HALO_CONV, FFN_TC), lambda j, i: (jnp.minimum((i + 1) * per, nrow * per - 1), j))

    nch = tm // 8
    rows8 = lambda r: pl.ds(pl.multiple_of(r * 8, 8), 8)

    def body(g_ref, gp_ref, gn_ref, v_ref, vp_ref, vn_ref, wg_ref, wv_ref, bg_ref, bv_ref, da_ref, dan_ref,
             dg_ref, dv_ref, dwg_ref, dwv_ref, og_ref, ov_ref):
        i = pl.program_id(1)
        first = i == 0
        last = i == nrow - 1
        row = lax.broadcasted_iota(jnp.int32, (8, FFN_STRIP), 0)

        keep_down = {k: row >= k for k in (1, 2)}
        keep_up = {k: row < 8 - k for k in (1, 2)}

        def down(cur, prev, k):
            return jnp.where(keep_down[k], pltpu.roll(cur, k, 0), pltpu.roll(prev, k, 0))

        def up(cur, nxt, k):
            return jnp.where(keep_up[k], pltpu.roll(cur, 8 - k, 0), pltpu.roll(nxt, 8 - k, 0))

        @pl.when(first)
        def _():
            dwg_ref[...] = jnp.zeros_like(dwg_ref)
            dwv_ref[...] = jnp.zeros_like(dwv_ref)

        for c in range(FFN_TC // FFN_STRIP):
            cols = slice(c * FFN_STRIP, (c + 1) * FFN_STRIP)
            wg = [functools.partial(lambda k: wg_ref[k:k + 1, cols], k) for k in range(3)]
            wv = [functools.partial(lambda k: wv_ref[k:k + 1, cols], k) for k in range(3)]

            def conv_grads(g_cur, g_prev, v_cur, v_prev, da):
                gate = wg[0]() * down(g_cur, g_prev, 2) + wg[1]() * down(g_cur, g_prev, 1) + wg[2]() * g_cur + bg_ref[:, cols]
                val = wv[0]() * down(v_cur, v_prev, 2) + wv[1]() * down(v_cur, v_prev, 1) + wv[2]() * v_cur + bv_ref[:, cols]
                sg = _sigmoid(gate)
                return (da * val) * (sg * (1.0 + gate * (1.0 - sg))), da * (gate * sg)

            def passes(q, carry):
                for u in range(FFN_UNROLL):
                    carry = one_pass(q * FFN_UNROLL + u, carry)
                return carry

            def one_pass(r, carry, tile_end=False):
                dg_cur, dv_cur, acc = carry
                g_r, v_r = g_ref[rows8(r), cols], v_ref[rows8(r), cols]
                if tile_end:
                    g_n, v_n, da_n = gn_ref[:, cols], vn_ref[:, cols], jnp.where(last, 0.0, dan_ref[:, cols])
                else:
                    g_n, v_n, da_n = g_ref[rows8(r + 1), cols], v_ref[rows8(r + 1), cols], da_ref[rows8(r + 1), cols]
                dg_n, dv_n = conv_grads(g_n, g_r, v_n, v_r, da_n)
                new_acc = []
                for o_ref, w, d_cur, d_n, x0, a in ((og_ref, wg, dg_cur, dg_n, g_r, acc[:4]), (ov_ref, wv, dv_cur, dv_n, v_r, acc[4:])):
                    d1, d2 = up(d_cur, d_n, 1), up(d_cur, d_n, 2)
                    o_ref[rows8(r), cols] = w[2]() * d_cur + w[1]() * d1 + w[0]() * d2
                    new_acc += [a[0] + d2 * x0, a[1] + d1 * x0, a[2] + d_cur * x0, a[3] + d_cur]
                return dg_n, dv_n, tuple(new_acc)

            g_p = jnp.where(first, 0.0, gp_ref[:, cols])
            v_p = jnp.where(first, 0.0, vp_ref[:, cols])
            dg0, dv0 = conv_grads(g_ref[0:8, cols], g_p, v_ref[0:8, cols], v_p, da_ref[0:8, cols])
            zero = jnp.zeros((8, FFN_STRIP), F32)
            carry = lax.fori_loop(0, nch // FFN_UNROLL - 1, passes, (dg0, dv0, (zero,) * 8))
            for r in range(nch - FFN_UNROLL, nch):
                carry = one_pass(r, carry, tile_end=r == nch - 1)
            for ref, a in ((dwg_ref, carry[2][:4]), (dwv_ref, carry[2][4:])):
                for k in range(4):
                    ref[k:k + 1, cols] += jnp.sum(a[k], axis=0, keepdims=True)
        dg_ref[...] = og_ref[...].astype(BF16)
        dv_ref[...] = ov_ref[...].astype(BF16)

    acc = pl.BlockSpec((8, FFN_TC), lambda j, i: (0, j))
    full = jax.ShapeDtypeStruct((s, D_FF), BF16)
    accs = jax.ShapeDtypeStruct((8, D_FF), F32)
    return pl.pallas_call(
        body, name=name, grid=(D_FF // FFN_TC, nrow), in_specs=[cur, prev, nxt, cur, prev, nxt, w, w, b, b, cur, nxt],
        out_specs=[cur, cur, acc, acc], out_shape=[full, full, accs, accs],
        scratch_shapes=[pltpu.VMEM((tm, FFN_TC), F32), pltpu.VMEM((tm, FFN_TC), F32)],
        compiler_params=_params("parallel", "arbitrary"),
    )(upg, upg, upg, upv, upv, upv, cwg, cwv, cbg, cbv, dact, dact)


def _mesh_place():
    return lax.axis_index("x"), lax.axis_index("y"), lax.axis_index("c")


def _all_gather(shards, name):
    na = len(shards)

    def body(*refs):
        x_refs, out_refs = refs[:na], refs[na:2 * na]
        send_sems, recv_sems, local_sems = refs[2 * na:]
        x, y, cc = _mesh_place()
        me, sibling = (x, y, cc), (x, y, 1 - cc)
        chips = [(1 - x, y), (x, 1 - y), (1 - x, 1 - y)]

        def copy(k, a, block, to, from_input=False):
            slot = out_refs[a].at[4 * block[0] + 2 * block[1] + block[2]]
            return pltpu.make_async_remote_copy(
                src_ref=x_refs[a] if from_input else slot, dst_ref=slot, send_sem=send_sems.at[k * na + a],
                recv_sem=recv_sems.at[k * na + a], device_id=to, device_id_type=pl.DeviceIdType.MESH)

        mine = [pltpu.make_async_copy(x_refs[a], out_refs[a].at[4 * x + 2 * y + cc], local_sems.at[a]) for a in range(na)]
        for cp in mine:
            cp.start()
        first = [copy(0, a, me, sibling, True) for a in range(na)]
        first += [copy(1 + j, a, me, (*chip, cc), True) for j, chip in enumerate(chips) for a in range(na)]
        for cp in first:
            cp.start()
        passed = []
        for j, chip in enumerate(chips):
            for a in range(na):
                copy(1 + j, a, (*chip, cc), me).wait_recv()
                passed.append(copy(4 + j, a, (*chip, cc), sibling))
                passed[-1].start()
        for a in range(na):
            copy(0, a, sibling, me).wait_recv()
        for j, chip in enumerate(chips):
            for a in range(na):
                copy(4 + j, a, (*chip, 1 - cc), me).wait_recv()
        for cp in first + passed:
            cp.wait_send()
        for cp in mine:
            cp.wait()

    hbm = pl.BlockSpec(memory_space=pl.ANY)
    return pl.pallas_call(
        body, name=name, out_shape=[jax.ShapeDtypeStruct((N_DEV,) + t.shape, t.dtype) for t in shards],
        in_specs=[hbm] * na, out_specs=[hbm] * na,
        scratch_shapes=[pltpu.SemaphoreType.DMA((7 * na,)), pltpu.SemaphoreType.DMA((7 * na,)), pltpu.SemaphoreType.DMA((na,))],
    )(*shards)


def _exchange(blocks, name):
    na = len(blocks)

    def body(*refs):
        g_refs, out_refs = refs[:na], refs[na:2 * na]
        send_sems, recv_sems, local_sems = refs[2 * na:]
        x, y, cc = _mesh_place()
        me = 4 * x + 2 * y + cc
        mine = [pltpu.make_async_copy(g_refs[a].at[me], out_refs[a].at[me], local_sems.at[a]) for a in range(na)]
        for cp in mine:
            cp.start()
        sends, lands = [], []
        for k in range(1, N_DEV):
            px = 1 - x if (k >> 2) & 1 else x
            py = 1 - y if (k >> 1) & 1 else y
            pc = 1 - cc if k & 1 else cc
            peer = 4 * px + 2 * py + pc
            for a in range(na):
                sem = (k - 1) * na + a
                sends.append(pltpu.make_async_remote_copy(
                    src_ref=g_refs[a].at[peer], dst_ref=out_refs[a].at[me], send_sem=send_sems.at[sem], recv_sem=recv_sems.at[sem],
                    device_id=(px, py, pc), device_id_type=pl.DeviceIdType.MESH))
                lands.append(pltpu.make_async_remote_copy(
                    src_ref=g_refs[a].at[peer], dst_ref=out_refs[a].at[peer], send_sem=send_sems.at[sem], recv_sem=recv_sems.at[sem],
                    device_id=(px, py, pc), device_id_type=pl.DeviceIdType.MESH))
        for cp in sends:
            cp.start()
        for cp in lands:
            cp.wait_recv()
        for cp in sends:
            cp.wait_send()
        for cp in mine:
            cp.wait()

    hbm = pl.BlockSpec(memory_space=pl.ANY)
    return pl.pallas_call(
        body, name=name, out_shape=[jax.ShapeDtypeStruct(t.shape, t.dtype) for t in blocks],
        in_specs=[hbm] * na, out_specs=[hbm] * na,
        scratch_shapes=[pltpu.SemaphoreType.DMA((7 * na,)), pltpu.SemaphoreType.DMA((7 * na,)), pltpu.SemaphoreType.DMA((na,))],
    )(*blocks)


def _peer(k):
    x, y, cc = _mesh_place()
    px = 1 - x if (k >> 2) & 1 else x
    py = 1 - y if (k >> 1) & 1 else y
    pc = 1 - cc if k & 1 else cc
    return (px, py, pc), 4 * px + 2 * py + pc


def _push_copy(src_ref, land_ref, k, a, na, send_sems, recv_sems, indexed, landed):
    x, y, cc = _mesh_place()
    place, peer = _peer(k)
    sem = (k - 1) * na + a
    return pltpu.make_async_remote_copy(
        src_ref=src_ref.at[peer] if indexed else src_ref, dst_ref=land_ref.at[peer if landed else 4 * x + 2 * y + cc],
        send_sem=send_sems.at[sem], recv_sem=recv_sems.at[sem], device_id=place, device_id_type=pl.DeviceIdType.MESH)


_HBM = pl.BlockSpec(memory_space=pltpu.HBM)
_SEM = pl.BlockSpec(memory_space=pltpu.SEMAPHORE)
_EFFECT = pltpu.SideEffectType.DATAFLOW_SIDE_EFFECTING


def _push_start(srcs, indexed, name):
    na = len(srcs)
    lands = [lax.empty(t.shape if indexed else (N_DEV,) + t.shape, t.dtype) for t in srcs]

    def body(*refs):
        src_refs, land_refs = refs[:na], refs[na:2 * na]
        send_sems, recv_sems = refs[2 * na], refs[2 * na + 1]
        token = refs[-1]
        for k in range(1, N_DEV):
            for a in range(na):
                _push_copy(src_refs[a], land_refs[a], k, a, na, send_sems, recv_sems, indexed, False).start()
        token[...] = jnp.zeros_like(token)

    sems = pltpu.SemaphoreType.DMA((7 * na,))
    out = pl.pallas_call(
        body, name=name,
        out_shape=(sems, sems, *[pltpu.HBM(t.shape, t.dtype) for t in srcs], *[pltpu.HBM(t.shape, t.dtype) for t in lands],
                   jax.ShapeDtypeStruct((8, LANES), F32)),
        in_specs=[_HBM] * (2 * na), out_specs=(_SEM, _SEM, *[_HBM] * (2 * na), pl.BlockSpec(memory_space=pltpu.VMEM)),
        input_output_aliases={i: 2 + i for i in range(2 * na)},
        compiler_params=pltpu.CompilerParams(has_side_effects=_EFFECT),
    )(*[pltpu.with_memory_space_constraint(t, pltpu.HBM) for t in srcs + lands])
    return out[0], out[1], list(out[2:2 + na]), list(out[2 + na:2 + 2 * na]), out[-1]


def _push_wait(started, indexed, after, name):
    send_sems, recv_sems, srcs, lands, _ = started
    na = len(srcs)

    def body(*refs):
        src_refs, land_refs = refs[:na], refs[na:2 * na]
        send_sems, recv_sems = refs[2 * na], refs[2 * na + 1]
        for k in range(1, N_DEV):
            for a in range(na):
                copy = _push_copy(src_refs[a], land_refs[a], k, a, na, send_sems, recv_sems, indexed, True)
                copy.wait_send()
                copy.wait_recv()

    out = pl.pallas_call(
        body, name=name, out_shape=[pltpu.HBM(t.shape, t.dtype) for t in srcs + lands],
        in_specs=[_HBM] * (2 * na) + [_SEM, _SEM, pl.BlockSpec(memory_space=pl.ANY)], out_specs=[_HBM] * (2 * na),
        input_output_aliases={i: i for i in range(2 * na)},
        compiler_params=pltpu.CompilerParams(has_side_effects=_EFFECT),
    )(*srcs, *lands, send_sems, recv_sems, after)
    x, y, cc = _mesh_place()
    me = 4 * x + 2 * y + cc
    return [lax.dynamic_update_index_in_dim(
        land, lax.dynamic_index_in_dim(src, me, 0, keepdims=False) if indexed else src, me, 0)
        for src, land in zip(out[:na], out[na:])]


def _adamw_sum(parts, w, m, v, name):
    _, r, c = parts.shape
    tr = _tile(r, (256, 128, 64, 32, 16, 8))

    def body(p_ref, w_ref, m_ref, v_ref, g_ref, d_ref, nm_ref, nv_ref):
        _adam_store(_sum_parts(p_ref), w_ref, m_ref, v_ref, g_ref, d_ref, nm_ref, nv_ref)

    row = pl.BlockSpec((tr, c), lambda i: (i, 0))
    shp = jax.ShapeDtypeStruct((r, c), F32)
    return pl.pallas_call(
        body, name=name, grid=(r // tr,), in_specs=[pl.BlockSpec((N_DEV, tr, c), lambda i: (0, i, 0)), row, row, row],
        out_specs=[row, row, row, row], out_shape=[shp, shp, shp, shp], compiler_params=_params("parallel"),
    )(parts, w, m, v)


def _sum_parts(p_ref):
    g = p_ref[0].astype(F32)
    for k in range(1, N_DEV):
        g = g + p_ref[k].astype(F32)
    return g


def _adam_store(g, w_ref, m_ref, v_ref, g_ref, d_ref, nm_ref, nv_ref):
    nm = ADAM_B1 * m_ref[...] + (1.0 - ADAM_B1) * g
    nv = ADAM_B2 * v_ref[...] + (1.0 - ADAM_B2) * (g * g)
    m_hat = nm / (1.0 - ADAM_B1 ** ADAM_STEP)
    v_hat = nv / (1.0 - ADAM_B2 ** ADAM_STEP)
    g_ref[...] = g
    nm_ref[...] = nm
    nv_ref[...] = nv
    d_ref[...] = -ADAM_LR * (m_hat / (jnp.sqrt(v_hat) + ADAM_EPS) + ADAM_WD * w_ref[...])


def _adamw_weight(parts, w, m, v, name):
    _, r, c = w.shape
    tr = _tile(r, (256, 128, 176))
    nr = r // tr

    def body(p0_ref, p1_ref, w_ref, m_ref, v_ref, g_ref, d_ref, nm_ref, nv_ref):
        g = jnp.where(pl.program_id(0) == 0, _sum_parts(p0_ref), _sum_parts(p1_ref))
        _adam_store(g, w_ref, m_ref, v_ref, g_ref, d_ref, nm_ref, nv_ref)

    part = lambda layer: pl.BlockSpec(
        (N_DEV, tr, c), lambda l, i: (0, jnp.where(l == layer, i, (nr - 1) * (1 - layer)), 0))
    row = pl.BlockSpec((None, tr, c), lambda l, i: (l, i, 0))
    shp = jax.ShapeDtypeStruct(w.shape, F32)
    return pl.pallas_call(
        body, name=name, grid=(DEPTH, nr), in_specs=[part(0), part(1), row, row, row],
        out_specs=[row, row, row, row], out_shape=[shp, shp, shp, shp], compiler_params=_params("arbitrary", "arbitrary"),
    )(parts[0], parts[1], w, m, v)


def _full_to_slots(name, t):
    k, n = t.shape
    if name in ROW_SHARDED:
        return t.reshape(N_DEV, k // N_DEV, n)
    return t.reshape(k, N_DEV, n // N_DEV).transpose(1, 0, 2)


def _slots_to_full(name, t):
    _, r, c = t.shape
    if name in ROW_SHARDED:
        return t.reshape(N_DEV * r, c)
    return t.transpose(1, 0, 2).reshape(r, N_DEV * c)


def _small_sizes(shapes, names):
    return [(n, shapes[n], -(-int(math.prod(shapes[n])) // (8 * LANES)) * 8) for n in names]


def _pack_small(tree, shapes, names, row_tile):
    rows = []
    for n, shp, nrow in _small_sizes(shapes, names):
        flat = tree[n].reshape(-1)
        rows.append(jnp.pad(flat, (0, nrow * LANES - flat.shape[0])).reshape(nrow, LANES))
    total = sum(r.shape[0] for r in rows)
    if total % row_tile:
        rows.append(jnp.zeros((-total % row_tile, LANES), F32))
    return jnp.concatenate(rows, axis=0)


def _unpack_small(buf, shapes, names):
    out, r0 = {}, 0
    for n, shp, nrow in _small_sizes(shapes, names):
        out[n] = buf[r0:r0 + nrow].reshape(-1)[:int(math.prod(shp))].reshape(shp)
        r0 += nrow
    return out


def _block_diag(w):
    g = w.shape[0]
    eye = jnp.eye(g, dtype=w.dtype)
    return (eye[:, None, :, None] * w[:, :, None, :]).reshape(g * HEAD_DIM, g * HEAD_DIM)


def kernel(x, positions, norm1, w_in, q_norm, k_norm, sinks, w_pool, pool_scale, sgu_v_norm, w_s, b_s, w_proj_a, w_proj_b, w_proj_c, w_out, norm2, w_up, conv_w, conv_b, w_down, loss_target, m_norm1, m_w_in, m_q_norm, m_k_norm, m_sinks, m_w_pool, m_pool_scale, m_sgu_v_norm, m_w_s, m_b_s, m_w_proj_a, m_w_proj_b, m_w_proj_c, m_w_out, m_norm2, m_w_up, m_conv_w, m_conv_b, m_w_down, v_norm1, v_w_in, v_q_norm, v_k_norm, v_sinks, v_w_pool, v_pool_scale, v_sgu_v_norm, v_w_s, v_b_s, v_w_proj_a, v_w_proj_b, v_w_proj_c, v_w_out, v_norm2, v_w_up, v_conv_w, v_conv_b, v_w_down):
    names = ("norm1", "w_in", "q_norm", "k_norm", "sinks", "w_pool", "pool_scale", "sgu_v_norm", "w_s", "b_s", "w_proj_a",
             "w_proj_b", "w_proj_c", "w_out", "norm2", "w_up", "conv_w", "conv_b", "w_down")
    wts = dict(zip(names, (norm1, w_in, q_norm, k_norm, sinks, w_pool, pool_scale, sgu_v_norm, w_s, b_s, w_proj_a, w_proj_b,
                           w_proj_c, w_out, norm2, w_up, conv_w, conv_b, w_down)))
    mom = dict(zip(names, (m_norm1, m_w_in, m_q_norm, m_k_norm, m_sinks, m_w_pool, m_pool_scale, m_sgu_v_norm, m_w_s, m_b_s,
                           m_w_proj_a, m_w_proj_b, m_w_proj_c, m_w_out, m_norm2, m_w_up, m_conv_w, m_conv_b, m_w_down)))
    var = dict(zip(names, (v_norm1, v_w_in, v_q_norm, v_k_norm, v_sinks, v_w_pool, v_pool_scale, v_sgu_v_norm, v_w_s, v_b_s,
                           v_w_proj_a, v_w_proj_b, v_w_proj_c, v_w_out, v_norm2, v_w_up, v_conv_w, v_conv_b, v_w_down)))
    xs = x[0]
    target = loss_target[0]
    s = xs.shape[0]

    inv_freq = ROPE_THETA ** (-jnp.arange(0, HEAD_DIM, 2, dtype=F32) / HEAD_DIM)
    ang = positions[0].astype(F32)[:, None] * inv_freq
    cosf = jnp.tile(jnp.cos(ang), (1, 4))
    sinf = jnp.tile(jnp.concatenate([-jnp.sin(ang), jnp.sin(ang)], axis=1), (1, 2))

    local = [{n: wts[n][l] if n == "conv_w" else wts[n][l].astype(BF16) for n in SHARDED} for l in range(DEPTH)]
    later = SHARDED[1:]
    full = [{"w_in": _slots_to_full("w_in", _all_gather([local[0]["w_in"]], "gather_w_in_0")[0])}, None]
    gather0 = _push_start([local[0][n] for n in later], False, "gather_rest_0_start")
    norm1_first = norm1[0] + gather0[4][0, 0]

    def layer_consts(l):
        return dict(
            bdw=_block_diag(w_pool[l]).astype(BF16), qn=jnp.tile(q_norm[l], 2).reshape(1, LANES),
            kn=jnp.tile(k_norm[l], 2).reshape(1, LANES), vn=jnp.tile(sgu_v_norm[l], 4).reshape(1, SGU_W),
            bcol=jnp.repeat(b_s[l].T, HEAD_DIM, axis=1),
            cbg=conv_b[l][:D_FF].reshape(1, D_FF), cbv=conv_b[l][D_FF:].reshape(1, D_FF))

    gate_cols, val_cols = (0, D_FF), (D_FF, D_FF)

    saved = []
    cur = xs
    for l in range(DEPTH):
        if l == 1:
            landed = _push_wait(gather1, False, cur, "gather_weights_1_wait")
            full[1] = {n: _slots_to_full(n, t) for n, t in zip(SHARDED, landed)}
        fw, k = full[l], layer_consts(l)
        h1 = _rms_fwd(cur, norm1_first if l == 0 else norm1[l], f"rms1_fwd_{l}")
        z = _mm(h1, fw["w_in"], name=f"in_proj_{l}")
        a = _pool_fwd(z, k["bdw"], pool_scale[l], f"pool_fwd_{l}")
        b = _attn_fwd(z, cosf, sinf, k["qn"], k["kn"], sinks[l], f"attn_fwd_{l}")
        c = _sgu_fwd(z, w_s[l], k["bcol"], k["vn"], f"sgu_fwd_{l}")
        w_proj_a_l = fw.get("w_proj_a")
        if l == 0:
            landed = _push_wait(gather0, False, c, "gather_rest_0_wait")
            fw.update({n: _slots_to_full(n, t) for n, t in zip(later, landed)})
            gather1 = _push_start([local[1][n] for n in SHARDED], False, "gather_weights_1_start")
            w_proj_a_l = fw["w_proj_a"] + gather1[4][0, 0].astype(BF16)
        merged = _merge_fwd(a, b, c, w_proj_a_l, fw["w_proj_b"], fw["w_proj_c"], z, f"merge_fwd_{l}")
        x1 = _mm(merged, fw["w_out"], add=cur, name=f"out_proj_{l}")
        h2 = _rms_fwd(x1, norm2[l], f"rms2_fwd_{l}")
        upg = _mm(h2, fw["w_up"], b_n=gate_cols, name=f"up_gate_{l}")
        upv = _mm(h2, fw["w_up"], b_n=val_cols, name=f"up_val_{l}")
        k["cwg"], k["cwv"] = fw["conv_w"][:, :D_FF], fw["conv_w"][:, D_FF:]
        act = _ffn_act_fwd(upg, upv, k["cwg"], k["cwv"], k["cbg"], k["cbv"], f"ffn_act_fwd_{l}")
        x2 = _mm(act, fw["w_down"], add=x1, name=f"down_proj_{l}")
        saved.append(dict(x0=cur, h1=h1, z=z, a=a, b=b, c=c, merged=merged, x1=x1, h2=h2, upg=upg, upv=upv, act=act))
        cur = x2

    dcur, loss_tile = _loss_head(cur, target)
    loss = lax.psum(loss_tile[0, 0], ("x", "y", "c"))

    gsmall = [None] * DEPTH
    small_shapes = {n: wts[n].shape for n in SMALL}

    def slots_of(grads):
        return [_full_to_slots(n, t) for n, t in grads.items()]

    for l in reversed(range(DEPTH)):
        fw, k, sv = full[l], layer_consts(l), saved[l]
        k["cwg"], k["cwv"] = fw["conv_w"][:, :D_FF], fw["conv_w"][:, D_FF:]
        staged = l == 0
        wgrad = functools.partial(_mm, ta=True, out_dtype=BF16)
        w_down_l = fw["w_down"] + exchange1[4][0, 0].astype(BF16) if staged else fw["w_down"]
        dact = _mm(dcur, w_down_l, tb=True, name=f"down_proj_bwd_{l}")
        g_down = wgrad(sv["act"], dcur, name=f"down_proj_wgrad_{l}")
        dg0, dv0, dcg, dcv = _ffn_bwd(sv["upg"], sv["upv"], k["cwg"], k["cwv"], k["cbg"], k["cbv"], dact, f"ffn_bwd_{l}")
        dh2 = _mm(dg0, fw["w_up"], tb=True, b_k=gate_cols, name=f"up_gate_bwd_{l}")
        dh2 = _mm(dv0, fw["w_up"], tb=True, b_k=val_cols, add=dh2, name=f"up_val_bwd_{l}")
        g_up = wgrad(sv["h2"], dg0, out_cols=(0, 2 * D_FF), name=f"up_gate_wgrad_{l}")
        g_up = wgrad(sv["h2"], dv0, out_cols=(D_FF, 2 * D_FF), out_into=g_up, name=f"up_val_wgrad_{l}")
        g_ffn = dict(w_up=g_up, w_down=g_down, conv_w=jnp.concatenate([dcg[0:3], dcv[0:3]], axis=1))
        norm2_l = norm2[l]
        if staged:
            parts1 = dict(zip(SHARDED, _push_wait(exchange1, True, g_up, "exchange_grads_1_wait")))
            exchange_ffn = _push_start(slots_of(g_ffn), True, "exchange_ffn_0_start")
            norm2_l = norm2_l + exchange_ffn[4][0, 0]
        dx1, g_norm2 = _rms_bwd(sv["x1"], norm2_l, dh2, dcur, f"rms2_bwd_{l}")
        dmerged = _mm(dx1, fw["w_out"], tb=True, name=f"out_proj_bwd_{l}")
        g_out = wgrad(sv["merged"], dx1, name=f"out_proj_wgrad_{l}")
        dz, dya, da = _branch_bwd(0, sv["a"], fw["w_proj_a"], sv["z"], dmerged, None, f"branch_a_bwd_{l}")
        dz, dyb, db = _branch_bwd(1, sv["b"], fw["w_proj_b"], sv["z"], dmerged, dz, f"branch_b_bwd_{l}")
        dz, dyc, dc = _branch_bwd(2, sv["c"], fw["w_proj_c"], sv["z"], dmerged, dz, f"branch_c_bwd_{l}")
        g_mix = dict(w_proj_a=wgrad(sv["a"], dya, name=f"proj_a_wgrad_{l}"), w_proj_b=wgrad(sv["b"], dyb, name=f"proj_b_wgrad_{l}"),
                     w_proj_c=wgrad(sv["c"], dyc, name=f"proj_c_wgrad_{l}"), w_out=g_out)
        pool_scale_l = pool_scale[l]
        if staged:
            exchange_mix = _push_start(slots_of(g_mix), True, "exchange_mixer_0_start")
            pool_scale_l = pool_scale_l + exchange_mix[4][0, 0]
        dxp, g_bdw, g_pscale = _pool_bwd(sv["z"], da, k["bdw"], pool_scale_l, f"pool_bwd_{l}")
        dq, dkc, dkp, dvc, dvp, g_qn, g_sink = _attn_bwd(sv["z"], cosf, sinf, k["qn"], k["kn"], sinks[l], db, f"attn_bwd_{l}")
        duv, g_ws, g_bacc, g_vn = _sgu_bwd(sv["z"], w_s[l], k["bcol"], k["vn"], dc, f"sgu_bwd_{l}")
        gsmall[l] = dict(
            q_norm=g_qn[0, :HEAD_DIM], sinks=g_sink[:, 0],
            w_pool=jnp.stack([g_bdw[g * HEAD_DIM:(g + 1) * HEAD_DIM, g * HEAD_DIM:(g + 1) * HEAD_DIM] for g in range(4)]),
            pool_scale=g_pscale[0], sgu_v_norm=g_vn[0, :HEAD_DIM], w_s=g_ws, b_s=g_bacc[:, ::HEAD_DIM].T,
            norm2=g_norm2[0], conv_b=jnp.concatenate([dcg[3], dcv[3]]))
        kn_l = k["kn"]
        if staged:
            early = _pack_small({n: jnp.stack([gsmall[i][n] for i in range(DEPTH)]) for n in SMALL_EARLY}, small_shapes,
                                SMALL_EARLY, SMALL_ROW_TILE)
            gather_early = _push_start([early], False, "gather_small_grads_start")
            kn_l = kn_l + gather_early[4][0, 0]
        dz, g_kn = _kv_post(sv["z"], cosf, sinf, kn_l, dkc, dkp, dvc, dvp, dxp, dq, duv, dz, f"kv_post_{l}")
        g_in = dict(w_in=wgrad(sv["h1"], dz, name=f"in_proj_wgrad_{l}"))
        norm1_l = norm1[l]
        if staged:
            exchange_in = _push_start(slots_of(g_in), True, "exchange_w_in_0_start")
            norm1_l = norm1_l + exchange_in[4][0, 0]
        dh1 = _mm(dz, fw["w_in"], tb=True, name=f"in_proj_bwd_{l}")
        dcur, g_norm1 = _rms_bwd(sv["x0"], norm1_l, dh1, dx1, f"rms1_bwd_{l}")
        if not staged:
            exchange1 = _push_start(slots_of({n: {**g_in, **g_mix, **g_ffn}[n] for n in SHARDED}), True, "exchange_grads_1_start")
        gsmall[l].update(norm1=g_norm1[0], k_norm=g_kn[0, :HEAD_DIM])
    grad_x = dcur[None]

    def update_small(gathered, names, row_tile, name):
        pack = lambda tree: _pack_small(tree, small_shapes, names, row_tile)
        return [_unpack_small(t, small_shapes, names) for t in _adamw_sum(gathered, pack(wts), pack(mom), pack(var), name)]

    late = _pack_small({n: jnp.stack([gsmall[i][n] for i in range(DEPTH)]) for n in SMALL_LATE}, small_shapes, SMALL_LATE, 8)
    small = update_small(_all_gather([late], "gather_late_small_grads")[0], SMALL_LATE, 8, "adamw_replicated_late")
    early_all = _push_wait(gather_early, False, small[0]["norm1"], "gather_small_grads_wait")[0]
    for kind, tree in enumerate(update_small(early_all, SMALL_EARLY, SMALL_ROW_TILE, "adamw_replicated")):
        small[kind].update(tree)
    g_s = small[0]["norm2"]

    parts0 = dict(zip(g_ffn, _push_wait(exchange_ffn, True, g_s, "exchange_ffn_0_wait")))
    parts0.update(zip(g_mix, _push_wait(exchange_mix, True, g_s, "exchange_mixer_0_wait")))
    update = lambda n: _adamw_weight([parts0[n], parts1[n]], wts[n], mom[n], var[n], f"adamw_{n}")
    big = {n: update(n) for n in SHARDED[1:]}
    parts0.update(zip(g_in, _push_wait(exchange_in, True, big["w_up"][0], "exchange_w_in_0_wait")))
    big["w_in"] = update("w_in")

    outs = [loss, grad_x]
    for kind in range(4):
        outs += [small[kind][n] if n in SMALL else big[n][kind] for n in names]
    return tuple(outs)
```

```python
import functools
import math

import jax
import jax.numpy as jnp
import numpy as np
from jax import lax
from jax.experimental import pallas as pl
from jax.experimental.pallas import tpu as pltpu

F32 = jnp.float32
BF16 = jnp.bfloat16

D_MODEL = 1024
DEPTH = 2
HEAD_DIM = 64
N_Q_HEADS = 8
Q_PER_KV = 4
BLOCK = 128
POOL_W = 256
ATTN_W = 512
KV_W = 128
SGU_W = 256
IN_COLS = 4608
GATE_COL0 = 1536
D_FF = 2816
EPS = 1e-6
ROPE_THETA = 10000.0
N_DEV = 8
LANES = 128
HALO_POOL = 16
HALO_CONV = 8

ADAM_LR = 0.001
ADAM_B1 = 0.9
ADAM_B2 = 0.999
ADAM_EPS = 1e-08
ADAM_WD = 0.01
ADAM_STEP = 10

VMEM_LIMIT = 48 * 1024 * 1024
MM_VMEM_BUDGET = 40 * 1024 * 1024

SHARDED = ("w_in", "w_proj_a", "w_proj_b", "w_proj_c", "w_out", "w_up", "w_down", "conv_w")
ROW_SHARDED = ("w_out", "w_down")
SMALL_ROW_TILE = 256
SMALL = ("norm1", "q_norm", "k_norm", "sinks", "w_pool", "pool_scale", "sgu_v_norm", "w_s", "b_s", "norm2", "conv_b")
SMALL_LATE = ("norm1", "k_norm")
SMALL_EARLY = tuple(n for n in SMALL if n not in SMALL_LATE)

_GELU_C = math.sqrt(2.0 / math.pi)
_GELU_A = 0.044715


def _params(*sem):
    return pltpu.CompilerParams(dimension_semantics=sem, vmem_limit_bytes=VMEM_LIMIT)


def _tile(n, prefs):
    for t in prefs:
        if t <= n and n % t == 0:
            return t
    return n


def _head_mean_matrix():
    r = lax.broadcasted_iota(jnp.int32, (LANES, LANES), 0)
    c = lax.broadcasted_iota(jnp.int32, (LANES, LANES), 1)
    return jnp.where((r >= HEAD_DIM) == (c >= HEAD_DIM), 1.0 / HEAD_DIM, 0.0).astype(BF16)


def _head_mean(v, bd):
    hi = v.astype(BF16)
    rest = v - hi.astype(F32)
    mid = rest.astype(BF16)
    lo = (rest - mid.astype(F32)).astype(BF16)
    mm = lambda p: jnp.dot(p, bd, preferred_element_type=F32)
    return mm(hi) + (mm(mid) + mm(lo))


def _rot_half(t):
    lane = lax.broadcasted_iota(jnp.int32, t.shape, 1)
    return jnp.where((lane & 32) == 0, pltpu.roll(t, LANES - 32, 1), pltpu.roll(t, 32, 1))


def _norm_rope(t, gn, cosf, sinf, bd):
    r = lax.rsqrt(_head_mean(t * t, bd) + EPS)
    n = t * r
    y = n * gn
    return y * cosf + _rot_half(y) * sinf, n, r


def _norm_rope_bwd(d, t, n, r, gn, cosf, sinf, bd):
    dy = d * cosf + _rot_half(d * sinf)
    dgn = jnp.sum(dy * n, axis=0, keepdims=True)
    u = dy * gn
    dt = r * u - t * (r * r * r) * _head_mean(t * u, bd)
    return dt, dgn


def _gelu(x):
    t = jnp.tanh(_GELU_C * (x + _GELU_A * (x * x * x)))
    return 0.5 * x * (1.0 + t), t


def _gelu_grad(x, t):
    return 0.5 * (1.0 + t) + 0.5 * x * (1.0 - t * t) * (_GELU_C * (1.0 + 3.0 * _GELU_A * x * x))


def _sigmoid(x):
    return jax.nn.sigmoid(x)


def _dot(a, b, ca=1, cb=0):
    return lax.dot_general(a.astype(BF16), b.astype(BF16), (((ca,), (cb,)), ((), ())), preferred_element_type=F32)


def _mm(a, b, *, ta=False, tb=False, add=None, out_dtype=F32, name, b_n=None, b_k=None, out_cols=None, out_into=None):
    m = a.shape[1] if ta else a.shape[0]
    k = a.shape[0] if ta else a.shape[1]
    n = b_n[1] if b_n else (b.shape[0] if tb else b.shape[1])
    tn = _tile(n, (1024, 1152, 1408, 512, 256, 128))
    has_add = add is not None
    fits = []
    for tm in (2048, 1024, 1408, 512, 256, 128):
        for tk in (k, 4608, 2816, 2048, 1408, 1152, 1024, 512, 256, 128):
            if tm <= m and m % tm == 0 and tk <= k and k % tk == 0:
                need = (2 * (tm * tk * a.dtype.itemsize + tk * tn * b.dtype.itemsize) + 2 * tm * tn * jnp.dtype(out_dtype).itemsize
                        + 2 * tm * tn * 4 * has_add + tm * tn * 4 * (tk < k))
                if need <= MM_VMEM_BUDGET:
                    fits.append((k // tk, -tm, tm, tk))
    if fits:
        _, _, tm, tk = min(fits)
    else:
        tm, tk = _tile(m, (256, 128)), _tile(k, (512, 256, 128))
    nk = k // tk
    n0 = b_n[0] // tn if b_n else 0
    k0 = b_k[0] // tk if b_k else 0
    o0, n_out = (out_cols[0] // tn, out_cols[1]) if out_cols else (0, n)
    n_in = 2 + has_add + (out_into is not None)

    def body(*refs):
        a_ref, b_ref = refs[0], refs[1]
        add_ref = refs[2] if has_add else None
        o_ref = refs[n_in]
        def finish(r):
            if has_add:
                r = r + add_ref[...]
            o_ref[...] = r.astype(out_dtype)

        if nk == 1:
            finish(_dot(a_ref[...], b_ref[...], 0 if ta else 1, 1 if tb else 0))
        else:
            acc_ref = refs[-1]
            kk = pl.program_id(2)

            @pl.when(kk == 0)
            def _():
                acc_ref[...] = jnp.zeros_like(acc_ref)

            acc_ref[...] += _dot(a_ref[...], b_ref[...], 0 if ta else 1, 1 if tb else 0)

            @pl.when(kk == nk - 1)
            def _():
                finish(acc_ref[...])

    a_spec = pl.BlockSpec((tk, tm), lambda i, j, kk: (kk, i)) if ta else pl.BlockSpec((tm, tk), lambda i, j, kk: (i, kk))
    if tb:
        b_spec = pl.BlockSpec((tn, tk), lambda i, j, kk: (j + n0, kk + k0))
    else:
        b_spec = pl.BlockSpec((tk, tn), lambda i, j, kk: (kk + k0, j + n0))
    in_specs = [a_spec, b_spec] + ([pl.BlockSpec((tm, tn), lambda i, j, kk: (i, j))] if has_add else [])
    args = (a, b) + ((add,) if has_add else ())
    if out_into is not None:
        in_specs.append(pl.BlockSpec(memory_space=pl.ANY))
        args += (out_into,)
    return pl.pallas_call(
        body, name=name, grid=(m // tm, n // tn, nk), in_specs=in_specs,
        out_specs=pl.BlockSpec((tm, tn), lambda i, j, kk: (i, j + o0)),
        out_shape=jax.ShapeDtypeStruct((m, n_out), out_dtype),
        scratch_shapes=[pltpu.VMEM((tm, tn), F32)] if nk > 1 else [],
        input_output_aliases={n_in - 1: 0} if out_into is not None else {},
        compiler_params=_params("parallel", "parallel", "arbitrary"),
    )(*args)


def _rms_fwd(x, g, name):
    s, d = x.shape
    tr = _tile(s, (512, 256, 128))

    def body(x_ref, g_ref, h_ref):
        xv = x_ref[...]
        r = lax.rsqrt(jnp.mean(xv * xv, axis=-1, keepdims=True) + EPS)
        h_ref[...] = ((xv * r) * g_ref[...]).astype(BF16)

    return pl.pallas_call(
        body, name=name, grid=(s // tr,),
        in_specs=[pl.BlockSpec((tr, d), lambda i: (i, 0)), pl.BlockSpec((1, d), lambda i: (0, 0))],
        out_specs=pl.BlockSpec((tr, d), lambda i: (i, 0)),
        out_shape=jax.ShapeDtypeStruct((s, d), BF16), compiler_params=_params("parallel"),
    )(x, g.reshape(1, d))


def _rms_bwd(x, g, dh, dres, name):
    s, d = x.shape
    tr = _tile(s, (512, 256, 128))

    def body(x_ref, g_ref, dh_ref, dres_ref, dx_ref, dg_ref):
        xv = x_ref[...]
        r = lax.rsqrt(jnp.mean(xv * xv, axis=-1, keepdims=True) + EPS)
        dhv = dh_ref[...]
        u = dhv * g_ref[...]
        dx_ref[...] = dres_ref[...] + (r * u - xv * (r * r * r) * jnp.mean(xv * u, axis=-1, keepdims=True))
        part = jnp.sum(dhv * (xv * r), axis=0, keepdims=True)

        @pl.when(pl.program_id(0) == 0)
        def _():
            dg_ref[...] = part

        @pl.when(pl.program_id(0) > 0)
        def _():
            dg_ref[...] += part

    row = pl.BlockSpec((tr, d), lambda i: (i, 0))
    vec = pl.BlockSpec((1, d), lambda i: (0, 0))
    return pl.pallas_call(
        body, name=name, grid=(s // tr,), in_specs=[row, vec, row, row], out_specs=[row, vec],
        out_shape=[jax.ShapeDtypeStruct((s, d), F32), jax.ShapeDtypeStruct((1, d), F32)],
        compiler_params=_params("arbitrary"),
    )(x, g.reshape(1, d), dh, dres)


def _loss_head(y, target):
    s, d = y.shape
    tr = _tile(s, (512, 256, 128))

    def body(y_ref, t_ref, dy_ref, l_ref):
        err = y_ref[...] - t_ref[...]
        dy_ref[...] = err * (1.0 / d)
        part = jnp.sum(jnp.sum(err * err, axis=-1, keepdims=True) * (1.0 / d), axis=0, keepdims=True) * 0.5
        part = jnp.broadcast_to(part, (8, LANES))

        @pl.when(pl.program_id(0) == 0)
        def _():
            l_ref[...] = part

        @pl.when(pl.program_id(0) > 0)
        def _():
            l_ref[...] += part

    row = pl.BlockSpec((tr, d), lambda i: (i, 0))
    acc = pl.BlockSpec((8, LANES), lambda i: (0, 0))
    return pl.pallas_call(
        body, name="loss_head", grid=(s // tr,), in_specs=[row, row], out_specs=[row, acc],
        out_shape=[jax.ShapeDtypeStruct((s, d), F32), jax.ShapeDtypeStruct((8, LANES), F32)],
        compiler_params=_params("arbitrary"),
    )(y, target)


def _pool_lane_select(lane, v2, v4, v8, v16):
    return jnp.where(lane < 64, v2, jnp.where(lane < 128, v4, jnp.where(lane < 192, v8, v16)))


def _pool_diff(xc, xp, row0):
    n = BLOCK + HALO_POOL
    cat = jnp.concatenate([xp, xc], axis=0)
    s2 = cat + pltpu.roll(cat, 1, 0)
    s4 = s2 + pltpu.roll(s2, 2, 0)
    s8 = s4 + pltpu.roll(s4, 4, 0)
    s16 = s8 + pltpu.roll(s8, 8, 0)
    lane = lax.broadcasted_iota(jnp.int32, (n, POOL_W), 1)
    wsum = _pool_lane_select(lane, s2, s4, s8, s16)[HALO_POOL:]
    return wsum / _pool_count(row0, BLOCK) - xc


def _pool_count(row0, rows):
    lane = lax.broadcasted_iota(jnp.int32, (rows, POOL_W), 1)
    t = lax.broadcasted_iota(jnp.int32, (rows, POOL_W), 0) + row0
    return jnp.minimum(t + 1, _pool_lane_select(lane, 2, 4, 8, 16)).astype(F32)


def _pool_fwd(z, bdw, scale, name):
    s = z.shape[0]
    nb = s // BLOCK

    def body(xc_ref, xp_ref, w_ref, sc_ref, a_ref):
        i = pl.program_id(0)
        xp = jnp.where(i > 0, xp_ref[...], 0.0)
        diff = _pool_diff(xc_ref[...], xp, i * BLOCK)
        a_ref[...] = (_dot(diff, w_ref[...]) * sc_ref[...]).astype(BF16)

    return pl.pallas_call(
        body, name=name, grid=(nb,),
        in_specs=[pl.BlockSpec((BLOCK, POOL_W), lambda i: (i, 0)),
                  pl.BlockSpec((HALO_POOL, POOL_W), lambda i: (jnp.maximum(i * (BLOCK // HALO_POOL) - 1, 0), 0)),
                  pl.BlockSpec((POOL_W, POOL_W), lambda i: (0, 0)),
                  pl.BlockSpec((1, POOL_W), lambda i: (0, 0))],
        out_specs=pl.BlockSpec((BLOCK, POOL_W), lambda i: (i, 0)),
        out_shape=jax.ShapeDtypeStruct((s, POOL_W), BF16), compiler_params=_params("parallel"),
    )(z, z, bdw, scale.reshape(1, POOL_W))


def _pool_bwd(z, da, bdw, scale, name):
    s = z.shape[0]
    nb = s // BLOCK
    per = BLOCK // HALO_POOL
    n = BLOCK + HALO_POOL

    def body(xc_ref, xp_ref, dac_ref, dan_ref, w_ref, sc_ref, dx_ref, dw_ref, dsc_ref):
        i = pl.program_id(0)
        xp = jnp.where(i > 0, xp_ref[...], 0.0)
        diff = _pool_diff(xc_ref[...], xp, i * BLOCK)
        mixed = _dot(diff, w_ref[...])
        dac = dac_ref[...]
        dan = jnp.where(i < nb - 1, dan_ref[...], 0.0)
        dmix = jnp.concatenate([dac, dan], axis=0) * sc_ref[...]
        ddiff = _dot(dmix, w_ref[...], 1, 1)
        e = ddiff / _pool_count(i * BLOCK, n)
        f2 = e + pltpu.roll(e, n - 1, 0)
        f4 = f2 + pltpu.roll(f2, n - 2, 0)
        f8 = f4 + pltpu.roll(f4, n - 4, 0)
        f16 = f8 + pltpu.roll(f8, n - 8, 0)
        lane = lax.broadcasted_iota(jnp.int32, (n, POOL_W), 1)
        back = _pool_lane_select(lane, f2, f4, f8, f16)
        dx_ref[...] = (back[:BLOCK] - ddiff[:BLOCK]).astype(BF16)
        dw = _dot(diff, dmix[:BLOCK], 0, 0)
        dsc = jnp.sum(dac * mixed, axis=0, keepdims=True)

        @pl.when(i == 0)
        def _():
            dw_ref[...] = dw
            dsc_ref[...] = dsc

        @pl.when(i > 0)
        def _():
            dw_ref[...] += dw
            dsc_ref[...] += dsc

    blk = pl.BlockSpec((BLOCK, POOL_W), lambda i: (i, 0))
    return pl.pallas_call(
        body, name=name, grid=(nb,),
        in_specs=[blk, pl.BlockSpec((HALO_POOL, POOL_W), lambda i: (jnp.maximum(i * per - 1, 0), 0)),
                  blk, pl.BlockSpec((HALO_POOL, POOL_W), lambda i: (jnp.minimum((i + 1) * per, nb * per - 1), 0)),
                  pl.BlockSpec((POOL_W, POOL_W), lambda i: (0, 0)), pl.BlockSpec((1, POOL_W), lambda i: (0, 0))],
        out_specs=[blk, pl.BlockSpec((POOL_W, POOL_W), lambda i: (0, 0)), pl.BlockSpec((1, POOL_W), lambda i: (0, 0))],
        out_shape=[jax.ShapeDtypeStruct((s, POOL_W), BF16), jax.ShapeDtypeStruct((POOL_W, POOL_W), F32),
                   jax.ShapeDtypeStruct((1, POOL_W), F32)],
        compiler_params=_params("arbitrary"),
    )(z, z, da, da, bdw, scale.reshape(1, POOL_W))


def _attn_setup(zc_ref, zp_ref, cc_ref, cp_ref, sc_ref, sp_ref, qn_ref, kn_ref, bd):
    q = []
    for j in range(ATTN_W // LANES):
        t = zc_ref[:, POOL_W + j * LANES:POOL_W + (j + 1) * LANES]
        q.append((t,) + _norm_rope(t, qn_ref[...], cc_ref[...], sc_ref[...], bd))
    kc_raw = zc_ref[:, POOL_W + ATTN_W:POOL_W + ATTN_W + KV_W]
    kc = _norm_rope(kc_raw, kn_ref[...], cc_ref[...], sc_ref[...], bd)[0]
    kp = _norm_rope(zp_ref[:, :KV_W], kn_ref[...], cp_ref[...], sp_ref[...], bd)[0]
    kband = jnp.concatenate([kp, kc], axis=0).astype(BF16)
    vband = jnp.concatenate([zp_ref[:, KV_W:], zc_ref[:, POOL_W + ATTN_W + KV_W:POOL_W + ATTN_W + 2 * KV_W]], axis=0).astype(BF16)
    return q, kband, vband


MASKED = -1e30


def _window_bias(first_block):
    row = np.arange(Q_PER_KV * BLOCK)[:, None] % BLOCK
    col = np.arange(2 * BLOCK)[None, :]
    dist = row + BLOCK - col
    inside = (dist >= 0) & (dist < BLOCK) & ((col >= BLOCK) | (not first_block))
    return jnp.asarray(np.where(inside, 0.0, MASKED), F32)


def _stack_heads(tiles, kvh):
    return jnp.concatenate([_to_kv_lanes(tiles[h // 2], h) for h in range(kvh * Q_PER_KV, (kvh + 1) * Q_PER_KV)], axis=0)


def _unstack_heads(stacked, kvh, tiles):
    for g in range(Q_PER_KV):
        h = kvh * Q_PER_KV + g
        t = _from_kv_lanes(stacked[g * BLOCK:(g + 1) * BLOCK], h)
        tiles[h // 2] = t if tiles[h // 2] is None else tiles[h // 2] + t


def _sink_column(sink_ref, kvh):
    grp = lax.broadcasted_iota(jnp.int32, (Q_PER_KV * BLOCK, 1), 0) >> 7
    s = [sink_ref[kvh * Q_PER_KV + g] for g in range(Q_PER_KV)]
    return jnp.where(grp == 0, s[0], jnp.where(grp == 1, s[1], jnp.where(grp == 2, s[2], s[3])))


def _to_kv_lanes(t, h):
    kvh = h // Q_PER_KV
    if (h % 2) != kvh:
        t = pltpu.roll(t, HEAD_DIM, 1)
    lane = lax.broadcasted_iota(jnp.int32, t.shape, 1)
    return jnp.where((lane >= HEAD_DIM) == (kvh == 1), t, 0.0)


def _from_kv_lanes(t, h):
    kvh = h // Q_PER_KV
    lane = lax.broadcasted_iota(jnp.int32, t.shape, 1)
    t = jnp.where((lane >= HEAD_DIM) == (kvh == 1), t, 0.0)
    if (h % 2) != kvh:
        t = pltpu.roll(t, HEAD_DIM, 1)
    return t


def _attn_probs(qh, kband, bias, sink):
    sc = _dot(qh, kband, 1, 1) * (HEAD_DIM ** -0.5) + bias
    m = jnp.maximum(jnp.max(sc, axis=1, keepdims=True), sink)
    p = jnp.exp(sc - m)
    psink = jnp.exp(sink - m)
    den = jnp.sum(p, axis=1, keepdims=True) + psink
    return p / den, psink / den


def _attn_specs(nb):
    cur = lambda i: (i, 0)
    prev = lambda i: (jnp.maximum(i - 1, 0), 0)
    tab = lambda f: pl.BlockSpec((BLOCK, LANES), f)
    vec = pl.BlockSpec((1, LANES), lambda i: (0, 0))
    return [pl.BlockSpec((BLOCK, 1024), cur),
            pl.BlockSpec((BLOCK, 2 * KV_W), lambda i: (jnp.maximum(i - 1, 0), 3)),
            tab(cur), tab(prev), tab(cur), tab(prev), vec, vec,
            pl.BlockSpec(memory_space=pltpu.SMEM)] + [pl.BlockSpec((Q_PER_KV * BLOCK, 2 * BLOCK), lambda i: (0, 0))] * 2


def _attn_fwd(z, cosf, sinf, qn, kn, sinks, name):
    s = z.shape[0]
    nb = s // BLOCK

    def body(zc_ref, zp_ref, cc_ref, cp_ref, sc_ref, sp_ref, qn_ref, kn_ref, sink_ref, bias_ref, bias0_ref, o_ref):
        i = pl.program_id(0)
        bd = _head_mean_matrix()
        q, kband, vband = _attn_setup(zc_ref, zp_ref, cc_ref, cp_ref, sc_ref, sp_ref, qn_ref, kn_ref, bd)
        mask = jnp.where(i > 0, bias_ref[...], bias0_ref[...])
        out = [None] * (ATTN_W // LANES)
        for kvh in range(N_Q_HEADS // Q_PER_KV):
            qs = _stack_heads([t[1] for t in q], kvh)
            probs, _ = _attn_probs(qs, kband, mask, _sink_column(sink_ref, kvh))
            _unstack_heads(_dot(probs, vband), kvh, out)
        for j, o in enumerate(out):
            o_ref[:, j * LANES:(j + 1) * LANES] = o.astype(BF16)

    return pl.pallas_call(
        body, name=name, grid=(nb,), in_specs=_attn_specs(nb),
        out_specs=pl.BlockSpec((BLOCK, ATTN_W), lambda i: (i, 0)),
        out_shape=jax.ShapeDtypeStruct((s, ATTN_W), BF16), compiler_params=_params("parallel"),
    )(z, z, cosf, cosf, sinf, sinf, qn, kn, sinks, _window_bias(False), _window_bias(True))


def _attn_bwd(z, cosf, sinf, qn, kn, sinks, d_out, name):
    s = z.shape[0]
    nb = s // BLOCK
    nt = ATTN_W // LANES

    def body(zc_ref, zp_ref, cc_ref, cp_ref, sc_ref, sp_ref, qn_ref, kn_ref, sink_ref, bias_ref, bias0_ref, do_ref,
             dq_ref, dkc_ref, dkp_ref, dvc_ref, dvp_ref, dqn_ref, dsink_ref):
        i = pl.program_id(0)
        bd = _head_mean_matrix()
        q, kband, vband = _attn_setup(zc_ref, zp_ref, cc_ref, cp_ref, sc_ref, sp_ref, qn_ref, kn_ref, bd)
        mask = jnp.where(i > 0, bias_ref[...], bias0_ref[...])

        @pl.when(i == 0)
        def _():
            dqn_ref[...] = jnp.zeros_like(dqn_ref)
            dsink_ref[...] = jnp.zeros_like(dsink_ref)

        dq = [None] * nt
        dk = jnp.zeros((2 * BLOCK, KV_W), F32)
        dv = jnp.zeros((2 * BLOCK, KV_W), F32)
        d_tiles = [do_ref[:, j * LANES:(j + 1) * LANES] for j in range(nt)]
        for kvh in range(N_Q_HEADS // Q_PER_KV):
            qs = _stack_heads([t[1] for t in q], kvh)
            probs, psink = _attn_probs(qs, kband, mask, _sink_column(sink_ref, kvh))
            dos = _stack_heads(d_tiles, kvh)
            dp = _dot(dos, vband, 1, 1)
            delta = jnp.sum(dp * probs, axis=1, keepdims=True)
            ds = (probs * (dp - delta)) * (HEAD_DIM ** -0.5)
            dsink = -psink * delta
            for g in range(Q_PER_KV):
                h = kvh * Q_PER_KV + g
                dsink_ref[h:h + 1, :] += jnp.broadcast_to(jnp.sum(dsink[g * BLOCK:(g + 1) * BLOCK], axis=0, keepdims=True), (1, LANES))
            _unstack_heads(_dot(ds, kband), kvh, dq)
            dk = dk + _dot(ds, qs, 0, 0)
            dv = dv + _dot(probs, dos, 0, 0)
        dgn = jnp.zeros((1, LANES), F32)
        for j in range(nt):
            t, _, n, r = q[j]
            dt, g = _norm_rope_bwd(dq[j], t, n, r, qn_ref[...], cc_ref[...], sc_ref[...], bd)
            dq_ref[:, j * LANES:(j + 1) * LANES] = dt.astype(BF16)
            dgn = dgn + g
        dqn_ref[...] += jnp.broadcast_to(dgn, (8, LANES))
        dkp_ref[...] = dk[:BLOCK]
        dkc_ref[...] = dk[BLOCK:]
        dvp_ref[...] = dv[:BLOCK]
        dvc_ref[...] = dv[BLOCK:]

        @pl.when(i == nb - 1)
        def _():
            acc = dqn_ref[...]
            dqn_ref[...] = acc + pltpu.roll(acc, HEAD_DIM, 1)

    blk = lambda w: pl.BlockSpec((BLOCK, w), lambda i: (i, 0))
    acc = pl.BlockSpec((8, LANES), lambda i: (0, 0))
    kv = jax.ShapeDtypeStruct((s, KV_W), F32)
    return pl.pallas_call(
        body, name=name, grid=(nb,), in_specs=_attn_specs(nb) + [blk(ATTN_W)],
        out_specs=[blk(ATTN_W), blk(KV_W), blk(KV_W), blk(KV_W), blk(KV_W), acc, acc],
        out_shape=[jax.ShapeDtypeStruct((s, ATTN_W), BF16), kv, kv, kv, kv,
                   jax.ShapeDtypeStruct((8, LANES), F32), jax.ShapeDtypeStruct((8, LANES), F32)],
        compiler_params=_params("arbitrary"),
    )(z, z, cosf, cosf, sinf, sinf, qn, kn, sinks, _window_bias(False), _window_bias(True), d_out)


def _kv_post(z, cosf, sinf, kn, dkc, dkp, dvc, dvp, dxp, dq, duv, dz, name):
    s = z.shape[0]
    nb = s // BLOCK

    def body(zk_ref, c_ref, s_ref, kn_ref, dkc_ref, dkp_ref, dvc_ref, dvp_ref, dxp_ref, dq_ref, duv_ref, dz_in,
             dz_ref, dkn_ref):
        j = pl.program_id(0)
        bd = _head_mean_matrix()
        last = j == nb - 1
        d = dkc_ref[...] + jnp.where(last, 0.0, dkp_ref[...])
        t = zk_ref[:, :KV_W]
        _, n, r = _norm_rope(t, kn_ref[...], c_ref[...], s_ref[...], bd)
        dt, g = _norm_rope_bwd(d, t, n, r, kn_ref[...], c_ref[...], s_ref[...], bd)
        dvv = dvc_ref[...] + jnp.where(last, 0.0, dvp_ref[...])
        dz_ref[:, 0:POOL_W] = dxp_ref[...]
        dz_ref[:, POOL_W:POOL_W + ATTN_W] = dq_ref[...]
        dz_ref[:, POOL_W + ATTN_W:POOL_W + ATTN_W + KV_W] = dt.astype(BF16)
        dz_ref[:, POOL_W + ATTN_W + KV_W:POOL_W + ATTN_W + 2 * KV_W] = dvv.astype(BF16)
        dz_ref[:, POOL_W + ATTN_W + 2 * KV_W:GATE_COL0] = duv_ref[...]

        @pl.when(j == 0)
        def _():
            dkn_ref[...] = jnp.zeros_like(dkn_ref)

        dkn_ref[...] += jnp.broadcast_to(g, (8, LANES))

        @pl.when(last)
        def _():
            acc = dkn_ref[...]
            dkn_ref[...] = acc + pltpu.roll(acc, HEAD_DIM, 1)

    cur = lambda w: pl.BlockSpec((BLOCK, w), lambda j: (j, 0))
    nxt = pl.BlockSpec((BLOCK, KV_W), lambda j: (jnp.minimum(j + 1, nb - 1), 0))
    vec = pl.BlockSpec((1, LANES), lambda j: (0, 0))
    return pl.pallas_call(
        body, name=name, grid=(nb,),
        in_specs=[pl.BlockSpec((BLOCK, 2 * KV_W), lambda j: (j, 3)), cur(LANES), cur(LANES), vec,
                  cur(KV_W), nxt, cur(KV_W), nxt, cur(POOL_W), cur(ATTN_W), cur(2 * SGU_W),
                  pl.BlockSpec(memory_space=pl.ANY)],
        out_specs=[pl.BlockSpec((BLOCK, GATE_COL0), lambda j: (j, 0)), pl.BlockSpec((8, LANES), lambda j: (0, 0))],
        out_shape=[jax.ShapeDtypeStruct(dz.shape, dz.dtype), jax.ShapeDtypeStruct((8, LANES), F32)],
        input_output_aliases={11: 0}, compiler_params=_params("arbitrary"),
    )(z, cosf, sinf, kn, dkc, dkp, dvc, dvp, dxp, dq, duv, dz)


def _sgu_setup(z_ref, ws_ref, vn_ref, bd):
    us = z_ref[:, :SGU_W]
    vs = z_ref[:, SGU_W:]
    ug, tu = _gelu(us)
    gv, tv = _gelu(vs)
    rr = jnp.concatenate([lax.rsqrt(_head_mean(gv[:, k * LANES:(k + 1) * LANES] ** 2, bd) + EPS) for k in range(2)], axis=1)
    vg = (gv * rr) * vn_ref[...]
    tril = lax.broadcasted_iota(jnp.int32, (BLOCK, BLOCK), 0) >= lax.broadcasted_iota(jnp.int32, (BLOCK, BLOCK), 1)
    w = [jnp.where(tril, ws_ref[g], 0.0).astype(BF16) for g in range(4)]
    return us, vs, ug, tu, gv, tv, rr, vg, w, tril


def _group_select(parts):
    lane = lax.broadcasted_iota(jnp.int32, parts[0].shape, 1)
    return _pool_lane_select(lane, *parts)


def _sgu_fwd(z, ws, bcol, vn, name):
    s = z.shape[0]
    nb = s // BLOCK

    def body(z_ref, ws_ref, b_ref, vn_ref, c_ref):
        bd = _head_mean_matrix()
        _, _, ug, _, _, _, _, vg, w, _ = _sgu_setup(z_ref, ws_ref, vn_ref, bd)
        sg = _group_select([_dot(w[g], vg) for g in range(4)]) + b_ref[...]
        c_ref[...] = (ug * sg).astype(BF16)

    return pl.pallas_call(
        body, name=name, grid=(nb,),
        in_specs=[pl.BlockSpec((BLOCK, 2 * SGU_W), lambda i: (i, 2)), pl.BlockSpec((4, BLOCK, BLOCK), lambda i: (0, 0, 0)),
                  pl.BlockSpec((BLOCK, SGU_W), lambda i: (0, 0)), pl.BlockSpec((1, SGU_W), lambda i: (0, 0))],
        out_specs=pl.BlockSpec((BLOCK, SGU_W), lambda i: (i, 0)),
        out_shape=jax.ShapeDtypeStruct((s, SGU_W), BF16), compiler_params=_params("parallel"),
    )(z, ws, bcol, vn)


def _sgu_bwd(z, ws, bcol, vn, dc, name):
    s = z.shape[0]
    nb = s // BLOCK

    def body(z_ref, ws_ref, b_ref, vn_ref, dc_ref, duv_ref, dws_ref, db_ref, dvn_ref):
        i = pl.program_id(0)
        bd = _head_mean_matrix()
        us, vs, ug, tu, gv, tv, rr, vg, w, tril = _sgu_setup(z_ref, ws_ref, vn_ref, bd)
        sg = _group_select([_dot(w[g], vg) for g in range(4)]) + b_ref[...]
        dcv = dc_ref[...]
        dug = dcv * sg
        dsg = dcv * ug
        lane = lax.broadcasted_iota(jnp.int32, dsg.shape, 1)

        @pl.when(i == 0)
        def _():
            dws_ref[...] = jnp.zeros_like(dws_ref)
            db_ref[...] = jnp.zeros_like(db_ref)
            dvn_ref[...] = jnp.zeros_like(dvn_ref)

        for g in range(4):
            dsg_g = jnp.where((lane >= g * HEAD_DIM) & (lane < (g + 1) * HEAD_DIM), dsg, 0.0)
            dws_ref[g] += jnp.where(tril, _dot(dsg_g, vg, 1, 1), 0.0)
        dvg = _group_select([_dot(w[g], dsg, 0, 0) for g in range(4)])
        db_ref[...] += dsg
        n = gv * rr
        part = jnp.sum(dvg * n, axis=0, keepdims=True)
        dvn_ref[...] += jnp.broadcast_to(part[:, :LANES] + part[:, LANES:], (8, LANES))
        u = dvg * vn_ref[...]
        tu_ = gv * u
        mean = jnp.concatenate([_head_mean(tu_[:, k * LANES:(k + 1) * LANES], bd) for k in range(2)], axis=1)
        dgv = rr * u - gv * (rr * rr * rr) * mean
        duv_ref[:, :SGU_W] = (dug * _gelu_grad(us, tu)).astype(BF16)
        duv_ref[:, SGU_W:] = (dgv * _gelu_grad(vs, tv)).astype(BF16)

        @pl.when(i == nb - 1)
        def _():
            acc = dvn_ref[...]
            dvn_ref[...] = acc + pltpu.roll(acc, HEAD_DIM, 1)
            for k in range(2):
                db_ref[:, k * LANES:(k + 1) * LANES] = _head_mean(db_ref[:, k * LANES:(k + 1) * LANES], bd) * float(HEAD_DIM)

    return pl.pallas_call(
        body, name=name, grid=(nb,),
        in_specs=[pl.BlockSpec((BLOCK, 2 * SGU_W), lambda i: (i, 2)), pl.BlockSpec((4, BLOCK, BLOCK), lambda i: (0, 0, 0)),
                  pl.BlockSpec((BLOCK, SGU_W), lambda i: (0, 0)), pl.BlockSpec((1, SGU_W), lambda i: (0, 0)),
                  pl.BlockSpec((BLOCK, SGU_W), lambda i: (i, 0))],
        out_specs=[pl.BlockSpec((BLOCK, 2 * SGU_W), lambda i: (i, 0)), pl.BlockSpec((4, BLOCK, BLOCK), lambda i: (0, 0, 0)),
                   pl.BlockSpec((BLOCK, SGU_W), lambda i: (0, 0)), pl.BlockSpec((8, LANES), lambda i: (0, 0))],
        out_shape=[jax.ShapeDtypeStruct((s, 2 * SGU_W), BF16), jax.ShapeDtypeStruct((4, BLOCK, BLOCK), F32),
                   jax.ShapeDtypeStruct((BLOCK, SGU_W), F32), jax.ShapeDtypeStruct((8, LANES), F32)],
        compiler_params=_params("arbitrary"),
    )(z, ws, bcol, vn, dc)


MERGE_TN = 512
MERGE_TM = 1024


def _merge_fwd(a, b, c, wpa, wpb, wpc, z, name):
    s = z.shape[0]
    tm = _tile(s, (MERGE_TM, BLOCK))
    gate0 = GATE_COL0 // MERGE_TN

    def body(a_ref, b_ref, c_ref, wa_ref, wb_ref, wc_ref, g0_ref, g1_ref, g2_ref, o_ref):
        r = _sigmoid(g0_ref[...]) * _dot(a_ref[...], wa_ref[...])
        r = r + _sigmoid(g1_ref[...]) * _dot(b_ref[...], wb_ref[...])
        r = r + _sigmoid(g2_ref[...]) * _dot(c_ref[...], wc_ref[...])
        o_ref[...] = r.astype(BF16)

    x_spec = lambda w: pl.BlockSpec((tm, w), lambda i, n: (i, 0))
    w_spec = lambda w: pl.BlockSpec((w, MERGE_TN), lambda i, n: (0, n))
    g_spec = lambda br: pl.BlockSpec((tm, MERGE_TN), lambda i, n: (i, gate0 + 2 * br + n))
    return pl.pallas_call(
        body, name=name, grid=(s // tm, D_MODEL // MERGE_TN),
        in_specs=[x_spec(POOL_W), x_spec(ATTN_W), x_spec(SGU_W), w_spec(POOL_W), w_spec(ATTN_W), w_spec(SGU_W),
                  g_spec(0), g_spec(1), g_spec(2)],
        out_specs=pl.BlockSpec((tm, MERGE_TN), lambda i, n: (i, n)),
        out_shape=jax.ShapeDtypeStruct((s, D_MODEL), BF16), compiler_params=_params("parallel", "parallel"),
    )(a, b, c, wpa, wpb, wpc, z, z, z)


def _branch_bwd(br, xb, wp, z, dm, dz, name):
    s = z.shape[0]
    kb = xb.shape[1]
    tm = _tile(s, (MERGE_TM, BLOCK))
    gate0 = GATE_COL0 // MERGE_TN
    aliased = dz is not None

    def body(*refs):
        x_ref, w_ref, g_ref, dm_ref = refs[:4]
        dz_ref, dy_ref, dx_ref = refs[-3:]
        n = pl.program_id(1)
        y = _dot(x_ref[...], w_ref[...])
        sg = _sigmoid(g_ref[...])
        dmv = dm_ref[...]
        dy = (dmv * sg).astype(BF16)
        dy_ref[...] = dy
        dz_ref[...] = ((dmv * y) * (sg * (1.0 - sg))).astype(BF16)
        dx = _dot(dy, w_ref[...], 1, 1)

        @pl.when(n == 0)
        def _():
            dx_ref[...] = dx

        @pl.when(n > 0)
        def _():
            dx_ref[...] += dx

    in_specs = [pl.BlockSpec((tm, kb), lambda i, n: (i, 0)), pl.BlockSpec((kb, MERGE_TN), lambda i, n: (0, n)),
                pl.BlockSpec((tm, MERGE_TN), lambda i, n: (i, gate0 + 2 * br + n)),
                pl.BlockSpec((tm, MERGE_TN), lambda i, n: (i, n))]
    args = [xb, wp, z, dm]
    if aliased:
        in_specs.append(pl.BlockSpec(memory_space=pl.ANY))
        args.append(dz)
    return pl.pallas_call(
        body, name=name, grid=(s // tm, D_MODEL // MERGE_TN), in_specs=in_specs,
        out_specs=[pl.BlockSpec((tm, MERGE_TN), lambda i, n: (i, gate0 + 2 * br + n)),
                   pl.BlockSpec((tm, MERGE_TN), lambda i, n: (i, n)),
                   pl.BlockSpec((tm, kb), lambda i, n: (i, 0))],
        out_shape=[jax.ShapeDtypeStruct((s, IN_COLS), BF16), jax.ShapeDtypeStruct((s, D_MODEL), BF16),
                   jax.ShapeDtypeStruct((s, kb), F32)],
        input_output_aliases={4: 0} if aliased else {},
        compiler_params=_params("parallel", "arbitrary"),
    )(*args)


FFN_TM = 256
FFN_TC = 2816
FFN_STRIP = 256
FFN_UNROLL = 4


def _conv3(cur, prev, w_ref, b_ref):
    cat = jnp.concatenate([prev, cur], axis=0)
    x1 = pltpu.roll(cat, 1, 0)[HALO_CONV:]
    x2 = pltpu.roll(cat, 2, 0)[HALO_CONV:]
    return w_ref[0:1, :] * x2 + w_ref[1:2, :] * x1 + w_ref[2:3, :] * cur + b_ref[...], x1, x2


def _ffn_specs(s, tm, rows_first):
    per = tm // HALO_CONV
    if rows_first:
        cur = pl.BlockSpec((tm, FFN_TC), lambda i, j: (i, j))
        prev = pl.BlockSpec((HALO_CONV, FFN_TC), lambda i, j: (jnp.maximum(i * per - 1, 0), j))
        w = pl.BlockSpec((3, FFN_TC), lambda i, j: (0, j))
        b = pl.BlockSpec((1, FFN_TC), lambda i, j: (0, j))
    else:
        cur = pl.BlockSpec((tm, FFN_TC), lambda j, i: (i, j))
        prev = pl.BlockSpec((HALO_CONV, FFN_TC), lambda j, i: (jnp.maximum(i * per - 1, 0), j))
        w = pl.BlockSpec((3, FFN_TC), lambda j, i: (0, j))
        b = pl.BlockSpec((1, FFN_TC), lambda j, i: (0, j))
    return cur, prev, w, b


def _ffn_act_fwd(upg, upv, cwg, cwv, cbg, cbv, name):
    s = upg.shape[0]
    tm = _tile(s, (FFN_TM, BLOCK))
    cur, prev, w, b = _ffn_specs(s, tm, True)

    def body(g_ref, gp_ref, v_ref, vp_ref, wg_ref, wv_ref, bg_ref, bv_ref, o_ref):
        first = pl.program_id(0) == 0
        gate = _conv3(g_ref[...], jnp.where(first, 0.0, gp_ref[...]), wg_ref, bg_ref)[0]
        val = _conv3(v_ref[...], jnp.where(first, 0.0, vp_ref[...]), wv_ref, bv_ref)[0]
        o_ref[...] = ((gate * _sigmoid(gate)) * val).astype(BF16)

    return pl.pallas_call(
        body, name=name, grid=(s // tm, D_FF // FFN_TC), in_specs=[cur, prev, cur, prev, w, w, b, b], out_specs=cur,
        out_shape=jax.ShapeDtypeStruct((s, D_FF), BF16), compiler_params=_params("parallel", "parallel"),
    )(upg, upg, upv, upv, cwg, cwv, cbg, cbv)


DOWN_TM = 256
DOWN_TK = 2816
DOWN_CHUNK = 256


def _ffn_down_fwd(upg, upv, cwg, cwv, cbg, cbv, w_down, res, name):
    s = upg.shape[0]
    d = w_down.shape[1]
    tm = _tile(s, (DOWN_TM, DOWN_CHUNK, BLOCK))
    chunk = min(DOWN_CHUNK, tm)
    per = tm // HALO_CONV
    nk = D_FF // DOWN_TK

    def body(g_ref, gp_ref, v_ref, vp_ref, wg_ref, wv_ref, bg_ref, bv_ref, w_ref, res_ref, o_ref, act_ref, acc_ref):
        first = pl.program_id(0) == 0
        kk = pl.program_id(1)

        @pl.when(kk == 0)
        def _():
            acc_ref[...] = jnp.zeros_like(acc_ref)

        for c in range(tm // chunk):
            rows = slice(c * chunk, (c + 1) * chunk)
            before = slice(c * chunk - HALO_CONV, c * chunk)
            g_p = jnp.where(first, 0.0, gp_ref[...]) if c == 0 else g_ref[before, :]
            v_p = jnp.where(first, 0.0, vp_ref[...]) if c == 0 else v_ref[before, :]
            gate = _conv3(g_ref[rows, :], g_p, wg_ref, bg_ref)[0]
            val = _conv3(v_ref[rows, :], v_p, wv_ref, bv_ref)[0]
            act = ((gate * _sigmoid(gate)) * val).astype(BF16)
            act_ref[rows, :] = act
            acc_ref[rows, :] += jnp.dot(act, w_ref[...], preferred_element_type=F32)

        @pl.when(kk == nk - 1)
        def _():
            o_ref[...] = acc_ref[...] + res_ref[...]

    cur = pl.BlockSpec((tm, DOWN_TK), lambda i, kk: (i, kk))
    prev = pl.BlockSpec((HALO_CONV, DOWN_TK), lambda i, kk: (jnp.maximum(i * per - 1, 0), kk))
    w = pl.BlockSpec((3, DOWN_TK), lambda i, kk: (0, kk))
    b = pl.BlockSpec((1, DOWN_TK), lambda i, kk: (0, kk))
    row = pl.BlockSpec((tm, d), lambda i, kk: (i, 0))
    return pl.pallas_call(
        body, name=name, grid=(s // tm, nk),
        in_specs=[cur, prev, cur, prev, w, w, b, b, pl.BlockSpec((DOWN_TK, d), lambda i, kk: (kk, 0)), row],
        out_specs=[row, cur], out_shape=[jax.ShapeDtypeStruct((s, d), F32), jax.ShapeDtypeStruct((s, D_FF), BF16)],
        scratch_shapes=[pltpu.VMEM((tm, d), F32)], compiler_params=_params("parallel", "arbitrary"),
    )(upg, upg, upv, upv, cwg, cwv, cbg, cbv, w_down, res)


def _ffn_bwd(upg, upv, cwg, cwv, cbg, cbv, dact, name):
    s = upg.shape[0]
    tm = _tile(s, (FFN_TM, BLOCK))
    per = tm // HALO_CONV
    nrow = s // tm
    n = tm + HALO_CONV
    cur, prev, w, b = _ffn_specs(s, tm, False)
    nxt = pl.BlockSpec((HALO_CONV, FFN_TC), lambda j, i: (jnp.minimum((i + 1) * per, nrow * per - 1), j))

    nch = tm // 8
    rows8 = lambda r: pl.ds(pl.multiple_of(r * 8, 8), 8)

    def body(g_ref, gp_ref, gn_ref, v_ref, vp_ref, vn_ref, wg_ref, wv_ref, bg_ref, bv_ref, da_ref, dan_ref,
             dg_ref, dv_ref, dwg_ref, dwv_ref, og_ref, ov_ref):
        i = pl.program_id(1)
        first = i == 0
        last = i == nrow - 1
        row = lax.broadcasted_iota(jnp.int32, (8, FFN_STRIP), 0)

        keep_down = {k: row >= k for k in (1, 2)}
        keep_up = {k: row < 8 - k for k in (1, 2)}

        def down(cur, prev, k):
            return jnp.where(keep_down[k], pltpu.roll(cur, k, 0), pltpu.roll(prev, k, 0))

        def up(cur, nxt, k):
            return jnp.where(keep_up[k], pltpu.roll(cur, 8 - k, 0), pltpu.roll(nxt, 8 - k, 0))

        @pl.when(first)
        def _():
            dwg_ref[...] = jnp.zeros_like(dwg_ref)
            dwv_ref[...] = jnp.zeros_like(dwv_ref)

        for c in range(FFN_TC // FFN_STRIP):
            cols = slice(c * FFN_STRIP, (c + 1) * FFN_STRIP)
            wg = [functools.partial(lambda k: wg_ref[k:k + 1, cols], k) for k in range(3)]
            wv = [functools.partial(lambda k: wv_ref[k:k + 1, cols], k) for k in range(3)]

            def conv_grads(g_cur, g_prev, v_cur, v_prev, da):
                gate = wg[0]() * down(g_cur, g_prev, 2) + wg[1]() * down(g_cur, g_prev, 1) + wg[2]() * g_cur + bg_ref[:, cols]
                val = wv[0]() * down(v_cur, v_prev, 2) + wv[1]() * down(v_cur, v_prev, 1) + wv[2]() * v_cur + bv_ref[:, cols]
                sg = _sigmoid(gate)
                return (da * val) * (sg * (1.0 + gate * (1.0 - sg))), da * (gate * sg)

            def passes(q, carry):
                for u in range(FFN_UNROLL):
                    carry = one_pass(q * FFN_UNROLL + u, carry)
                return carry

            def one_pass(r, carry, tile_end=False):
                dg_cur, dv_cur, acc = carry
                g_r, v_r = g_ref[rows8(r), cols], v_ref[rows8(r), cols]
                if tile_end:
                    g_n, v_n, da_n = gn_ref[:, cols], vn_ref[:, cols], jnp.where(last, 0.0, dan_ref[:, cols])
                else:
                    g_n, v_n, da_n = g_ref[rows8(r + 1), cols], v_ref[rows8(r + 1), cols], da_ref[rows8(r + 1), cols]
                dg_n, dv_n = conv_grads(g_n, g_r, v_n, v_r, da_n)
                new_acc = []
                for o_ref, w, d_cur, d_n, x0, a in ((og_ref, wg, dg_cur, dg_n, g_r, acc[:4]), (ov_ref, wv, dv_cur, dv_n, v_r, acc[4:])):
                    d1, d2 = up(d_cur, d_n, 1), up(d_cur, d_n, 2)
                    o_ref[rows8(r), cols] = w[2]() * d_cur + w[1]() * d1 + w[0]() * d2
                    new_acc += [a[0] + d2 * x0, a[1] + d1 * x0, a[2] + d_cur * x0, a[3] + d_cur]
                return dg_n, dv_n, tuple(new_acc)

            g_p = jnp.where(first, 0.0, gp_ref[:, cols])
            v_p = jnp.where(first, 0.0, vp_ref[:, cols])
            dg0, dv0 = conv_grads(g_ref[0:8, cols], g_p, v_ref[0:8, cols], v_p, da_ref[0:8, cols])
            zero = jnp.zeros((8, FFN_STRIP), F32)
            carry = lax.fori_loop(0, nch // FFN_UNROLL - 1, passes, (dg0, dv0, (zero,) * 8))
            for r in range(nch - FFN_UNROLL, nch):
                carry = one_pass(r, carry, tile_end=r == nch - 1)
            for ref, a in ((dwg_ref, carry[2][:4]), (dwv_ref, carry[2][4:])):
                for k in range(4):
                    ref[k:k + 1, cols] += jnp.sum(a[k], axis=0, keepdims=True)
        dg_ref[...] = og_ref[...].astype(BF16)
        dv_ref[...] = ov_ref[...].astype(BF16)

    acc = pl.BlockSpec((8, FFN_TC), lambda j, i: (0, j))
    full = jax.ShapeDtypeStruct((s, D_FF), BF16)
    accs = jax.ShapeDtypeStruct((8, D_FF), F32)
    return pl.pallas_call(
        body, name=name, grid=(D_FF // FFN_TC, nrow), in_specs=[cur, prev, nxt, cur, prev, nxt, w, w, b, b, cur, nxt],
        out_specs=[cur, cur, acc, acc], out_shape=[full, full, accs, accs],
        scratch_shapes=[pltpu.VMEM((tm, FFN_TC), F32), pltpu.VMEM((tm, FFN_TC), F32)],
        compiler_params=_params("parallel", "arbitrary"),
    )(upg, upg, upg, upv, upv, upv, cwg, cwv, cbg, cbv, dact, dact)


def _mesh_place():
    return lax.axis_index("x"), lax.axis_index("y"), lax.axis_index("c")


def _all_gather(shards, name):
    na = len(shards)

    def body(*refs):
        x_refs, out_refs = refs[:na], refs[na:2 * na]
        send_sems, recv_sems, local_sems = refs[2 * na:]
        x, y, cc = _mesh_place()
        me, sibling = (x, y, cc), (x, y, 1 - cc)
        chips = [(1 - x, y), (x, 1 - y), (1 - x, 1 - y)]

        def copy(k, a, block, to, from_input=False):
            slot = out_refs[a].at[4 * block[0] + 2 * block[1] + block[2]]
            return pltpu.make_async_remote_copy(
                src_ref=x_refs[a] if from_input else slot, dst_ref=slot, send_sem=send_sems.at[k * na + a],
                recv_sem=recv_sems.at[k * na + a], device_id=to, device_id_type=pl.DeviceIdType.MESH)

        mine = [pltpu.make_async_copy(x_refs[a], out_refs[a].at[4 * x + 2 * y + cc], local_sems.at[a]) for a in range(na)]
        for cp in mine:
            cp.start()
        first = [copy(0, a, me, sibling, True) for a in range(na)]
        first += [copy(1 + j, a, me, (*chip, cc), True) for j, chip in enumerate(chips) for a in range(na)]
        for cp in first:
            cp.start()
        passed = []
        for j, chip in enumerate(chips):
            for a in range(na):
                copy(1 + j, a, (*chip, cc), me).wait_recv()
                passed.append(copy(4 + j, a, (*chip, cc), sibling))
                passed[-1].start()
        for a in range(na):
            copy(0, a, sibling, me).wait_recv()
        for j, chip in enumerate(chips):
            for a in range(na):
                copy(4 + j, a, (*chip, 1 - cc), me).wait_recv()
        for cp in first + passed:
            cp.wait_send()
        for cp in mine:
            cp.wait()

    hbm = pl.BlockSpec(memory_space=pl.ANY)
    return pl.pallas_call(
        body, name=name, out_shape=[jax.ShapeDtypeStruct((N_DEV,) + t.shape, t.dtype) for t in shards],
        in_specs=[hbm] * na, out_specs=[hbm] * na,
        scratch_shapes=[pltpu.SemaphoreType.DMA((7 * na,)), pltpu.SemaphoreType.DMA((7 * na,)), pltpu.SemaphoreType.DMA((na,))],
    )(*shards)


def _exchange(blocks, name):
    na = len(blocks)

    def body(*refs):
        g_refs, out_refs = refs[:na], refs[na:2 * na]
        send_sems, recv_sems, local_sems = refs[2 * na:]
        x, y, cc = _mesh_place()
        me = 4 * x + 2 * y + cc
        mine = [pltpu.make_async_copy(g_refs[a].at[me], out_refs[a].at[me], local_sems.at[a]) for a in range(na)]
        for cp in mine:
            cp.start()
        sends, lands = [], []
        for k in range(1, N_DEV):
            px = 1 - x if (k >> 2) & 1 else x
            py = 1 - y if (k >> 1) & 1 else y
            pc = 1 - cc if k & 1 else cc
            peer = 4 * px + 2 * py + pc
            for a in range(na):
                sem = (k - 1) * na + a
                sends.append(pltpu.make_async_remote_copy(
                    src_ref=g_refs[a].at[peer], dst_ref=out_refs[a].at[me], send_sem=send_sems.at[sem], recv_sem=recv_sems.at[sem],
                    device_id=(px, py, pc), device_id_type=pl.DeviceIdType.MESH))
                lands.append(pltpu.make_async_remote_copy(
                    src_ref=g_refs[a].at[peer], dst_ref=out_refs[a].at[peer], send_sem=send_sems.at[sem], recv_sem=recv_sems.at[sem],
                    device_id=(px, py, pc), device_id_type=pl.DeviceIdType.MESH))
        for cp in sends:
            cp.start()
        for cp in lands:
            cp.wait_recv()
        for cp in sends:
            cp.wait_send()
        for cp in mine:
            cp.wait()

    hbm = pl.BlockSpec(memory_space=pl.ANY)
    return pl.pallas_call(
        body, name=name, out_shape=[jax.ShapeDtypeStruct(t.shape, t.dtype) for t in blocks],
        in_specs=[hbm] * na, out_specs=[hbm] * na,
        scratch_shapes=[pltpu.SemaphoreType.DMA((7 * na,)), pltpu.SemaphoreType.DMA((7 * na,)), pltpu.SemaphoreType.DMA((na,))],
    )(*blocks)


def _peer(k):
    x, y, cc = _mesh_place()
    px = 1 - x if (k >> 2) & 1 else x
    py = 1 - y if (k >> 1) & 1 else y
    pc = 1 - cc if k & 1 else cc
    return (px, py, pc), 4 * px + 2 * py + pc


def _push_copy(src_ref, land_ref, k, a, na, send_sems, recv_sems, indexed, landed):
    x, y, cc = _mesh_place()
    place, peer = _peer(k)
    sem = (k - 1) * na + a
    return pltpu.make_async_remote_copy(
        src_ref=src_ref.at[peer] if indexed else src_ref, dst_ref=land_ref.at[peer if landed else 4 * x + 2 * y + cc],
        send_sem=send_sems.at[sem], recv_sem=recv_sems.at[sem], device_id=place, device_id_type=pl.DeviceIdType.MESH)


_HBM = pl.BlockSpec(memory_space=pltpu.HBM)
_SEM = pl.BlockSpec(memory_space=pltpu.SEMAPHORE)
_EFFECT = pltpu.SideEffectType.DATAFLOW_SIDE_EFFECTING


def _push_start(srcs, indexed, name):
    na = len(srcs)
    lands = [lax.empty(t.shape if indexed else (N_DEV,) + t.shape, t.dtype) for t in srcs]

    def body(*refs):
        src_refs, land_refs = refs[:na], refs[na:2 * na]
        send_sems, recv_sems = refs[2 * na], refs[2 * na + 1]
        token = refs[-1]
        for k in range(1, N_DEV):
            for a in range(na):
                _push_copy(src_refs[a], land_refs[a], k, a, na, send_sems, recv_sems, indexed, False).start()
        token[...] = jnp.zeros_like(token)

    sems = pltpu.SemaphoreType.DMA((7 * na,))
    out = pl.pallas_call(
        body, name=name,
        out_shape=(sems, sems, *[pltpu.HBM(t.shape, t.dtype) for t in srcs], *[pltpu.HBM(t.shape, t.dtype) for t in lands],
                   jax.ShapeDtypeStruct((8, LANES), F32)),
        in_specs=[_HBM] * (2 * na), out_specs=(_SEM, _SEM, *[_HBM] * (2 * na), pl.BlockSpec(memory_space=pltpu.VMEM)),
        input_output_aliases={i: 2 + i for i in range(2 * na)},
        compiler_params=pltpu.CompilerParams(has_side_effects=_EFFECT),
    )(*[pltpu.with_memory_space_constraint(t, pltpu.HBM) for t in srcs + lands])
    return out[0], out[1], list(out[2:2 + na]), list(out[2 + na:2 + 2 * na]), out[-1]


def _push_wait(started, indexed, after, name):
    send_sems, recv_sems, srcs, lands, _ = started
    na = len(srcs)

    def body(*refs):
        src_refs, land_refs = refs[:na], refs[na:2 * na]
        send_sems, recv_sems = refs[2 * na], refs[2 * na + 1]
        for k in range(1, N_DEV):
            for a in range(na):
                copy = _push_copy(src_refs[a], land_refs[a], k, a, na, send_sems, recv_sems, indexed, True)
                copy.wait_send()
                copy.wait_recv()

    out = pl.pallas_call(
        body, name=name, out_shape=[pltpu.HBM(t.shape, t.dtype) for t in srcs + lands],
        in_specs=[_HBM] * (2 * na) + [_SEM, _SEM, pl.BlockSpec(memory_space=pl.ANY)], out_specs=[_HBM] * (2 * na),
        input_output_aliases={i: i for i in range(2 * na)},
        compiler_params=pltpu.CompilerParams(has_side_effects=_EFFECT),
    )(*srcs, *lands, send_sems, recv_sems, after)
    x, y, cc = _mesh_place()
    me = 4 * x + 2 * y + cc
    return [lax.dynamic_update_index_in_dim(
        land, lax.dynamic_index_in_dim(src, me, 0, keepdims=False) if indexed else src, me, 0)
        for src, land in zip(out[:na], out[na:])]


def _adamw_sum(parts, w, m, v, name):
    _, r, c = parts.shape
    tr = _tile(r, (256, 128, 64, 32, 16, 8))

    def body(p_ref, w_ref, m_ref, v_ref, g_ref, d_ref, nm_ref, nv_ref):
        _adam_store(_sum_parts(p_ref), w_ref, m_ref, v_ref, g_ref, d_ref, nm_ref, nv_ref)

    row = pl.BlockSpec((tr, c), lambda i: (i, 0))
    shp = jax.ShapeDtypeStruct((r, c), F32)
    return pl.pallas_call(
        body, name=name, grid=(r // tr,), in_specs=[pl.BlockSpec((N_DEV, tr, c), lambda i: (0, i, 0)), row, row, row],
        out_specs=[row, row, row, row], out_shape=[shp, shp, shp, shp], compiler_params=_params("parallel"),
    )(parts, w, m, v)


def _sum_parts(p_ref):
    g = p_ref[0].astype(F32)
    for k in range(1, N_DEV):
        g = g + p_ref[k].astype(F32)
    return g


def _adam_store(g, w_ref, m_ref, v_ref, g_ref, d_ref, nm_ref, nv_ref):
    nm = ADAM_B1 * m_ref[...] + (1.0 - ADAM_B1) * g
    nv = ADAM_B2 * v_ref[...] + (1.0 - ADAM_B2) * (g * g)
    m_hat = nm / (1.0 - ADAM_B1 ** ADAM_STEP)
    v_hat = nv / (1.0 - ADAM_B2 ** ADAM_STEP)
    g_ref[...] = g
    nm_ref[...] = nm
    nv_ref[...] = nv
    d_ref[...] = -ADAM_LR * (m_hat / (jnp.sqrt(v_hat) + ADAM_EPS) + ADAM_WD * w_ref[...])


def _adamw_weight(parts, w, m, v, name):
    _, r, c = w.shape
    tr = _tile(r, (256, 128, 176))
    nr = r // tr

    def body(p0_ref, p1_ref, w_ref, m_ref, v_ref, g_ref, d_ref, nm_ref, nv_ref):
        g = jnp.where(pl.program_id(0) == 0, _sum_parts(p0_ref), _sum_parts(p1_ref))
        _adam_store(g, w_ref, m_ref, v_ref, g_ref, d_ref, nm_ref, nv_ref)

    part = lambda layer: pl.BlockSpec(
        (N_DEV, tr, c), lambda l, i: (0, jnp.where(l == layer, i, (nr - 1) * (1 - layer)), 0))
    row = pl.BlockSpec((None, tr, c), lambda l, i: (l, i, 0))
    shp = jax.ShapeDtypeStruct(w.shape, F32)
    return pl.pallas_call(
        body, name=name, grid=(DEPTH, nr), in_specs=[part(0), part(1), row, row, row],
        out_specs=[row, row, row, row], out_shape=[shp, shp, shp, shp], compiler_params=_params("arbitrary", "arbitrary"),
    )(parts[0], parts[1], w, m, v)


def _full_to_slots(name, t):
    k, n = t.shape
    if name in ROW_SHARDED:
        return t.reshape(N_DEV, k // N_DEV, n)
    return t.reshape(k, N_DEV, n // N_DEV).transpose(1, 0, 2)


def _slots_to_full(name, t):
    _, r, c = t.shape
    if name in ROW_SHARDED:
        return t.reshape(N_DEV * r, c)
    return t.transpose(1, 0, 2).reshape(r, N_DEV * c)


def _small_sizes(shapes, names):
    return [(n, shapes[n], -(-int(math.prod(shapes[n])) // (8 * LANES)) * 8) for n in names]


def _pack_small(tree, shapes, names, row_tile):
    rows = []
    for n, shp, nrow in _small_sizes(shapes, names):
        flat = tree[n].reshape(-1)
        rows.append(jnp.pad(flat, (0, nrow * LANES - flat.shape[0])).reshape(nrow, LANES))
    total = sum(r.shape[0] for r in rows)
    if total % row_tile:
        rows.append(jnp.zeros((-total % row_tile, LANES), F32))
    return jnp.concatenate(rows, axis=0)


def _unpack_small(buf, shapes, names):
    out, r0 = {}, 0
    for n, shp, nrow in _small_sizes(shapes, names):
        out[n] = buf[r0:r0 + nrow].reshape(-1)[:int(math.prod(shp))].reshape(shp)
        r0 += nrow
    return out


def _block_diag(w):
    g = w.shape[0]
    eye = jnp.eye(g, dtype=w.dtype)
    return (eye[:, None, :, None] * w[:, :, None, :]).reshape(g * HEAD_DIM, g * HEAD_DIM)


def kernel(x, positions, norm1, w_in, q_norm, k_norm, sinks, w_pool, pool_scale, sgu_v_norm, w_s, b_s, w_proj_a, w_proj_b, w_proj_c, w_out, norm2, w_up, conv_w, conv_b, w_down, loss_target, m_norm1, m_w_in, m_q_norm, m_k_norm, m_sinks, m_w_pool, m_pool_scale, m_sgu_v_norm, m_w_s, m_b_s, m_w_proj_a, m_w_proj_b, m_w_proj_c, m_w_out, m_norm2, m_w_up, m_conv_w, m_conv_b, m_w_down, v_norm1, v_w_in, v_q_norm, v_k_norm, v_sinks, v_w_pool, v_pool_scale, v_sgu_v_norm, v_w_s, v_b_s, v_w_proj_a, v_w_proj_b, v_w_proj_c, v_w_out, v_norm2, v_w_up, v_conv_w, v_conv_b, v_w_down):
    names = ("norm1", "w_in", "q_norm", "k_norm", "sinks", "w_pool", "pool_scale", "sgu_v_norm", "w_s", "b_s", "w_proj_a",
             "w_proj_b", "w_proj_c", "w_out", "norm2", "w_up", "conv_w", "conv_b", "w_down")
    wts = dict(zip(names, (norm1, w_in, q_norm, k_norm, sinks, w_pool, pool_scale, sgu_v_norm, w_s, b_s, w_proj_a, w_proj_b,
                           w_proj_c, w_out, norm2, w_up, conv_w, conv_b, w_down)))
    mom = dict(zip(names, (m_norm1, m_w_in, m_q_norm, m_k_norm, m_sinks, m_w_pool, m_pool_scale, m_sgu_v_norm, m_w_s, m_b_s,
                           m_w_proj_a, m_w_proj_b, m_w_proj_c, m_w_out, m_norm2, m_w_up, m_conv_w, m_conv_b, m_w_down)))
    var = dict(zip(names, (v_norm1, v_w_in, v_q_norm, v_k_norm, v_sinks, v_w_pool, v_pool_scale, v_sgu_v_norm, v_w_s, v_b_s,
                           v_w_proj_a, v_w_proj_b, v_w_proj_c, v_w_out, v_norm2, v_w_up, v_conv_w, v_conv_b, v_w_down)))
    xs = x[0]
    target = loss_target[0]
    s = xs.shape[0]

    inv_freq = ROPE_THETA ** (-jnp.arange(0, HEAD_DIM, 2, dtype=F32) / HEAD_DIM)
    ang = positions[0].astype(F32)[:, None] * inv_freq
    cosf = jnp.tile(jnp.cos(ang), (1, 4))
    sinf = jnp.tile(jnp.concatenate([-jnp.sin(ang), jnp.sin(ang)], axis=1), (1, 2))

    local = [{n: wts[n][l] if n == "conv_w" else wts[n][l].astype(BF16) for n in SHARDED} for l in range(DEPTH)]
    later = SHARDED[1:]
    full = [{"w_in": _slots_to_full("w_in", _all_gather([local[0]["w_in"]], "gather_w_in_0")[0])}, None]
    gather0 = _push_start([local[0][n] for n in later], False, "gather_rest_0_start")
    norm1_first = norm1[0] + gather0[4][0, 0]

    def layer_consts(l):
        return dict(
            bdw=_block_diag(w_pool[l]).astype(BF16), qn=jnp.tile(q_norm[l], 2).reshape(1, LANES),
            kn=jnp.tile(k_norm[l], 2).reshape(1, LANES), vn=jnp.tile(sgu_v_norm[l], 4).reshape(1, SGU_W),
            bcol=jnp.repeat(b_s[l].T, HEAD_DIM, axis=1),
            cbg=conv_b[l][:D_FF].reshape(1, D_FF), cbv=conv_b[l][D_FF:].reshape(1, D_FF))

    gate_cols, val_cols = (0, D_FF), (D_FF, D_FF)

    saved = []
    cur = xs
    for l in range(DEPTH):
        if l == 1:
            landed = _push_wait(gather1, False, cur, "gather_weights_1_wait")
            full[1] = {n: _slots_to_full(n, t) for n, t in zip(SHARDED, landed)}
        fw, k = full[l], layer_consts(l)
        h1 = _rms_fwd(cur, norm1_first if l == 0 else norm1[l], f"rms1_fwd_{l}")
        z = _mm(h1, fw["w_in"], name=f"in_proj_{l}")
        a = _pool_fwd(z, k["bdw"], pool_scale[l], f"pool_fwd_{l}")
        b = _attn_fwd(z, cosf, sinf, k["qn"], k["kn"], sinks[l], f"attn_fwd_{l}")
        c = _sgu_fwd(z, w_s[l], k["bcol"], k["vn"], f"sgu_fwd_{l}")
        w_proj_a_l = fw.get("w_proj_a")
        if l == 0:
            landed = _push_wait(gather0, False, c, "gather_rest_0_wait")
            fw.update({n: _slots_to_full(n, t) for n, t in zip(later, landed)})
            gather1 = _push_start([local[1][n] for n in SHARDED], False, "gather_weights_1_start")
            w_proj_a_l = fw["w_proj_a"] + gather1[4][0, 0].astype(BF16)
        merged = _merge_fwd(a, b, c, w_proj_a_l, fw["w_proj_b"], fw["w_proj_c"], z, f"merge_fwd_{l}")
        x1 = _mm(merged, fw["w_out"], add=cur, name=f"out_proj_{l}")
        h2 = _rms_fwd(x1, norm2[l], f"rms2_fwd_{l}")
        upg = _mm(h2, fw["w_up"], b_n=gate_cols, name=f"up_gate_{l}")
        upv = _mm(h2, fw["w_up"], b_n=val_cols, name=f"up_val_{l}")
        k["cwg"], k["cwv"] = fw["conv_w"][:, :D_FF], fw["conv_w"][:, D_FF:]
        x2, act = _ffn_down_fwd(upg, upv, k["cwg"], k["cwv"], k["cbg"], k["cbv"], fw["w_down"], x1, f"ffn_down_fwd_{l}")
        saved.append(dict(x0=cur, h1=h1, z=z, a=a, b=b, c=c, merged=merged, x1=x1, h2=h2, upg=upg, upv=upv, act=act))
        cur = x2

    dcur, loss_tile = _loss_head(cur, target)
    loss = lax.psum(loss_tile[0, 0], ("x", "y", "c"))

    gsmall = [None] * DEPTH
    small_shapes = {n: wts[n].shape for n in SMALL}

    def slots_of(grads):
        return [_full_to_slots(n, t) for n, t in grads.items()]

    for l in reversed(range(DEPTH)):
        fw, k, sv = full[l], layer_consts(l), saved[l]
        k["cwg"], k["cwv"] = fw["conv_w"][:, :D_FF], fw["conv_w"][:, D_FF:]
        staged = l == 0
        wgrad = functools.partial(_mm, ta=True, out_dtype=BF16)
        w_down_l = fw["w_down"] + exchange1[4][0, 0].astype(BF16) if staged else fw["w_down"]
        dact = _mm(dcur, w_down_l, tb=True, name=f"down_proj_bwd_{l}")
        g_down = wgrad(sv["act"], dcur, name=f"down_proj_wgrad_{l}")
        dg0, dv0, dcg, dcv = _ffn_bwd(sv["upg"], sv["upv"], k["cwg"], k["cwv"], k["cbg"], k["cbv"], dact, f"ffn_bwd_{l}")
        dh2 = _mm(dg0, fw["w_up"], tb=True, b_k=gate_cols, name=f"up_gate_bwd_{l}")
        dh2 = _mm(dv0, fw["w_up"], tb=True, b_k=val_cols, add=dh2, name=f"up_val_bwd_{l}")
        g_up = wgrad(sv["h2"], dg0, out_cols=(0, 2 * D_FF), name=f"up_gate_wgrad_{l}")
        g_up = wgrad(sv["h2"], dv0, out_cols=(D_FF, 2 * D_FF), out_into=g_up, name=f"up_val_wgrad_{l}")
        g_ffn = dict(w_up=g_up, w_down=g_down, conv_w=jnp.concatenate([dcg[0:3], dcv[0:3]], axis=1))
        norm2_l = norm2[l]
        if staged:
            parts1 = dict(zip(SHARDED, _push_wait(exchange1, True, g_up, "exchange_grads_1_wait")))
            exchange_ffn = _push_start(slots_of(g_ffn), True, "exchange_ffn_0_start")
            norm2_l = norm2_l + exchange_ffn[4][0, 0]
        dx1, g_norm2 = _rms_bwd(sv["x1"], norm2_l, dh2, dcur, f"rms2_bwd_{l}")
        dmerged = _mm(dx1, fw["w_out"], tb=True, name=f"out_proj_bwd_{l}")
        g_out = wgrad(sv["merged"], dx1, name=f"out_proj_wgrad_{l}")
        dz, dya, da = _branch_bwd(0, sv["a"], fw["w_proj_a"], sv["z"], dmerged, None, f"branch_a_bwd_{l}")
        dz, dyb, db = _branch_bwd(1, sv["b"], fw["w_proj_b"], sv["z"], dmerged, dz, f"branch_b_bwd_{l}")
        dz, dyc, dc = _branch_bwd(2, sv["c"], fw["w_proj_c"], sv["z"], dmerged, dz, f"branch_c_bwd_{l}")
        g_mix = dict(w_proj_a=wgrad(sv["a"], dya, name=f"proj_a_wgrad_{l}"), w_proj_b=wgrad(sv["b"], dyb, name=f"proj_b_wgrad_{l}"),
                     w_proj_c=wgrad(sv["c"], dyc, name=f"proj_c_wgrad_{l}"), w_out=g_out)
        pool_scale_l = pool_scale[l]
        if staged:
            exchange_mix = _push_start(slots_of(g_mix), True, "exchange_mixer_0_start")
            pool_scale_l = pool_scale_l + exchange_mix[4][0, 0]
        dxp, g_bdw, g_pscale = _pool_bwd(sv["z"], da, k["bdw"], pool_scale_l, f"pool_bwd_{l}")
        dq, dkc, dkp, dvc, dvp, g_qn, g_sink = _attn_bwd(sv["z"], cosf, sinf, k["qn"], k["kn"], sinks[l], db, f"attn_bwd_{l}")
        duv, g_ws, g_bacc, g_vn = _sgu_bwd(sv["z"], w_s[l], k["bcol"], k["vn"], dc, f"sgu_bwd_{l}")
        gsmall[l] = dict(
            q_norm=g_qn[0, :HEAD_DIM], sinks=g_sink[:, 0],
            w_pool=jnp.stack([g_bdw[g * HEAD_DIM:(g + 1) * HEAD_DIM, g * HEAD_DIM:(g + 1) * HEAD_DIM] for g in range(4)]),
            pool_scale=g_pscale[0], sgu_v_norm=g_vn[0, :HEAD_DIM], w_s=g_ws, b_s=g_bacc[:, ::HEAD_DIM].T,
            norm2=g_norm2[0], conv_b=jnp.concatenate([dcg[3], dcv[3]]))
        kn_l = k["kn"]
        if staged:
            early = _pack_small({n: jnp.stack([gsmall[i][n] for i in range(DEPTH)]) for n in SMALL_EARLY}, small_shapes,
                                SMALL_EARLY, SMALL_ROW_TILE)
            gather_early = _push_start([early], False, "gather_small_grads_start")
            kn_l = kn_l + gather_early[4][0, 0]
        dz, g_kn = _kv_post(sv["z"], cosf, sinf, kn_l, dkc, dkp, dvc, dvp, dxp, dq, duv, dz, f"kv_post_{l}")
        g_in = dict(w_in=wgrad(sv["h1"], dz, name=f"in_proj_wgrad_{l}"))
        norm1_l = norm1[l]
        if staged:
            exchange_in = _push_start(slots_of(g_in), True, "exchange_w_in_0_start")
            norm1_l = norm1_l + exchange_in[4][0, 0]
        dh1 = _mm(dz, fw["w_in"], tb=True, name=f"in_proj_bwd_{l}")
        dcur, g_norm1 = _rms_bwd(sv["x0"], norm1_l, dh1, dx1, f"rms1_bwd_{l}")
        if not staged:
            exchange1 = _push_start(slots_of({n: {**g_in, **g_mix, **g_ffn}[n] for n in SHARDED}), True, "exchange_grads_1_start")
        gsmall[l].update(norm1=g_norm1[0], k_norm=g_kn[0, :HEAD_DIM])
    grad_x = dcur[None]

    def update_small(gathered, names, row_tile, name):
        pack = lambda tree: _pack_small(tree, small_shapes, names, row_tile)
        return [_unpack_small(t, small_shapes, names) for t in _adamw_sum(gathered, pack(wts), pack(mom), pack(var), name)]

    late = _pack_small({n: jnp.stack([gsmall[i][n] for i in range(DEPTH)]) for n in SMALL_LATE}, small_shapes, SMALL_LATE, 8)
    small = update_small(_all_gather([late], "gather_late_small_grads")[0], SMALL_LATE, 8, "adamw_replicated_late")
    early_all = _push_wait(gather_early, False, small[0]["norm1"], "gather_small_grads_wait")[0]
    for kind, tree in enumerate(update_small(early_all, SMALL_EARLY, SMALL_ROW_TILE, "adamw_replicated")):
        small[kind].update(tree)
    g_s = small[0]["norm2"]

    parts0 = dict(zip(g_ffn, _push_wait(exchange_ffn, True, g_s, "exchange_ffn_0_wait")))
    parts0.update(zip(g_mix, _push_wait(exchange_mix, True, g_s, "exchange_mixer_0_wait")))
    update = lambda n: _adamw_weight([parts0[n], parts1[n]], wts[n], mom[n], var[n], f"adamw_{n}")
    big = {n: update(n) for n in SHARDED[1:]}
    parts0.update(zip(g_in, _push_wait(exchange_in, True, big["w_up"][0], "exchange_w_in_0_wait")))
    big["w_in"] = update("w_in")

    outs = [loss, grad_x]
    for kind in range(4):
        outs += [small[kind][n] if n in SMALL else big[n][kind] for n in names]
    return tuple(outs)
```

```python
import functools
import math

import jax
import jax.numpy as jnp
import numpy as np
from jax import lax
from jax.experimental import pallas as pl
from jax.experimental.pallas import tpu as pltpu

F32 = jnp.float32
BF16 = jnp.bfloat16

D_MODEL = 1024
DEPTH = 2
HEAD_DIM = 64
N_Q_HEADS = 8
Q_PER_KV = 4
BLOCK = 128
POOL_W = 256
ATTN_W = 512
KV_W = 128
SGU_W = 256
IN_COLS = 4608
GATE_COL0 = 1536
D_FF = 2816
EPS = 1e-6
ROPE_THETA = 10000.0
N_DEV = 8
LANES = 128
HALO_POOL = 16
HALO_CONV = 8

ADAM_LR = 0.001
ADAM_B1 = 0.9
ADAM_B2 = 0.999
ADAM_EPS = 1e-08
ADAM_WD = 0.01
ADAM_STEP = 10

VMEM_LIMIT = 48 * 1024 * 1024
MM_VMEM_BUDGET = 40 * 1024 * 1024

SHARDED = ("w_in", "w_proj_a", "w_proj_b", "w_proj_c", "w_out", "w_up", "w_down", "conv_w")
ROW_SHARDED = ("w_out", "w_down")
SMALL_ROW_TILE = 256
SMALL = ("norm1", "q_norm", "k_norm", "sinks", "w_pool", "pool_scale", "sgu_v_norm", "w_s", "b_s", "norm2", "conv_b")
SMALL_LATE = ("norm1", "k_norm")
SMALL_EARLY = tuple(n for n in SMALL if n not in SMALL_LATE)

_GELU_C = math.sqrt(2.0 / math.pi)
_GELU_A = 0.044715


def _params(*sem):
    return pltpu.CompilerParams(dimension_semantics=sem, vmem_limit_bytes=VMEM_LIMIT)


def _tile(n, prefs):
    for t in prefs:
        if t <= n and n % t == 0:
            return t
    return n


def _head_mean_matrix():
    r = lax.broadcasted_iota(jnp.int32, (LANES, LANES), 0)
    c = lax.broadcasted_iota(jnp.int32, (LANES, LANES), 1)
    return jnp.where((r >= HEAD_DIM) == (c >= HEAD_DIM), 1.0 / HEAD_DIM, 0.0).astype(BF16)


def _head_mean(v, bd):
    hi = v.astype(BF16)
    rest = v - hi.astype(F32)
    mid = rest.astype(BF16)
    lo = (rest - mid.astype(F32)).astype(BF16)
    mm = lambda p: jnp.dot(p, bd, preferred_element_type=F32)
    return mm(hi) + (mm(mid) + mm(lo))


def _rot_half(t):
    lane = lax.broadcasted_iota(jnp.int32, t.shape, 1)
    return jnp.where((lane & 32) == 0, pltpu.roll(t, LANES - 32, 1), pltpu.roll(t, 32, 1))


def _norm_rope(t, gn, cosf, sinf, bd):
    r = lax.rsqrt(_head_mean(t * t, bd) + EPS)
    n = t * r
    y = n * gn
    return y * cosf + _rot_half(y) * sinf, n, r


def _norm_rope_bwd(d, t, n, r, gn, cosf, sinf, bd):
    dy = d * cosf + _rot_half(d * sinf)
    dgn = jnp.sum(dy * n, axis=0, keepdims=True)
    u = dy * gn
    dt = r * u - t * (r * r * r) * _head_mean(t * u, bd)
    return dt, dgn


def _gelu(x):
    t = jnp.tanh(_GELU_C * (x + _GELU_A * (x * x * x)))
    return 0.5 * x * (1.0 + t), t


def _gelu_grad(x, t):
    return 0.5 * (1.0 + t) + 0.5 * x * (1.0 - t * t) * (_GELU_C * (1.0 + 3.0 * _GELU_A * x * x))


def _sigmoid(x):
    return jax.nn.sigmoid(x)


def _dot(a, b, ca=1, cb=0):
    return lax.dot_general(a.astype(BF16), b.astype(BF16), (((ca,), (cb,)), ((), ())), preferred_element_type=F32)


def _mm(a, b, *, ta=False, tb=False, add=None, out_dtype=F32, name, b_n=None, b_k=None, out_cols=None, out_into=None):
    m = a.shape[1] if ta else a.shape[0]
    k = a.shape[0] if ta else a.shape[1]
    n = b_n[1] if b_n else (b.shape[0] if tb else b.shape[1])
    tn = _tile(n, (1024, 1152, 1408, 512, 256, 128))
    has_add = add is not None
    fits = []
    for tm in (2048, 1024, 1408, 512, 256, 128):
        for tk in (k, 4608, 2816, 2048, 1408, 1152, 1024, 512, 256, 128):
            if tm <= m and m % tm == 0 and tk <= k and k % tk == 0:
                need = (2 * (tm * tk * a.dtype.itemsize + tk * tn * b.dtype.itemsize) + 2 * tm * tn * jnp.dtype(out_dtype).itemsize
                        + 2 * tm * tn * 4 * has_add + tm * tn * 4 * (tk < k))
                if need <= MM_VMEM_BUDGET:
                    fits.append((k // tk, -tm, tm, tk))
    if fits:
        _, _, tm, tk = min(fits)
    else:
        tm, tk = _tile(m, (256, 128)), _tile(k, (512, 256, 128))
    nk = k // tk
    n0 = b_n[0] // tn if b_n else 0
    k0 = b_k[0] // tk if b_k else 0
    o0, n_out = (out_cols[0] // tn, out_cols[1]) if out_cols else (0, n)
    n_in = 2 + has_add + (out_into is not None)

    def body(*refs):
        a_ref, b_ref = refs[0], refs[1]
        add_ref = refs[2] if has_add else None
        o_ref = refs[n_in]
        def finish(r):
            if has_add:
                r = r + add_ref[...]
            o_ref[...] = r.astype(out_dtype)

        if nk == 1:
            finish(_dot(a_ref[...], b_ref[...], 0 if ta else 1, 1 if tb else 0))
        else:
            acc_ref = refs[-1]
            kk = pl.program_id(2)

            @pl.when(kk == 0)
            def _():
                acc_ref[...] = jnp.zeros_like(acc_ref)

            acc_ref[...] += _dot(a_ref[...], b_ref[...], 0 if ta else 1, 1 if tb else 0)

            @pl.when(kk == nk - 1)
            def _():
                finish(acc_ref[...])

    a_spec = pl.BlockSpec((tk, tm), lambda i, j, kk: (kk, i)) if ta else pl.BlockSpec((tm, tk), lambda i, j, kk: (i, kk))
    if tb:
        b_spec = pl.BlockSpec((tn, tk), lambda i, j, kk: (j + n0, kk + k0))
    else:
        b_spec = pl.BlockSpec((tk, tn), lambda i, j, kk: (kk + k0, j + n0))
    in_specs = [a_spec, b_spec] + ([pl.BlockSpec((tm, tn), lambda i, j, kk: (i, j))] if has_add else [])
    args = (a, b) + ((add,) if has_add else ())
    if out_into is not None:
        in_specs.append(pl.BlockSpec(memory_space=pl.ANY))
        args += (out_into,)
    return pl.pallas_call(
        body, name=name, grid=(m // tm, n // tn, nk), in_specs=in_specs,
        out_specs=pl.BlockSpec((tm, tn), lambda i, j, kk: (i, j + o0)),
        out_shape=jax.ShapeDtypeStruct((m, n_out), out_dtype),
        scratch_shapes=[pltpu.VMEM((tm, tn), F32)] if nk > 1 else [],
        input_output_aliases={n_in - 1: 0} if out_into is not None else {},
        compiler_params=_params("parallel", "parallel", "arbitrary"),
    )(*args)


def _rms_fwd(x, g, name):
    s, d = x.shape
    tr = _tile(s, (512, 256, 128))

    def body(x_ref, g_ref, h_ref):
        xv = x_ref[...]
        r = lax.rsqrt(jnp.mean(xv * xv, axis=-1, keepdims=True) + EPS)
        h_ref[...] = ((xv * r) * g_ref[...]).astype(BF16)

    return pl.pallas_call(
        body, name=name, grid=(s // tr,),
        in_specs=[pl.BlockSpec((tr, d), lambda i: (i, 0)), pl.BlockSpec((1, d), lambda i: (0, 0))],
        out_specs=pl.BlockSpec((tr, d), lambda i: (i, 0)),
        out_shape=jax.ShapeDtypeStruct((s, d), BF16), compiler_params=_params("parallel"),
    )(x, g.reshape(1, d))


def _rms_bwd(x, g, dh, dres, name):
    s, d = x.shape
    tr = _tile(s, (512, 256, 128))

    def body(x_ref, g_ref, dh_ref, dres_ref, dx_ref, dg_ref):
        xv = x_ref[...]
        r = lax.rsqrt(jnp.mean(xv * xv, axis=-1, keepdims=True) + EPS)
        dhv = dh_ref[...]
        u = dhv * g_ref[...]
        dx_ref[...] = dres_ref[...] + (r * u - xv * (r * r * r) * jnp.mean(xv * u, axis=-1, keepdims=True))
        part = jnp.sum(dhv * (xv * r), axis=0, keepdims=True)

        @pl.when(pl.program_id(0) == 0)
        def _():
            dg_ref[...] = part

        @pl.when(pl.program_id(0) > 0)
        def _():
            dg_ref[...] += part

    row = pl.BlockSpec((tr, d), lambda i: (i, 0))
    vec = pl.BlockSpec((1, d), lambda i: (0, 0))
    return pl.pallas_call(
        body, name=name, grid=(s // tr,), in_specs=[row, vec, row, row], out_specs=[row, vec],
        out_shape=[jax.ShapeDtypeStruct((s, d), F32), jax.ShapeDtypeStruct((1, d), F32)],
        compiler_params=_params("arbitrary"),
    )(x, g.reshape(1, d), dh, dres)


def _loss_head(y, target):
    s, d = y.shape
    tr = _tile(s, (512, 256, 128))

    def body(y_ref, t_ref, dy_ref, l_ref):
        err = y_ref[...] - t_ref[...]
        dy_ref[...] = err * (1.0 / d)
        part = jnp.sum(jnp.sum(err * err, axis=-1, keepdims=True) * (1.0 / d), axis=0, keepdims=True) * 0.5
        part = jnp.broadcast_to(part, (8, LANES))

        @pl.when(pl.program_id(0) == 0)
        def _():
            l_ref[...] = part

        @pl.when(pl.program_id(0) > 0)
        def _():
            l_ref[...] += part

    row = pl.BlockSpec((tr, d), lambda i: (i, 0))
    acc = pl.BlockSpec((8, LANES), lambda i: (0, 0))
    return pl.pallas_call(
        body, name="loss_head", grid=(s // tr,), in_specs=[row, row], out_specs=[row, acc],
        out_shape=[jax.ShapeDtypeStruct((s, d), F32), jax.ShapeDtypeStruct((8, LANES), F32)],
        compiler_params=_params("arbitrary"),
    )(y, target)


def _pool_lane_select(lane, v2, v4, v8, v16):
    return jnp.where(lane < 64, v2, jnp.where(lane < 128, v4, jnp.where(lane < 192, v8, v16)))


def _pool_diff(xc, xp, row0):
    n = BLOCK + HALO_POOL
    cat = jnp.concatenate([xp, xc], axis=0)
    s2 = cat + pltpu.roll(cat, 1, 0)
    s4 = s2 + pltpu.roll(s2, 2, 0)
    s8 = s4 + pltpu.roll(s4, 4, 0)
    s16 = s8 + pltpu.roll(s8, 8, 0)
    lane = lax.broadcasted_iota(jnp.int32, (n, POOL_W), 1)
    wsum = _pool_lane_select(lane, s2, s4, s8, s16)[HALO_POOL:]
    return wsum / _pool_count(row0, BLOCK) - xc


def _pool_count(row0, rows):
    lane = lax.broadcasted_iota(jnp.int32, (rows, POOL_W), 1)
    t = lax.broadcasted_iota(jnp.int32, (rows, POOL_W), 0) + row0
    return jnp.minimum(t + 1, _pool_lane_select(lane, 2, 4, 8, 16)).astype(F32)


def _pool_fwd(z, bdw, scale, name):
    s = z.shape[0]
    nb = s // BLOCK

    def body(xc_ref, xp_ref, w_ref, sc_ref, a_ref):
        i = pl.program_id(0)
        xp = jnp.where(i > 0, xp_ref[...], 0.0)
        diff = _pool_diff(xc_ref[...], xp, i * BLOCK)
        a_ref[...] = (_dot(diff, w_ref[...]) * sc_ref[...]).astype(BF16)

    return pl.pallas_call(
        body, name=name, grid=(nb,),
        in_specs=[pl.BlockSpec((BLOCK, POOL_W), lambda i: (i, 0)),
                  pl.BlockSpec((HALO_POOL, POOL_W), lambda i: (jnp.maximum(i * (BLOCK // HALO_POOL) - 1, 0), 0)),
                  pl.BlockSpec((POOL_W, POOL_W), lambda i: (0, 0)),
                  pl.BlockSpec((1, POOL_W), lambda i: (0, 0))],
        out_specs=pl.BlockSpec((BLOCK, POOL_W), lambda i: (i, 0)),
        out_shape=jax.ShapeDtypeStruct((s, POOL_W), BF16), compiler_params=_params("parallel"),
    )(z, z, bdw, scale.reshape(1, POOL_W))


def _pool_bwd(z, da, bdw, scale, name):
    s = z.shape[0]
    nb = s // BLOCK
    per = BLOCK // HALO_POOL
    n = BLOCK + HALO_POOL

    def body(xc_ref, xp_ref, dac_ref, dan_ref, w_ref, sc_ref, dx_ref, dw_ref, dsc_ref):
        i = pl.program_id(0)
        xp = jnp.where(i > 0, xp_ref[...], 0.0)
        diff = _pool_diff(xc_ref[...], xp, i * BLOCK)
        mixed = _dot(diff, w_ref[...])
        dac = dac_ref[...]
        dan = jnp.where(i < nb - 1, dan_ref[...], 0.0)
        dmix = jnp.concatenate([dac, dan], axis=0) * sc_ref[...]
        ddiff = _dot(dmix, w_ref[...], 1, 1)
        e = ddiff / _pool_count(i * BLOCK, n)
        f2 = e + pltpu.roll(e, n - 1, 0)
        f4 = f2 + pltpu.roll(f2, n - 2, 0)
        f8 = f4 + pltpu.roll(f4, n - 4, 0)
        f16 = f8 + pltpu.roll(f8, n - 8, 0)
        lane = lax.broadcasted_iota(jnp.int32, (n, POOL_W), 1)
        back = _pool_lane_select(lane, f2, f4, f8, f16)
        dx_ref[...] = (back[:BLOCK] - ddiff[:BLOCK]).astype(BF16)
        dw = _dot(diff, dmix[:BLOCK], 0, 0)
        dsc = jnp.sum(dac * mixed, axis=0, keepdims=True)

        @pl.when(i == 0)
        def _():
            dw_ref[...] = dw
            dsc_ref[...] = dsc

        @pl.when(i > 0)
        def _():
            dw_ref[...] += dw
            dsc_ref[...] += dsc

    blk = pl.BlockSpec((BLOCK, POOL_W), lambda i: (i, 0))
    return pl.pallas_call(
        body, name=name, grid=(nb,),
        in_specs=[blk, pl.BlockSpec((HALO_POOL, POOL_W), lambda i: (jnp.maximum(i * per - 1, 0), 0)),
                  blk, pl.BlockSpec((HALO_POOL, POOL_W), lambda i: (jnp.minimum((i + 1) * per, nb * per - 1), 0)),
                  pl.BlockSpec((POOL_W, POOL_W), lambda i: (0, 0)), pl.BlockSpec((1, POOL_W), lambda i: (0, 0))],
        out_specs=[blk, pl.BlockSpec((POOL_W, POOL_W), lambda i: (0, 0)), pl.BlockSpec((1, POOL_W), lambda i: (0, 0))],
        out_shape=[jax.ShapeDtypeStruct((s, POOL_W), BF16), jax.ShapeDtypeStruct((POOL_W, POOL_W), F32),
                   jax.ShapeDtypeStruct((1, POOL_W), F32)],
        compiler_params=_params("arbitrary"),
    )(z, z, da, da, bdw, scale.reshape(1, POOL_W))


def _attn_setup(zc_ref, zp_ref, cc_ref, cp_ref, sc_ref, sp_ref, qn_ref, kn_ref, bd):
    q = []
    for j in range(ATTN_W // LANES):
        t = zc_ref[:, POOL_W + j * LANES:POOL_W + (j + 1) * LANES]
        q.append((t,) + _norm_rope(t, qn_ref[...], cc_ref[...], sc_ref[...], bd))
    kc_raw = zc_ref[:, POOL_W + ATTN_W:POOL_W + ATTN_W + KV_W]
    kc = _norm_rope(kc_raw, kn_ref[...], cc_ref[...], sc_ref[...], bd)[0]
    kp = _norm_rope(zp_ref[:, :KV_W], kn_ref[...], cp_ref[...], sp_ref[...], bd)[0]
    kband = jnp.concatenate([kp, kc], axis=0).astype(BF16)
    vband = jnp.concatenate([zp_ref[:, KV_W:], zc_ref[:, POOL_W + ATTN_W + KV_W:POOL_W + ATTN_W + 2 * KV_W]], axis=0).astype(BF16)
    return q, kband, vband


MASKED = -1e30


def _window_bias(first_block):
    row = np.arange(Q_PER_KV * BLOCK)[:, None] % BLOCK
    col = np.arange(2 * BLOCK)[None, :]
    dist = row + BLOCK - col
    inside = (dist >= 0) & (dist < BLOCK) & ((col >= BLOCK) | (not first_block))
    return jnp.asarray(np.where(inside, 0.0, MASKED), F32)


def _stack_heads(tiles, kvh):
    return jnp.concatenate([_to_kv_lanes(tiles[h // 2], h) for h in range(kvh * Q_PER_KV, (kvh + 1) * Q_PER_KV)], axis=0)


def _unstack_heads(stacked, kvh, tiles):
    for g in range(Q_PER_KV):
        h = kvh * Q_PER_KV + g
        t = _from_kv_lanes(stacked[g * BLOCK:(g + 1) * BLOCK], h)
        tiles[h // 2] = t if tiles[h // 2] is None else tiles[h // 2] + t


def _sink_column(sink_ref, kvh):
    grp = lax.broadcasted_iota(jnp.int32, (Q_PER_KV * BLOCK, 1), 0) >> 7
    s = [sink_ref[kvh * Q_PER_KV + g] for g in range(Q_PER_KV)]
    return jnp.where(grp == 0, s[0], jnp.where(grp == 1, s[1], jnp.where(grp == 2, s[2], s[3])))


def _to_kv_lanes(t, h):
    kvh = h // Q_PER_KV
    if (h % 2) != kvh:
        t = pltpu.roll(t, HEAD_DIM, 1)
    lane = lax.broadcasted_iota(jnp.int32, t.shape, 1)
    return jnp.where((lane >= HEAD_DIM) == (kvh == 1), t, 0.0)


def _from_kv_lanes(t, h):
    kvh = h // Q_PER_KV
    lane = lax.broadcasted_iota(jnp.int32, t.shape, 1)
    t = jnp.where((lane >= HEAD_DIM) == (kvh == 1), t, 0.0)
    if (h % 2) != kvh:
        t = pltpu.roll(t, HEAD_DIM, 1)
    return t


def _attn_probs(qh, kband, bias, sink):
    sc = _dot(qh, kband, 1, 1) * (HEAD_DIM ** -0.5) + bias
    m = jnp.maximum(jnp.max(sc, axis=1, keepdims=True), sink)
    p = jnp.exp(sc - m)
    psink = jnp.exp(sink - m)
    den = jnp.sum(p, axis=1, keepdims=True) + psink
    return p / den, psink / den


def _attn_specs(nb):
    cur = lambda i: (i, 0)
    prev = lambda i: (jnp.maximum(i - 1, 0), 0)
    tab = lambda f: pl.BlockSpec((BLOCK, LANES), f)
    vec = pl.BlockSpec((1, LANES), lambda i: (0, 0))
    return [pl.BlockSpec((BLOCK, 1024), cur),
            pl.BlockSpec((BLOCK, 2 * KV_W), lambda i: (jnp.maximum(i - 1, 0), 3)),
            tab(cur), tab(prev), tab(cur), tab(prev), vec, vec,
            pl.BlockSpec(memory_space=pltpu.SMEM)] + [pl.BlockSpec((Q_PER_KV * BLOCK, 2 * BLOCK), lambda i: (0, 0))] * 2


def _attn_fwd(z, cosf, sinf, qn, kn, sinks, name):
    s = z.shape[0]
    nb = s // BLOCK

    def body(zc_ref, zp_ref, cc_ref, cp_ref, sc_ref, sp_ref, qn_ref, kn_ref, sink_ref, bias_ref, bias0_ref, o_ref):
        i = pl.program_id(0)
        bd = _head_mean_matrix()
        q, kband, vband = _attn_setup(zc_ref, zp_ref, cc_ref, cp_ref, sc_ref, sp_ref, qn_ref, kn_ref, bd)
        mask = jnp.where(i > 0, bias_ref[...], bias0_ref[...])
        out = [None] * (ATTN_W // LANES)
        for kvh in range(N_Q_HEADS // Q_PER_KV):
            qs = _stack_heads([t[1] for t in q], kvh)
            probs, _ = _attn_probs(qs, kband, mask, _sink_column(sink_ref, kvh))
            _unstack_heads(_dot(probs, vband), kvh, out)
        for j, o in enumerate(out):
            o_ref[:, j * LANES:(j + 1) * LANES] = o.astype(BF16)

    return pl.pallas_call(
        body, name=name, grid=(nb,), in_specs=_attn_specs(nb),
        out_specs=pl.BlockSpec((BLOCK, ATTN_W), lambda i: (i, 0)),
        out_shape=jax.ShapeDtypeStruct((s, ATTN_W), BF16), compiler_params=_params("parallel"),
    )(z, z, cosf, cosf, sinf, sinf, qn, kn, sinks, _window_bias(False), _window_bias(True))


def _attn_bwd(z, cosf, sinf, qn, kn, sinks, d_out, name):
    s = z.shape[0]
    nb = s // BLOCK
    nt = ATTN_W // LANES

    def body(zc_ref, zp_ref, cc_ref, cp_ref, sc_ref, sp_ref, qn_ref, kn_ref, sink_ref, bias_ref, bias0_ref, do_ref,
             dq_ref, dkc_ref, dkp_ref, dvc_ref, dvp_ref, dqn_ref, dsink_ref):
        i = pl.program_id(0)
        bd = _head_mean_matrix()
        q, kband, vband = _attn_setup(zc_ref, zp_ref, cc_ref, cp_ref, sc_ref, sp_ref, qn_ref, kn_ref, bd)
        mask = jnp.where(i > 0, bias_ref[...], bias0_ref[...])

        @pl.when(i == 0)
        def _():
            dqn_ref[...] = jnp.zeros_like(dqn_ref)
            dsink_ref[...] = jnp.zeros_like(dsink_ref)

        dq = [None] * nt
        dk = jnp.zeros((2 * BLOCK, KV_W), F32)
        dv = jnp.zeros((2 * BLOCK, KV_W), F32)
        d_tiles = [do_ref[:, j * LANES:(j + 1) * LANES] for j in range(nt)]
        for kvh in range(N_Q_HEADS // Q_PER_KV):
            qs = _stack_heads([t[1] for t in q], kvh)
            probs, psink = _attn_probs(qs, kband, mask, _sink_column(sink_ref, kvh))
            dos = _stack_heads(d_tiles, kvh)
            dp = _dot(dos, vband, 1, 1)
            delta = jnp.sum(dp * probs, axis=1, keepdims=True)
            ds = (probs * (dp - delta)) * (HEAD_DIM ** -0.5)
            dsink = -psink * delta
            for g in range(Q_PER_KV):
                h = kvh * Q_PER_KV + g
                dsink_ref[h:h + 1, :] += jnp.broadcast_to(jnp.sum(dsink[g * BLOCK:(g + 1) * BLOCK], axis=0, keepdims=True), (1, LANES))
            _unstack_heads(_dot(ds, kband), kvh, dq)
            dk = dk + _dot(ds, qs, 0, 0)
            dv = dv + _dot(probs, dos, 0, 0)
        dgn = jnp.zeros((1, LANES), F32)
        for j in range(nt):
            t, _, n, r = q[j]
            dt, g = _norm_rope_bwd(dq[j], t, n, r, qn_ref[...], cc_ref[...], sc_ref[...], bd)
            dq_ref[:, j * LANES:(j + 1) * LANES] = dt.astype(BF16)
            dgn = dgn + g
        dqn_ref[...] += jnp.broadcast_to(dgn, (8, LANES))
        dkp_ref[...] = dk[:BLOCK]
        dkc_ref[...] = dk[BLOCK:]
        dvp_ref[...] = dv[:BLOCK]
        dvc_ref[...] = dv[BLOCK:]

        @pl.when(i == nb - 1)
        def _():
            acc = dqn_ref[...]
            dqn_ref[...] = acc + pltpu.roll(acc, HEAD_DIM, 1)

    blk = lambda w: pl.BlockSpec((BLOCK, w), lambda i: (i, 0))
    acc = pl.BlockSpec((8, LANES), lambda i: (0, 0))
    kv = jax.ShapeDtypeStruct((s, KV_W), F32)
    return pl.pallas_call(
        body, name=name, grid=(nb,), in_specs=_attn_specs(nb) + [blk(ATTN_W)],
        out_specs=[blk(ATTN_W), blk(KV_W), blk(KV_W), blk(KV_W), blk(KV_W), acc, acc],
        out_shape=[jax.ShapeDtypeStruct((s, ATTN_W), BF16), kv, kv, kv, kv,
                   jax.ShapeDtypeStruct((8, LANES), F32), jax.ShapeDtypeStruct((8, LANES), F32)],
        compiler_params=_params("arbitrary"),
    )(z, z, cosf, cosf, sinf, sinf, qn, kn, sinks, _window_bias(False), _window_bias(True), d_out)


def _kv_post(z, cosf, sinf, kn, dkc, dkp, dvc, dvp, dxp, dq, duv, dz, name):
    s = z.shape[0]
    nb = s // BLOCK

    def body(zk_ref, c_ref, s_ref, kn_ref, dkc_ref, dkp_ref, dvc_ref, dvp_ref, dxp_ref, dq_ref, duv_ref, dz_in,
             dz_ref, dkn_ref):
        j = pl.program_id(0)
        bd = _head_mean_matrix()
        last = j == nb - 1
        d = dkc_ref[...] + jnp.where(last, 0.0, dkp_ref[...])
        t = zk_ref[:, :KV_W]
        _, n, r = _norm_rope(t, kn_ref[...], c_ref[...], s_ref[...], bd)
        dt, g = _norm_rope_bwd(d, t, n, r, kn_ref[...], c_ref[...], s_ref[...], bd)
        dvv = dvc_ref[...] + jnp.where(last, 0.0, dvp_ref[...])
        dz_ref[:, 0:POOL_W] = dxp_ref[...]
        dz_ref[:, POOL_W:POOL_W + ATTN_W] = dq_ref[...]
        dz_ref[:, POOL_W + ATTN_W:POOL_W + ATTN_W + KV_W] = dt.astype(BF16)
        dz_ref[:, POOL_W + ATTN_W + KV_W:POOL_W + ATTN_W + 2 * KV_W] = dvv.astype(BF16)
        dz_ref[:, POOL_W + ATTN_W + 2 * KV_W:GATE_COL0] = duv_ref[...]

        @pl.when(j == 0)
        def _():
            dkn_ref[...] = jnp.zeros_like(dkn_ref)

        dkn_ref[...] += jnp.broadcast_to(g, (8, LANES))

        @pl.when(last)
        def _():
            acc = dkn_ref[...]
            dkn_ref[...] = acc + pltpu.roll(acc, HEAD_DIM, 1)

    cur = lambda w: pl.BlockSpec((BLOCK, w), lambda j: (j, 0))
    nxt = pl.BlockSpec((BLOCK, KV_W), lambda j: (jnp.minimum(j + 1, nb - 1), 0))
    vec = pl.BlockSpec((1, LANES), lambda j: (0, 0))
    return pl.pallas_call(
        body, name=name, grid=(nb,),
        in_specs=[pl.BlockSpec((BLOCK, 2 * KV_W), lambda j: (j, 3)), cur(LANES), cur(LANES), vec,
                  cur(KV_W), nxt, cur(KV_W), nxt, cur(POOL_W), cur(ATTN_W), cur(2 * SGU_W),
                  pl.BlockSpec(memory_space=pl.ANY)],
        out_specs=[pl.BlockSpec((BLOCK, GATE_COL0), lambda j: (j, 0)), pl.BlockSpec((8, LANES), lambda j: (0, 0))],
        out_shape=[jax.ShapeDtypeStruct(dz.shape, dz.dtype), jax.ShapeDtypeStruct((8, LANES), F32)],
        input_output_aliases={11: 0}, compiler_params=_params("arbitrary"),
    )(z, cosf, sinf, kn, dkc, dkp, dvc, dvp, dxp, dq, duv, dz)


def _sgu_setup(z_ref, ws_ref, vn_ref, bd):
    us = z_ref[:, :SGU_W]
    vs = z_ref[:, SGU_W:]
    ug, tu = _gelu(us)
    gv, tv = _gelu(vs)
    rr = jnp.concatenate([lax.rsqrt(_head_mean(gv[:, k * LANES:(k + 1) * LANES] ** 2, bd) + EPS) for k in range(2)], axis=1)
    vg = (gv * rr) * vn_ref[...]
    tril = lax.broadcasted_iota(jnp.int32, (BLOCK, BLOCK), 0) >= lax.broadcasted_iota(jnp.int32, (BLOCK, BLOCK), 1)
    w = [jnp.where(tril, ws_ref[g], 0.0).astype(BF16) for g in range(4)]
    return us, vs, ug, tu, gv, tv, rr, vg, w, tril


def _group_select(parts):
    lane = lax.broadcasted_iota(jnp.int32, parts[0].shape, 1)
    return _pool_lane_select(lane, *parts)


def _sgu_fwd(z, ws, bcol, vn, name):
    s = z.shape[0]
    nb = s // BLOCK

    def body(z_ref, ws_ref, b_ref, vn_ref, c_ref):
        bd = _head_mean_matrix()
        _, _, ug, _, _, _, _, vg, w, _ = _sgu_setup(z_ref, ws_ref, vn_ref, bd)
        sg = _group_select([_dot(w[g], vg) for g in range(4)]) + b_ref[...]
        c_ref[...] = (ug * sg).astype(BF16)

    return pl.pallas_call(
        body, name=name, grid=(nb,),
        in_specs=[pl.BlockSpec((BLOCK, 2 * SGU_W), lambda i: (i, 2)), pl.BlockSpec((4, BLOCK, BLOCK), lambda i: (0, 0, 0)),
                  pl.BlockSpec((BLOCK, SGU_W), lambda i: (0, 0)), pl.BlockSpec((1, SGU_W), lambda i: (0, 0))],
        out_specs=pl.BlockSpec((BLOCK, SGU_W), lambda i: (i, 0)),
        out_shape=jax.ShapeDtypeStruct((s, SGU_W), BF16), compiler_params=_params("parallel"),
    )(z, ws, bcol, vn)


def _sgu_bwd(z, ws, bcol, vn, dc, name):
    s = z.shape[0]
    nb = s // BLOCK

    def body(z_ref, ws_ref, b_ref, vn_ref, dc_ref, duv_ref, dws_ref, db_ref, dvn_ref):
        i = pl.program_id(0)
        bd = _head_mean_matrix()
        us, vs, ug, tu, gv, tv, rr, vg, w, tril = _sgu_setup(z_ref, ws_ref, vn_ref, bd)
        sg = _group_select([_dot(w[g], vg) for g in range(4)]) + b_ref[...]
        dcv = dc_ref[...]
        dug = dcv * sg
        dsg = dcv * ug
        lane = lax.broadcasted_iota(jnp.int32, dsg.shape, 1)

        @pl.when(i == 0)
        def _():
            dws_ref[...] = jnp.zeros_like(dws_ref)
            db_ref[...] = jnp.zeros_like(db_ref)
            dvn_ref[...] = jnp.zeros_like(dvn_ref)

        for g in range(4):
            dsg_g = jnp.where((lane >= g * HEAD_DIM) & (lane < (g + 1) * HEAD_DIM), dsg, 0.0)
            dws_ref[g] += jnp.where(tril, _dot(dsg_g, vg, 1, 1), 0.0)
        dvg = _group_select([_dot(w[g], dsg, 0, 0) for g in range(4)])
        db_ref[...] += dsg
        n = gv * rr
        part = jnp.sum(dvg * n, axis=0, keepdims=True)
        dvn_ref[...] += jnp.broadcast_to(part[:, :LANES] + part[:, LANES:], (8, LANES))
        u = dvg * vn_ref[...]
        tu_ = gv * u
        mean = jnp.concatenate([_head_mean(tu_[:, k * LANES:(k + 1) * LANES], bd) for k in range(2)], axis=1)
        dgv = rr * u - gv * (rr * rr * rr) * mean
        duv_ref[:, :SGU_W] = (dug * _gelu_grad(us, tu)).astype(BF16)
        duv_ref[:, SGU_W:] = (dgv * _gelu_grad(vs, tv)).astype(BF16)

        @pl.when(i == nb - 1)
        def _():
            acc = dvn_ref[...]
            dvn_ref[...] = acc + pltpu.roll(acc, HEAD_DIM, 1)
            for k in range(2):
                db_ref[:, k * LANES:(k + 1) * LANES] = _head_mean(db_ref[:, k * LANES:(k + 1) * LANES], bd) * float(HEAD_DIM)

    return pl.pallas_call(
        body, name=name, grid=(nb,),
        in_specs=[pl.BlockSpec((BLOCK, 2 * SGU_W), lambda i: (i, 2)), pl.BlockSpec((4, BLOCK, BLOCK), lambda i: (0, 0, 0)),
                  pl.BlockSpec((BLOCK, SGU_W), lambda i: (0, 0)), pl.BlockSpec((1, SGU_W), lambda i: (0, 0)),
                  pl.BlockSpec((BLOCK, SGU_W), lambda i: (i, 0))],
        out_specs=[pl.BlockSpec((BLOCK, 2 * SGU_W), lambda i: (i, 0)), pl.BlockSpec((4, BLOCK, BLOCK), lambda i: (0, 0, 0)),
                   pl.BlockSpec((BLOCK, SGU_W), lambda i: (0, 0)), pl.BlockSpec((8, LANES), lambda i: (0, 0))],
        out_shape=[jax.ShapeDtypeStruct((s, 2 * SGU_W), BF16), jax.ShapeDtypeStruct((4, BLOCK, BLOCK), F32),
                   jax.ShapeDtypeStruct((BLOCK, SGU_W), F32), jax.ShapeDtypeStruct((8, LANES), F32)],
        compiler_params=_params("arbitrary"),
    )(z, ws, bcol, vn, dc)


MERGE_TN = 512
MERGE_TM = 1024


def _merge_out_fwd(a, b, c, wpa, wpb, wpc, z, w_out, res, name):
    s = z.shape[0]
    tm = _tile(s, (MERGE_TM, BLOCK))
    gate0 = GATE_COL0 // MERGE_TN
    nn = D_MODEL // MERGE_TN

    def body(a_ref, b_ref, c_ref, wa_ref, wb_ref, wc_ref, g0_ref, g1_ref, g2_ref, wo_ref, res_ref, o_ref, m_ref, acc_ref):
        n = pl.program_id(1)
        r = _sigmoid(g0_ref[...]) * _dot(a_ref[...], wa_ref[...])
        r = r + _sigmoid(g1_ref[...]) * _dot(b_ref[...], wb_ref[...])
        r = r + _sigmoid(g2_ref[...]) * _dot(c_ref[...], wc_ref[...])
        merged = r.astype(BF16)
        m_ref[...] = merged

        @pl.when(n == 0)
        def _():
            acc_ref[...] = jnp.zeros_like(acc_ref)

        acc_ref[...] += jnp.dot(merged, wo_ref[...], preferred_element_type=F32)

        @pl.when(n == nn - 1)
        def _():
            o_ref[...] = acc_ref[...] + res_ref[...]

    x_spec = lambda w: pl.BlockSpec((tm, w), lambda i, n: (i, 0))
    w_spec = lambda w: pl.BlockSpec((w, MERGE_TN), lambda i, n: (0, n))
    g_spec = lambda br: pl.BlockSpec((tm, MERGE_TN), lambda i, n: (i, gate0 + 2 * br + n))
    row = pl.BlockSpec((tm, D_MODEL), lambda i, n: (i, 0))
    return pl.pallas_call(
        body, name=name, grid=(s // tm, nn),
        in_specs=[x_spec(POOL_W), x_spec(ATTN_W), x_spec(SGU_W), w_spec(POOL_W), w_spec(ATTN_W), w_spec(SGU_W),
                  g_spec(0), g_spec(1), g_spec(2), pl.BlockSpec((MERGE_TN, D_MODEL), lambda i, n: (n, 0)), row],
        out_specs=[row, pl.BlockSpec((tm, MERGE_TN), lambda i, n: (i, n))],
        out_shape=[jax.ShapeDtypeStruct((s, D_MODEL), F32), jax.ShapeDtypeStruct((s, D_MODEL), BF16)],
        scratch_shapes=[pltpu.VMEM((tm, D_MODEL), F32)], compiler_params=_params("parallel", "arbitrary"),
    )(a, b, c, wpa, wpb, wpc, z, z, z, w_out, res)


def _branch_bwd(br, xb, wp, z, dm, dz, name):
    s = z.shape[0]
    kb = xb.shape[1]
    tm = _tile(s, (MERGE_TM, BLOCK))
    gate0 = GATE_COL0 // MERGE_TN
    aliased = dz is not None

    def body(*refs):
        x_ref, w_ref, g_ref, dm_ref = refs[:4]
        dz_ref, dy_ref, dx_ref = refs[-3:]
        n = pl.program_id(1)
        y = _dot(x_ref[...], w_ref[...])
        sg = _sigmoid(g_ref[...])
        dmv = dm_ref[...]
        dy = (dmv * sg).astype(BF16)
        dy_ref[...] = dy
        dz_ref[...] = ((dmv * y) * (sg * (1.0 - sg))).astype(BF16)
        dx = _dot(dy, w_ref[...], 1, 1)

        @pl.when(n == 0)
        def _():
            dx_ref[...] = dx

        @pl.when(n > 0)
        def _():
            dx_ref[...] += dx

    in_specs = [pl.BlockSpec((tm, kb), lambda i, n: (i, 0)), pl.BlockSpec((kb, MERGE_TN), lambda i, n: (0, n)),
                pl.BlockSpec((tm, MERGE_TN), lambda i, n: (i, gate0 + 2 * br + n)),
                pl.BlockSpec((tm, MERGE_TN), lambda i, n: (i, n))]
    args = [xb, wp, z, dm]
    if aliased:
        in_specs.append(pl.BlockSpec(memory_space=pl.ANY))
        args.append(dz)
    return pl.pallas_call(
        body, name=name, grid=(s // tm, D_MODEL // MERGE_TN), in_specs=in_specs,
        out_specs=[pl.BlockSpec((tm, MERGE_TN), lambda i, n: (i, gate0 + 2 * br + n)),
                   pl.BlockSpec((tm, MERGE_TN), lambda i, n: (i, n)),
                   pl.BlockSpec((tm, kb), lambda i, n: (i, 0))],
        out_shape=[jax.ShapeDtypeStruct((s, IN_COLS), BF16), jax.ShapeDtypeStruct((s, D_MODEL), BF16),
                   jax.ShapeDtypeStruct((s, kb), F32)],
        input_output_aliases={4: 0} if aliased else {},
        compiler_params=_params("parallel", "arbitrary"),
    )(*args)


FFN_TM = 256
FFN_TC = 2816
FFN_STRIP = 256
FFN_UNROLL = 4


def _conv3(cur, prev, w_ref, b_ref):
    cat = jnp.concatenate([prev, cur], axis=0)
    x1 = pltpu.roll(cat, 1, 0)[HALO_CONV:]
    x2 = pltpu.roll(cat, 2, 0)[HALO_CONV:]
    return w_ref[0:1, :] * x2 + w_ref[1:2, :] * x1 + w_ref[2:3, :] * cur + b_ref[...], x1, x2


def _ffn_specs(tm):
    per = tm // HALO_CONV
    cur = pl.BlockSpec((tm, FFN_TC), lambda j, i: (i, j))
    prev = pl.BlockSpec((HALO_CONV, FFN_TC), lambda j, i: (jnp.maximum(i * per - 1, 0), j))
    w = pl.BlockSpec((3, FFN_TC), lambda j, i: (0, j))
    b = pl.BlockSpec((1, FFN_TC), lambda j, i: (0, j))
    return cur, prev, w, b


DOWN_TM = 256
DOWN_TK = 2816
DOWN_CHUNK = 256


def _ffn_down_fwd(upg, upv, cwg, cwv, cbg, cbv, w_down, res, name):
    s = upg.shape[0]
    d = w_down.shape[1]
    tm = _tile(s, (DOWN_TM, DOWN_CHUNK, BLOCK))
    chunk = min(DOWN_CHUNK, tm)
    per = tm // HALO_CONV
    nk = D_FF // DOWN_TK

    def body(g_ref, gp_ref, v_ref, vp_ref, wg_ref, wv_ref, bg_ref, bv_ref, w_ref, res_ref, o_ref, act_ref, acc_ref):
        first = pl.program_id(0) == 0
        kk = pl.program_id(1)

        @pl.when(kk == 0)
        def _():
            acc_ref[...] = jnp.zeros_like(acc_ref)

        for c in range(tm // chunk):
            rows = slice(c * chunk, (c + 1) * chunk)
            before = slice(c * chunk - HALO_CONV, c * chunk)
            g_p = jnp.where(first, 0.0, gp_ref[...]) if c == 0 else g_ref[before, :]
            v_p = jnp.where(first, 0.0, vp_ref[...]) if c == 0 else v_ref[before, :]
            gate = _conv3(g_ref[rows, :], g_p, wg_ref, bg_ref)[0]
            val = _conv3(v_ref[rows, :], v_p, wv_ref, bv_ref)[0]
            act = ((gate * _sigmoid(gate)) * val).astype(BF16)
            act_ref[rows, :] = act
            acc_ref[rows, :] += jnp.dot(act, w_ref[...], preferred_element_type=F32)

        @pl.when(kk == nk - 1)
        def _():
            o_ref[...] = acc_ref[...] + res_ref[...]

    cur = pl.BlockSpec((tm, DOWN_TK), lambda i, kk: (i, kk))
    prev = pl.BlockSpec((HALO_CONV, DOWN_TK), lambda i, kk: (jnp.maximum(i * per - 1, 0), kk))
    w = pl.BlockSpec((3, DOWN_TK), lambda i, kk: (0, kk))
    b = pl.BlockSpec((1, DOWN_TK), lambda i, kk: (0, kk))
    row = pl.BlockSpec((tm, d), lambda i, kk: (i, 0))
    return pl.pallas_call(
        body, name=name, grid=(s // tm, nk),
        in_specs=[cur, prev, cur, prev, w, w, b, b, pl.BlockSpec((DOWN_TK, d), lambda i, kk: (kk, 0)), row],
        out_specs=[row, cur], out_shape=[jax.ShapeDtypeStruct((s, d), F32), jax.ShapeDtypeStruct((s, D_FF), BF16)],
        scratch_shapes=[pltpu.VMEM((tm, d), F32)], compiler_params=_params("parallel", "arbitrary"),
    )(upg, upg, upv, upv, cwg, cwv, cbg, cbv, w_down, res)


def _ffn_bwd(upg, upv, cwg, cwv, cbg, cbv, dact, name):
    s = upg.shape[0]
    tm = _tile(s, (FFN_TM, BLOCK))
    per = tm // HALO_CONV
    nrow = s // tm
    cur, prev, w, b = _ffn_specs(tm)
    nxt = pl.BlockSpec((HALO_CONV, FFN_TC), lambda j, i: (jnp.minimum((i + 1) * per, nrow * per - 1), j))

    nch = tm // 8
    rows8 = lambda r: pl.ds(pl.multiple_of(r * 8, 8), 8)

    def body(g_ref, gp_ref, gn_ref, v_ref, vp_ref, vn_ref, wg_ref, wv_ref, bg_ref, bv_ref, da_ref, dan_ref,
             dg_ref, dv_ref, dwg_ref, dwv_ref, og_ref, ov_ref):
        i = pl.program_id(1)
        first = i == 0
        last = i == nrow - 1
        row = lax.broadcasted_iota(jnp.int32, (8, FFN_STRIP), 0)

        keep_down = {k: row >= k for k in (1, 2)}
        keep_up = {k: row < 8 - k for k in (1, 2)}

        def down(cur, prev, k):
            return jnp.where(keep_down[k], pltpu.roll(cur, k, 0), pltpu.roll(prev, k, 0))

        def up(cur, nxt, k):
            return jnp.where(keep_up[k], pltpu.roll(cur, 8 - k, 0), pltpu.roll(nxt, 8 - k, 0))

        @pl.when(first)
        def _():
            dwg_ref[...] = jnp.zeros_like(dwg_ref)
            dwv_ref[...] = jnp.zeros_like(dwv_ref)

        for c in range(FFN_TC // FFN_STRIP):
            cols = slice(c * FFN_STRIP, (c + 1) * FFN_STRIP)
            wg = [functools.partial(lambda k: wg_ref[k:k + 1, cols], k) for k in range(3)]
            wv = [functools.partial(lambda k: wv_ref[k:k + 1, cols], k) for k in range(3)]

            def conv_grads(g_cur, g_prev, v_cur, v_prev, da):
                gate = wg[0]() * down(g_cur, g_prev, 2) + wg[1]() * down(g_cur, g_prev, 1) + wg[2]() * g_cur + bg_ref[:, cols]
                val = wv[0]() * down(v_cur, v_prev, 2) + wv[1]() * down(v_cur, v_prev, 1) + wv[2]() * v_cur + bv_ref[:, cols]
                sg = _sigmoid(gate)
                return (da * val) * (sg * (1.0 + gate * (1.0 - sg))), da * (gate * sg)

            def passes(q, carry):
                for u in range(FFN_UNROLL):
                    carry = one_pass(q * FFN_UNROLL + u, carry)
                return carry

            def one_pass(r, carry, tile_end=False):
                dg_cur, dv_cur, acc = carry
                g_r, v_r = g_ref[rows8(r), cols], v_ref[rows8(r), cols]
                if tile_end:
                    g_n, v_n, da_n = gn_ref[:, cols], vn_ref[:, cols], jnp.where(last, 0.0, dan_ref[:, cols])
                else:
                    g_n, v_n, da_n = g_ref[rows8(r + 1), cols], v_ref[rows8(r + 1), cols], da_ref[rows8(r + 1), cols]
                dg_n, dv_n = conv_grads(g_n, g_r, v_n, v_r, da_n)
                new_acc = []
                for o_ref, w, d_cur, d_n, x0, a in ((og_ref, wg, dg_cur, dg_n, g_r, acc[:4]), (ov_ref, wv, dv_cur, dv_n, v_r, acc[4:])):
                    d1, d2 = up(d_cur, d_n, 1), up(d_cur, d_n, 2)
                    o_ref[rows8(r), cols] = w[2]() * d_cur + w[1]() * d1 + w[0]() * d2
                    new_acc += [a[0] + d2 * x0, a[1] + d1 * x0, a[2] + d_cur * x0, a[3] + d_cur]
                return dg_n, dv_n, tuple(new_acc)

            g_p = jnp.where(first, 0.0, gp_ref[:, cols])
            v_p = jnp.where(first, 0.0, vp_ref[:, cols])
            dg0, dv0 = conv_grads(g_ref[0:8, cols], g_p, v_ref[0:8, cols], v_p, da_ref[0:8, cols])
            zero = jnp.zeros((8, FFN_STRIP), F32)
            carry = lax.fori_loop(0, nch // FFN_UNROLL - 1, passes, (dg0, dv0, (zero,) * 8))
            for r in range(nch - FFN_UNROLL, nch):
                carry = one_pass(r, carry, tile_end=r == nch - 1)
            for ref, a in ((dwg_ref, carry[2][:4]), (dwv_ref, carry[2][4:])):
                for k in range(4):
                    ref[k:k + 1, cols] += jnp.sum(a[k], axis=0, keepdims=True)
        dg_ref[...] = og_ref[...].astype(BF16)
        dv_ref[...] = ov_ref[...].astype(BF16)

    acc = pl.BlockSpec((8, FFN_TC), lambda j, i: (0, j))
    full = jax.ShapeDtypeStruct((s, D_FF), BF16)
    accs = jax.ShapeDtypeStruct((8, D_FF), F32)
    return pl.pallas_call(
        body, name=name, grid=(D_FF // FFN_TC, nrow), in_specs=[cur, prev, nxt, cur, prev, nxt, w, w, b, b, cur, nxt],
        out_specs=[cur, cur, acc, acc], out_shape=[full, full, accs, accs],
        scratch_shapes=[pltpu.VMEM((tm, FFN_TC), F32), pltpu.VMEM((tm, FFN_TC), F32)],
        compiler_params=_params("parallel", "arbitrary"),
    )(upg, upg, upg, upv, upv, upv, cwg, cwv, cbg, cbv, dact, dact)


def _mesh_place():
    return lax.axis_index("x"), lax.axis_index("y"), lax.axis_index("c")


def _all_gather(shards, name):
    na = len(shards)

    def body(*refs):
        x_refs, out_refs = refs[:na], refs[na:2 * na]
        send_sems, recv_sems, local_sems = refs[2 * na:]
        x, y, cc = _mesh_place()
        me, sibling = (x, y, cc), (x, y, 1 - cc)
        chips = [(1 - x, y), (x, 1 - y), (1 - x, 1 - y)]

        def copy(k, a, block, to, from_input=False):
            slot = out_refs[a].at[4 * block[0] + 2 * block[1] + block[2]]
            return pltpu.make_async_remote_copy(
                src_ref=x_refs[a] if from_input else slot, dst_ref=slot, send_sem=send_sems.at[k * na + a],
                recv_sem=recv_sems.at[k * na + a], device_id=to, device_id_type=pl.DeviceIdType.MESH)

        mine = [pltpu.make_async_copy(x_refs[a], out_refs[a].at[4 * x + 2 * y + cc], local_sems.at[a]) for a in range(na)]
        for cp in mine:
            cp.start()
        first = [copy(0, a, me, sibling, True) for a in range(na)]
        first += [copy(1 + j, a, me, (*chip, cc), True) for j, chip in enumerate(chips) for a in range(na)]
        for cp in first:
            cp.start()
        passed = []
        for j, chip in enumerate(chips):
            for a in range(na):
                copy(1 + j, a, (*chip, cc), me).wait_recv()
                passed.append(copy(4 + j, a, (*chip, cc), sibling))
                passed[-1].start()
        for a in range(na):
            copy(0, a, sibling, me).wait_recv()
        for j, chip in enumerate(chips):
            for a in range(na):
                copy(4 + j, a, (*chip, 1 - cc), me).wait_recv()
        for cp in first + passed:
            cp.wait_send()
        for cp in mine:
            cp.wait()

    hbm = pl.BlockSpec(memory_space=pl.ANY)
    return pl.pallas_call(
        body, name=name, out_shape=[jax.ShapeDtypeStruct((N_DEV,) + t.shape, t.dtype) for t in shards],
        in_specs=[hbm] * na, out_specs=[hbm] * na,
        scratch_shapes=[pltpu.SemaphoreType.DMA((7 * na,)), pltpu.SemaphoreType.DMA((7 * na,)), pltpu.SemaphoreType.DMA((na,))],
    )(*shards)


def _peer(k):
    x, y, cc = _mesh_place()
    px = 1 - x if (k >> 2) & 1 else x
    py = 1 - y if (k >> 1) & 1 else y
    pc = 1 - cc if k & 1 else cc
    return (px, py, pc), 4 * px + 2 * py + pc


def _push_copy(src_ref, land_ref, k, a, na, send_sems, recv_sems, indexed, landed):
    x, y, cc = _mesh_place()
    place, peer = _peer(k)
    sem = (k - 1) * na + a
    return pltpu.make_async_remote_copy(
        src_ref=src_ref.at[peer] if indexed else src_ref, dst_ref=land_ref.at[peer if landed else 4 * x + 2 * y + cc],
        send_sem=send_sems.at[sem], recv_sem=recv_sems.at[sem], device_id=place, device_id_type=pl.DeviceIdType.MESH)


_HBM = pl.BlockSpec(memory_space=pltpu.HBM)
_SEM = pl.BlockSpec(memory_space=pltpu.SEMAPHORE)
_EFFECT = pltpu.SideEffectType.DATAFLOW_SIDE_EFFECTING


def _push_start(srcs, indexed, name):
    na = len(srcs)
    lands = [lax.empty(t.shape if indexed else (N_DEV,) + t.shape, t.dtype) for t in srcs]

    def body(*refs):
        src_refs, land_refs = refs[:na], refs[na:2 * na]
        send_sems, recv_sems = refs[2 * na], refs[2 * na + 1]
        token = refs[-1]
        for k in range(1, N_DEV):
            for a in range(na):
                _push_copy(src_refs[a], land_refs[a], k, a, na, send_sems, recv_sems, indexed, False).start()
        token[...] = jnp.zeros_like(token)

    sems = pltpu.SemaphoreType.DMA((7 * na,))
    out = pl.pallas_call(
        body, name=name,
        out_shape=(sems, sems, *[pltpu.HBM(t.shape, t.dtype) for t in srcs], *[pltpu.HBM(t.shape, t.dtype) for t in lands],
                   jax.ShapeDtypeStruct((8, LANES), F32)),
        in_specs=[_HBM] * (2 * na), out_specs=(_SEM, _SEM, *[_HBM] * (2 * na), pl.BlockSpec(memory_space=pltpu.VMEM)),
        input_output_aliases={i: 2 + i for i in range(2 * na)},
        compiler_params=pltpu.CompilerParams(has_side_effects=_EFFECT),
    )(*[pltpu.with_memory_space_constraint(t, pltpu.HBM) for t in srcs + lands])
    return out[0], out[1], list(out[2:2 + na]), list(out[2 + na:2 + 2 * na]), out[-1]


def _push_wait(started, indexed, after, name):
    send_sems, recv_sems, srcs, lands, _ = started
    na = len(srcs)

    def body(*refs):
        src_refs, land_refs = refs[:na], refs[na:2 * na]
        send_sems, recv_sems = refs[2 * na], refs[2 * na + 1]
        for k in range(1, N_DEV):
            for a in range(na):
                copy = _push_copy(src_refs[a], land_refs[a], k, a, na, send_sems, recv_sems, indexed, True)
                copy.wait_send()
                copy.wait_recv()

    out = pl.pallas_call(
        body, name=name, out_shape=[pltpu.HBM(t.shape, t.dtype) for t in srcs + lands],
        in_specs=[_HBM] * (2 * na) + [_SEM, _SEM, pl.BlockSpec(memory_space=pl.ANY)], out_specs=[_HBM] * (2 * na),
        input_output_aliases={i: i for i in range(2 * na)},
        compiler_params=pltpu.CompilerParams(has_side_effects=_EFFECT),
    )(*srcs, *lands, send_sems, recv_sems, after)
    x, y, cc = _mesh_place()
    me = 4 * x + 2 * y + cc
    return [lax.dynamic_update_index_in_dim(
        land, lax.dynamic_index_in_dim(src, me, 0, keepdims=False) if indexed else src, me, 0)
        for src, land in zip(out[:na], out[na:])]


def _adamw_sum(parts, w, m, v, name):
    _, r, c = parts.shape
    tr = _tile(r, (256, 128, 64, 32, 16, 8))

    def body(p_ref, w_ref, m_ref, v_ref, g_ref, d_ref, nm_ref, nv_ref):
        _adam_store(_sum_parts(p_ref), w_ref, m_ref, v_ref, g_ref, d_ref, nm_ref, nv_ref)

    row = pl.BlockSpec((tr, c), lambda i: (i, 0))
    shp = jax.ShapeDtypeStruct((r, c), F32)
    return pl.pallas_call(
        body, name=name, grid=(r // tr,), in_specs=[pl.BlockSpec((N_DEV, tr, c), lambda i: (0, i, 0)), row, row, row],
        out_specs=[row, row, row, row], out_shape=[shp, shp, shp, shp], compiler_params=_params("parallel"),
    )(parts, w, m, v)


def _sum_parts(p_ref):
    g = p_ref[0].astype(F32)
    for k in range(1, N_DEV):
        g = g + p_ref[k].astype(F32)
    return g


def _adam_store(g, w_ref, m_ref, v_ref, g_ref, d_ref, nm_ref, nv_ref):
    nm = ADAM_B1 * m_ref[...] + (1.0 - ADAM_B1) * g
    nv = ADAM_B2 * v_ref[...] + (1.0 - ADAM_B2) * (g * g)
    m_hat = nm / (1.0 - ADAM_B1 ** ADAM_STEP)
    v_hat = nv / (1.0 - ADAM_B2 ** ADAM_STEP)
    g_ref[...] = g
    nm_ref[...] = nm
    nv_ref[...] = nv
    d_ref[...] = -ADAM_LR * (m_hat / (jnp.sqrt(v_hat) + ADAM_EPS) + ADAM_WD * w_ref[...])


def _adamw_weight(parts, w, m, v, name):
    _, r, c = w.shape
    tr = _tile(r, (256, 128, 176))
    nr = r // tr

    def body(p0_ref, p1_ref, w_ref, m_ref, v_ref, g_ref, d_ref, nm_ref, nv_ref):
        g = jnp.where(pl.program_id(0) == 0, _sum_parts(p0_ref), _sum_parts(p1_ref))
        _adam_store(g, w_ref, m_ref, v_ref, g_ref, d_ref, nm_ref, nv_ref)

    part = lambda layer: pl.BlockSpec(
        (N_DEV, tr, c), lambda l, i: (0, jnp.where(l == layer, i, (nr - 1) * (1 - layer)), 0))
    row = pl.BlockSpec((None, tr, c), lambda l, i: (l, i, 0))
    shp = jax.ShapeDtypeStruct(w.shape, F32)
    return pl.pallas_call(
        body, name=name, grid=(DEPTH, nr), in_specs=[part(0), part(1), row, row, row],
        out_specs=[row, row, row, row], out_shape=[shp, shp, shp, shp], compiler_params=_params("arbitrary", "arbitrary"),
    )(parts[0], parts[1], w, m, v)


def _full_to_slots(name, t):
    k, n = t.shape
    if name in ROW_SHARDED:
        return t.reshape(N_DEV, k // N_DEV, n)
    return t.reshape(k, N_DEV, n // N_DEV).transpose(1, 0, 2)


def _slots_to_full(name, t):
    _, r, c = t.shape
    if name in ROW_SHARDED:
        return t.reshape(N_DEV * r, c)
    return t.transpose(1, 0, 2).reshape(r, N_DEV * c)


def _small_sizes(shapes, names):
    return [(n, shapes[n], -(-int(math.prod(shapes[n])) // (8 * LANES)) * 8) for n in names]


def _pack_small(tree, shapes, names, row_tile):
    rows = []
    for n, shp, nrow in _small_sizes(shapes, names):
        flat = tree[n].reshape(-1)
        rows.append(jnp.pad(flat, (0, nrow * LANES - flat.shape[0])).reshape(nrow, LANES))
    total = sum(r.shape[0] for r in rows)
    if total % row_tile:
        rows.append(jnp.zeros((-total % row_tile, LANES), F32))
    return jnp.concatenate(rows, axis=0)


def _unpack_small(buf, shapes, names):
    out, r0 = {}, 0
    for n, shp, nrow in _small_sizes(shapes, names):
        out[n] = buf[r0:r0 + nrow].reshape(-1)[:int(math.prod(shp))].reshape(shp)
        r0 += nrow
    return out


def _block_diag(w):
    g = w.shape[0]
    eye = jnp.eye(g, dtype=w.dtype)
    return (eye[:, None, :, None] * w[:, :, None, :]).reshape(g * HEAD_DIM, g * HEAD_DIM)


def kernel(x, positions, norm1, w_in, q_norm, k_norm, sinks, w_pool, pool_scale, sgu_v_norm, w_s, b_s, w_proj_a, w_proj_b, w_proj_c, w_out, norm2, w_up, conv_w, conv_b, w_down, loss_target, m_norm1, m_w_in, m_q_norm, m_k_norm, m_sinks, m_w_pool, m_pool_scale, m_sgu_v_norm, m_w_s, m_b_s, m_w_proj_a, m_w_proj_b, m_w_proj_c, m_w_out, m_norm2, m_w_up, m_conv_w, m_conv_b, m_w_down, v_norm1, v_w_in, v_q_norm, v_k_norm, v_sinks, v_w_pool, v_pool_scale, v_sgu_v_norm, v_w_s, v_b_s, v_w_proj_a, v_w_proj_b, v_w_proj_c, v_w_out, v_norm2, v_w_up, v_conv_w, v_conv_b, v_w_down):
    names = ("norm1", "w_in", "q_norm", "k_norm", "sinks", "w_pool", "pool_scale", "sgu_v_norm", "w_s", "b_s", "w_proj_a",
             "w_proj_b", "w_proj_c", "w_out", "norm2", "w_up", "conv_w", "conv_b", "w_down")
    wts = dict(zip(names, (norm1, w_in, q_norm, k_norm, sinks, w_pool, pool_scale, sgu_v_norm, w_s, b_s, w_proj_a, w_proj_b,
                           w_proj_c, w_out, norm2, w_up, conv_w, conv_b, w_down)))
    mom = dict(zip(names, (m_norm1, m_w_in, m_q_norm, m_k_norm, m_sinks, m_w_pool, m_pool_scale, m_sgu_v_norm, m_w_s, m_b_s,
                           m_w_proj_a, m_w_proj_b, m_w_proj_c, m_w_out, m_norm2, m_w_up, m_conv_w, m_conv_b, m_w_down)))
    var = dict(zip(names, (v_norm1, v_w_in, v_q_norm, v_k_norm, v_sinks, v_w_pool, v_pool_scale, v_sgu_v_norm, v_w_s, v_b_s,
                           v_w_proj_a, v_w_proj_b, v_w_proj_c, v_w_out, v_norm2, v_w_up, v_conv_w, v_conv_b, v_w_down)))
    xs = x[0]
    target = loss_target[0]
    s = xs.shape[0]

    inv_freq = ROPE_THETA ** (-jnp.arange(0, HEAD_DIM, 2, dtype=F32) / HEAD_DIM)
    ang = positions[0].astype(F32)[:, None] * inv_freq
    cosf = jnp.tile(jnp.cos(ang), (1, 4))
    sinf = jnp.tile(jnp.concatenate([-jnp.sin(ang), jnp.sin(ang)], axis=1), (1, 2))

    local = [{n: wts[n][l] if n == "conv_w" else wts[n][l].astype(BF16) for n in SHARDED} for l in range(DEPTH)]
    later = SHARDED[1:]
    full = [{"w_in": _slots_to_full("w_in", _all_gather([local[0]["w_in"]], "gather_w_in_0")[0])}, None]
    gather0 = _push_start([local[0][n] for n in later], False, "gather_rest_0_start")
    norm1_first = norm1[0] + gather0[4][0, 0]

    def layer_consts(l):
        return dict(
            bdw=_block_diag(w_pool[l]).astype(BF16), qn=jnp.tile(q_norm[l], 2).reshape(1, LANES),
            kn=jnp.tile(k_norm[l], 2).reshape(1, LANES), vn=jnp.tile(sgu_v_norm[l], 4).reshape(1, SGU_W),
            bcol=jnp.repeat(b_s[l].T, HEAD_DIM, axis=1),
            cbg=conv_b[l][:D_FF].reshape(1, D_FF), cbv=conv_b[l][D_FF:].reshape(1, D_FF))

    gate_cols, val_cols = (0, D_FF), (D_FF, D_FF)

    saved = []
    cur = xs
    for l in range(DEPTH):
        if l == 1:
            landed = _push_wait(gather1, False, cur, "gather_weights_1_wait")
            full[1] = {n: _slots_to_full(n, t) for n, t in zip(SHARDED, landed)}
        fw, k = full[l], layer_consts(l)
        h1 = _rms_fwd(cur, norm1_first if l == 0 else norm1[l], f"rms1_fwd_{l}")
        z = _mm(h1, fw["w_in"], name=f"in_proj_{l}")
        a = _pool_fwd(z, k["bdw"], pool_scale[l], f"pool_fwd_{l}")
        b = _attn_fwd(z, cosf, sinf, k["qn"], k["kn"], sinks[l], f"attn_fwd_{l}")
        c = _sgu_fwd(z, w_s[l], k["bcol"], k["vn"], f"sgu_fwd_{l}")
        w_proj_a_l = fw.get("w_proj_a")
        if l == 0:
            landed = _push_wait(gather0, False, c, "gather_rest_0_wait")
            fw.update({n: _slots_to_full(n, t) for n, t in zip(later, landed)})
            gather1 = _push_start([local[1][n] for n in SHARDED], False, "gather_weights_1_start")
            w_proj_a_l = fw["w_proj_a"] + gather1[4][0, 0].astype(BF16)
        x1, merged = _merge_out_fwd(a, b, c, w_proj_a_l, fw["w_proj_b"], fw["w_proj_c"], z, fw["w_out"], cur, f"merge_out_fwd_{l}")
        h2 = _rms_fwd(x1, norm2[l], f"rms2_fwd_{l}")
        upg = _mm(h2, fw["w_up"], b_n=gate_cols, name=f"up_gate_{l}")
        upv = _mm(h2, fw["w_up"], b_n=val_cols, name=f"up_val_{l}")
        k["cwg"], k["cwv"] = fw["conv_w"][:, :D_FF], fw["conv_w"][:, D_FF:]
        x2, act = _ffn_down_fwd(upg, upv, k["cwg"], k["cwv"], k["cbg"], k["cbv"], fw["w_down"], x1, f"ffn_down_fwd_{l}")
        saved.append(dict(x0=cur, h1=h1, z=z, a=a, b=b, c=c, merged=merged, x1=x1, h2=h2, upg=upg, upv=upv, act=act))
        cur = x2

    dcur, loss_tile = _loss_head(cur, target)
    loss = lax.psum(loss_tile[0, 0], ("x", "y", "c"))

    gsmall = [None] * DEPTH
    small_shapes = {n: wts[n].shape for n in SMALL}

    def slots_of(grads):
        return [_full_to_slots(n, t) for n, t in grads.items()]

    for l in reversed(range(DEPTH)):
        fw, k, sv = full[l], layer_consts(l), saved[l]
        k["cwg"], k["cwv"] = fw["conv_w"][:, :D_FF], fw["conv_w"][:, D_FF:]
        staged = l == 0
        wgrad = functools.partial(_mm, ta=True, out_dtype=BF16)
        w_down_l = fw["w_down"] + exchange1[4][0, 0].astype(BF16) if staged else fw["w_down"]
        dact = _mm(dcur, w_down_l, tb=True, name=f"down_proj_bwd_{l}")
        g_down = wgrad(sv["act"], dcur, name=f"down_proj_wgrad_{l}")
        dg0, dv0, dcg, dcv = _ffn_bwd(sv["upg"], sv["upv"], k["cwg"], k["cwv"], k["cbg"], k["cbv"], dact, f"ffn_bwd_{l}")
        dh2 = _mm(dg0, fw["w_up"], tb=True, b_k=gate_cols, name=f"up_gate_bwd_{l}")
        dh2 = _mm(dv0, fw["w_up"], tb=True, b_k=val_cols, add=dh2, name=f"up_val_bwd_{l}")
        g_up = wgrad(sv["h2"], dg0, out_cols=(0, 2 * D_FF), name=f"up_gate_wgrad_{l}")
        g_up = wgrad(sv["h2"], dv0, out_cols=(D_FF, 2 * D_FF), out_into=g_up, name=f"up_val_wgrad_{l}")
        g_ffn = dict(w_up=g_up, w_down=g_down, conv_w=jnp.concatenate([dcg[0:3], dcv[0:3]], axis=1))
        norm2_l = norm2[l]
        if staged:
            parts1 = dict(zip(SHARDED, _push_wait(exchange1, True, g_up, "exchange_grads_1_wait")))
            exchange_ffn = _push_start(slots_of(g_ffn), True, "exchange_ffn_0_start")
            norm2_l = norm2_l + exchange_ffn[4][0, 0]
        dx1, g_norm2 = _rms_bwd(sv["x1"], norm2_l, dh2, dcur, f"rms2_bwd_{l}")
        dmerged = _mm(dx1, fw["w_out"], tb=True, name=f"out_proj_bwd_{l}")
        g_out = wgrad(sv["merged"], dx1, name=f"out_proj_wgrad_{l}")
        dz, dya, da = _branch_bwd(0, sv["a"], fw["w_proj_a"], sv["z"], dmerged, None, f"branch_a_bwd_{l}")
        dz, dyb, db = _branch_bwd(1, sv["b"], fw["w_proj_b"], sv["z"], dmerged, dz, f"branch_b_bwd_{l}")
        dz, dyc, dc = _branch_bwd(2, sv["c"], fw["w_proj_c"], sv["z"], dmerged, dz, f"branch_c_bwd_{l}")
        g_mix = dict(w_proj_a=wgrad(sv["a"], dya, name=f"proj_a_wgrad_{l}"), w_proj_b=wgrad(sv["b"], dyb, name=f"proj_b_wgrad_{l}"),
                     w_proj_c=wgrad(sv["c"], dyc, name=f"proj_c_wgrad_{l}"), w_out=g_out)
        pool_scale_l = pool_scale[l]
        if staged:
            exchange_mix = _push_start(slots_of(g_mix), True, "exchange_mixer_0_start")
            pool_scale_l = pool_scale_l + exchange_mix[4][0, 0]
        dxp, g_bdw, g_pscale = _pool_bwd(sv["z"], da, k["bdw"], pool_scale_l, f"pool_bwd_{l}")
        dq, dkc, dkp, dvc, dvp, g_qn, g_sink = _attn_bwd(sv["z"], cosf, sinf, k["qn"], k["kn"], sinks[l], db, f"attn_bwd_{l}")
        duv, g_ws, g_bacc, g_vn = _sgu_bwd(sv["z"], w_s[l], k["bcol"], k["vn"], dc, f"sgu_bwd_{l}")
        gsmall[l] = dict(
            q_norm=g_qn[0, :HEAD_DIM], sinks=g_sink[:, 0],
            w_pool=jnp.stack([g_bdw[g * HEAD_DIM:(g + 1) * HEAD_DIM, g * HEAD_DIM:(g + 1) * HEAD_DIM] for g in range(4)]),
            pool_scale=g_pscale[0], sgu_v_norm=g_vn[0, :HEAD_DIM], w_s=g_ws, b_s=g_bacc[:, ::HEAD_DIM].T,
            norm2=g_norm2[0], conv_b=jnp.concatenate([dcg[3], dcv[3]]))
        kn_l = k["kn"]
        if staged:
            early = _pack_small({n: jnp.stack([gsmall[i][n] for i in range(DEPTH)]) for n in SMALL_EARLY}, small_shapes,
                                SMALL_EARLY, SMALL_ROW_TILE)
            gather_early = _push_start([early], False, "gather_small_grads_start")
            kn_l = kn_l + gather_early[4][0, 0]
        dz, g_kn = _kv_post(sv["z"], cosf, sinf, kn_l, dkc, dkp, dvc, dvp, dxp, dq, duv, dz, f"kv_post_{l}")
        g_in = dict(w_in=wgrad(sv["h1"], dz, name=f"in_proj_wgrad_{l}"))
        norm1_l = norm1[l]
        if staged:
            exchange_in = _push_start(slots_of(g_in), True, "exchange_w_in_0_start")
            norm1_l = norm1_l + exchange_in[4][0, 0]
        dh1 = _mm(dz, fw["w_in"], tb=True, name=f"in_proj_bwd_{l}")
        dcur, g_norm1 = _rms_bwd(sv["x0"], norm1_l, dh1, dx1, f"rms1_bwd_{l}")
        if not staged:
            exchange1 = _push_start(slots_of({n: {**g_in, **g_mix, **g_ffn}[n] for n in SHARDED}), True, "exchange_grads_1_start")
        gsmall[l].update(norm1=g_norm1[0], k_norm=g_kn[0, :HEAD_DIM])
    grad_x = dcur[None]

    def update_small(gathered, names, row_tile, name):
        pack = lambda tree: _pack_small(tree, small_shapes, names, row_tile)
        return [_unpack_small(t, small_shapes, names) for t in _adamw_sum(gathered, pack(wts), pack(mom), pack(var), name)]

    late = _pack_small({n: jnp.stack([gsmall[i][n] for i in range(DEPTH)]) for n in SMALL_LATE}, small_shapes, SMALL_LATE, 8)
    small = update_small(_all_gather([late], "gather_late_small_grads")[0], SMALL_LATE, 8, "adamw_replicated_late")
    early_all = _push_wait(gather_early, False, small[0]["norm1"], "gather_small_grads_wait")[0]
    for kind, tree in enumerate(update_small(early_all, SMALL_EARLY, SMALL_ROW_TILE, "adamw_replicated")):
        small[kind].update(tree)
    g_s = small[0]["norm2"]

    parts0 = dict(zip(g_ffn, _push_wait(exchange_ffn, True, g_s, "exchange_ffn_0_wait")))
    parts0.update(zip(g_mix, _push_wait(exchange_mix, True, g_s, "exchange_mixer_0_wait")))
    update = lambda n: _adamw_weight([parts0[n], parts1[n]], wts[n], mom[n], var[n], f"adamw_{n}")
    big = {n: update(n) for n in SHARDED[1:]}
    parts0.update(zip(g_in, _push_wait(exchange_in, True, big["w_up"][0], "exchange_w_in_0_wait")))
    big["w_in"] = update("w_in")

    outs = [loss, grad_x]
    for kind in range(4):
        outs += [small[kind][n] if n in SMALL else big[n][kind] for n in names]
    return tuple(outs)
```

```python
import functools
import math

import jax
import jax.numpy as jnp
import numpy as np
from jax import lax
from jax.experimental import pallas as pl
from jax.experimental.pallas import tpu as pltpu

F32 = jnp.float32
BF16 = jnp.bfloat16

D_MODEL = 1024
DEPTH = 2
HEAD_DIM = 64
N_Q_HEADS = 8
Q_PER_KV = 4
BLOCK = 128
POOL_W = 256
ATTN_W = 512
KV_W = 128
SGU_W = 256
IN_COLS = 4608
GATE_COL0 = 1536
D_FF = 2816
EPS = 1e-6
ROPE_THETA = 10000.0
N_DEV = 8
LANES = 128
HALO_POOL = 16
HALO_CONV = 8

ADAM_LR = 0.001
ADAM_B1 = 0.9
ADAM_B2 = 0.999
ADAM_EPS = 1e-08
ADAM_WD = 0.01
ADAM_STEP = 10

VMEM_LIMIT = 48 * 1024 * 1024
MM_VMEM_BUDGET = 40 * 1024 * 1024

SHARDED = ("w_in", "w_proj_a", "w_proj_b", "w_proj_c", "w_out", "w_up", "w_down", "conv_w")
ROW_SHARDED = ("w_out", "w_down")
SMALL_ROW_TILE = 256
SMALL = ("norm1", "q_norm", "k_norm", "sinks", "w_pool", "pool_scale", "sgu_v_norm", "w_s", "b_s", "norm2", "conv_b")
SMALL_LATE = ("norm1", "k_norm")
SMALL_EARLY = tuple(n for n in SMALL if n not in SMALL_LATE)

_GELU_C = math.sqrt(2.0 / math.pi)
_GELU_A = 0.044715


def _params(*sem):
    return pltpu.CompilerParams(dimension_semantics=sem, vmem_limit_bytes=VMEM_LIMIT)


def _tile(n, prefs):
    for t in prefs:
        if t <= n and n % t == 0:
            return t
    return n


def _head_mean_matrix():
    r = lax.broadcasted_iota(jnp.int32, (LANES, LANES), 0)
    c = lax.broadcasted_iota(jnp.int32, (LANES, LANES), 1)
    return jnp.where((r >= HEAD_DIM) == (c >= HEAD_DIM), 1.0 / HEAD_DIM, 0.0).astype(BF16)


def _head_mean(v, bd):
    hi = v.astype(BF16)
    rest = v - hi.astype(F32)
    mid = rest.astype(BF16)
    lo = (rest - mid.astype(F32)).astype(BF16)
    mm = lambda p: jnp.dot(p, bd, preferred_element_type=F32)
    return mm(hi) + (mm(mid) + mm(lo))


def _rot_half(t):
    lane = lax.broadcasted_iota(jnp.int32, t.shape, 1)
    return jnp.where((lane & 32) == 0, pltpu.roll(t, LANES - 32, 1), pltpu.roll(t, 32, 1))


def _norm_rope(t, gn, cosf, sinf, bd):
    r = lax.rsqrt(_head_mean(t * t, bd) + EPS)
    n = t * r
    y = n * gn
    return y * cosf + _rot_half(y) * sinf, n, r


def _norm_rope_bwd(d, t, n, r, gn, cosf, sinf, bd):
    dy = d * cosf + _rot_half(d * sinf)
    dgn = jnp.sum(dy * n, axis=0, keepdims=True)
    u = dy * gn
    dt = r * u - t * (r * r * r) * _head_mean(t * u, bd)
    return dt, dgn


def _gelu(x):
    t = jnp.tanh(_GELU_C * (x + _GELU_A * (x * x * x)))
    return 0.5 * x * (1.0 + t), t


def _gelu_grad(x, t):
    return 0.5 * (1.0 + t) + 0.5 * x * (1.0 - t * t) * (_GELU_C * (1.0 + 3.0 * _GELU_A * x * x))


def _sigmoid(x):
    return jax.nn.sigmoid(x)


def _dot(a, b, ca=1, cb=0):
    return lax.dot_general(a.astype(BF16), b.astype(BF16), (((ca,), (cb,)), ((), ())), preferred_element_type=F32)


def _mm(a, b, *, ta=False, tb=False, add=None, out_dtype=F32, name, b_n=None, b_k=None, out_cols=None, out_into=None):
    m = a.shape[1] if ta else a.shape[0]
    k = a.shape[0] if ta else a.shape[1]
    n = b_n[1] if b_n else (b.shape[0] if tb else b.shape[1])
    tn = _tile(n, (1024, 1152, 1408, 512, 256, 128))
    has_add = add is not None
    fits = []
    for tm in (2048, 1024, 1408, 512, 256, 128):
        for tk in (k, 4608, 2816, 2048, 1408, 1152, 1024, 512, 256, 128):
            if tm <= m and m % tm == 0 and tk <= k and k % tk == 0:
                need = (2 * (tm * tk * a.dtype.itemsize + tk * tn * b.dtype.itemsize) + 2 * tm * tn * jnp.dtype(out_dtype).itemsize
                        + 2 * tm * tn * 4 * has_add + tm * tn * 4 * (tk < k))
                if need <= MM_VMEM_BUDGET:
                    fits.append((k // tk, -tm, tm, tk))
    if fits:
        _, _, tm, tk = min(fits)
    else:
        tm, tk = _tile(m, (256, 128)), _tile(k, (512, 256, 128))
    nk = k // tk
    n0 = b_n[0] // tn if b_n else 0
    k0 = b_k[0] // tk if b_k else 0
    o0, n_out = (out_cols[0] // tn, out_cols[1]) if out_cols else (0, n)
    n_in = 2 + has_add + (out_into is not None)

    def body(*refs):
        a_ref, b_ref = refs[0], refs[1]
        add_ref = refs[2] if has_add else None
        o_ref = refs[n_in]
        def finish(r):
            if has_add:
                r = r + add_ref[...]
            o_ref[...] = r.astype(out_dtype)

        if nk == 1:
            finish(_dot(a_ref[...], b_ref[...], 0 if ta else 1, 1 if tb else 0))
        else:
            acc_ref = refs[-1]
            kk = pl.program_id(2)

            @pl.when(kk == 0)
            def _():
                acc_ref[...] = jnp.zeros_like(acc_ref)

            acc_ref[...] += _dot(a_ref[...], b_ref[...], 0 if ta else 1, 1 if tb else 0)

            @pl.when(kk == nk - 1)
            def _():
                finish(acc_ref[...])

    a_spec = pl.BlockSpec((tk, tm), lambda i, j, kk: (kk, i)) if ta else pl.BlockSpec((tm, tk), lambda i, j, kk: (i, kk))
    if tb:
        b_spec = pl.BlockSpec((tn, tk), lambda i, j, kk: (j + n0, kk + k0))
    else:
        b_spec = pl.BlockSpec((tk, tn), lambda i, j, kk: (kk + k0, j + n0))
    in_specs = [a_spec, b_spec] + ([pl.BlockSpec((tm, tn), lambda i, j, kk: (i, j))] if has_add else [])
    args = (a, b) + ((add,) if has_add else ())
    if out_into is not None:
        in_specs.append(pl.BlockSpec(memory_space=pl.ANY))
        args += (out_into,)
    return pl.pallas_call(
        body, name=name, grid=(m // tm, n // tn, nk), in_specs=in_specs,
        out_specs=pl.BlockSpec((tm, tn), lambda i, j, kk: (i, j + o0)),
        out_shape=jax.ShapeDtypeStruct((m, n_out), out_dtype),
        scratch_shapes=[pltpu.VMEM((tm, tn), F32)] if nk > 1 else [],
        input_output_aliases={n_in - 1: 0} if out_into is not None else {},
        compiler_params=_params("parallel", "parallel", "arbitrary"),
    )(*args)


def _rms_fwd(x, g, name):
    s, d = x.shape
    tr = _tile(s, (512, 256, 128))

    def body(x_ref, g_ref, h_ref):
        xv = x_ref[...]
        r = lax.rsqrt(jnp.mean(xv * xv, axis=-1, keepdims=True) + EPS)
        h_ref[...] = ((xv * r) * g_ref[...]).astype(BF16)

    return pl.pallas_call(
        body, name=name, grid=(s // tr,),
        in_specs=[pl.BlockSpec((tr, d), lambda i: (i, 0)), pl.BlockSpec((1, d), lambda i: (0, 0))],
        out_specs=pl.BlockSpec((tr, d), lambda i: (i, 0)),
        out_shape=jax.ShapeDtypeStruct((s, d), BF16), compiler_params=_params("parallel"),
    )(x, g.reshape(1, d))


def _rms_bwd(x, g, dh, dres, name):
    s, d = x.shape
    tr = _tile(s, (512, 256, 128))

    def body(x_ref, g_ref, dh_ref, dres_ref, dx_ref, dg_ref):
        xv = x_ref[...]
        r = lax.rsqrt(jnp.mean(xv * xv, axis=-1, keepdims=True) + EPS)
        dhv = dh_ref[...]
        u = dhv * g_ref[...]
        dx_ref[...] = dres_ref[...] + (r * u - xv * (r * r * r) * jnp.mean(xv * u, axis=-1, keepdims=True))
        part = jnp.sum(dhv * (xv * r), axis=0, keepdims=True)

        @pl.when(pl.program_id(0) == 0)
        def _():
            dg_ref[...] = part

        @pl.when(pl.program_id(0) > 0)
        def _():
            dg_ref[...] += part

    row = pl.BlockSpec((tr, d), lambda i: (i, 0))
    vec = pl.BlockSpec((1, d), lambda i: (0, 0))
    return pl.pallas_call(
        body, name=name, grid=(s // tr,), in_specs=[row, vec, row, row], out_specs=[row, vec],
        out_shape=[jax.ShapeDtypeStruct((s, d), F32), jax.ShapeDtypeStruct((1, d), F32)],
        compiler_params=_params("arbitrary"),
    )(x, g.reshape(1, d), dh, dres)


def _loss_head(y, target):
    s, d = y.shape
    tr = _tile(s, (512, 256, 128))

    def body(y_ref, t_ref, dy_ref, l_ref):
        err = y_ref[...] - t_ref[...]
        dy_ref[...] = err * (1.0 / d)
        part = jnp.sum(jnp.sum(err * err, axis=-1, keepdims=True) * (1.0 / d), axis=0, keepdims=True) * 0.5
        part = jnp.broadcast_to(part, (8, LANES))

        @pl.when(pl.program_id(0) == 0)
        def _():
            l_ref[...] = part

        @pl.when(pl.program_id(0) > 0)
        def _():
            l_ref[...] += part

    row = pl.BlockSpec((tr, d), lambda i: (i, 0))
    acc = pl.BlockSpec((8, LANES), lambda i: (0, 0))
    return pl.pallas_call(
        body, name="loss_head", grid=(s // tr,), in_specs=[row, row], out_specs=[row, acc],
        out_shape=[jax.ShapeDtypeStruct((s, d), F32), jax.ShapeDtypeStruct((8, LANES), F32)],
        compiler_params=_params("arbitrary"),
    )(y, target)


def _pool_lane_select(lane, v2, v4, v8, v16):
    return jnp.where(lane < 64, v2, jnp.where(lane < 128, v4, jnp.where(lane < 192, v8, v16)))


def _pool_diff(xc, xp, row0):
    n = BLOCK + HALO_POOL
    cat = jnp.concatenate([xp, xc], axis=0)
    s2 = cat + pltpu.roll(cat, 1, 0)
    s4 = s2 + pltpu.roll(s2, 2, 0)
    s8 = s4 + pltpu.roll(s4, 4, 0)
    s16 = s8 + pltpu.roll(s8, 8, 0)
    lane = lax.broadcasted_iota(jnp.int32, (n, POOL_W), 1)
    wsum = _pool_lane_select(lane, s2, s4, s8, s16)[HALO_POOL:]
    return wsum / _pool_count(row0, BLOCK) - xc


def _pool_count(row0, rows):
    lane = lax.broadcasted_iota(jnp.int32, (rows, POOL_W), 1)
    t = lax.broadcasted_iota(jnp.int32, (rows, POOL_W), 0) + row0
    return jnp.minimum(t + 1, _pool_lane_select(lane, 2, 4, 8, 16)).astype(F32)


def _pool_fwd(z, bdw, scale, name):
    s = z.shape[0]
    nb = s // BLOCK

    def body(xc_ref, xp_ref, w_ref, sc_ref, a_ref):
        i = pl.program_id(0)
        xp = jnp.where(i > 0, xp_ref[...], 0.0)
        diff = _pool_diff(xc_ref[...], xp, i * BLOCK)
        a_ref[...] = (_dot(diff, w_ref[...]) * sc_ref[...]).astype(BF16)

    return pl.pallas_call(
        body, name=name, grid=(nb,),
        in_specs=[pl.BlockSpec((BLOCK, POOL_W), lambda i: (i, 0)),
                  pl.BlockSpec((HALO_POOL, POOL_W), lambda i: (jnp.maximum(i * (BLOCK // HALO_POOL) - 1, 0), 0)),
                  pl.BlockSpec((POOL_W, POOL_W), lambda i: (0, 0)),
                  pl.BlockSpec((1, POOL_W), lambda i: (0, 0))],
        out_specs=pl.BlockSpec((BLOCK, POOL_W), lambda i: (i, 0)),
        out_shape=jax.ShapeDtypeStruct((s, POOL_W), BF16), compiler_params=_params("parallel"),
    )(z, z, bdw, scale.reshape(1, POOL_W))


def _pool_bwd(z, da, bdw, scale, name):
    s = z.shape[0]
    nb = s // BLOCK
    per = BLOCK // HALO_POOL
    n = BLOCK + HALO_POOL

    def body(xc_ref, xp_ref, dac_ref, dan_ref, w_ref, sc_ref, dx_ref, dw_ref, dsc_ref):
        i = pl.program_id(0)
        xp = jnp.where(i > 0, xp_ref[...], 0.0)
        diff = _pool_diff(xc_ref[...], xp, i * BLOCK)
        mixed = _dot(diff, w_ref[...])
        dac = dac_ref[...]
        dan = jnp.where(i < nb - 1, dan_ref[...], 0.0)
        dmix = jnp.concatenate([dac, dan], axis=0) * sc_ref[...]
        ddiff = _dot(dmix, w_ref[...], 1, 1)
        e = ddiff / _pool_count(i * BLOCK, n)
        f2 = e + pltpu.roll(e, n - 1, 0)
        f4 = f2 + pltpu.roll(f2, n - 2, 0)
        f8 = f4 + pltpu.roll(f4, n - 4, 0)
        f16 = f8 + pltpu.roll(f8, n - 8, 0)
        lane = lax.broadcasted_iota(jnp.int32, (n, POOL_W), 1)
        back = _pool_lane_select(lane, f2, f4, f8, f16)
        dx_ref[...] = (back[:BLOCK] - ddiff[:BLOCK]).astype(BF16)
        dw = _dot(diff, dmix[:BLOCK], 0, 0)
        dsc = jnp.sum(dac * mixed, axis=0, keepdims=True)

        @pl.when(i == 0)
        def _():
            dw_ref[...] = dw
            dsc_ref[...] = dsc

        @pl.when(i > 0)
        def _():
            dw_ref[...] += dw
            dsc_ref[...] += dsc

    blk = pl.BlockSpec((BLOCK, POOL_W), lambda i: (i, 0))
    return pl.pallas_call(
        body, name=name, grid=(nb,),
        in_specs=[blk, pl.BlockSpec((HALO_POOL, POOL_W), lambda i: (jnp.maximum(i * per - 1, 0), 0)),
                  blk, pl.BlockSpec((HALO_POOL, POOL_W), lambda i: (jnp.minimum((i + 1) * per, nb * per - 1), 0)),
                  pl.BlockSpec((POOL_W, POOL_W), lambda i: (0, 0)), pl.BlockSpec((1, POOL_W), lambda i: (0, 0))],
        out_specs=[blk, pl.BlockSpec((POOL_W, POOL_W), lambda i: (0, 0)), pl.BlockSpec((1, POOL_W), lambda i: (0, 0))],
        out_shape=[jax.ShapeDtypeStruct((s, POOL_W), BF16), jax.ShapeDtypeStruct((POOL_W, POOL_W), F32),
                   jax.ShapeDtypeStruct((1, POOL_W), F32)],
        compiler_params=_params("arbitrary"),
    )(z, z, da, da, bdw, scale.reshape(1, POOL_W))


def _attn_setup(zc_ref, zp_ref, cc_ref, cp_ref, sc_ref, sp_ref, qn_ref, kn_ref, bd):
    q = []
    for j in range(ATTN_W // LANES):
        t = zc_ref[:, POOL_W + j * LANES:POOL_W + (j + 1) * LANES]
        q.append((t,) + _norm_rope(t, qn_ref[...], cc_ref[...], sc_ref[...], bd))
    kc_raw = zc_ref[:, POOL_W + ATTN_W:POOL_W + ATTN_W + KV_W]
    kc = _norm_rope(kc_raw, kn_ref[...], cc_ref[...], sc_ref[...], bd)[0]
    kp = _norm_rope(zp_ref[:, :KV_W], kn_ref[...], cp_ref[...], sp_ref[...], bd)[0]
    kband = jnp.concatenate([kp, kc], axis=0).astype(BF16)
    vband = jnp.concatenate([zp_ref[:, KV_W:], zc_ref[:, POOL_W + ATTN_W + KV_W:POOL_W + ATTN_W + 2 * KV_W]], axis=0).astype(BF16)
    return q, kband, vband


MASKED = -1e30


def _window_bias(first_block):
    row = np.arange(Q_PER_KV * BLOCK)[:, None] % BLOCK
    col = np.arange(2 * BLOCK)[None, :]
    dist = row + BLOCK - col
    inside = (dist >= 0) & (dist < BLOCK) & ((col >= BLOCK) | (not first_block))
    return jnp.asarray(np.where(inside, 0.0, MASKED), F32)


def _stack_heads(tiles, kvh):
    return jnp.concatenate([_to_kv_lanes(tiles[h // 2], h) for h in range(kvh * Q_PER_KV, (kvh + 1) * Q_PER_KV)], axis=0)


def _unstack_heads(stacked, kvh, tiles):
    for g in range(Q_PER_KV):
        h = kvh * Q_PER_KV + g
        t = _from_kv_lanes(stacked[g * BLOCK:(g + 1) * BLOCK], h)
        tiles[h // 2] = t if tiles[h // 2] is None else tiles[h // 2] + t


def _sink_column(sink_ref, kvh):
    grp = lax.broadcasted_iota(jnp.int32, (Q_PER_KV * BLOCK, 1), 0) >> 7
    s = [sink_ref[kvh * Q_PER_KV + g] for g in range(Q_PER_KV)]
    return jnp.where(grp == 0, s[0], jnp.where(grp == 1, s[1], jnp.where(grp == 2, s[2], s[3])))


def _to_kv_lanes(t, h):
    kvh = h // Q_PER_KV
    if (h % 2) != kvh:
        t = pltpu.roll(t, HEAD_DIM, 1)
    lane = lax.broadcasted_iota(jnp.int32, t.shape, 1)
    return jnp.where((lane >= HEAD_DIM) == (kvh == 1), t, 0.0)


def _from_kv_lanes(t, h):
    kvh = h // Q_PER_KV
    lane = lax.broadcasted_iota(jnp.int32, t.shape, 1)
    t = jnp.where((lane >= HEAD_DIM) == (kvh == 1), t, 0.0)
    if (h % 2) != kvh:
        t = pltpu.roll(t, HEAD_DIM, 1)
    return t


def _attn_probs(qh, kband, bias, sink):
    sc = _dot(qh, kband, 1, 1) * (HEAD_DIM ** -0.5) + bias
    m = jnp.maximum(jnp.max(sc, axis=1, keepdims=True), sink)
    p = jnp.exp(sc - m)
    psink = jnp.exp(sink - m)
    den = jnp.sum(p, axis=1, keepdims=True) + psink
    return p / den, psink / den


def _attn_specs(nb):
    cur = lambda i: (i, 0)
    prev = lambda i: (jnp.maximum(i - 1, 0), 0)
    tab = lambda f: pl.BlockSpec((BLOCK, LANES), f)
    vec = pl.BlockSpec((1, LANES), lambda i: (0, 0))
    return [pl.BlockSpec((BLOCK, 1024), cur),
            pl.BlockSpec((BLOCK, 2 * KV_W), lambda i: (jnp.maximum(i - 1, 0), 3)),
            tab(cur), tab(prev), tab(cur), tab(prev), vec, vec,
            pl.BlockSpec(memory_space=pltpu.SMEM)] + [pl.BlockSpec((Q_PER_KV * BLOCK, 2 * BLOCK), lambda i: (0, 0))] * 2


def _attn_fwd(z, cosf, sinf, qn, kn, sinks, name):
    s = z.shape[0]
    nb = s // BLOCK

    def body(zc_ref, zp_ref, cc_ref, cp_ref, sc_ref, sp_ref, qn_ref, kn_ref, sink_ref, bias_ref, bias0_ref, o_ref):
        i = pl.program_id(0)
        bd = _head_mean_matrix()
        q, kband, vband = _attn_setup(zc_ref, zp_ref, cc_ref, cp_ref, sc_ref, sp_ref, qn_ref, kn_ref, bd)
        mask = jnp.where(i > 0, bias_ref[...], bias0_ref[...])
        out = [None] * (ATTN_W // LANES)
        for kvh in range(N_Q_HEADS // Q_PER_KV):
            qs = _stack_heads([t[1] for t in q], kvh)
            probs, _ = _attn_probs(qs, kband, mask, _sink_column(sink_ref, kvh))
            _unstack_heads(_dot(probs, vband), kvh, out)
        for j, o in enumerate(out):
            o_ref[:, j * LANES:(j + 1) * LANES] = o.astype(BF16)

    return pl.pallas_call(
        body, name=name, grid=(nb,), in_specs=_attn_specs(nb),
        out_specs=pl.BlockSpec((BLOCK, ATTN_W), lambda i: (i, 0)),
        out_shape=jax.ShapeDtypeStruct((s, ATTN_W), BF16), compiler_params=_params("parallel"),
    )(z, z, cosf, cosf, sinf, sinf, qn, kn, sinks, _window_bias(False), _window_bias(True))


def _attn_bwd(z, cosf, sinf, qn, kn, sinks, d_out, name):
    s = z.shape[0]
    nb = s // BLOCK
    nt = ATTN_W // LANES

    def body(zc_ref, zp_ref, cc_ref, cp_ref, sc_ref, sp_ref, qn_ref, kn_ref, sink_ref, bias_ref, bias0_ref, do_ref,
             dq_ref, dkc_ref, dkp_ref, dvc_ref, dvp_ref, dqn_ref, dsink_ref):
        i = pl.program_id(0)
        bd = _head_mean_matrix()
        q, kband, vband = _attn_setup(zc_ref, zp_ref, cc_ref, cp_ref, sc_ref, sp_ref, qn_ref, kn_ref, bd)
        mask = jnp.where(i > 0, bias_ref[...], bias0_ref[...])

        @pl.when(i == 0)
        def _():
            dqn_ref[...] = jnp.zeros_like(dqn_ref)
            dsink_ref[...] = jnp.zeros_like(dsink_ref)

        dq = [None] * nt
        dk = jnp.zeros((2 * BLOCK, KV_W), F32)
        dv = jnp.zeros((2 * BLOCK, KV_W), F32)
        d_tiles = [do_ref[:, j * LANES:(j + 1) * LANES] for j in range(nt)]
        for kvh in range(N_Q_HEADS // Q_PER_KV):
            qs = _stack_heads([t[1] for t in q], kvh)
            probs, psink = _attn_probs(qs, kband, mask, _sink_column(sink_ref, kvh))
            dos = _stack_heads(d_tiles, kvh)
            dp = _dot(dos, vband, 1, 1)
            delta = jnp.sum(dp * probs, axis=1, keepdims=True)
            ds = (probs * (dp - delta)) * (HEAD_DIM ** -0.5)
            dsink = -psink * delta
            for g in range(Q_PER_KV):
                h = kvh * Q_PER_KV + g
                dsink_ref[h:h + 1, :] += jnp.broadcast_to(jnp.sum(dsink[g * BLOCK:(g + 1) * BLOCK], axis=0, keepdims=True), (1, LANES))
            _unstack_heads(_dot(ds, kband), kvh, dq)
            dk = dk + _dot(ds, qs, 0, 0)
            dv = dv + _dot(probs, dos, 0, 0)
        dgn = jnp.zeros((1, LANES), F32)
        for j in range(nt):
            t, _, n, r = q[j]
            dt, g = _norm_rope_bwd(dq[j], t, n, r, qn_ref[...], cc_ref[...], sc_ref[...], bd)
            dq_ref[:, j * LANES:(j + 1) * LANES] = dt.astype(BF16)
            dgn = dgn + g
        dqn_ref[...] += jnp.broadcast_to(dgn, (8, LANES))
        dkp_ref[...] = dk[:BLOCK]
        dkc_ref[...] = dk[BLOCK:]
        dvp_ref[...] = dv[:BLOCK]
        dvc_ref[...] = dv[BLOCK:]

        @pl.when(i == nb - 1)
        def _():
            acc = dqn_ref[...]
            dqn_ref[...] = acc + pltpu.roll(acc, HEAD_DIM, 1)

    blk = lambda w: pl.BlockSpec((BLOCK, w), lambda i: (i, 0))
    acc = pl.BlockSpec((8, LANES), lambda i: (0, 0))
    kv = jax.ShapeDtypeStruct((s, KV_W), F32)
    return pl.pallas_call(
        body, name=name, grid=(nb,), in_specs=_attn_specs(nb) + [blk(ATTN_W)],
        out_specs=[blk(ATTN_W), blk(KV_W), blk(KV_W), blk(KV_W), blk(KV_W), acc, acc],
        out_shape=[jax.ShapeDtypeStruct((s, ATTN_W), BF16), kv, kv, kv, kv,
                   jax.ShapeDtypeStruct((8, LANES), F32), jax.ShapeDtypeStruct((8, LANES), F32)],
        compiler_params=_params("arbitrary"),
    )(z, z, cosf, cosf, sinf, sinf, qn, kn, sinks, _window_bias(False), _window_bias(True), d_out)


def _kv_post(z, cosf, sinf, kn, dkc, dkp, dvc, dvp, dxp, dq, duv, dz, name):
    s = z.shape[0]
    nb = s // BLOCK

    def body(zk_ref, c_ref, s_ref, kn_ref, dkc_ref, dkp_ref, dvc_ref, dvp_ref, dxp_ref, dq_ref, duv_ref, dz_in,
             dz_ref, dkn_ref):
        j = pl.program_id(0)
        bd = _head_mean_matrix()
        last = j == nb - 1
        d = dkc_ref[...] + jnp.where(last, 0.0, dkp_ref[...])
        t = zk_ref[:, :KV_W]
        _, n, r = _norm_rope(t, kn_ref[...], c_ref[...], s_ref[...], bd)
        dt, g = _norm_rope_bwd(d, t, n, r, kn_ref[...], c_ref[...], s_ref[...], bd)
        dvv = dvc_ref[...] + jnp.where(last, 0.0, dvp_ref[...])
        dz_ref[:, 0:POOL_W] = dxp_ref[...]
        dz_ref[:, POOL_W:POOL_W + ATTN_W] = dq_ref[...]
        dz_ref[:, POOL_W + ATTN_W:POOL_W + ATTN_W + KV_W] = dt.astype(BF16)
        dz_ref[:, POOL_W + ATTN_W + KV_W:POOL_W + ATTN_W + 2 * KV_W] = dvv.astype(BF16)
        dz_ref[:, POOL_W + ATTN_W + 2 * KV_W:GATE_COL0] = duv_ref[...]

        @pl.when(j == 0)
        def _():
            dkn_ref[...] = jnp.zeros_like(dkn_ref)

        dkn_ref[...] += jnp.broadcast_to(g, (8, LANES))

        @pl.when(last)
        def _():
            acc = dkn_ref[...]
            dkn_ref[...] = acc + pltpu.roll(acc, HEAD_DIM, 1)

    cur = lambda w: pl.BlockSpec((BLOCK, w), lambda j: (j, 0))
    nxt = pl.BlockSpec((BLOCK, KV_W), lambda j: (jnp.minimum(j + 1, nb - 1), 0))
    vec = pl.BlockSpec((1, LANES), lambda j: (0, 0))
    return pl.pallas_call(
        body, name=name, grid=(nb,),
        in_specs=[pl.BlockSpec((BLOCK, 2 * KV_W), lambda j: (j, 3)), cur(LANES), cur(LANES), vec,
                  cur(KV_W), nxt, cur(KV_W), nxt, cur(POOL_W), cur(ATTN_W), cur(2 * SGU_W),
                  pl.BlockSpec(memory_space=pl.ANY)],
        out_specs=[pl.BlockSpec((BLOCK, GATE_COL0), lambda j: (j, 0)), pl.BlockSpec((8, LANES), lambda j: (0, 0))],
        out_shape=[jax.ShapeDtypeStruct(dz.shape, dz.dtype), jax.ShapeDtypeStruct((8, LANES), F32)],
        input_output_aliases={11: 0}, compiler_params=_params("arbitrary"),
    )(z, cosf, sinf, kn, dkc, dkp, dvc, dvp, dxp, dq, duv, dz)


def _sgu_setup(z_ref, ws_ref, vn_ref, bd):
    us = z_ref[:, :SGU_W]
    vs = z_ref[:, SGU_W:]
    ug, tu = _gelu(us)
    gv, tv = _gelu(vs)
    rr = jnp.concatenate([lax.rsqrt(_head_mean(gv[:, k * LANES:(k + 1) * LANES] ** 2, bd) + EPS) for k in range(2)], axis=1)
    vg = (gv * rr) * vn_ref[...]
    tril = lax.broadcasted_iota(jnp.int32, (BLOCK, BLOCK), 0) >= lax.broadcasted_iota(jnp.int32, (BLOCK, BLOCK), 1)
    w = [jnp.where(tril, ws_ref[g], 0.0).astype(BF16) for g in range(4)]
    return us, vs, ug, tu, gv, tv, rr, vg, w, tril


def _group_select(parts):
    lane = lax.broadcasted_iota(jnp.int32, parts[0].shape, 1)
    return _pool_lane_select(lane, *parts)


def _sgu_fwd(z, ws, bcol, vn, name):
    s = z.shape[0]
    nb = s // BLOCK

    def body(z_ref, ws_ref, b_ref, vn_ref, c_ref):
        bd = _head_mean_matrix()
        _, _, ug, _, _, _, _, vg, w, _ = _sgu_setup(z_ref, ws_ref, vn_ref, bd)
        sg = _group_select([_dot(w[g], vg) for g in range(4)]) + b_ref[...]
        c_ref[...] = (ug * sg).astype(BF16)

    return pl.pallas_call(
        body, name=name, grid=(nb,),
        in_specs=[pl.BlockSpec((BLOCK, 2 * SGU_W), lambda i: (i, 2)), pl.BlockSpec((4, BLOCK, BLOCK), lambda i: (0, 0, 0)),
                  pl.BlockSpec((BLOCK, SGU_W), lambda i: (0, 0)), pl.BlockSpec((1, SGU_W), lambda i: (0, 0))],
        out_specs=pl.BlockSpec((BLOCK, SGU_W), lambda i: (i, 0)),
        out_shape=jax.ShapeDtypeStruct((s, SGU_W), BF16), compiler_params=_params("parallel"),
    )(z, ws, bcol, vn)


def _sgu_bwd(z, ws, bcol, vn, dc, name):
    s = z.shape[0]
    nb = s // BLOCK

    def body(z_ref, ws_ref, b_ref, vn_ref, dc_ref, duv_ref, dws_ref, db_ref, dvn_ref):
        i = pl.program_id(0)
        bd = _head_mean_matrix()
        us, vs, ug, tu, gv, tv, rr, vg, w, tril = _sgu_setup(z_ref, ws_ref, vn_ref, bd)
        sg = _group_select([_dot(w[g], vg) for g in range(4)]) + b_ref[...]
        dcv = dc_ref[...]
        dug = dcv * sg
        dsg = dcv * ug
        lane = lax.broadcasted_iota(jnp.int32, dsg.shape, 1)

        @pl.when(i == 0)
        def _():
            dws_ref[...] = jnp.zeros_like(dws_ref)
            db_ref[...] = jnp.zeros_like(db_ref)
            dvn_ref[...] = jnp.zeros_like(dvn_ref)

        for g in range(4):
            dsg_g = jnp.where((lane >= g * HEAD_DIM) & (lane < (g + 1) * HEAD_DIM), dsg, 0.0)
            dws_ref[g] += jnp.where(tril, _dot(dsg_g, vg, 1, 1), 0.0)
        dvg = _group_select([_dot(w[g], dsg, 0, 0) for g in range(4)])
        db_ref[...] += dsg
        n = gv * rr
        part = jnp.sum(dvg * n, axis=0, keepdims=True)
        dvn_ref[...] += jnp.broadcast_to(part[:, :LANES] + part[:, LANES:], (8, LANES))
        u = dvg * vn_ref[...]
        tu_ = gv * u
        mean = jnp.concatenate([_head_mean(tu_[:, k * LANES:(k + 1) * LANES], bd) for k in range(2)], axis=1)
        dgv = rr * u - gv * (rr * rr * rr) * mean
        duv_ref[:, :SGU_W] = (dug * _gelu_grad(us, tu)).astype(BF16)
        duv_ref[:, SGU_W:] = (dgv * _gelu_grad(vs, tv)).astype(BF16)

        @pl.when(i == nb - 1)
        def _():
            acc = dvn_ref[...]
            dvn_ref[...] = acc + pltpu.roll(acc, HEAD_DIM, 1)
            for k in range(2):
                db_ref[:, k * LANES:(k + 1) * LANES] = _head_mean(db_ref[:, k * LANES:(k + 1) * LANES], bd) * float(HEAD_DIM)

    return pl.pallas_call(
        body, name=name, grid=(nb,),
        in_specs=[pl.BlockSpec((BLOCK, 2 * SGU_W), lambda i: (i, 2)), pl.BlockSpec((4, BLOCK, BLOCK), lambda i: (0, 0, 0)),
                  pl.BlockSpec((BLOCK, SGU_W), lambda i: (0, 0)), pl.BlockSpec((1, SGU_W), lambda i: (0, 0)),
                  pl.BlockSpec((BLOCK, SGU_W), lambda i: (i, 0))],
        out_specs=[pl.BlockSpec((BLOCK, 2 * SGU_W), lambda i: (i, 0)), pl.BlockSpec((4, BLOCK, BLOCK), lambda i: (0, 0, 0)),
                   pl.BlockSpec((BLOCK, SGU_W), lambda i: (0, 0)), pl.BlockSpec((8, LANES), lambda i: (0, 0))],
        out_shape=[jax.ShapeDtypeStruct((s, 2 * SGU_W), BF16), jax.ShapeDtypeStruct((4, BLOCK, BLOCK), F32),
                   jax.ShapeDtypeStruct((BLOCK, SGU_W), F32), jax.ShapeDtypeStruct((8, LANES), F32)],
        compiler_params=_params("arbitrary"),
    )(z, ws, bcol, vn, dc)


MERGE_TN = 512
MERGE_TM = 1024


def _merge_out_fwd(a, b, c, wpa, wpb, wpc, z, w_out, res, name):
    s = z.shape[0]
    tm = _tile(s, (MERGE_TM, BLOCK))
    gate0 = GATE_COL0 // MERGE_TN
    nn = D_MODEL // MERGE_TN

    def body(a_ref, b_ref, c_ref, wa_ref, wb_ref, wc_ref, g0_ref, g1_ref, g2_ref, wo_ref, res_ref, o_ref, m_ref, acc_ref):
        n = pl.program_id(1)
        r = _sigmoid(g0_ref[...]) * _dot(a_ref[...], wa_ref[...])
        r = r + _sigmoid(g1_ref[...]) * _dot(b_ref[...], wb_ref[...])
        r = r + _sigmoid(g2_ref[...]) * _dot(c_ref[...], wc_ref[...])
        merged = r.astype(BF16)
        m_ref[...] = merged

        @pl.when(n == 0)
        def _():
            acc_ref[...] = jnp.zeros_like(acc_ref)

        acc_ref[...] += jnp.dot(merged, wo_ref[...], preferred_element_type=F32)

        @pl.when(n == nn - 1)
        def _():
            o_ref[...] = acc_ref[...] + res_ref[...]

    x_spec = lambda w: pl.BlockSpec((tm, w), lambda i, n: (i, 0))
    w_spec = lambda w: pl.BlockSpec((w, MERGE_TN), lambda i, n: (0, n))
    g_spec = lambda br: pl.BlockSpec((tm, MERGE_TN), lambda i, n: (i, gate0 + 2 * br + n))
    row = pl.BlockSpec((tm, D_MODEL), lambda i, n: (i, 0))
    return pl.pallas_call(
        body, name=name, grid=(s // tm, nn),
        in_specs=[x_spec(POOL_W), x_spec(ATTN_W), x_spec(SGU_W), w_spec(POOL_W), w_spec(ATTN_W), w_spec(SGU_W),
                  g_spec(0), g_spec(1), g_spec(2), pl.BlockSpec((MERGE_TN, D_MODEL), lambda i, n: (n, 0)), row],
        out_specs=[row, pl.BlockSpec((tm, MERGE_TN), lambda i, n: (i, n))],
        out_shape=[jax.ShapeDtypeStruct((s, D_MODEL), F32), jax.ShapeDtypeStruct((s, D_MODEL), BF16)],
        scratch_shapes=[pltpu.VMEM((tm, D_MODEL), F32)], compiler_params=_params("parallel", "arbitrary"),
    )(a, b, c, wpa, wpb, wpc, z, z, z, w_out, res)


def _branch_bwd(br, xb, wp, z, dm, dz, name):
    s = z.shape[0]
    kb = xb.shape[1]
    tm = _tile(s, (MERGE_TM, BLOCK))
    gate0 = GATE_COL0 // MERGE_TN
    aliased = dz is not None

    def body(*refs):
        x_ref, w_ref, g_ref, dm_ref = refs[:4]
        dz_ref, dy_ref, dx_ref = refs[-3:]
        n = pl.program_id(1)
        y = _dot(x_ref[...], w_ref[...])
        sg = _sigmoid(g_ref[...])
        dmv = dm_ref[...]
        dy = (dmv * sg).astype(BF16)
        dy_ref[...] = dy
        dz_ref[...] = ((dmv * y) * (sg * (1.0 - sg))).astype(BF16)
        dx = _dot(dy, w_ref[...], 1, 1)

        @pl.when(n == 0)
        def _():
            dx_ref[...] = dx

        @pl.when(n > 0)
        def _():
            dx_ref[...] += dx

    in_specs = [pl.BlockSpec((tm, kb), lambda i, n: (i, 0)), pl.BlockSpec((kb, MERGE_TN), lambda i, n: (0, n)),
                pl.BlockSpec((tm, MERGE_TN), lambda i, n: (i, gate0 + 2 * br + n)),
                pl.BlockSpec((tm, MERGE_TN), lambda i, n: (i, n))]
    args = [xb, wp, z, dm]
    if aliased:
        in_specs.append(pl.BlockSpec(memory_space=pl.ANY))
        args.append(dz)
    return pl.pallas_call(
        body, name=name, grid=(s // tm, D_MODEL // MERGE_TN), in_specs=in_specs,
        out_specs=[pl.BlockSpec((tm, MERGE_TN), lambda i, n: (i, gate0 + 2 * br + n)),
                   pl.BlockSpec((tm, MERGE_TN), lambda i, n: (i, n)),
                   pl.BlockSpec((tm, kb), lambda i, n: (i, 0))],
        out_shape=[jax.ShapeDtypeStruct((s, IN_COLS), BF16), jax.ShapeDtypeStruct((s, D_MODEL), BF16),
                   jax.ShapeDtypeStruct((s, kb), F32)],
        input_output_aliases={4: 0} if aliased else {},
        compiler_params=_params("parallel", "arbitrary"),
    )(*args)


FFN_TM = 256
FFN_TC = 2816
FFN_STRIP = 256
FFN_UNROLL = 4


def _conv3(cur, prev, w_ref, b_ref):
    cat = jnp.concatenate([prev, cur], axis=0)
    x1 = pltpu.roll(cat, 1, 0)[HALO_CONV:]
    x2 = pltpu.roll(cat, 2, 0)[HALO_CONV:]
    return w_ref[0:1, :] * x2 + w_ref[1:2, :] * x1 + w_ref[2:3, :] * cur + b_ref[...], x1, x2


def _ffn_specs(tm):
    per = tm // HALO_CONV
    cur = pl.BlockSpec((tm, FFN_TC), lambda j, i: (i, j))
    prev = pl.BlockSpec((HALO_CONV, FFN_TC), lambda j, i: (jnp.maximum(i * per - 1, 0), j))
    w = pl.BlockSpec((3, FFN_TC), lambda j, i: (0, j))
    b = pl.BlockSpec((1, FFN_TC), lambda j, i: (0, j))
    return cur, prev, w, b


DOWN_TM = 256
DOWN_TK = 2816
DOWN_CHUNK = 256


def _ffn_down_fwd(upg, upv, cwg, cwv, cbg, cbv, w_down, res, name):
    s = upg.shape[0]
    d = w_down.shape[1]
    tm = _tile(s, (DOWN_TM, DOWN_CHUNK, BLOCK))
    chunk = min(DOWN_CHUNK, tm)
    per = tm // HALO_CONV
    nk = D_FF // DOWN_TK

    def body(g_ref, gp_ref, v_ref, vp_ref, wg_ref, wv_ref, bg_ref, bv_ref, w_ref, res_ref, o_ref, act_ref, acc_ref):
        first = pl.program_id(0) == 0
        kk = pl.program_id(1)

        @pl.when(kk == 0)
        def _():
            acc_ref[...] = jnp.zeros_like(acc_ref)

        for c in range(tm // chunk):
            rows = slice(c * chunk, (c + 1) * chunk)
            before = slice(c * chunk - HALO_CONV, c * chunk)
            g_p = jnp.where(first, 0.0, gp_ref[...]) if c == 0 else g_ref[before, :]
            v_p = jnp.where(first, 0.0, vp_ref[...]) if c == 0 else v_ref[before, :]
            gate = _conv3(g_ref[rows, :], g_p, wg_ref, bg_ref)[0]
            val = _conv3(v_ref[rows, :], v_p, wv_ref, bv_ref)[0]
            act = ((gate * _sigmoid(gate)) * val).astype(BF16)
            act_ref[rows, :] = act
            acc_ref[rows, :] += jnp.dot(act, w_ref[...], preferred_element_type=F32)

        @pl.when(kk == nk - 1)
        def _():
            o_ref[...] = acc_ref[...] + res_ref[...]

    cur = pl.BlockSpec((tm, DOWN_TK), lambda i, kk: (i, kk))
    prev = pl.BlockSpec((HALO_CONV, DOWN_TK), lambda i, kk: (jnp.maximum(i * per - 1, 0), kk))
    w = pl.BlockSpec((3, DOWN_TK), lambda i, kk: (0, kk))
    b = pl.BlockSpec((1, DOWN_TK), lambda i, kk: (0, kk))
    row = pl.BlockSpec((tm, d), lambda i, kk: (i, 0))
    return pl.pallas_call(
        body, name=name, grid=(s // tm, nk),
        in_specs=[cur, prev, cur, prev, w, w, b, b, pl.BlockSpec((DOWN_TK, d), lambda i, kk: (kk, 0)), row],
        out_specs=[row, cur], out_shape=[jax.ShapeDtypeStruct((s, d), F32), jax.ShapeDtypeStruct((s, D_FF), BF16)],
        scratch_shapes=[pltpu.VMEM((tm, d), F32)], compiler_params=_params("parallel", "arbitrary"),
    )(upg, upg, upv, upv, cwg, cwv, cbg, cbv, w_down, res)


def _ffn_bwd(upg, upv, cwg, cwv, cbg, cbv, dact, name):
    s = upg.shape[0]
    tm = _tile(s, (FFN_TM, BLOCK))
    per = tm // HALO_CONV
    nrow = s // tm
    cur, prev, w, b = _ffn_specs(tm)
    nxt = pl.BlockSpec((HALO_CONV, FFN_TC), lambda j, i: (jnp.minimum((i + 1) * per, nrow * per - 1), j))

    nch = tm // 8
    rows8 = lambda r: pl.ds(pl.multiple_of(r * 8, 8), 8)

    def body(g_ref, gp_ref, gn_ref, v_ref, vp_ref, vn_ref, wg_ref, wv_ref, bg_ref, bv_ref, da_ref, dan_ref,
             dg_ref, dv_ref, dwg_ref, dwv_ref, og_ref, ov_ref):
        i = pl.program_id(1)
        first = i == 0
        last = i == nrow - 1
        row = lax.broadcasted_iota(jnp.int32, (8, FFN_STRIP), 0)

        keep_down = {k: row >= k for k in (1, 2)}
        keep_up = {k: row < 8 - k for k in (1, 2)}

        def down(cur, prev, k):
            return jnp.where(keep_down[k], pltpu.roll(cur, k, 0), pltpu.roll(prev, k, 0))

        def up(cur, nxt, k):
            return jnp.where(keep_up[k], pltpu.roll(cur, 8 - k, 0), pltpu.roll(nxt, 8 - k, 0))

        @pl.when(first)
        def _():
            dwg_ref[...] = jnp.zeros_like(dwg_ref)
            dwv_ref[...] = jnp.zeros_like(dwv_ref)

        for c in range(FFN_TC // FFN_STRIP):
            cols = slice(c * FFN_STRIP, (c + 1) * FFN_STRIP)
            wg = [functools.partial(lambda k: wg_ref[k:k + 1, cols], k) for k in range(3)]
            wv = [functools.partial(lambda k: wv_ref[k:k + 1, cols], k) for k in range(3)]

            def conv_grads(g_cur, g_prev, v_cur, v_prev, da):
                gate = wg[0]() * down(g_cur, g_prev, 2) + wg[1]() * down(g_cur, g_prev, 1) + wg[2]() * g_cur + bg_ref[:, cols]
                val = wv[0]() * down(v_cur, v_prev, 2) + wv[1]() * down(v_cur, v_prev, 1) + wv[2]() * v_cur + bv_ref[:, cols]
                sg = _sigmoid(gate)
                return (da * val) * (sg * (1.0 + gate * (1.0 - sg))), da * (gate * sg)

            def passes(q, carry):
                for u in range(FFN_UNROLL):
                    carry = one_pass(q * FFN_UNROLL + u, carry)
                return carry

            def one_pass(r, carry, tile_end=False):
                dg_cur, dv_cur, acc = carry
                g_r, v_r = g_ref[rows8(r), cols], v_ref[rows8(r), cols]
                if tile_end:
                    g_n, v_n, da_n = gn_ref[:, cols], vn_ref[:, cols], jnp.where(last, 0.0, dan_ref[:, cols])
                else:
                    g_n, v_n, da_n = g_ref[rows8(r + 1), cols], v_ref[rows8(r + 1), cols], da_ref[rows8(r + 1), cols]
                dg_n, dv_n = conv_grads(g_n, g_r, v_n, v_r, da_n)
                new_acc = []
                for o_ref, w, d_cur, d_n, x0, a in ((og_ref, wg, dg_cur, dg_n, g_r, acc[:4]), (ov_ref, wv, dv_cur, dv_n, v_r, acc[4:])):
                    d1, d2 = up(d_cur, d_n, 1), up(d_cur, d_n, 2)
                    o_ref[rows8(r), cols] = w[2]() * d_cur + w[1]() * d1 + w[0]() * d2
                    new_acc += [a[0] + d2 * x0, a[1] + d1 * x0, a[2] + d_cur * x0, a[3] + d_cur]
                return dg_n, dv_n, tuple(new_acc)

            g_p = jnp.where(first, 0.0, gp_ref[:, cols])
            v_p = jnp.where(first, 0.0, vp_ref[:, cols])
            dg0, dv0 = conv_grads(g_ref[0:8, cols], g_p, v_ref[0:8, cols], v_p, da_ref[0:8, cols])
            zero = jnp.zeros((8, FFN_STRIP), F32)
            carry = lax.fori_loop(0, nch // FFN_UNROLL - 1, passes, (dg0, dv0, (zero,) * 8))
            for r in range(nch - FFN_UNROLL, nch):
                carry = one_pass(r, carry, tile_end=r == nch - 1)
            for ref, a in ((dwg_ref, carry[2][:4]), (dwv_ref, carry[2][4:])):
                for k in range(4):
                    ref[k:k + 1, cols] += jnp.sum(a[k], axis=0, keepdims=True)
        dg_ref[...] = og_ref[...].astype(BF16)
        dv_ref[...] = ov_ref[...].astype(BF16)

    acc = pl.BlockSpec((8, FFN_TC), lambda j, i: (0, j))
    full = jax.ShapeDtypeStruct((s, D_FF), BF16)
    accs = jax.ShapeDtypeStruct((8, D_FF), F32)
    return pl.pallas_call(
        body, name=name, grid=(D_FF // FFN_TC, nrow), in_specs=[cur, prev, nxt, cur, prev, nxt, w, w, b, b, cur, nxt],
        out_specs=[cur, cur, acc, acc], out_shape=[full, full, accs, accs],
        scratch_shapes=[pltpu.VMEM((tm, FFN_TC), F32), pltpu.VMEM((tm, FFN_TC), F32)],
        compiler_params=_params("parallel", "arbitrary"),
    )(upg, upg, upg, upv, upv, upv, cwg, cwv, cbg, cbv, dact, dact)


def _mesh_place():
    return lax.axis_index("x"), lax.axis_index("y"), lax.axis_index("c")


def _all_gather(shards, name):
    na = len(shards)

    def body(*refs):
        x_refs, out_refs = refs[:na], refs[na:2 * na]
        send_sems, recv_sems, local_sems = refs[2 * na:]
        x, y, cc = _mesh_place()
        me, sibling = (x, y, cc), (x, y, 1 - cc)
        chips = [(1 - x, y), (x, 1 - y), (1 - x, 1 - y)]

        def copy(k, a, block, to, from_input=False):
            slot = out_refs[a].at[4 * block[0] + 2 * block[1] + block[2]]
            return pltpu.make_async_remote_copy(
                src_ref=x_refs[a] if from_input else slot, dst_ref=slot, send_sem=send_sems.at[k * na + a],
                recv_sem=recv_sems.at[k * na + a], device_id=to, device_id_type=pl.DeviceIdType.MESH)

        mine = [pltpu.make_async_copy(x_refs[a], out_refs[a].at[4 * x + 2 * y + cc], local_sems.at[a]) for a in range(na)]
        for cp in mine:
            cp.start()
        first = [copy(0, a, me, sibling, True) for a in range(na)]
        first += [copy(1 + j, a, me, (*chip, cc), True) for j, chip in enumerate(chips) for a in range(na)]
        for cp in first:
            cp.start()
        passed = []
        for j, chip in enumerate(chips):
            for a in range(na):
                copy(1 + j, a, (*chip, cc), me).wait_recv()
                passed.append(copy(4 + j, a, (*chip, cc), sibling))
                passed[-1].start()
        for a in range(na):
            copy(0, a, sibling, me).wait_recv()
        for j, chip in enumerate(chips):
            for a in range(na):
                copy(4 + j, a, (*chip, 1 - cc), me).wait_recv()
        for cp in first + passed:
            cp.wait_send()
        for cp in mine:
            cp.wait()

    hbm = pl.BlockSpec(memory_space=pl.ANY)
    return pl.pallas_call(
        body, name=name, out_shape=[jax.ShapeDtypeStruct((N_DEV,) + t.shape, t.dtype) for t in shards],
        in_specs=[hbm] * na, out_specs=[hbm] * na,
        scratch_shapes=[pltpu.SemaphoreType.DMA((7 * na,)), pltpu.SemaphoreType.DMA((7 * na,)), pltpu.SemaphoreType.DMA((na,))],
    )(*shards)


def _peer(k):
    x, y, cc = _mesh_place()
    px = 1 - x if (k >> 2) & 1 else x
    py = 1 - y if (k >> 1) & 1 else y
    pc = 1 - cc if k & 1 else cc
    return (px, py, pc), 4 * px + 2 * py + pc


def _push_copy(src_ref, land_ref, k, a, na, send_sems, recv_sems, indexed, landed):
    x, y, cc = _mesh_place()
    place, peer = _peer(k)
    sem = (k - 1) * na + a
    return pltpu.make_async_remote_copy(
        src_ref=src_ref.at[peer] if indexed else src_ref, dst_ref=land_ref.at[peer if landed else 4 * x + 2 * y + cc],
        send_sem=send_sems.at[sem], recv_sem=recv_sems.at[sem], device_id=place, device_id_type=pl.DeviceIdType.MESH)


_HBM = pl.BlockSpec(memory_space=pltpu.HBM)
_SEM = pl.BlockSpec(memory_space=pltpu.SEMAPHORE)
_EFFECT = pltpu.SideEffectType.DATAFLOW_SIDE_EFFECTING


def _push_start(srcs, indexed, name):
    na = len(srcs)
    lands = [lax.empty(t.shape if indexed else (N_DEV,) + t.shape, t.dtype) for t in srcs]

    def body(*refs):
        src_refs, land_refs = refs[:na], refs[na:2 * na]
        send_sems, recv_sems = refs[2 * na], refs[2 * na + 1]
        token = refs[-1]
        for k in range(1, N_DEV):
            for a in range(na):
                _push_copy(src_refs[a], land_refs[a], k, a, na, send_sems, recv_sems, indexed, False).start()
        token[...] = jnp.zeros_like(token)

    sems = pltpu.SemaphoreType.DMA((7 * na,))
    out = pl.pallas_call(
        body, name=name,
        out_shape=(sems, sems, *[pltpu.HBM(t.shape, t.dtype) for t in srcs], *[pltpu.HBM(t.shape, t.dtype) for t in lands],
                   jax.ShapeDtypeStruct((8, LANES), F32)),
        in_specs=[_HBM] * (2 * na), out_specs=(_SEM, _SEM, *[_HBM] * (2 * na), pl.BlockSpec(memory_space=pltpu.VMEM)),
        input_output_aliases={i: 2 + i for i in range(2 * na)},
        compiler_params=pltpu.CompilerParams(has_side_effects=_EFFECT),
    )(*[pltpu.with_memory_space_constraint(t, pltpu.HBM) for t in srcs + lands])
    return out[0], out[1], list(out[2:2 + na]), list(out[2 + na:2 + 2 * na]), out[-1]


def _push_wait(started, indexed, after, name):
    send_sems, recv_sems, srcs, lands, _ = started
    na = len(srcs)

    def body(*refs):
        src_refs, land_refs = refs[:na], refs[na:2 * na]
        send_sems, recv_sems = refs[2 * na], refs[2 * na + 1]
        for k in range(1, N_DEV):
            for a in range(na):
                copy = _push_copy(src_refs[a], land_refs[a], k, a, na, send_sems, recv_sems, indexed, True)
                copy.wait_send()
                copy.wait_recv()

    out = pl.pallas_call(
        body, name=name, out_shape=[pltpu.HBM(t.shape, t.dtype) for t in srcs + lands],
        in_specs=[_HBM] * (2 * na) + [_SEM, _SEM, pl.BlockSpec(memory_space=pl.ANY)], out_specs=[_HBM] * (2 * na),
        input_output_aliases={i: i for i in range(2 * na)},
        compiler_params=pltpu.CompilerParams(has_side_effects=_EFFECT),
    )(*srcs, *lands, send_sems, recv_sems, after)
    x, y, cc = _mesh_place()
    me = 4 * x + 2 * y + cc
    return [lax.dynamic_update_index_in_dim(
        land, lax.dynamic_index_in_dim(src, me, 0, keepdims=False) if indexed else src, me, 0)
        for src, land in zip(out[:na], out[na:])]


def _adamw_sum(parts, w, m, v, name):
    _, r, c = parts.shape
    tr = _tile(r, (256, 128, 64, 32, 16, 8))

    def body(p_ref, w_ref, m_ref, v_ref, g_ref, d_ref, nm_ref, nv_ref):
        _adam_store(_sum_parts(p_ref), w_ref, m_ref, v_ref, g_ref, d_ref, nm_ref, nv_ref)

    row = pl.BlockSpec((tr, c), lambda i: (i, 0))
    shp = jax.ShapeDtypeStruct((r, c), F32)
    return pl.pallas_call(
        body, name=name, grid=(r // tr,), in_specs=[pl.BlockSpec((N_DEV, tr, c), lambda i: (0, i, 0)), row, row, row],
        out_specs=[row, row, row, row], out_shape=[shp, shp, shp, shp], compiler_params=_params("parallel"),
    )(parts, w, m, v)


def _sum_parts(p_ref):
    g = p_ref[0].astype(F32)
    for k in range(1, N_DEV):
        g = g + p_ref[k].astype(F32)
    return g


def _adam_store(g, w_ref, m_ref, v_ref, g_ref, d_ref, nm_ref, nv_ref):
    nm = ADAM_B1 * m_ref[...] + (1.0 - ADAM_B1) * g
    nv = ADAM_B2 * v_ref[...] + (1.0 - ADAM_B2) * (g * g)
    m_hat = nm / (1.0 - ADAM_B1 ** ADAM_STEP)
    v_hat = nv / (1.0 - ADAM_B2 ** ADAM_STEP)
    g_ref[...] = g
    nm_ref[...] = nm
    nv_ref[...] = nv
    d_ref[...] = -ADAM_LR * (m_hat / (jnp.sqrt(v_hat) + ADAM_EPS) + ADAM_WD * w_ref[...])


def _adamw_weight(parts, w, m, v, name):
    _, r, c = w.shape
    tr = _tile(r, (256, 128, 176))
    nr = r // tr

    def body(p0_ref, p1_ref, w_ref, m_ref, v_ref, g_ref, d_ref, nm_ref, nv_ref):
        g = jnp.where(pl.program_id(0) == 0, _sum_parts(p0_ref), _sum_parts(p1_ref))
        _adam_store(g, w_ref, m_ref, v_ref, g_ref, d_ref, nm_ref, nv_ref)

    part = lambda layer: pl.BlockSpec(
        (N_DEV, tr, c), lambda l, i: (0, jnp.where(l == layer, i, (nr - 1) * (1 - layer)), 0))
    row = pl.BlockSpec((None, tr, c), lambda l, i: (l, i, 0))
    shp = jax.ShapeDtypeStruct(w.shape, F32)
    return pl.pallas_call(
        body, name=name, grid=(DEPTH, nr), in_specs=[part(0), part(1), row, row, row],
        out_specs=[row, row, row, row], out_shape=[shp, shp, shp, shp], compiler_params=_params("arbitrary", "arbitrary"),
    )(parts[0], parts[1], w, m, v)


def _full_to_slots(name, t):
    k, n = t.shape
    if name in ROW_SHARDED:
        return t.reshape(N_DEV, k // N_DEV, n)
    return t.reshape(k, N_DEV, n // N_DEV).transpose(1, 0, 2)


def _slots_to_full(name, t):
    _, r, c = t.shape
    if name in ROW_SHARDED:
        return t.reshape(N_DEV * r, c)
    return t.transpose(1, 0, 2).reshape(r, N_DEV * c)


def _small_sizes(shapes, names):
    return [(n, shapes[n], -(-int(math.prod(shapes[n])) // (8 * LANES)) * 8) for n in names]


def _pack_small(tree, shapes, names, row_tile):
    rows = []
    for n, shp, nrow in _small_sizes(shapes, names):
        flat = tree[n].reshape(-1)
        rows.append(jnp.pad(flat, (0, nrow * LANES - flat.shape[0])).reshape(nrow, LANES))
    total = sum(r.shape[0] for r in rows)
    if total % row_tile:
        rows.append(jnp.zeros((-total % row_tile, LANES), F32))
    return jnp.concatenate(rows, axis=0)


def _unpack_small(buf, shapes, names):
    out, r0 = {}, 0
    for n, shp, nrow in _small_sizes(shapes, names):
        out[n] = buf[r0:r0 + nrow].reshape(-1)[:int(math.prod(shp))].reshape(shp)
        r0 += nrow
    return out


def _block_diag(w):
    g = w.shape[0]
    eye = jnp.eye(g, dtype=w.dtype)
    return (eye[:, None, :, None] * w[:, :, None, :]).reshape(g * HEAD_DIM, g * HEAD_DIM)


def kernel(x, positions, norm1, w_in, q_norm, k_norm, sinks, w_pool, pool_scale, sgu_v_norm, w_s, b_s, w_proj_a, w_proj_b, w_proj_c, w_out, norm2, w_up, conv_w, conv_b, w_down, loss_target, m_norm1, m_w_in, m_q_norm, m_k_norm, m_sinks, m_w_pool, m_pool_scale, m_sgu_v_norm, m_w_s, m_b_s, m_w_proj_a, m_w_proj_b, m_w_proj_c, m_w_out, m_norm2, m_w_up, m_conv_w, m_conv_b, m_w_down, v_norm1, v_w_in, v_q_norm, v_k_norm, v_sinks, v_w_pool, v_pool_scale, v_sgu_v_norm, v_w_s, v_b_s, v_w_proj_a, v_w_proj_b, v_w_proj_c, v_w_out, v_norm2, v_w_up, v_conv_w, v_conv_b, v_w_down):
    names = ("norm1", "w_in", "q_norm", "k_norm", "sinks", "w_pool", "pool_scale", "sgu_v_norm", "w_s", "b_s", "w_proj_a",
             "w_proj_b", "w_proj_c", "w_out", "norm2", "w_up", "conv_w", "conv_b", "w_down")
    wts = dict(zip(names, (norm1, w_in, q_norm, k_norm, sinks, w_pool, pool_scale, sgu_v_norm, w_s, b_s, w_proj_a, w_proj_b,
                           w_proj_c, w_out, norm2, w_up, conv_w, conv_b, w_down)))
    mom = dict(zip(names, (m_norm1, m_w_in, m_q_norm, m_k_norm, m_sinks, m_w_pool, m_pool_scale, m_sgu_v_norm, m_w_s, m_b_s,
                           m_w_proj_a, m_w_proj_b, m_w_proj_c, m_w_out, m_norm2, m_w_up, m_conv_w, m_conv_b, m_w_down)))
    var = dict(zip(names, (v_norm1, v_w_in, v_q_norm, v_k_norm, v_sinks, v_w_pool, v_pool_scale, v_sgu_v_norm, v_w_s, v_b_s,
                           v_w_proj_a, v_w_proj_b, v_w_proj_c, v_w_out, v_norm2, v_w_up, v_conv_w, v_conv_b, v_w_down)))
    xs = x[0]
    target = loss_target[0]
    s = xs.shape[0]

    inv_freq = ROPE_THETA ** (-jnp.arange(0, HEAD_DIM, 2, dtype=F32) / HEAD_DIM)
    ang = positions[0].astype(F32)[:, None] * inv_freq
    cosf = jnp.tile(jnp.cos(ang), (1, 4))
    sinf = jnp.tile(jnp.concatenate([-jnp.sin(ang), jnp.sin(ang)], axis=1), (1, 2))

    local = [{n: wts[n][l] if n == "conv_w" else wts[n][l].astype(BF16) for n in SHARDED} for l in range(DEPTH)]
    later = SHARDED[1:]
    full = [{"w_in": _slots_to_full("w_in", _all_gather([local[0]["w_in"]], "gather_w_in_0")[0])}, None]
    gather0 = _push_start([local[0][n] for n in later], False, "gather_rest_0_start")
    norm1_first = norm1[0] + gather0[4][0, 0]

    def layer_consts(l):
        return dict(
            bdw=_block_diag(w_pool[l]).astype(BF16), qn=jnp.tile(q_norm[l], 2).reshape(1, LANES),
            kn=jnp.tile(k_norm[l], 2).reshape(1, LANES), vn=jnp.tile(sgu_v_norm[l], 4).reshape(1, SGU_W),
            bcol=jnp.repeat(b_s[l].T, HEAD_DIM, axis=1),
            cbg=conv_b[l][:D_FF].reshape(1, D_FF), cbv=conv_b[l][D_FF:].reshape(1, D_FF))

    gate_cols, val_cols = (0, D_FF), (D_FF, D_FF)

    saved = []
    cur = xs
    for l in range(DEPTH):
        if l == 1:
            landed = _push_wait(gather1, False, cur, "gather_weights_1_wait")
            full[1] = {n: _slots_to_full(n, t) for n, t in zip(SHARDED, landed)}
        fw, k = full[l], layer_consts(l)
        h1 = _rms_fwd(cur, norm1_first if l == 0 else norm1[l], f"rms1_fwd_{l}")
        z = _mm(h1, fw["w_in"], name=f"in_proj_{l}")
        a = _pool_fwd(z, k["bdw"], pool_scale[l], f"pool_fwd_{l}")
        b = _attn_fwd(z, cosf, sinf, k["qn"], k["kn"], sinks[l], f"attn_fwd_{l}")
        c = _sgu_fwd(z, w_s[l], k["bcol"], k["vn"], f"sgu_fwd_{l}")
        w_proj_a_l = fw.get("w_proj_a")
        if l == 0:
            landed = _push_wait(gather0, False, c, "gather_rest_0_wait")
            fw.update({n: _slots_to_full(n, t) for n, t in zip(later, landed)})
            gather1 = _push_start([local[1][n] for n in SHARDED], False, "gather_weights_1_start")
            w_proj_a_l = fw["w_proj_a"] + gather1[4][0, 0].astype(BF16)
        x1, merged = _merge_out_fwd(a, b, c, w_proj_a_l, fw["w_proj_b"], fw["w_proj_c"], z, fw["w_out"], cur, f"merge_out_fwd_{l}")
        h2 = _rms_fwd(x1, norm2[l], f"rms2_fwd_{l}")
        upg = _mm(h2, fw["w_up"], b_n=gate_cols, name=f"up_gate_{l}")
        upv = _mm(h2, fw["w_up"], b_n=val_cols, name=f"up_val_{l}")
        k["cwg"], k["cwv"] = fw["conv_w"][:, :D_FF], fw["conv_w"][:, D_FF:]
        x2, act = _ffn_down_fwd(upg, upv, k["cwg"], k["cwv"], k["cbg"], k["cbv"], fw["w_down"], x1, f"ffn_down_fwd_{l}")
        saved.append(dict(x0=cur, h1=h1, z=z, a=a, b=b, c=c, merged=merged, x1=x1, h2=h2, upg=upg, upv=upv, act=act))
        cur = x2

    dcur, loss_tile = _loss_head(cur, target)
    loss = lax.psum(loss_tile[0, 0], ("x", "y", "c"))

    gsmall = [None] * DEPTH
    small_shapes = {n: wts[n].shape for n in SMALL}

    def slots_of(grads):
        return [_full_to_slots(n, t) for n, t in grads.items()]

    for l in reversed(range(DEPTH)):
        fw, k, sv = full[l], layer_consts(l), saved[l]
        k["cwg"], k["cwv"] = fw["conv_w"][:, :D_FF], fw["conv_w"][:, D_FF:]
        staged = l == 0
        wgrad = functools.partial(_mm, ta=True, out_dtype=BF16)
        w_down_l = fw["w_down"] + exchange1[4][0, 0].astype(BF16) if staged else fw["w_down"]
        dact = _mm(dcur, w_down_l, tb=True, name=f"down_proj_bwd_{l}")
        g_down = wgrad(sv["act"], dcur, name=f"down_proj_wgrad_{l}")
        dg0, dv0, dcg, dcv = _ffn_bwd(sv["upg"], sv["upv"], k["cwg"], k["cwv"], k["cbg"], k["cbv"], dact, f"ffn_bwd_{l}")
        dh2 = _mm(dg0, fw["w_up"], tb=True, b_k=gate_cols, name=f"up_gate_bwd_{l}")
        dh2 = _mm(dv0, fw["w_up"], tb=True, b_k=val_cols, add=dh2, name=f"up_val_bwd_{l}")
        g_up = wgrad(sv["h2"], dg0, out_cols=(0, 2 * D_FF), name=f"up_gate_wgrad_{l}")
        g_up = wgrad(sv["h2"], dv0, out_cols=(D_FF, 2 * D_FF), out_into=g_up, name=f"up_val_wgrad_{l}")
        g_ffn = dict(w_up=g_up, w_down=g_down, conv_w=jnp.concatenate([dcg[0:3], dcv[0:3]], axis=1))
        norm2_l = norm2[l]
        if staged:
            parts1 = dict(zip(SHARDED, _push_wait(exchange1, True, g_up, "exchange_grads_1_wait")))
            exchange_ffn = _push_start(slots_of(g_ffn), True, "exchange_ffn_0_start")
            norm2_l = norm2_l + exchange_ffn[4][0, 0]
        dx1, g_norm2 = _rms_bwd(sv["x1"], norm2_l, dh2, dcur, f"rms2_bwd_{l}")
        dmerged = _mm(dx1, fw["w_out"], tb=True, name=f"out_proj_bwd_{l}")
        g_out = wgrad(sv["merged"], dx1, name=f"out_proj_wgrad_{l}")
        dz, dya, da = _branch_bwd(0, sv["a"], fw["w_proj_a"], sv["z"], dmerged, None, f"branch_a_bwd_{l}")
        dz, dyb, db = _branch_bwd(1, sv["b"], fw["w_proj_b"], sv["z"], dmerged, dz, f"branch_b_bwd_{l}")
        dz, dyc, dc = _branch_bwd(2, sv["c"], fw["w_proj_c"], sv["z"], dmerged, dz, f"branch_c_bwd_{l}")
        g_mix = dict(w_proj_a=wgrad(sv["a"], dya, name=f"proj_a_wgrad_{l}"), w_proj_b=wgrad(sv["b"], dyb, name=f"proj_b_wgrad_{l}"),
                     w_proj_c=wgrad(sv["c"], dyc, name=f"proj_c_wgrad_{l}"), w_out=g_out)
        pool_scale_l = pool_scale[l]
        if staged:
            exchange_mix = _push_start(slots_of(g_mix), True, "exchange_mixer_0_start")
            pool_scale_l = pool_scale_l + exchange_mix[4][0, 0]
        dxp, g_bdw, g_pscale = _pool_bwd(sv["z"], da, k["bdw"], pool_scale_l, f"pool_bwd_{l}")
        dq, dkc, dkp, dvc, dvp, g_qn, g_sink = _attn_bwd(sv["z"], cosf, sinf, k["qn"], k["kn"], sinks[l], db, f"attn_bwd_{l}")
        duv, g_ws, g_bacc, g_vn = _sgu_bwd(sv["z"], w_s[l], k["bcol"], k["vn"], dc, f"sgu_bwd_{l}")
        gsmall[l] = dict(
            q_norm=g_qn[0, :HEAD_DIM], sinks=g_sink[:, 0],
            w_pool=jnp.stack([g_bdw[g * HEAD_DIM:(g + 1) * HEAD_DIM, g * HEAD_DIM:(g + 1) * HEAD_DIM] for g in range(4)]),
            pool_scale=g_pscale[0], sgu_v_norm=g_vn[0, :HEAD_DIM], w_s=g_ws, b_s=g_bacc[:, ::HEAD_DIM].T,
            norm2=g_norm2[0], conv_b=jnp.concatenate([dcg[3], dcv[3]]))
        kn_l = k["kn"]
        if staged:
            early = _pack_small({n: jnp.stack([gsmall[i][n] for i in range(DEPTH)]) for n in SMALL_EARLY}, small_shapes,
                                SMALL_EARLY, SMALL_ROW_TILE)
            gather_early = _push_start([early], False, "gather_small_grads_start")
            kn_l = kn_l + gather_early[4][0, 0]
        dz, g_kn = _kv_post(sv["z"], cosf, sinf, kn_l, dkc, dkp, dvc, dvp, dxp, dq, duv, dz, f"kv_post_{l}")
        g_in = dict(w_in=wgrad(sv["h1"], dz, name=f"in_proj_wgrad_{l}"))
        norm1_l = norm1[l]
        if staged:
            exchange_in = _push_start(slots_of(g_in), True, "exchange_w_in_0_start")
            norm1_l = norm1_l + exchange_in[4][0, 0]
        dh1 = _mm(dz, fw["w_in"], tb=True, name=f"in_proj_bwd_{l}")
        dcur, g_norm1 = _rms_bwd(sv["x0"], norm1_l, dh1, dx1, f"rms1_bwd_{l}")
        if not staged:
            exchange1 = _push_start(slots_of({n: {**g_in, **g_mix, **g_ffn}[n] for n in SHARDED}), True, "exchange_grads_1_start")
        gsmall[l].update(norm1=g_norm1[0], k_norm=g_kn[0, :HEAD_DIM])
    grad_x = dcur[None]

    def update_small(gathered, names, row_tile, name):
        pack = lambda tree: _pack_small(tree, small_shapes, names, row_tile)
        return [_unpack_small(t, small_shapes, names) for t in _adamw_sum(gathered, pack(wts), pack(mom), pack(var), name)]

    late = _pack_small({n: jnp.stack([gsmall[i][n] for i in range(DEPTH)]) for n in SMALL_LATE}, small_shapes, SMALL_LATE, 8)
    gather_late = _push_start([late], False, "gather_late_small_grads_start")

    parts0 = dict(zip(g_ffn, _push_wait(exchange_ffn, True, gather_late[4], "exchange_ffn_0_wait")))
    parts0.update(zip(g_mix, _push_wait(exchange_mix, True, gather_late[4], "exchange_mixer_0_wait")))
    update = lambda n: _adamw_weight([parts0[n], parts1[n]], wts[n], mom[n], var[n], f"adamw_{n}")
    big = {n: update(n) for n in SHARDED[1:]}
    parts0.update(zip(g_in, _push_wait(exchange_in, True, big["w_up"][0], "exchange_w_in_0_wait")))
    big["w_in"] = update("w_in")

    late_all = _push_wait(gather_late, False, big["w_in"][0], "gather_late_small_grads_wait")[0]
    small = update_small(late_all, SMALL_LATE, 8, "adamw_replicated_late")
    early_all = _push_wait(gather_early, False, big["w_in"][0], "gather_small_grads_wait")[0]
    for kind, tree in enumerate(update_small(early_all, SMALL_EARLY, SMALL_ROW_TILE, "adamw_replicated")):
        small[kind].update(tree)

    outs = [loss, grad_x]
    for kind in range(4):
        outs += [small[kind][n] if n in SMALL else big[n][kind] for n in names]
    return tuple(outs)
```

```python
import functools
import math

import jax
import jax.numpy as jnp
import numpy as np
from jax import lax
from jax.experimental import pallas as pl
from jax.experimental.pallas import tpu as pltpu

F32 = jnp.float32
BF16 = jnp.bfloat16

D_MODEL = 1024
DEPTH = 2
HEAD_DIM = 64
N_Q_HEADS = 8
Q_PER_KV = 4
BLOCK = 128
POOL_W = 256
ATTN_W = 512
KV_W = 128
SGU_W = 256
IN_COLS = 4608
GATE_COL0 = 1536
D_FF = 2816
EPS = 1e-6
ROPE_THETA = 10000.0
N_DEV = 8
LANES = 128
HALO_POOL = 16
HALO_CONV = 8

ADAM_LR = 0.001
ADAM_B1 = 0.9
ADAM_B2 = 0.999
ADAM_EPS = 1e-08
ADAM_WD = 0.01
ADAM_STEP = 10

VMEM_LIMIT = 48 * 1024 * 1024
MM_VMEM_BUDGET = 40 * 1024 * 1024

SHARDED = ("w_in", "w_proj_a", "w_proj_b", "w_proj_c", "w_out", "w_up", "w_down", "conv_w")
ROW_SHARDED = ("w_out", "w_down")
SMALL_ROW_TILE = 256
SMALL = ("norm1", "q_norm", "k_norm", "sinks", "w_pool", "pool_scale", "sgu_v_norm", "w_s", "b_s", "norm2", "conv_b")
SMALL_LATE = ("norm1", "k_norm")
SMALL_EARLY = tuple(n for n in SMALL if n not in SMALL_LATE)

_GELU_C = math.sqrt(2.0 / math.pi)
_GELU_A = 0.044715


def _params(*sem):
    return pltpu.CompilerParams(dimension_semantics=sem, vmem_limit_bytes=VMEM_LIMIT)


def _tile(n, prefs):
    for t in prefs:
        if t <= n and n % t == 0:
            return t
    return n


def _head_mean_matrix():
    r = lax.broadcasted_iota(jnp.int32, (LANES, LANES), 0)
    c = lax.broadcasted_iota(jnp.int32, (LANES, LANES), 1)
    return jnp.where((r >= HEAD_DIM) == (c >= HEAD_DIM), 1.0 / HEAD_DIM, 0.0).astype(BF16)


def _head_mean(v, bd):
    hi = v.astype(BF16)
    rest = v - hi.astype(F32)
    mid = rest.astype(BF16)
    lo = (rest - mid.astype(F32)).astype(BF16)
    mm = lambda p: jnp.dot(p, bd, preferred_element_type=F32)
    return mm(hi) + (mm(mid) + mm(lo))


def _rot_half(t):
    lane = lax.broadcasted_iota(jnp.int32, t.shape, 1)
    return jnp.where((lane & 32) == 0, pltpu.roll(t, LANES - 32, 1), pltpu.roll(t, 32, 1))


def _norm_rope(t, gn, cosf, sinf, bd):
    r = lax.rsqrt(_head_mean(t * t, bd) + EPS)
    n = t * r
    y = n * gn
    return y * cosf + _rot_half(y) * sinf, n, r


def _norm_rope_bwd(d, t, n, r, gn, cosf, sinf, bd):
    dy = d * cosf + _rot_half(d * sinf)
    dgn = jnp.sum(dy * n, axis=0, keepdims=True)
    u = dy * gn
    dt = r * u - t * (r * r * r) * _head_mean(t * u, bd)
    return dt, dgn


def _gelu(x):
    t = jnp.tanh(_GELU_C * (x + _GELU_A * (x * x * x)))
    return 0.5 * x * (1.0 + t), t


def _gelu_grad(x, t):
    return 0.5 * (1.0 + t) + 0.5 * x * (1.0 - t * t) * (_GELU_C * (1.0 + 3.0 * _GELU_A * x * x))


def _sigmoid(x):
    return jax.nn.sigmoid(x)


def _dot(a, b, ca=1, cb=0):
    return lax.dot_general(a.astype(BF16), b.astype(BF16), (((ca,), (cb,)), ((), ())), preferred_element_type=F32)


def _mm(a, b, *, ta=False, tb=False, add=None, out_dtype=F32, name, b_n=None, b_k=None, out_cols=None, out_into=None):
    m = a.shape[1] if ta else a.shape[0]
    k = a.shape[0] if ta else a.shape[1]
    n = b_n[1] if b_n else (b.shape[0] if tb else b.shape[1])
    tn = _tile(n, (1024, 1152, 1408, 512, 256, 128))
    has_add = add is not None
    fits = []
    for tm in (2048, 1024, 1408, 512, 256, 128):
        for tk in (k, 4608, 2816, 2048, 1408, 1152, 1024, 512, 256, 128):
            if tm <= m and m % tm == 0 and tk <= k and k % tk == 0:
                need = (2 * (tm * tk * a.dtype.itemsize + tk * tn * b.dtype.itemsize) + 2 * tm * tn * jnp.dtype(out_dtype).itemsize
                        + 2 * tm * tn * 4 * has_add + tm * tn * 4 * (tk < k))
                if need <= MM_VMEM_BUDGET:
                    fits.append((k // tk, -tm, tm, tk))
    if fits:
        _, _, tm, tk = min(fits)
    else:
        tm, tk = _tile(m, (256, 128)), _tile(k, (512, 256, 128))
    nk = k // tk
    n0 = b_n[0] // tn if b_n else 0
    k0 = b_k[0] // tk if b_k else 0
    o0, n_out = (out_cols[0] // tn, out_cols[1]) if out_cols else (0, n)
    n_in = 2 + has_add + (out_into is not None)

    def body(*refs):
        a_ref, b_ref = refs[0], refs[1]
        add_ref = refs[2] if has_add else None
        o_ref = refs[n_in]
        def finish(r):
            if has_add:
                r = r + add_ref[...]
            o_ref[...] = r.astype(out_dtype)

        if nk == 1:
            finish(_dot(a_ref[...], b_ref[...], 0 if ta else 1, 1 if tb else 0))
        else:
            acc_ref = refs[-1]
            kk = pl.program_id(2)

            @pl.when(kk == 0)
            def _():
                acc_ref[...] = jnp.zeros_like(acc_ref)

            acc_ref[...] += _dot(a_ref[...], b_ref[...], 0 if ta else 1, 1 if tb else 0)

            @pl.when(kk == nk - 1)
            def _():
                finish(acc_ref[...])

    a_spec = pl.BlockSpec((tk, tm), lambda i, j, kk: (kk, i)) if ta else pl.BlockSpec((tm, tk), lambda i, j, kk: (i, kk))
    if tb:
        b_spec = pl.BlockSpec((tn, tk), lambda i, j, kk: (j + n0, kk + k0))
    else:
        b_spec = pl.BlockSpec((tk, tn), lambda i, j, kk: (kk + k0, j + n0))
    in_specs = [a_spec, b_spec] + ([pl.BlockSpec((tm, tn), lambda i, j, kk: (i, j))] if has_add else [])
    args = (a, b) + ((add,) if has_add else ())
    if out_into is not None:
        in_specs.append(pl.BlockSpec(memory_space=pl.ANY))
        args += (out_into,)
    return pl.pallas_call(
        body, name=name, grid=(m // tm, n // tn, nk), in_specs=in_specs,
        out_specs=pl.BlockSpec((tm, tn), lambda i, j, kk: (i, j + o0)),
        out_shape=jax.ShapeDtypeStruct((m, n_out), out_dtype),
        scratch_shapes=[pltpu.VMEM((tm, tn), F32)] if nk > 1 else [],
        input_output_aliases={n_in - 1: 0} if out_into is not None else {},
        compiler_params=_params("parallel", "parallel", "arbitrary"),
    )(*args)


def _rms_fwd(x, g, name):
    s, d = x.shape
    tr = _tile(s, (512, 256, 128))

    def body(x_ref, g_ref, h_ref):
        xv = x_ref[...]
        r = lax.rsqrt(jnp.mean(xv * xv, axis=-1, keepdims=True) + EPS)
        h_ref[...] = ((xv * r) * g_ref[...]).astype(BF16)

    return pl.pallas_call(
        body, name=name, grid=(s // tr,),
        in_specs=[pl.BlockSpec((tr, d), lambda i: (i, 0)), pl.BlockSpec((1, d), lambda i: (0, 0))],
        out_specs=pl.BlockSpec((tr, d), lambda i: (i, 0)),
        out_shape=jax.ShapeDtypeStruct((s, d), BF16), compiler_params=_params("parallel"),
    )(x, g.reshape(1, d))


def _rms_bwd(x, g, dh, dres, name):
    s, d = x.shape
    tr = _tile(s, (512, 256, 128))

    def body(x_ref, g_ref, dh_ref, dres_ref, dx_ref, dg_ref):
        xv = x_ref[...]
        r = lax.rsqrt(jnp.mean(xv * xv, axis=-1, keepdims=True) + EPS)
        dhv = dh_ref[...]
        u = dhv * g_ref[...]
        dx_ref[...] = dres_ref[...] + (r * u - xv * (r * r * r) * jnp.mean(xv * u, axis=-1, keepdims=True))
        part = jnp.sum(dhv * (xv * r), axis=0, keepdims=True)

        @pl.when(pl.program_id(0) == 0)
        def _():
            dg_ref[...] = part

        @pl.when(pl.program_id(0) > 0)
        def _():
            dg_ref[...] += part

    row = pl.BlockSpec((tr, d), lambda i: (i, 0))
    vec = pl.BlockSpec((1, d), lambda i: (0, 0))
    return pl.pallas_call(
        body, name=name, grid=(s // tr,), in_specs=[row, vec, row, row], out_specs=[row, vec],
        out_shape=[jax.ShapeDtypeStruct((s, d), F32), jax.ShapeDtypeStruct((1, d), F32)],
        compiler_params=_params("arbitrary"),
    )(x, g.reshape(1, d), dh, dres)


def _pool_lane_select(lane, v2, v4, v8, v16):
    return jnp.where(lane < 64, v2, jnp.where(lane < 128, v4, jnp.where(lane < 192, v8, v16)))


def _pool_diff(xc, xp, row0):
    n = BLOCK + HALO_POOL
    cat = jnp.concatenate([xp, xc], axis=0)
    s2 = cat + pltpu.roll(cat, 1, 0)
    s4 = s2 + pltpu.roll(s2, 2, 0)
    s8 = s4 + pltpu.roll(s4, 4, 0)
    s16 = s8 + pltpu.roll(s8, 8, 0)
    lane = lax.broadcasted_iota(jnp.int32, (n, POOL_W), 1)
    wsum = _pool_lane_select(lane, s2, s4, s8, s16)[HALO_POOL:]
    return wsum / _pool_count(row0, BLOCK) - xc


def _pool_count(row0, rows):
    lane = lax.broadcasted_iota(jnp.int32, (rows, POOL_W), 1)
    t = lax.broadcasted_iota(jnp.int32, (rows, POOL_W), 0) + row0
    return jnp.minimum(t + 1, _pool_lane_select(lane, 2, 4, 8, 16)).astype(F32)


def _pool_fwd(z, bdw, scale, name):
    s = z.shape[0]
    nb = s // BLOCK

    def body(xc_ref, xp_ref, w_ref, sc_ref, a_ref):
        i = pl.program_id(0)
        xp = jnp.where(i > 0, xp_ref[...], 0.0)
        diff = _pool_diff(xc_ref[...], xp, i * BLOCK)
        a_ref[...] = (_dot(diff, w_ref[...]) * sc_ref[...]).astype(BF16)

    return pl.pallas_call(
        body, name=name, grid=(nb,),
        in_specs=[pl.BlockSpec((BLOCK, POOL_W), lambda i: (i, 0)),
                  pl.BlockSpec((HALO_POOL, POOL_W), lambda i: (jnp.maximum(i * (BLOCK // HALO_POOL) - 1, 0), 0)),
                  pl.BlockSpec((POOL_W, POOL_W), lambda i: (0, 0)),
                  pl.BlockSpec((1, POOL_W), lambda i: (0, 0))],
        out_specs=pl.BlockSpec((BLOCK, POOL_W), lambda i: (i, 0)),
        out_shape=jax.ShapeDtypeStruct((s, POOL_W), BF16), compiler_params=_params("parallel"),
    )(z, z, bdw, scale.reshape(1, POOL_W))


def _pool_bwd(z, da, bdw, scale, name):
    s = z.shape[0]
    nb = s // BLOCK
    per = BLOCK // HALO_POOL
    n = BLOCK + HALO_POOL

    def body(xc_ref, xp_ref, dac_ref, dan_ref, w_ref, sc_ref, dx_ref, dw_ref, dsc_ref):
        i = pl.program_id(0)
        xp = jnp.where(i > 0, xp_ref[...], 0.0)
        diff = _pool_diff(xc_ref[...], xp, i * BLOCK)
        mixed = _dot(diff, w_ref[...])
        dac = dac_ref[...]
        dan = jnp.where(i < nb - 1, dan_ref[...], 0.0)
        dmix = jnp.concatenate([dac, dan], axis=0) * sc_ref[...]
        ddiff = _dot(dmix, w_ref[...], 1, 1)
        e = ddiff / _pool_count(i * BLOCK, n)
        f2 = e + pltpu.roll(e, n - 1, 0)
        f4 = f2 + pltpu.roll(f2, n - 2, 0)
        f8 = f4 + pltpu.roll(f4, n - 4, 0)
        f16 = f8 + pltpu.roll(f8, n - 8, 0)
        lane = lax.broadcasted_iota(jnp.int32, (n, POOL_W), 1)
        back = _pool_lane_select(lane, f2, f4, f8, f16)
        dx_ref[...] = (back[:BLOCK] - ddiff[:BLOCK]).astype(BF16)
        dw = _dot(diff, dmix[:BLOCK], 0, 0)
        dsc = jnp.sum(dac * mixed, axis=0, keepdims=True)

        @pl.when(i == 0)
        def _():
            dw_ref[...] = dw
            dsc_ref[...] = dsc

        @pl.when(i > 0)
        def _():
            dw_ref[...] += dw
            dsc_ref[...] += dsc

    blk = pl.BlockSpec((BLOCK, POOL_W), lambda i: (i, 0))
    return pl.pallas_call(
        body, name=name, grid=(nb,),
        in_specs=[blk, pl.BlockSpec((HALO_POOL, POOL_W), lambda i: (jnp.maximum(i * per - 1, 0), 0)),
                  blk, pl.BlockSpec((HALO_POOL, POOL_W), lambda i: (jnp.minimum((i + 1) * per, nb * per - 1), 0)),
                  pl.BlockSpec((POOL_W, POOL_W), lambda i: (0, 0)), pl.BlockSpec((1, POOL_W), lambda i: (0, 0))],
        out_specs=[blk, pl.BlockSpec((POOL_W, POOL_W), lambda i: (0, 0)), pl.BlockSpec((1, POOL_W), lambda i: (0, 0))],
        out_shape=[jax.ShapeDtypeStruct((s, POOL_W), BF16), jax.ShapeDtypeStruct((POOL_W, POOL_W), F32),
                   jax.ShapeDtypeStruct((1, POOL_W), F32)],
        compiler_params=_params("arbitrary"),
    )(z, z, da, da, bdw, scale.reshape(1, POOL_W))


def _attn_setup(zc_ref, zp_ref, cc_ref, cp_ref, sc_ref, sp_ref, qn_ref, kn_ref, bd):
    q = []
    for j in range(ATTN_W // LANES):
        t = zc_ref[:, POOL_W + j * LANES:POOL_W + (j + 1) * LANES]
        q.append((t,) + _norm_rope(t, qn_ref[...], cc_ref[...], sc_ref[...], bd))
    kc_raw = zc_ref[:, POOL_W + ATTN_W:POOL_W + ATTN_W + KV_W]
    kc = _norm_rope(kc_raw, kn_ref[...], cc_ref[...], sc_ref[...], bd)[0]
    kp = _norm_rope(zp_ref[:, :KV_W], kn_ref[...], cp_ref[...], sp_ref[...], bd)[0]
    kband = jnp.concatenate([kp, kc], axis=0).astype(BF16)
    vband = jnp.concatenate([zp_ref[:, KV_W:], zc_ref[:, POOL_W + ATTN_W + KV_W:POOL_W + ATTN_W + 2 * KV_W]], axis=0).astype(BF16)
    return q, kband, vband


MASKED = -1e30


def _window_bias(first_block):
    row = np.arange(Q_PER_KV * BLOCK)[:, None] % BLOCK
    col = np.arange(2 * BLOCK)[None, :]
    dist = row + BLOCK - col
    inside = (dist >= 0) & (dist < BLOCK) & ((col >= BLOCK) | (not first_block))
    return jnp.asarray(np.where(inside, 0.0, MASKED), F32)


def _stack_heads(tiles, kvh):
    return jnp.concatenate([_to_kv_lanes(tiles[h // 2], h) for h in range(kvh * Q_PER_KV, (kvh + 1) * Q_PER_KV)], axis=0)


def _unstack_heads(stacked, kvh, tiles):
    for g in range(Q_PER_KV):
        h = kvh * Q_PER_KV + g
        t = _from_kv_lanes(stacked[g * BLOCK:(g + 1) * BLOCK], h)
        tiles[h // 2] = t if tiles[h // 2] is None else tiles[h // 2] + t


def _sink_column(sink_ref, kvh):
    grp = lax.broadcasted_iota(jnp.int32, (Q_PER_KV * BLOCK, 1), 0) >> 7
    s = [sink_ref[kvh * Q_PER_KV + g] for g in range(Q_PER_KV)]
    return jnp.where(grp == 0, s[0], jnp.where(grp == 1, s[1], jnp.where(grp == 2, s[2], s[3])))


def _to_kv_lanes(t, h):
    kvh = h // Q_PER_KV
    if (h % 2) != kvh:
        t = pltpu.roll(t, HEAD_DIM, 1)
    lane = lax.broadcasted_iota(jnp.int32, t.shape, 1)
    return jnp.where((lane >= HEAD_DIM) == (kvh == 1), t, 0.0)


def _from_kv_lanes(t, h):
    kvh = h // Q_PER_KV
    lane = lax.broadcasted_iota(jnp.int32, t.shape, 1)
    t = jnp.where((lane >= HEAD_DIM) == (kvh == 1), t, 0.0)
    if (h % 2) != kvh:
        t = pltpu.roll(t, HEAD_DIM, 1)
    return t


def _attn_probs(qh, kband, bias, sink):
    sc = _dot(qh, kband, 1, 1) * (HEAD_DIM ** -0.5) + bias
    m = jnp.maximum(jnp.max(sc, axis=1, keepdims=True), sink)
    p = jnp.exp(sc - m)
    psink = jnp.exp(sink - m)
    den = jnp.sum(p, axis=1, keepdims=True) + psink
    return p / den, psink / den


def _attn_specs(nb):
    cur = lambda i: (i, 0)
    prev = lambda i: (jnp.maximum(i - 1, 0), 0)
    tab = lambda f: pl.BlockSpec((BLOCK, LANES), f)
    vec = pl.BlockSpec((1, LANES), lambda i: (0, 0))
    return [pl.BlockSpec((BLOCK, 1024), cur),
            pl.BlockSpec((BLOCK, 2 * KV_W), lambda i: (jnp.maximum(i - 1, 0), 3)),
            tab(cur), tab(prev), tab(cur), tab(prev), vec, vec,
            pl.BlockSpec(memory_space=pltpu.SMEM)] + [pl.BlockSpec((Q_PER_KV * BLOCK, 2 * BLOCK), lambda i: (0, 0))] * 2


def _attn_fwd(z, cosf, sinf, qn, kn, sinks, name):
    s = z.shape[0]
    nb = s // BLOCK

    def body(zc_ref, zp_ref, cc_ref, cp_ref, sc_ref, sp_ref, qn_ref, kn_ref, sink_ref, bias_ref, bias0_ref, o_ref):
        i = pl.program_id(0)
        bd = _head_mean_matrix()
        q, kband, vband = _attn_setup(zc_ref, zp_ref, cc_ref, cp_ref, sc_ref, sp_ref, qn_ref, kn_ref, bd)
        mask = jnp.where(i > 0, bias_ref[...], bias0_ref[...])
        out = [None] * (ATTN_W // LANES)
        for kvh in range(N_Q_HEADS // Q_PER_KV):
            qs = _stack_heads([t[1] for t in q], kvh)
            probs, _ = _attn_probs(qs, kband, mask, _sink_column(sink_ref, kvh))
            _unstack_heads(_dot(probs, vband), kvh, out)
        for j, o in enumerate(out):
            o_ref[:, j * LANES:(j + 1) * LANES] = o.astype(BF16)

    return pl.pallas_call(
        body, name=name, grid=(nb,), in_specs=_attn_specs(nb),
        out_specs=pl.BlockSpec((BLOCK, ATTN_W), lambda i: (i, 0)),
        out_shape=jax.ShapeDtypeStruct((s, ATTN_W), BF16), compiler_params=_params("parallel"),
    )(z, z, cosf, cosf, sinf, sinf, qn, kn, sinks, _window_bias(False), _window_bias(True))


def _attn_bwd(z, cosf, sinf, qn, kn, sinks, d_out, name):
    s = z.shape[0]
    nb = s // BLOCK
    nt = ATTN_W // LANES

    def body(zc_ref, zp_ref, cc_ref, cp_ref, sc_ref, sp_ref, qn_ref, kn_ref, sink_ref, bias_ref, bias0_ref, do_ref,
             dq_ref, dkc_ref, dkp_ref, dvc_ref, dvp_ref, dqn_ref, dsink_ref):
        i = pl.program_id(0)
        bd = _head_mean_matrix()
        q, kband, vband = _attn_setup(zc_ref, zp_ref, cc_ref, cp_ref, sc_ref, sp_ref, qn_ref, kn_ref, bd)
        mask = jnp.where(i > 0, bias_ref[...], bias0_ref[...])

        @pl.when(i == 0)
        def _():
            dqn_ref[...] = jnp.zeros_like(dqn_ref)
            dsink_ref[...] = jnp.zeros_like(dsink_ref)

        dq = [None] * nt
        dk = jnp.zeros((2 * BLOCK, KV_W), F32)
        dv = jnp.zeros((2 * BLOCK, KV_W), F32)
        d_tiles = [do_ref[:, j * LANES:(j + 1) * LANES] for j in range(nt)]
        for kvh in range(N_Q_HEADS // Q_PER_KV):
            qs = _stack_heads([t[1] for t in q], kvh)
            probs, psink = _attn_probs(qs, kband, mask, _sink_column(sink_ref, kvh))
            dos = _stack_heads(d_tiles, kvh)
            dp = _dot(dos, vband, 1, 1)
            delta = jnp.sum(dp * probs, axis=1, keepdims=True)
            ds = (probs * (dp - delta)) * (HEAD_DIM ** -0.5)
            dsink = -psink * delta
            for g in range(Q_PER_KV):
                h = kvh * Q_PER_KV + g
                dsink_ref[h:h + 1, :] += jnp.broadcast_to(jnp.sum(dsink[g * BLOCK:(g + 1) * BLOCK], axis=0, keepdims=True), (1, LANES))
            _unstack_heads(_dot(ds, kband), kvh, dq)
            dk = dk + _dot(ds, qs, 0, 0)
            dv = dv + _dot(probs, dos, 0, 0)
        dgn = jnp.zeros((1, LANES), F32)
        for j in range(nt):
            t, _, n, r = q[j]
            dt, g = _norm_rope_bwd(dq[j], t, n, r, qn_ref[...], cc_ref[...], sc_ref[...], bd)
            dq_ref[:, j * LANES:(j + 1) * LANES] = dt.astype(BF16)
            dgn = dgn + g
        dqn_ref[...] += jnp.broadcast_to(dgn, (8, LANES))
        dkp_ref[...] = dk[:BLOCK]
        dkc_ref[...] = dk[BLOCK:]
        dvp_ref[...] = dv[:BLOCK]
        dvc_ref[...] = dv[BLOCK:]

        @pl.when(i == nb - 1)
        def _():
            acc = dqn_ref[...]
            dqn_ref[...] = acc + pltpu.roll(acc, HEAD_DIM, 1)

    blk = lambda w: pl.BlockSpec((BLOCK, w), lambda i: (i, 0))
    acc = pl.BlockSpec((8, LANES), lambda i: (0, 0))
    kv = jax.ShapeDtypeStruct((s, KV_W), F32)
    return pl.pallas_call(
        body, name=name, grid=(nb,), in_specs=_attn_specs(nb) + [blk(ATTN_W)],
        out_specs=[blk(ATTN_W), blk(KV_W), blk(KV_W), blk(KV_W), blk(KV_W), acc, acc],
        out_shape=[jax.ShapeDtypeStruct((s, ATTN_W), BF16), kv, kv, kv, kv,
                   jax.ShapeDtypeStruct((8, LANES), F32), jax.ShapeDtypeStruct((8, LANES), F32)],
        compiler_params=_params("arbitrary"),
    )(z, z, cosf, cosf, sinf, sinf, qn, kn, sinks, _window_bias(False), _window_bias(True), d_out)


def _kv_post(z, cosf, sinf, kn, dkc, dkp, dvc, dvp, dxp, dq, duv, dz, name):
    s = z.shape[0]
    nb = s // BLOCK

    def body(zk_ref, c_ref, s_ref, kn_ref, dkc_ref, dkp_ref, dvc_ref, dvp_ref, dxp_ref, dq_ref, duv_ref, dz_in,
             dz_ref, dkn_ref):
        j = pl.program_id(0)
        bd = _head_mean_matrix()
        last = j == nb - 1
        d = dkc_ref[...] + jnp.where(last, 0.0, dkp_ref[...])
        t = zk_ref[:, :KV_W]
        _, n, r = _norm_rope(t, kn_ref[...], c_ref[...], s_ref[...], bd)
        dt, g = _norm_rope_bwd(d, t, n, r, kn_ref[...], c_ref[...], s_ref[...], bd)
        dvv = dvc_ref[...] + jnp.where(last, 0.0, dvp_ref[...])
        dz_ref[:, 0:POOL_W] = dxp_ref[...]
        dz_ref[:, POOL_W:POOL_W + ATTN_W] = dq_ref[...]
        dz_ref[:, POOL_W + ATTN_W:POOL_W + ATTN_W + KV_W] = dt.astype(BF16)
        dz_ref[:, POOL_W + ATTN_W + KV_W:POOL_W + ATTN_W + 2 * KV_W] = dvv.astype(BF16)
        dz_ref[:, POOL_W + ATTN_W + 2 * KV_W:GATE_COL0] = duv_ref[...]

        @pl.when(j == 0)
        def _():
            dkn_ref[...] = jnp.zeros_like(dkn_ref)

        dkn_ref[...] += jnp.broadcast_to(g, (8, LANES))

        @pl.when(last)
        def _():
            acc = dkn_ref[...]
            dkn_ref[...] = acc + pltpu.roll(acc, HEAD_DIM, 1)

    cur = lambda w: pl.BlockSpec((BLOCK, w), lambda j: (j, 0))
    nxt = pl.BlockSpec((BLOCK, KV_W), lambda j: (jnp.minimum(j + 1, nb - 1), 0))
    vec = pl.BlockSpec((1, LANES), lambda j: (0, 0))
    return pl.pallas_call(
        body, name=name, grid=(nb,),
        in_specs=[pl.BlockSpec((BLOCK, 2 * KV_W), lambda j: (j, 3)), cur(LANES), cur(LANES), vec,
                  cur(KV_W), nxt, cur(KV_W), nxt, cur(POOL_W), cur(ATTN_W), cur(2 * SGU_W),
                  pl.BlockSpec(memory_space=pl.ANY)],
        out_specs=[pl.BlockSpec((BLOCK, GATE_COL0), lambda j: (j, 0)), pl.BlockSpec((8, LANES), lambda j: (0, 0))],
        out_shape=[jax.ShapeDtypeStruct(dz.shape, dz.dtype), jax.ShapeDtypeStruct((8, LANES), F32)],
        input_output_aliases={11: 0}, compiler_params=_params("arbitrary"),
    )(z, cosf, sinf, kn, dkc, dkp, dvc, dvp, dxp, dq, duv, dz)


def _sgu_setup(z_ref, ws_ref, vn_ref, bd):
    us = z_ref[:, :SGU_W]
    vs = z_ref[:, SGU_W:]
    ug, tu = _gelu(us)
    gv, tv = _gelu(vs)
    rr = jnp.concatenate([lax.rsqrt(_head_mean(gv[:, k * LANES:(k + 1) * LANES] ** 2, bd) + EPS) for k in range(2)], axis=1)
    vg = (gv * rr) * vn_ref[...]
    tril = lax.broadcasted_iota(jnp.int32, (BLOCK, BLOCK), 0) >= lax.broadcasted_iota(jnp.int32, (BLOCK, BLOCK), 1)
    w = [jnp.where(tril, ws_ref[g], 0.0).astype(BF16) for g in range(4)]
    return us, vs, ug, tu, gv, tv, rr, vg, w, tril


def _group_select(parts):
    lane = lax.broadcasted_iota(jnp.int32, parts[0].shape, 1)
    return _pool_lane_select(lane, *parts)


def _sgu_fwd(z, ws, bcol, vn, name):
    s = z.shape[0]
    nb = s // BLOCK

    def body(z_ref, ws_ref, b_ref, vn_ref, c_ref):
        bd = _head_mean_matrix()
        _, _, ug, _, _, _, _, vg, w, _ = _sgu_setup(z_ref, ws_ref, vn_ref, bd)
        sg = _group_select([_dot(w[g], vg) for g in range(4)]) + b_ref[...]
        c_ref[...] = (ug * sg).astype(BF16)

    return pl.pallas_call(
        body, name=name, grid=(nb,),
        in_specs=[pl.BlockSpec((BLOCK, 2 * SGU_W), lambda i: (i, 2)), pl.BlockSpec((4, BLOCK, BLOCK), lambda i: (0, 0, 0)),
                  pl.BlockSpec((BLOCK, SGU_W), lambda i: (0, 0)), pl.BlockSpec((1, SGU_W), lambda i: (0, 0))],
        out_specs=pl.BlockSpec((BLOCK, SGU_W), lambda i: (i, 0)),
        out_shape=jax.ShapeDtypeStruct((s, SGU_W), BF16), compiler_params=_params("parallel"),
    )(z, ws, bcol, vn)


def _sgu_bwd(z, ws, bcol, vn, dc, name):
    s = z.shape[0]
    nb = s // BLOCK

    def body(z_ref, ws_ref, b_ref, vn_ref, dc_ref, duv_ref, dws_ref, db_ref, dvn_ref):
        i = pl.program_id(0)
        bd = _head_mean_matrix()
        us, vs, ug, tu, gv, tv, rr, vg, w, tril = _sgu_setup(z_ref, ws_ref, vn_ref, bd)
        sg = _group_select([_dot(w[g], vg) for g in range(4)]) + b_ref[...]
        dcv = dc_ref[...]
        dug = dcv * sg
        dsg = dcv * ug
        lane = lax.broadcasted_iota(jnp.int32, dsg.shape, 1)

        @pl.when(i == 0)
        def _():
            dws_ref[...] = jnp.zeros_like(dws_ref)
            db_ref[...] = jnp.zeros_like(db_ref)
            dvn_ref[...] = jnp.zeros_like(dvn_ref)

        for g in range(4):
            dsg_g = jnp.where((lane >= g * HEAD_DIM) & (lane < (g + 1) * HEAD_DIM), dsg, 0.0)
            dws_ref[g] += jnp.where(tril, _dot(dsg_g, vg, 1, 1), 0.0)
        dvg = _group_select([_dot(w[g], dsg, 0, 0) for g in range(4)])
        db_ref[...] += dsg
        n = gv * rr
        part = jnp.sum(dvg * n, axis=0, keepdims=True)
        dvn_ref[...] += jnp.broadcast_to(part[:, :LANES] + part[:, LANES:], (8, LANES))
        u = dvg * vn_ref[...]
        tu_ = gv * u
        mean = jnp.concatenate([_head_mean(tu_[:, k * LANES:(k + 1) * LANES], bd) for k in range(2)], axis=1)
        dgv = rr * u - gv * (rr * rr * rr) * mean
        duv_ref[:, :SGU_W] = (dug * _gelu_grad(us, tu)).astype(BF16)
        duv_ref[:, SGU_W:] = (dgv * _gelu_grad(vs, tv)).astype(BF16)

        @pl.when(i == nb - 1)
        def _():
            acc = dvn_ref[...]
            dvn_ref[...] = acc + pltpu.roll(acc, HEAD_DIM, 1)
            for k in range(2):
                db_ref[:, k * LANES:(k + 1) * LANES] = _head_mean(db_ref[:, k * LANES:(k + 1) * LANES], bd) * float(HEAD_DIM)

    return pl.pallas_call(
        body, name=name, grid=(nb,),
        in_specs=[pl.BlockSpec((BLOCK, 2 * SGU_W), lambda i: (i, 2)), pl.BlockSpec((4, BLOCK, BLOCK), lambda i: (0, 0, 0)),
                  pl.BlockSpec((BLOCK, SGU_W), lambda i: (0, 0)), pl.BlockSpec((1, SGU_W), lambda i: (0, 0)),
                  pl.BlockSpec((BLOCK, SGU_W), lambda i: (i, 0))],
        out_specs=[pl.BlockSpec((BLOCK, 2 * SGU_W), lambda i: (i, 0)), pl.BlockSpec((4, BLOCK, BLOCK), lambda i: (0, 0, 0)),
                   pl.BlockSpec((BLOCK, SGU_W), lambda i: (0, 0)), pl.BlockSpec((8, LANES), lambda i: (0, 0))],
        out_shape=[jax.ShapeDtypeStruct((s, 2 * SGU_W), BF16), jax.ShapeDtypeStruct((4, BLOCK, BLOCK), F32),
                   jax.ShapeDtypeStruct((BLOCK, SGU_W), F32), jax.ShapeDtypeStruct((8, LANES), F32)],
        compiler_params=_params("arbitrary"),
    )(z, ws, bcol, vn, dc)


MERGE_TN = 512
MERGE_TM = 1024
MERGE_OUT_TM = 512


def _rms_rows(x, g):
    r = lax.rsqrt(jnp.mean(x * x, axis=-1, keepdims=True) + EPS)
    return ((x * r) * g).astype(BF16)


def _merge_out_fwd(a, b, c, wpa, wpb, wpc, z, w_out, res, gain, name):
    s = z.shape[0]
    tm = _tile(s, (MERGE_OUT_TM, BLOCK))
    gate0 = GATE_COL0 // MERGE_TN
    nn = D_MODEL // MERGE_TN

    def body(a_ref, b_ref, c_ref, wa_ref, wb_ref, wc_ref, g0_ref, g1_ref, g2_ref, wo_ref, res_ref, gain_ref,
             o_ref, m_ref, h_ref, acc_ref):
        n = pl.program_id(1)
        r = _sigmoid(g0_ref[...]) * _dot(a_ref[...], wa_ref[...])
        r = r + _sigmoid(g1_ref[...]) * _dot(b_ref[...], wb_ref[...])
        r = r + _sigmoid(g2_ref[...]) * _dot(c_ref[...], wc_ref[...])
        merged = r.astype(BF16)
        m_ref[...] = merged

        @pl.when(n == 0)
        def _():
            acc_ref[...] = jnp.zeros_like(acc_ref)

        acc_ref[...] += jnp.dot(merged, wo_ref[...], preferred_element_type=F32)

        @pl.when(n == nn - 1)
        def _():
            x = acc_ref[...] + res_ref[...]
            o_ref[...] = x
            h_ref[...] = _rms_rows(x, gain_ref[...])

    x_spec = lambda w: pl.BlockSpec((tm, w), lambda i, n: (i, 0))
    w_spec = lambda w: pl.BlockSpec((w, MERGE_TN), lambda i, n: (0, n))
    g_spec = lambda br: pl.BlockSpec((tm, MERGE_TN), lambda i, n: (i, gate0 + 2 * br + n))
    row = pl.BlockSpec((tm, D_MODEL), lambda i, n: (i, 0))
    return pl.pallas_call(
        body, name=name, grid=(s // tm, nn),
        in_specs=[x_spec(POOL_W), x_spec(ATTN_W), x_spec(SGU_W), w_spec(POOL_W), w_spec(ATTN_W), w_spec(SGU_W),
                  g_spec(0), g_spec(1), g_spec(2), pl.BlockSpec((MERGE_TN, D_MODEL), lambda i, n: (n, 0)), row,
                  pl.BlockSpec((1, D_MODEL), lambda i, n: (0, 0))],
        out_specs=[row, pl.BlockSpec((tm, MERGE_TN), lambda i, n: (i, n)), row],
        out_shape=[jax.ShapeDtypeStruct((s, D_MODEL), F32), jax.ShapeDtypeStruct((s, D_MODEL), BF16),
                   jax.ShapeDtypeStruct((s, D_MODEL), BF16)],
        scratch_shapes=[pltpu.VMEM((tm, D_MODEL), F32)], compiler_params=_params("parallel", "arbitrary"),
    )(a, b, c, wpa, wpb, wpc, z, z, z, w_out, res, gain.reshape(1, D_MODEL))


def _branch_bwd(br, xb, wp, z, dm, dz, name):
    s = z.shape[0]
    kb = xb.shape[1]
    tm = _tile(s, (MERGE_TM, BLOCK))
    gate0 = GATE_COL0 // MERGE_TN
    aliased = dz is not None

    def body(*refs):
        x_ref, w_ref, g_ref, dm_ref = refs[:4]
        dz_ref, dy_ref, dx_ref = refs[-3:]
        n = pl.program_id(1)
        y = _dot(x_ref[...], w_ref[...])
        sg = _sigmoid(g_ref[...])
        dmv = dm_ref[...]
        dy = (dmv * sg).astype(BF16)
        dy_ref[...] = dy
        dz_ref[...] = ((dmv * y) * (sg * (1.0 - sg))).astype(BF16)
        dx = _dot(dy, w_ref[...], 1, 1)

        @pl.when(n == 0)
        def _():
            dx_ref[...] = dx

        @pl.when(n > 0)
        def _():
            dx_ref[...] += dx

    in_specs = [pl.BlockSpec((tm, kb), lambda i, n: (i, 0)), pl.BlockSpec((kb, MERGE_TN), lambda i, n: (0, n)),
                pl.BlockSpec((tm, MERGE_TN), lambda i, n: (i, gate0 + 2 * br + n)),
                pl.BlockSpec((tm, MERGE_TN), lambda i, n: (i, n))]
    args = [xb, wp, z, dm]
    if aliased:
        in_specs.append(pl.BlockSpec(memory_space=pl.ANY))
        args.append(dz)
    return pl.pallas_call(
        body, name=name, grid=(s // tm, D_MODEL // MERGE_TN), in_specs=in_specs,
        out_specs=[pl.BlockSpec((tm, MERGE_TN), lambda i, n: (i, gate0 + 2 * br + n)),
                   pl.BlockSpec((tm, MERGE_TN), lambda i, n: (i, n)),
                   pl.BlockSpec((tm, kb), lambda i, n: (i, 0))],
        out_shape=[jax.ShapeDtypeStruct((s, IN_COLS), BF16), jax.ShapeDtypeStruct((s, D_MODEL), BF16),
                   jax.ShapeDtypeStruct((s, kb), F32)],
        input_output_aliases={4: 0} if aliased else {},
        compiler_params=_params("parallel", "arbitrary"),
    )(*args)


FFN_TM = 256
FFN_TC = 2816
FFN_STRIP = 256
FFN_UNROLL = 4


def _conv3(cur, prev, w_ref, b_ref):
    cat = jnp.concatenate([prev, cur], axis=0)
    x1 = pltpu.roll(cat, 1, 0)[HALO_CONV:]
    x2 = pltpu.roll(cat, 2, 0)[HALO_CONV:]
    return w_ref[0:1, :] * x2 + w_ref[1:2, :] * x1 + w_ref[2:3, :] * cur + b_ref[...], x1, x2


def _ffn_specs(tm):
    per = tm // HALO_CONV
    cur = pl.BlockSpec((tm, FFN_TC), lambda j, i: (i, j))
    prev = pl.BlockSpec((HALO_CONV, FFN_TC), lambda j, i: (jnp.maximum(i * per - 1, 0), j))
    w = pl.BlockSpec((3, FFN_TC), lambda j, i: (0, j))
    b = pl.BlockSpec((1, FFN_TC), lambda j, i: (0, j))
    return cur, prev, w, b


DOWN_TM = 256
DOWN_TK = 2816
DOWN_CHUNK = 256


def _ffn_down_fwd(upg, upv, cwg, cwv, cbg, cbv, w_down, res, gain, target, name):
    s = upg.shape[0]
    d = w_down.shape[1]
    tm = _tile(s, (DOWN_TM, DOWN_CHUNK, BLOCK))
    chunk = min(DOWN_CHUNK, tm)
    per = tm // HALO_CONV
    nk = D_FF // DOWN_TK
    last_layer = target is not None

    def body(g_ref, gp_ref, v_ref, vp_ref, wg_ref, wv_ref, bg_ref, bv_ref, w_ref, res_ref, extra_ref, o_ref, act_ref, tail_ref,
             acc_ref):
        first = pl.program_id(0) == 0
        kk = pl.program_id(1)

        @pl.when(kk == 0)
        def _():
            acc_ref[...] = jnp.zeros_like(acc_ref)

        for c in range(tm // chunk):
            rows = slice(c * chunk, (c + 1) * chunk)
            before = slice(c * chunk - HALO_CONV, c * chunk)
            g_p = jnp.where(first, 0.0, gp_ref[...]) if c == 0 else g_ref[before, :]
            v_p = jnp.where(first, 0.0, vp_ref[...]) if c == 0 else v_ref[before, :]
            gate = _conv3(g_ref[rows, :], g_p, wg_ref, bg_ref)[0]
            val = _conv3(v_ref[rows, :], v_p, wv_ref, bv_ref)[0]
            act = ((gate * _sigmoid(gate)) * val).astype(BF16)
            act_ref[rows, :] = act
            acc_ref[rows, :] += jnp.dot(act, w_ref[...], preferred_element_type=F32)

        @pl.when(kk == nk - 1)
        def _():
            x = acc_ref[...] + res_ref[...]
            if last_layer:
                err = x - extra_ref[...]
                o_ref[...] = err * (1.0 / d)
                part = jnp.sum(jnp.sum(err * err, axis=-1, keepdims=True) * (1.0 / d), axis=0, keepdims=True) * 0.5
                part = jnp.broadcast_to(part, (8, LANES))

                @pl.when(first)
                def _():
                    tail_ref[...] = part

                @pl.when(jnp.logical_not(first))
                def _():
                    tail_ref[...] += part
            else:
                o_ref[...] = x
                tail_ref[...] = _rms_rows(x, extra_ref[...])

    cur = pl.BlockSpec((tm, DOWN_TK), lambda i, kk: (i, kk))
    prev = pl.BlockSpec((HALO_CONV, DOWN_TK), lambda i, kk: (jnp.maximum(i * per - 1, 0), kk))
    w = pl.BlockSpec((3, DOWN_TK), lambda i, kk: (0, kk))
    b = pl.BlockSpec((1, DOWN_TK), lambda i, kk: (0, kk))
    row = pl.BlockSpec((tm, d), lambda i, kk: (i, 0))
    if last_layer:
        extra, extra_spec = target, row
        tail_spec, tail_shape = pl.BlockSpec((8, LANES), lambda i, kk: (0, 0)), jax.ShapeDtypeStruct((8, LANES), F32)
    else:
        extra, extra_spec = gain.reshape(1, d), pl.BlockSpec((1, d), lambda i, kk: (0, 0))
        tail_spec, tail_shape = row, jax.ShapeDtypeStruct((s, d), BF16)
    return pl.pallas_call(
        body, name=name, grid=(s // tm, nk),
        in_specs=[cur, prev, cur, prev, w, w, b, b, pl.BlockSpec((DOWN_TK, d), lambda i, kk: (kk, 0)), row, extra_spec],
        out_specs=[row, cur, tail_spec],
        out_shape=[jax.ShapeDtypeStruct((s, d), F32), jax.ShapeDtypeStruct((s, D_FF), BF16), tail_shape],
        scratch_shapes=[pltpu.VMEM((tm, d), F32)], compiler_params=_params("arbitrary", "arbitrary"),
    )(upg, upg, upv, upv, cwg, cwv, cbg, cbv, w_down, res, extra)


def _ffn_bwd(upg, upv, cwg, cwv, cbg, cbv, dact, name):
    s = upg.shape[0]
    tm = _tile(s, (FFN_TM, BLOCK))
    per = tm // HALO_CONV
    nrow = s // tm
    cur, prev, w, b = _ffn_specs(tm)
    nxt = pl.BlockSpec((HALO_CONV, FFN_TC), lambda j, i: (jnp.minimum((i + 1) * per, nrow * per - 1), j))

    nch = tm // 8
    rows8 = lambda r: pl.ds(pl.multiple_of(r * 8, 8), 8)

    def body(g_ref, gp_ref, gn_ref, v_ref, vp_ref, vn_ref, wg_ref, wv_ref, bg_ref, bv_ref, da_ref, dan_ref,
             dg_ref, dv_ref, dwg_ref, dwv_ref, og_ref, ov_ref):
        i = pl.program_id(1)
        first = i == 0
        last = i == nrow - 1
        row = lax.broadcasted_iota(jnp.int32, (8, FFN_STRIP), 0)

        keep_down = {k: row >= k for k in (1, 2)}
        keep_up = {k: row < 8 - k for k in (1, 2)}

        def down(cur, prev, k):
            return jnp.where(keep_down[k], pltpu.roll(cur, k, 0), pltpu.roll(prev, k, 0))

        def up(cur, nxt, k):
            return jnp.where(keep_up[k], pltpu.roll(cur, 8 - k, 0), pltpu.roll(nxt, 8 - k, 0))

        @pl.when(first)
        def _():
            dwg_ref[...] = jnp.zeros_like(dwg_ref)
            dwv_ref[...] = jnp.zeros_like(dwv_ref)

        for c in range(FFN_TC // FFN_STRIP):
            cols = slice(c * FFN_STRIP, (c + 1) * FFN_STRIP)
            wg = [functools.partial(lambda k: wg_ref[k:k + 1, cols], k) for k in range(3)]
            wv = [functools.partial(lambda k: wv_ref[k:k + 1, cols], k) for k in range(3)]

            def conv_grads(g_cur, g_prev, v_cur, v_prev, da):
                gate = wg[0]() * down(g_cur, g_prev, 2) + wg[1]() * down(g_cur, g_prev, 1) + wg[2]() * g_cur + bg_ref[:, cols]
                val = wv[0]() * down(v_cur, v_prev, 2) + wv[1]() * down(v_cur, v_prev, 1) + wv[2]() * v_cur + bv_ref[:, cols]
                sg = _sigmoid(gate)
                return (da * val) * (sg * (1.0 + gate * (1.0 - sg))), da * (gate * sg)

            def passes(q, carry):
                for u in range(FFN_UNROLL):
                    carry = one_pass(q * FFN_UNROLL + u, carry)
                return carry

            def one_pass(r, carry, tile_end=False):
                dg_cur, dv_cur, acc = carry
                g_r, v_r = g_ref[rows8(r), cols], v_ref[rows8(r), cols]
                if tile_end:
                    g_n, v_n, da_n = gn_ref[:, cols], vn_ref[:, cols], jnp.where(last, 0.0, dan_ref[:, cols])
                else:
                    g_n, v_n, da_n = g_ref[rows8(r + 1), cols], v_ref[rows8(r + 1), cols], da_ref[rows8(r + 1), cols]
                dg_n, dv_n = conv_grads(g_n, g_r, v_n, v_r, da_n)
                new_acc = []
                for o_ref, w, d_cur, d_n, x0, a in ((og_ref, wg, dg_cur, dg_n, g_r, acc[:4]), (ov_ref, wv, dv_cur, dv_n, v_r, acc[4:])):
                    d1, d2 = up(d_cur, d_n, 1), up(d_cur, d_n, 2)
                    o_ref[rows8(r), cols] = w[2]() * d_cur + w[1]() * d1 + w[0]() * d2
                    new_acc += [a[0] + d2 * x0, a[1] + d1 * x0, a[2] + d_cur * x0, a[3] + d_cur]
                return dg_n, dv_n, tuple(new_acc)

            g_p = jnp.where(first, 0.0, gp_ref[:, cols])
            v_p = jnp.where(first, 0.0, vp_ref[:, cols])
            dg0, dv0 = conv_grads(g_ref[0:8, cols], g_p, v_ref[0:8, cols], v_p, da_ref[0:8, cols])
            zero = jnp.zeros((8, FFN_STRIP), F32)
            carry = lax.fori_loop(0, nch // FFN_UNROLL - 1, passes, (dg0, dv0, (zero,) * 8))
            for r in range(nch - FFN_UNROLL, nch):
                carry = one_pass(r, carry, tile_end=r == nch - 1)
            for ref, a in ((dwg_ref, carry[2][:4]), (dwv_ref, carry[2][4:])):
                for k in range(4):
                    ref[k:k + 1, cols] += jnp.sum(a[k], axis=0, keepdims=True)
        dg_ref[...] = og_ref[...].astype(BF16)
        dv_ref[...] = ov_ref[...].astype(BF16)

    acc = pl.BlockSpec((8, FFN_TC), lambda j, i: (0, j))
    full = jax.ShapeDtypeStruct((s, D_FF), BF16)
    accs = jax.ShapeDtypeStruct((8, D_FF), F32)
    return pl.pallas_call(
        body, name=name, grid=(D_FF // FFN_TC, nrow), in_specs=[cur, prev, nxt, cur, prev, nxt, w, w, b, b, cur, nxt],
        out_specs=[cur, cur, acc, acc], out_shape=[full, full, accs, accs],
        scratch_shapes=[pltpu.VMEM((tm, FFN_TC), F32), pltpu.VMEM((tm, FFN_TC), F32)],
        compiler_params=_params("parallel", "arbitrary"),
    )(upg, upg, upg, upv, upv, upv, cwg, cwv, cbg, cbv, dact, dact)


def _mesh_place():
    return lax.axis_index("x"), lax.axis_index("y"), lax.axis_index("c")


def _all_gather(shards, name):
    na = len(shards)

    def body(*refs):
        x_refs, out_refs = refs[:na], refs[na:2 * na]
        send_sems, recv_sems, local_sems = refs[2 * na:]
        x, y, cc = _mesh_place()
        me, sibling = (x, y, cc), (x, y, 1 - cc)
        chips = [(1 - x, y), (x, 1 - y), (1 - x, 1 - y)]

        def copy(k, a, block, to, from_input=False):
            slot = out_refs[a].at[4 * block[0] + 2 * block[1] + block[2]]
            return pltpu.make_async_remote_copy(
                src_ref=x_refs[a] if from_input else slot, dst_ref=slot, send_sem=send_sems.at[k * na + a],
                recv_sem=recv_sems.at[k * na + a], device_id=to, device_id_type=pl.DeviceIdType.MESH)

        mine = [pltpu.make_async_copy(x_refs[a], out_refs[a].at[4 * x + 2 * y + cc], local_sems.at[a]) for a in range(na)]
        for cp in mine:
            cp.start()
        first = [copy(0, a, me, sibling, True) for a in range(na)]
        first += [copy(1 + j, a, me, (*chip, cc), True) for j, chip in enumerate(chips) for a in range(na)]
        for cp in first:
            cp.start()
        passed = []
        for j, chip in enumerate(chips):
            for a in range(na):
                copy(1 + j, a, (*chip, cc), me).wait_recv()
                passed.append(copy(4 + j, a, (*chip, cc), sibling))
                passed[-1].start()
        for a in range(na):
            copy(0, a, sibling, me).wait_recv()
        for j, chip in enumerate(chips):
            for a in range(na):
                copy(4 + j, a, (*chip, 1 - cc), me).wait_recv()
        for cp in first + passed:
            cp.wait_send()
        for cp in mine:
            cp.wait()

    hbm = pl.BlockSpec(memory_space=pl.ANY)
    return pl.pallas_call(
        body, name=name, out_shape=[jax.ShapeDtypeStruct((N_DEV,) + t.shape, t.dtype) for t in shards],
        in_specs=[hbm] * na, out_specs=[hbm] * na,
        scratch_shapes=[pltpu.SemaphoreType.DMA((7 * na,)), pltpu.SemaphoreType.DMA((7 * na,)), pltpu.SemaphoreType.DMA((na,))],
    )(*shards)


def _peer(k):
    x, y, cc = _mesh_place()
    px = 1 - x if (k >> 2) & 1 else x
    py = 1 - y if (k >> 1) & 1 else y
    pc = 1 - cc if k & 1 else cc
    return (px, py, pc), 4 * px + 2 * py + pc


def _push_copy(src_ref, land_ref, k, a, na, send_sems, recv_sems, indexed, landed):
    x, y, cc = _mesh_place()
    place, peer = _peer(k)
    sem = (k - 1) * na + a
    return pltpu.make_async_remote_copy(
        src_ref=src_ref.at[peer] if indexed else src_ref, dst_ref=land_ref.at[peer if landed else 4 * x + 2 * y + cc],
        send_sem=send_sems.at[sem], recv_sem=recv_sems.at[sem], device_id=place, device_id_type=pl.DeviceIdType.MESH)


_HBM = pl.BlockSpec(memory_space=pltpu.HBM)
_SEM = pl.BlockSpec(memory_space=pltpu.SEMAPHORE)
_EFFECT = pltpu.SideEffectType.DATAFLOW_SIDE_EFFECTING


def _push_start(srcs, indexed, name):
    na = len(srcs)
    lands = [lax.empty(t.shape if indexed else (N_DEV,) + t.shape, t.dtype) for t in srcs]

    def body(*refs):
        src_refs, land_refs = refs[:na], refs[na:2 * na]
        send_sems, recv_sems = refs[2 * na], refs[2 * na + 1]
        token = refs[-1]
        for k in range(1, N_DEV):
            for a in range(na):
                _push_copy(src_refs[a], land_refs[a], k, a, na, send_sems, recv_sems, indexed, False).start()
        token[...] = jnp.zeros_like(token)

    sems = pltpu.SemaphoreType.DMA((7 * na,))
    out = pl.pallas_call(
        body, name=name,
        out_shape=(sems, sems, *[pltpu.HBM(t.shape, t.dtype) for t in srcs], *[pltpu.HBM(t.shape, t.dtype) for t in lands],
                   jax.ShapeDtypeStruct((8, LANES), F32)),
        in_specs=[_HBM] * (2 * na), out_specs=(_SEM, _SEM, *[_HBM] * (2 * na), pl.BlockSpec(memory_space=pltpu.VMEM)),
        input_output_aliases={i: 2 + i for i in range(2 * na)},
        compiler_params=pltpu.CompilerParams(has_side_effects=_EFFECT),
    )(*[pltpu.with_memory_space_constraint(t, pltpu.HBM) for t in srcs + lands])
    return out[0], out[1], list(out[2:2 + na]), list(out[2 + na:2 + 2 * na]), out[-1]


def _push_wait(started, indexed, after, name):
    send_sems, recv_sems, srcs, lands, _ = started
    na = len(srcs)

    def body(*refs):
        src_refs, land_refs = refs[:na], refs[na:2 * na]
        send_sems, recv_sems = refs[2 * na], refs[2 * na + 1]
        for k in range(1, N_DEV):
            for a in range(na):
                copy = _push_copy(src_refs[a], land_refs[a], k, a, na, send_sems, recv_sems, indexed, True)
                copy.wait_send()
                copy.wait_recv()

    out = pl.pallas_call(
        body, name=name, out_shape=[pltpu.HBM(t.shape, t.dtype) for t in srcs + lands],
        in_specs=[_HBM] * (2 * na) + [_SEM, _SEM, pl.BlockSpec(memory_space=pl.ANY)], out_specs=[_HBM] * (2 * na),
        input_output_aliases={i: i for i in range(2 * na)},
        compiler_params=pltpu.CompilerParams(has_side_effects=_EFFECT),
    )(*srcs, *lands, send_sems, recv_sems, after)
    x, y, cc = _mesh_place()
    me = 4 * x + 2 * y + cc
    return [lax.dynamic_update_index_in_dim(
        land, lax.dynamic_index_in_dim(src, me, 0, keepdims=False) if indexed else src, me, 0)
        for src, land in zip(out[:na], out[na:])]


def _adamw_sum(parts, w, m, v, name):
    _, r, c = parts.shape
    tr = _tile(r, (256, 128, 64, 32, 16, 8))

    def body(p_ref, w_ref, m_ref, v_ref, g_ref, d_ref, nm_ref, nv_ref):
        _adam_store(_sum_parts(p_ref), w_ref, m_ref, v_ref, g_ref, d_ref, nm_ref, nv_ref)

    row = pl.BlockSpec((tr, c), lambda i: (i, 0))
    shp = jax.ShapeDtypeStruct((r, c), F32)
    return pl.pallas_call(
        body, name=name, grid=(r // tr,), in_specs=[pl.BlockSpec((N_DEV, tr, c), lambda i: (0, i, 0)), row, row, row],
        out_specs=[row, row, row, row], out_shape=[shp, shp, shp, shp], compiler_params=_params("parallel"),
    )(parts, w, m, v)


def _sum_parts(p_ref):
    g = p_ref[0].astype(F32)
    for k in range(1, N_DEV):
        g = g + p_ref[k].astype(F32)
    return g


def _adam_store(g, w_ref, m_ref, v_ref, g_ref, d_ref, nm_ref, nv_ref):
    nm = ADAM_B1 * m_ref[...] + (1.0 - ADAM_B1) * g
    nv = ADAM_B2 * v_ref[...] + (1.0 - ADAM_B2) * (g * g)
    m_hat = nm / (1.0 - ADAM_B1 ** ADAM_STEP)
    v_hat = nv / (1.0 - ADAM_B2 ** ADAM_STEP)
    g_ref[...] = g
    nm_ref[...] = nm
    nv_ref[...] = nv
    d_ref[...] = -ADAM_LR * (m_hat / (jnp.sqrt(v_hat) + ADAM_EPS) + ADAM_WD * w_ref[...])


def _adamw_weight(parts, w, m, v, name):
    _, r, c = w.shape
    tr = _tile(r, (256, 128, 176))
    nr = r // tr

    def body(p0_ref, p1_ref, w_ref, m_ref, v_ref, g_ref, d_ref, nm_ref, nv_ref):
        g = jnp.where(pl.program_id(0) == 0, _sum_parts(p0_ref), _sum_parts(p1_ref))
        _adam_store(g, w_ref, m_ref, v_ref, g_ref, d_ref, nm_ref, nv_ref)

    part = lambda layer: pl.BlockSpec(
        (N_DEV, tr, c), lambda l, i: (0, jnp.where(l == layer, i, (nr - 1) * (1 - layer)), 0))
    row = pl.BlockSpec((None, tr, c), lambda l, i: (l, i, 0))
    shp = jax.ShapeDtypeStruct(w.shape, F32)
    return pl.pallas_call(
        body, name=name, grid=(DEPTH, nr), in_specs=[part(0), part(1), row, row, row],
        out_specs=[row, row, row, row], out_shape=[shp, shp, shp, shp], compiler_params=_params("arbitrary", "arbitrary"),
    )(parts[0], parts[1], w, m, v)


def _full_to_slots(name, t):
    k, n = t.shape
    if name in ROW_SHARDED:
        return t.reshape(N_DEV, k // N_DEV, n)
    return t.reshape(k, N_DEV, n // N_DEV).transpose(1, 0, 2)


def _slots_to_full(name, t):
    _, r, c = t.shape
    if name in ROW_SHARDED:
        return t.reshape(N_DEV * r, c)
    return t.transpose(1, 0, 2).reshape(r, N_DEV * c)


def _small_sizes(shapes, names):
    return [(n, shapes[n], -(-int(math.prod(shapes[n])) // (8 * LANES)) * 8) for n in names]


def _pack_small(tree, shapes, names, row_tile):
    rows = []
    for n, shp, nrow in _small_sizes(shapes, names):
        flat = tree[n].reshape(-1)
        rows.append(jnp.pad(flat, (0, nrow * LANES - flat.shape[0])).reshape(nrow, LANES))
    total = sum(r.shape[0] for r in rows)
    if total % row_tile:
        rows.append(jnp.zeros((-total % row_tile, LANES), F32))
    return jnp.concatenate(rows, axis=0)


def _unpack_small(buf, shapes, names):
    out, r0 = {}, 0
    for n, shp, nrow in _small_sizes(shapes, names):
        out[n] = buf[r0:r0 + nrow].reshape(-1)[:int(math.prod(shp))].reshape(shp)
        r0 += nrow
    return out


def _block_diag(w):
    g = w.shape[0]
    eye = jnp.eye(g, dtype=w.dtype)
    return (eye[:, None, :, None] * w[:, :, None, :]).reshape(g * HEAD_DIM, g * HEAD_DIM)


def kernel(x, positions, norm1, w_in, q_norm, k_norm, sinks, w_pool, pool_scale, sgu_v_norm, w_s, b_s, w_proj_a, w_proj_b, w_proj_c, w_out, norm2, w_up, conv_w, conv_b, w_down, loss_target, m_norm1, m_w_in, m_q_norm, m_k_norm, m_sinks, m_w_pool, m_pool_scale, m_sgu_v_norm, m_w_s, m_b_s, m_w_proj_a, m_w_proj_b, m_w_proj_c, m_w_out, m_norm2, m_w_up, m_conv_w, m_conv_b, m_w_down, v_norm1, v_w_in, v_q_norm, v_k_norm, v_sinks, v_w_pool, v_pool_scale, v_sgu_v_norm, v_w_s, v_b_s, v_w_proj_a, v_w_proj_b, v_w_proj_c, v_w_out, v_norm2, v_w_up, v_conv_w, v_conv_b, v_w_down):
    names = ("norm1", "w_in", "q_norm", "k_norm", "sinks", "w_pool", "pool_scale", "sgu_v_norm", "w_s", "b_s", "w_proj_a",
             "w_proj_b", "w_proj_c", "w_out", "norm2", "w_up", "conv_w", "conv_b", "w_down")
    wts = dict(zip(names, (norm1, w_in, q_norm, k_norm, sinks, w_pool, pool_scale, sgu_v_norm, w_s, b_s, w_proj_a, w_proj_b,
                           w_proj_c, w_out, norm2, w_up, conv_w, conv_b, w_down)))
    mom = dict(zip(names, (m_norm1, m_w_in, m_q_norm, m_k_norm, m_sinks, m_w_pool, m_pool_scale, m_sgu_v_norm, m_w_s, m_b_s,
                           m_w_proj_a, m_w_proj_b, m_w_proj_c, m_w_out, m_norm2, m_w_up, m_conv_w, m_conv_b, m_w_down)))
    var = dict(zip(names, (v_norm1, v_w_in, v_q_norm, v_k_norm, v_sinks, v_w_pool, v_pool_scale, v_sgu_v_norm, v_w_s, v_b_s,
                           v_w_proj_a, v_w_proj_b, v_w_proj_c, v_w_out, v_norm2, v_w_up, v_conv_w, v_conv_b, v_w_down)))
    xs = x[0]
    target = loss_target[0]
    s = xs.shape[0]

    inv_freq = ROPE_THETA ** (-jnp.arange(0, HEAD_DIM, 2, dtype=F32) / HEAD_DIM)
    ang = positions[0].astype(F32)[:, None] * inv_freq
    cosf = jnp.tile(jnp.cos(ang), (1, 4))
    sinf = jnp.tile(jnp.concatenate([-jnp.sin(ang), jnp.sin(ang)], axis=1), (1, 2))

    local = [{n: wts[n][l] if n == "conv_w" else wts[n][l].astype(BF16) for n in SHARDED} for l in range(DEPTH)]
    later = SHARDED[1:]
    full = [{"w_in": _slots_to_full("w_in", _all_gather([local[0]["w_in"]], "gather_w_in_0")[0])}, None]
    gather0 = _push_start([local[0][n] for n in later], False, "gather_rest_0_start")
    norm1_first = norm1[0] + gather0[4][0, 0]

    def layer_consts(l):
        return dict(
            bdw=_block_diag(w_pool[l]).astype(BF16), qn=jnp.tile(q_norm[l], 2).reshape(1, LANES),
            kn=jnp.tile(k_norm[l], 2).reshape(1, LANES), vn=jnp.tile(sgu_v_norm[l], 4).reshape(1, SGU_W),
            bcol=jnp.repeat(b_s[l].T, HEAD_DIM, axis=1),
            cbg=conv_b[l][:D_FF].reshape(1, D_FF), cbv=conv_b[l][D_FF:].reshape(1, D_FF))

    gate_cols, val_cols = (0, D_FF), (D_FF, D_FF)

    saved = []
    cur = xs
    for l in range(DEPTH):
        if l == 1:
            landed = _push_wait(gather1, False, cur, "gather_weights_1_wait")
            full[1] = {n: _slots_to_full(n, t) for n, t in zip(SHARDED, landed)}
        fw, k = full[l], layer_consts(l)
        h1 = _rms_fwd(cur, norm1_first, "rms1_fwd_0") if l == 0 else h_next
        z = _mm(h1, fw["w_in"], name=f"in_proj_{l}")
        a = _pool_fwd(z, k["bdw"], pool_scale[l], f"pool_fwd_{l}")
        b = _attn_fwd(z, cosf, sinf, k["qn"], k["kn"], sinks[l], f"attn_fwd_{l}")
        c = _sgu_fwd(z, w_s[l], k["bcol"], k["vn"], f"sgu_fwd_{l}")
        w_proj_a_l = fw.get("w_proj_a")
        if l == 0:
            landed = _push_wait(gather0, False, c, "gather_rest_0_wait")
            fw.update({n: _slots_to_full(n, t) for n, t in zip(later, landed)})
            gather1 = _push_start([local[1][n] for n in SHARDED], False, "gather_weights_1_start")
            w_proj_a_l = fw["w_proj_a"] + gather1[4][0, 0].astype(BF16)
        x1, merged, h2 = _merge_out_fwd(a, b, c, w_proj_a_l, fw["w_proj_b"], fw["w_proj_c"], z, fw["w_out"], cur, norm2[l],
                                        f"merge_out_fwd_{l}")
        upg = _mm(h2, fw["w_up"], b_n=gate_cols, name=f"up_gate_{l}")
        upv = _mm(h2, fw["w_up"], b_n=val_cols, name=f"up_val_{l}")
        k["cwg"], k["cwv"] = fw["conv_w"][:, :D_FF], fw["conv_w"][:, D_FF:]
        last = l == DEPTH - 1
        x2, act, h_next = _ffn_down_fwd(upg, upv, k["cwg"], k["cwv"], k["cbg"], k["cbv"], fw["w_down"], x1,
                                        None if last else norm1[l + 1], target if last else None, f"ffn_down_fwd_{l}")
        saved.append(dict(x0=cur, h1=h1, z=z, a=a, b=b, c=c, merged=merged, x1=x1, h2=h2, upg=upg, upv=upv, act=act))
        cur = x2

    dcur, loss_tile = cur, h_next
    loss = lax.psum(loss_tile[0, 0], ("x", "y", "c"))

    gsmall = [None] * DEPTH
    small_shapes = {n: wts[n].shape for n in SMALL}

    def slots_of(grads):
        return [_full_to_slots(n, t) for n, t in grads.items()]

    for l in reversed(range(DEPTH)):
        fw, k, sv = full[l], layer_consts(l), saved[l]
        k["cwg"], k["cwv"] = fw["conv_w"][:, :D_FF], fw["conv_w"][:, D_FF:]
        staged = l == 0
        wgrad = functools.partial(_mm, ta=True, out_dtype=BF16)
        w_down_l = fw["w_down"] + exchange1[4][0, 0].astype(BF16) if staged else fw["w_down"]
        dact = _mm(dcur, w_down_l, tb=True, name=f"down_proj_bwd_{l}")
        g_down = wgrad(sv["act"], dcur, name=f"down_proj_wgrad_{l}")
        dg0, dv0, dcg, dcv = _ffn_bwd(sv["upg"], sv["upv"], k["cwg"], k["cwv"], k["cbg"], k["cbv"], dact, f"ffn_bwd_{l}")
        dh2 = _mm(dg0, fw["w_up"], tb=True, b_k=gate_cols, name=f"up_gate_bwd_{l}")
        dh2 = _mm(dv0, fw["w_up"], tb=True, b_k=val_cols, add=dh2, name=f"up_val_bwd_{l}")
        g_up = wgrad(sv["h2"], dg0, out_cols=(0, 2 * D_FF), name=f"up_gate_wgrad_{l}")
        g_up = wgrad(sv["h2"], dv0, out_cols=(D_FF, 2 * D_FF), out_into=g_up, name=f"up_val_wgrad_{l}")
        g_ffn = dict(w_up=g_up, w_down=g_down, conv_w=jnp.concatenate([dcg[0:3], dcv[0:3]], axis=1))
        norm2_l = norm2[l]
        if staged:
            parts1 = dict(zip(SHARDED, _push_wait(exchange1, True, g_up, "exchange_grads_1_wait")))
            exchange_ffn = _push_start(slots_of(g_ffn), True, "exchange_ffn_0_start")
            norm2_l = norm2_l + exchange_ffn[4][0, 0]
        dx1, g_norm2 = _rms_bwd(sv["x1"], norm2_l, dh2, dcur, f"rms2_bwd_{l}")
        dmerged = _mm(dx1, fw["w_out"], tb=True, name=f"out_proj_bwd_{l}")
        g_out = wgrad(sv["merged"], dx1, name=f"out_proj_wgrad_{l}")
        dz, dya, da = _branch_bwd(0, sv["a"], fw["w_proj_a"], sv["z"], dmerged, None, f"branch_a_bwd_{l}")
        dz, dyb, db = _branch_bwd(1, sv["b"], fw["w_proj_b"], sv["z"], dmerged, dz, f"branch_b_bwd_{l}")
        dz, dyc, dc = _branch_bwd(2, sv["c"], fw["w_proj_c"], sv["z"], dmerged, dz, f"branch_c_bwd_{l}")
        g_mix = dict(w_proj_a=wgrad(sv["a"], dya, name=f"proj_a_wgrad_{l}"), w_proj_b=wgrad(sv["b"], dyb, name=f"proj_b_wgrad_{l}"),
                     w_proj_c=wgrad(sv["c"], dyc, name=f"proj_c_wgrad_{l}"), w_out=g_out)
        pool_scale_l = pool_scale[l]
        if staged:
            exchange_mix = _push_start(slots_of(g_mix), True, "exchange_mixer_0_start")
            pool_scale_l = pool_scale_l + exchange_mix[4][0, 0]
        dxp, g_bdw, g_pscale = _pool_bwd(sv["z"], da, k["bdw"], pool_scale_l, f"pool_bwd_{l}")
        dq, dkc, dkp, dvc, dvp, g_qn, g_sink = _attn_bwd(sv["z"], cosf, sinf, k["qn"], k["kn"], sinks[l], db, f"attn_bwd_{l}")
        duv, g_ws, g_bacc, g_vn = _sgu_bwd(sv["z"], w_s[l], k["bcol"], k["vn"], dc, f"sgu_bwd_{l}")
        gsmall[l] = dict(
            q_norm=g_qn[0, :HEAD_DIM], sinks=g_sink[:, 0],
            w_pool=jnp.stack([g_bdw[g * HEAD_DIM:(g + 1) * HEAD_DIM, g * HEAD_DIM:(g + 1) * HEAD_DIM] for g in range(4)]),
            pool_scale=g_pscale[0], sgu_v_norm=g_vn[0, :HEAD_DIM], w_s=g_ws, b_s=g_bacc[:, ::HEAD_DIM].T,
            norm2=g_norm2[0], conv_b=jnp.concatenate([dcg[3], dcv[3]]))
        kn_l = k["kn"]
        if staged:
            early = _pack_small({n: jnp.stack([gsmall[i][n] for i in range(DEPTH)]) for n in SMALL_EARLY}, small_shapes,
                                SMALL_EARLY, SMALL_ROW_TILE)
            gather_early = _push_start([early], False, "gather_small_grads_start")
            kn_l = kn_l + gather_early[4][0, 0]
        dz, g_kn = _kv_post(sv["z"], cosf, sinf, kn_l, dkc, dkp, dvc, dvp, dxp, dq, duv, dz, f"kv_post_{l}")
        g_in = dict(w_in=wgrad(sv["h1"], dz, name=f"in_proj_wgrad_{l}"))
        norm1_l = norm1[l]
        if staged:
            exchange_in = _push_start(slots_of(g_in), True, "exchange_w_in_0_start")
            norm1_l = norm1_l + exchange_in[4][0, 0]
        dh1 = _mm(dz, fw["w_in"], tb=True, name=f"in_proj_bwd_{l}")
        dcur, g_norm1 = _rms_bwd(sv["x0"], norm1_l, dh1, dx1, f"rms1_bwd_{l}")
        if not staged:
            exchange1 = _push_start(slots_of({n: {**g_in, **g_mix, **g_ffn}[n] for n in SHARDED}), True, "exchange_grads_1_start")
        gsmall[l].update(norm1=g_norm1[0], k_norm=g_kn[0, :HEAD_DIM])
    grad_x = dcur[None]

    def update_small(gathered, names, row_tile, name):
        pack = lambda tree: _pack_small(tree, small_shapes, names, row_tile)
        return [_unpack_small(t, small_shapes, names) for t in _adamw_sum(gathered, pack(wts), pack(mom), pack(var), name)]

    late = _pack_small({n: jnp.stack([gsmall[i][n] for i in range(DEPTH)]) for n in SMALL_LATE}, small_shapes, SMALL_LATE, 8)
    gather_late = _push_start([late], False, "gather_late_small_grads_start")

    parts0 = dict(zip(g_ffn, _push_wait(exchange_ffn, True, gather_late[4], "exchange_ffn_0_wait")))
    parts0.update(zip(g_mix, _push_wait(exchange_mix, True, gather_late[4], "exchange_mixer_0_wait")))
    update = lambda n: _adamw_weight([parts0[n], parts1[n]], wts[n], mom[n], var[n], f"adamw_{n}")
    big = {n: update(n) for n in SHARDED[1:]}
    parts0.update(zip(g_in, _push_wait(exchange_in, True, big["w_up"][0], "exchange_w_in_0_wait")))
    big["w_in"] = update("w_in")

    late_all = _push_wait(gather_late, False, big["w_in"][0], "gather_late_small_grads_wait")[0]
    small = update_small(late_all, SMALL_LATE, 8, "adamw_replicated_late")
    early_all = _push_wait(gather_early, False, big["w_in"][0], "gather_small_grads_wait")[0]
    for kind, tree in enumerate(update_small(early_all, SMALL_EARLY, SMALL_ROW_TILE, "adamw_replicated")):
        small[kind].update(tree)

    outs = [loss, grad_x]
    for kind in range(4):
        outs += [small[kind][n] if n in SMALL else big[n][kind] for n in names]
    return tuple(outs)
```

```python
import functools
import math

import jax
import jax.numpy as jnp
import numpy as np
from jax import lax
from jax.experimental import pallas as pl
from jax.experimental.pallas import tpu as pltpu

F32 = jnp.float32
BF16 = jnp.bfloat16

D_MODEL = 1024
DEPTH = 2
HEAD_DIM = 64
N_Q_HEADS = 8
Q_PER_KV = 4
BLOCK = 128
POOL_W = 256
ATTN_W = 512
KV_W = 128
SGU_W = 256
IN_COLS = 4608
GATE_COL0 = 1536
D_FF = 2816
EPS = 1e-6
ROPE_THETA = 10000.0
N_DEV = 8
LANES = 128
HALO_POOL = 16
HALO_CONV = 8

ADAM_LR = 0.001
ADAM_B1 = 0.9
ADAM_B2 = 0.999
ADAM_EPS = 1e-08
ADAM_WD = 0.01
ADAM_STEP = 10

VMEM_LIMIT = 48 * 1024 * 1024
MM_VMEM_BUDGET = 40 * 1024 * 1024

SHARDED = ("w_in", "w_proj_a", "w_proj_b", "w_proj_c", "w_out", "w_up", "w_down", "conv_w")
ROW_SHARDED = ("w_out", "w_down")
SMALL_ROW_TILE = 256
SMALL = ("norm1", "q_norm", "k_norm", "sinks", "w_pool", "pool_scale", "sgu_v_norm", "w_s", "b_s", "norm2", "conv_b")
SMALL_LATE = ("norm1", "k_norm")
SMALL_EARLY = tuple(n for n in SMALL if n not in SMALL_LATE)

_GELU_C = math.sqrt(2.0 / math.pi)
_GELU_A = 0.044715


def _params(*sem):
    return pltpu.CompilerParams(dimension_semantics=sem, vmem_limit_bytes=VMEM_LIMIT)


def _tile(n, prefs):
    for t in prefs:
        if t <= n and n % t == 0:
            return t
    return n


def _head_mean_matrix():
    r = lax.broadcasted_iota(jnp.int32, (LANES, LANES), 0)
    c = lax.broadcasted_iota(jnp.int32, (LANES, LANES), 1)
    return jnp.where((r >= HEAD_DIM) == (c >= HEAD_DIM), 1.0 / HEAD_DIM, 0.0).astype(BF16)


def _head_mean(v, bd):
    hi = v.astype(BF16)
    rest = v - hi.astype(F32)
    mid = rest.astype(BF16)
    lo = (rest - mid.astype(F32)).astype(BF16)
    mm = lambda p: jnp.dot(p, bd, preferred_element_type=F32)
    return mm(hi) + (mm(mid) + mm(lo))


def _rot_half(t):
    lane = lax.broadcasted_iota(jnp.int32, t.shape, 1)
    return jnp.where((lane & 32) == 0, pltpu.roll(t, LANES - 32, 1), pltpu.roll(t, 32, 1))


def _norm_rope(t, gn, cosf, sinf, bd):
    r = lax.rsqrt(_head_mean(t * t, bd) + EPS)
    n = t * r
    y = n * gn
    return y * cosf + _rot_half(y) * sinf, n, r


def _norm_rope_bwd(d, t, n, r, gn, cosf, sinf, bd):
    dy = d * cosf + _rot_half(d * sinf)
    dgn = jnp.sum(dy * n, axis=0, keepdims=True)
    u = dy * gn
    dt = r * u - t * (r * r * r) * _head_mean(t * u, bd)
    return dt, dgn


def _gelu(x):
    t = jnp.tanh(_GELU_C * (x + _GELU_A * (x * x * x)))
    return 0.5 * x * (1.0 + t), t


def _gelu_grad(x, t):
    return 0.5 * (1.0 + t) + 0.5 * x * (1.0 - t * t) * (_GELU_C * (1.0 + 3.0 * _GELU_A * x * x))


def _sigmoid(x):
    return jax.nn.sigmoid(x)


def _dot(a, b, ca=1, cb=0):
    return lax.dot_general(a.astype(BF16), b.astype(BF16), (((ca,), (cb,)), ((), ())), preferred_element_type=F32)


def _mm(a, b, *, ta=False, tb=False, add=None, out_dtype=F32, name, b_n=None, b_k=None, out_cols=None, out_into=None,
        rms=None):
    m = a.shape[1] if ta else a.shape[0]
    k = a.shape[0] if ta else a.shape[1]
    n = b_n[1] if b_n else (b.shape[0] if tb else b.shape[1])
    tn = _tile(n, (1024, 1152, 1408, 512, 256, 128))
    has_add = add is not None
    has_rms = rms is not None
    assert not has_rms or (tn == n and out_into is None)
    fits = []
    for tm in (2048, 1024, 1408, 512, 256, 128):
        for tk in (k, 4608, 2816, 2048, 1408, 1152, 1024, 512, 256, 128):
            if tm <= m and m % tm == 0 and tk <= k and k % tk == 0:
                need = (2 * (tm * tk * a.dtype.itemsize + tk * tn * b.dtype.itemsize) + 2 * tm * tn * jnp.dtype(out_dtype).itemsize
                        + 2 * tm * tn * 4 * (has_add + 2 * has_rms) + tm * tn * 4 * (tk < k))
                if need <= MM_VMEM_BUDGET:
                    fits.append((k // tk, -tm, tm, tk))
    if fits:
        _, _, tm, tk = min(fits)
    else:
        tm, tk = _tile(m, (256, 128)), _tile(k, (512, 256, 128))
    nk = k // tk
    n0 = b_n[0] // tn if b_n else 0
    k0 = b_k[0] // tk if b_k else 0
    o0, n_out = (out_cols[0] // tn, out_cols[1]) if out_cols else (0, n)
    n_in = 2 + has_add + (out_into is not None) + 3 * has_rms

    def body(*refs):
        a_ref, b_ref = refs[0], refs[1]
        add_ref = refs[2] if has_add else None
        o_ref = refs[n_in]

        def finish(r):
            if has_add:
                r = r + add_ref[...]
            if has_rms:
                x_ref, g_ref, dres_ref = refs[n_in - 3:n_in]
                dg_ref = refs[n_in + 1]
                x = x_ref[...]
                rr = lax.rsqrt(jnp.mean(x * x, axis=-1, keepdims=True) + EPS)
                u = r * g_ref[...]
                part = jnp.sum(r * (x * rr), axis=0, keepdims=True)
                r = dres_ref[...] + (rr * u - x * (rr * rr * rr) * jnp.mean(x * u, axis=-1, keepdims=True))

                @pl.when(pl.program_id(0) == 0)
                def _():
                    dg_ref[...] = part

                @pl.when(pl.program_id(0) > 0)
                def _():
                    dg_ref[...] += part

            o_ref[...] = r.astype(out_dtype)

        if nk == 1:
            finish(_dot(a_ref[...], b_ref[...], 0 if ta else 1, 1 if tb else 0))
        else:
            acc_ref = refs[-1]
            kk = pl.program_id(2)

            @pl.when(kk == 0)
            def _():
                acc_ref[...] = jnp.zeros_like(acc_ref)

            acc_ref[...] += _dot(a_ref[...], b_ref[...], 0 if ta else 1, 1 if tb else 0)

            @pl.when(kk == nk - 1)
            def _():
                finish(acc_ref[...])

    a_spec = pl.BlockSpec((tk, tm), lambda i, j, kk: (kk, i)) if ta else pl.BlockSpec((tm, tk), lambda i, j, kk: (i, kk))
    if tb:
        b_spec = pl.BlockSpec((tn, tk), lambda i, j, kk: (j + n0, kk + k0))
    else:
        b_spec = pl.BlockSpec((tk, tn), lambda i, j, kk: (kk + k0, j + n0))
    tile = pl.BlockSpec((tm, tn), lambda i, j, kk: (i, j))
    vec = pl.BlockSpec((1, tn), lambda i, j, kk: (0, j))
    in_specs = [a_spec, b_spec] + ([tile] if has_add else [])
    args = (a, b) + ((add,) if has_add else ())
    if out_into is not None:
        in_specs.append(pl.BlockSpec(memory_space=pl.ANY))
        args += (out_into,)
    out_specs = [pl.BlockSpec((tm, tn), lambda i, j, kk: (i, j + o0))]
    out_shape = [jax.ShapeDtypeStruct((m, n_out), out_dtype)]
    if has_rms:
        in_specs += [tile, vec, tile]
        args += (rms[0], rms[1].reshape(1, n), rms[2])
        out_specs.append(vec)
        out_shape.append(jax.ShapeDtypeStruct((1, n), F32))
    out = pl.pallas_call(
        body, name=name, grid=(m // tm, n // tn, nk), in_specs=in_specs, out_specs=out_specs, out_shape=out_shape,
        scratch_shapes=[pltpu.VMEM((tm, tn), F32)] if nk > 1 else [],
        input_output_aliases={n_in - 1: 0} if out_into is not None else {},
        compiler_params=_params("arbitrary" if has_rms else "parallel", "parallel", "arbitrary"),
    )(*args)
    return out if has_rms else out[0]


def _rms_fwd(x, g, name):
    s, d = x.shape
    tr = _tile(s, (512, 256, 128))

    def body(x_ref, g_ref, h_ref):
        xv = x_ref[...]
        r = lax.rsqrt(jnp.mean(xv * xv, axis=-1, keepdims=True) + EPS)
        h_ref[...] = ((xv * r) * g_ref[...]).astype(BF16)

    return pl.pallas_call(
        body, name=name, grid=(s // tr,),
        in_specs=[pl.BlockSpec((tr, d), lambda i: (i, 0)), pl.BlockSpec((1, d), lambda i: (0, 0))],
        out_specs=pl.BlockSpec((tr, d), lambda i: (i, 0)),
        out_shape=jax.ShapeDtypeStruct((s, d), BF16), compiler_params=_params("parallel"),
    )(x, g.reshape(1, d))


def _pool_lane_select(lane, v2, v4, v8, v16):
    return jnp.where(lane < 64, v2, jnp.where(lane < 128, v4, jnp.where(lane < 192, v8, v16)))


def _pool_diff(xc, xp, row0):
    n = BLOCK + HALO_POOL
    cat = jnp.concatenate([xp, xc], axis=0)
    s2 = cat + pltpu.roll(cat, 1, 0)
    s4 = s2 + pltpu.roll(s2, 2, 0)
    s8 = s4 + pltpu.roll(s4, 4, 0)
    s16 = s8 + pltpu.roll(s8, 8, 0)
    lane = lax.broadcasted_iota(jnp.int32, (n, POOL_W), 1)
    wsum = _pool_lane_select(lane, s2, s4, s8, s16)[HALO_POOL:]
    return wsum / _pool_count(row0, BLOCK) - xc


def _pool_count(row0, rows):
    lane = lax.broadcasted_iota(jnp.int32, (rows, POOL_W), 1)
    t = lax.broadcasted_iota(jnp.int32, (rows, POOL_W), 0) + row0
    return jnp.minimum(t + 1, _pool_lane_select(lane, 2, 4, 8, 16)).astype(F32)


def _pool_fwd(z, bdw, scale, name):
    s = z.shape[0]
    nb = s // BLOCK

    def body(xc_ref, xp_ref, w_ref, sc_ref, a_ref):
        i = pl.program_id(0)
        xp = jnp.where(i > 0, xp_ref[...], 0.0)
        diff = _pool_diff(xc_ref[...], xp, i * BLOCK)
        a_ref[...] = (_dot(diff, w_ref[...]) * sc_ref[...]).astype(BF16)

    return pl.pallas_call(
        body, name=name, grid=(nb,),
        in_specs=[pl.BlockSpec((BLOCK, POOL_W), lambda i: (i, 0)),
                  pl.BlockSpec((HALO_POOL, POOL_W), lambda i: (jnp.maximum(i * (BLOCK // HALO_POOL) - 1, 0), 0)),
                  pl.BlockSpec((POOL_W, POOL_W), lambda i: (0, 0)),
                  pl.BlockSpec((1, POOL_W), lambda i: (0, 0))],
        out_specs=pl.BlockSpec((BLOCK, POOL_W), lambda i: (i, 0)),
        out_shape=jax.ShapeDtypeStruct((s, POOL_W), BF16), compiler_params=_params("parallel"),
    )(z, z, bdw, scale.reshape(1, POOL_W))


def _pool_bwd(z, da, bdw, scale, name):
    s = z.shape[0]
    nb = s // BLOCK
    per = BLOCK // HALO_POOL
    n = BLOCK + HALO_POOL

    def body(xc_ref, xp_ref, dac_ref, dan_ref, w_ref, sc_ref, dx_ref, dw_ref, dsc_ref):
        i = pl.program_id(0)
        xp = jnp.where(i > 0, xp_ref[...], 0.0)
        diff = _pool_diff(xc_ref[...], xp, i * BLOCK)
        mixed = _dot(diff, w_ref[...])
        dac = dac_ref[...]
        dan = jnp.where(i < nb - 1, dan_ref[...], 0.0)
        dmix = jnp.concatenate([dac, dan], axis=0) * sc_ref[...]
        ddiff = _dot(dmix, w_ref[...], 1, 1)
        e = ddiff / _pool_count(i * BLOCK, n)
        f2 = e + pltpu.roll(e, n - 1, 0)
        f4 = f2 + pltpu.roll(f2, n - 2, 0)
        f8 = f4 + pltpu.roll(f4, n - 4, 0)
        f16 = f8 + pltpu.roll(f8, n - 8, 0)
        lane = lax.broadcasted_iota(jnp.int32, (n, POOL_W), 1)
        back = _pool_lane_select(lane, f2, f4, f8, f16)
        dx_ref[...] = (back[:BLOCK] - ddiff[:BLOCK]).astype(BF16)
        dw = _dot(diff, dmix[:BLOCK], 0, 0)
        dsc = jnp.sum(dac * mixed, axis=0, keepdims=True)

        @pl.when(i == 0)
        def _():
            dw_ref[...] = dw
            dsc_ref[...] = dsc

        @pl.when(i > 0)
        def _():
            dw_ref[...] += dw
            dsc_ref[...] += dsc

    blk = pl.BlockSpec((BLOCK, POOL_W), lambda i: (i, 0))
    return pl.pallas_call(
        body, name=name, grid=(nb,),
        in_specs=[blk, pl.BlockSpec((HALO_POOL, POOL_W), lambda i: (jnp.maximum(i * per - 1, 0), 0)),
                  blk, pl.BlockSpec((HALO_POOL, POOL_W), lambda i: (jnp.minimum((i + 1) * per, nb * per - 1), 0)),
                  pl.BlockSpec((POOL_W, POOL_W), lambda i: (0, 0)), pl.BlockSpec((1, POOL_W), lambda i: (0, 0))],
        out_specs=[blk, pl.BlockSpec((POOL_W, POOL_W), lambda i: (0, 0)), pl.BlockSpec((1, POOL_W), lambda i: (0, 0))],
        out_shape=[jax.ShapeDtypeStruct((s, POOL_W), BF16), jax.ShapeDtypeStruct((POOL_W, POOL_W), F32),
                   jax.ShapeDtypeStruct((1, POOL_W), F32)],
        compiler_params=_params("arbitrary"),
    )(z, z, da, da, bdw, scale.reshape(1, POOL_W))


def _attn_setup(zc_ref, zp_ref, cc_ref, cp_ref, sc_ref, sp_ref, qn_ref, kn_ref, bd):
    q = []
    for j in range(ATTN_W // LANES):
        t = zc_ref[:, POOL_W + j * LANES:POOL_W + (j + 1) * LANES]
        q.append((t,) + _norm_rope(t, qn_ref[...], cc_ref[...], sc_ref[...], bd))
    kc_raw = zc_ref[:, POOL_W + ATTN_W:POOL_W + ATTN_W + KV_W]
    kc = _norm_rope(kc_raw, kn_ref[...], cc_ref[...], sc_ref[...], bd)[0]
    kp = _norm_rope(zp_ref[:, :KV_W], kn_ref[...], cp_ref[...], sp_ref[...], bd)[0]
    kband = jnp.concatenate([kp, kc], axis=0).astype(BF16)
    vband = jnp.concatenate([zp_ref[:, KV_W:], zc_ref[:, POOL_W + ATTN_W + KV_W:POOL_W + ATTN_W + 2 * KV_W]], axis=0).astype(BF16)
    return q, kband, vband


MASKED = -1e30


def _window_bias(first_block):
    row = np.arange(Q_PER_KV * BLOCK)[:, None] % BLOCK
    col = np.arange(2 * BLOCK)[None, :]
    dist = row + BLOCK - col
    inside = (dist >= 0) & (dist < BLOCK) & ((col >= BLOCK) | (not first_block))
    return jnp.asarray(np.where(inside, 0.0, MASKED), F32)


def _stack_heads(tiles, kvh):
    return jnp.concatenate([_to_kv_lanes(tiles[h // 2], h) for h in range(kvh * Q_PER_KV, (kvh + 1) * Q_PER_KV)], axis=0)


def _unstack_heads(stacked, kvh, tiles):
    for g in range(Q_PER_KV):
        h = kvh * Q_PER_KV + g
        t = _from_kv_lanes(stacked[g * BLOCK:(g + 1) * BLOCK], h)
        tiles[h // 2] = t if tiles[h // 2] is None else tiles[h // 2] + t


def _sink_column(sink_ref, kvh):
    grp = lax.broadcasted_iota(jnp.int32, (Q_PER_KV * BLOCK, 1), 0) >> 7
    s = [sink_ref[kvh * Q_PER_KV + g] for g in range(Q_PER_KV)]
    return jnp.where(grp == 0, s[0], jnp.where(grp == 1, s[1], jnp.where(grp == 2, s[2], s[3])))


def _to_kv_lanes(t, h):
    kvh = h // Q_PER_KV
    if (h % 2) != kvh:
        t = pltpu.roll(t, HEAD_DIM, 1)
    lane = lax.broadcasted_iota(jnp.int32, t.shape, 1)
    return jnp.where((lane >= HEAD_DIM) == (kvh == 1), t, 0.0)


def _from_kv_lanes(t, h):
    kvh = h // Q_PER_KV
    lane = lax.broadcasted_iota(jnp.int32, t.shape, 1)
    t = jnp.where((lane >= HEAD_DIM) == (kvh == 1), t, 0.0)
    if (h % 2) != kvh:
        t = pltpu.roll(t, HEAD_DIM, 1)
    return t


def _attn_probs(qh, kband, bias, sink):
    sc = _dot(qh, kband, 1, 1) * (HEAD_DIM ** -0.5) + bias
    m = jnp.maximum(jnp.max(sc, axis=1, keepdims=True), sink)
    p = jnp.exp(sc - m)
    psink = jnp.exp(sink - m)
    den = jnp.sum(p, axis=1, keepdims=True) + psink
    return p / den, psink / den


def _attn_specs(nb):
    cur = lambda i: (i, 0)
    prev = lambda i: (jnp.maximum(i - 1, 0), 0)
    tab = lambda f: pl.BlockSpec((BLOCK, LANES), f)
    vec = pl.BlockSpec((1, LANES), lambda i: (0, 0))
    return [pl.BlockSpec((BLOCK, 1024), cur),
            pl.BlockSpec((BLOCK, 2 * KV_W), lambda i: (jnp.maximum(i - 1, 0), 3)),
            tab(cur), tab(prev), tab(cur), tab(prev), vec, vec,
            pl.BlockSpec(memory_space=pltpu.SMEM)] + [pl.BlockSpec((Q_PER_KV * BLOCK, 2 * BLOCK), lambda i: (0, 0))] * 2


def _attn_fwd(z, cosf, sinf, qn, kn, sinks, name):
    s = z.shape[0]
    nb = s // BLOCK

    def body(zc_ref, zp_ref, cc_ref, cp_ref, sc_ref, sp_ref, qn_ref, kn_ref, sink_ref, bias_ref, bias0_ref, o_ref):
        i = pl.program_id(0)
        bd = _head_mean_matrix()
        q, kband, vband = _attn_setup(zc_ref, zp_ref, cc_ref, cp_ref, sc_ref, sp_ref, qn_ref, kn_ref, bd)
        mask = jnp.where(i > 0, bias_ref[...], bias0_ref[...])
        out = [None] * (ATTN_W // LANES)
        for kvh in range(N_Q_HEADS // Q_PER_KV):
            qs = _stack_heads([t[1] for t in q], kvh)
            probs, _ = _attn_probs(qs, kband, mask, _sink_column(sink_ref, kvh))
            _unstack_heads(_dot(probs, vband), kvh, out)
        for j, o in enumerate(out):
            o_ref[:, j * LANES:(j + 1) * LANES] = o.astype(BF16)

    return pl.pallas_call(
        body, name=name, grid=(nb,), in_specs=_attn_specs(nb),
        out_specs=pl.BlockSpec((BLOCK, ATTN_W), lambda i: (i, 0)),
        out_shape=jax.ShapeDtypeStruct((s, ATTN_W), BF16), compiler_params=_params("parallel"),
    )(z, z, cosf, cosf, sinf, sinf, qn, kn, sinks, _window_bias(False), _window_bias(True))


def _attn_bwd(z, cosf, sinf, qn, kn, sinks, d_out, name):
    s = z.shape[0]
    nb = s // BLOCK
    nt = ATTN_W // LANES

    def body(zc_ref, zp_ref, cc_ref, cp_ref, sc_ref, sp_ref, qn_ref, kn_ref, sink_ref, bias_ref, bias0_ref, do_ref,
             dq_ref, dkc_ref, dkp_ref, dvc_ref, dvp_ref, dqn_ref, dsink_ref):
        i = pl.program_id(0)
        bd = _head_mean_matrix()
        q, kband, vband = _attn_setup(zc_ref, zp_ref, cc_ref, cp_ref, sc_ref, sp_ref, qn_ref, kn_ref, bd)
        mask = jnp.where(i > 0, bias_ref[...], bias0_ref[...])

        @pl.when(i == 0)
        def _():
            dqn_ref[...] = jnp.zeros_like(dqn_ref)
            dsink_ref[...] = jnp.zeros_like(dsink_ref)

        dq = [None] * nt
        dk = jnp.zeros((2 * BLOCK, KV_W), F32)
        dv = jnp.zeros((2 * BLOCK, KV_W), F32)
        d_tiles = [do_ref[:, j * LANES:(j + 1) * LANES] for j in range(nt)]
        for kvh in range(N_Q_HEADS // Q_PER_KV):
            qs = _stack_heads([t[1] for t in q], kvh)
            probs, psink = _attn_probs(qs, kband, mask, _sink_column(sink_ref, kvh))
            dos = _stack_heads(d_tiles, kvh)
            dp = _dot(dos, vband, 1, 1)
            delta = jnp.sum(dp * probs, axis=1, keepdims=True)
            ds = (probs * (dp - delta)) * (HEAD_DIM ** -0.5)
            dsink = -psink * delta
            for g in range(Q_PER_KV):
                h = kvh * Q_PER_KV + g
                dsink_ref[h:h + 1, :] += jnp.broadcast_to(jnp.sum(dsink[g * BLOCK:(g + 1) * BLOCK], axis=0, keepdims=True), (1, LANES))
            _unstack_heads(_dot(ds, kband), kvh, dq)
            dk = dk + _dot(ds, qs, 0, 0)
            dv = dv + _dot(probs, dos, 0, 0)
        dgn = jnp.zeros((1, LANES), F32)
        for j in range(nt):
            t, _, n, r = q[j]
            dt, g = _norm_rope_bwd(dq[j], t, n, r, qn_ref[...], cc_ref[...], sc_ref[...], bd)
            dq_ref[:, j * LANES:(j + 1) * LANES] = dt.astype(BF16)
            dgn = dgn + g
        dqn_ref[...] += jnp.broadcast_to(dgn, (8, LANES))
        dkp_ref[...] = dk[:BLOCK]
        dkc_ref[...] = dk[BLOCK:]
        dvp_ref[...] = dv[:BLOCK]
        dvc_ref[...] = dv[BLOCK:]

        @pl.when(i == nb - 1)
        def _():
            acc = dqn_ref[...]
            dqn_ref[...] = acc + pltpu.roll(acc, HEAD_DIM, 1)

    blk = lambda w: pl.BlockSpec((BLOCK, w), lambda i: (i, 0))
    acc = pl.BlockSpec((8, LANES), lambda i: (0, 0))
    kv = jax.ShapeDtypeStruct((s, KV_W), F32)
    return pl.pallas_call(
        body, name=name, grid=(nb,), in_specs=_attn_specs(nb) + [blk(ATTN_W)],
        out_specs=[blk(ATTN_W), blk(KV_W), blk(KV_W), blk(KV_W), blk(KV_W), acc, acc],
        out_shape=[jax.ShapeDtypeStruct((s, ATTN_W), BF16), kv, kv, kv, kv,
                   jax.ShapeDtypeStruct((8, LANES), F32), jax.ShapeDtypeStruct((8, LANES), F32)],
        compiler_params=_params("arbitrary"),
    )(z, z, cosf, cosf, sinf, sinf, qn, kn, sinks, _window_bias(False), _window_bias(True), d_out)


def _kv_post(z, cosf, sinf, kn, dkc, dkp, dvc, dvp, dxp, dq, duv, dz, name):
    s = z.shape[0]
    nb = s // BLOCK

    def body(zk_ref, c_ref, s_ref, kn_ref, dkc_ref, dkp_ref, dvc_ref, dvp_ref, dxp_ref, dq_ref, duv_ref, dz_in,
             dz_ref, dkn_ref):
        j = pl.program_id(0)
        bd = _head_mean_matrix()
        last = j == nb - 1
        d = dkc_ref[...] + jnp.where(last, 0.0, dkp_ref[...])
        t = zk_ref[:, :KV_W]
        _, n, r = _norm_rope(t, kn_ref[...], c_ref[...], s_ref[...], bd)
        dt, g = _norm_rope_bwd(d, t, n, r, kn_ref[...], c_ref[...], s_ref[...], bd)
        dvv = dvc_ref[...] + jnp.where(last, 0.0, dvp_ref[...])
        dz_ref[:, 0:POOL_W] = dxp_ref[...]
        dz_ref[:, POOL_W:POOL_W + ATTN_W] = dq_ref[...]
        dz_ref[:, POOL_W + ATTN_W:POOL_W + ATTN_W + KV_W] = dt.astype(BF16)
        dz_ref[:, POOL_W + ATTN_W + KV_W:POOL_W + ATTN_W + 2 * KV_W] = dvv.astype(BF16)
        dz_ref[:, POOL_W + ATTN_W + 2 * KV_W:GATE_COL0] = duv_ref[...]

        @pl.when(j == 0)
        def _():
            dkn_ref[...] = jnp.zeros_like(dkn_ref)

        dkn_ref[...] += jnp.broadcast_to(g, (8, LANES))

        @pl.when(last)
        def _():
            acc = dkn_ref[...]
            dkn_ref[...] = acc + pltpu.roll(acc, HEAD_DIM, 1)

    cur = lambda w: pl.BlockSpec((BLOCK, w), lambda j: (j, 0))
    nxt = pl.BlockSpec((BLOCK, KV_W), lambda j: (jnp.minimum(j + 1, nb - 1), 0))
    vec = pl.BlockSpec((1, LANES), lambda j: (0, 0))
    return pl.pallas_call(
        body, name=name, grid=(nb,),
        in_specs=[pl.BlockSpec((BLOCK, 2 * KV_W), lambda j: (j, 3)), cur(LANES), cur(LANES), vec,
                  cur(KV_W), nxt, cur(KV_W), nxt, cur(POOL_W), cur(ATTN_W), cur(2 * SGU_W),
                  pl.BlockSpec(memory_space=pl.ANY)],
        out_specs=[pl.BlockSpec((BLOCK, GATE_COL0), lambda j: (j, 0)), pl.BlockSpec((8, LANES), lambda j: (0, 0))],
        out_shape=[jax.ShapeDtypeStruct(dz.shape, dz.dtype), jax.ShapeDtypeStruct((8, LANES), F32)],
        input_output_aliases={11: 0}, compiler_params=_params("arbitrary"),
    )(z, cosf, sinf, kn, dkc, dkp, dvc, dvp, dxp, dq, duv, dz)


def _sgu_setup(z_ref, ws_ref, vn_ref, bd):
    us = z_ref[:, :SGU_W]
    vs = z_ref[:, SGU_W:]
    ug, tu = _gelu(us)
    gv, tv = _gelu(vs)
    rr = jnp.concatenate([lax.rsqrt(_head_mean(gv[:, k * LANES:(k + 1) * LANES] ** 2, bd) + EPS) for k in range(2)], axis=1)
    vg = (gv * rr) * vn_ref[...]
    tril = lax.broadcasted_iota(jnp.int32, (BLOCK, BLOCK), 0) >= lax.broadcasted_iota(jnp.int32, (BLOCK, BLOCK), 1)
    w = [jnp.where(tril, ws_ref[g], 0.0).astype(BF16) for g in range(4)]
    return us, vs, ug, tu, gv, tv, rr, vg, w, tril


def _group_select(parts):
    lane = lax.broadcasted_iota(jnp.int32, parts[0].shape, 1)
    return _pool_lane_select(lane, *parts)


def _sgu_fwd(z, ws, bcol, vn, name):
    s = z.shape[0]
    nb = s // BLOCK

    def body(z_ref, ws_ref, b_ref, vn_ref, c_ref):
        bd = _head_mean_matrix()
        _, _, ug, _, _, _, _, vg, w, _ = _sgu_setup(z_ref, ws_ref, vn_ref, bd)
        sg = _group_select([_dot(w[g], vg) for g in range(4)]) + b_ref[...]
        c_ref[...] = (ug * sg).astype(BF16)

    return pl.pallas_call(
        body, name=name, grid=(nb,),
        in_specs=[pl.BlockSpec((BLOCK, 2 * SGU_W), lambda i: (i, 2)), pl.BlockSpec((4, BLOCK, BLOCK), lambda i: (0, 0, 0)),
                  pl.BlockSpec((BLOCK, SGU_W), lambda i: (0, 0)), pl.BlockSpec((1, SGU_W), lambda i: (0, 0))],
        out_specs=pl.BlockSpec((BLOCK, SGU_W), lambda i: (i, 0)),
        out_shape=jax.ShapeDtypeStruct((s, SGU_W), BF16), compiler_params=_params("parallel"),
    )(z, ws, bcol, vn)


def _sgu_bwd(z, ws, bcol, vn, dc, name):
    s = z.shape[0]
    nb = s // BLOCK

    def body(z_ref, ws_ref, b_ref, vn_ref, dc_ref, duv_ref, dws_ref, db_ref, dvn_ref):
        i = pl.program_id(0)
        bd = _head_mean_matrix()
        us, vs, ug, tu, gv, tv, rr, vg, w, tril = _sgu_setup(z_ref, ws_ref, vn_ref, bd)
        sg = _group_select([_dot(w[g], vg) for g in range(4)]) + b_ref[...]
        dcv = dc_ref[...]
        dug = dcv * sg
        dsg = dcv * ug
        lane = lax.broadcasted_iota(jnp.int32, dsg.shape, 1)

        @pl.when(i == 0)
        def _():
            dws_ref[...] = jnp.zeros_like(dws_ref)
            db_ref[...] = jnp.zeros_like(db_ref)
            dvn_ref[...] = jnp.zeros_like(dvn_ref)

        for g in range(4):
            dsg_g = jnp.where((lane >= g * HEAD_DIM) & (lane < (g + 1) * HEAD_DIM), dsg, 0.0)
            dws_ref[g] += jnp.where(tril, _dot(dsg_g, vg, 1, 1), 0.0)
        dvg = _group_select([_dot(w[g], dsg, 0, 0) for g in range(4)])
        db_ref[...] += dsg
        n = gv * rr
        part = jnp.sum(dvg * n, axis=0, keepdims=True)
        dvn_ref[...] += jnp.broadcast_to(part[:, :LANES] + part[:, LANES:], (8, LANES))
        u = dvg * vn_ref[...]
        tu_ = gv * u
        mean = jnp.concatenate([_head_mean(tu_[:, k * LANES:(k + 1) * LANES], bd) for k in range(2)], axis=1)
        dgv = rr * u - gv * (rr * rr * rr) * mean
        duv_ref[:, :SGU_W] = (dug * _gelu_grad(us, tu)).astype(BF16)
        duv_ref[:, SGU_W:] = (dgv * _gelu_grad(vs, tv)).astype(BF16)

        @pl.when(i == nb - 1)
        def _():
            acc = dvn_ref[...]
            dvn_ref[...] = acc + pltpu.roll(acc, HEAD_DIM, 1)
            for k in range(2):
                db_ref[:, k * LANES:(k + 1) * LANES] = _head_mean(db_ref[:, k * LANES:(k + 1) * LANES], bd) * float(HEAD_DIM)

    return pl.pallas_call(
        body, name=name, grid=(nb,),
        in_specs=[pl.BlockSpec((BLOCK, 2 * SGU_W), lambda i: (i, 2)), pl.BlockSpec((4, BLOCK, BLOCK), lambda i: (0, 0, 0)),
                  pl.BlockSpec((BLOCK, SGU_W), lambda i: (0, 0)), pl.BlockSpec((1, SGU_W), lambda i: (0, 0)),
                  pl.BlockSpec((BLOCK, SGU_W), lambda i: (i, 0))],
        out_specs=[pl.BlockSpec((BLOCK, 2 * SGU_W), lambda i: (i, 0)), pl.BlockSpec((4, BLOCK, BLOCK), lambda i: (0, 0, 0)),
                   pl.BlockSpec((BLOCK, SGU_W), lambda i: (0, 0)), pl.BlockSpec((8, LANES), lambda i: (0, 0))],
        out_shape=[jax.ShapeDtypeStruct((s, 2 * SGU_W), BF16), jax.ShapeDtypeStruct((4, BLOCK, BLOCK), F32),
                   jax.ShapeDtypeStruct((BLOCK, SGU_W), F32), jax.ShapeDtypeStruct((8, LANES), F32)],
        compiler_params=_params("arbitrary"),
    )(z, ws, bcol, vn, dc)


MERGE_TN = 512
MERGE_TM = 1024
MERGE_OUT_TM = 512


def _rms_rows(x, g):
    r = lax.rsqrt(jnp.mean(x * x, axis=-1, keepdims=True) + EPS)
    return ((x * r) * g).astype(BF16)


def _merge_out_fwd(a, b, c, wpa, wpb, wpc, z, w_out, res, gain, name):
    s = z.shape[0]
    tm = _tile(s, (MERGE_OUT_TM, BLOCK))
    gate0 = GATE_COL0 // MERGE_TN
    nn = D_MODEL // MERGE_TN

    def body(a_ref, b_ref, c_ref, wa_ref, wb_ref, wc_ref, g0_ref, g1_ref, g2_ref, wo_ref, res_ref, gain_ref,
             o_ref, m_ref, h_ref, acc_ref):
        n = pl.program_id(1)
        r = _sigmoid(g0_ref[...]) * _dot(a_ref[...], wa_ref[...])
        r = r + _sigmoid(g1_ref[...]) * _dot(b_ref[...], wb_ref[...])
        r = r + _sigmoid(g2_ref[...]) * _dot(c_ref[...], wc_ref[...])
        merged = r.astype(BF16)
        m_ref[...] = merged

        @pl.when(n == 0)
        def _():
            acc_ref[...] = jnp.zeros_like(acc_ref)

        acc_ref[...] += jnp.dot(merged, wo_ref[...], preferred_element_type=F32)

        @pl.when(n == nn - 1)
        def _():
            x = acc_ref[...] + res_ref[...]
            o_ref[...] = x
            h_ref[...] = _rms_rows(x, gain_ref[...])

    x_spec = lambda w: pl.BlockSpec((tm, w), lambda i, n: (i, 0))
    w_spec = lambda w: pl.BlockSpec((w, MERGE_TN), lambda i, n: (0, n))
    g_spec = lambda br: pl.BlockSpec((tm, MERGE_TN), lambda i, n: (i, gate0 + 2 * br + n))
    row = pl.BlockSpec((tm, D_MODEL), lambda i, n: (i, 0))
    return pl.pallas_call(
        body, name=name, grid=(s // tm, nn),
        in_specs=[x_spec(POOL_W), x_spec(ATTN_W), x_spec(SGU_W), w_spec(POOL_W), w_spec(ATTN_W), w_spec(SGU_W),
                  g_spec(0), g_spec(1), g_spec(2), pl.BlockSpec((MERGE_TN, D_MODEL), lambda i, n: (n, 0)), row,
                  pl.BlockSpec((1, D_MODEL), lambda i, n: (0, 0))],
        out_specs=[row, pl.BlockSpec((tm, MERGE_TN), lambda i, n: (i, n)), row],
        out_shape=[jax.ShapeDtypeStruct((s, D_MODEL), F32), jax.ShapeDtypeStruct((s, D_MODEL), BF16),
                   jax.ShapeDtypeStruct((s, D_MODEL), BF16)],
        scratch_shapes=[pltpu.VMEM((tm, D_MODEL), F32)], compiler_params=_params("parallel", "arbitrary"),
    )(a, b, c, wpa, wpb, wpc, z, z, z, w_out, res, gain.reshape(1, D_MODEL))


def _branch_bwd(br, xb, wp, z, dm, dz, name):
    s = z.shape[0]
    kb = xb.shape[1]
    tm = _tile(s, (MERGE_TM, BLOCK))
    gate0 = GATE_COL0 // MERGE_TN
    aliased = dz is not None

    def body(*refs):
        x_ref, w_ref, g_ref, dm_ref = refs[:4]
        dz_ref, dy_ref, dx_ref = refs[-3:]
        n = pl.program_id(1)
        y = _dot(x_ref[...], w_ref[...])
        sg = _sigmoid(g_ref[...])
        dmv = dm_ref[...]
        dy = (dmv * sg).astype(BF16)
        dy_ref[...] = dy
        dz_ref[...] = ((dmv * y) * (sg * (1.0 - sg))).astype(BF16)
        dx = _dot(dy, w_ref[...], 1, 1)

        @pl.when(n == 0)
        def _():
            dx_ref[...] = dx

        @pl.when(n > 0)
        def _():
            dx_ref[...] += dx

    in_specs = [pl.BlockSpec((tm, kb), lambda i, n: (i, 0)), pl.BlockSpec((kb, MERGE_TN), lambda i, n: (0, n)),
                pl.BlockSpec((tm, MERGE_TN), lambda i, n: (i, gate0 + 2 * br + n)),
                pl.BlockSpec((tm, MERGE_TN), lambda i, n: (i, n))]
    args = [xb, wp, z, dm]
    if aliased:
        in_specs.append(pl.BlockSpec(memory_space=pl.ANY))
        args.append(dz)
    return pl.pallas_call(
        body, name=name, grid=(s // tm, D_MODEL // MERGE_TN), in_specs=in_specs,
        out_specs=[pl.BlockSpec((tm, MERGE_TN), lambda i, n: (i, gate0 + 2 * br + n)),
                   pl.BlockSpec((tm, MERGE_TN), lambda i, n: (i, n)),
                   pl.BlockSpec((tm, kb), lambda i, n: (i, 0))],
        out_shape=[jax.ShapeDtypeStruct((s, IN_COLS), BF16), jax.ShapeDtypeStruct((s, D_MODEL), BF16),
                   jax.ShapeDtypeStruct((s, kb), F32)],
        input_output_aliases={4: 0} if aliased else {},
        compiler_params=_params("parallel", "arbitrary"),
    )(*args)


FFN_TM = 256
FFN_TC = 2816
FFN_STRIP = 256
FFN_UNROLL = 4


def _conv3(cur, prev, w_ref, b_ref):
    cat = jnp.concatenate([prev, cur], axis=0)
    x1 = pltpu.roll(cat, 1, 0)[HALO_CONV:]
    x2 = pltpu.roll(cat, 2, 0)[HALO_CONV:]
    return w_ref[0:1, :] * x2 + w_ref[1:2, :] * x1 + w_ref[2:3, :] * cur + b_ref[...], x1, x2


def _ffn_specs(tm):
    per = tm // HALO_CONV
    cur = pl.BlockSpec((tm, FFN_TC), lambda j, i: (i, j))
    prev = pl.BlockSpec((HALO_CONV, FFN_TC), lambda j, i: (jnp.maximum(i * per - 1, 0), j))
    w = pl.BlockSpec((3, FFN_TC), lambda j, i: (0, j))
    b = pl.BlockSpec((1, FFN_TC), lambda j, i: (0, j))
    return cur, prev, w, b


DOWN_TM = 256
DOWN_TK = 2816
DOWN_CHUNK = 256


def _ffn_down_fwd(upg, upv, cwg, cwv, cbg, cbv, w_down, res, gain, target, name):
    s = upg.shape[0]
    d = w_down.shape[1]
    tm = _tile(s, (DOWN_TM, DOWN_CHUNK, BLOCK))
    chunk = min(DOWN_CHUNK, tm)
    per = tm // HALO_CONV
    nk = D_FF // DOWN_TK
    last_layer = target is not None

    def body(g_ref, gp_ref, v_ref, vp_ref, wg_ref, wv_ref, bg_ref, bv_ref, w_ref, res_ref, extra_ref, o_ref, act_ref, tail_ref,
             acc_ref):
        first = pl.program_id(0) == 0
        kk = pl.program_id(1)

        @pl.when(kk == 0)
        def _():
            acc_ref[...] = jnp.zeros_like(acc_ref)

        for c in range(tm // chunk):
            rows = slice(c * chunk, (c + 1) * chunk)
            before = slice(c * chunk - HALO_CONV, c * chunk)
            g_p = jnp.where(first, 0.0, gp_ref[...]) if c == 0 else g_ref[before, :]
            v_p = jnp.where(first, 0.0, vp_ref[...]) if c == 0 else v_ref[before, :]
            gate = _conv3(g_ref[rows, :], g_p, wg_ref, bg_ref)[0]
            val = _conv3(v_ref[rows, :], v_p, wv_ref, bv_ref)[0]
            act = ((gate * _sigmoid(gate)) * val).astype(BF16)
            act_ref[rows, :] = act
            acc_ref[rows, :] += jnp.dot(act, w_ref[...], preferred_element_type=F32)

        @pl.when(kk == nk - 1)
        def _():
            x = acc_ref[...] + res_ref[...]
            if last_layer:
                err = x - extra_ref[...]
                o_ref[...] = err * (1.0 / d)
                part = jnp.sum(jnp.sum(err * err, axis=-1, keepdims=True) * (1.0 / d), axis=0, keepdims=True) * 0.5
                part = jnp.broadcast_to(part, (8, LANES))

                @pl.when(first)
                def _():
                    tail_ref[...] = part

                @pl.when(jnp.logical_not(first))
                def _():
                    tail_ref[...] += part
            else:
                o_ref[...] = x
                tail_ref[...] = _rms_rows(x, extra_ref[...])

    cur = pl.BlockSpec((tm, DOWN_TK), lambda i, kk: (i, kk))
    prev = pl.BlockSpec((HALO_CONV, DOWN_TK), lambda i, kk: (jnp.maximum(i * per - 1, 0), kk))
    w = pl.BlockSpec((3, DOWN_TK), lambda i, kk: (0, kk))
    b = pl.BlockSpec((1, DOWN_TK), lambda i, kk: (0, kk))
    row = pl.BlockSpec((tm, d), lambda i, kk: (i, 0))
    if last_layer:
        extra, extra_spec = target, row
        tail_spec, tail_shape = pl.BlockSpec((8, LANES), lambda i, kk: (0, 0)), jax.ShapeDtypeStruct((8, LANES), F32)
    else:
        extra, extra_spec = gain.reshape(1, d), pl.BlockSpec((1, d), lambda i, kk: (0, 0))
        tail_spec, tail_shape = row, jax.ShapeDtypeStruct((s, d), BF16)
    return pl.pallas_call(
        body, name=name, grid=(s // tm, nk),
        in_specs=[cur, prev, cur, prev, w, w, b, b, pl.BlockSpec((DOWN_TK, d), lambda i, kk: (kk, 0)), row, extra_spec],
        out_specs=[row, cur, tail_spec],
        out_shape=[jax.ShapeDtypeStruct((s, d), F32), jax.ShapeDtypeStruct((s, D_FF), BF16), tail_shape],
        scratch_shapes=[pltpu.VMEM((tm, d), F32)], compiler_params=_params("arbitrary", "arbitrary"),
    )(upg, upg, upv, upv, cwg, cwv, cbg, cbv, w_down, res, extra)


def _ffn_bwd(upg, upv, cwg, cwv, cbg, cbv, dact, name):
    s = upg.shape[0]
    tm = _tile(s, (FFN_TM, BLOCK))
    per = tm // HALO_CONV
    nrow = s // tm
    cur, prev, w, b = _ffn_specs(tm)
    nxt = pl.BlockSpec((HALO_CONV, FFN_TC), lambda j, i: (jnp.minimum((i + 1) * per, nrow * per - 1), j))

    nch = tm // 8
    rows8 = lambda r: pl.ds(pl.multiple_of(r * 8, 8), 8)

    def body(g_ref, gp_ref, gn_ref, v_ref, vp_ref, vn_ref, wg_ref, wv_ref, bg_ref, bv_ref, da_ref, dan_ref,
             dg_ref, dv_ref, dwg_ref, dwv_ref, og_ref, ov_ref):
        i = pl.program_id(1)
        first = i == 0
        last = i == nrow - 1
        row = lax.broadcasted_iota(jnp.int32, (8, FFN_STRIP), 0)

        keep_down = {k: row >= k for k in (1, 2)}
        keep_up = {k: row < 8 - k for k in (1, 2)}

        def down(cur, prev, k):
            return jnp.where(keep_down[k], pltpu.roll(cur, k, 0), pltpu.roll(prev, k, 0))

        def up(cur, nxt, k):
            return jnp.where(keep_up[k], pltpu.roll(cur, 8 - k, 0), pltpu.roll(nxt, 8 - k, 0))

        @pl.when(first)
        def _():
            dwg_ref[...] = jnp.zeros_like(dwg_ref)
            dwv_ref[...] = jnp.zeros_like(dwv_ref)

        for c in range(FFN_TC // FFN_STRIP):
            cols = slice(c * FFN_STRIP, (c + 1) * FFN_STRIP)
            wg = [functools.partial(lambda k: wg_ref[k:k + 1, cols], k) for k in range(3)]
            wv = [functools.partial(lambda k: wv_ref[k:k + 1, cols], k) for k in range(3)]

            def conv_grads(g_cur, g_prev, v_cur, v_prev, da):
                gate = wg[0]() * down(g_cur, g_prev, 2) + wg[1]() * down(g_cur, g_prev, 1) + wg[2]() * g_cur + bg_ref[:, cols]
                val = wv[0]() * down(v_cur, v_prev, 2) + wv[1]() * down(v_cur, v_prev, 1) + wv[2]() * v_cur + bv_ref[:, cols]
                sg = _sigmoid(gate)
                return (da * val) * (sg * (1.0 + gate * (1.0 - sg))), da * (gate * sg)

            def passes(q, carry):
                for u in range(FFN_UNROLL):
                    carry = one_pass(q * FFN_UNROLL + u, carry)
                return carry

            def one_pass(r, carry, tile_end=False):
                dg_cur, dv_cur, acc = carry
                g_r, v_r = g_ref[rows8(r), cols], v_ref[rows8(r), cols]
                if tile_end:
                    g_n, v_n, da_n = gn_ref[:, cols], vn_ref[:, cols], jnp.where(last, 0.0, dan_ref[:, cols])
                else:
                    g_n, v_n, da_n = g_ref[rows8(r + 1), cols], v_ref[rows8(r + 1), cols], da_ref[rows8(r + 1), cols]
                dg_n, dv_n = conv_grads(g_n, g_r, v_n, v_r, da_n)
                new_acc = []
                for o_ref, w, d_cur, d_n, x0, a in ((og_ref, wg, dg_cur, dg_n, g_r, acc[:4]), (ov_ref, wv, dv_cur, dv_n, v_r, acc[4:])):
                    d1, d2 = up(d_cur, d_n, 1), up(d_cur, d_n, 2)
                    o_ref[rows8(r), cols] = w[2]() * d_cur + w[1]() * d1 + w[0]() * d2
                    new_acc += [a[0] + d2 * x0, a[1] + d1 * x0, a[2] + d_cur * x0, a[3] + d_cur]
                return dg_n, dv_n, tuple(new_acc)

            g_p = jnp.where(first, 0.0, gp_ref[:, cols])
            v_p = jnp.where(first, 0.0, vp_ref[:, cols])
            dg0, dv0 = conv_grads(g_ref[0:8, cols], g_p, v_ref[0:8, cols], v_p, da_ref[0:8, cols])
            zero = jnp.zeros((8, FFN_STRIP), F32)
            carry = lax.fori_loop(0, nch // FFN_UNROLL - 1, passes, (dg0, dv0, (zero,) * 8))
            for r in range(nch - FFN_UNROLL, nch):
                carry = one_pass(r, carry, tile_end=r == nch - 1)
            for ref, a in ((dwg_ref, carry[2][:4]), (dwv_ref, carry[2][4:])):
                for k in range(4):
                    ref[k:k + 1, cols] += jnp.sum(a[k], axis=0, keepdims=True)
        dg_ref[...] = og_ref[...].astype(BF16)
        dv_ref[...] = ov_ref[...].astype(BF16)

    acc = pl.BlockSpec((8, FFN_TC), lambda j, i: (0, j))
    full = jax.ShapeDtypeStruct((s, D_FF), BF16)
    accs = jax.ShapeDtypeStruct((8, D_FF), F32)
    return pl.pallas_call(
        body, name=name, grid=(D_FF // FFN_TC, nrow), in_specs=[cur, prev, nxt, cur, prev, nxt, w, w, b, b, cur, nxt],
        out_specs=[cur, cur, acc, acc], out_shape=[full, full, accs, accs],
        scratch_shapes=[pltpu.VMEM((tm, FFN_TC), F32), pltpu.VMEM((tm, FFN_TC), F32)],
        compiler_params=_params("parallel", "arbitrary"),
    )(upg, upg, upg, upv, upv, upv, cwg, cwv, cbg, cbv, dact, dact)


def _mesh_place():
    return lax.axis_index("x"), lax.axis_index("y"), lax.axis_index("c")


def _all_gather(shards, name):
    na = len(shards)

    def body(*refs):
        x_refs, out_refs = refs[:na], refs[na:2 * na]
        send_sems, recv_sems, local_sems = refs[2 * na:]
        x, y, cc = _mesh_place()
        me, sibling = (x, y, cc), (x, y, 1 - cc)
        chips = [(1 - x, y), (x, 1 - y), (1 - x, 1 - y)]

        def copy(k, a, block, to, from_input=False):
            slot = out_refs[a].at[4 * block[0] + 2 * block[1] + block[2]]
            return pltpu.make_async_remote_copy(
                src_ref=x_refs[a] if from_input else slot, dst_ref=slot, send_sem=send_sems.at[k * na + a],
                recv_sem=recv_sems.at[k * na + a], device_id=to, device_id_type=pl.DeviceIdType.MESH)

        mine = [pltpu.make_async_copy(x_refs[a], out_refs[a].at[4 * x + 2 * y + cc], local_sems.at[a]) for a in range(na)]
        for cp in mine:
            cp.start()
        first = [copy(0, a, me, sibling, True) for a in range(na)]
        first += [copy(1 + j, a, me, (*chip, cc), True) for j, chip in enumerate(chips) for a in range(na)]
        for cp in first:
            cp.start()
        passed = []
        for j, chip in enumerate(chips):
            for a in range(na):
                copy(1 + j, a, (*chip, cc), me).wait_recv()
                passed.append(copy(4 + j, a, (*chip, cc), sibling))
                passed[-1].start()
        for a in range(na):
            copy(0, a, sibling, me).wait_recv()
        for j, chip in enumerate(chips):
            for a in range(na):
                copy(4 + j, a, (*chip, 1 - cc), me).wait_recv()
        for cp in first + passed:
            cp.wait_send()
        for cp in mine:
            cp.wait()

    hbm = pl.BlockSpec(memory_space=pl.ANY)
    return pl.pallas_call(
        body, name=name, out_shape=[jax.ShapeDtypeStruct((N_DEV,) + t.shape, t.dtype) for t in shards],
        in_specs=[hbm] * na, out_specs=[hbm] * na,
        scratch_shapes=[pltpu.SemaphoreType.DMA((7 * na,)), pltpu.SemaphoreType.DMA((7 * na,)), pltpu.SemaphoreType.DMA((na,))],
    )(*shards)


def _peer(k):
    x, y, cc = _mesh_place()
    px = 1 - x if (k >> 2) & 1 else x
    py = 1 - y if (k >> 1) & 1 else y
    pc = 1 - cc if k & 1 else cc
    return (px, py, pc), 4 * px + 2 * py + pc


def _push_copy(src_ref, land_ref, k, a, na, send_sems, recv_sems, indexed, landed):
    x, y, cc = _mesh_place()
    place, peer = _peer(k)
    sem = (k - 1) * na + a
    return pltpu.make_async_remote_copy(
        src_ref=src_ref.at[peer] if indexed else src_ref, dst_ref=land_ref.at[peer if landed else 4 * x + 2 * y + cc],
        send_sem=send_sems.at[sem], recv_sem=recv_sems.at[sem], device_id=place, device_id_type=pl.DeviceIdType.MESH)


_HBM = pl.BlockSpec(memory_space=pltpu.HBM)
_SEM = pl.BlockSpec(memory_space=pltpu.SEMAPHORE)
_EFFECT = pltpu.SideEffectType.DATAFLOW_SIDE_EFFECTING


def _push_start(srcs, indexed, name):
    na = len(srcs)
    lands = [lax.empty(t.shape if indexed else (N_DEV,) + t.shape, t.dtype) for t in srcs]

    def body(*refs):
        src_refs, land_refs = refs[:na], refs[na:2 * na]
        send_sems, recv_sems = refs[2 * na], refs[2 * na + 1]
        token = refs[-1]
        for k in range(1, N_DEV):
            for a in range(na):
                _push_copy(src_refs[a], land_refs[a], k, a, na, send_sems, recv_sems, indexed, False).start()
        token[...] = jnp.zeros_like(token)

    sems = pltpu.SemaphoreType.DMA((7 * na,))
    out = pl.pallas_call(
        body, name=name,
        out_shape=(sems, sems, *[pltpu.HBM(t.shape, t.dtype) for t in srcs], *[pltpu.HBM(t.shape, t.dtype) for t in lands],
                   jax.ShapeDtypeStruct((8, LANES), F32)),
        in_specs=[_HBM] * (2 * na), out_specs=(_SEM, _SEM, *[_HBM] * (2 * na), pl.BlockSpec(memory_space=pltpu.VMEM)),
        input_output_aliases={i: 2 + i for i in range(2 * na)},
        compiler_params=pltpu.CompilerParams(has_side_effects=_EFFECT),
    )(*[pltpu.with_memory_space_constraint(t, pltpu.HBM) for t in srcs + lands])
    return out[0], out[1], list(out[2:2 + na]), list(out[2 + na:2 + 2 * na]), out[-1]


def _push_wait(started, indexed, after, name):
    send_sems, recv_sems, srcs, lands, _ = started
    na = len(srcs)

    def body(*refs):
        src_refs, land_refs = refs[:na], refs[na:2 * na]
        send_sems, recv_sems = refs[2 * na], refs[2 * na + 1]
        for k in range(1, N_DEV):
            for a in range(na):
                copy = _push_copy(src_refs[a], land_refs[a], k, a, na, send_sems, recv_sems, indexed, True)
                copy.wait_send()
                copy.wait_recv()

    out = pl.pallas_call(
        body, name=name, out_shape=[pltpu.HBM(t.shape, t.dtype) for t in srcs + lands],
        in_specs=[_HBM] * (2 * na) + [_SEM, _SEM, pl.BlockSpec(memory_space=pl.ANY)], out_specs=[_HBM] * (2 * na),
        input_output_aliases={i: i for i in range(2 * na)},
        compiler_params=pltpu.CompilerParams(has_side_effects=_EFFECT),
    )(*srcs, *lands, send_sems, recv_sems, after)
    x, y, cc = _mesh_place()
    me = 4 * x + 2 * y + cc
    return [lax.dynamic_update_index_in_dim(
        land, lax.dynamic_index_in_dim(src, me, 0, keepdims=False) if indexed else src, me, 0)
        for src, land in zip(out[:na], out[na:])]


def _adamw_sum(parts, w, m, v, name):
    _, r, c = parts.shape
    tr = _tile(r, (256, 128, 64, 32, 16, 8))

    def body(p_ref, w_ref, m_ref, v_ref, g_ref, d_ref, nm_ref, nv_ref):
        _adam_store(_sum_parts(p_ref), w_ref, m_ref, v_ref, g_ref, d_ref, nm_ref, nv_ref)

    row = pl.BlockSpec((tr, c), lambda i: (i, 0))
    shp = jax.ShapeDtypeStruct((r, c), F32)
    return pl.pallas_call(
        body, name=name, grid=(r // tr,), in_specs=[pl.BlockSpec((N_DEV, tr, c), lambda i: (0, i, 0)), row, row, row],
        out_specs=[row, row, row, row], out_shape=[shp, shp, shp, shp], compiler_params=_params("parallel"),
    )(parts, w, m, v)


def _sum_parts(p_ref):
    g = p_ref[0].astype(F32)
    for k in range(1, N_DEV):
        g = g + p_ref[k].astype(F32)
    return g


def _adam_store(g, w_ref, m_ref, v_ref, g_ref, d_ref, nm_ref, nv_ref):
    nm = ADAM_B1 * m_ref[...] + (1.0 - ADAM_B1) * g
    nv = ADAM_B2 * v_ref[...] + (1.0 - ADAM_B2) * (g * g)
    m_hat = nm / (1.0 - ADAM_B1 ** ADAM_STEP)
    v_hat = nv / (1.0 - ADAM_B2 ** ADAM_STEP)
    g_ref[...] = g
    nm_ref[...] = nm
    nv_ref[...] = nv
    d_ref[...] = -ADAM_LR * (m_hat / (jnp.sqrt(v_hat) + ADAM_EPS) + ADAM_WD * w_ref[...])


def _adamw_weight(parts, w, m, v, name):
    _, r, c = w.shape
    tr = _tile(r, (256, 128, 176))
    nr = r // tr

    def body(p0_ref, p1_ref, w_ref, m_ref, v_ref, g_ref, d_ref, nm_ref, nv_ref):
        g = jnp.where(pl.program_id(0) == 0, _sum_parts(p0_ref), _sum_parts(p1_ref))
        _adam_store(g, w_ref, m_ref, v_ref, g_ref, d_ref, nm_ref, nv_ref)

    part = lambda layer: pl.BlockSpec(
        (N_DEV, tr, c), lambda l, i: (0, jnp.where(l == layer, i, (nr - 1) * (1 - layer)), 0))
    row = pl.BlockSpec((None, tr, c), lambda l, i: (l, i, 0))
    shp = jax.ShapeDtypeStruct(w.shape, F32)
    return pl.pallas_call(
        body, name=name, grid=(DEPTH, nr), in_specs=[part(0), part(1), row, row, row],
        out_specs=[row, row, row, row], out_shape=[shp, shp, shp, shp], compiler_params=_params("arbitrary", "arbitrary"),
    )(parts[0], parts[1], w, m, v)


def _full_to_slots(name, t):
    k, n = t.shape
    if name in ROW_SHARDED:
        return t.reshape(N_DEV, k // N_DEV, n)
    return t.reshape(k, N_DEV, n // N_DEV).transpose(1, 0, 2)


def _slots_to_full(name, t):
    _, r, c = t.shape
    if name in ROW_SHARDED:
        return t.reshape(N_DEV * r, c)
    return t.transpose(1, 0, 2).reshape(r, N_DEV * c)


def _small_sizes(shapes, names):
    return [(n, shapes[n], -(-int(math.prod(shapes[n])) // (8 * LANES)) * 8) for n in names]


def _pack_small(tree, shapes, names, row_tile):
    rows = []
    for n, shp, nrow in _small_sizes(shapes, names):
        flat = tree[n].reshape(-1)
        rows.append(jnp.pad(flat, (0, nrow * LANES - flat.shape[0])).reshape(nrow, LANES))
    total = sum(r.shape[0] for r in rows)
    if total % row_tile:
        rows.append(jnp.zeros((-total % row_tile, LANES), F32))
    return jnp.concatenate(rows, axis=0)


def _unpack_small(buf, shapes, names):
    out, r0 = {}, 0
    for n, shp, nrow in _small_sizes(shapes, names):
        out[n] = buf[r0:r0 + nrow].reshape(-1)[:int(math.prod(shp))].reshape(shp)
        r0 += nrow
    return out


def _block_diag(w):
    g = w.shape[0]
    eye = jnp.eye(g, dtype=w.dtype)
    return (eye[:, None, :, None] * w[:, :, None, :]).reshape(g * HEAD_DIM, g * HEAD_DIM)


def kernel(x, positions, norm1, w_in, q_norm, k_norm, sinks, w_pool, pool_scale, sgu_v_norm, w_s, b_s, w_proj_a, w_proj_b, w_proj_c, w_out, norm2, w_up, conv_w, conv_b, w_down, loss_target, m_norm1, m_w_in, m_q_norm, m_k_norm, m_sinks, m_w_pool, m_pool_scale, m_sgu_v_norm, m_w_s, m_b_s, m_w_proj_a, m_w_proj_b, m_w_proj_c, m_w_out, m_norm2, m_w_up, m_conv_w, m_conv_b, m_w_down, v_norm1, v_w_in, v_q_norm, v_k_norm, v_sinks, v_w_pool, v_pool_scale, v_sgu_v_norm, v_w_s, v_b_s, v_w_proj_a, v_w_proj_b, v_w_proj_c, v_w_out, v_norm2, v_w_up, v_conv_w, v_conv_b, v_w_down):
    names = ("norm1", "w_in", "q_norm", "k_norm", "sinks", "w_pool", "pool_scale", "sgu_v_norm", "w_s", "b_s", "w_proj_a",
             "w_proj_b", "w_proj_c", "w_out", "norm2", "w_up", "conv_w", "conv_b", "w_down")
    wts = dict(zip(names, (norm1, w_in, q_norm, k_norm, sinks, w_pool, pool_scale, sgu_v_norm, w_s, b_s, w_proj_a, w_proj_b,
                           w_proj_c, w_out, norm2, w_up, conv_w, conv_b, w_down)))
    mom = dict(zip(names, (m_norm1, m_w_in, m_q_norm, m_k_norm, m_sinks, m_w_pool, m_pool_scale, m_sgu_v_norm, m_w_s, m_b_s,
                           m_w_proj_a, m_w_proj_b, m_w_proj_c, m_w_out, m_norm2, m_w_up, m_conv_w, m_conv_b, m_w_down)))
    var = dict(zip(names, (v_norm1, v_w_in, v_q_norm, v_k_norm, v_sinks, v_w_pool, v_pool_scale, v_sgu_v_norm, v_w_s, v_b_s,
                           v_w_proj_a, v_w_proj_b, v_w_proj_c, v_w_out, v_norm2, v_w_up, v_conv_w, v_conv_b, v_w_down)))
    xs = x[0]
    target = loss_target[0]
    s = xs.shape[0]

    inv_freq = ROPE_THETA ** (-jnp.arange(0, HEAD_DIM, 2, dtype=F32) / HEAD_DIM)
    ang = positions[0].astype(F32)[:, None] * inv_freq
    cosf = jnp.tile(jnp.cos(ang), (1, 4))
    sinf = jnp.tile(jnp.concatenate([-jnp.sin(ang), jnp.sin(ang)], axis=1), (1, 2))

    local = [{n: wts[n][l] if n == "conv_w" else wts[n][l].astype(BF16) for n in SHARDED} for l in range(DEPTH)]
    later = SHARDED[1:]
    full = [{"w_in": _slots_to_full("w_in", _all_gather([local[0]["w_in"]], "gather_w_in_0")[0])}, None]
    gather0 = _push_start([local[0][n] for n in later], False, "gather_rest_0_start")
    norm1_first = norm1[0] + gather0[4][0, 0]

    def layer_consts(l):
        return dict(
            bdw=_block_diag(w_pool[l]).astype(BF16), qn=jnp.tile(q_norm[l], 2).reshape(1, LANES),
            kn=jnp.tile(k_norm[l], 2).reshape(1, LANES), vn=jnp.tile(sgu_v_norm[l], 4).reshape(1, SGU_W),
            bcol=jnp.repeat(b_s[l].T, HEAD_DIM, axis=1),
            cbg=conv_b[l][:D_FF].reshape(1, D_FF), cbv=conv_b[l][D_FF:].reshape(1, D_FF))

    gate_cols, val_cols = (0, D_FF), (D_FF, D_FF)

    saved = []
    cur = xs
    for l in range(DEPTH):
        if l == 1:
            landed = _push_wait(gather1, False, cur, "gather_weights_1_wait")
            full[1] = {n: _slots_to_full(n, t) for n, t in zip(SHARDED, landed)}
        fw, k = full[l], layer_consts(l)
        h1 = _rms_fwd(cur, norm1_first, "rms1_fwd_0") if l == 0 else h_next
        z = _mm(h1, fw["w_in"], name=f"in_proj_{l}")
        a = _pool_fwd(z, k["bdw"], pool_scale[l], f"pool_fwd_{l}")
        b = _attn_fwd(z, cosf, sinf, k["qn"], k["kn"], sinks[l], f"attn_fwd_{l}")
        c = _sgu_fwd(z, w_s[l], k["bcol"], k["vn"], f"sgu_fwd_{l}")
        w_proj_a_l = fw.get("w_proj_a")
        if l == 0:
            landed = _push_wait(gather0, False, c, "gather_rest_0_wait")
            fw.update({n: _slots_to_full(n, t) for n, t in zip(later, landed)})
            gather1 = _push_start([local[1][n] for n in SHARDED], False, "gather_weights_1_start")
            w_proj_a_l = fw["w_proj_a"] + gather1[4][0, 0].astype(BF16)
        x1, merged, h2 = _merge_out_fwd(a, b, c, w_proj_a_l, fw["w_proj_b"], fw["w_proj_c"], z, fw["w_out"], cur, norm2[l],
                                        f"merge_out_fwd_{l}")
        upg = _mm(h2, fw["w_up"], b_n=gate_cols, name=f"up_gate_{l}")
        upv = _mm(h2, fw["w_up"], b_n=val_cols, name=f"up_val_{l}")
        k["cwg"], k["cwv"] = fw["conv_w"][:, :D_FF], fw["conv_w"][:, D_FF:]
        last = l == DEPTH - 1
        x2, act, h_next = _ffn_down_fwd(upg, upv, k["cwg"], k["cwv"], k["cbg"], k["cbv"], fw["w_down"], x1,
                                        None if last else norm1[l + 1], target if last else None, f"ffn_down_fwd_{l}")
        saved.append(dict(x0=cur, h1=h1, z=z, a=a, b=b, c=c, merged=merged, x1=x1, h2=h2, upg=upg, upv=upv, act=act))
        cur = x2

    dcur, loss_tile = cur, h_next
    loss = lax.psum(loss_tile[0, 0], ("x", "y", "c"))

    gsmall = [None] * DEPTH
    small_shapes = {n: wts[n].shape for n in SMALL}

    def slots_of(grads):
        return [_full_to_slots(n, t) for n, t in grads.items()]

    for l in reversed(range(DEPTH)):
        fw, k, sv = full[l], layer_consts(l), saved[l]
        k["cwg"], k["cwv"] = fw["conv_w"][:, :D_FF], fw["conv_w"][:, D_FF:]
        staged = l == 0
        wgrad = functools.partial(_mm, ta=True, out_dtype=BF16)
        w_down_l = fw["w_down"] + exchange1[4][0, 0].astype(BF16) if staged else fw["w_down"]
        dact = _mm(dcur, w_down_l, tb=True, name=f"down_proj_bwd_{l}")
        g_down = wgrad(sv["act"], dcur, name=f"down_proj_wgrad_{l}")
        dg0, dv0, dcg, dcv = _ffn_bwd(sv["upg"], sv["upv"], k["cwg"], k["cwv"], k["cbg"], k["cbv"], dact, f"ffn_bwd_{l}")
        dh2 = _mm(dg0, fw["w_up"], tb=True, b_k=gate_cols, name=f"up_gate_bwd_{l}")
        dh2_gate = dh2
        g_up = wgrad(sv["h2"], dg0, out_cols=(0, 2 * D_FF), name=f"up_gate_wgrad_{l}")
        g_up = wgrad(sv["h2"], dv0, out_cols=(D_FF, 2 * D_FF), out_into=g_up, name=f"up_val_wgrad_{l}")
        g_ffn = dict(w_up=g_up, w_down=g_down, conv_w=jnp.concatenate([dcg[0:3], dcv[0:3]], axis=1))
        norm2_l = norm2[l]
        if staged:
            parts1 = dict(zip(SHARDED, _push_wait(exchange1, True, g_up, "exchange_grads_1_wait")))
            exchange_ffn = _push_start(slots_of(g_ffn), True, "exchange_ffn_0_start")
            norm2_l = norm2_l + exchange_ffn[4][0, 0]
        dx1, g_norm2 = _mm(dv0, fw["w_up"], tb=True, b_k=val_cols, add=dh2_gate, rms=(sv["x1"], norm2_l, dcur),
                           name=f"up_val_bwd_{l}")
        dmerged = _mm(dx1, fw["w_out"], tb=True, name=f"out_proj_bwd_{l}")
        g_out = wgrad(sv["merged"], dx1, name=f"out_proj_wgrad_{l}")
        dz, dya, da = _branch_bwd(0, sv["a"], fw["w_proj_a"], sv["z"], dmerged, None, f"branch_a_bwd_{l}")
        dz, dyb, db = _branch_bwd(1, sv["b"], fw["w_proj_b"], sv["z"], dmerged, dz, f"branch_b_bwd_{l}")
        dz, dyc, dc = _branch_bwd(2, sv["c"], fw["w_proj_c"], sv["z"], dmerged, dz, f"branch_c_bwd_{l}")
        g_mix = dict(w_proj_a=wgrad(sv["a"], dya, name=f"proj_a_wgrad_{l}"), w_proj_b=wgrad(sv["b"], dyb, name=f"proj_b_wgrad_{l}"),
                     w_proj_c=wgrad(sv["c"], dyc, name=f"proj_c_wgrad_{l}"), w_out=g_out)
        pool_scale_l = pool_scale[l]
        if staged:
            exchange_mix = _push_start(slots_of(g_mix), True, "exchange_mixer_0_start")
            pool_scale_l = pool_scale_l + exchange_mix[4][0, 0]
        dxp, g_bdw, g_pscale = _pool_bwd(sv["z"], da, k["bdw"], pool_scale_l, f"pool_bwd_{l}")
        dq, dkc, dkp, dvc, dvp, g_qn, g_sink = _attn_bwd(sv["z"], cosf, sinf, k["qn"], k["kn"], sinks[l], db, f"attn_bwd_{l}")
        duv, g_ws, g_bacc, g_vn = _sgu_bwd(sv["z"], w_s[l], k["bcol"], k["vn"], dc, f"sgu_bwd_{l}")
        gsmall[l] = dict(
            q_norm=g_qn[0, :HEAD_DIM], sinks=g_sink[:, 0],
            w_pool=jnp.stack([g_bdw[g * HEAD_DIM:(g + 1) * HEAD_DIM, g * HEAD_DIM:(g + 1) * HEAD_DIM] for g in range(4)]),
            pool_scale=g_pscale[0], sgu_v_norm=g_vn[0, :HEAD_DIM], w_s=g_ws, b_s=g_bacc[:, ::HEAD_DIM].T,
            norm2=g_norm2[0], conv_b=jnp.concatenate([dcg[3], dcv[3]]))
        kn_l = k["kn"]
        if staged:
            early = _pack_small({n: jnp.stack([gsmall[i][n] for i in range(DEPTH)]) for n in SMALL_EARLY}, small_shapes,
                                SMALL_EARLY, SMALL_ROW_TILE)
            gather_early = _push_start([early], False, "gather_small_grads_start")
            kn_l = kn_l + gather_early[4][0, 0]
        dz, g_kn = _kv_post(sv["z"], cosf, sinf, kn_l, dkc, dkp, dvc, dvp, dxp, dq, duv, dz, f"kv_post_{l}")
        g_in = dict(w_in=wgrad(sv["h1"], dz, name=f"in_proj_wgrad_{l}"))
        norm1_l = norm1[l]
        if staged:
            exchange_in = _push_start(slots_of(g_in), True, "exchange_w_in_0_start")
            norm1_l = norm1_l + exchange_in[4][0, 0]
        dcur, g_norm1 = _mm(dz, fw["w_in"], tb=True, rms=(sv["x0"], norm1_l, dx1), name=f"in_proj_bwd_{l}")
        if not staged:
            exchange1 = _push_start(slots_of({n: {**g_in, **g_mix, **g_ffn}[n] for n in SHARDED}), True, "exchange_grads_1_start")
        gsmall[l].update(norm1=g_norm1[0], k_norm=g_kn[0, :HEAD_DIM])
    grad_x = dcur[None]

    def update_small(gathered, names, row_tile, name):
        pack = lambda tree: _pack_small(tree, small_shapes, names, row_tile)
        return [_unpack_small(t, small_shapes, names) for t in _adamw_sum(gathered, pack(wts), pack(mom), pack(var), name)]

    late = _pack_small({n: jnp.stack([gsmall[i][n] for i in range(DEPTH)]) for n in SMALL_LATE}, small_shapes, SMALL_LATE, 8)
    gather_late = _push_start([late], False, "gather_late_small_grads_start")

    parts0 = dict(zip(g_ffn, _push_wait(exchange_ffn, True, gather_late[4], "exchange_ffn_0_wait")))
    parts0.update(zip(g_mix, _push_wait(exchange_mix, True, gather_late[4], "exchange_mixer_0_wait")))
    update = lambda n: _adamw_weight([parts0[n], parts1[n]], wts[n], mom[n], var[n], f"adamw_{n}")
    big = {n: update(n) for n in SHARDED[1:]}
    parts0.update(zip(g_in, _push_wait(exchange_in, True, big["w_up"][0], "exchange_w_in_0_wait")))
    big["w_in"] = update("w_in")

    late_all = _push_wait(gather_late, False, big["w_in"][0], "gather_late_small_grads_wait")[0]
    small = update_small(late_all, SMALL_LATE, 8, "adamw_replicated_late")
    early_all = _push_wait(gather_early, False, big["w_in"][0], "gather_small_grads_wait")[0]
    for kind, tree in enumerate(update_small(early_all, SMALL_EARLY, SMALL_ROW_TILE, "adamw_replicated")):
        small[kind].update(tree)

    outs = [loss, grad_x]
    for kind in range(4):
        outs += [small[kind][n] if n in SMALL else big[n][kind] for n in names]
    return tuple(outs)
```

```python
import functools
import math

import jax
import jax.numpy as jnp
import numpy as np
from jax import lax
from jax.experimental import pallas as pl
from jax.experimental.pallas import tpu as pltpu

F32 = jnp.float32
BF16 = jnp.bfloat16

D_MODEL = 1024
DEPTH = 2
HEAD_DIM = 64
N_Q_HEADS = 8
Q_PER_KV = 4
BLOCK = 128
POOL_W = 256
ATTN_W = 512
KV_W = 128
SGU_W = 256
IN_COLS = 4608
GATE_COL0 = 1536
D_FF = 2816
EPS = 1e-6
ROPE_THETA = 10000.0
N_DEV = 8
LANES = 128
HALO_POOL = 16
HALO_CONV = 8

ADAM_LR = 0.001
ADAM_B1 = 0.9
ADAM_B2 = 0.999
ADAM_EPS = 1e-08
ADAM_WD = 0.01
ADAM_STEP = 10

VMEM_LIMIT = 48 * 1024 * 1024
MM_VMEM_BUDGET = 40 * 1024 * 1024

SHARDED = ("w_in", "w_proj_a", "w_proj_b", "w_proj_c", "w_out", "w_up", "w_down", "conv_w")
ROW_SHARDED = ("w_out", "w_down")
SMALL_ROW_TILE = 256
SMALL = ("norm1", "q_norm", "k_norm", "sinks", "w_pool", "pool_scale", "sgu_v_norm", "w_s", "b_s", "norm2", "conv_b")
SMALL_LATE = ("norm1", "k_norm")
SMALL_EARLY = tuple(n for n in SMALL if n not in SMALL_LATE)

_GELU_C = math.sqrt(2.0 / math.pi)
_GELU_A = 0.044715


def _params(*sem):
    return pltpu.CompilerParams(dimension_semantics=sem, vmem_limit_bytes=VMEM_LIMIT)


def _tile(n, prefs):
    for t in prefs:
        if t <= n and n % t == 0:
            return t
    return n


def _head_mean_matrix():
    r = lax.broadcasted_iota(jnp.int32, (LANES, LANES), 0)
    c = lax.broadcasted_iota(jnp.int32, (LANES, LANES), 1)
    return jnp.where((r >= HEAD_DIM) == (c >= HEAD_DIM), 1.0 / HEAD_DIM, 0.0).astype(BF16)


def _head_mean(v, bd):
    hi = v.astype(BF16)
    rest = v - hi.astype(F32)
    mid = rest.astype(BF16)
    lo = (rest - mid.astype(F32)).astype(BF16)
    mm = lambda p: jnp.dot(p, bd, preferred_element_type=F32)
    return mm(hi) + (mm(mid) + mm(lo))


def _rot_half(t):
    lane = lax.broadcasted_iota(jnp.int32, t.shape, 1)
    return jnp.where((lane & 32) == 0, pltpu.roll(t, LANES - 32, 1), pltpu.roll(t, 32, 1))


def _norm_rope(t, gn, cosf, sinf, bd):
    r = lax.rsqrt(_head_mean(t * t, bd) + EPS)
    n = t * r
    y = n * gn
    return y * cosf + _rot_half(y) * sinf, n, r


def _norm_rope_bwd(d, t, n, r, gn, cosf, sinf, bd):
    dy = d * cosf + _rot_half(d * sinf)
    dgn = jnp.sum(dy * n, axis=0, keepdims=True)
    u = dy * gn
    dt = r * u - t * (r * r * r) * _head_mean(t * u, bd)
    return dt, dgn


def _gelu(x):
    t = jnp.tanh(_GELU_C * (x + _GELU_A * (x * x * x)))
    return 0.5 * x * (1.0 + t), t


def _gelu_grad(x, t):
    return 0.5 * (1.0 + t) + 0.5 * x * (1.0 - t * t) * (_GELU_C * (1.0 + 3.0 * _GELU_A * x * x))


def _sigmoid(x):
    return jax.nn.sigmoid(x)


def _dot(a, b, ca=1, cb=0):
    return lax.dot_general(a.astype(BF16), b.astype(BF16), (((ca,), (cb,)), ((), ())), preferred_element_type=F32)


def _mm(a, b, *, ta=False, tb=False, add=None, out_dtype=F32, name, b_n=None, b_k=None, out_cols=None, out_into=None,
        rms=None):
    m = a.shape[1] if ta else a.shape[0]
    k = a.shape[0] if ta else a.shape[1]
    n = b_n[1] if b_n else (b.shape[0] if tb else b.shape[1])
    tn = _tile(n, (1024, 1152, 1408, 512, 256, 128))
    has_add = add is not None
    has_rms = rms is not None
    assert not has_rms or (tn == n and out_into is None)
    fits = []
    for tm in (2048, 1024, 1408, 512, 256, 128):
        for tk in (k, 4608, 2816, 2048, 1408, 1152, 1024, 512, 256, 128):
            if tm <= m and m % tm == 0 and tk <= k and k % tk == 0:
                need = (2 * (tm * tk * a.dtype.itemsize + tk * tn * b.dtype.itemsize) + 2 * tm * tn * jnp.dtype(out_dtype).itemsize
                        + 2 * tm * tn * 4 * (has_add + 2 * has_rms) + tm * tn * 4 * (tk < k))
                if need <= MM_VMEM_BUDGET:
                    fits.append((k // tk, -tm, tm, tk))
    if fits:
        _, _, tm, tk = min(fits)
    else:
        tm, tk = _tile(m, (256, 128)), _tile(k, (512, 256, 128))
    nk = k // tk
    n0 = b_n[0] // tn if b_n else 0
    k0 = b_k[0] // tk if b_k else 0
    o0, n_out = (out_cols[0] // tn, out_cols[1]) if out_cols else (0, n)
    n_in = 2 + has_add + (out_into is not None) + 3 * has_rms

    def body(*refs):
        a_ref, b_ref = refs[0], refs[1]
        add_ref = refs[2] if has_add else None
        o_ref = refs[n_in]

        def finish(r):
            if has_add:
                r = r + add_ref[...]
            if has_rms:
                x_ref, g_ref, dres_ref = refs[n_in - 3:n_in]
                dg_ref = refs[n_in + 1]
                x = x_ref[...]
                rr = lax.rsqrt(jnp.mean(x * x, axis=-1, keepdims=True) + EPS)
                u = r * g_ref[...]
                part = jnp.sum(r * (x * rr), axis=0, keepdims=True)
                r = dres_ref[...] + (rr * u - x * (rr * rr * rr) * jnp.mean(x * u, axis=-1, keepdims=True))

                @pl.when(pl.program_id(0) == 0)
                def _():
                    dg_ref[...] = part

                @pl.when(pl.program_id(0) > 0)
                def _():
                    dg_ref[...] += part

            o_ref[...] = r.astype(out_dtype)

        if nk == 1:
            finish(_dot(a_ref[...], b_ref[...], 0 if ta else 1, 1 if tb else 0))
        else:
            acc_ref = refs[-1]
            kk = pl.program_id(2)

            @pl.when(kk == 0)
            def _():
                acc_ref[...] = jnp.zeros_like(acc_ref)

            acc_ref[...] += _dot(a_ref[...], b_ref[...], 0 if ta else 1, 1 if tb else 0)

            @pl.when(kk == nk - 1)
            def _():
                finish(acc_ref[...])

    a_spec = pl.BlockSpec((tk, tm), lambda i, j, kk: (kk, i)) if ta else pl.BlockSpec((tm, tk), lambda i, j, kk: (i, kk))
    if tb:
        b_spec = pl.BlockSpec((tn, tk), lambda i, j, kk: (j + n0, kk + k0))
    else:
        b_spec = pl.BlockSpec((tk, tn), lambda i, j, kk: (kk + k0, j + n0))
    tile = pl.BlockSpec((tm, tn), lambda i, j, kk: (i, j))
    vec = pl.BlockSpec((1, tn), lambda i, j, kk: (0, j))
    in_specs = [a_spec, b_spec] + ([tile] if has_add else [])
    args = (a, b) + ((add,) if has_add else ())
    if out_into is not None:
        in_specs.append(pl.BlockSpec(memory_space=pl.ANY))
        args += (out_into,)
    out_specs = [pl.BlockSpec((tm, tn), lambda i, j, kk: (i, j + o0))]
    out_shape = [jax.ShapeDtypeStruct((m, n_out), out_dtype)]
    if has_rms:
        in_specs += [tile, vec, tile]
        args += (rms[0], rms[1].reshape(1, n), rms[2])
        out_specs.append(vec)
        out_shape.append(jax.ShapeDtypeStruct((1, n), F32))
    out = pl.pallas_call(
        body, name=name, grid=(m // tm, n // tn, nk), in_specs=in_specs, out_specs=out_specs, out_shape=out_shape,
        scratch_shapes=[pltpu.VMEM((tm, tn), F32)] if nk > 1 else [],
        input_output_aliases={n_in - 1: 0} if out_into is not None else {},
        compiler_params=_params("arbitrary" if has_rms else "parallel", "parallel", "arbitrary"),
    )(*args)
    return out if has_rms else out[0]


def _rms_fwd(x, g, name):
    s, d = x.shape
    tr = _tile(s, (512, 256, 128))

    def body(x_ref, g_ref, h_ref):
        xv = x_ref[...]
        r = lax.rsqrt(jnp.mean(xv * xv, axis=-1, keepdims=True) + EPS)
        h_ref[...] = ((xv * r) * g_ref[...]).astype(BF16)

    return pl.pallas_call(
        body, name=name, grid=(s // tr,),
        in_specs=[pl.BlockSpec((tr, d), lambda i: (i, 0)), pl.BlockSpec((1, d), lambda i: (0, 0))],
        out_specs=pl.BlockSpec((tr, d), lambda i: (i, 0)),
        out_shape=jax.ShapeDtypeStruct((s, d), BF16), compiler_params=_params("parallel"),
    )(x, g.reshape(1, d))


def _pool_lane_select(lane, v2, v4, v8, v16):
    return jnp.where(lane < 64, v2, jnp.where(lane < 128, v4, jnp.where(lane < 192, v8, v16)))


def _pool_diff(xc, xp, row0):
    n = BLOCK + HALO_POOL
    cat = jnp.concatenate([xp, xc], axis=0)
    s2 = cat + pltpu.roll(cat, 1, 0)
    s4 = s2 + pltpu.roll(s2, 2, 0)
    s8 = s4 + pltpu.roll(s4, 4, 0)
    s16 = s8 + pltpu.roll(s8, 8, 0)
    lane = lax.broadcasted_iota(jnp.int32, (n, POOL_W), 1)
    wsum = _pool_lane_select(lane, s2, s4, s8, s16)[HALO_POOL:]
    return wsum / _pool_count(row0, BLOCK) - xc


def _pool_count(row0, rows):
    lane = lax.broadcasted_iota(jnp.int32, (rows, POOL_W), 1)
    t = lax.broadcasted_iota(jnp.int32, (rows, POOL_W), 0) + row0
    return jnp.minimum(t + 1, _pool_lane_select(lane, 2, 4, 8, 16)).astype(F32)


def _pool_fwd(z, bdw, scale, name):
    s = z.shape[0]
    nb = s // BLOCK

    def body(xc_ref, xp_ref, w_ref, sc_ref, a_ref):
        i = pl.program_id(0)
        xp = jnp.where(i > 0, xp_ref[...], 0.0)
        diff = _pool_diff(xc_ref[...], xp, i * BLOCK)
        a_ref[...] = (_dot(diff, w_ref[...]) * sc_ref[...]).astype(BF16)

    return pl.pallas_call(
        body, name=name, grid=(nb,),
        in_specs=[pl.BlockSpec((BLOCK, POOL_W), lambda i: (i, 0)),
                  pl.BlockSpec((HALO_POOL, POOL_W), lambda i: (jnp.maximum(i * (BLOCK // HALO_POOL) - 1, 0), 0)),
                  pl.BlockSpec((POOL_W, POOL_W), lambda i: (0, 0)),
                  pl.BlockSpec((1, POOL_W), lambda i: (0, 0))],
        out_specs=pl.BlockSpec((BLOCK, POOL_W), lambda i: (i, 0)),
        out_shape=jax.ShapeDtypeStruct((s, POOL_W), BF16), compiler_params=_params("parallel"),
    )(z, z, bdw, scale.reshape(1, POOL_W))


def _pool_bwd(z, da, bdw, scale, name):
    s = z.shape[0]
    nb = s // BLOCK
    per = BLOCK // HALO_POOL
    n = BLOCK + HALO_POOL

    def body(xc_ref, xp_ref, dac_ref, dan_ref, w_ref, sc_ref, dx_ref, dw_ref, dsc_ref):
        i = pl.program_id(0)
        xp = jnp.where(i > 0, xp_ref[...], 0.0)
        diff = _pool_diff(xc_ref[...], xp, i * BLOCK)
        mixed = _dot(diff, w_ref[...])
        dac = dac_ref[...]
        dan = jnp.where(i < nb - 1, dan_ref[...], 0.0)
        dmix = jnp.concatenate([dac, dan], axis=0) * sc_ref[...]
        ddiff = _dot(dmix, w_ref[...], 1, 1)
        e = ddiff / _pool_count(i * BLOCK, n)
        f2 = e + pltpu.roll(e, n - 1, 0)
        f4 = f2 + pltpu.roll(f2, n - 2, 0)
        f8 = f4 + pltpu.roll(f4, n - 4, 0)
        f16 = f8 + pltpu.roll(f8, n - 8, 0)
        lane = lax.broadcasted_iota(jnp.int32, (n, POOL_W), 1)
        back = _pool_lane_select(lane, f2, f4, f8, f16)
        dx_ref[...] = (back[:BLOCK] - ddiff[:BLOCK]).astype(BF16)
        dw = _dot(diff, dmix[:BLOCK], 0, 0)
        dsc = jnp.sum(dac * mixed, axis=0, keepdims=True)

        @pl.when(i == 0)
        def _():
            dw_ref[...] = dw
            dsc_ref[...] = dsc

        @pl.when(i > 0)
        def _():
            dw_ref[...] += dw
            dsc_ref[...] += dsc

    blk = pl.BlockSpec((BLOCK, POOL_W), lambda i: (i, 0))
    return pl.pallas_call(
        body, name=name, grid=(nb,),
        in_specs=[blk, pl.BlockSpec((HALO_POOL, POOL_W), lambda i: (jnp.maximum(i * per - 1, 0), 0)),
                  blk, pl.BlockSpec((HALO_POOL, POOL_W), lambda i: (jnp.minimum((i + 1) * per, nb * per - 1), 0)),
                  pl.BlockSpec((POOL_W, POOL_W), lambda i: (0, 0)), pl.BlockSpec((1, POOL_W), lambda i: (0, 0))],
        out_specs=[blk, pl.BlockSpec((POOL_W, POOL_W), lambda i: (0, 0)), pl.BlockSpec((1, POOL_W), lambda i: (0, 0))],
        out_shape=[jax.ShapeDtypeStruct((s, POOL_W), BF16), jax.ShapeDtypeStruct((POOL_W, POOL_W), F32),
                   jax.ShapeDtypeStruct((1, POOL_W), F32)],
        compiler_params=_params("arbitrary"),
    )(z, z, da, da, bdw, scale.reshape(1, POOL_W))


def _attn_setup(zc_ref, zp_ref, cc_ref, cp_ref, sc_ref, sp_ref, qn_ref, kn_ref, bd):
    q = []
    for j in range(ATTN_W // LANES):
        t = zc_ref[:, POOL_W + j * LANES:POOL_W + (j + 1) * LANES]
        q.append((t,) + _norm_rope(t, qn_ref[...], cc_ref[...], sc_ref[...], bd))
    kc_raw = zc_ref[:, POOL_W + ATTN_W:POOL_W + ATTN_W + KV_W]
    kc = _norm_rope(kc_raw, kn_ref[...], cc_ref[...], sc_ref[...], bd)[0]
    kp = _norm_rope(zp_ref[:, :KV_W], kn_ref[...], cp_ref[...], sp_ref[...], bd)[0]
    kband = jnp.concatenate([kp, kc], axis=0).astype(BF16)
    vband = jnp.concatenate([zp_ref[:, KV_W:], zc_ref[:, POOL_W + ATTN_W + KV_W:POOL_W + ATTN_W + 2 * KV_W]], axis=0).astype(BF16)
    return q, kband, vband


MASKED = -1e30


def _window_bias(first_block):
    row = np.arange(Q_PER_KV * BLOCK)[:, None] % BLOCK
    col = np.arange(2 * BLOCK)[None, :]
    dist = row + BLOCK - col
    inside = (dist >= 0) & (dist < BLOCK) & ((col >= BLOCK) | (not first_block))
    return jnp.asarray(np.where(inside, 0.0, MASKED), F32)


def _stack_heads(tiles, kvh):
    return jnp.concatenate([_to_kv_lanes(tiles[h // 2], h) for h in range(kvh * Q_PER_KV, (kvh + 1) * Q_PER_KV)], axis=0)


def _unstack_heads(stacked, kvh, tiles):
    for g in range(Q_PER_KV):
        h = kvh * Q_PER_KV + g
        t = _from_kv_lanes(stacked[g * BLOCK:(g + 1) * BLOCK], h)
        tiles[h // 2] = t if tiles[h // 2] is None else tiles[h // 2] + t


def _sink_column(sink_ref, kvh):
    grp = lax.broadcasted_iota(jnp.int32, (Q_PER_KV * BLOCK, 1), 0) >> 7
    s = [sink_ref[kvh * Q_PER_KV + g] for g in range(Q_PER_KV)]
    return jnp.where(grp == 0, s[0], jnp.where(grp == 1, s[1], jnp.where(grp == 2, s[2], s[3])))


def _to_kv_lanes(t, h):
    kvh = h // Q_PER_KV
    if (h % 2) != kvh:
        t = pltpu.roll(t, HEAD_DIM, 1)
    lane = lax.broadcasted_iota(jnp.int32, t.shape, 1)
    return jnp.where((lane >= HEAD_DIM) == (kvh == 1), t, 0.0)


def _from_kv_lanes(t, h):
    kvh = h // Q_PER_KV
    lane = lax.broadcasted_iota(jnp.int32, t.shape, 1)
    t = jnp.where((lane >= HEAD_DIM) == (kvh == 1), t, 0.0)
    if (h % 2) != kvh:
        t = pltpu.roll(t, HEAD_DIM, 1)
    return t


def _attn_probs(qh, kband, bias, sink):
    sc = _dot(qh, kband, 1, 1) * (HEAD_DIM ** -0.5) + bias
    m = jnp.maximum(jnp.max(sc, axis=1, keepdims=True), sink)
    p = jnp.exp(sc - m)
    psink = jnp.exp(sink - m)
    den = jnp.sum(p, axis=1, keepdims=True) + psink
    return p / den, psink / den


def _attn_specs(nb):
    cur = lambda i: (i, 0)
    prev = lambda i: (jnp.maximum(i - 1, 0), 0)
    tab = lambda f: pl.BlockSpec((BLOCK, LANES), f)
    vec = pl.BlockSpec((1, LANES), lambda i: (0, 0))
    return [pl.BlockSpec((BLOCK, 1024), cur),
            pl.BlockSpec((BLOCK, 2 * KV_W), lambda i: (jnp.maximum(i - 1, 0), 3)),
            tab(cur), tab(prev), tab(cur), tab(prev), vec, vec,
            pl.BlockSpec(memory_space=pltpu.SMEM)] + [pl.BlockSpec((Q_PER_KV * BLOCK, 2 * BLOCK), lambda i: (0, 0))] * 2


def _attn_fwd(z, cosf, sinf, qn, kn, sinks, name):
    s = z.shape[0]
    nb = s // BLOCK

    def body(zc_ref, zp_ref, cc_ref, cp_ref, sc_ref, sp_ref, qn_ref, kn_ref, sink_ref, bias_ref, bias0_ref, o_ref):
        i = pl.program_id(0)
        bd = _head_mean_matrix()
        q, kband, vband = _attn_setup(zc_ref, zp_ref, cc_ref, cp_ref, sc_ref, sp_ref, qn_ref, kn_ref, bd)
        mask = jnp.where(i > 0, bias_ref[...], bias0_ref[...])
        out = [None] * (ATTN_W // LANES)
        for kvh in range(N_Q_HEADS // Q_PER_KV):
            qs = _stack_heads([t[1] for t in q], kvh)
            probs, _ = _attn_probs(qs, kband, mask, _sink_column(sink_ref, kvh))
            _unstack_heads(_dot(probs, vband), kvh, out)
        for j, o in enumerate(out):
            o_ref[:, j * LANES:(j + 1) * LANES] = o.astype(BF16)

    return pl.pallas_call(
        body, name=name, grid=(nb,), in_specs=_attn_specs(nb),
        out_specs=pl.BlockSpec((BLOCK, ATTN_W), lambda i: (i, 0)),
        out_shape=jax.ShapeDtypeStruct((s, ATTN_W), BF16), compiler_params=_params("parallel"),
    )(z, z, cosf, cosf, sinf, sinf, qn, kn, sinks, _window_bias(False), _window_bias(True))


def _attn_bwd(z, cosf, sinf, qn, kn, sinks, d_out, name):
    s = z.shape[0]
    nb = s // BLOCK
    nt = ATTN_W // LANES

    def body(zc_ref, zp_ref, cc_ref, cp_ref, sc_ref, sp_ref, qn_ref, kn_ref, sink_ref, bias_ref, bias0_ref, do_ref,
             dq_ref, dkc_ref, dkp_ref, dvc_ref, dvp_ref, dqn_ref, dsink_ref):
        i = pl.program_id(0)
        bd = _head_mean_matrix()
        q, kband, vband = _attn_setup(zc_ref, zp_ref, cc_ref, cp_ref, sc_ref, sp_ref, qn_ref, kn_ref, bd)
        mask = jnp.where(i > 0, bias_ref[...], bias0_ref[...])

        @pl.when(i == 0)
        def _():
            dqn_ref[...] = jnp.zeros_like(dqn_ref)
            dsink_ref[...] = jnp.zeros_like(dsink_ref)

        dq = [None] * nt
        dk = jnp.zeros((2 * BLOCK, KV_W), F32)
        dv = jnp.zeros((2 * BLOCK, KV_W), F32)
        d_tiles = [do_ref[:, j * LANES:(j + 1) * LANES] for j in range(nt)]
        for kvh in range(N_Q_HEADS // Q_PER_KV):
            qs = _stack_heads([t[1] for t in q], kvh)
            probs, psink = _attn_probs(qs, kband, mask, _sink_column(sink_ref, kvh))
            dos = _stack_heads(d_tiles, kvh)
            dp = _dot(dos, vband, 1, 1)
            delta = jnp.sum(dp * probs, axis=1, keepdims=True)
            ds = (probs * (dp - delta)) * (HEAD_DIM ** -0.5)
            dsink = -psink * delta
            for g in range(Q_PER_KV):
                h = kvh * Q_PER_KV + g
                dsink_ref[h:h + 1, :] += jnp.broadcast_to(jnp.sum(dsink[g * BLOCK:(g + 1) * BLOCK], axis=0, keepdims=True), (1, LANES))
            _unstack_heads(_dot(ds, kband), kvh, dq)
            dk = dk + _dot(ds, qs, 0, 0)
            dv = dv + _dot(probs, dos, 0, 0)
        dgn = jnp.zeros((1, LANES), F32)
        for j in range(nt):
            t, _, n, r = q[j]
            dt, g = _norm_rope_bwd(dq[j], t, n, r, qn_ref[...], cc_ref[...], sc_ref[...], bd)
            dq_ref[:, j * LANES:(j + 1) * LANES] = dt.astype(BF16)
            dgn = dgn + g
        dqn_ref[...] += jnp.broadcast_to(dgn, (8, LANES))
        dkp_ref[...] = dk[:BLOCK]
        dkc_ref[...] = dk[BLOCK:]
        dvp_ref[...] = dv[:BLOCK]
        dvc_ref[...] = dv[BLOCK:]

        @pl.when(i == nb - 1)
        def _():
            acc = dqn_ref[...]
            dqn_ref[...] = acc + pltpu.roll(acc, HEAD_DIM, 1)

    blk = lambda w: pl.BlockSpec((BLOCK, w), lambda i: (i, 0))
    acc = pl.BlockSpec((8, LANES), lambda i: (0, 0))
    kv = jax.ShapeDtypeStruct((s, KV_W), F32)
    return pl.pallas_call(
        body, name=name, grid=(nb,), in_specs=_attn_specs(nb) + [blk(ATTN_W)],
        out_specs=[blk(ATTN_W), blk(KV_W), blk(KV_W), blk(KV_W), blk(KV_W), acc, acc],
        out_shape=[jax.ShapeDtypeStruct((s, ATTN_W), BF16), kv, kv, kv, kv,
                   jax.ShapeDtypeStruct((8, LANES), F32), jax.ShapeDtypeStruct((8, LANES), F32)],
        compiler_params=_params("arbitrary"),
    )(z, z, cosf, cosf, sinf, sinf, qn, kn, sinks, _window_bias(False), _window_bias(True), d_out)


def _kv_post(z, cosf, sinf, kn, dkc, dkp, dvc, dvp, dxp, dq, duv, dz, name):
    s = z.shape[0]
    nb = s // BLOCK

    def body(zk_ref, c_ref, s_ref, kn_ref, dkc_ref, dkp_ref, dvc_ref, dvp_ref, dxp_ref, dq_ref, duv_ref, dz_in,
             dz_ref, dkn_ref):
        j = pl.program_id(0)
        bd = _head_mean_matrix()
        last = j == nb - 1
        d = dkc_ref[...] + jnp.where(last, 0.0, dkp_ref[...])
        t = zk_ref[:, :KV_W]
        _, n, r = _norm_rope(t, kn_ref[...], c_ref[...], s_ref[...], bd)
        dt, g = _norm_rope_bwd(d, t, n, r, kn_ref[...], c_ref[...], s_ref[...], bd)
        dvv = dvc_ref[...] + jnp.where(last, 0.0, dvp_ref[...])
        dz_ref[:, 0:POOL_W] = dxp_ref[...]
        dz_ref[:, POOL_W:POOL_W + ATTN_W] = dq_ref[...]
        dz_ref[:, POOL_W + ATTN_W:POOL_W + ATTN_W + KV_W] = dt.astype(BF16)
        dz_ref[:, POOL_W + ATTN_W + KV_W:POOL_W + ATTN_W + 2 * KV_W] = dvv.astype(BF16)
        dz_ref[:, POOL_W + ATTN_W + 2 * KV_W:GATE_COL0] = duv_ref[...]

        @pl.when(j == 0)
        def _():
            dkn_ref[...] = jnp.zeros_like(dkn_ref)

        dkn_ref[...] += jnp.broadcast_to(g, (8, LANES))

        @pl.when(last)
        def _():
            acc = dkn_ref[...]
            dkn_ref[...] = acc + pltpu.roll(acc, HEAD_DIM, 1)

    cur = lambda w: pl.BlockSpec((BLOCK, w), lambda j: (j, 0))
    nxt = pl.BlockSpec((BLOCK, KV_W), lambda j: (jnp.minimum(j + 1, nb - 1), 0))
    vec = pl.BlockSpec((1, LANES), lambda j: (0, 0))
    return pl.pallas_call(
        body, name=name, grid=(nb,),
        in_specs=[pl.BlockSpec((BLOCK, 2 * KV_W), lambda j: (j, 3)), cur(LANES), cur(LANES), vec,
                  cur(KV_W), nxt, cur(KV_W), nxt, cur(POOL_W), cur(ATTN_W), cur(2 * SGU_W),
                  pl.BlockSpec(memory_space=pl.ANY)],
        out_specs=[pl.BlockSpec((BLOCK, GATE_COL0), lambda j: (j, 0)), pl.BlockSpec((8, LANES), lambda j: (0, 0))],
        out_shape=[jax.ShapeDtypeStruct(dz.shape, dz.dtype), jax.ShapeDtypeStruct((8, LANES), F32)],
        input_output_aliases={11: 0}, compiler_params=_params("arbitrary"),
    )(z, cosf, sinf, kn, dkc, dkp, dvc, dvp, dxp, dq, duv, dz)


def _sgu_setup(z_ref, ws_ref, vn_ref, bd):
    us = z_ref[:, :SGU_W]
    vs = z_ref[:, SGU_W:]
    ug, tu = _gelu(us)
    gv, tv = _gelu(vs)
    rr = jnp.concatenate([lax.rsqrt(_head_mean(gv[:, k * LANES:(k + 1) * LANES] ** 2, bd) + EPS) for k in range(2)], axis=1)
    vg = (gv * rr) * vn_ref[...]
    tril = lax.broadcasted_iota(jnp.int32, (BLOCK, BLOCK), 0) >= lax.broadcasted_iota(jnp.int32, (BLOCK, BLOCK), 1)
    w = [jnp.where(tril, ws_ref[g], 0.0).astype(BF16) for g in range(4)]
    return us, vs, ug, tu, gv, tv, rr, vg, w, tril


def _group_select(parts):
    lane = lax.broadcasted_iota(jnp.int32, parts[0].shape, 1)
    return _pool_lane_select(lane, *parts)


def _sgu_fwd(z, ws, bcol, vn, name):
    s = z.shape[0]
    nb = s // BLOCK

    def body(z_ref, ws_ref, b_ref, vn_ref, c_ref):
        bd = _head_mean_matrix()
        _, _, ug, _, _, _, _, vg, w, _ = _sgu_setup(z_ref, ws_ref, vn_ref, bd)
        sg = _group_select([_dot(w[g], vg) for g in range(4)]) + b_ref[...]
        c_ref[...] = (ug * sg).astype(BF16)

    return pl.pallas_call(
        body, name=name, grid=(nb,),
        in_specs=[pl.BlockSpec((BLOCK, 2 * SGU_W), lambda i: (i, 2)), pl.BlockSpec((4, BLOCK, BLOCK), lambda i: (0, 0, 0)),
                  pl.BlockSpec((BLOCK, SGU_W), lambda i: (0, 0)), pl.BlockSpec((1, SGU_W), lambda i: (0, 0))],
        out_specs=pl.BlockSpec((BLOCK, SGU_W), lambda i: (i, 0)),
        out_shape=jax.ShapeDtypeStruct((s, SGU_W), BF16), compiler_params=_params("parallel"),
    )(z, ws, bcol, vn)


def _sgu_bwd(z, ws, bcol, vn, dc, name):
    s = z.shape[0]
    nb = s // BLOCK

    def body(z_ref, ws_ref, b_ref, vn_ref, dc_ref, duv_ref, dws_ref, db_ref, dvn_ref):
        i = pl.program_id(0)
        bd = _head_mean_matrix()
        us, vs, ug, tu, gv, tv, rr, vg, w, tril = _sgu_setup(z_ref, ws_ref, vn_ref, bd)
        sg = _group_select([_dot(w[g], vg) for g in range(4)]) + b_ref[...]
        dcv = dc_ref[...]
        dug = dcv * sg
        dsg = dcv * ug
        lane = lax.broadcasted_iota(jnp.int32, dsg.shape, 1)

        @pl.when(i == 0)
        def _():
            dws_ref[...] = jnp.zeros_like(dws_ref)
            db_ref[...] = jnp.zeros_like(db_ref)
            dvn_ref[...] = jnp.zeros_like(dvn_ref)

        for g in range(4):
            dsg_g = jnp.where((lane >= g * HEAD_DIM) & (lane < (g + 1) * HEAD_DIM), dsg, 0.0)
            dws_ref[g] += jnp.where(tril, _dot(dsg_g, vg, 1, 1), 0.0)
        dvg = _group_select([_dot(w[g], dsg, 0, 0) for g in range(4)])
        db_ref[...] += dsg
        n = gv * rr
        part = jnp.sum(dvg * n, axis=0, keepdims=True)
        dvn_ref[...] += jnp.broadcast_to(part[:, :LANES] + part[:, LANES:], (8, LANES))
        u = dvg * vn_ref[...]
        tu_ = gv * u
        mean = jnp.concatenate([_head_mean(tu_[:, k * LANES:(k + 1) * LANES], bd) for k in range(2)], axis=1)
        dgv = rr * u - gv * (rr * rr * rr) * mean
        duv_ref[:, :SGU_W] = (dug * _gelu_grad(us, tu)).astype(BF16)
        duv_ref[:, SGU_W:] = (dgv * _gelu_grad(vs, tv)).astype(BF16)

        @pl.when(i == nb - 1)
        def _():
            acc = dvn_ref[...]
            dvn_ref[...] = acc + pltpu.roll(acc, HEAD_DIM, 1)
            for k in range(2):
                db_ref[:, k * LANES:(k + 1) * LANES] = _head_mean(db_ref[:, k * LANES:(k + 1) * LANES], bd) * float(HEAD_DIM)

    return pl.pallas_call(
        body, name=name, grid=(nb,),
        in_specs=[pl.BlockSpec((BLOCK, 2 * SGU_W), lambda i: (i, 2)), pl.BlockSpec((4, BLOCK, BLOCK), lambda i: (0, 0, 0)),
                  pl.BlockSpec((BLOCK, SGU_W), lambda i: (0, 0)), pl.BlockSpec((1, SGU_W), lambda i: (0, 0)),
                  pl.BlockSpec((BLOCK, SGU_W), lambda i: (i, 0))],
        out_specs=[pl.BlockSpec((BLOCK, 2 * SGU_W), lambda i: (i, 0)), pl.BlockSpec((4, BLOCK, BLOCK), lambda i: (0, 0, 0)),
                   pl.BlockSpec((BLOCK, SGU_W), lambda i: (0, 0)), pl.BlockSpec((8, LANES), lambda i: (0, 0))],
        out_shape=[jax.ShapeDtypeStruct((s, 2 * SGU_W), BF16), jax.ShapeDtypeStruct((4, BLOCK, BLOCK), F32),
                   jax.ShapeDtypeStruct((BLOCK, SGU_W), F32), jax.ShapeDtypeStruct((8, LANES), F32)],
        compiler_params=_params("arbitrary"),
    )(z, ws, bcol, vn, dc)


MERGE_TN = 512
MERGE_TM = 1024
MERGE_OUT_TM = 512


def _rms_rows(x, g):
    r = lax.rsqrt(jnp.mean(x * x, axis=-1, keepdims=True) + EPS)
    return ((x * r) * g).astype(BF16)


def _merge_out_fwd(a, b, c, wpa, wpb, wpc, z, w_out, res, gain, name):
    s = z.shape[0]
    tm = _tile(s, (MERGE_OUT_TM, BLOCK))
    gate0 = GATE_COL0 // MERGE_TN
    nn = D_MODEL // MERGE_TN

    def body(a_ref, b_ref, c_ref, wa_ref, wb_ref, wc_ref, g0_ref, g1_ref, g2_ref, wo_ref, res_ref, gain_ref,
             o_ref, m_ref, h_ref, acc_ref):
        n = pl.program_id(1)
        r = _sigmoid(g0_ref[...]) * _dot(a_ref[...], wa_ref[...])
        r = r + _sigmoid(g1_ref[...]) * _dot(b_ref[...], wb_ref[...])
        r = r + _sigmoid(g2_ref[...]) * _dot(c_ref[...], wc_ref[...])
        merged = r.astype(BF16)
        m_ref[...] = merged

        @pl.when(n == 0)
        def _():
            acc_ref[...] = jnp.zeros_like(acc_ref)

        acc_ref[...] += jnp.dot(merged, wo_ref[...], preferred_element_type=F32)

        @pl.when(n == nn - 1)
        def _():
            x = acc_ref[...] + res_ref[...]
            o_ref[...] = x
            h_ref[...] = _rms_rows(x, gain_ref[...])

    x_spec = lambda w: pl.BlockSpec((tm, w), lambda i, n: (i, 0))
    w_spec = lambda w: pl.BlockSpec((w, MERGE_TN), lambda i, n: (0, n))
    g_spec = lambda br: pl.BlockSpec((tm, MERGE_TN), lambda i, n: (i, gate0 + 2 * br + n))
    row = pl.BlockSpec((tm, D_MODEL), lambda i, n: (i, 0))
    return pl.pallas_call(
        body, name=name, grid=(s // tm, nn),
        in_specs=[x_spec(POOL_W), x_spec(ATTN_W), x_spec(SGU_W), w_spec(POOL_W), w_spec(ATTN_W), w_spec(SGU_W),
                  g_spec(0), g_spec(1), g_spec(2), pl.BlockSpec((MERGE_TN, D_MODEL), lambda i, n: (n, 0)), row,
                  pl.BlockSpec((1, D_MODEL), lambda i, n: (0, 0))],
        out_specs=[row, pl.BlockSpec((tm, MERGE_TN), lambda i, n: (i, n)), row],
        out_shape=[jax.ShapeDtypeStruct((s, D_MODEL), F32), jax.ShapeDtypeStruct((s, D_MODEL), BF16),
                   jax.ShapeDtypeStruct((s, D_MODEL), BF16)],
        scratch_shapes=[pltpu.VMEM((tm, D_MODEL), F32)], compiler_params=_params("parallel", "arbitrary"),
    )(a, b, c, wpa, wpb, wpc, z, z, z, w_out, res, gain.reshape(1, D_MODEL))


def _branch_bwd(br, xb, wp, z, dm, dz, name):
    s = z.shape[0]
    kb = xb.shape[1]
    tm = _tile(s, (MERGE_TM, BLOCK))
    gate0 = GATE_COL0 // MERGE_TN
    aliased = dz is not None

    def body(*refs):
        x_ref, w_ref, g_ref, dm_ref = refs[:4]
        dz_ref, dy_ref, dx_ref = refs[-3:]
        n = pl.program_id(1)
        y = _dot(x_ref[...], w_ref[...])
        sg = _sigmoid(g_ref[...])
        dmv = dm_ref[...]
        dy = (dmv * sg).astype(BF16)
        dy_ref[...] = dy
        dz_ref[...] = ((dmv * y) * (sg * (1.0 - sg))).astype(BF16)
        dx = _dot(dy, w_ref[...], 1, 1)

        @pl.when(n == 0)
        def _():
            dx_ref[...] = dx

        @pl.when(n > 0)
        def _():
            dx_ref[...] += dx

    in_specs = [pl.BlockSpec((tm, kb), lambda i, n: (i, 0)), pl.BlockSpec((kb, MERGE_TN), lambda i, n: (0, n)),
                pl.BlockSpec((tm, MERGE_TN), lambda i, n: (i, gate0 + 2 * br + n)),
                pl.BlockSpec((tm, MERGE_TN), lambda i, n: (i, n))]
    args = [xb, wp, z, dm]
    if aliased:
        in_specs.append(pl.BlockSpec(memory_space=pl.ANY))
        args.append(dz)
    return pl.pallas_call(
        body, name=name, grid=(s // tm, D_MODEL // MERGE_TN), in_specs=in_specs,
        out_specs=[pl.BlockSpec((tm, MERGE_TN), lambda i, n: (i, gate0 + 2 * br + n)),
                   pl.BlockSpec((tm, MERGE_TN), lambda i, n: (i, n)),
                   pl.BlockSpec((tm, kb), lambda i, n: (i, 0))],
        out_shape=[jax.ShapeDtypeStruct((s, IN_COLS), BF16), jax.ShapeDtypeStruct((s, D_MODEL), BF16),
                   jax.ShapeDtypeStruct((s, kb), F32)],
        input_output_aliases={4: 0} if aliased else {},
        compiler_params=_params("parallel", "arbitrary"),
    )(*args)


FFN_TM = 256
FFN_TC = 2816
FFN_STRIP = 256
FFN_UNROLL = 4


def _conv3(cur, prev, w_ref, b_ref):
    cat = jnp.concatenate([prev, cur], axis=0)
    x1 = pltpu.roll(cat, 1, 0)[HALO_CONV:]
    x2 = pltpu.roll(cat, 2, 0)[HALO_CONV:]
    return w_ref[0:1, :] * x2 + w_ref[1:2, :] * x1 + w_ref[2:3, :] * cur + b_ref[...], x1, x2


def _ffn_specs(tm):
    per = tm // HALO_CONV
    cur = pl.BlockSpec((tm, FFN_TC), lambda j, i: (i, j))
    prev = pl.BlockSpec((HALO_CONV, FFN_TC), lambda j, i: (jnp.maximum(i * per - 1, 0), j))
    w = pl.BlockSpec((3, FFN_TC), lambda j, i: (0, j))
    b = pl.BlockSpec((1, FFN_TC), lambda j, i: (0, j))
    return cur, prev, w, b


DOWN_TM = 256
DOWN_TK = 2816
DOWN_CHUNK = 256


def _ffn_down_fwd(upg, upv, cwg, cwv, cbg, cbv, w_down, res, gain, target, name):
    s = upg.shape[0]
    d = w_down.shape[1]
    tm = _tile(s, (DOWN_TM, DOWN_CHUNK, BLOCK))
    chunk = min(DOWN_CHUNK, tm)
    per = tm // HALO_CONV
    nk = D_FF // DOWN_TK
    last_layer = target is not None

    def body(g_ref, gp_ref, v_ref, vp_ref, wg_ref, wv_ref, bg_ref, bv_ref, w_ref, res_ref, extra_ref, o_ref, act_ref, tail_ref,
             acc_ref):
        first = pl.program_id(0) == 0
        kk = pl.program_id(1)

        @pl.when(kk == 0)
        def _():
            acc_ref[...] = jnp.zeros_like(acc_ref)

        for c in range(tm // chunk):
            rows = slice(c * chunk, (c + 1) * chunk)
            before = slice(c * chunk - HALO_CONV, c * chunk)
            g_p = jnp.where(first, 0.0, gp_ref[...]) if c == 0 else g_ref[before, :]
            v_p = jnp.where(first, 0.0, vp_ref[...]) if c == 0 else v_ref[before, :]
            gate = _conv3(g_ref[rows, :], g_p, wg_ref, bg_ref)[0]
            val = _conv3(v_ref[rows, :], v_p, wv_ref, bv_ref)[0]
            act = ((gate * _sigmoid(gate)) * val).astype(BF16)
            act_ref[rows, :] = act
            acc_ref[rows, :] += jnp.dot(act, w_ref[...], preferred_element_type=F32)

        @pl.when(kk == nk - 1)
        def _():
            x = acc_ref[...] + res_ref[...]
            if last_layer:
                err = x - extra_ref[...]
                o_ref[...] = err * (1.0 / d)
                part = jnp.sum(jnp.sum(err * err, axis=-1, keepdims=True) * (1.0 / d), axis=0, keepdims=True) * 0.5
                part = jnp.broadcast_to(part, (8, LANES))

                @pl.when(first)
                def _():
                    tail_ref[...] = part

                @pl.when(jnp.logical_not(first))
                def _():
                    tail_ref[...] += part
            else:
                o_ref[...] = x
                tail_ref[...] = _rms_rows(x, extra_ref[...])

    cur = pl.BlockSpec((tm, DOWN_TK), lambda i, kk: (i, kk))
    prev = pl.BlockSpec((HALO_CONV, DOWN_TK), lambda i, kk: (jnp.maximum(i * per - 1, 0), kk))
    w = pl.BlockSpec((3, DOWN_TK), lambda i, kk: (0, kk))
    b = pl.BlockSpec((1, DOWN_TK), lambda i, kk: (0, kk))
    row = pl.BlockSpec((tm, d), lambda i, kk: (i, 0))
    if last_layer:
        extra, extra_spec = target, row
        tail_spec, tail_shape = pl.BlockSpec((8, LANES), lambda i, kk: (0, 0)), jax.ShapeDtypeStruct((8, LANES), F32)
    else:
        extra, extra_spec = gain.reshape(1, d), pl.BlockSpec((1, d), lambda i, kk: (0, 0))
        tail_spec, tail_shape = row, jax.ShapeDtypeStruct((s, d), BF16)
    return pl.pallas_call(
        body, name=name, grid=(s // tm, nk),
        in_specs=[cur, prev, cur, prev, w, w, b, b, pl.BlockSpec((DOWN_TK, d), lambda i, kk: (kk, 0)), row, extra_spec],
        out_specs=[row, cur, tail_spec],
        out_shape=[jax.ShapeDtypeStruct((s, d), F32), jax.ShapeDtypeStruct((s, D_FF), BF16), tail_shape],
        scratch_shapes=[pltpu.VMEM((tm, d), F32)], compiler_params=_params("arbitrary", "arbitrary"),
    )(upg, upg, upv, upv, cwg, cwv, cbg, cbv, w_down, res, extra)


def _ffn_bwd(upg, upv, cwg, cwv, cbg, cbv, dact, name):
    s = upg.shape[0]
    tm = _tile(s, (FFN_TM, BLOCK))
    per = tm // HALO_CONV
    nrow = s // tm
    cur, prev, w, b = _ffn_specs(tm)
    nxt = pl.BlockSpec((HALO_CONV, FFN_TC), lambda j, i: (jnp.minimum((i + 1) * per, nrow * per - 1), j))

    nch = tm // 8
    rows8 = lambda r: pl.ds(pl.multiple_of(r * 8, 8), 8)

    def body(g_ref, gp_ref, gn_ref, v_ref, vp_ref, vn_ref, wg_ref, wv_ref, bg_ref, bv_ref, da_ref, dan_ref,
             d_ref, dwg_ref, dwv_ref, og_ref, ov_ref):
        i = pl.program_id(1)
        first = i == 0
        last = i == nrow - 1
        row = lax.broadcasted_iota(jnp.int32, (8, FFN_STRIP), 0)

        keep_down = {k: row >= k for k in (1, 2)}
        keep_up = {k: row < 8 - k for k in (1, 2)}

        def down(cur, prev, k):
            return jnp.where(keep_down[k], pltpu.roll(cur, k, 0), pltpu.roll(prev, k, 0))

        def up(cur, nxt, k):
            return jnp.where(keep_up[k], pltpu.roll(cur, 8 - k, 0), pltpu.roll(nxt, 8 - k, 0))

        @pl.when(first)
        def _():
            dwg_ref[...] = jnp.zeros_like(dwg_ref)
            dwv_ref[...] = jnp.zeros_like(dwv_ref)

        for c in range(FFN_TC // FFN_STRIP):
            cols = slice(c * FFN_STRIP, (c + 1) * FFN_STRIP)
            wg = [functools.partial(lambda k: wg_ref[k:k + 1, cols], k) for k in range(3)]
            wv = [functools.partial(lambda k: wv_ref[k:k + 1, cols], k) for k in range(3)]

            def conv_grads(g_cur, g_prev, v_cur, v_prev, da):
                gate = wg[0]() * down(g_cur, g_prev, 2) + wg[1]() * down(g_cur, g_prev, 1) + wg[2]() * g_cur + bg_ref[:, cols]
                val = wv[0]() * down(v_cur, v_prev, 2) + wv[1]() * down(v_cur, v_prev, 1) + wv[2]() * v_cur + bv_ref[:, cols]
                sg = _sigmoid(gate)
                return (da * val) * (sg * (1.0 + gate * (1.0 - sg))), da * (gate * sg)

            def passes(q, carry):
                for u in range(FFN_UNROLL):
                    carry = one_pass(q * FFN_UNROLL + u, carry)
                return carry

            def one_pass(r, carry, tile_end=False):
                dg_cur, dv_cur, acc = carry
                g_r, v_r = g_ref[rows8(r), cols], v_ref[rows8(r), cols]
                if tile_end:
                    g_n, v_n, da_n = gn_ref[:, cols], vn_ref[:, cols], jnp.where(last, 0.0, dan_ref[:, cols])
                else:
                    g_n, v_n, da_n = g_ref[rows8(r + 1), cols], v_ref[rows8(r + 1), cols], da_ref[rows8(r + 1), cols]
                dg_n, dv_n = conv_grads(g_n, g_r, v_n, v_r, da_n)
                new_acc = []
                for o_ref, w, d_cur, d_n, x0, a in ((og_ref, wg, dg_cur, dg_n, g_r, acc[:4]), (ov_ref, wv, dv_cur, dv_n, v_r, acc[4:])):
                    d1, d2 = up(d_cur, d_n, 1), up(d_cur, d_n, 2)
                    o_ref[rows8(r), cols] = w[2]() * d_cur + w[1]() * d1 + w[0]() * d2
                    new_acc += [a[0] + d2 * x0, a[1] + d1 * x0, a[2] + d_cur * x0, a[3] + d_cur]
                return dg_n, dv_n, tuple(new_acc)

            g_p = jnp.where(first, 0.0, gp_ref[:, cols])
            v_p = jnp.where(first, 0.0, vp_ref[:, cols])
            dg0, dv0 = conv_grads(g_ref[0:8, cols], g_p, v_ref[0:8, cols], v_p, da_ref[0:8, cols])
            zero = jnp.zeros((8, FFN_STRIP), F32)
            carry = lax.fori_loop(0, nch // FFN_UNROLL - 1, passes, (dg0, dv0, (zero,) * 8))
            for r in range(nch - FFN_UNROLL, nch):
                carry = one_pass(r, carry, tile_end=r == nch - 1)
            for ref, a in ((dwg_ref, carry[2][:4]), (dwv_ref, carry[2][4:])):
                for k in range(4):
                    ref[k:k + 1, cols] += jnp.sum(a[k], axis=0, keepdims=True)
        d_ref[:, :D_FF] = og_ref[...].astype(BF16)
        d_ref[:, D_FF:] = ov_ref[...].astype(BF16)

    assert FFN_TC == D_FF
    acc = pl.BlockSpec((8, FFN_TC), lambda j, i: (0, j))
    accs = jax.ShapeDtypeStruct((8, D_FF), F32)
    return pl.pallas_call(
        body, name=name, grid=(D_FF // FFN_TC, nrow), in_specs=[cur, prev, nxt, cur, prev, nxt, w, w, b, b, cur, nxt],
        out_specs=[pl.BlockSpec((tm, 2 * D_FF), lambda j, i: (i, 0)), acc, acc],
        out_shape=[jax.ShapeDtypeStruct((s, 2 * D_FF), BF16), accs, accs],
        scratch_shapes=[pltpu.VMEM((tm, FFN_TC), F32), pltpu.VMEM((tm, FFN_TC), F32)],
        compiler_params=_params("parallel", "arbitrary"),
    )(upg, upg, upg, upv, upv, upv, cwg, cwv, cbg, cbv, dact, dact)


def _mesh_place():
    return lax.axis_index("x"), lax.axis_index("y"), lax.axis_index("c")


def _all_gather(shards, name):
    na = len(shards)

    def body(*refs):
        x_refs, out_refs = refs[:na], refs[na:2 * na]
        send_sems, recv_sems, local_sems = refs[2 * na:]
        x, y, cc = _mesh_place()
        me, sibling = (x, y, cc), (x, y, 1 - cc)
        chips = [(1 - x, y), (x, 1 - y), (1 - x, 1 - y)]

        def copy(k, a, block, to, from_input=False):
            slot = out_refs[a].at[4 * block[0] + 2 * block[1] + block[2]]
            return pltpu.make_async_remote_copy(
                src_ref=x_refs[a] if from_input else slot, dst_ref=slot, send_sem=send_sems.at[k * na + a],
                recv_sem=recv_sems.at[k * na + a], device_id=to, device_id_type=pl.DeviceIdType.MESH)

        mine = [pltpu.make_async_copy(x_refs[a], out_refs[a].at[4 * x + 2 * y + cc], local_sems.at[a]) for a in range(na)]
        for cp in mine:
            cp.start()
        first = [copy(0, a, me, sibling, True) for a in range(na)]
        first += [copy(1 + j, a, me, (*chip, cc), True) for j, chip in enumerate(chips) for a in range(na)]
        for cp in first:
            cp.start()
        passed = []
        for j, chip in enumerate(chips):
            for a in range(na):
                copy(1 + j, a, (*chip, cc), me).wait_recv()
                passed.append(copy(4 + j, a, (*chip, cc), sibling))
                passed[-1].start()
        for a in range(na):
            copy(0, a, sibling, me).wait_recv()
        for j, chip in enumerate(chips):
            for a in range(na):
                copy(4 + j, a, (*chip, 1 - cc), me).wait_recv()
        for cp in first + passed:
            cp.wait_send()
        for cp in mine:
            cp.wait()

    hbm = pl.BlockSpec(memory_space=pl.ANY)
    return pl.pallas_call(
        body, name=name, out_shape=[jax.ShapeDtypeStruct((N_DEV,) + t.shape, t.dtype) for t in shards],
        in_specs=[hbm] * na, out_specs=[hbm] * na,
        scratch_shapes=[pltpu.SemaphoreType.DMA((7 * na,)), pltpu.SemaphoreType.DMA((7 * na,)), pltpu.SemaphoreType.DMA((na,))],
    )(*shards)


def _peer(k):
    x, y, cc = _mesh_place()
    px = 1 - x if (k >> 2) & 1 else x
    py = 1 - y if (k >> 1) & 1 else y
    pc = 1 - cc if k & 1 else cc
    return (px, py, pc), 4 * px + 2 * py + pc


def _push_copy(src_ref, land_ref, k, a, na, send_sems, recv_sems, indexed, landed):
    x, y, cc = _mesh_place()
    place, peer = _peer(k)
    sem = (k - 1) * na + a
    return pltpu.make_async_remote_copy(
        src_ref=src_ref.at[peer] if indexed else src_ref, dst_ref=land_ref.at[peer if landed else 4 * x + 2 * y + cc],
        send_sem=send_sems.at[sem], recv_sem=recv_sems.at[sem], device_id=place, device_id_type=pl.DeviceIdType.MESH)


_HBM = pl.BlockSpec(memory_space=pltpu.HBM)
_SEM = pl.BlockSpec(memory_space=pltpu.SEMAPHORE)
_EFFECT = pltpu.SideEffectType.DATAFLOW_SIDE_EFFECTING


def _push_start(srcs, indexed, name):
    na = len(srcs)
    lands = [lax.empty(t.shape if indexed else (N_DEV,) + t.shape, t.dtype) for t in srcs]

    def body(*refs):
        src_refs, land_refs = refs[:na], refs[na:2 * na]
        send_sems, recv_sems = refs[2 * na], refs[2 * na + 1]
        token = refs[-1]
        for k in range(1, N_DEV):
            for a in range(na):
                _push_copy(src_refs[a], land_refs[a], k, a, na, send_sems, recv_sems, indexed, False).start()
        token[...] = jnp.zeros_like(token)

    sems = pltpu.SemaphoreType.DMA((7 * na,))
    out = pl.pallas_call(
        body, name=name,
        out_shape=(sems, sems, *[pltpu.HBM(t.shape, t.dtype) for t in srcs], *[pltpu.HBM(t.shape, t.dtype) for t in lands],
                   jax.ShapeDtypeStruct((8, LANES), F32)),
        in_specs=[_HBM] * (2 * na), out_specs=(_SEM, _SEM, *[_HBM] * (2 * na), pl.BlockSpec(memory_space=pltpu.VMEM)),
        input_output_aliases={i: 2 + i for i in range(2 * na)},
        compiler_params=pltpu.CompilerParams(has_side_effects=_EFFECT),
    )(*[pltpu.with_memory_space_constraint(t, pltpu.HBM) for t in srcs + lands])
    return out[0], out[1], list(out[2:2 + na]), list(out[2 + na:2 + 2 * na]), out[-1]


def _push_wait(started, indexed, after, name):
    send_sems, recv_sems, srcs, lands, _ = started
    na = len(srcs)

    def body(*refs):
        src_refs, land_refs = refs[:na], refs[na:2 * na]
        send_sems, recv_sems = refs[2 * na], refs[2 * na + 1]
        for k in range(1, N_DEV):
            for a in range(na):
                copy = _push_copy(src_refs[a], land_refs[a], k, a, na, send_sems, recv_sems, indexed, True)
                copy.wait_send()
                copy.wait_recv()

    out = pl.pallas_call(
        body, name=name, out_shape=[pltpu.HBM(t.shape, t.dtype) for t in srcs + lands],
        in_specs=[_HBM] * (2 * na) + [_SEM, _SEM, pl.BlockSpec(memory_space=pl.ANY)], out_specs=[_HBM] * (2 * na),
        input_output_aliases={i: i for i in range(2 * na)},
        compiler_params=pltpu.CompilerParams(has_side_effects=_EFFECT),
    )(*srcs, *lands, send_sems, recv_sems, after)
    x, y, cc = _mesh_place()
    me = 4 * x + 2 * y + cc
    return [lax.dynamic_update_index_in_dim(
        land, lax.dynamic_index_in_dim(src, me, 0, keepdims=False) if indexed else src, me, 0)
        for src, land in zip(out[:na], out[na:])]


def _adamw_sum(parts, w, m, v, name):
    _, r, c = parts.shape
    tr = _tile(r, (256, 128, 64, 32, 16, 8))

    def body(p_ref, w_ref, m_ref, v_ref, g_ref, d_ref, nm_ref, nv_ref):
        _adam_store(_sum_parts(p_ref), w_ref, m_ref, v_ref, g_ref, d_ref, nm_ref, nv_ref)

    row = pl.BlockSpec((tr, c), lambda i: (i, 0))
    shp = jax.ShapeDtypeStruct((r, c), F32)
    return pl.pallas_call(
        body, name=name, grid=(r // tr,), in_specs=[pl.BlockSpec((N_DEV, tr, c), lambda i: (0, i, 0)), row, row, row],
        out_specs=[row, row, row, row], out_shape=[shp, shp, shp, shp], compiler_params=_params("parallel"),
    )(parts, w, m, v)


def _sum_parts(p_ref):
    g = p_ref[0].astype(F32)
    for k in range(1, N_DEV):
        g = g + p_ref[k].astype(F32)
    return g


def _adam_store(g, w_ref, m_ref, v_ref, g_ref, d_ref, nm_ref, nv_ref):
    nm = ADAM_B1 * m_ref[...] + (1.0 - ADAM_B1) * g
    nv = ADAM_B2 * v_ref[...] + (1.0 - ADAM_B2) * (g * g)
    m_hat = nm / (1.0 - ADAM_B1 ** ADAM_STEP)
    v_hat = nv / (1.0 - ADAM_B2 ** ADAM_STEP)
    g_ref[...] = g
    nm_ref[...] = nm
    nv_ref[...] = nv
    d_ref[...] = -ADAM_LR * (m_hat / (jnp.sqrt(v_hat) + ADAM_EPS) + ADAM_WD * w_ref[...])


def _adamw_weight(parts, w, m, v, name):
    _, r, c = w.shape
    tr = _tile(r, (256, 128, 176))
    nr = r // tr

    def body(p0_ref, p1_ref, w_ref, m_ref, v_ref, g_ref, d_ref, nm_ref, nv_ref):
        g = jnp.where(pl.program_id(0) == 0, _sum_parts(p0_ref), _sum_parts(p1_ref))
        _adam_store(g, w_ref, m_ref, v_ref, g_ref, d_ref, nm_ref, nv_ref)

    part = lambda layer: pl.BlockSpec(
        (N_DEV, tr, c), lambda l, i: (0, jnp.where(l == layer, i, (nr - 1) * (1 - layer)), 0))
    row = pl.BlockSpec((None, tr, c), lambda l, i: (l, i, 0))
    shp = jax.ShapeDtypeStruct(w.shape, F32)
    return pl.pallas_call(
        body, name=name, grid=(DEPTH, nr), in_specs=[part(0), part(1), row, row, row],
        out_specs=[row, row, row, row], out_shape=[shp, shp, shp, shp], compiler_params=_params("arbitrary", "arbitrary"),
    )(parts[0], parts[1], w, m, v)


def _full_to_slots(name, t):
    k, n = t.shape
    if name in ROW_SHARDED:
        return t.reshape(N_DEV, k // N_DEV, n)
    return t.reshape(k, N_DEV, n // N_DEV).transpose(1, 0, 2)


def _slots_to_full(name, t):
    _, r, c = t.shape
    if name in ROW_SHARDED:
        return t.reshape(N_DEV * r, c)
    return t.transpose(1, 0, 2).reshape(r, N_DEV * c)


def _small_sizes(shapes, names):
    return [(n, shapes[n], -(-int(math.prod(shapes[n])) // (8 * LANES)) * 8) for n in names]


def _pack_small(tree, shapes, names, row_tile):
    rows = []
    for n, shp, nrow in _small_sizes(shapes, names):
        flat = tree[n].reshape(-1)
        rows.append(jnp.pad(flat, (0, nrow * LANES - flat.shape[0])).reshape(nrow, LANES))
    total = sum(r.shape[0] for r in rows)
    if total % row_tile:
        rows.append(jnp.zeros((-total % row_tile, LANES), F32))
    return jnp.concatenate(rows, axis=0)


def _unpack_small(buf, shapes, names):
    out, r0 = {}, 0
    for n, shp, nrow in _small_sizes(shapes, names):
        out[n] = buf[r0:r0 + nrow].reshape(-1)[:int(math.prod(shp))].reshape(shp)
        r0 += nrow
    return out


def _block_diag(w):
    g = w.shape[0]
    eye = jnp.eye(g, dtype=w.dtype)
    return (eye[:, None, :, None] * w[:, :, None, :]).reshape(g * HEAD_DIM, g * HEAD_DIM)


def kernel(x, positions, norm1, w_in, q_norm, k_norm, sinks, w_pool, pool_scale, sgu_v_norm, w_s, b_s, w_proj_a, w_proj_b, w_proj_c, w_out, norm2, w_up, conv_w, conv_b, w_down, loss_target, m_norm1, m_w_in, m_q_norm, m_k_norm, m_sinks, m_w_pool, m_pool_scale, m_sgu_v_norm, m_w_s, m_b_s, m_w_proj_a, m_w_proj_b, m_w_proj_c, m_w_out, m_norm2, m_w_up, m_conv_w, m_conv_b, m_w_down, v_norm1, v_w_in, v_q_norm, v_k_norm, v_sinks, v_w_pool, v_pool_scale, v_sgu_v_norm, v_w_s, v_b_s, v_w_proj_a, v_w_proj_b, v_w_proj_c, v_w_out, v_norm2, v_w_up, v_conv_w, v_conv_b, v_w_down):
    names = ("norm1", "w_in", "q_norm", "k_norm", "sinks", "w_pool", "pool_scale", "sgu_v_norm", "w_s", "b_s", "w_proj_a",
             "w_proj_b", "w_proj_c", "w_out", "norm2", "w_up", "conv_w", "conv_b", "w_down")
    wts = dict(zip(names, (norm1, w_in, q_norm, k_norm, sinks, w_pool, pool_scale, sgu_v_norm, w_s, b_s, w_proj_a, w_proj_b,
                           w_proj_c, w_out, norm2, w_up, conv_w, conv_b, w_down)))
    mom = dict(zip(names, (m_norm1, m_w_in, m_q_norm, m_k_norm, m_sinks, m_w_pool, m_pool_scale, m_sgu_v_norm, m_w_s, m_b_s,
                           m_w_proj_a, m_w_proj_b, m_w_proj_c, m_w_out, m_norm2, m_w_up, m_conv_w, m_conv_b, m_w_down)))
    var = dict(zip(names, (v_norm1, v_w_in, v_q_norm, v_k_norm, v_sinks, v_w_pool, v_pool_scale, v_sgu_v_norm, v_w_s, v_b_s,
                           v_w_proj_a, v_w_proj_b, v_w_proj_c, v_w_out, v_norm2, v_w_up, v_conv_w, v_conv_b, v_w_down)))
    xs = x[0]
    target = loss_target[0]
    s = xs.shape[0]

    inv_freq = ROPE_THETA ** (-jnp.arange(0, HEAD_DIM, 2, dtype=F32) / HEAD_DIM)
    ang = positions[0].astype(F32)[:, None] * inv_freq
    cosf = jnp.tile(jnp.cos(ang), (1, 4))
    sinf = jnp.tile(jnp.concatenate([-jnp.sin(ang), jnp.sin(ang)], axis=1), (1, 2))

    local = [{n: wts[n][l] if n == "conv_w" else wts[n][l].astype(BF16) for n in SHARDED} for l in range(DEPTH)]
    later = SHARDED[1:]
    full = [{"w_in": _slots_to_full("w_in", _all_gather([local[0]["w_in"]], "gather_w_in_0")[0])}, None]
    gather0 = _push_start([local[0][n] for n in later], False, "gather_rest_0_start")
    norm1_first = norm1[0] + gather0[4][0, 0]

    def layer_consts(l):
        return dict(
            bdw=_block_diag(w_pool[l]).astype(BF16), qn=jnp.tile(q_norm[l], 2).reshape(1, LANES),
            kn=jnp.tile(k_norm[l], 2).reshape(1, LANES), vn=jnp.tile(sgu_v_norm[l], 4).reshape(1, SGU_W),
            bcol=jnp.repeat(b_s[l].T, HEAD_DIM, axis=1),
            cbg=conv_b[l][:D_FF].reshape(1, D_FF), cbv=conv_b[l][D_FF:].reshape(1, D_FF))

    gate_cols, val_cols = (0, D_FF), (D_FF, D_FF)

    saved = []
    cur = xs
    for l in range(DEPTH):
        if l == 1:
            landed = _push_wait(gather1, False, cur, "gather_weights_1_wait")
            full[1] = {n: _slots_to_full(n, t) for n, t in zip(SHARDED, landed)}
        fw, k = full[l], layer_consts(l)
        h1 = _rms_fwd(cur, norm1_first, "rms1_fwd_0") if l == 0 else h_next
        z = _mm(h1, fw["w_in"], name=f"in_proj_{l}")
        a = _pool_fwd(z, k["bdw"], pool_scale[l], f"pool_fwd_{l}")
        b = _attn_fwd(z, cosf, sinf, k["qn"], k["kn"], sinks[l], f"attn_fwd_{l}")
        c = _sgu_fwd(z, w_s[l], k["bcol"], k["vn"], f"sgu_fwd_{l}")
        w_proj_a_l = fw.get("w_proj_a")
        if l == 0:
            landed = _push_wait(gather0, False, c, "gather_rest_0_wait")
            fw.update({n: _slots_to_full(n, t) for n, t in zip(later, landed)})
            gather1 = _push_start([local[1][n] for n in SHARDED], False, "gather_weights_1_start")
            w_proj_a_l = fw["w_proj_a"] + gather1[4][0, 0].astype(BF16)
        x1, merged, h2 = _merge_out_fwd(a, b, c, w_proj_a_l, fw["w_proj_b"], fw["w_proj_c"], z, fw["w_out"], cur, norm2[l],
                                        f"merge_out_fwd_{l}")
        upg = _mm(h2, fw["w_up"], b_n=gate_cols, name=f"up_gate_{l}")
        upv = _mm(h2, fw["w_up"], b_n=val_cols, name=f"up_val_{l}")
        k["cwg"], k["cwv"] = fw["conv_w"][:, :D_FF], fw["conv_w"][:, D_FF:]
        last = l == DEPTH - 1
        x2, act, h_next = _ffn_down_fwd(upg, upv, k["cwg"], k["cwv"], k["cbg"], k["cbv"], fw["w_down"], x1,
                                        None if last else norm1[l + 1], target if last else None, f"ffn_down_fwd_{l}")
        saved.append(dict(x0=cur, h1=h1, z=z, a=a, b=b, c=c, merged=merged, x1=x1, h2=h2, upg=upg, upv=upv, act=act))
        cur = x2

    dcur, loss_tile = cur, h_next
    loss = lax.psum(loss_tile[0, 0], ("x", "y", "c"))

    gsmall = [None] * DEPTH
    small_shapes = {n: wts[n].shape for n in SMALL}

    def slots_of(grads):
        return [_full_to_slots(n, t) for n, t in grads.items()]

    for l in reversed(range(DEPTH)):
        fw, k, sv = full[l], layer_consts(l), saved[l]
        k["cwg"], k["cwv"] = fw["conv_w"][:, :D_FF], fw["conv_w"][:, D_FF:]
        staged = l == 0
        wgrad = functools.partial(_mm, ta=True, out_dtype=BF16)
        w_down_l = fw["w_down"] + exchange1[4][0, 0].astype(BF16) if staged else fw["w_down"]
        dact = _mm(dcur, w_down_l, tb=True, name=f"down_proj_bwd_{l}")
        g_down = wgrad(sv["act"], dcur, name=f"down_proj_wgrad_{l}")
        d_up, dcg, dcv = _ffn_bwd(sv["upg"], sv["upv"], k["cwg"], k["cwv"], k["cbg"], k["cbv"], dact, f"ffn_bwd_{l}")
        g_up = wgrad(sv["h2"], d_up, name=f"up_wgrad_{l}")
        g_ffn = dict(w_up=g_up, w_down=g_down, conv_w=jnp.concatenate([dcg[0:3], dcv[0:3]], axis=1))
        norm2_l = norm2[l]
        if staged:
            parts1 = dict(zip(SHARDED, _push_wait(exchange1, True, g_up, "exchange_grads_1_wait")))
            exchange_ffn = _push_start(slots_of(g_ffn), True, "exchange_ffn_0_start")
            norm2_l = norm2_l + exchange_ffn[4][0, 0]
        dx1, g_norm2 = _mm(d_up, fw["w_up"], tb=True, rms=(sv["x1"], norm2_l, dcur), name=f"up_bwd_{l}")
        dmerged = _mm(dx1, fw["w_out"], tb=True, name=f"out_proj_bwd_{l}")
        g_out = wgrad(sv["merged"], dx1, name=f"out_proj_wgrad_{l}")
        dz, dya, da = _branch_bwd(0, sv["a"], fw["w_proj_a"], sv["z"], dmerged, None, f"branch_a_bwd_{l}")
        dz, dyb, db = _branch_bwd(1, sv["b"], fw["w_proj_b"], sv["z"], dmerged, dz, f"branch_b_bwd_{l}")
        dz, dyc, dc = _branch_bwd(2, sv["c"], fw["w_proj_c"], sv["z"], dmerged, dz, f"branch_c_bwd_{l}")
        g_mix = dict(w_proj_a=wgrad(sv["a"], dya, name=f"proj_a_wgrad_{l}"), w_proj_b=wgrad(sv["b"], dyb, name=f"proj_b_wgrad_{l}"),
                     w_proj_c=wgrad(sv["c"], dyc, name=f"proj_c_wgrad_{l}"), w_out=g_out)
        pool_scale_l = pool_scale[l]
        if staged:
            exchange_mix = _push_start(slots_of(g_mix), True, "exchange_mixer_0_start")
            pool_scale_l = pool_scale_l + exchange_mix[4][0, 0]
        dxp, g_bdw, g_pscale = _pool_bwd(sv["z"], da, k["bdw"], pool_scale_l, f"pool_bwd_{l}")
        dq, dkc, dkp, dvc, dvp, g_qn, g_sink = _attn_bwd(sv["z"], cosf, sinf, k["qn"], k["kn"], sinks[l], db, f"attn_bwd_{l}")
        duv, g_ws, g_bacc, g_vn = _sgu_bwd(sv["z"], w_s[l], k["bcol"], k["vn"], dc, f"sgu_bwd_{l}")
        gsmall[l] = dict(
            q_norm=g_qn[0, :HEAD_DIM], sinks=g_sink[:, 0],
            w_pool=jnp.stack([g_bdw[g * HEAD_DIM:(g + 1) * HEAD_DIM, g * HEAD_DIM:(g + 1) * HEAD_DIM] for g in range(4)]),
            pool_scale=g_pscale[0], sgu_v_norm=g_vn[0, :HEAD_DIM], w_s=g_ws, b_s=g_bacc[:, ::HEAD_DIM].T,
            norm2=g_norm2[0], conv_b=jnp.concatenate([dcg[3], dcv[3]]))
        kn_l = k["kn"]
        if staged:
            early = _pack_small({n: jnp.stack([gsmall[i][n] for i in range(DEPTH)]) for n in SMALL_EARLY}, small_shapes,
                                SMALL_EARLY, SMALL_ROW_TILE)
            gather_early = _push_start([early], False, "gather_small_grads_start")
            kn_l = kn_l + gather_early[4][0, 0]
        dz, g_kn = _kv_post(sv["z"], cosf, sinf, kn_l, dkc, dkp, dvc, dvp, dxp, dq, duv, dz, f"kv_post_{l}")
        g_in = dict(w_in=wgrad(sv["h1"], dz, name=f"in_proj_wgrad_{l}"))
        norm1_l = norm1[l]
        if staged:
            exchange_in = _push_start(slots_of(g_in), True, "exchange_w_in_0_start")
            norm1_l = norm1_l + exchange_in[4][0, 0]
        dcur, g_norm1 = _mm(dz, fw["w_in"], tb=True, rms=(sv["x0"], norm1_l, dx1), name=f"in_proj_bwd_{l}")
        if not staged:
            exchange1 = _push_start(slots_of({n: {**g_in, **g_mix, **g_ffn}[n] for n in SHARDED}), True, "exchange_grads_1_start")
        gsmall[l].update(norm1=g_norm1[0], k_norm=g_kn[0, :HEAD_DIM])
    grad_x = dcur[None]

    def update_small(gathered, names, row_tile, name):
        pack = lambda tree: _pack_small(tree, small_shapes, names, row_tile)
        return [_unpack_small(t, small_shapes, names) for t in _adamw_sum(gathered, pack(wts), pack(mom), pack(var), name)]

    late = _pack_small({n: jnp.stack([gsmall[i][n] for i in range(DEPTH)]) for n in SMALL_LATE}, small_shapes, SMALL_LATE, 8)
    gather_late = _push_start([late], False, "gather_late_small_grads_start")

    parts0 = dict(zip(g_ffn, _push_wait(exchange_ffn, True, gather_late[4], "exchange_ffn_0_wait")))
    parts0.update(zip(g_mix, _push_wait(exchange_mix, True, gather_late[4], "exchange_mixer_0_wait")))
    update = lambda n: _adamw_weight([parts0[n], parts1[n]], wts[n], mom[n], var[n], f"adamw_{n}")
    big = {n: update(n) for n in SHARDED[1:]}
    parts0.update(zip(g_in, _push_wait(exchange_in, True, big["w_up"][0], "exchange_w_in_0_wait")))
    big["w_in"] = update("w_in")

    late_all = _push_wait(gather_late, False, big["w_in"][0], "gather_late_small_grads_wait")[0]
    small = update_small(late_all, SMALL_LATE, 8, "adamw_replicated_late")
    early_all = _push_wait(gather_early, False, big["w_in"][0], "gather_small_grads_wait")[0]
    for kind, tree in enumerate(update_small(early_all, SMALL_EARLY, SMALL_ROW_TILE, "adamw_replicated")):
        small[kind].update(tree)

    outs = [loss, grad_x]
    for kind in range(4):
        outs += [small[kind][n] if n in SMALL else big[n][kind] for n in names]
    return tuple(outs)
```

```python
import functools
import math

import jax
import jax.numpy as jnp
import numpy as np
from jax import lax
from jax.experimental import pallas as pl
from jax.experimental.pallas import tpu as pltpu

F32 = jnp.float32
BF16 = jnp.bfloat16

D_MODEL = 1024
DEPTH = 2
HEAD_DIM = 64
N_Q_HEADS = 8
Q_PER_KV = 4
BLOCK = 128
POOL_W = 256
ATTN_W = 512
KV_W = 128
SGU_W = 256
IN_COLS = 4608
GATE_COL0 = 1536
D_FF = 2816
EPS = 1e-6
ROPE_THETA = 10000.0
N_DEV = 8
LANES = 128
HALO_POOL = 16
HALO_CONV = 8

ADAM_LR = 0.001
ADAM_B1 = 0.9
ADAM_B2 = 0.999
ADAM_EPS = 1e-08
ADAM_WD = 0.01
ADAM_STEP = 10

VMEM_LIMIT = 56 * 1024 * 1024
MM_VMEM_BUDGET = 40 * 1024 * 1024

SHARDED = ("w_in", "w_proj_a", "w_proj_b", "w_proj_c", "w_out", "w_up", "w_down", "conv_w")
ROW_SHARDED = ("w_out", "w_down")
SMALL_ROW_TILE = 256
SMALL = ("norm1", "q_norm", "k_norm", "sinks", "w_pool", "pool_scale", "sgu_v_norm", "w_s", "b_s", "norm2", "conv_b")
SMALL_LATE = ("norm1", "k_norm")
SMALL_EARLY = tuple(n for n in SMALL if n not in SMALL_LATE)

_GELU_C = math.sqrt(2.0 / math.pi)
_GELU_A = 0.044715


def _params(*sem):
    return pltpu.CompilerParams(dimension_semantics=sem, vmem_limit_bytes=VMEM_LIMIT)


def _tile(n, prefs):
    for t in prefs:
        if t <= n and n % t == 0:
            return t
    return n


def _head_mean_matrix():
    r = lax.broadcasted_iota(jnp.int32, (LANES, LANES), 0)
    c = lax.broadcasted_iota(jnp.int32, (LANES, LANES), 1)
    return jnp.where((r >= HEAD_DIM) == (c >= HEAD_DIM), 1.0 / HEAD_DIM, 0.0).astype(BF16)


def _head_mean(v, bd):
    hi = v.astype(BF16)
    rest = v - hi.astype(F32)
    mid = rest.astype(BF16)
    lo = (rest - mid.astype(F32)).astype(BF16)
    mm = lambda p: jnp.dot(p, bd, preferred_element_type=F32)
    return mm(hi) + (mm(mid) + mm(lo))


def _rot_half(t):
    lane = lax.broadcasted_iota(jnp.int32, t.shape, 1)
    return jnp.where((lane & 32) == 0, pltpu.roll(t, LANES - 32, 1), pltpu.roll(t, 32, 1))


def _norm_rope(t, gn, cosf, sinf, bd):
    r = lax.rsqrt(_head_mean(t * t, bd) + EPS)
    n = t * r
    y = n * gn
    return y * cosf + _rot_half(y) * sinf, n, r


def _norm_rope_bwd(d, t, n, r, gn, cosf, sinf, bd):
    dy = d * cosf + _rot_half(d * sinf)
    dgn = jnp.sum(dy * n, axis=0, keepdims=True)
    u = dy * gn
    dt = r * u - t * (r * r * r) * _head_mean(t * u, bd)
    return dt, dgn


def _gelu(x):
    t = jnp.tanh(_GELU_C * (x + _GELU_A * (x * x * x)))
    return 0.5 * x * (1.0 + t), t


def _gelu_grad(x, t):
    return 0.5 * (1.0 + t) + 0.5 * x * (1.0 - t * t) * (_GELU_C * (1.0 + 3.0 * _GELU_A * x * x))


def _sigmoid(x):
    return jax.nn.sigmoid(x)


def _dot(a, b, ca=1, cb=0):
    return lax.dot_general(a.astype(BF16), b.astype(BF16), (((ca,), (cb,)), ((), ())), preferred_element_type=F32)


def _mm(a, b, *, ta=False, tb=False, add=None, out_dtype=F32, name, b_n=None, b_k=None, out_cols=None, out_into=None,
        rms=None):
    m = a.shape[1] if ta else a.shape[0]
    k = a.shape[0] if ta else a.shape[1]
    n = b_n[1] if b_n else (b.shape[0] if tb else b.shape[1])
    tn = _tile(n, (1024, 1152, 1408, 512, 256, 128))
    has_add = add is not None
    has_rms = rms is not None
    assert not has_rms or (tn == n and out_into is None)
    fits = []
    for tm in (2048, 1024, 1408, 512, 256, 128):
        for tk in (k, 4608, 2816, 2048, 1408, 1152, 1024, 512, 256, 128):
            if tm <= m and m % tm == 0 and tk <= k and k % tk == 0:
                need = (2 * (tm * tk * a.dtype.itemsize + tk * tn * b.dtype.itemsize) + 2 * tm * tn * jnp.dtype(out_dtype).itemsize
                        + 2 * tm * tn * 4 * (has_add + 2 * has_rms) + tm * tn * 4 * (tk < k))
                if need <= MM_VMEM_BUDGET:
                    fits.append((k // tk, -tm, tm, tk))
    if fits:
        _, _, tm, tk = min(fits)
    else:
        tm, tk = _tile(m, (256, 128)), _tile(k, (512, 256, 128))
    nk = k // tk
    n0 = b_n[0] // tn if b_n else 0
    k0 = b_k[0] // tk if b_k else 0
    o0, n_out = (out_cols[0] // tn, out_cols[1]) if out_cols else (0, n)
    n_in = 2 + has_add + (out_into is not None) + 3 * has_rms

    def body(*refs):
        a_ref, b_ref = refs[0], refs[1]
        add_ref = refs[2] if has_add else None
        o_ref = refs[n_in]

        def finish(r):
            if has_add:
                r = r + add_ref[...]
            if has_rms:
                x_ref, g_ref, dres_ref = refs[n_in - 3:n_in]
                dg_ref = refs[n_in + 1]
                x = x_ref[...]
                rr = lax.rsqrt(jnp.mean(x * x, axis=-1, keepdims=True) + EPS)
                u = r * g_ref[...]
                part = jnp.sum(r * (x * rr), axis=0, keepdims=True)
                r = dres_ref[...] + (rr * u - x * (rr * rr * rr) * jnp.mean(x * u, axis=-1, keepdims=True))

                @pl.when(pl.program_id(0) == 0)
                def _():
                    dg_ref[...] = part

                @pl.when(pl.program_id(0) > 0)
                def _():
                    dg_ref[...] += part

            o_ref[...] = r.astype(out_dtype)

        if nk == 1:
            finish(_dot(a_ref[...], b_ref[...], 0 if ta else 1, 1 if tb else 0))
        else:
            acc_ref = refs[-1]
            kk = pl.program_id(2)

            @pl.when(kk == 0)
            def _():
                acc_ref[...] = jnp.zeros_like(acc_ref)

            acc_ref[...] += _dot(a_ref[...], b_ref[...], 0 if ta else 1, 1 if tb else 0)

            @pl.when(kk == nk - 1)
            def _():
                finish(acc_ref[...])

    a_spec = pl.BlockSpec((tk, tm), lambda i, j, kk: (kk, i)) if ta else pl.BlockSpec((tm, tk), lambda i, j, kk: (i, kk))
    if tb:
        b_spec = pl.BlockSpec((tn, tk), lambda i, j, kk: (j + n0, kk + k0))
    else:
        b_spec = pl.BlockSpec((tk, tn), lambda i, j, kk: (kk + k0, j + n0))
    tile = pl.BlockSpec((tm, tn), lambda i, j, kk: (i, j))
    vec = pl.BlockSpec((1, tn), lambda i, j, kk: (0, j))
    in_specs = [a_spec, b_spec] + ([tile] if has_add else [])
    args = (a, b) + ((add,) if has_add else ())
    if out_into is not None:
        in_specs.append(pl.BlockSpec(memory_space=pl.ANY))
        args += (out_into,)
    out_specs = [pl.BlockSpec((tm, tn), lambda i, j, kk: (i, j + o0))]
    out_shape = [jax.ShapeDtypeStruct((m, n_out), out_dtype)]
    if has_rms:
        in_specs += [tile, vec, tile]
        args += (rms[0], rms[1].reshape(1, n), rms[2])
        out_specs.append(vec)
        out_shape.append(jax.ShapeDtypeStruct((1, n), F32))
    out = pl.pallas_call(
        body, name=name, grid=(m // tm, n // tn, nk), in_specs=in_specs, out_specs=out_specs, out_shape=out_shape,
        scratch_shapes=[pltpu.VMEM((tm, tn), F32)] if nk > 1 else [],
        input_output_aliases={n_in - 1: 0} if out_into is not None else {},
        compiler_params=_params("arbitrary" if has_rms else "parallel", "parallel", "arbitrary"),
    )(*args)
    return out if has_rms else out[0]


def _rms_fwd(x, g, name):
    s, d = x.shape
    tr = _tile(s, (512, 256, 128))

    def body(x_ref, g_ref, h_ref):
        xv = x_ref[...]
        r = lax.rsqrt(jnp.mean(xv * xv, axis=-1, keepdims=True) + EPS)
        h_ref[...] = ((xv * r) * g_ref[...]).astype(BF16)

    return pl.pallas_call(
        body, name=name, grid=(s // tr,),
        in_specs=[pl.BlockSpec((tr, d), lambda i: (i, 0)), pl.BlockSpec((1, d), lambda i: (0, 0))],
        out_specs=pl.BlockSpec((tr, d), lambda i: (i, 0)),
        out_shape=jax.ShapeDtypeStruct((s, d), BF16), compiler_params=_params("parallel"),
    )(x, g.reshape(1, d))


def _pool_lane_select(lane, v2, v4, v8, v16):
    return jnp.where(lane < 64, v2, jnp.where(lane < 128, v4, jnp.where(lane < 192, v8, v16)))


def _pool_diff(xc, xp, row0):
    n = BLOCK + HALO_POOL
    cat = jnp.concatenate([xp, xc], axis=0)
    s2 = cat + pltpu.roll(cat, 1, 0)
    s4 = s2 + pltpu.roll(s2, 2, 0)
    s8 = s4 + pltpu.roll(s4, 4, 0)
    s16 = s8 + pltpu.roll(s8, 8, 0)
    lane = lax.broadcasted_iota(jnp.int32, (n, POOL_W), 1)
    wsum = _pool_lane_select(lane, s2, s4, s8, s16)[HALO_POOL:]
    return wsum / _pool_count(row0, BLOCK) - xc


def _pool_count(row0, rows):
    lane = lax.broadcasted_iota(jnp.int32, (rows, POOL_W), 1)
    t = lax.broadcasted_iota(jnp.int32, (rows, POOL_W), 0) + row0
    return jnp.minimum(t + 1, _pool_lane_select(lane, 2, 4, 8, 16)).astype(F32)


def _pool_fwd(z, bdw, scale, name):
    s = z.shape[0]
    nb = s // BLOCK

    def body(xc_ref, xp_ref, w_ref, sc_ref, a_ref):
        i = pl.program_id(0)
        xp = jnp.where(i > 0, xp_ref[...], 0.0)
        diff = _pool_diff(xc_ref[...], xp, i * BLOCK)
        a_ref[...] = (_dot(diff, w_ref[...]) * sc_ref[...]).astype(BF16)

    return pl.pallas_call(
        body, name=name, grid=(nb,),
        in_specs=[pl.BlockSpec((BLOCK, POOL_W), lambda i: (i, 0)),
                  pl.BlockSpec((HALO_POOL, POOL_W), lambda i: (jnp.maximum(i * (BLOCK // HALO_POOL) - 1, 0), 0)),
                  pl.BlockSpec((POOL_W, POOL_W), lambda i: (0, 0)),
                  pl.BlockSpec((1, POOL_W), lambda i: (0, 0))],
        out_specs=pl.BlockSpec((BLOCK, POOL_W), lambda i: (i, 0)),
        out_shape=jax.ShapeDtypeStruct((s, POOL_W), BF16), compiler_params=_params("parallel"),
    )(z, z, bdw, scale.reshape(1, POOL_W))


def _pool_bwd(z, da, bdw, scale, name):
    s = z.shape[0]
    nb = s // BLOCK
    per = BLOCK // HALO_POOL
    n = BLOCK + HALO_POOL

    def body(xc_ref, xp_ref, dac_ref, dan_ref, w_ref, sc_ref, dx_ref, dw_ref, dsc_ref):
        i = pl.program_id(0)
        xp = jnp.where(i > 0, xp_ref[...], 0.0)
        diff = _pool_diff(xc_ref[...], xp, i * BLOCK)
        mixed = _dot(diff, w_ref[...])
        dac = dac_ref[...]
        dan = jnp.where(i < nb - 1, dan_ref[...], 0.0)
        dmix = jnp.concatenate([dac, dan], axis=0) * sc_ref[...]
        ddiff = _dot(dmix, w_ref[...], 1, 1)
        e = ddiff / _pool_count(i * BLOCK, n)
        f2 = e + pltpu.roll(e, n - 1, 0)
        f4 = f2 + pltpu.roll(f2, n - 2, 0)
        f8 = f4 + pltpu.roll(f4, n - 4, 0)
        f16 = f8 + pltpu.roll(f8, n - 8, 0)
        lane = lax.broadcasted_iota(jnp.int32, (n, POOL_W), 1)
        back = _pool_lane_select(lane, f2, f4, f8, f16)
        dx_ref[...] = (back[:BLOCK] - ddiff[:BLOCK]).astype(BF16)
        dw = _dot(diff, dmix[:BLOCK], 0, 0)
        dsc = jnp.sum(dac * mixed, axis=0, keepdims=True)

        @pl.when(i == 0)
        def _():
            dw_ref[...] = dw
            dsc_ref[...] = dsc

        @pl.when(i > 0)
        def _():
            dw_ref[...] += dw
            dsc_ref[...] += dsc

    blk = pl.BlockSpec((BLOCK, POOL_W), lambda i: (i, 0))
    return pl.pallas_call(
        body, name=name, grid=(nb,),
        in_specs=[blk, pl.BlockSpec((HALO_POOL, POOL_W), lambda i: (jnp.maximum(i * per - 1, 0), 0)),
                  blk, pl.BlockSpec((HALO_POOL, POOL_W), lambda i: (jnp.minimum((i + 1) * per, nb * per - 1), 0)),
                  pl.BlockSpec((POOL_W, POOL_W), lambda i: (0, 0)), pl.BlockSpec((1, POOL_W), lambda i: (0, 0))],
        out_specs=[blk, pl.BlockSpec((POOL_W, POOL_W), lambda i: (0, 0)), pl.BlockSpec((1, POOL_W), lambda i: (0, 0))],
        out_shape=[jax.ShapeDtypeStruct((s, POOL_W), BF16), jax.ShapeDtypeStruct((POOL_W, POOL_W), F32),
                   jax.ShapeDtypeStruct((1, POOL_W), F32)],
        compiler_params=_params("arbitrary"),
    )(z, z, da, da, bdw, scale.reshape(1, POOL_W))


def _attn_setup(zc_ref, zp_ref, cc_ref, cp_ref, sc_ref, sp_ref, qn_ref, kn_ref, bd):
    q = []
    for j in range(ATTN_W // LANES):
        t = zc_ref[:, POOL_W + j * LANES:POOL_W + (j + 1) * LANES]
        q.append((t,) + _norm_rope(t, qn_ref[...], cc_ref[...], sc_ref[...], bd))
    kc_raw = zc_ref[:, POOL_W + ATTN_W:POOL_W + ATTN_W + KV_W]
    kc = _norm_rope(kc_raw, kn_ref[...], cc_ref[...], sc_ref[...], bd)[0]
    kp = _norm_rope(zp_ref[:, :KV_W], kn_ref[...], cp_ref[...], sp_ref[...], bd)[0]
    kband = jnp.concatenate([kp, kc], axis=0).astype(BF16)
    vband = jnp.concatenate([zp_ref[:, KV_W:], zc_ref[:, POOL_W + ATTN_W + KV_W:POOL_W + ATTN_W + 2 * KV_W]], axis=0).astype(BF16)
    return q, kband, vband


MASKED = -1e30


def _window_bias(first_block):
    row = np.arange(Q_PER_KV * BLOCK)[:, None] % BLOCK
    col = np.arange(2 * BLOCK)[None, :]
    dist = row + BLOCK - col
    inside = (dist >= 0) & (dist < BLOCK) & ((col >= BLOCK) | (not first_block))
    return jnp.asarray(np.where(inside, 0.0, MASKED), F32)


def _stack_heads(tiles, kvh):
    return jnp.concatenate([_to_kv_lanes(tiles[h // 2], h) for h in range(kvh * Q_PER_KV, (kvh + 1) * Q_PER_KV)], axis=0)


def _unstack_heads(stacked, kvh, tiles):
    for g in range(Q_PER_KV):
        h = kvh * Q_PER_KV + g
        t = _from_kv_lanes(stacked[g * BLOCK:(g + 1) * BLOCK], h)
        tiles[h // 2] = t if tiles[h // 2] is None else tiles[h // 2] + t


def _sink_column(sink_ref, kvh):
    grp = lax.broadcasted_iota(jnp.int32, (Q_PER_KV * BLOCK, 1), 0) >> 7
    s = [sink_ref[kvh * Q_PER_KV + g] for g in range(Q_PER_KV)]
    return jnp.where(grp == 0, s[0], jnp.where(grp == 1, s[1], jnp.where(grp == 2, s[2], s[3])))


def _to_kv_lanes(t, h):
    kvh = h // Q_PER_KV
    if (h % 2) != kvh:
        t = pltpu.roll(t, HEAD_DIM, 1)
    lane = lax.broadcasted_iota(jnp.int32, t.shape, 1)
    return jnp.where((lane >= HEAD_DIM) == (kvh == 1), t, 0.0)


def _from_kv_lanes(t, h):
    kvh = h // Q_PER_KV
    lane = lax.broadcasted_iota(jnp.int32, t.shape, 1)
    t = jnp.where((lane >= HEAD_DIM) == (kvh == 1), t, 0.0)
    if (h % 2) != kvh:
        t = pltpu.roll(t, HEAD_DIM, 1)
    return t


def _attn_probs(qh, kband, bias, sink):
    sc = _dot(qh, kband, 1, 1) * (HEAD_DIM ** -0.5) + bias
    m = jnp.maximum(jnp.max(sc, axis=1, keepdims=True), sink)
    p = jnp.exp(sc - m)
    psink = jnp.exp(sink - m)
    den = jnp.sum(p, axis=1, keepdims=True) + psink
    return p / den, psink / den


def _attn_specs(nb):
    cur = lambda i: (i, 0)
    prev = lambda i: (jnp.maximum(i - 1, 0), 0)
    tab = lambda f: pl.BlockSpec((BLOCK, LANES), f)
    vec = pl.BlockSpec((1, LANES), lambda i: (0, 0))
    return [pl.BlockSpec((BLOCK, 1024), cur),
            pl.BlockSpec((BLOCK, 2 * KV_W), lambda i: (jnp.maximum(i - 1, 0), 3)),
            tab(cur), tab(prev), tab(cur), tab(prev), vec, vec,
            pl.BlockSpec(memory_space=pltpu.SMEM)] + [pl.BlockSpec((Q_PER_KV * BLOCK, 2 * BLOCK), lambda i: (0, 0))] * 2


def _attn_fwd(z, cosf, sinf, qn, kn, sinks, name):
    s = z.shape[0]
    nb = s // BLOCK

    def body(zc_ref, zp_ref, cc_ref, cp_ref, sc_ref, sp_ref, qn_ref, kn_ref, sink_ref, bias_ref, bias0_ref, o_ref):
        i = pl.program_id(0)
        bd = _head_mean_matrix()
        q, kband, vband = _attn_setup(zc_ref, zp_ref, cc_ref, cp_ref, sc_ref, sp_ref, qn_ref, kn_ref, bd)
        mask = jnp.where(i > 0, bias_ref[...], bias0_ref[...])
        out = [None] * (ATTN_W // LANES)
        for kvh in range(N_Q_HEADS // Q_PER_KV):
            qs = _stack_heads([t[1] for t in q], kvh)
            probs, _ = _attn_probs(qs, kband, mask, _sink_column(sink_ref, kvh))
            _unstack_heads(_dot(probs, vband), kvh, out)
        for j, o in enumerate(out):
            o_ref[:, j * LANES:(j + 1) * LANES] = o.astype(BF16)

    return pl.pallas_call(
        body, name=name, grid=(nb,), in_specs=_attn_specs(nb),
        out_specs=pl.BlockSpec((BLOCK, ATTN_W), lambda i: (i, 0)),
        out_shape=jax.ShapeDtypeStruct((s, ATTN_W), BF16), compiler_params=_params("parallel"),
    )(z, z, cosf, cosf, sinf, sinf, qn, kn, sinks, _window_bias(False), _window_bias(True))


def _attn_bwd(z, cosf, sinf, qn, kn, sinks, d_out, name):
    s = z.shape[0]
    nb = s // BLOCK
    nt = ATTN_W // LANES

    def body(zc_ref, zp_ref, cc_ref, cp_ref, sc_ref, sp_ref, qn_ref, kn_ref, sink_ref, bias_ref, bias0_ref, do_ref,
             dq_ref, dkc_ref, dkp_ref, dvc_ref, dvp_ref, dqn_ref, dsink_ref):
        i = pl.program_id(0)
        bd = _head_mean_matrix()
        q, kband, vband = _attn_setup(zc_ref, zp_ref, cc_ref, cp_ref, sc_ref, sp_ref, qn_ref, kn_ref, bd)
        mask = jnp.where(i > 0, bias_ref[...], bias0_ref[...])

        @pl.when(i == 0)
        def _():
            dqn_ref[...] = jnp.zeros_like(dqn_ref)
            dsink_ref[...] = jnp.zeros_like(dsink_ref)

        dq = [None] * nt
        dk = jnp.zeros((2 * BLOCK, KV_W), F32)
        dv = jnp.zeros((2 * BLOCK, KV_W), F32)
        d_tiles = [do_ref[:, j * LANES:(j + 1) * LANES] for j in range(nt)]
        for kvh in range(N_Q_HEADS // Q_PER_KV):
            qs = _stack_heads([t[1] for t in q], kvh)
            probs, psink = _attn_probs(qs, kband, mask, _sink_column(sink_ref, kvh))
            dos = _stack_heads(d_tiles, kvh)
            dp = _dot(dos, vband, 1, 1)
            delta = jnp.sum(dp * probs, axis=1, keepdims=True)
            ds = (probs * (dp - delta)) * (HEAD_DIM ** -0.5)
            dsink = -psink * delta
            for g in range(Q_PER_KV):
                h = kvh * Q_PER_KV + g
                dsink_ref[h:h + 1, :] += jnp.broadcast_to(jnp.sum(dsink[g * BLOCK:(g + 1) * BLOCK], axis=0, keepdims=True), (1, LANES))
            _unstack_heads(_dot(ds, kband), kvh, dq)
            dk = dk + _dot(ds, qs, 0, 0)
            dv = dv + _dot(probs, dos, 0, 0)
        dgn = jnp.zeros((1, LANES), F32)
        for j in range(nt):
            t, _, n, r = q[j]
            dt, g = _norm_rope_bwd(dq[j], t, n, r, qn_ref[...], cc_ref[...], sc_ref[...], bd)
            dq_ref[:, j * LANES:(j + 1) * LANES] = dt.astype(BF16)
            dgn = dgn + g
        dqn_ref[...] += jnp.broadcast_to(dgn, (8, LANES))
        dkp_ref[...] = dk[:BLOCK]
        dkc_ref[...] = dk[BLOCK:]
        dvp_ref[...] = dv[:BLOCK]
        dvc_ref[...] = dv[BLOCK:]

        @pl.when(i == nb - 1)
        def _():
            acc = dqn_ref[...]
            dqn_ref[...] = acc + pltpu.roll(acc, HEAD_DIM, 1)

    blk = lambda w: pl.BlockSpec((BLOCK, w), lambda i: (i, 0))
    acc = pl.BlockSpec((8, LANES), lambda i: (0, 0))
    kv = jax.ShapeDtypeStruct((s, KV_W), F32)
    return pl.pallas_call(
        body, name=name, grid=(nb,), in_specs=_attn_specs(nb) + [blk(ATTN_W)],
        out_specs=[blk(ATTN_W), blk(KV_W), blk(KV_W), blk(KV_W), blk(KV_W), acc, acc],
        out_shape=[jax.ShapeDtypeStruct((s, ATTN_W), BF16), kv, kv, kv, kv,
                   jax.ShapeDtypeStruct((8, LANES), F32), jax.ShapeDtypeStruct((8, LANES), F32)],
        compiler_params=_params("arbitrary"),
    )(z, z, cosf, cosf, sinf, sinf, qn, kn, sinks, _window_bias(False), _window_bias(True), d_out)


def _kv_post(z, cosf, sinf, kn, dkc, dkp, dvc, dvp, dxp, dq, duv, dz, name):
    s = z.shape[0]
    nb = s // BLOCK

    def body(zk_ref, c_ref, s_ref, kn_ref, dkc_ref, dkp_ref, dvc_ref, dvp_ref, dxp_ref, dq_ref, duv_ref, dz_in,
             dz_ref, dkn_ref):
        j = pl.program_id(0)
        bd = _head_mean_matrix()
        last = j == nb - 1
        d = dkc_ref[...] + jnp.where(last, 0.0, dkp_ref[...])
        t = zk_ref[:, :KV_W]
        _, n, r = _norm_rope(t, kn_ref[...], c_ref[...], s_ref[...], bd)
        dt, g = _norm_rope_bwd(d, t, n, r, kn_ref[...], c_ref[...], s_ref[...], bd)
        dvv = dvc_ref[...] + jnp.where(last, 0.0, dvp_ref[...])
        dz_ref[:, 0:POOL_W] = dxp_ref[...]
        dz_ref[:, POOL_W:POOL_W + ATTN_W] = dq_ref[...]
        dz_ref[:, POOL_W + ATTN_W:POOL_W + ATTN_W + KV_W] = dt.astype(BF16)
        dz_ref[:, POOL_W + ATTN_W + KV_W:POOL_W + ATTN_W + 2 * KV_W] = dvv.astype(BF16)
        dz_ref[:, POOL_W + ATTN_W + 2 * KV_W:GATE_COL0] = duv_ref[...]

        @pl.when(j == 0)
        def _():
            dkn_ref[...] = jnp.zeros_like(dkn_ref)

        dkn_ref[...] += jnp.broadcast_to(g, (8, LANES))

        @pl.when(last)
        def _():
            acc = dkn_ref[...]
            dkn_ref[...] = acc + pltpu.roll(acc, HEAD_DIM, 1)

    cur = lambda w: pl.BlockSpec((BLOCK, w), lambda j: (j, 0))
    nxt = pl.BlockSpec((BLOCK, KV_W), lambda j: (jnp.minimum(j + 1, nb - 1), 0))
    vec = pl.BlockSpec((1, LANES), lambda j: (0, 0))
    return pl.pallas_call(
        body, name=name, grid=(nb,),
        in_specs=[pl.BlockSpec((BLOCK, 2 * KV_W), lambda j: (j, 3)), cur(LANES), cur(LANES), vec,
                  cur(KV_W), nxt, cur(KV_W), nxt, cur(POOL_W), cur(ATTN_W), cur(2 * SGU_W),
                  pl.BlockSpec(memory_space=pl.ANY)],
        out_specs=[pl.BlockSpec((BLOCK, GATE_COL0), lambda j: (j, 0)), pl.BlockSpec((8, LANES), lambda j: (0, 0))],
        out_shape=[jax.ShapeDtypeStruct(dz.shape, dz.dtype), jax.ShapeDtypeStruct((8, LANES), F32)],
        input_output_aliases={11: 0}, compiler_params=_params("arbitrary"),
    )(z, cosf, sinf, kn, dkc, dkp, dvc, dvp, dxp, dq, duv, dz)


def _sgu_setup(z_ref, ws_ref, vn_ref, bd):
    us = z_ref[:, :SGU_W]
    vs = z_ref[:, SGU_W:]
    ug, tu = _gelu(us)
    gv, tv = _gelu(vs)
    rr = jnp.concatenate([lax.rsqrt(_head_mean(gv[:, k * LANES:(k + 1) * LANES] ** 2, bd) + EPS) for k in range(2)], axis=1)
    vg = (gv * rr) * vn_ref[...]
    tril = lax.broadcasted_iota(jnp.int32, (BLOCK, BLOCK), 0) >= lax.broadcasted_iota(jnp.int32, (BLOCK, BLOCK), 1)
    w = [jnp.where(tril, ws_ref[g], 0.0).astype(BF16) for g in range(4)]
    return us, vs, ug, tu, gv, tv, rr, vg, w, tril


def _group_select(parts):
    lane = lax.broadcasted_iota(jnp.int32, parts[0].shape, 1)
    return _pool_lane_select(lane, *parts)


def _sgu_fwd(z, ws, bcol, vn, name):
    s = z.shape[0]
    nb = s // BLOCK

    def body(z_ref, ws_ref, b_ref, vn_ref, c_ref):
        bd = _head_mean_matrix()
        _, _, ug, _, _, _, _, vg, w, _ = _sgu_setup(z_ref, ws_ref, vn_ref, bd)
        sg = _group_select([_dot(w[g], vg) for g in range(4)]) + b_ref[...]
        c_ref[...] = (ug * sg).astype(BF16)

    return pl.pallas_call(
        body, name=name, grid=(nb,),
        in_specs=[pl.BlockSpec((BLOCK, 2 * SGU_W), lambda i: (i, 2)), pl.BlockSpec((4, BLOCK, BLOCK), lambda i: (0, 0, 0)),
                  pl.BlockSpec((BLOCK, SGU_W), lambda i: (0, 0)), pl.BlockSpec((1, SGU_W), lambda i: (0, 0))],
        out_specs=pl.BlockSpec((BLOCK, SGU_W), lambda i: (i, 0)),
        out_shape=jax.ShapeDtypeStruct((s, SGU_W), BF16), compiler_params=_params("parallel"),
    )(z, ws, bcol, vn)


def _sgu_bwd(z, ws, bcol, vn, dc, name):
    s = z.shape[0]
    nb = s // BLOCK

    def body(z_ref, ws_ref, b_ref, vn_ref, dc_ref, duv_ref, dws_ref, db_ref, dvn_ref):
        i = pl.program_id(0)
        bd = _head_mean_matrix()
        us, vs, ug, tu, gv, tv, rr, vg, w, tril = _sgu_setup(z_ref, ws_ref, vn_ref, bd)
        sg = _group_select([_dot(w[g], vg) for g in range(4)]) + b_ref[...]
        dcv = dc_ref[...]
        dug = dcv * sg
        dsg = dcv * ug
        lane = lax.broadcasted_iota(jnp.int32, dsg.shape, 1)

        @pl.when(i == 0)
        def _():
            dws_ref[...] = jnp.zeros_like(dws_ref)
            db_ref[...] = jnp.zeros_like(db_ref)
            dvn_ref[...] = jnp.zeros_like(dvn_ref)

        for g in range(4):
            dsg_g = jnp.where((lane >= g * HEAD_DIM) & (lane < (g + 1) * HEAD_DIM), dsg, 0.0)
            dws_ref[g] += jnp.where(tril, _dot(dsg_g, vg, 1, 1), 0.0)
        dvg = _group_select([_dot(w[g], dsg, 0, 0) for g in range(4)])
        db_ref[...] += dsg
        n = gv * rr
        part = jnp.sum(dvg * n, axis=0, keepdims=True)
        dvn_ref[...] += jnp.broadcast_to(part[:, :LANES] + part[:, LANES:], (8, LANES))
        u = dvg * vn_ref[...]
        tu_ = gv * u
        mean = jnp.concatenate([_head_mean(tu_[:, k * LANES:(k + 1) * LANES], bd) for k in range(2)], axis=1)
        dgv = rr * u - gv * (rr * rr * rr) * mean
        duv_ref[:, :SGU_W] = (dug * _gelu_grad(us, tu)).astype(BF16)
        duv_ref[:, SGU_W:] = (dgv * _gelu_grad(vs, tv)).astype(BF16)

        @pl.when(i == nb - 1)
        def _():
            acc = dvn_ref[...]
            dvn_ref[...] = acc + pltpu.roll(acc, HEAD_DIM, 1)
            for k in range(2):
                db_ref[:, k * LANES:(k + 1) * LANES] = _head_mean(db_ref[:, k * LANES:(k + 1) * LANES], bd) * float(HEAD_DIM)

    return pl.pallas_call(
        body, name=name, grid=(nb,),
        in_specs=[pl.BlockSpec((BLOCK, 2 * SGU_W), lambda i: (i, 2)), pl.BlockSpec((4, BLOCK, BLOCK), lambda i: (0, 0, 0)),
                  pl.BlockSpec((BLOCK, SGU_W), lambda i: (0, 0)), pl.BlockSpec((1, SGU_W), lambda i: (0, 0)),
                  pl.BlockSpec((BLOCK, SGU_W), lambda i: (i, 0))],
        out_specs=[pl.BlockSpec((BLOCK, 2 * SGU_W), lambda i: (i, 0)), pl.BlockSpec((4, BLOCK, BLOCK), lambda i: (0, 0, 0)),
                   pl.BlockSpec((BLOCK, SGU_W), lambda i: (0, 0)), pl.BlockSpec((8, LANES), lambda i: (0, 0))],
        out_shape=[jax.ShapeDtypeStruct((s, 2 * SGU_W), BF16), jax.ShapeDtypeStruct((4, BLOCK, BLOCK), F32),
                   jax.ShapeDtypeStruct((BLOCK, SGU_W), F32), jax.ShapeDtypeStruct((8, LANES), F32)],
        compiler_params=_params("arbitrary"),
    )(z, ws, bcol, vn, dc)


MERGE_TN = 512
MERGE_TM = 1024
MERGE_OUT_TM = 1024


def _rms_rows(x, g):
    r = lax.rsqrt(jnp.mean(x * x, axis=-1, keepdims=True) + EPS)
    return ((x * r) * g).astype(BF16)


def _merge_out_fwd(a, b, c, wpa, wpb, wpc, z, w_out, res, gain, name):
    s = z.shape[0]
    tm = _tile(s, (MERGE_OUT_TM, BLOCK))
    gate0 = GATE_COL0 // MERGE_TN
    nn = D_MODEL // MERGE_TN

    def body(a_ref, b_ref, c_ref, wa_ref, wb_ref, wc_ref, g0_ref, g1_ref, g2_ref, wo_ref, res_ref, gain_ref,
             o_ref, m_ref, h_ref, acc_ref):
        n = pl.program_id(1)
        r = _sigmoid(g0_ref[...]) * _dot(a_ref[...], wa_ref[...])
        r = r + _sigmoid(g1_ref[...]) * _dot(b_ref[...], wb_ref[...])
        r = r + _sigmoid(g2_ref[...]) * _dot(c_ref[...], wc_ref[...])
        merged = r.astype(BF16)
        m_ref[...] = merged

        @pl.when(n == 0)
        def _():
            acc_ref[...] = jnp.zeros_like(acc_ref)

        acc_ref[...] += jnp.dot(merged, wo_ref[...], preferred_element_type=F32)

        @pl.when(n == nn - 1)
        def _():
            x = acc_ref[...] + res_ref[...]
            o_ref[...] = x
            h_ref[...] = _rms_rows(x, gain_ref[...])

    x_spec = lambda w: pl.BlockSpec((tm, w), lambda i, n: (i, 0))
    w_spec = lambda w: pl.BlockSpec((w, MERGE_TN), lambda i, n: (0, n))
    g_spec = lambda br: pl.BlockSpec((tm, MERGE_TN), lambda i, n: (i, gate0 + 2 * br + n))
    row = pl.BlockSpec((tm, D_MODEL), lambda i, n: (i, 0))
    return pl.pallas_call(
        body, name=name, grid=(s // tm, nn),
        in_specs=[x_spec(POOL_W), x_spec(ATTN_W), x_spec(SGU_W), w_spec(POOL_W), w_spec(ATTN_W), w_spec(SGU_W),
                  g_spec(0), g_spec(1), g_spec(2), pl.BlockSpec((MERGE_TN, D_MODEL), lambda i, n: (n, 0)), row,
                  pl.BlockSpec((1, D_MODEL), lambda i, n: (0, 0))],
        out_specs=[row, pl.BlockSpec((tm, MERGE_TN), lambda i, n: (i, n)), row],
        out_shape=[jax.ShapeDtypeStruct((s, D_MODEL), F32), jax.ShapeDtypeStruct((s, D_MODEL), BF16),
                   jax.ShapeDtypeStruct((s, D_MODEL), BF16)],
        scratch_shapes=[pltpu.VMEM((tm, D_MODEL), F32)], compiler_params=_params("parallel", "arbitrary"),
    )(a, b, c, wpa, wpb, wpc, z, z, z, w_out, res, gain.reshape(1, D_MODEL))


def _branch_bwd(br, xb, wp, z, dm, dz, name):
    s = z.shape[0]
    kb = xb.shape[1]
    tm = _tile(s, (MERGE_TM, BLOCK))
    gate0 = GATE_COL0 // MERGE_TN
    aliased = dz is not None

    def body(*refs):
        x_ref, w_ref, g_ref, dm_ref = refs[:4]
        dz_ref, dy_ref, dx_ref = refs[-3:]
        n = pl.program_id(1)
        y = _dot(x_ref[...], w_ref[...])
        sg = _sigmoid(g_ref[...])
        dmv = dm_ref[...]
        dy = (dmv * sg).astype(BF16)
        dy_ref[...] = dy
        dz_ref[...] = ((dmv * y) * (sg * (1.0 - sg))).astype(BF16)
        dx = _dot(dy, w_ref[...], 1, 1)

        @pl.when(n == 0)
        def _():
            dx_ref[...] = dx

        @pl.when(n > 0)
        def _():
            dx_ref[...] += dx

    in_specs = [pl.BlockSpec((tm, kb), lambda i, n: (i, 0)), pl.BlockSpec((kb, MERGE_TN), lambda i, n: (0, n)),
                pl.BlockSpec((tm, MERGE_TN), lambda i, n: (i, gate0 + 2 * br + n)),
                pl.BlockSpec((tm, MERGE_TN), lambda i, n: (i, n))]
    args = [xb, wp, z, dm]
    if aliased:
        in_specs.append(pl.BlockSpec(memory_space=pl.ANY))
        args.append(dz)
    return pl.pallas_call(
        body, name=name, grid=(s // tm, D_MODEL // MERGE_TN), in_specs=in_specs,
        out_specs=[pl.BlockSpec((tm, MERGE_TN), lambda i, n: (i, gate0 + 2 * br + n)),
                   pl.BlockSpec((tm, MERGE_TN), lambda i, n: (i, n)),
                   pl.BlockSpec((tm, kb), lambda i, n: (i, 0))],
        out_shape=[jax.ShapeDtypeStruct((s, IN_COLS), BF16), jax.ShapeDtypeStruct((s, D_MODEL), BF16),
                   jax.ShapeDtypeStruct((s, kb), F32)],
        input_output_aliases={4: 0} if aliased else {},
        compiler_params=_params("parallel", "arbitrary"),
    )(*args)


FFN_TM = 256
FFN_TC = 2816
FFN_STRIP = 256
FFN_UNROLL = 4


def _conv3(cur, prev, w_ref, b_ref):
    cat = jnp.concatenate([prev, cur], axis=0)
    x1 = pltpu.roll(cat, 1, 0)[HALO_CONV:]
    x2 = pltpu.roll(cat, 2, 0)[HALO_CONV:]
    return w_ref[0:1, :] * x2 + w_ref[1:2, :] * x1 + w_ref[2:3, :] * cur + b_ref[...], x1, x2


def _ffn_specs(tm):
    per = tm // HALO_CONV
    cur = pl.BlockSpec((tm, FFN_TC), lambda j, i: (i, j))
    prev = pl.BlockSpec((HALO_CONV, FFN_TC), lambda j, i: (jnp.maximum(i * per - 1, 0), j))
    w = pl.BlockSpec((3, FFN_TC), lambda j, i: (0, j))
    b = pl.BlockSpec((1, FFN_TC), lambda j, i: (0, j))
    return cur, prev, w, b


DOWN_TM = 256
DOWN_TK = 2816
DOWN_CHUNK = 256


def _ffn_down_fwd(upg, upv, cwg, cwv, cbg, cbv, w_down, res, gain, target, name):
    s = upg.shape[0]
    d = w_down.shape[1]
    tm = _tile(s, (DOWN_TM, DOWN_CHUNK, BLOCK))
    chunk = min(DOWN_CHUNK, tm)
    per = tm // HALO_CONV
    nk = D_FF // DOWN_TK
    last_layer = target is not None

    def body(g_ref, gp_ref, v_ref, vp_ref, wg_ref, wv_ref, bg_ref, bv_ref, w_ref, res_ref, extra_ref, o_ref, act_ref, tail_ref,
             acc_ref):
        first = pl.program_id(0) == 0
        kk = pl.program_id(1)

        @pl.when(kk == 0)
        def _():
            acc_ref[...] = jnp.zeros_like(acc_ref)

        for c in range(tm // chunk):
            rows = slice(c * chunk, (c + 1) * chunk)
            before = slice(c * chunk - HALO_CONV, c * chunk)
            g_p = jnp.where(first, 0.0, gp_ref[...]) if c == 0 else g_ref[before, :]
            v_p = jnp.where(first, 0.0, vp_ref[...]) if c == 0 else v_ref[before, :]
            gate = _conv3(g_ref[rows, :], g_p, wg_ref, bg_ref)[0]
            val = _conv3(v_ref[rows, :], v_p, wv_ref, bv_ref)[0]
            act = ((gate * _sigmoid(gate)) * val).astype(BF16)
            act_ref[rows, :] = act
            acc_ref[rows, :] += jnp.dot(act, w_ref[...], preferred_element_type=F32)

        @pl.when(kk == nk - 1)
        def _():
            x = acc_ref[...] + res_ref[...]
            if last_layer:
                err = x - extra_ref[...]
                o_ref[...] = err * (1.0 / d)
                part = jnp.sum(jnp.sum(err * err, axis=-1, keepdims=True) * (1.0 / d), axis=0, keepdims=True) * 0.5
                part = jnp.broadcast_to(part, (8, LANES))

                @pl.when(first)
                def _():
                    tail_ref[...] = part

                @pl.when(jnp.logical_not(first))
                def _():
                    tail_ref[...] += part
            else:
                o_ref[...] = x
                tail_ref[...] = _rms_rows(x, extra_ref[...])

    cur = pl.BlockSpec((tm, DOWN_TK), lambda i, kk: (i, kk))
    prev = pl.BlockSpec((HALO_CONV, DOWN_TK), lambda i, kk: (jnp.maximum(i * per - 1, 0), kk))
    w = pl.BlockSpec((3, DOWN_TK), lambda i, kk: (0, kk))
    b = pl.BlockSpec((1, DOWN_TK), lambda i, kk: (0, kk))
    row = pl.BlockSpec((tm, d), lambda i, kk: (i, 0))
    if last_layer:
        extra, extra_spec = target, row
        tail_spec, tail_shape = pl.BlockSpec((8, LANES), lambda i, kk: (0, 0)), jax.ShapeDtypeStruct((8, LANES), F32)
    else:
        extra, extra_spec = gain.reshape(1, d), pl.BlockSpec((1, d), lambda i, kk: (0, 0))
        tail_spec, tail_shape = row, jax.ShapeDtypeStruct((s, d), BF16)
    return pl.pallas_call(
        body, name=name, grid=(s // tm, nk),
        in_specs=[cur, prev, cur, prev, w, w, b, b, pl.BlockSpec((DOWN_TK, d), lambda i, kk: (kk, 0)), row, extra_spec],
        out_specs=[row, cur, tail_spec],
        out_shape=[jax.ShapeDtypeStruct((s, d), F32), jax.ShapeDtypeStruct((s, D_FF), BF16), tail_shape],
        scratch_shapes=[pltpu.VMEM((tm, d), F32)], compiler_params=_params("arbitrary", "arbitrary"),
    )(upg, upg, upv, upv, cwg, cwv, cbg, cbv, w_down, res, extra)


def _ffn_bwd(upg, upv, cwg, cwv, cbg, cbv, dact, name):
    s = upg.shape[0]
    tm = _tile(s, (FFN_TM, BLOCK))
    per = tm // HALO_CONV
    nrow = s // tm
    cur, prev, w, b = _ffn_specs(tm)
    nxt = pl.BlockSpec((HALO_CONV, FFN_TC), lambda j, i: (jnp.minimum((i + 1) * per, nrow * per - 1), j))

    nch = tm // 8
    rows8 = lambda r: pl.ds(pl.multiple_of(r * 8, 8), 8)

    def body(g_ref, gp_ref, gn_ref, v_ref, vp_ref, vn_ref, wg_ref, wv_ref, bg_ref, bv_ref, da_ref, dan_ref,
             d_ref, dwg_ref, dwv_ref, og_ref, ov_ref):
        i = pl.program_id(1)
        first = i == 0
        last = i == nrow - 1
        row = lax.broadcasted_iota(jnp.int32, (8, FFN_STRIP), 0)

        keep_down = {k: row >= k for k in (1, 2)}
        keep_up = {k: row < 8 - k for k in (1, 2)}

        def down(cur, prev, k):
            return jnp.where(keep_down[k], pltpu.roll(cur, k, 0), pltpu.roll(prev, k, 0))

        def up(cur, nxt, k):
            return jnp.where(keep_up[k], pltpu.roll(cur, 8 - k, 0), pltpu.roll(nxt, 8 - k, 0))

        @pl.when(first)
        def _():
            dwg_ref[...] = jnp.zeros_like(dwg_ref)
            dwv_ref[...] = jnp.zeros_like(dwv_ref)

        for c in range(FFN_TC // FFN_STRIP):
            cols = slice(c * FFN_STRIP, (c + 1) * FFN_STRIP)
            wg = [functools.partial(lambda k: wg_ref[k:k + 1, cols], k) for k in range(3)]
            wv = [functools.partial(lambda k: wv_ref[k:k + 1, cols], k) for k in range(3)]

            def conv_grads(g_cur, g_prev, v_cur, v_prev, da):
                gate = wg[0]() * down(g_cur, g_prev, 2) + wg[1]() * down(g_cur, g_prev, 1) + wg[2]() * g_cur + bg_ref[:, cols]
                val = wv[0]() * down(v_cur, v_prev, 2) + wv[1]() * down(v_cur, v_prev, 1) + wv[2]() * v_cur + bv_ref[:, cols]
                sg = _sigmoid(gate)
                return (da * val) * (sg * (1.0 + gate * (1.0 - sg))), da * (gate * sg)

            def passes(q, carry):
                for u in range(FFN_UNROLL):
                    carry = one_pass(q * FFN_UNROLL + u, carry)
                return carry

            def one_pass(r, carry, tile_end=False):
                dg_cur, dv_cur, acc = carry
                g_r, v_r = g_ref[rows8(r), cols], v_ref[rows8(r), cols]
                if tile_end:
                    g_n, v_n, da_n = gn_ref[:, cols], vn_ref[:, cols], jnp.where(last, 0.0, dan_ref[:, cols])
                else:
                    g_n, v_n, da_n = g_ref[rows8(r + 1), cols], v_ref[rows8(r + 1), cols], da_ref[rows8(r + 1), cols]
                dg_n, dv_n = conv_grads(g_n, g_r, v_n, v_r, da_n)
                new_acc = []
                for o_ref, w, d_cur, d_n, x0, a in ((og_ref, wg, dg_cur, dg_n, g_r, acc[:4]), (ov_ref, wv, dv_cur, dv_n, v_r, acc[4:])):
                    d1, d2 = up(d_cur, d_n, 1), up(d_cur, d_n, 2)
                    o_ref[rows8(r), cols] = w[2]() * d_cur + w[1]() * d1 + w[0]() * d2
                    new_acc += [a[0] + d2 * x0, a[1] + d1 * x0, a[2] + d_cur * x0, a[3] + d_cur]
                return dg_n, dv_n, tuple(new_acc)

            g_p = jnp.where(first, 0.0, gp_ref[:, cols])
            v_p = jnp.where(first, 0.0, vp_ref[:, cols])
            dg0, dv0 = conv_grads(g_ref[0:8, cols], g_p, v_ref[0:8, cols], v_p, da_ref[0:8, cols])
            zero = jnp.zeros((8, FFN_STRIP), F32)
            carry = lax.fori_loop(0, nch // FFN_UNROLL - 1, passes, (dg0, dv0, (zero,) * 8))
            for r in range(nch - FFN_UNROLL, nch):
                carry = one_pass(r, carry, tile_end=r == nch - 1)
            for ref, a in ((dwg_ref, carry[2][:4]), (dwv_ref, carry[2][4:])):
                for k in range(4):
                    ref[k:k + 1, cols] += jnp.sum(a[k], axis=0, keepdims=True)
        d_ref[:, :D_FF] = og_ref[...].astype(BF16)
        d_ref[:, D_FF:] = ov_ref[...].astype(BF16)

    assert FFN_TC == D_FF
    acc = pl.BlockSpec((8, FFN_TC), lambda j, i: (0, j))
    accs = jax.ShapeDtypeStruct((8, D_FF), F32)
    return pl.pallas_call(
        body, name=name, grid=(D_FF // FFN_TC, nrow), in_specs=[cur, prev, nxt, cur, prev, nxt, w, w, b, b, cur, nxt],
        out_specs=[pl.BlockSpec((tm, 2 * D_FF), lambda j, i: (i, 0)), acc, acc],
        out_shape=[jax.ShapeDtypeStruct((s, 2 * D_FF), BF16), accs, accs],
        scratch_shapes=[pltpu.VMEM((tm, FFN_TC), F32), pltpu.VMEM((tm, FFN_TC), F32)],
        compiler_params=_params("parallel", "arbitrary"),
    )(upg, upg, upg, upv, upv, upv, cwg, cwv, cbg, cbv, dact, dact)


def _mesh_place():
    return lax.axis_index("x"), lax.axis_index("y"), lax.axis_index("c")


def _all_gather(shards, name):
    na = len(shards)

    def body(*refs):
        x_refs, out_refs = refs[:na], refs[na:2 * na]
        send_sems, recv_sems, local_sems = refs[2 * na:]
        x, y, cc = _mesh_place()
        me, sibling = (x, y, cc), (x, y, 1 - cc)
        chips = [(1 - x, y), (x, 1 - y), (1 - x, 1 - y)]

        def copy(k, a, block, to, from_input=False):
            slot = out_refs[a].at[4 * block[0] + 2 * block[1] + block[2]]
            return pltpu.make_async_remote_copy(
                src_ref=x_refs[a] if from_input else slot, dst_ref=slot, send_sem=send_sems.at[k * na + a],
                recv_sem=recv_sems.at[k * na + a], device_id=to, device_id_type=pl.DeviceIdType.MESH)

        mine = [pltpu.make_async_copy(x_refs[a], out_refs[a].at[4 * x + 2 * y + cc], local_sems.at[a]) for a in range(na)]
        for cp in mine:
            cp.start()
        first = [copy(0, a, me, sibling, True) for a in range(na)]
        first += [copy(1 + j, a, me, (*chip, cc), True) for j, chip in enumerate(chips) for a in range(na)]
        for cp in first:
            cp.start()
        passed = []
        for j, chip in enumerate(chips):
            for a in range(na):
                copy(1 + j, a, (*chip, cc), me).wait_recv()
                passed.append(copy(4 + j, a, (*chip, cc), sibling))
                passed[-1].start()
        for a in range(na):
            copy(0, a, sibling, me).wait_recv()
        for j, chip in enumerate(chips):
            for a in range(na):
                copy(4 + j, a, (*chip, 1 - cc), me).wait_recv()
        for cp in first + passed:
            cp.wait_send()
        for cp in mine:
            cp.wait()

    hbm = pl.BlockSpec(memory_space=pl.ANY)
    return pl.pallas_call(
        body, name=name, out_shape=[jax.ShapeDtypeStruct((N_DEV,) + t.shape, t.dtype) for t in shards],
        in_specs=[hbm] * na, out_specs=[hbm] * na,
        scratch_shapes=[pltpu.SemaphoreType.DMA((7 * na,)), pltpu.SemaphoreType.DMA((7 * na,)), pltpu.SemaphoreType.DMA((na,))],
    )(*shards)


def _peer(k):
    x, y, cc = _mesh_place()
    px = 1 - x if (k >> 2) & 1 else x
    py = 1 - y if (k >> 1) & 1 else y
    pc = 1 - cc if k & 1 else cc
    return (px, py, pc), 4 * px + 2 * py + pc


def _push_copy(src_ref, land_ref, k, a, na, send_sems, recv_sems, indexed, landed):
    x, y, cc = _mesh_place()
    place, peer = _peer(k)
    sem = (k - 1) * na + a
    return pltpu.make_async_remote_copy(
        src_ref=src_ref.at[peer] if indexed else src_ref, dst_ref=land_ref.at[peer if landed else 4 * x + 2 * y + cc],
        send_sem=send_sems.at[sem], recv_sem=recv_sems.at[sem], device_id=place, device_id_type=pl.DeviceIdType.MESH)


_HBM = pl.BlockSpec(memory_space=pltpu.HBM)
_SEM = pl.BlockSpec(memory_space=pltpu.SEMAPHORE)
_EFFECT = pltpu.SideEffectType.DATAFLOW_SIDE_EFFECTING


def _push_start(srcs, indexed, name):
    na = len(srcs)
    lands = [lax.empty(t.shape if indexed else (N_DEV,) + t.shape, t.dtype) for t in srcs]

    def body(*refs):
        src_refs, land_refs = refs[:na], refs[na:2 * na]
        send_sems, recv_sems = refs[2 * na], refs[2 * na + 1]
        token = refs[-1]
        for k in range(1, N_DEV):
            for a in range(na):
                _push_copy(src_refs[a], land_refs[a], k, a, na, send_sems, recv_sems, indexed, False).start()
        token[...] = jnp.zeros_like(token)

    sems = pltpu.SemaphoreType.DMA((7 * na,))
    out = pl.pallas_call(
        body, name=name,
        out_shape=(sems, sems, *[pltpu.HBM(t.shape, t.dtype) for t in srcs], *[pltpu.HBM(t.shape, t.dtype) for t in lands],
                   jax.ShapeDtypeStruct((8, LANES), F32)),
        in_specs=[_HBM] * (2 * na), out_specs=(_SEM, _SEM, *[_HBM] * (2 * na), pl.BlockSpec(memory_space=pltpu.VMEM)),
        input_output_aliases={i: 2 + i for i in range(2 * na)},
        compiler_params=pltpu.CompilerParams(has_side_effects=_EFFECT),
    )(*[pltpu.with_memory_space_constraint(t, pltpu.HBM) for t in srcs + lands])
    return out[0], out[1], list(out[2:2 + na]), list(out[2 + na:2 + 2 * na]), out[-1]


def _push_wait(started, indexed, after, name):
    send_sems, recv_sems, srcs, lands, _ = started
    na = len(srcs)

    def body(*refs):
        src_refs, land_refs = refs[:na], refs[na:2 * na]
        send_sems, recv_sems = refs[2 * na], refs[2 * na + 1]
        for k in range(1, N_DEV):
            for a in range(na):
                copy = _push_copy(src_refs[a], land_refs[a], k, a, na, send_sems, recv_sems, indexed, True)
                copy.wait_send()
                copy.wait_recv()

    out = pl.pallas_call(
        body, name=name, out_shape=[pltpu.HBM(t.shape, t.dtype) for t in srcs + lands],
        in_specs=[_HBM] * (2 * na) + [_SEM, _SEM, pl.BlockSpec(memory_space=pl.ANY)], out_specs=[_HBM] * (2 * na),
        input_output_aliases={i: i for i in range(2 * na)},
        compiler_params=pltpu.CompilerParams(has_side_effects=_EFFECT),
    )(*srcs, *lands, send_sems, recv_sems, after)
    x, y, cc = _mesh_place()
    me = 4 * x + 2 * y + cc
    return [lax.dynamic_update_index_in_dim(
        land, lax.dynamic_index_in_dim(src, me, 0, keepdims=False) if indexed else src, me, 0)
        for src, land in zip(out[:na], out[na:])]


def _adamw_sum(parts, w, m, v, name):
    _, r, c = parts.shape
    tr = _tile(r, (256, 128, 64, 32, 16, 8))

    def body(p_ref, w_ref, m_ref, v_ref, g_ref, d_ref, nm_ref, nv_ref):
        _adam_store(_sum_parts(p_ref), w_ref, m_ref, v_ref, g_ref, d_ref, nm_ref, nv_ref)

    row = pl.BlockSpec((tr, c), lambda i: (i, 0))
    shp = jax.ShapeDtypeStruct((r, c), F32)
    return pl.pallas_call(
        body, name=name, grid=(r // tr,), in_specs=[pl.BlockSpec((N_DEV, tr, c), lambda i: (0, i, 0)), row, row, row],
        out_specs=[row, row, row, row], out_shape=[shp, shp, shp, shp], compiler_params=_params("parallel"),
    )(parts, w, m, v)


def _sum_parts(p_ref):
    g = p_ref[0].astype(F32)
    for k in range(1, N_DEV):
        g = g + p_ref[k].astype(F32)
    return g


def _adam_store(g, w_ref, m_ref, v_ref, g_ref, d_ref, nm_ref, nv_ref):
    nm = ADAM_B1 * m_ref[...] + (1.0 - ADAM_B1) * g
    nv = ADAM_B2 * v_ref[...] + (1.0 - ADAM_B2) * (g * g)
    m_hat = nm / (1.0 - ADAM_B1 ** ADAM_STEP)
    v_hat = nv / (1.0 - ADAM_B2 ** ADAM_STEP)
    g_ref[...] = g
    nm_ref[...] = nm
    nv_ref[...] = nv
    d_ref[...] = -ADAM_LR * (m_hat / (jnp.sqrt(v_hat) + ADAM_EPS) + ADAM_WD * w_ref[...])


def _adamw_weight(parts, w, m, v, name):
    _, r, c = w.shape
    tr = _tile(r, (256, 128, 176))
    nr = r // tr

    def body(p0_ref, p1_ref, w_ref, m_ref, v_ref, g_ref, d_ref, nm_ref, nv_ref):
        g = jnp.where(pl.program_id(0) == 0, _sum_parts(p0_ref), _sum_parts(p1_ref))
        _adam_store(g, w_ref, m_ref, v_ref, g_ref, d_ref, nm_ref, nv_ref)

    part = lambda layer: pl.BlockSpec(
        (N_DEV, tr, c), lambda l, i: (0, jnp.where(l == layer, i, (nr - 1) * (1 - layer)), 0))
    row = pl.BlockSpec((None, tr, c), lambda l, i: (l, i, 0))
    shp = jax.ShapeDtypeStruct(w.shape, F32)
    return pl.pallas_call(
        body, name=name, grid=(DEPTH, nr), in_specs=[part(0), part(1), row, row, row],
        out_specs=[row, row, row, row], out_shape=[shp, shp, shp, shp], compiler_params=_params("arbitrary", "arbitrary"),
    )(parts[0], parts[1], w, m, v)


def _full_to_slots(name, t):
    k, n = t.shape
    if name in ROW_SHARDED:
        return t.reshape(N_DEV, k // N_DEV, n)
    return t.reshape(k, N_DEV, n // N_DEV).transpose(1, 0, 2)


def _slots_to_full(name, t):
    _, r, c = t.shape
    if name in ROW_SHARDED:
        return t.reshape(N_DEV * r, c)
    return t.transpose(1, 0, 2).reshape(r, N_DEV * c)


def _small_sizes(shapes, names):
    return [(n, shapes[n], -(-int(math.prod(shapes[n])) // (8 * LANES)) * 8) for n in names]


def _pack_small(tree, shapes, names, row_tile):
    rows = []
    for n, shp, nrow in _small_sizes(shapes, names):
        flat = tree[n].reshape(-1)
        rows.append(jnp.pad(flat, (0, nrow * LANES - flat.shape[0])).reshape(nrow, LANES))
    total = sum(r.shape[0] for r in rows)
    if total % row_tile:
        rows.append(jnp.zeros((-total % row_tile, LANES), F32))
    return jnp.concatenate(rows, axis=0)


def _unpack_small(buf, shapes, names):
    out, r0 = {}, 0
    for n, shp, nrow in _small_sizes(shapes, names):
        out[n] = buf[r0:r0 + nrow].reshape(-1)[:int(math.prod(shp))].reshape(shp)
        r0 += nrow
    return out


def _block_diag(w):
    g = w.shape[0]
    eye = jnp.eye(g, dtype=w.dtype)
    return (eye[:, None, :, None] * w[:, :, None, :]).reshape(g * HEAD_DIM, g * HEAD_DIM)


def kernel(x, positions, norm1, w_in, q_norm, k_norm, sinks, w_pool, pool_scale, sgu_v_norm, w_s, b_s, w_proj_a, w_proj_b, w_proj_c, w_out, norm2, w_up, conv_w, conv_b, w_down, loss_target, m_norm1, m_w_in, m_q_norm, m_k_norm, m_sinks, m_w_pool, m_pool_scale, m_sgu_v_norm, m_w_s, m_b_s, m_w_proj_a, m_w_proj_b, m_w_proj_c, m_w_out, m_norm2, m_w_up, m_conv_w, m_conv_b, m_w_down, v_norm1, v_w_in, v_q_norm, v_k_norm, v_sinks, v_w_pool, v_pool_scale, v_sgu_v_norm, v_w_s, v_b_s, v_w_proj_a, v_w_proj_b, v_w_proj_c, v_w_out, v_norm2, v_w_up, v_conv_w, v_conv_b, v_w_down):
    names = ("norm1", "w_in", "q_norm", "k_norm", "sinks", "w_pool", "pool_scale", "sgu_v_norm", "w_s", "b_s", "w_proj_a",
             "w_proj_b", "w_proj_c", "w_out", "norm2", "w_up", "conv_w", "conv_b", "w_down")
    wts = dict(zip(names, (norm1, w_in, q_norm, k_norm, sinks, w_pool, pool_scale, sgu_v_norm, w_s, b_s, w_proj_a, w_proj_b,
                           w_proj_c, w_out, norm2, w_up, conv_w, conv_b, w_down)))
    mom = dict(zip(names, (m_norm1, m_w_in, m_q_norm, m_k_norm, m_sinks, m_w_pool, m_pool_scale, m_sgu_v_norm, m_w_s, m_b_s,
                           m_w_proj_a, m_w_proj_b, m_w_proj_c, m_w_out, m_norm2, m_w_up, m_conv_w, m_conv_b, m_w_down)))
    var = dict(zip(names, (v_norm1, v_w_in, v_q_norm, v_k_norm, v_sinks, v_w_pool, v_pool_scale, v_sgu_v_norm, v_w_s, v_b_s,
                           v_w_proj_a, v_w_proj_b, v_w_proj_c, v_w_out, v_norm2, v_w_up, v_conv_w, v_conv_b, v_w_down)))
    xs = x[0]
    target = loss_target[0]
    s = xs.shape[0]

    inv_freq = ROPE_THETA ** (-jnp.arange(0, HEAD_DIM, 2, dtype=F32) / HEAD_DIM)
    ang = positions[0].astype(F32)[:, None] * inv_freq
    cosf = jnp.tile(jnp.cos(ang), (1, 4))
    sinf = jnp.tile(jnp.concatenate([-jnp.sin(ang), jnp.sin(ang)], axis=1), (1, 2))

    local = [{n: wts[n][l] if n == "conv_w" else wts[n][l].astype(BF16) for n in SHARDED} for l in range(DEPTH)]
    later = SHARDED[1:]
    full = [{"w_in": _slots_to_full("w_in", _all_gather([local[0]["w_in"]], "gather_w_in_0")[0])}, None]
    gather0 = _push_start([local[0][n] for n in later], False, "gather_rest_0_start")
    norm1_first = norm1[0] + gather0[4][0, 0]

    def layer_consts(l):
        return dict(
            bdw=_block_diag(w_pool[l]).astype(BF16), qn=jnp.tile(q_norm[l], 2).reshape(1, LANES),
            kn=jnp.tile(k_norm[l], 2).reshape(1, LANES), vn=jnp.tile(sgu_v_norm[l], 4).reshape(1, SGU_W),
            bcol=jnp.repeat(b_s[l].T, HEAD_DIM, axis=1),
            cbg=conv_b[l][:D_FF].reshape(1, D_FF), cbv=conv_b[l][D_FF:].reshape(1, D_FF))

    gate_cols, val_cols = (0, D_FF), (D_FF, D_FF)

    saved = []
    cur = xs
    for l in range(DEPTH):
        if l == 1:
            landed = _push_wait(gather1, False, cur, "gather_weights_1_wait")
            full[1] = {n: _slots_to_full(n, t) for n, t in zip(SHARDED, landed)}
        fw, k = full[l], layer_consts(l)
        h1 = _rms_fwd(cur, norm1_first, "rms1_fwd_0") if l == 0 else h_next
        z = _mm(h1, fw["w_in"], name=f"in_proj_{l}")
        a = _pool_fwd(z, k["bdw"], pool_scale[l], f"pool_fwd_{l}")
        b = _attn_fwd(z, cosf, sinf, k["qn"], k["kn"], sinks[l], f"attn_fwd_{l}")
        c = _sgu_fwd(z, w_s[l], k["bcol"], k["vn"], f"sgu_fwd_{l}")
        w_proj_a_l = fw.get("w_proj_a")
        if l == 0:
            landed = _push_wait(gather0, False, c, "gather_rest_0_wait")
            fw.update({n: _slots_to_full(n, t) for n, t in zip(later, landed)})
            gather1 = _push_start([local[1][n] for n in SHARDED], False, "gather_weights_1_start")
            w_proj_a_l = fw["w_proj_a"] + gather1[4][0, 0].astype(BF16)
        x1, merged, h2 = _merge_out_fwd(a, b, c, w_proj_a_l, fw["w_proj_b"], fw["w_proj_c"], z, fw["w_out"], cur, norm2[l],
                                        f"merge_out_fwd_{l}")
        upg = _mm(h2, fw["w_up"], b_n=gate_cols, name=f"up_gate_{l}")
        upv = _mm(h2, fw["w_up"], b_n=val_cols, name=f"up_val_{l}")
        k["cwg"], k["cwv"] = fw["conv_w"][:, :D_FF], fw["conv_w"][:, D_FF:]
        last = l == DEPTH - 1
        x2, act, h_next = _ffn_down_fwd(upg, upv, k["cwg"], k["cwv"], k["cbg"], k["cbv"], fw["w_down"], x1,
                                        None if last else norm1[l + 1], target if last else None, f"ffn_down_fwd_{l}")
        saved.append(dict(x0=cur, h1=h1, z=z, a=a, b=b, c=c, merged=merged, x1=x1, h2=h2, upg=upg, upv=upv, act=act))
        cur = x2

    dcur, loss_tile = cur, h_next
    loss = lax.psum(loss_tile[0, 0], ("x", "y", "c"))

    gsmall = [None] * DEPTH
    small_shapes = {n: wts[n].shape for n in SMALL}

    def slots_of(grads):
        return [_full_to_slots(n, t) for n, t in grads.items()]

    for l in reversed(range(DEPTH)):
        fw, k, sv = full[l], layer_consts(l), saved[l]
        k["cwg"], k["cwv"] = fw["conv_w"][:, :D_FF], fw["conv_w"][:, D_FF:]
        staged = l == 0
        wgrad = functools.partial(_mm, ta=True, out_dtype=BF16)
        w_down_l = fw["w_down"] + exchange1[4][0, 0].astype(BF16) if staged else fw["w_down"]
        dact = _mm(dcur, w_down_l, tb=True, name=f"down_proj_bwd_{l}")
        g_down = wgrad(sv["act"], dcur, name=f"down_proj_wgrad_{l}")
        d_up, dcg, dcv = _ffn_bwd(sv["upg"], sv["upv"], k["cwg"], k["cwv"], k["cbg"], k["cbv"], dact, f"ffn_bwd_{l}")
        g_up = wgrad(sv["h2"], d_up, name=f"up_wgrad_{l}")
        g_ffn = dict(w_up=g_up, w_down=g_down, conv_w=jnp.concatenate([dcg[0:3], dcv[0:3]], axis=1))
        norm2_l = norm2[l]
        if staged:
            parts1 = dict(zip(SHARDED, _push_wait(exchange1, True, g_up, "exchange_grads_1_wait")))
            exchange_ffn = _push_start(slots_of(g_ffn), True, "exchange_ffn_0_start")
            norm2_l = norm2_l + exchange_ffn[4][0, 0]
        dx1, g_norm2 = _mm(d_up, fw["w_up"], tb=True, rms=(sv["x1"], norm2_l, dcur), name=f"up_bwd_{l}")
        dmerged = _mm(dx1, fw["w_out"], tb=True, name=f"out_proj_bwd_{l}")
        g_out = wgrad(sv["merged"], dx1, name=f"out_proj_wgrad_{l}")
        dz, dya, da = _branch_bwd(0, sv["a"], fw["w_proj_a"], sv["z"], dmerged, None, f"branch_a_bwd_{l}")
        dz, dyb, db = _branch_bwd(1, sv["b"], fw["w_proj_b"], sv["z"], dmerged, dz, f"branch_b_bwd_{l}")
        dz, dyc, dc = _branch_bwd(2, sv["c"], fw["w_proj_c"], sv["z"], dmerged, dz, f"branch_c_bwd_{l}")
        g_mix = dict(w_proj_a=wgrad(sv["a"], dya, name=f"proj_a_wgrad_{l}"), w_proj_b=wgrad(sv["b"], dyb, name=f"proj_b_wgrad_{l}"),
                     w_proj_c=wgrad(sv["c"], dyc, name=f"proj_c_wgrad_{l}"), w_out=g_out)
        pool_scale_l = pool_scale[l]
        if staged:
            exchange_mix = _push_start(slots_of(g_mix), True, "exchange_mixer_0_start")
            pool_scale_l = pool_scale_l + exchange_mix[4][0, 0]
        dxp, g_bdw, g_pscale = _pool_bwd(sv["z"], da, k["bdw"], pool_scale_l, f"pool_bwd_{l}")
        dq, dkc, dkp, dvc, dvp, g_qn, g_sink = _attn_bwd(sv["z"], cosf, sinf, k["qn"], k["kn"], sinks[l], db, f"attn_bwd_{l}")
        duv, g_ws, g_bacc, g_vn = _sgu_bwd(sv["z"], w_s[l], k["bcol"], k["vn"], dc, f"sgu_bwd_{l}")
        gsmall[l] = dict(
            q_norm=g_qn[0, :HEAD_DIM], sinks=g_sink[:, 0],
            w_pool=jnp.stack([g_bdw[g * HEAD_DIM:(g + 1) * HEAD_DIM, g * HEAD_DIM:(g + 1) * HEAD_DIM] for g in range(4)]),
            pool_scale=g_pscale[0], sgu_v_norm=g_vn[0, :HEAD_DIM], w_s=g_ws, b_s=g_bacc[:, ::HEAD_DIM].T,
            norm2=g_norm2[0], conv_b=jnp.concatenate([dcg[3], dcv[3]]))
        kn_l = k["kn"]
        if staged:
            early = _pack_small({n: jnp.stack([gsmall[i][n] for i in range(DEPTH)]) for n in SMALL_EARLY}, small_shapes,
                                SMALL_EARLY, SMALL_ROW_TILE)
            gather_early = _push_start([early], False, "gather_small_grads_start")
            kn_l = kn_l + gather_early[4][0, 0]
        dz, g_kn = _kv_post(sv["z"], cosf, sinf, kn_l, dkc, dkp, dvc, dvp, dxp, dq, duv, dz, f"kv_post_{l}")
        g_in = dict(w_in=wgrad(sv["h1"], dz, name=f"in_proj_wgrad_{l}"))
        norm1_l = norm1[l]
        if staged:
            exchange_in = _push_start(slots_of(g_in), True, "exchange_w_in_0_start")
            norm1_l = norm1_l + exchange_in[4][0, 0]
        dcur, g_norm1 = _mm(dz, fw["w_in"], tb=True, rms=(sv["x0"], norm1_l, dx1), name=f"in_proj_bwd_{l}")
        if not staged:
            exchange1 = _push_start(slots_of({n: {**g_in, **g_mix, **g_ffn}[n] for n in SHARDED}), True, "exchange_grads_1_start")
        gsmall[l].update(norm1=g_norm1[0], k_norm=g_kn[0, :HEAD_DIM])
    grad_x = dcur[None]

    def update_small(gathered, names, row_tile, name):
        pack = lambda tree: _pack_small(tree, small_shapes, names, row_tile)
        return [_unpack_small(t, small_shapes, names) for t in _adamw_sum(gathered, pack(wts), pack(mom), pack(var), name)]

    late = _pack_small({n: jnp.stack([gsmall[i][n] for i in range(DEPTH)]) for n in SMALL_LATE}, small_shapes, SMALL_LATE, 8)
    gather_late = _push_start([late], False, "gather_late_small_grads_start")

    parts0 = dict(zip(g_ffn, _push_wait(exchange_ffn, True, gather_late[4], "exchange_ffn_0_wait")))
    parts0.update(zip(g_mix, _push_wait(exchange_mix, True, gather_late[4], "exchange_mixer_0_wait")))
    update = lambda n: _adamw_weight([parts0[n], parts1[n]], wts[n], mom[n], var[n], f"adamw_{n}")
    big = {n: update(n) for n in SHARDED[1:]}
    parts0.update(zip(g_in, _push_wait(exchange_in, True, big["w_up"][0], "exchange_w_in_0_wait")))
    big["w_in"] = update("w_in")

    late_all = _push_wait(gather_late, False, big["w_in"][0], "gather_late_small_grads_wait")[0]
    small = update_small(late_all, SMALL_LATE, 8, "adamw_replicated_late")
    early_all = _push_wait(gather_early, False, big["w_in"][0], "gather_small_grads_wait")[0]
    for kind, tree in enumerate(update_small(early_all, SMALL_EARLY, SMALL_ROW_TILE, "adamw_replicated")):
        small[kind].update(tree)

    outs = [loss, grad_x]
    for kind in range(4):
        outs += [small[kind][n] if n in SMALL else big[n][kind] for n in names]
    return tuple(outs)
```
